```python
import jax, jax.numpy as jnp
from jax import lax
import numpy as np

D_MODEL = 1024
BATCH = 8
SEQ = 4096
DEPTH = 1

N_META = 16
D_CONV = D_MODEL
CONV_WIDTH = 31
HG_HEADS = 8
HG_DK = 128
HG_DV = D_MODEL // HG_HEADS
D_HK = HG_HEADS * HG_DK
D_HV = HG_HEADS * HG_DV
CHUNK = 64
EPS = 1e-6
SPLIT_SIZES = (D_CONV, D_CONV, D_CONV, D_HK, D_HK, D_HV, D_HV, D_MODEL, D_MODEL)
D_IN = D_CONV * 3 + D_HK * 2 + D_HV * 2 + D_MODEL * 2

kernel_name = "hybrid_conformer_hgrn2_gated_block"


def rmsnorm(x, g):
    xf = x.astype(jnp.float32)
    y = xf * lax.rsqrt(jnp.mean(xf * xf, axis=-1, keepdims=True) + EPS)
    return (y * g.astype(jnp.float32)).astype(x.dtype)


def layernorm(x, g, b):
    xf = x.astype(jnp.float32)
    mu = jnp.mean(xf, axis=-1, keepdims=True)
    var = jnp.mean(jnp.square(xf - mu), axis=-1, keepdims=True)
    y = (xf - mu) * lax.rsqrt(var + EPS)
    return (y * g.astype(jnp.float32) + b.astype(jnp.float32)).astype(x.dtype)


def conformer_branch(u_a, u_b, z, conv_w, conv_b, ln_g, ln_b, w_out):
    a = u_a * jax.nn.sigmoid(u_b)
    c = lax.conv_general_dilated(
        a, conv_w[:, None, :].astype(a.dtype), window_strides=(1,),
        padding=[(CONV_WIDTH - 1, 0)],
        dimension_numbers=('NWC', 'WIO', 'NWC'),
        feature_group_count=D_CONV) + conv_b
    c = jax.nn.silu(layernorm(c, ln_g, ln_b))
    return (c * jax.nn.silu(z)) @ w_out


def hgrn2_branch(q_raw, f_raw, i_raw, g, lb, gnorm_g, w_out):
    bsz, seqlen, _ = q_raw.shape
    out_dtype = i_raw.dtype
    q = jax.nn.silu(q_raw.astype(jnp.float32))
    f = lb + (1.0 - lb) * jax.nn.sigmoid(f_raw.astype(jnp.float32))
    log_f = jnp.log(f)
    k = 1.0 - f
    v = i_raw.astype(jnp.float32)
    pad = (-seqlen) % CHUNK
    n_chunks = (seqlen + pad) // CHUNK

    def to_chunks(t, d):
        t = jnp.pad(t, ((0, 0), (pad, 0), (0, 0)))
        t = t.reshape(bsz, n_chunks, CHUNK, HG_HEADS, d)
        return jnp.transpose(t, (1, 0, 3, 2, 4))

    qc = to_chunks(q, HG_DK)
    kc = to_chunks(k, HG_DK)
    vc = to_chunks(v, HG_DV)
    bc = jnp.cumsum(to_chunks(log_f, HG_DK), axis=3)
    causal = jnp.tril(jnp.ones((CHUNK, CHUNK), dtype=bool))[None, None, :, :, None]

    def step(S, inp):
        qi, ki, vi, bi = inp
        o_inter = jnp.einsum('bhtk,bhkv->bhtv', qi * jnp.exp(bi), S)
        diff = bi[:, :, :, None, :] - bi[:, :, None, :, :]
        decay = jnp.exp(jnp.where(causal, diff, -jnp.inf))
        attn = jnp.einsum('bhtk,bhsk,bhtsk->bhts', qi, ki, decay)
        o_intra = jnp.einsum('bhts,bhsv->bhtv', attn, vi)
        b_last = bi[:, :, -1:, :]
        S_new = jnp.exp(b_last[:, :, 0, :])[..., None] * S + jnp.einsum(
            'bhsk,bhsv->bhkv', ki * jnp.exp(b_last - bi), vi)
        return S_new, o_inter + o_intra

    S0 = jnp.zeros((bsz, HG_HEADS, HG_DK, HG_DV), jnp.float32)
    _, o = lax.scan(step, S0, (qc, kc, vc, bc))
    o = jnp.transpose(o, (1, 0, 3, 2, 4)).reshape(bsz, n_chunks * CHUNK, HG_HEADS, HG_DV)
    o = o[:, pad:]
    o = o * lax.rsqrt(jnp.mean(o * o, axis=-1, keepdims=True) + EPS)
    o = o * gnorm_g.astype(jnp.float32).reshape(HG_HEADS, HG_DV)
    o = o.reshape(bsz, seqlen, D_HV).astype(out_dtype)
    return (o * jax.nn.silu(g)) @ w_out


def _fwd_setup_inputs(seed: int = 0) -> dict:
    key = jax.random.key(seed)
    ks = jax.random.split(key, 16)
    f32 = jnp.float32
    nrm = lambda k, shape, s: jax.random.normal(k, shape, f32) * s
    return {
        "x": nrm(ks[0], (BATCH, SEQ, D_MODEL), 1.0),
        "meta_tokens": nrm(ks[1], (N_META, D_MODEL), 1.0),
        "norm_g": 1.0 + nrm(ks[2], (DEPTH, D_MODEL), 0.02),
        "w_in": nrm(ks[3], (DEPTH, D_MODEL, D_IN), D_MODEL ** -0.5),
        "conv_w": nrm(ks[4], (DEPTH, CONV_WIDTH, D_CONV), CONV_WIDTH ** -0.5),
        "conv_b": nrm(ks[5], (DEPTH, D_CONV), 0.02),
        "ln_g": 1.0 + nrm(ks[6], (DEPTH, D_CONV), 0.02),
        "ln_b": nrm(ks[7], (DEPTH, D_CONV), 0.02),
        "w_conv_out": nrm(ks[8], (DEPTH, D_CONV, D_MODEL), D_CONV ** -0.5),
        "lb_logits": nrm(ks[9], (DEPTH + 1, D_HK), 0.5),
        "gnorm_g": 1.0 + nrm(ks[10], (DEPTH, D_HV), 0.02),
        "w_rec_out": nrm(ks[11], (DEPTH, D_HV, D_MODEL), D_HV ** -0.5),
        "w_out": nrm(ks[12], (DEPTH, D_MODEL, D_MODEL), D_MODEL ** -0.5),
        "final_g": 1.0 + nrm(ks[13], (D_MODEL,), 0.02),
    }


def _fwd_reference(x, meta_tokens, norm_g, w_in, conv_w, conv_b, ln_g, ln_b, w_conv_out,
              lb_logits, gnorm_g, w_rec_out, w_out, final_g):
    bsz = x.shape[0]
    meta = jnp.broadcast_to(meta_tokens.astype(x.dtype)[None], (bsz, N_META, D_MODEL))
    h_res = jnp.concatenate([meta, x], axis=1)
    lb_all = jnp.cumsum(jax.nn.softmax(lb_logits.astype(jnp.float32), axis=0), axis=0)
    split_idx = [int(v) for v in np.cumsum(SPLIT_SIZES)[:-1]]
    for l in range(DEPTH):
        h = rmsnorm(h_res, norm_g[l])
        proj = h @ w_in[l]
        glu_a, glu_b, z_conv, q, f, i, g_rec, m_conv, m_rec = jnp.split(proj, split_idx, axis=-1)
        y_conv = conformer_branch(glu_a, glu_b, z_conv, conv_w[l], conv_b[l],
                                  ln_g[l], ln_b[l], w_conv_out[l])
        y_rec = hgrn2_branch(q, f, i, g_rec, lb_all[l], gnorm_g[l], w_rec_out[l])
        merged = jax.nn.sigmoid(m_conv) * y_conv + jax.nn.sigmoid(m_rec) * y_rec
        h_res = h_res + merged @ w_out[l]
    return rmsnorm(h_res[:, N_META:], final_g)


import jax as _jax
import jax.numpy as _jnp

TWIN_FORMAT = 'train_step'
FWD_PARAMS = ['x', 'meta_tokens', 'norm_g', 'w_in', 'conv_w', 'conv_b', 'ln_g', 'ln_b', 'w_conv_out', 'lb_logits', 'gnorm_g', 'w_rec_out', 'w_out', 'final_g']
TWIN_WEIGHTS = ['meta_tokens', 'norm_g', 'w_in', 'conv_w', 'conv_b', 'ln_g', 'ln_b', 'w_conv_out', 'lb_logits', 'gnorm_g', 'w_rec_out', 'w_out', 'final_g']
TWIN_DIFF_INPUT = 'x'
TWIN_INPUTS = ['x', 'meta_tokens', 'norm_g', 'w_in', 'conv_w', 'conv_b', 'ln_g', 'ln_b', 'w_conv_out', 'lb_logits', 'gnorm_g', 'w_rec_out', 'w_out', 'final_g', 'loss_target', 'm_meta_tokens', 'm_norm_g', 'm_w_in', 'm_conv_w', 'm_conv_b', 'm_ln_g', 'm_ln_b', 'm_w_conv_out', 'm_lb_logits', 'm_gnorm_g', 'm_w_rec_out', 'm_w_out', 'm_final_g', 'v_meta_tokens', 'v_norm_g', 'v_w_in', 'v_conv_w', 'v_conv_b', 'v_ln_g', 'v_ln_b', 'v_w_conv_out', 'v_lb_logits', 'v_gnorm_g', 'v_w_rec_out', 'v_w_out', 'v_final_g']
TWIN_OUTPUTS = ['loss', 'grad_x', 'grad_meta_tokens', 'grad_norm_g', 'grad_w_in', 'grad_conv_w', 'grad_conv_b', 'grad_ln_g', 'grad_ln_b', 'grad_w_conv_out', 'grad_lb_logits', 'grad_gnorm_g', 'grad_w_rec_out', 'grad_w_out', 'grad_final_g', 'delta_meta_tokens', 'delta_norm_g', 'delta_w_in', 'delta_conv_w', 'delta_conv_b', 'delta_ln_g', 'delta_ln_b', 'delta_w_conv_out', 'delta_lb_logits', 'delta_gnorm_g', 'delta_w_rec_out', 'delta_w_out', 'delta_final_g', 'new_m_meta_tokens', 'new_m_norm_g', 'new_m_w_in', 'new_m_conv_w', 'new_m_conv_b', 'new_m_ln_g', 'new_m_ln_b', 'new_m_w_conv_out', 'new_m_lb_logits', 'new_m_gnorm_g', 'new_m_w_rec_out', 'new_m_w_out', 'new_m_final_g', 'new_v_meta_tokens', 'new_v_norm_g', 'new_v_w_in', 'new_v_conv_w', 'new_v_conv_b', 'new_v_ln_g', 'new_v_ln_b', 'new_v_w_conv_out', 'new_v_lb_logits', 'new_v_gnorm_g', 'new_v_w_rec_out', 'new_v_w_out', 'new_v_final_g']
TWIN_LEAF_KINDS = {'loss': 'loss', 'grad_x': 'grad_x', 'grad_meta_tokens': 'grad_w', 'grad_norm_g': 'grad_w', 'grad_w_in': 'grad_w', 'grad_conv_w': 'grad_w', 'grad_conv_b': 'grad_w', 'grad_ln_g': 'grad_w', 'grad_ln_b': 'grad_w', 'grad_w_conv_out': 'grad_w', 'grad_lb_logits': 'grad_w', 'grad_gnorm_g': 'grad_w', 'grad_w_rec_out': 'grad_w', 'grad_w_out': 'grad_w', 'grad_final_g': 'grad_w', 'delta_meta_tokens': 'delta_w', 'delta_norm_g': 'delta_w', 'delta_w_in': 'delta_w', 'delta_conv_w': 'delta_w', 'delta_conv_b': 'delta_w', 'delta_ln_g': 'delta_w', 'delta_ln_b': 'delta_w', 'delta_w_conv_out': 'delta_w', 'delta_lb_logits': 'delta_w', 'delta_gnorm_g': 'delta_w', 'delta_w_rec_out': 'delta_w', 'delta_w_out': 'delta_w', 'delta_final_g': 'delta_w', 'new_m_meta_tokens': 'new_m', 'new_m_norm_g': 'new_m', 'new_m_w_in': 'new_m', 'new_m_conv_w': 'new_m', 'new_m_conv_b': 'new_m', 'new_m_ln_g': 'new_m', 'new_m_ln_b': 'new_m', 'new_m_w_conv_out': 'new_m', 'new_m_lb_logits': 'new_m', 'new_m_gnorm_g': 'new_m', 'new_m_w_rec_out': 'new_m', 'new_m_w_out': 'new_m', 'new_m_final_g': 'new_m', 'new_v_meta_tokens': 'new_v', 'new_v_norm_g': 'new_v', 'new_v_w_in': 'new_v', 'new_v_conv_w': 'new_v', 'new_v_conv_b': 'new_v', 'new_v_ln_g': 'new_v', 'new_v_ln_b': 'new_v', 'new_v_w_conv_out': 'new_v', 'new_v_lb_logits': 'new_v', 'new_v_gnorm_g': 'new_v', 'new_v_w_rec_out': 'new_v', 'new_v_w_out': 'new_v', 'new_v_final_g': 'new_v'}


def _forward(args):
    return _fwd_reference(*[args[k] for k in FWD_PARAMS])


def _output_shape():
    out = _jax.eval_shape(lambda: _forward(_fwd_setup_inputs(0)))
    return out.shape, out.dtype

N_MICROBATCH = 1
ADAM_LR = 0.001
ADAM_B1 = 0.9
ADAM_B2 = 0.999
ADAM_EPS = 1e-08
ADAM_WD = 0.01
ADAM_STEP = 10
PER_EXAMPLE_BATCH_AXIS = {'x': 0, 'loss_target': 0}
SHARED_INPUTS = []
_WEIGHT_DTYPES = {'meta_tokens': _jnp.float32, 'norm_g': _jnp.float32, 'w_in': _jnp.float32, 'conv_w': _jnp.float32, 'conv_b': _jnp.float32, 'ln_g': _jnp.float32, 'ln_b': _jnp.float32, 'w_conv_out': _jnp.float32, 'lb_logits': _jnp.float32, 'gnorm_g': _jnp.float32, 'w_rec_out': _jnp.float32, 'w_out': _jnp.float32, 'final_g': _jnp.float32}
MOMENT_SCALE = {'meta_tokens': 2.314417e-03, 'norm_g': 9.452196e-02, 'w_in': 3.163819e-02, 'conv_w': 3.304344e-02, 'conv_b': 6.178043e-02, 'ln_g': 3.920176e-02, 'ln_b': 3.337497e-02, 'w_conv_out': 3.217200e-02, 'lb_logits': 4.815361e-03, 'gnorm_g': 5.744713e-02, 'w_rec_out': 5.371012e-02, 'w_out': 6.300245e-02, 'final_g': 3.198830e+01}


def _to_microbatches(a, axis):
    t = _jnp.moveaxis(a, axis, 0)
    t = t.reshape((N_MICROBATCH, t.shape[0] // N_MICROBATCH) + t.shape[1:])
    return _jnp.moveaxis(t, 1, axis + 1)


def setup_inputs(seed: int = 0) -> dict:
    inp = _fwd_setup_inputs(seed)
    key = _jax.random.fold_in(_jax.random.key(seed), 7919)
    shape, _ = _output_shape()
    out = dict(inp)
    out["loss_target"] = _jax.random.normal(_jax.random.fold_in(key, 0), shape, _jnp.float32)
    for i, name in enumerate(TWIN_WEIGHTS):
        w = inp[name].astype(_jnp.float32)
        if MOMENT_SCALE is None:
            s = _jnp.sqrt(_jnp.mean(_jnp.square(w)) + 1e-30)
        else:
            s = MOMENT_SCALE[name]
        km, kv = _jax.random.split(_jax.random.fold_in(key, i + 1))
        out[name] = w
        out["m_" + name] = s * _jax.random.normal(km, w.shape, _jnp.float32)
        out["v_" + name] = (s * s) * _jax.random.uniform(kv, w.shape, _jnp.float32, 0.5, 1.5)
    if N_MICROBATCH > 1:
        for name, axis in PER_EXAMPLE_BATCH_AXIS.items():
            out[name] = _to_microbatches(out[name], axis)
    return {'x': out['x'], 'meta_tokens': out['meta_tokens'], 'norm_g': out['norm_g'], 'w_in': out['w_in'], 'conv_w': out['conv_w'], 'conv_b': out['conv_b'], 'ln_g': out['ln_g'], 'ln_b': out['ln_b'], 'w_conv_out': out['w_conv_out'], 'lb_logits': out['lb_logits'], 'gnorm_g': out['gnorm_g'], 'w_rec_out': out['w_rec_out'], 'w_out': out['w_out'], 'final_g': out['final_g'], 'loss_target': out['loss_target'], 'm_meta_tokens': out['m_meta_tokens'], 'm_norm_g': out['m_norm_g'], 'm_w_in': out['m_w_in'], 'm_conv_w': out['m_conv_w'], 'm_conv_b': out['m_conv_b'], 'm_ln_g': out['m_ln_g'], 'm_ln_b': out['m_ln_b'], 'm_w_conv_out': out['m_w_conv_out'], 'm_lb_logits': out['m_lb_logits'], 'm_gnorm_g': out['m_gnorm_g'], 'm_w_rec_out': out['m_w_rec_out'], 'm_w_out': out['m_w_out'], 'm_final_g': out['m_final_g'], 'v_meta_tokens': out['v_meta_tokens'], 'v_norm_g': out['v_norm_g'], 'v_w_in': out['v_w_in'], 'v_conv_w': out['v_conv_w'], 'v_conv_b': out['v_conv_b'], 'v_ln_g': out['v_ln_g'], 'v_ln_b': out['v_ln_b'], 'v_w_conv_out': out['v_w_conv_out'], 'v_lb_logits': out['v_lb_logits'], 'v_gnorm_g': out['v_gnorm_g'], 'v_w_rec_out': out['v_w_rec_out'], 'v_w_out': out['v_w_out'], 'v_final_g': out['v_final_g']}


def _loss(weights, diff, rest, loss_target):
    with _jax.named_scope("forward"):
        args = {**rest, TWIN_DIFF_INPUT: diff, **{k: w.astype(_WEIGHT_DTYPES[k]) for k, w in weights.items()}}
        y = _forward(args)
    with _jax.named_scope("loss_head"):
        err = _jnp.square(y.astype(_jnp.float32) - loss_target)
        return 0.5 * _jnp.sum(_jnp.mean(err, axis=-1)) if err.ndim else 0.5 * err


def _adamw(w, g, m, v):
    m = ADAM_B1 * m + (1.0 - ADAM_B1) * g
    v = ADAM_B2 * v + (1.0 - ADAM_B2) * _jnp.square(g)
    m_hat = m / (1.0 - ADAM_B1 ** ADAM_STEP)
    v_hat = v / (1.0 - ADAM_B2 ** ADAM_STEP)
    delta = -ADAM_LR * (m_hat / (_jnp.sqrt(v_hat) + ADAM_EPS) + ADAM_WD * w)
    return delta, m, v


def reference(x, meta_tokens, norm_g, w_in, conv_w, conv_b, ln_g, ln_b, w_conv_out, lb_logits, gnorm_g, w_rec_out, w_out, final_g, loss_target, m_meta_tokens, m_norm_g, m_w_in, m_conv_w, m_conv_b, m_ln_g, m_ln_b, m_w_conv_out, m_lb_logits, m_gnorm_g, m_w_rec_out, m_w_out, m_final_g, v_meta_tokens, v_norm_g, v_w_in, v_conv_w, v_conv_b, v_ln_g, v_ln_b, v_w_conv_out, v_lb_logits, v_gnorm_g, v_w_rec_out, v_w_out, v_final_g):
    given = dict(x=x, meta_tokens=meta_tokens, norm_g=norm_g, w_in=w_in, conv_w=conv_w, conv_b=conv_b, ln_g=ln_g, ln_b=ln_b, w_conv_out=w_conv_out, lb_logits=lb_logits, gnorm_g=gnorm_g, w_rec_out=w_rec_out, w_out=w_out, final_g=final_g, loss_target=loss_target, m_meta_tokens=m_meta_tokens, m_norm_g=m_norm_g, m_w_in=m_w_in, m_conv_w=m_conv_w, m_conv_b=m_conv_b, m_ln_g=m_ln_g, m_ln_b=m_ln_b, m_w_conv_out=m_w_conv_out, m_lb_logits=m_lb_logits, m_gnorm_g=m_gnorm_g, m_w_rec_out=m_w_rec_out, m_w_out=m_w_out, m_final_g=m_final_g, v_meta_tokens=v_meta_tokens, v_norm_g=v_norm_g, v_w_in=v_w_in, v_conv_w=v_conv_w, v_conv_b=v_conv_b, v_ln_g=v_ln_g, v_ln_b=v_ln_b, v_w_conv_out=v_w_conv_out, v_lb_logits=v_lb_logits, v_gnorm_g=v_gnorm_g, v_w_rec_out=v_w_rec_out, v_w_out=v_w_out, v_final_g=v_final_g)
    weights = {n: given[n] for n in TWIN_WEIGHTS}
    shared = {n: given[n] for n in SHARED_INPUTS}
    per_example = {n: given[n] for n in ['x']}
    grad_fn = _jax.value_and_grad(_loss, argnums=(0, 1))

    def one_microbatch(ex, loss_target):
        ex = dict(ex)
        diff = ex.pop(TWIN_DIFF_INPUT)
        return grad_fn(weights, diff, {**shared, **ex}, loss_target)

    if N_MICROBATCH == 1:
        loss, (grad_w, grad_x) = one_microbatch(per_example, given["loss_target"])
    else:
        def body(carry, xs):
            loss_sum, grad_sum = carry
            l_k, (gw_k, gx_k) = one_microbatch(xs[0], xs[1])
            with _jax.named_scope("update"):
                return (loss_sum + l_k, _jax.tree.map(_jnp.add, grad_sum, gw_k)), gx_k

        init = (_jnp.zeros((), _jnp.float32), _jax.tree.map(_jnp.zeros_like, weights))
        (loss, grad_w), grad_x = _jax.lax.scan(body, init, (per_example, given["loss_target"]))
    with _jax.named_scope("update"):
        delta_w, new_m, new_v = {}, {}, {}
        for n in TWIN_WEIGHTS:
            delta_w[n], new_m[n], new_v[n] = _adamw(weights[n], grad_w[n], given["m_" + n], given["v_" + n])
    return (loss, grad_x, *[grad_w[n] for n in TWIN_WEIGHTS], *[delta_w[n] for n in TWIN_WEIGHTS],
            *[new_m[n] for n in TWIN_WEIGHTS], *[new_v[n] for n in TWIN_WEIGHTS])
```

```python
import functools

import jax
import jax.numpy as jnp
from jax import lax
from jax.experimental import pallas as pl
from jax.experimental.pallas import tpu as pltpu

F32 = jnp.float32
BF16 = jnp.bfloat16

D = 1024
SEQ = 4096
N_META = 16
CHUNK = 64
PAD_FRONT = 48
ROW0 = PAD_FRONT + N_META
TP = ROW0 + SEQ
N_CHUNK = TP // CHUNK
HEADS = 8
HEAD_W = 128
D_IN = 9 * D
N_DEV = 8
W_IN_BLK = D_IN // N_DEV
W_ROW_BLK = D // N_DEV
CONV_K = 31
SMALL_ROWS = 48
META_ROW = 32
EPS = 1e-6
HALO = 32

TM_MAT = 832
TM_ELT = 208

ADAM_LR = 0.001
ADAM_B1 = 0.9
ADAM_B2 = 0.999
ADAM_EPS = 1e-08
ADAM_WD = 0.01
ADAM_STEP = 10

MESH_ID = pl.DeviceIdType.MESH
ANY = pl.BlockSpec(memory_space=pl.ANY)


def _sigmoid(v):
    return jax.nn.sigmoid(v)


def _dsilu(v, s):
    return s * (1.0 + v * (1.0 - s))


def _dot(a, b):
    return jnp.dot(a, b, preferred_element_type=F32)


def _dot_nt(a, b):
    return lax.dot_general(a, b, (((1,), (1,)), ((), ())), preferred_element_type=F32)


def _dot_tn(a, b):
    return lax.dot_general(a, b, (((0,), (0,)), ((), ())), preferred_element_type=F32)


def _split3(v):
    hi = v.astype(BF16)
    r1 = v - hi.astype(F32)
    mid = r1.astype(BF16)
    lo = (r1 - mid.astype(F32)).astype(BF16)
    return hi, mid, lo


def _tri_matmul(tri, v):
    hi, mid, lo = _split3(v)
    return _dot(tri, hi) + _dot(tri, mid) + _dot(tri, lo)


def _adamw(w, g, m, v):
    m2 = ADAM_B1 * m + (1.0 - ADAM_B1) * g
    v2 = ADAM_B2 * v + (1.0 - ADAM_B2) * jnp.square(g)
    m_hat = m2 / (1.0 - ADAM_B1 ** ADAM_STEP)
    v_hat = v2 / (1.0 - ADAM_B2 ** ADAM_STEP)
    delta = -ADAM_LR * (m_hat / (jnp.sqrt(v_hat) + ADAM_EPS) + ADAM_WD * w)
    return delta, m2, v2


def _my_place():
    return lax.axis_index("x"), lax.axis_index("y"), lax.axis_index("c")


def _dev_index(px, py, pc):
    return 4 * px + 2 * py + pc


def _cast_shards(w_in_s, w3_s):
    def body(a_ref, b_ref, oa_ref, ob_ref):
        oa_ref[...] = a_ref[...].astype(BF16)
        ob_ref[...] = b_ref[...].astype(BF16)

    return pl.pallas_call(
        body, name="cast_shards",
        out_shape=(jax.ShapeDtypeStruct(w_in_s.shape, BF16), jax.ShapeDtypeStruct(w3_s.shape, BF16)),
    )(w_in_s, w3_s)


def _all_gather_weights(w_in_b, w3_b, small_s):
    n_arr = 3

    def body(s0, s1, s2, o0, o1, o2, send_sems, recv_sems, local_sems):
        x, y, c = _my_place()
        me, sibling = (x, y, c), (x, y, 1 - c)
        chips = [(1 - x, y), (x, 1 - y), (1 - x, 1 - y)]
        srcs, outs = (s0, s1, s2), (o0, o1, o2)

        def block(a, place):
            d = _dev_index(*place)
            if a == 1:
                return outs[1].at[:, pl.ds(pl.multiple_of(d * W_ROW_BLK, W_ROW_BLK), W_ROW_BLK), :]
            return outs[a].at[d]

        def copy(a, k, place, to, from_src=False):
            return pltpu.make_async_remote_copy(
                src_ref=srcs[a] if from_src else block(a, place), dst_ref=block(a, place),
                send_sem=send_sems.at[a * 7 + k], recv_sem=recv_sems.at[a * 7 + k],
                device_id=to, device_id_type=MESH_ID)

        mine = [pltpu.make_async_copy(srcs[a], block(a, me), local_sems.at[a]) for a in range(n_arr)]
        for cp in mine:
            cp.start()
        started = []
        for a in range(n_arr):
            cp = copy(a, 0, me, sibling, from_src=True)
            cp.start()
            started.append(cp)
            for j, chip in enumerate(chips):
                cp = copy(a, 1 + j, me, (*chip, c), from_src=True)
                cp.start()
                started.append(cp)
        for j, chip in enumerate(chips):
            for a in range(n_arr):
                copy(a, 1 + j, (*chip, c), me).wait_recv()
                cp = copy(a, 4 + j, (*chip, c), sibling)
                cp.start()
                started.append(cp)
        for a in range(n_arr):
            copy(a, 0, sibling, me).wait_recv()
            for j, chip in enumerate(chips):
                copy(a, 4 + j, (*chip, 1 - c), me).wait_recv()
        for cp in started:
            cp.wait_send()
        for cp in mine:
            cp.wait()

    return pl.pallas_call(
        body, name="all_gather_weights",
        out_shape=(jax.ShapeDtypeStruct((N_DEV,) + w_in_b.shape, BF16),
                   jax.ShapeDtypeStruct((3, D, D), BF16),
                   jax.ShapeDtypeStruct((N_DEV,) + small_s.shape, F32)),
        in_specs=[ANY, ANY, ANY], out_specs=(ANY, ANY, ANY),
        scratch_shapes=[pltpu.SemaphoreType.DMA((7 * n_arr,)), pltpu.SemaphoreType.DMA((7 * n_arr,)),
                        pltpu.SemaphoreType.DMA((n_arr,))],
    )(w_in_b, w3_b, small_s)


def _chip_rel(x, y, r):
    return (jnp.bitwise_xor(x, r >> 1), jnp.bitwise_xor(y, r & 1))


def _reduce_to_sibling(p_in, p3):
    def body(q0, q1, m0, m1, l0, l1, send_sems, recv_sems, local_sems):
        x, y, c = _my_place()
        sibling = (x, y, 1 - c)
        parts, mine, landed = (q0, q1), (m0, m1), (l0, l1)

        def block(a, place):
            d = _dev_index(*place)
            if a == 1:
                return parts[1].at[:, pl.ds(pl.multiple_of(d * W_ROW_BLK, W_ROW_BLK), W_ROW_BLK), :]
            return parts[0].at[d]

        local, remote = [], []
        for a in range(2):
            for r in range(4):
                chip = _chip_rel(x, y, r)
                cp = pltpu.make_async_copy(block(a, (*chip, c)), mine[a].at[r], local_sems.at[a * 4 + r])
                cp.start()
                local.append(cp)
                cp = pltpu.make_async_remote_copy(
                    src_ref=block(a, (*chip, 1 - c)), dst_ref=landed[a].at[r],
                    send_sem=send_sems.at[a * 4 + r], recv_sem=recv_sems.at[a * 4 + r],
                    device_id=sibling, device_id_type=MESH_ID)
                cp.start()
                remote.append(cp)
        for cp in remote:
            cp.wait_recv()
        for cp in remote:
            cp.wait_send()
        for cp in local:
            cp.wait()

    s0 = jax.ShapeDtypeStruct((4, D, W_IN_BLK), F32)
    s1 = jax.ShapeDtypeStruct((4, 3, W_ROW_BLK, D), F32)
    return pl.pallas_call(
        body, name="reduce_to_sibling", out_shape=(s0, s1, s0, s1),
        in_specs=[ANY, ANY], out_specs=(ANY, ANY, ANY, ANY),
        scratch_shapes=[pltpu.SemaphoreType.DMA((8,)), pltpu.SemaphoreType.DMA((8,)), pltpu.SemaphoreType.DMA((8,))],
    )(p_in, p3)


def _reduce_to_chips(c0, c1):
    def body(q0, q1, l0, l1, send_sems, recv_sems):
        x, y, c = _my_place()
        parts, landed = (q0, q1), (l0, l1)
        copies = []
        for a in range(2):
            for r in range(1, 4):
                chip = _chip_rel(x, y, r)
                cp = pltpu.make_async_remote_copy(
                    src_ref=parts[a].at[r], dst_ref=landed[a].at[r - 1],
                    send_sem=send_sems.at[a * 3 + r - 1], recv_sem=recv_sems.at[a * 3 + r - 1],
                    device_id=(*chip, c), device_id_type=MESH_ID)
                cp.start()
                copies.append(cp)
        for cp in copies:
            cp.wait_recv()
        for cp in copies:
            cp.wait_send()

    return pl.pallas_call(
        body, name="reduce_to_chips",
        out_shape=(jax.ShapeDtypeStruct((3, D, W_IN_BLK), F32), jax.ShapeDtypeStruct((3, 3, W_ROW_BLK, D), F32)),
        in_specs=[ANY, ANY], out_specs=(ANY, ANY),
        scratch_shapes=[pltpu.SemaphoreType.DMA((6,)), pltpu.SemaphoreType.DMA((6,))],
    )(c0, c1)


def _exchange_small(pack, srs):
    def body(pk, sr, pk_all, sr_all, send_sems, recv_sems, local_sems):
        x, y, c = _my_place()
        my_id = _dev_index(x, y, c)
        mine = [pltpu.make_async_copy(pk, pk_all.at[my_id], local_sems.at[0]),
                pltpu.make_async_copy(sr.at[my_id], sr_all.at[my_id], local_sems.at[1])]
        for cp in mine:
            cp.start()
        copies = []
        for r in range(1, N_DEV):
            peer = (jnp.bitwise_xor(x, (r >> 2) & 1), jnp.bitwise_xor(y, (r >> 1) & 1), jnp.bitwise_xor(c, r & 1))
            peer_id = _dev_index(*peer)
            for a, (src, dst) in enumerate(((pk, pk_all.at[my_id]), (sr.at[peer_id], sr_all.at[my_id]))):
                cp = pltpu.make_async_remote_copy(
                    src_ref=src, dst_ref=dst, send_sem=send_sems.at[a * 7 + r - 1], recv_sem=recv_sems.at[a * 7 + r - 1],
                    device_id=peer, device_id_type=MESH_ID)
                cp.start()
                copies.append(cp)
        for cp in copies:
            cp.wait_recv()
        for cp in copies:
            cp.wait_send()
        for cp in mine:
            cp.wait()

    return pl.pallas_call(
        body, name="exchange_small",
        out_shape=(jax.ShapeDtypeStruct((N_DEV,) + pack.shape, F32), jax.ShapeDtypeStruct(srs.shape, F32)),
        in_specs=[ANY, ANY], out_specs=(ANY, ANY),
        scratch_shapes=[pltpu.SemaphoreType.DMA((14,)), pltpu.SemaphoreType.DMA((14,)), pltpu.SemaphoreType.DMA((2,))],
    )(pack, srs)


def _rms_h(xin, norm_g):
    tm = TM_MAT

    def body(x_ref, g_ref, h_ref):
        xv = x_ref[...]
        r = lax.rsqrt(jnp.mean(xv * xv, axis=-1, keepdims=True) + EPS)
        h_ref[...] = (xv * r * g_ref[...]).astype(BF16)

    return pl.pallas_call(
        body, name="rms_h", grid=(TP // tm,),
        in_specs=[pl.BlockSpec((tm, D), lambda i: (i, 0)), pl.BlockSpec((1, D), lambda i: (0, 0))],
        out_specs=pl.BlockSpec((tm, D), lambda i: (i, 0)),
        out_shape=jax.ShapeDtypeStruct((TP, D), BF16),
    )(xin, norm_g)


def _proj(h, w_in_full):
    tm = TM_MAT

    def body(h_ref, w_ref, o_ref):
        o_ref[...] = _dot(h_ref[...], w_ref[0]).astype(BF16)

    return pl.pallas_call(
        body, name="proj", grid=(N_DEV, TP // tm),
        in_specs=[pl.BlockSpec((tm, D), lambda n, m: (m, 0)), pl.BlockSpec((1, D, W_IN_BLK), lambda n, m: (n, 0, 0))],
        out_specs=pl.BlockSpec((tm, W_IN_BLK), lambda n, m: (m, n)),
        out_shape=jax.ShapeDtypeStruct((TP, D_IN), BF16),
    )(h, w_in_full)


def _conv_fwd(proj, conv_w, conv_b):
    tm = TM_ELT

    def body(p_ref, w_ref, b_ref, c0_ref, abuf):
        i = pl.program_id(0)

        @pl.when(i == 0)
        def _():
            abuf[0:HALO, :] = jnp.zeros((HALO, D), F32)

        @pl.when(i > 0)
        def _():
            abuf[0:HALO, :] = abuf[tm:tm + HALO, :]

        ga = p_ref[:, 0:D].astype(F32)
        gb = p_ref[:, D:2 * D].astype(F32)
        abuf[HALO:HALO + tm, :] = ga * _sigmoid(gb)
        acc = jnp.broadcast_to(b_ref[...], (tm, D))
        for j in range(CONV_K):
            off = HALO - (CONV_K - 1) + j
            acc = acc + w_ref[j:j + 1, :] * abuf[off:off + tm, :]
        c0_ref[...] = acc

    return pl.pallas_call(
        body, name="conv_fwd", grid=(TP // tm,),
        in_specs=[pl.BlockSpec((tm, 2 * D), lambda i: (i, 0)), pl.BlockSpec((CONV_K, D), lambda i: (0, 0)),
                  pl.BlockSpec((1, D), lambda i: (0, 0))],
        out_specs=pl.BlockSpec((tm, D), lambda i: (i, 0)),
        out_shape=jax.ShapeDtypeStruct((TP, D), F32),
        scratch_shapes=[pltpu.VMEM((HALO + tm, D), F32)],
        compiler_params=pltpu.CompilerParams(dimension_semantics=("arbitrary",)),
    )(proj, conv_w, conv_b)


def _gates(p_ref, lbl_ref, chunk, bsc):
    lb = _sigmoid(lbl_ref[0:1, :] - lbl_ref[1:2, :])
    q_raw = p_ref[:, 0:D].astype(F32)
    f_raw = p_ref[:, D:2 * D].astype(F32)
    sq = _sigmoid(q_raw)
    q = q_raw * sq
    sg = _sigmoid(f_raw)
    f = lb + (1.0 - lb) * sg
    row = lax.broadcasted_iota(jnp.int32, (CHUNK, 1), 0) + chunk * CHUNK
    valid = row >= PAD_FRONT
    lf = jnp.where(valid, jnp.log(f), 0.0)
    k = jnp.where(valid, 1.0 - f, 0.0)
    r_i = lax.broadcasted_iota(jnp.int32, (CHUNK, CHUNK), 0)
    c_i = lax.broadcasted_iota(jnp.int32, (CHUNK, CHUNK), 1)
    causal = r_i >= c_i
    bsc[...] = _tri_matmul(causal.astype(BF16), lf)
    b = bsc[...]
    b_mid = bsc[CHUNK // 2 - 1:CHUNK // 2, :]
    b_last = bsc[CHUNK - 1:CHUNK, :]
    e_q = jnp.exp(b)
    e_qm = jnp.exp(b - b_mid)
    e_km = jnp.exp(b_mid - b)
    e_kh = jnp.exp(b_last - b)
    e_last = jnp.exp(b_last)
    return dict(lb=lb, q_raw=q_raw, sq=sq, q=q, sg=sg, f=f, k=k, valid=valid, causal=causal,
                e_q=e_q, e_qm=e_qm, e_km=e_km, e_kh=e_kh, e_last=e_last)


def _rec_fwd(proj, lb_logits):
    def body(p_ref, lbl_ref, o_ref, s_ref, st, bsc):
        n = pl.program_id(0)

        @pl.when(n == 0)
        def _():
            st[...] = jnp.zeros_like(st)

        s_ref[0] = st[...]
        g = _gates(p_ref, lbl_ref, n, bsc)
        q1 = (g["q"] * g["e_q"]).astype(BF16)
        qm = (g["q"] * g["e_qm"]).astype(BF16)
        km = (g["k"] * g["e_km"]).astype(BF16)
        kh = (g["k"] * g["e_kh"]).astype(BF16)
        for h in range(HEADS):
            sl = slice(h * HEAD_W, (h + 1) * HEAD_W)
            v = p_ref[:, 2 * D + h * HEAD_W:2 * D + (h + 1) * HEAD_W]
            att = jnp.where(g["causal"], _dot_nt(qm[:, sl], km[:, sl]), 0.0).astype(BF16)
            s_h = st[h]
            o_ref[:, sl] = _dot_nt(q1[:, sl], s_h.astype(BF16)) + _dot(att, v)
            st[h] = s_h * g["e_last"][:, sl] + _dot_tn(v, kh[:, sl])

    return pl.pallas_call(
        body, name="rec_fwd", grid=(N_CHUNK,),
        in_specs=[pl.BlockSpec((CHUNK, 3 * D), lambda n: (n, 1)), pl.BlockSpec((2, D), lambda n: (0, 0))],
        out_specs=(pl.BlockSpec((CHUNK, D), lambda n: (n, 0)),
                   pl.BlockSpec((1, HEADS, HEAD_W, HEAD_W), lambda n: (n, 0, 0, 0))),
        out_shape=(jax.ShapeDtypeStruct((TP, D), F32), jax.ShapeDtypeStruct((N_CHUNK, HEADS, HEAD_W, HEAD_W), F32)),
        scratch_shapes=[pltpu.VMEM((HEADS, HEAD_W, HEAD_W), F32), pltpu.VMEM((CHUNK, D), F32)],
        compiler_params=pltpu.CompilerParams(dimension_semantics=("arbitrary",)),
    )(proj, lb_logits)


def _rec_bwd(proj, lb_logits, d_o, s_start, dproj):
    last = N_CHUNK - 1

    def body(p_ref, lbl_ref, do_ref, s_ref, dproj_in, dp_ref, dlb_ref, dst, bsc, dq_sc, dk_sc, g_sc):
        del dproj_in
        n = pl.program_id(0)

        @pl.when(n == 0)
        def _():
            dst[...] = jnp.zeros_like(dst)
            dlb_ref[...] = jnp.zeros_like(dlb_ref)

        g = _gates(p_ref, lbl_ref, last - n, bsc)
        q1 = (g["q"] * g["e_q"]).astype(BF16)
        qm_f = g["q"] * g["e_qm"]
        km_f = g["k"] * g["e_km"]
        qm = qm_f.astype(BF16)
        km = km_f.astype(BF16)
        qm_lo = (qm_f - qm.astype(F32)).astype(BF16)
        km_lo = (km_f - km.astype(F32)).astype(BF16)
        kh_f = g["k"] * g["e_kh"]
        kh = kh_f.astype(BF16)
        for h in range(HEADS):
            sl = slice(h * HEAD_W, (h + 1) * HEAD_W)
            v = p_ref[:, 2 * D + h * HEAD_W:2 * D + (h + 1) * HEAD_W]
            d_oh = do_ref[:, sl].astype(BF16)
            s0 = s_ref[0, h]
            ds_end = dst[h]
            ds_end_b = ds_end.astype(BF16)
            att = jnp.where(g["causal"], _dot_nt(qm[:, sl], km[:, sl]), 0.0).astype(BF16)
            d_att = jnp.where(g["causal"], _dot_nt(d_oh, v), 0.0).astype(BF16)
            d_v = _dot_tn(att, d_oh) + _dot_nt(kh[:, sl], ds_end_b)
            d_qm = _dot(d_att, km[:, sl]) + _dot(d_att, km_lo[:, sl])
            d_q1 = _dot(d_oh, s0.astype(BF16))
            d_km = _dot_tn(d_att, qm[:, sl]) + _dot_tn(d_att, qm_lo[:, sl])
            d_kh = _dot(v, ds_end_b)
            dq_sc[:, sl] = d_qm * g["e_qm"][:, sl] + d_q1 * g["e_q"][:, sl]
            dk_sc[:, sl] = d_km * g["e_km"][:, sl] + d_kh * g["e_kh"][:, sl]
            g_sc[:, sl] = (jnp.sum(kh_f[:, sl] * d_kh, axis=0, keepdims=True)
                           + g["e_last"][:, sl] * jnp.sum(ds_end * s0, axis=0, keepdims=True))
            dst[h] = ds_end * g["e_last"][:, sl] + _dot_tn(d_oh, q1[:, sl])
            dp_ref[:, 2 * D + h * HEAD_W:2 * D + (h + 1) * HEAD_W] = d_v.astype(BF16)
        d_q = dq_sc[...]
        d_k = dk_sc[...]
        d_b = g["q"] * d_q - g["k"] * d_k
        anti = jnp.logical_not(g["causal"]) | (lax.broadcasted_iota(jnp.int32, (CHUNK, CHUNK), 0)
                                                == lax.broadcasted_iota(jnp.int32, (CHUNK, CHUNK), 1))
        d_lf = _tri_matmul(anti.astype(BF16), d_b) + g_sc[...]
        d_f = jnp.where(g["valid"], d_lf / g["f"] - d_k, 0.0)
        sg = g["sg"]
        dlb_ref[0:1, :] += jnp.sum(d_f * (1.0 - sg), axis=0, keepdims=True)
        dp_ref[:, 0:D] = (d_q * _dsilu(g["q_raw"], g["sq"])).astype(BF16)
        dp_ref[:, D:2 * D] = (d_f * (1.0 - g["lb"]) * sg * (1.0 - sg)).astype(BF16)

    return pl.pallas_call(
        body, name="rec_bwd", grid=(N_CHUNK,),
        in_specs=[pl.BlockSpec((CHUNK, 3 * D), lambda n: (last - n, 1)), pl.BlockSpec((2, D), lambda n: (0, 0)),
                  pl.BlockSpec((CHUNK, D), lambda n: (last - n, 0)),
                  pl.BlockSpec((1, HEADS, HEAD_W, HEAD_W), lambda n: (last - n, 0, 0, 0)), ANY],
        out_specs=(pl.BlockSpec((CHUNK, 3 * D), lambda n: (last - n, 1)), pl.BlockSpec((8, D), lambda n: (0, 0))),
        out_shape=(jax.ShapeDtypeStruct((TP, D_IN), BF16), jax.ShapeDtypeStruct((8, D), F32)),
        scratch_shapes=[pltpu.VMEM((HEADS, HEAD_W, HEAD_W), F32), pltpu.VMEM((CHUNK, D), F32),
                        pltpu.VMEM((CHUNK, D), F32), pltpu.VMEM((CHUNK, D), F32), pltpu.VMEM((1, D), F32)],
        input_output_aliases={4: 0},
        compiler_params=pltpu.CompilerParams(dimension_semantics=("arbitrary",)),
    )(proj, lb_logits, d_o, s_start, dproj)


def _mid(xin, tgt, o, c0, proj, w3, ln_g, ln_b, gnorm_g, final_g):
    tm = TM_ELT

    def body(x_ref, t_ref, o_ref, c0_ref, z_ref, gr_ref, mc_ref, mr_ref, w_ref, lng_ref, lnb_ref, gng_ref, fg_ref,
             dres_ref, do_ref, dc0_ref, dz_ref, dp_ref, a3_ref, b3_ref, red_ref, on_sc, don_sc):
        i = pl.program_id(0)

        @pl.when(i == 0)
        def _():
            red_ref[...] = jnp.zeros_like(red_ref)

        w_conv, w_rec, w_out = w_ref[0], w_ref[1], w_ref[2]
        c0v = c0_ref[...]
        mu = jnp.mean(c0v, axis=-1, keepdims=True)
        xc = c0v - mu
        rstd = lax.rsqrt(jnp.mean(xc * xc, axis=-1, keepdims=True) + EPS)
        xh = xc * rstd
        c1 = xh * lng_ref[...] + lnb_ref[...]
        s1 = _sigmoid(c1)
        c2 = c1 * s1
        z = z_ref[...].astype(F32)
        sz = _sigmoid(z)
        silu_z = z * sz
        u_conv = (c2 * silu_z).astype(BF16)
        y_conv = _dot(u_conv, w_conv)
        ov = o_ref[...]
        r3 = []
        for h in range(HEADS):
            sl = slice(h * HEAD_W, (h + 1) * HEAD_W)
            oh = ov[:, sl]
            r_h = lax.rsqrt(jnp.mean(oh * oh, axis=-1, keepdims=True) + EPS)
            r3.append(r_h)
            on_sc[:, sl] = oh * r_h
        o_n = on_sc[...]
        o_g = o_n * gng_ref[...]
        gr = gr_ref[...].astype(F32)
        sgr = _sigmoid(gr)
        silu_g = gr * sgr
        u_rec = (o_g * silu_g).astype(BF16)
        y_rec = _dot(u_rec, w_rec)
        mc = mc_ref[...].astype(F32)
        mr = mr_ref[...].astype(F32)
        smc = _sigmoid(mc)
        smr = _sigmoid(mr)
        merged = (smc * y_conv + smr * y_rec).astype(BF16)
        res = x_ref[...] + _dot(merged, w_out)
        r2 = lax.rsqrt(jnp.mean(res * res, axis=-1, keepdims=True) + EPS)
        xh2 = res * r2
        row = lax.broadcasted_iota(jnp.int32, (tm, 1), 0) + i * tm
        real = row >= ROW0
        diff = jnp.where(real, xh2 * fg_ref[...] - t_ref[...], 0.0)
        d_y = diff * (1.0 / D)
        d_xh2 = d_y * fg_ref[...]
        d_res = r2 * (d_xh2 - xh2 * jnp.mean(d_xh2 * xh2, axis=-1, keepdims=True))
        dres_ref[...] = d_res
        d_res_b = d_res.astype(BF16)
        d_merged = _dot_nt(d_res_b, w_out)
        d_yc = (d_merged * smc).astype(BF16)
        d_yr = (d_merged * smr).astype(BF16)
        dp_ref[:, D:2 * D] = (d_merged * y_conv * smc * (1.0 - smc)).astype(BF16)
        dp_ref[:, 2 * D:3 * D] = (d_merged * y_rec * smr * (1.0 - smr)).astype(BF16)
        d_ur = _dot_nt(d_yr, w_rec)
        d_og = d_ur * silu_g
        dp_ref[:, 0:D] = (d_ur * o_g * _dsilu(gr, sgr)).astype(BF16)
        d_on = d_og * gng_ref[...]
        for h in range(HEADS):
            sl = slice(h * HEAD_W, (h + 1) * HEAD_W)
            d_h = d_on[:, sl]
            n_h = o_n[:, sl]
            don_sc[:, sl] = r3[h] * (d_h - n_h * jnp.mean(d_h * n_h, axis=-1, keepdims=True))
        do_ref[...] = don_sc[...]
        d_uc = _dot_nt(d_yc, w_conv)
        d_c2 = d_uc * silu_z
        dz_ref[...] = (d_uc * c2 * _dsilu(z, sz)).astype(BF16)
        d_c1 = d_c2 * _dsilu(c1, s1)
        d_xh = d_c1 * lng_ref[...]
        d_c0 = rstd * (d_xh - jnp.mean(d_xh, axis=-1, keepdims=True)
                       - xh * jnp.mean(d_xh * xh, axis=-1, keepdims=True))
        dc0_ref[...] = d_c0
        a3_ref[0] = u_conv
        b3_ref[0] = d_yc
        a3_ref[1] = u_rec
        b3_ref[1] = d_yr
        a3_ref[2] = merged
        b3_ref[2] = d_res_b
        def colsum(vv):
            return jnp.sum(vv, axis=0, keepdims=True)

        red_ref[0:1, :] += colsum(d_y * xh2)
        red_ref[1:2, :] += colsum(d_og * o_n)
        red_ref[2:3, :] += colsum(d_c1 * xh)
        red_ref[3:4, :] += colsum(d_c1)
        red_ref[4:5, :] += colsum(d_c0)
        red_ref[5:6, :] += colsum(diff * diff) * (0.5 / D)

    def row_block(width, col):
        return pl.BlockSpec((tm, width), lambda i: (i, col))

    def const_block(shape):
        return pl.BlockSpec(shape, lambda i: (0,) * len(shape))

    stack = jax.ShapeDtypeStruct((3, TP, D), BF16)
    stack_spec = pl.BlockSpec((3, tm, D), lambda i: (0, i, 0))
    return pl.pallas_call(
        body, name="mid", grid=(TP // tm,),
        in_specs=[row_block(D, 0), row_block(D, 0), row_block(D, 0), row_block(D, 0),
                  row_block(D, 2), row_block(D, 6), row_block(D, 7), row_block(D, 8),
                  pl.BlockSpec((3, D, D), lambda i: (0, 0, 0), pipeline_mode=pl.Buffered(1)),
                  const_block((1, D)), const_block((1, D)), const_block((1, D)), const_block((1, D))],
        out_specs=(row_block(D, 0), row_block(D, 0), row_block(D, 0), row_block(D, 0), row_block(3 * D, 2),
                   stack_spec, stack_spec, const_block((8, D))),
        out_shape=(jax.ShapeDtypeStruct((TP, D), F32), jax.ShapeDtypeStruct((TP, D), F32),
                   jax.ShapeDtypeStruct((TP, D), F32), jax.ShapeDtypeStruct((TP, D), BF16),
                   jax.ShapeDtypeStruct((TP, D_IN), BF16), stack, stack, jax.ShapeDtypeStruct((8, D), F32)),
        scratch_shapes=[pltpu.VMEM((tm, D), F32), pltpu.VMEM((tm, D), F32)],
        compiler_params=pltpu.CompilerParams(dimension_semantics=("arbitrary",), vmem_limit_bytes=60 * 1024 * 1024),
    )(xin, tgt, o, c0, proj, proj, proj, proj, w3, ln_g, ln_b, gnorm_g, final_g)


def _conv_bwd(proj, d_c0, d_z, conv_w, dproj):
    tm = TM_ELT
    n_tile = TP // tm
    lastt = n_tile - 1

    def body(p_ref, dc_ref, dz_ref, w_ref, dproj_in, dp_ref, dw_ref, dbuf, acc):
        del dproj_in
        i = pl.program_id(0)

        @pl.when(i == 0)
        def _():
            dbuf[tm:tm + HALO, :] = jnp.zeros((HALO, D), F32)
            acc[...] = jnp.zeros_like(acc)

        @pl.when(i > 0)
        def _():
            dbuf[tm:tm + HALO, :] = dbuf[0:HALO, :]

        dbuf[0:tm, :] = dc_ref[...]
        ga = p_ref[:, 0:D].astype(F32)
        gb = p_ref[:, D:2 * D].astype(F32)
        sb = _sigmoid(gb)
        a = ga * sb
        d_a = jnp.zeros((tm, D), F32)
        for j in range(CONV_K):
            off = CONV_K - 1 - j
            shifted = dbuf[off:off + tm, :]
            d_a = d_a + w_ref[j:j + 1, :] * shifted
            acc[j] += jnp.sum((a * shifted).reshape(tm // 8, 8, D), axis=0)
        dp_ref[:, 0:D] = (d_a * sb).astype(BF16)
        dp_ref[:, D:2 * D] = (d_a * ga * sb * (1.0 - sb)).astype(BF16)
        dp_ref[:, 2 * D:3 * D] = dz_ref[...]

        @pl.when(i == lastt)
        def _():
            for j in range(CONV_K):
                dw_ref[j:j + 1, :] = jnp.sum(acc[j], axis=0, keepdims=True)
            dw_ref[CONV_K:CONV_K + 1, :] = jnp.zeros((1, D), F32)

    return pl.pallas_call(
        body, name="conv_bwd", grid=(n_tile,),
        in_specs=[pl.BlockSpec((tm, 2 * D), lambda i: (lastt - i, 0)), pl.BlockSpec((tm, D), lambda i: (lastt - i, 0)),
                  pl.BlockSpec((tm, D), lambda i: (lastt - i, 0)), pl.BlockSpec((CONV_K, D), lambda i: (0, 0)), ANY],
        out_specs=(pl.BlockSpec((tm, 3 * D), lambda i: (lastt - i, 0)), pl.BlockSpec((CONV_K + 1, D), lambda i: (0, 0))),
        out_shape=(jax.ShapeDtypeStruct((TP, D_IN), BF16), jax.ShapeDtypeStruct((CONV_K + 1, D), F32)),
        scratch_shapes=[pltpu.VMEM((tm + HALO, D), F32), pltpu.VMEM((CONV_K, 8, D), F32)],
        input_output_aliases={4: 0},
        compiler_params=pltpu.CompilerParams(dimension_semantics=("arbitrary",)),
    )(proj, d_c0, d_z, conv_w, dproj)


def _wgrad3(a3, b3):
    tt = TM_MAT

    def body(a_ref, b_ref, o_ref):
        @pl.when(pl.program_id(1) == 0)
        def _():
            o_ref[...] = jnp.zeros_like(o_ref)

        o_ref[0] += _dot_tn(a_ref[0], b_ref[0])

    return pl.pallas_call(
        body, name="wgrad3", grid=(3, TP // tt),
        in_specs=[pl.BlockSpec((1, tt, D), lambda g, t: (g, t, 0)), pl.BlockSpec((1, tt, D), lambda g, t: (g, t, 0))],
        out_specs=pl.BlockSpec((1, D, D), lambda g, t: (g, 0, 0)),
        out_shape=jax.ShapeDtypeStruct((3, D, D), F32),
        compiler_params=pltpu.CompilerParams(dimension_semantics=("arbitrary", "arbitrary")),
    )(a3, b3)


def _wgrad_in(h, dproj):
    tt = TM_MAT

    def body(a_ref, b_ref, o_ref):
        @pl.when(pl.program_id(1) == 0)
        def _():
            o_ref[...] = jnp.zeros_like(o_ref)

        o_ref[0] += _dot_tn(a_ref[...], b_ref[...])

    return pl.pallas_call(
        body, name="wgrad_in", grid=(N_DEV, TP // tt),
        in_specs=[pl.BlockSpec((tt, D), lambda n, t: (t, 0)), pl.BlockSpec((tt, W_IN_BLK), lambda n, t: (t, n))],
        out_specs=pl.BlockSpec((1, D, W_IN_BLK), lambda n, t: (n, 0, 0)),
        out_shape=jax.ShapeDtypeStruct((N_DEV, D, W_IN_BLK), F32),
        compiler_params=pltpu.CompilerParams(dimension_semantics=("arbitrary", "arbitrary")),
    )(h, dproj)


def _dh_and_norm_bwd(dproj, w_in_full, xin, d_res, norm_g):
    tm = TM_MAT
    n_k = N_DEV

    def body(dp_ref, w_ref, x_ref, dr_ref, g_ref, dx_ref, dg_ref, acc):
        m = pl.program_id(0)
        k = pl.program_id(1)

        @pl.when(k == 0)
        def _():
            acc[...] = jnp.zeros_like(acc)

        acc[...] += _dot_nt(dp_ref[...], w_ref[0])

        @pl.when((k == n_k - 1) & (m == 0))
        def _():
            dg_ref[...] = jnp.zeros_like(dg_ref)

        @pl.when(k == n_k - 1)
        def _():
            xv = x_ref[...]
            r1 = lax.rsqrt(jnp.mean(xv * xv, axis=-1, keepdims=True) + EPS)
            xh = xv * r1
            d_h = acc[...]
            dg_ref[0:1, :] += jnp.sum(d_h * xh, axis=0, keepdims=True)
            d_xh = d_h * g_ref[...]
            dx_ref[...] = dr_ref[...] + r1 * (d_xh - xh * jnp.mean(d_xh * xh, axis=-1, keepdims=True))

    return pl.pallas_call(
        body, name="dh_norm_bwd", grid=(TP // tm, n_k),
        in_specs=[pl.BlockSpec((tm, W_IN_BLK), lambda m, k: (m, k)), pl.BlockSpec((1, D, W_IN_BLK), lambda m, k: (k, 0, 0)),
                  pl.BlockSpec((tm, D), lambda m, k: (m, 0)), pl.BlockSpec((tm, D), lambda m, k: (m, 0)),
                  pl.BlockSpec((1, D), lambda m, k: (0, 0))],
        out_specs=(pl.BlockSpec((tm, D), lambda m, k: (m, 0)), pl.BlockSpec((8, D), lambda m, k: (0, 0))),
        out_shape=(jax.ShapeDtypeStruct((TP, D), F32), jax.ShapeDtypeStruct((8, D), F32)),
        scratch_shapes=[pltpu.VMEM((tm, D), F32)],
        compiler_params=pltpu.CompilerParams(dimension_semantics=("arbitrary", "arbitrary")),
    )(dproj, w_in_full, xin, d_res, norm_g)


def _add2(a, b, name):
    rows, cols = a.shape
    tr = 512

    def body(a_ref, b_ref, o_ref):
        o_ref[...] = a_ref[...] + b_ref[...]

    spec = pl.BlockSpec((tr, cols), lambda i: (i, 0))
    return pl.pallas_call(body, name=name, grid=(rows // tr,), in_specs=[spec, spec], out_specs=spec,
                          out_shape=jax.ShapeDtypeStruct((rows, cols), F32))(a, b)


def _sum_adamw(own, landed, w, m, v, tr, name):
    rows, cols = w.shape
    n_t = rows // tr

    def body(o_ref, l1_ref, l2_ref, l3_ref, w_ref, m_ref, v_ref, g_ref, d_ref, m2_ref, v2_ref):
        g = ((o_ref[...] + l1_ref[...]) + l2_ref[...]) + l3_ref[...]
        delta, m2, v2 = _adamw(w_ref[...], g, m_ref[...], v_ref[...])
        g_ref[...] = g
        d_ref[...] = delta
        m2_ref[...] = m2
        v2_ref[...] = v2

    def spec(k):
        return pl.BlockSpec((tr, cols), lambda i: (i + k * n_t, 0))

    out = jax.ShapeDtypeStruct((rows, cols), F32)
    return pl.pallas_call(
        body, name=name, grid=(n_t,),
        in_specs=[spec(0), spec(0), spec(1), spec(2), spec(0), spec(0), spec(0)],
        out_specs=(spec(0),) * 4, out_shape=(out,) * 4,
    )(own, landed, landed, landed, w, m, v)


def _small_update(pack_all, srs_all, lb_logits, p8, m8, v8, ws, ms, vs):
    def body(pk_ref, sr_ref, lbl_ref, p_ref, m_ref, v_ref, ws_ref, ms_ref, vs_ref,
             g8_ref, d8_ref, m8_ref, v8_ref, loss_ref, gs_ref, ds_ref, ms2_ref, vs2_ref):
        tot = pk_ref[0]
        tot_s = sr_ref[0]
        for d in range(1, N_DEV):
            tot = tot + pk_ref[d]
            tot_s = tot_s + sr_ref[d]
        p0 = _sigmoid(lbl_ref[0:1, :] - lbl_ref[1:2, :])
        row = lax.broadcasted_iota(jnp.int32, (8, D), 0)
        d_lb = jnp.sum(jnp.where(row == 4, tot, 0.0), axis=0, keepdims=True)
        d_l0 = d_lb * p0 * (1.0 - p0)
        loss_ref[...] = jnp.sum(jnp.where(row == 5, tot, 0.0), keepdims=True).reshape(1, 1)
        g8 = jnp.where(row == 4, d_l0, jnp.where(row == 5, -d_l0, tot))
        delta, m2, v2 = _adamw(p_ref[...], g8, m_ref[...], v_ref[...])
        g8_ref[...] = g8
        d8_ref[...] = delta
        m8_ref[...] = m2
        v8_ref[...] = v2
        delta, m2, v2 = _adamw(ws_ref[...], tot_s, ms_ref[...], vs_ref[...])
        gs_ref[...] = tot_s
        ds_ref[...] = delta
        ms2_ref[...] = m2
        vs2_ref[...] = v2

    o8 = jax.ShapeDtypeStruct((8, D), F32)
    os_ = jax.ShapeDtypeStruct((SMALL_ROWS, HEAD_W), F32)
    return pl.pallas_call(
        body, name="small_update",
        out_shape=(o8, o8, o8, o8, jax.ShapeDtypeStruct((1, 1), F32), os_, os_, os_, os_),
    )(pack_all, srs_all, lb_logits, p8, m8, v8, ws, ms, vs)


def _local_step(x_seq, target, meta_full, norm_g, w_in_full, conv_w_full, conv_b, ln_g, ln_b, w3_full, lb_logits,
                gnorm_g, final_g):
    xin = jnp.concatenate([jnp.zeros((PAD_FRONT, D), F32), meta_full, x_seq], axis=0)
    tgt = jnp.concatenate([jnp.zeros((ROW0, D), F32), target], axis=0)
    fg = final_g.reshape(1, D)
    h = _rms_h(xin, norm_g)
    proj = _proj(h, w_in_full)
    c0 = _conv_fwd(proj, conv_w_full, conv_b)
    o, s_start = _rec_fwd(proj, lb_logits)
    d_res, d_o, d_c0, d_z, dproj, a3, b3, red = _mid(xin, tgt, o, c0, proj, w3_full, ln_g, ln_b, gnorm_g, fg)
    dproj, dlb = _rec_bwd(proj, lb_logits, d_o, s_start, dproj)
    dproj, d_conv_w = _conv_bwd(proj, d_c0, d_z, conv_w_full, dproj)
    p3 = _wgrad3(a3, b3)
    p_in = _wgrad_in(h, dproj)
    d_xin, dng = _dh_and_norm_bwd(dproj, w_in_full, xin, d_res, norm_g)
    pack = jnp.concatenate([dng[0:1], red[4:5], red[2:3], red[3:4], dlb[0:1], red[5:6], red[1:2], red[0:1]], axis=0)
    return d_xin, p_in, p3, d_conv_w, pack


def kernel(x, meta_tokens, norm_g, w_in, conv_w, conv_b, ln_g, ln_b, w_conv_out, lb_logits, gnorm_g, w_rec_out, w_out, final_g, loss_target, m_meta_tokens, m_norm_g, m_w_in, m_conv_w, m_conv_b, m_ln_g, m_ln_b, m_w_conv_out, m_lb_logits, m_gnorm_g, m_w_rec_out, m_w_out, m_final_g, v_meta_tokens, v_norm_g, v_w_in, v_conv_w, v_conv_b, v_ln_g, v_ln_b, v_w_conv_out, v_lb_logits, v_gnorm_g, v_w_rec_out, v_w_out, v_final_g):
    def small_pack(cw, mt):
        return jnp.concatenate([cw[0], jnp.zeros((1, HEAD_W), F32), mt], axis=0)

    def stack3(a, b, c):
        return jnp.concatenate([a, b, c], axis=0)

    def stack8(ng, cb, lg, lb_, lbl, gg, fg):
        return jnp.concatenate([ng, cb, lg, lb_, lbl, gg, fg.reshape(1, D)], axis=0)

    w3_s = stack3(w_conv_out, w_rec_out, w_out)
    ws_s = small_pack(conv_w, meta_tokens)
    w_in_b, w3_b = _cast_shards(w_in[0], w3_s)
    w_in_full, w3_full, small_full = _all_gather_weights(w_in_b, w3_b, ws_s)
    small_full = jnp.transpose(small_full, (1, 0, 2)).reshape(SMALL_ROWS, D)
    conv_w_full = small_full[0:CONV_K]
    meta_full = small_full[META_ROW:META_ROW + N_META]

    d_xin, p_in, p3, d_conv_w, pack = _local_step(
        x[0], loss_target[0], meta_full, norm_g, w_in_full, conv_w_full, conv_b, ln_g, ln_b, w3_full, lb_logits,
        gnorm_g, final_g)

    mine0, mine1, land0, land1 = _reduce_to_sibling(p_in, p3)
    chip0 = _add2(mine0.reshape(4 * D, W_IN_BLK), land0.reshape(4 * D, W_IN_BLK), "chip_sum_in")
    chip1 = _add2(mine1.reshape(12 * W_ROW_BLK, D), land1.reshape(12 * W_ROW_BLK, D), "chip_sum_3")
    far0, far1 = _reduce_to_chips(chip0.reshape(4, D, W_IN_BLK), chip1.reshape(4, 3, W_ROW_BLK, D))
    g_in, d_in, m_in, v_in = _sum_adamw(chip0, far0.reshape(3 * D, W_IN_BLK), w_in[0], m_w_in[0], v_w_in[0], 256,
                                        "adamw_in")
    g_3, d_3, m_3, v_3 = _sum_adamw(
        chip1, far1.reshape(9 * W_ROW_BLK, D), w3_s.reshape(3 * W_ROW_BLK, D),
        stack3(m_w_conv_out, m_w_rec_out, m_w_out).reshape(3 * W_ROW_BLK, D),
        stack3(v_w_conv_out, v_w_rec_out, v_w_out).reshape(3 * W_ROW_BLK, D), 3 * W_ROW_BLK, "adamw_3")

    srs = jnp.concatenate([d_conv_w, d_xin[PAD_FRONT:ROW0]], axis=0)
    srs = jnp.transpose(srs.reshape(SMALL_ROWS, N_DEV, HEAD_W), (1, 0, 2))
    pack_all, srs_all = _exchange_small(pack, srs)
    g8, d8, m8, v8, loss, gs, ds, ms, vs = _small_update(
        pack_all, srs_all, lb_logits,
        stack8(norm_g, conv_b, ln_g, ln_b, lb_logits, gnorm_g, final_g),
        stack8(m_norm_g, m_conv_b, m_ln_g, m_ln_b, m_lb_logits, m_gnorm_g, m_final_g),
        stack8(v_norm_g, v_conv_b, v_ln_g, v_ln_b, v_lb_logits, v_gnorm_g, v_final_g),
        ws_s, small_pack(m_conv_w, m_meta_tokens), small_pack(v_conv_w, v_meta_tokens))

    def unpack(a_in, a_3, a_s, a_8):
        t3 = a_3.reshape(3, 1, W_ROW_BLK, D)
        return (a_s[META_ROW:META_ROW + N_META], a_8[0:1], a_in[None], a_s[0:CONV_K][None], a_8[1:2], a_8[2:3],
                a_8[3:4], t3[0], a_8[4:6], a_8[6:7], t3[1], t3[2], a_8[7])

    grad_x = d_xin[ROW0:][None]
    return (loss.reshape(()), grad_x, *unpack(g_in, g_3, gs, g8), *unpack(d_in, d_3, ds, d8),
            *unpack(m_in, m_3, ms, m8), *unpack(v_in, v_3, vs, v8))
```

```python
import functools

import jax
import jax.numpy as jnp
from jax import lax
from jax.experimental import pallas as pl
from jax.experimental.pallas import tpu as pltpu

F32 = jnp.float32
BF16 = jnp.bfloat16

D = 1024
SEQ = 4096
N_META = 16
CHUNK = 64
PAD_FRONT = 48
ROW0 = PAD_FRONT + N_META
TP = ROW0 + SEQ
N_CHUNK = TP // CHUNK
HEADS = 8
HEAD_W = 128
D_IN = 9 * D
N_DEV = 8
W_IN_BLK = D_IN // N_DEV
W_ROW_BLK = D // N_DEV
CONV_K = 31
SMALL_ROWS = 48
META_ROW = 32
EPS = 1e-6
HALO = 32

TM_MAT = 832
TM_ELT = 208

ADAM_LR = 0.001
ADAM_B1 = 0.9
ADAM_B2 = 0.999
ADAM_EPS = 1e-08
ADAM_WD = 0.01
ADAM_STEP = 10

MESH_ID = pl.DeviceIdType.MESH
ANY = pl.BlockSpec(memory_space=pl.ANY)


def _sigmoid(v):
    return jax.nn.sigmoid(v)


def _dsilu(v, s):
    return s * (1.0 + v * (1.0 - s))


def _dot(a, b):
    return jnp.dot(a, b, preferred_element_type=F32)


def _dot_nt(a, b):
    return lax.dot_general(a, b, (((1,), (1,)), ((), ())), preferred_element_type=F32)


def _dot_tn(a, b):
    return lax.dot_general(a, b, (((0,), (0,)), ((), ())), preferred_element_type=F32)


def _split3(v):
    hi = v.astype(BF16)
    r1 = v - hi.astype(F32)
    mid = r1.astype(BF16)
    lo = (r1 - mid.astype(F32)).astype(BF16)
    return hi, mid, lo


def _tri_matmul(tri, v):
    hi, mid, lo = _split3(v)
    return _dot(tri, hi) + _dot(tri, mid) + _dot(tri, lo)


def _adamw(w, g, m, v):
    m2 = ADAM_B1 * m + (1.0 - ADAM_B1) * g
    v2 = ADAM_B2 * v + (1.0 - ADAM_B2) * jnp.square(g)
    m_hat = m2 / (1.0 - ADAM_B1 ** ADAM_STEP)
    v_hat = v2 / (1.0 - ADAM_B2 ** ADAM_STEP)
    delta = -ADAM_LR * (m_hat / (jnp.sqrt(v_hat) + ADAM_EPS) + ADAM_WD * w)
    return delta, m2, v2


def _my_place():
    return lax.axis_index("x"), lax.axis_index("y"), lax.axis_index("c")


def _dev_index(px, py, pc):
    return 4 * px + 2 * py + pc


def _cast_shards(w_in_s, w3_s):
    def body(a_ref, b_ref, oa_ref, ob_ref):
        oa_ref[...] = a_ref[...].astype(BF16)
        ob_ref[...] = b_ref[...].astype(BF16)

    return pl.pallas_call(
        body, name="cast_shards",
        out_shape=(jax.ShapeDtypeStruct(w_in_s.shape, BF16), jax.ShapeDtypeStruct(w3_s.shape, BF16)),
    )(w_in_s, w3_s)


def _all_gather_weights(w_in_b, w3_b, small_s):
    n_arr = 3

    def body(s0, s1, s2, o0, o1, o2, send_sems, recv_sems, local_sems):
        x, y, c = _my_place()
        me, sibling = (x, y, c), (x, y, 1 - c)
        chips = [(1 - x, y), (x, 1 - y), (1 - x, 1 - y)]
        srcs, outs = (s0, s1, s2), (o0, o1, o2)

        def block(a, place):
            d = _dev_index(*place)
            if a == 1:
                return outs[1].at[:, pl.ds(pl.multiple_of(d * W_ROW_BLK, W_ROW_BLK), W_ROW_BLK), :]
            return outs[a].at[d]

        def copy(a, k, place, to, from_src=False):
            return pltpu.make_async_remote_copy(
                src_ref=srcs[a] if from_src else block(a, place), dst_ref=block(a, place),
                send_sem=send_sems.at[a * 7 + k], recv_sem=recv_sems.at[a * 7 + k],
                device_id=to, device_id_type=MESH_ID)

        mine = [pltpu.make_async_copy(srcs[a], block(a, me), local_sems.at[a]) for a in range(n_arr)]
        for cp in mine:
            cp.start()
        started = []
        for a in range(n_arr):
            cp = copy(a, 0, me, sibling, from_src=True)
            cp.start()
            started.append(cp)
            for j, chip in enumerate(chips):
                cp = copy(a, 1 + j, me, (*chip, c), from_src=True)
                cp.start()
                started.append(cp)
        for j, chip in enumerate(chips):
            for a in range(n_arr):
                copy(a, 1 + j, (*chip, c), me).wait_recv()
                cp = copy(a, 4 + j, (*chip, c), sibling)
                cp.start()
                started.append(cp)
        for a in range(n_arr):
            copy(a, 0, sibling, me).wait_recv()
            for j, chip in enumerate(chips):
                copy(a, 4 + j, (*chip, 1 - c), me).wait_recv()
        for cp in started:
            cp.wait_send()
        for cp in mine:
            cp.wait()

    return pl.pallas_call(
        body, name="all_gather_weights",
        out_shape=(jax.ShapeDtypeStruct((N_DEV,) + w_in_b.shape, BF16),
                   jax.ShapeDtypeStruct((3, D, D), BF16),
                   jax.ShapeDtypeStruct((N_DEV,) + small_s.shape, F32)),
        in_specs=[ANY, ANY, ANY], out_specs=(ANY, ANY, ANY),
        scratch_shapes=[pltpu.SemaphoreType.DMA((7 * n_arr,)), pltpu.SemaphoreType.DMA((7 * n_arr,)),
                        pltpu.SemaphoreType.DMA((n_arr,))],
    )(w_in_b, w3_b, small_s)


def _chip_rel(x, y, r):
    return (jnp.bitwise_xor(x, r >> 1), jnp.bitwise_xor(y, r & 1))


def _reduce_to_sibling(p_in, p3):
    def body(q0, q1, l0, l1, send_sems, recv_sems):
        x, y, c = _my_place()
        sibling = (x, y, 1 - c)
        parts, landed = (q0, q1), (l0, l1)

        def block(a, place):
            d = _dev_index(*place)
            if a == 1:
                return parts[1].at[:, pl.ds(pl.multiple_of(d * W_ROW_BLK, W_ROW_BLK), W_ROW_BLK), :]
            return parts[0].at[d]

        remote = []
        for a in range(2):
            for r in range(4):
                chip = _chip_rel(x, y, r)
                cp = pltpu.make_async_remote_copy(
                    src_ref=block(a, (*chip, 1 - c)), dst_ref=landed[a].at[r],
                    send_sem=send_sems.at[a * 4 + r], recv_sem=recv_sems.at[a * 4 + r],
                    device_id=sibling, device_id_type=MESH_ID)
                cp.start()
                remote.append(cp)
        for cp in remote:
            cp.wait_recv()
        for cp in remote:
            cp.wait_send()

    s0 = jax.ShapeDtypeStruct((4, D, W_IN_BLK), F32)
    s1 = jax.ShapeDtypeStruct((4, 3, W_ROW_BLK, D), F32)
    return pl.pallas_call(
        body, name="reduce_to_sibling", out_shape=(s0, s1),
        in_specs=[ANY, ANY], out_specs=(ANY, ANY),
        scratch_shapes=[pltpu.SemaphoreType.DMA((8,)), pltpu.SemaphoreType.DMA((8,))],
    )(p_in, p3)


def _chip_sums(p_in, p3, land0, land1, ids):
    tr = 256

    def body_in(ids_ref, p_ref, l_ref, o_ref):
        del ids_ref
        o_ref[...] = p_ref[...] + l_ref[...]

    chip0 = pl.pallas_call(
        body_in, name="chip_sum_in",
        grid_spec=pltpu.PrefetchScalarGridSpec(
            num_scalar_prefetch=1, grid=(4, D // tr),
            in_specs=[pl.BlockSpec((1, tr, W_IN_BLK), lambda r, i, ids_ref: (ids_ref[r], i, 0)),
                      pl.BlockSpec((1, tr, W_IN_BLK), lambda r, i, ids_ref: (r, i, 0))],
            out_specs=pl.BlockSpec((1, tr, W_IN_BLK), lambda r, i, ids_ref: (r, i, 0))),
        out_shape=jax.ShapeDtypeStruct((4, D, W_IN_BLK), F32),
    )(ids, p_in, land0)

    def body_3(ids_ref, p_ref, l_ref, o_ref):
        del ids_ref
        o_ref[0] = p_ref[...] + l_ref[0]

    chip1 = pl.pallas_call(
        body_3, name="chip_sum_3",
        grid_spec=pltpu.PrefetchScalarGridSpec(
            num_scalar_prefetch=1, grid=(4,),
            in_specs=[pl.BlockSpec((3, W_ROW_BLK, D), lambda r, ids_ref: (0, ids_ref[r], 0)),
                      pl.BlockSpec((1, 3, W_ROW_BLK, D), lambda r, ids_ref: (r, 0, 0, 0))],
            out_specs=pl.BlockSpec((1, 3, W_ROW_BLK, D), lambda r, ids_ref: (r, 0, 0, 0))),
        out_shape=jax.ShapeDtypeStruct((4, 3, W_ROW_BLK, D), F32),
    )(ids, p3, land1)
    return chip0, chip1


def _reduce_to_chips(c0, c1):
    def body(q0, q1, l0, l1, send_sems, recv_sems):
        x, y, c = _my_place()
        parts, landed = (q0, q1), (l0, l1)
        copies = []
        for a in range(2):
            for r in range(1, 4):
                chip = _chip_rel(x, y, r)
                cp = pltpu.make_async_remote_copy(
                    src_ref=parts[a].at[r], dst_ref=landed[a].at[r - 1],
                    send_sem=send_sems.at[a * 3 + r - 1], recv_sem=recv_sems.at[a * 3 + r - 1],
                    device_id=(*chip, c), device_id_type=MESH_ID)
                cp.start()
                copies.append(cp)
        for cp in copies:
            cp.wait_recv()
        for cp in copies:
            cp.wait_send()

    return pl.pallas_call(
        body, name="reduce_to_chips",
        out_shape=(jax.ShapeDtypeStruct((3, D, W_IN_BLK), F32), jax.ShapeDtypeStruct((3, 3, W_ROW_BLK, D), F32)),
        in_specs=[ANY, ANY], out_specs=(ANY, ANY),
        scratch_shapes=[pltpu.SemaphoreType.DMA((6,)), pltpu.SemaphoreType.DMA((6,))],
    )(c0, c1)


def _exchange_small(pack, srs):
    def body(pk, sr, pk_all, sr_all, send_sems, recv_sems, local_sems):
        x, y, c = _my_place()
        my_id = _dev_index(x, y, c)
        mine = [pltpu.make_async_copy(pk, pk_all.at[my_id], local_sems.at[0]),
                pltpu.make_async_copy(sr.at[my_id], sr_all.at[my_id], local_sems.at[1])]
        for cp in mine:
            cp.start()
        copies = []
        for r in range(1, N_DEV):
            peer = (jnp.bitwise_xor(x, (r >> 2) & 1), jnp.bitwise_xor(y, (r >> 1) & 1), jnp.bitwise_xor(c, r & 1))
            peer_id = _dev_index(*peer)
            for a, (src, dst) in enumerate(((pk, pk_all.at[my_id]), (sr.at[peer_id], sr_all.at[my_id]))):
                cp = pltpu.make_async_remote_copy(
                    src_ref=src, dst_ref=dst, send_sem=send_sems.at[a * 7 + r - 1], recv_sem=recv_sems.at[a * 7 + r - 1],
                    device_id=peer, device_id_type=MESH_ID)
                cp.start()
                copies.append(cp)
        for cp in copies:
            cp.wait_recv()
        for cp in copies:
            cp.wait_send()
        for cp in mine:
            cp.wait()

    return pl.pallas_call(
        body, name="exchange_small",
        out_shape=(jax.ShapeDtypeStruct((N_DEV,) + pack.shape, F32), jax.ShapeDtypeStruct(srs.shape, F32)),
        in_specs=[ANY, ANY], out_specs=(ANY, ANY),
        scratch_shapes=[pltpu.SemaphoreType.DMA((14,)), pltpu.SemaphoreType.DMA((14,)), pltpu.SemaphoreType.DMA((2,))],
    )(pack, srs)


def _rms_h(xin, norm_g):
    tm = TM_MAT

    def body(x_ref, g_ref, h_ref):
        xv = x_ref[...]
        r = lax.rsqrt(jnp.mean(xv * xv, axis=-1, keepdims=True) + EPS)
        h_ref[...] = (xv * r * g_ref[...]).astype(BF16)

    return pl.pallas_call(
        body, name="rms_h", grid=(TP // tm,),
        in_specs=[pl.BlockSpec((tm, D), lambda i: (i, 0)), pl.BlockSpec((1, D), lambda i: (0, 0))],
        out_specs=pl.BlockSpec((tm, D), lambda i: (i, 0)),
        out_shape=jax.ShapeDtypeStruct((TP, D), BF16),
    )(xin, norm_g)


def _proj(h, w_in_full):
    tm = TM_MAT

    def body(h_ref, w_ref, o_ref):
        o_ref[...] = _dot(h_ref[...], w_ref[0]).astype(BF16)

    return pl.pallas_call(
        body, name="proj", grid=(N_DEV, TP // tm),
        in_specs=[pl.BlockSpec((tm, D), lambda n, m: (m, 0)), pl.BlockSpec((1, D, W_IN_BLK), lambda n, m: (n, 0, 0))],
        out_specs=pl.BlockSpec((tm, W_IN_BLK), lambda n, m: (m, n)),
        out_shape=jax.ShapeDtypeStruct((TP, D_IN), BF16),
    )(h, w_in_full)


def _conv_fwd(proj, conv_w, conv_b):
    tm = TM_ELT

    def body(p_ref, w_ref, b_ref, c0_ref, abuf):
        i = pl.program_id(0)

        @pl.when(i == 0)
        def _():
            abuf[0:HALO, :] = jnp.zeros((HALO, D), F32)

        @pl.when(i > 0)
        def _():
            abuf[0:HALO, :] = abuf[tm:tm + HALO, :]

        ga = p_ref[:, 0:D].astype(F32)
        gb = p_ref[:, D:2 * D].astype(F32)
        abuf[HALO:HALO + tm, :] = ga * _sigmoid(gb)
        acc = jnp.broadcast_to(b_ref[...], (tm, D))
        for j in range(CONV_K):
            off = HALO - (CONV_K - 1) + j
            acc = acc + w_ref[j:j + 1, :] * abuf[off:off + tm, :]
        c0_ref[...] = acc

    return pl.pallas_call(
        body, name="conv_fwd", grid=(TP // tm,),
        in_specs=[pl.BlockSpec((tm, 2 * D), lambda i: (i, 0)), pl.BlockSpec((CONV_K, D), lambda i: (0, 0)),
                  pl.BlockSpec((1, D), lambda i: (0, 0))],
        out_specs=pl.BlockSpec((tm, D), lambda i: (i, 0)),
        out_shape=jax.ShapeDtypeStruct((TP, D), F32),
        scratch_shapes=[pltpu.VMEM((HALO + tm, D), F32)],
        compiler_params=pltpu.CompilerParams(dimension_semantics=("arbitrary",)),
    )(proj, conv_w, conv_b)


def _gates(p_ref, lbl_ref, chunk, bsc):
    lb = _sigmoid(lbl_ref[0:1, :] - lbl_ref[1:2, :])
    q_raw = p_ref[:, 0:D].astype(F32)
    f_raw = p_ref[:, D:2 * D].astype(F32)
    sq = _sigmoid(q_raw)
    q = q_raw * sq
    sg = _sigmoid(f_raw)
    f = lb + (1.0 - lb) * sg
    row = lax.broadcasted_iota(jnp.int32, (CHUNK, 1), 0) + chunk * CHUNK
    valid = row >= PAD_FRONT
    lf = jnp.where(valid, jnp.log(f), 0.0)
    k = jnp.where(valid, 1.0 - f, 0.0)
    r_i = lax.broadcasted_iota(jnp.int32, (CHUNK, CHUNK), 0)
    c_i = lax.broadcasted_iota(jnp.int32, (CHUNK, CHUNK), 1)
    causal = r_i >= c_i
    bsc[...] = _tri_matmul(causal.astype(BF16), lf)
    b = bsc[...]
    b_mid = bsc[CHUNK // 2 - 1:CHUNK // 2, :]
    b_last = bsc[CHUNK - 1:CHUNK, :]
    e_q = jnp.exp(b)
    e_qm = jnp.exp(b - b_mid)
    e_km = jnp.exp(b_mid - b)
    e_kh = jnp.exp(b_last - b)
    e_last = jnp.exp(b_last)
    return dict(lb=lb, q_raw=q_raw, sq=sq, q=q, sg=sg, f=f, k=k, valid=valid, causal=causal,
                e_q=e_q, e_qm=e_qm, e_km=e_km, e_kh=e_kh, e_last=e_last)


def _rec_fwd(proj, lb_logits):
    def body(p_ref, lbl_ref, o_ref, s_ref, st, bsc):
        n = pl.program_id(0)

        @pl.when(n == 0)
        def _():
            st[...] = jnp.zeros_like(st)

        s_ref[0] = st[...]
        g = _gates(p_ref, lbl_ref, n, bsc)
        q1 = (g["q"] * g["e_q"]).astype(BF16)
        qm = (g["q"] * g["e_qm"]).astype(BF16)
        km = (g["k"] * g["e_km"]).astype(BF16)
        kh = (g["k"] * g["e_kh"]).astype(BF16)
        for h in range(HEADS):
            sl = slice(h * HEAD_W, (h + 1) * HEAD_W)
            v = p_ref[:, 2 * D + h * HEAD_W:2 * D + (h + 1) * HEAD_W]
            att = jnp.where(g["causal"], _dot_nt(qm[:, sl], km[:, sl]), 0.0).astype(BF16)
            s_h = st[h]
            o_ref[:, sl] = _dot_nt(q1[:, sl], s_h.astype(BF16)) + _dot(att, v)
            st[h] = s_h * g["e_last"][:, sl] + _dot_tn(v, kh[:, sl])

    return pl.pallas_call(
        body, name="rec_fwd", grid=(N_CHUNK,),
        in_specs=[pl.BlockSpec((CHUNK, 3 * D), lambda n: (n, 1)), pl.BlockSpec((2, D), lambda n: (0, 0))],
        out_specs=(pl.BlockSpec((CHUNK, D), lambda n: (n, 0)),
                   pl.BlockSpec((1, HEADS, HEAD_W, HEAD_W), lambda n: (n, 0, 0, 0))),
        out_shape=(jax.ShapeDtypeStruct((TP, D), F32), jax.ShapeDtypeStruct((N_CHUNK, HEADS, HEAD_W, HEAD_W), F32)),
        scratch_shapes=[pltpu.VMEM((HEADS, HEAD_W, HEAD_W), F32), pltpu.VMEM((CHUNK, D), F32)],
        compiler_params=pltpu.CompilerParams(dimension_semantics=("arbitrary",)),
    )(proj, lb_logits)


def _rec_bwd(proj, lb_logits, d_o, s_start, dproj):
    last = N_CHUNK - 1

    def body(p_ref, lbl_ref, do_ref, s_ref, dproj_in, dp_ref, dlb_ref, dst, bsc, dq_sc, dk_sc, g_sc):
        del dproj_in
        n = pl.program_id(0)

        @pl.when(n == 0)
        def _():
            dst[...] = jnp.zeros_like(dst)
            dlb_ref[...] = jnp.zeros_like(dlb_ref)

        g = _gates(p_ref, lbl_ref, last - n, bsc)
        q1 = (g["q"] * g["e_q"]).astype(BF16)
        qm_f = g["q"] * g["e_qm"]
        km_f = g["k"] * g["e_km"]
        qm = qm_f.astype(BF16)
        km = km_f.astype(BF16)
        qm_lo = (qm_f - qm.astype(F32)).astype(BF16)
        km_lo = (km_f - km.astype(F32)).astype(BF16)
        kh_f = g["k"] * g["e_kh"]
        kh = kh_f.astype(BF16)
        for h in range(HEADS):
            sl = slice(h * HEAD_W, (h + 1) * HEAD_W)
            v = p_ref[:, 2 * D + h * HEAD_W:2 * D + (h + 1) * HEAD_W]
            d_oh = do_ref[:, sl].astype(BF16)
            s0 = s_ref[0, h]
            ds_end = dst[h]
            ds_end_b = ds_end.astype(BF16)
            att = jnp.where(g["causal"], _dot_nt(qm[:, sl], km[:, sl]), 0.0).astype(BF16)
            d_att = jnp.where(g["causal"], _dot_nt(d_oh, v), 0.0).astype(BF16)
            d_v = _dot_tn(att, d_oh) + _dot_nt(kh[:, sl], ds_end_b)
            d_qm = _dot(d_att, km[:, sl]) + _dot(d_att, km_lo[:, sl])
            d_q1 = _dot(d_oh, s0.astype(BF16))
            d_km = _dot_tn(d_att, qm[:, sl]) + _dot_tn(d_att, qm_lo[:, sl])
            d_kh = _dot(v, ds_end_b)
            dq_sc[:, sl] = d_qm * g["e_qm"][:, sl] + d_q1 * g["e_q"][:, sl]
            dk_sc[:, sl] = d_km * g["e_km"][:, sl] + d_kh * g["e_kh"][:, sl]
            g_sc[:, sl] = (jnp.sum(kh_f[:, sl] * d_kh, axis=0, keepdims=True)
                           + g["e_last"][:, sl] * jnp.sum(ds_end * s0, axis=0, keepdims=True))
            dst[h] = ds_end * g["e_last"][:, sl] + _dot_tn(d_oh, q1[:, sl])
            dp_ref[:, 2 * D + h * HEAD_W:2 * D + (h + 1) * HEAD_W] = d_v.astype(BF16)
        d_q = dq_sc[...]
        d_k = dk_sc[...]
        d_b = g["q"] * d_q - g["k"] * d_k
        anti = jnp.logical_not(g["causal"]) | (lax.broadcasted_iota(jnp.int32, (CHUNK, CHUNK), 0)
                                                == lax.broadcasted_iota(jnp.int32, (CHUNK, CHUNK), 1))
        d_lf = _tri_matmul(anti.astype(BF16), d_b) + g_sc[...]
        d_f = jnp.where(g["valid"], d_lf / g["f"] - d_k, 0.0)
        sg = g["sg"]
        dlb_ref[0:1, :] += jnp.sum(d_f * (1.0 - sg), axis=0, keepdims=True)
        dp_ref[:, 0:D] = (d_q * _dsilu(g["q_raw"], g["sq"])).astype(BF16)
        dp_ref[:, D:2 * D] = (d_f * (1.0 - g["lb"]) * sg * (1.0 - sg)).astype(BF16)

    return pl.pallas_call(
        body, name="rec_bwd", grid=(N_CHUNK,),
        in_specs=[pl.BlockSpec((CHUNK, 3 * D), lambda n: (last - n, 1)), pl.BlockSpec((2, D), lambda n: (0, 0)),
                  pl.BlockSpec((CHUNK, D), lambda n: (last - n, 0)),
                  pl.BlockSpec((1, HEADS, HEAD_W, HEAD_W), lambda n: (last - n, 0, 0, 0)), ANY],
        out_specs=(pl.BlockSpec((CHUNK, 3 * D), lambda n: (last - n, 1)), pl.BlockSpec((8, D), lambda n: (0, 0))),
        out_shape=(jax.ShapeDtypeStruct((TP, D_IN), BF16), jax.ShapeDtypeStruct((8, D), F32)),
        scratch_shapes=[pltpu.VMEM((HEADS, HEAD_W, HEAD_W), F32), pltpu.VMEM((CHUNK, D), F32),
                        pltpu.VMEM((CHUNK, D), F32), pltpu.VMEM((CHUNK, D), F32), pltpu.VMEM((1, D), F32)],
        input_output_aliases={4: 0},
        compiler_params=pltpu.CompilerParams(dimension_semantics=("arbitrary",)),
    )(proj, lb_logits, d_o, s_start, dproj)


def _mid(xin, tgt, o, c0, proj, w3, ln_g, ln_b, gnorm_g, final_g):
    tm = TM_ELT

    def body(x_ref, t_ref, o_ref, c0_ref, z_ref, gr_ref, mc_ref, mr_ref, w_ref, lng_ref, lnb_ref, gng_ref, fg_ref,
             dres_ref, do_ref, dc0_ref, dz_ref, dp_ref, a3_ref, b3_ref, red_ref, on_sc, don_sc):
        i = pl.program_id(0)

        @pl.when(i == 0)
        def _():
            red_ref[...] = jnp.zeros_like(red_ref)

        w_conv, w_rec, w_out = w_ref[0], w_ref[1], w_ref[2]
        c0v = c0_ref[...]
        mu = jnp.mean(c0v, axis=-1, keepdims=True)
        xc = c0v - mu
        rstd = lax.rsqrt(jnp.mean(xc * xc, axis=-1, keepdims=True) + EPS)
        xh = xc * rstd
        c1 = xh * lng_ref[...] + lnb_ref[...]
        s1 = _sigmoid(c1)
        c2 = c1 * s1
        z = z_ref[...].astype(F32)
        sz = _sigmoid(z)
        silu_z = z * sz
        u_conv = (c2 * silu_z).astype(BF16)
        y_conv = _dot(u_conv, w_conv)
        ov = o_ref[...]
        r3 = []
        for h in range(HEADS):
            sl = slice(h * HEAD_W, (h + 1) * HEAD_W)
            oh = ov[:, sl]
            r_h = lax.rsqrt(jnp.mean(oh * oh, axis=-1, keepdims=True) + EPS)
            r3.append(r_h)
            on_sc[:, sl] = oh * r_h
        o_n = on_sc[...]
        o_g = o_n * gng_ref[...]
        gr = gr_ref[...].astype(F32)
        sgr = _sigmoid(gr)
        silu_g = gr * sgr
        u_rec = (o_g * silu_g).astype(BF16)
        y_rec = _dot(u_rec, w_rec)
        mc = mc_ref[...].astype(F32)
        mr = mr_ref[...].astype(F32)
        smc = _sigmoid(mc)
        smr = _sigmoid(mr)
        merged = (smc * y_conv + smr * y_rec).astype(BF16)
        res = x_ref[...] + _dot(merged, w_out)
        r2 = lax.rsqrt(jnp.mean(res * res, axis=-1, keepdims=True) + EPS)
        xh2 = res * r2
        row = lax.broadcasted_iota(jnp.int32, (tm, 1), 0) + i * tm
        real = row >= ROW0
        diff = jnp.where(real, xh2 * fg_ref[...] - t_ref[...], 0.0)
        d_y = diff * (1.0 / D)
        d_xh2 = d_y * fg_ref[...]
        d_res = r2 * (d_xh2 - xh2 * jnp.mean(d_xh2 * xh2, axis=-1, keepdims=True))
        dres_ref[...] = d_res
        d_res_b = d_res.astype(BF16)
        d_merged = _dot_nt(d_res_b, w_out)
        d_yc = (d_merged * smc).astype(BF16)
        d_yr = (d_merged * smr).astype(BF16)
        dp_ref[:, D:2 * D] = (d_merged * y_conv * smc * (1.0 - smc)).astype(BF16)
        dp_ref[:, 2 * D:3 * D] = (d_merged * y_rec * smr * (1.0 - smr)).astype(BF16)
        d_ur = _dot_nt(d_yr, w_rec)
        d_og = d_ur * silu_g
        dp_ref[:, 0:D] = (d_ur * o_g * _dsilu(gr, sgr)).astype(BF16)
        d_on = d_og * gng_ref[...]
        for h in range(HEADS):
            sl = slice(h * HEAD_W, (h + 1) * HEAD_W)
            d_h = d_on[:, sl]
            n_h = o_n[:, sl]
            don_sc[:, sl] = r3[h] * (d_h - n_h * jnp.mean(d_h * n_h, axis=-1, keepdims=True))
        do_ref[...] = don_sc[...]
        d_uc = _dot_nt(d_yc, w_conv)
        d_c2 = d_uc * silu_z
        dz_ref[...] = (d_uc * c2 * _dsilu(z, sz)).astype(BF16)
        d_c1 = d_c2 * _dsilu(c1, s1)
        d_xh = d_c1 * lng_ref[...]
        d_c0 = rstd * (d_xh - jnp.mean(d_xh, axis=-1, keepdims=True)
                       - xh * jnp.mean(d_xh * xh, axis=-1, keepdims=True))
        dc0_ref[...] = d_c0
        a3_ref[0] = u_conv
        b3_ref[0] = d_yc
        a3_ref[1] = u_rec
        b3_ref[1] = d_yr
        a3_ref[2] = merged
        b3_ref[2] = d_res_b
        def colsum(vv):
            return jnp.sum(vv, axis=0, keepdims=True)

        red_ref[0:1, :] += colsum(d_y * xh2)
        red_ref[1:2, :] += colsum(d_og * o_n)
        red_ref[2:3, :] += colsum(d_c1 * xh)
        red_ref[3:4, :] += colsum(d_c1)
        red_ref[4:5, :] += colsum(d_c0)
        red_ref[5:6, :] += colsum(diff * diff) * (0.5 / D)

    def row_block(width, col):
        return pl.BlockSpec((tm, width), lambda i: (i, col))

    def const_block(shape):
        return pl.BlockSpec(shape, lambda i: (0,) * len(shape))

    stack = jax.ShapeDtypeStruct((3, TP, D), BF16)
    stack_spec = pl.BlockSpec((3, tm, D), lambda i: (0, i, 0))
    return pl.pallas_call(
        body, name="mid", grid=(TP // tm,),
        in_specs=[row_block(D, 0), row_block(D, 0), row_block(D, 0), row_block(D, 0),
                  row_block(D, 2), row_block(D, 6), row_block(D, 7), row_block(D, 8),
                  pl.BlockSpec((3, D, D), lambda i: (0, 0, 0), pipeline_mode=pl.Buffered(1)),
                  const_block((1, D)), const_block((1, D)), const_block((1, D)), const_block((1, D))],
        out_specs=(row_block(D, 0), row_block(D, 0), row_block(D, 0), row_block(D, 0), row_block(3 * D, 2),
                   stack_spec, stack_spec, const_block((8, D))),
        out_shape=(jax.ShapeDtypeStruct((TP, D), F32), jax.ShapeDtypeStruct((TP, D), F32),
                   jax.ShapeDtypeStruct((TP, D), F32), jax.ShapeDtypeStruct((TP, D), BF16),
                   jax.ShapeDtypeStruct((TP, D_IN), BF16), stack, stack, jax.ShapeDtypeStruct((8, D), F32)),
        scratch_shapes=[pltpu.VMEM((tm, D), F32), pltpu.VMEM((tm, D), F32)],
        compiler_params=pltpu.CompilerParams(dimension_semantics=("arbitrary",), vmem_limit_bytes=60 * 1024 * 1024),
    )(xin, tgt, o, c0, proj, proj, proj, proj, w3, ln_g, ln_b, gnorm_g, final_g)


def _conv_bwd(proj, d_c0, d_z, conv_w, dproj):
    tm = TM_ELT
    n_tile = TP // tm
    lastt = n_tile - 1

    def body(p_ref, dc_ref, dz_ref, w_ref, dproj_in, dp_ref, dw_ref, dbuf, acc):
        del dproj_in
        i = pl.program_id(0)

        @pl.when(i == 0)
        def _():
            dbuf[tm:tm + HALO, :] = jnp.zeros((HALO, D), F32)
            acc[...] = jnp.zeros_like(acc)

        @pl.when(i > 0)
        def _():
            dbuf[tm:tm + HALO, :] = dbuf[0:HALO, :]

        dbuf[0:tm, :] = dc_ref[...]
        ga = p_ref[:, 0:D].astype(F32)
        gb = p_ref[:, D:2 * D].astype(F32)
        sb = _sigmoid(gb)
        a = ga * sb
        d_a = jnp.zeros((tm, D), F32)
        for j in range(CONV_K):
            off = CONV_K - 1 - j
            shifted = dbuf[off:off + tm, :]
            d_a = d_a + w_ref[j:j + 1, :] * shifted
            acc[j] += jnp.sum((a * shifted).reshape(tm // 8, 8, D), axis=0)
        dp_ref[:, 0:D] = (d_a * sb).astype(BF16)
        dp_ref[:, D:2 * D] = (d_a * ga * sb * (1.0 - sb)).astype(BF16)
        dp_ref[:, 2 * D:3 * D] = dz_ref[...]

        @pl.when(i == lastt)
        def _():
            for j in range(CONV_K):
                dw_ref[j:j + 1, :] = jnp.sum(acc[j], axis=0, keepdims=True)
            dw_ref[CONV_K:CONV_K + 1, :] = jnp.zeros((1, D), F32)

    return pl.pallas_call(
        body, name="conv_bwd", grid=(n_tile,),
        in_specs=[pl.BlockSpec((tm, 2 * D), lambda i: (lastt - i, 0)), pl.BlockSpec((tm, D), lambda i: (lastt - i, 0)),
                  pl.BlockSpec((tm, D), lambda i: (lastt - i, 0)), pl.BlockSpec((CONV_K, D), lambda i: (0, 0)), ANY],
        out_specs=(pl.BlockSpec((tm, 3 * D), lambda i: (lastt - i, 0)), pl.BlockSpec((CONV_K + 1, D), lambda i: (0, 0))),
        out_shape=(jax.ShapeDtypeStruct((TP, D_IN), BF16), jax.ShapeDtypeStruct((CONV_K + 1, D), F32)),
        scratch_shapes=[pltpu.VMEM((tm + HALO, D), F32), pltpu.VMEM((CONV_K, 8, D), F32)],
        input_output_aliases={4: 0},
        compiler_params=pltpu.CompilerParams(dimension_semantics=("arbitrary",)),
    )(proj, d_c0, d_z, conv_w, dproj)


def _wgrad3(a3, b3):
    tt = TM_MAT

    def body(a_ref, b_ref, o_ref):
        @pl.when(pl.program_id(1) == 0)
        def _():
            o_ref[...] = jnp.zeros_like(o_ref)

        o_ref[0] += _dot_tn(a_ref[0], b_ref[0])

    return pl.pallas_call(
        body, name="wgrad3", grid=(3, TP // tt),
        in_specs=[pl.BlockSpec((1, tt, D), lambda g, t: (g, t, 0)), pl.BlockSpec((1, tt, D), lambda g, t: (g, t, 0))],
        out_specs=pl.BlockSpec((1, D, D), lambda g, t: (g, 0, 0)),
        out_shape=jax.ShapeDtypeStruct((3, D, D), F32),
        compiler_params=pltpu.CompilerParams(dimension_semantics=("arbitrary", "arbitrary")),
    )(a3, b3)


def _wgrad_in(h, dproj):
    tt = TM_MAT

    def body(a_ref, b_ref, o_ref):
        @pl.when(pl.program_id(1) == 0)
        def _():
            o_ref[...] = jnp.zeros_like(o_ref)

        o_ref[0] += _dot_tn(a_ref[...], b_ref[...])

    return pl.pallas_call(
        body, name="wgrad_in", grid=(N_DEV, TP // tt),
        in_specs=[pl.BlockSpec((tt, D), lambda n, t: (t, 0)), pl.BlockSpec((tt, W_IN_BLK), lambda n, t: (t, n))],
        out_specs=pl.BlockSpec((1, D, W_IN_BLK), lambda n, t: (n, 0, 0)),
        out_shape=jax.ShapeDtypeStruct((N_DEV, D, W_IN_BLK), F32),
        compiler_params=pltpu.CompilerParams(dimension_semantics=("arbitrary", "arbitrary")),
    )(h, dproj)


def _dh_and_norm_bwd(dproj, w_in_full, xin, d_res, norm_g):
    tm = TM_MAT
    n_k = N_DEV

    def body(dp_ref, w_ref, x_ref, dr_ref, g_ref, dx_ref, dg_ref, acc):
        m = pl.program_id(0)
        k = pl.program_id(1)

        @pl.when(k == 0)
        def _():
            acc[...] = jnp.zeros_like(acc)

        acc[...] += _dot_nt(dp_ref[...], w_ref[0])

        @pl.when((k == n_k - 1) & (m == 0))
        def _():
            dg_ref[...] = jnp.zeros_like(dg_ref)

        @pl.when(k == n_k - 1)
        def _():
            xv = x_ref[...]
            r1 = lax.rsqrt(jnp.mean(xv * xv, axis=-1, keepdims=True) + EPS)
            xh = xv * r1
            d_h = acc[...]
            dg_ref[0:1, :] += jnp.sum(d_h * xh, axis=0, keepdims=True)
            d_xh = d_h * g_ref[...]
            dx_ref[...] = dr_ref[...] + r1 * (d_xh - xh * jnp.mean(d_xh * xh, axis=-1, keepdims=True))

    return pl.pallas_call(
        body, name="dh_norm_bwd", grid=(TP // tm, n_k),
        in_specs=[pl.BlockSpec((tm, W_IN_BLK), lambda m, k: (m, k)), pl.BlockSpec((1, D, W_IN_BLK), lambda m, k: (k, 0, 0)),
                  pl.BlockSpec((tm, D), lambda m, k: (m, 0)), pl.BlockSpec((tm, D), lambda m, k: (m, 0)),
                  pl.BlockSpec((1, D), lambda m, k: (0, 0))],
        out_specs=(pl.BlockSpec((tm, D), lambda m, k: (m, 0)), pl.BlockSpec((8, D), lambda m, k: (0, 0))),
        out_shape=(jax.ShapeDtypeStruct((TP, D), F32), jax.ShapeDtypeStruct((8, D), F32)),
        scratch_shapes=[pltpu.VMEM((tm, D), F32)],
        compiler_params=pltpu.CompilerParams(dimension_semantics=("arbitrary", "arbitrary")),
    )(dproj, w_in_full, xin, d_res, norm_g)


def _sum_adamw(own, landed, w, m, v, tr, name):
    rows, cols = w.shape
    n_t = rows // tr

    def body(o_ref, l1_ref, l2_ref, l3_ref, w_ref, m_ref, v_ref, g_ref, d_ref, m2_ref, v2_ref):
        g = ((o_ref[...] + l1_ref[...]) + l2_ref[...]) + l3_ref[...]
        delta, m2, v2 = _adamw(w_ref[...], g, m_ref[...], v_ref[...])
        g_ref[...] = g
        d_ref[...] = delta
        m2_ref[...] = m2
        v2_ref[...] = v2

    def spec(k):
        return pl.BlockSpec((tr, cols), lambda i: (i + k * n_t, 0))

    out = jax.ShapeDtypeStruct((rows, cols), F32)
    return pl.pallas_call(
        body, name=name, grid=(n_t,),
        in_specs=[spec(0), spec(0), spec(1), spec(2), spec(0), spec(0), spec(0)],
        out_specs=(spec(0),) * 4, out_shape=(out,) * 4,
    )(own, landed, landed, landed, w, m, v)


def _small_update(pack_all, srs_all, lb_logits, p8, m8, v8, ws, ms, vs):
    def body(pk_ref, sr_ref, lbl_ref, p_ref, m_ref, v_ref, ws_ref, ms_ref, vs_ref,
             g8_ref, d8_ref, m8_ref, v8_ref, loss_ref, gs_ref, ds_ref, ms2_ref, vs2_ref):
        tot = pk_ref[0]
        tot_s = sr_ref[0]
        for d in range(1, N_DEV):
            tot = tot + pk_ref[d]
            tot_s = tot_s + sr_ref[d]
        p0 = _sigmoid(lbl_ref[0:1, :] - lbl_ref[1:2, :])
        row = lax.broadcasted_iota(jnp.int32, (8, D), 0)
        d_lb = jnp.sum(jnp.where(row == 4, tot, 0.0), axis=0, keepdims=True)
        d_l0 = d_lb * p0 * (1.0 - p0)
        loss_ref[...] = jnp.sum(jnp.where(row == 5, tot, 0.0), keepdims=True).reshape(1, 1)
        g8 = jnp.where(row == 4, d_l0, jnp.where(row == 5, -d_l0, tot))
        delta, m2, v2 = _adamw(p_ref[...], g8, m_ref[...], v_ref[...])
        g8_ref[...] = g8
        d8_ref[...] = delta
        m8_ref[...] = m2
        v8_ref[...] = v2
        delta, m2, v2 = _adamw(ws_ref[...], tot_s, ms_ref[...], vs_ref[...])
        gs_ref[...] = tot_s
        ds_ref[...] = delta
        ms2_ref[...] = m2
        vs2_ref[...] = v2

    o8 = jax.ShapeDtypeStruct((8, D), F32)
    os_ = jax.ShapeDtypeStruct((SMALL_ROWS, HEAD_W), F32)
    return pl.pallas_call(
        body, name="small_update",
        out_shape=(o8, o8, o8, o8, jax.ShapeDtypeStruct((1, 1), F32), os_, os_, os_, os_),
    )(pack_all, srs_all, lb_logits, p8, m8, v8, ws, ms, vs)


def _local_step(x_seq, target, meta_full, norm_g, w_in_full, conv_w_full, conv_b, ln_g, ln_b, w3_full, lb_logits,
                gnorm_g, final_g):
    xin = jnp.concatenate([jnp.zeros((PAD_FRONT, D), F32), meta_full, x_seq], axis=0)
    tgt = jnp.concatenate([jnp.zeros((ROW0, D), F32), target], axis=0)
    fg = final_g.reshape(1, D)
    h = _rms_h(xin, norm_g)
    proj = _proj(h, w_in_full)
    c0 = _conv_fwd(proj, conv_w_full, conv_b)
    o, s_start = _rec_fwd(proj, lb_logits)
    d_res, d_o, d_c0, d_z, dproj, a3, b3, red = _mid(xin, tgt, o, c0, proj, w3_full, ln_g, ln_b, gnorm_g, fg)
    dproj, dlb = _rec_bwd(proj, lb_logits, d_o, s_start, dproj)
    dproj, d_conv_w = _conv_bwd(proj, d_c0, d_z, conv_w_full, dproj)
    p3 = _wgrad3(a3, b3)
    p_in = _wgrad_in(h, dproj)
    d_xin, dng = _dh_and_norm_bwd(dproj, w_in_full, xin, d_res, norm_g)
    pack = jnp.concatenate([dng[0:1], red[4:5], red[2:3], red[3:4], dlb[0:1], red[5:6], red[1:2], red[0:1]], axis=0)
    return d_xin, p_in, p3, d_conv_w, pack


def kernel(x, meta_tokens, norm_g, w_in, conv_w, conv_b, ln_g, ln_b, w_conv_out, lb_logits, gnorm_g, w_rec_out, w_out, final_g, loss_target, m_meta_tokens, m_norm_g, m_w_in, m_conv_w, m_conv_b, m_ln_g, m_ln_b, m_w_conv_out, m_lb_logits, m_gnorm_g, m_w_rec_out, m_w_out, m_final_g, v_meta_tokens, v_norm_g, v_w_in, v_conv_w, v_conv_b, v_ln_g, v_ln_b, v_w_conv_out, v_lb_logits, v_gnorm_g, v_w_rec_out, v_w_out, v_final_g):
    def small_pack(cw, mt):
        return jnp.concatenate([cw[0], jnp.zeros((1, HEAD_W), F32), mt], axis=0)

    def stack3(a, b, c):
        return jnp.concatenate([a, b, c], axis=0)

    def stack8(ng, cb, lg, lb_, lbl, gg, fg):
        return jnp.concatenate([ng, cb, lg, lb_, lbl, gg, fg.reshape(1, D)], axis=0)

    w3_s = stack3(w_conv_out, w_rec_out, w_out)
    ws_s = small_pack(conv_w, meta_tokens)
    w_in_b, w3_b = _cast_shards(w_in[0], w3_s)
    w_in_full, w3_full, small_full = _all_gather_weights(w_in_b, w3_b, ws_s)
    small_full = jnp.transpose(small_full, (1, 0, 2)).reshape(SMALL_ROWS, D)
    conv_w_full = small_full[0:CONV_K]
    meta_full = small_full[META_ROW:META_ROW + N_META]

    d_xin, p_in, p3, d_conv_w, pack = _local_step(
        x[0], loss_target[0], meta_full, norm_g, w_in_full, conv_w_full, conv_b, ln_g, ln_b, w3_full, lb_logits,
        gnorm_g, final_g)

    land0, land1 = _reduce_to_sibling(p_in, p3)
    mx, my, mc = _my_place()
    ids = jnp.stack([_dev_index(*_chip_rel(mx, my, r), mc) for r in range(4)]).astype(jnp.int32)
    chip0, chip1 = _chip_sums(p_in, p3, land0, land1, ids)
    far0, far1 = _reduce_to_chips(chip0, chip1)
    g_in, d_in, m_in, v_in = _sum_adamw(chip0.reshape(4 * D, W_IN_BLK), far0.reshape(3 * D, W_IN_BLK), w_in[0],
                                        m_w_in[0], v_w_in[0], 256, "adamw_in")
    g_3, d_3, m_3, v_3 = _sum_adamw(
        chip1.reshape(12 * W_ROW_BLK, D), far1.reshape(9 * W_ROW_BLK, D), w3_s.reshape(3 * W_ROW_BLK, D),
        stack3(m_w_conv_out, m_w_rec_out, m_w_out).reshape(3 * W_ROW_BLK, D),
        stack3(v_w_conv_out, v_w_rec_out, v_w_out).reshape(3 * W_ROW_BLK, D), 3 * W_ROW_BLK, "adamw_3")

    srs = jnp.concatenate([d_conv_w, d_xin[PAD_FRONT:ROW0]], axis=0)
    srs = jnp.transpose(srs.reshape(SMALL_ROWS, N_DEV, HEAD_W), (1, 0, 2))
    pack_all, srs_all = _exchange_small(pack, srs)
    g8, d8, m8, v8, loss, gs, ds, ms, vs = _small_update(
        pack_all, srs_all, lb_logits,
        stack8(norm_g, conv_b, ln_g, ln_b, lb_logits, gnorm_g, final_g),
        stack8(m_norm_g, m_conv_b, m_ln_g, m_ln_b, m_lb_logits, m_gnorm_g, m_final_g),
        stack8(v_norm_g, v_conv_b, v_ln_g, v_ln_b, v_lb_logits, v_gnorm_g, v_final_g),
        ws_s, small_pack(m_conv_w, m_meta_tokens), small_pack(v_conv_w, v_meta_tokens))

    def unpack(a_in, a_3, a_s, a_8):
        t3 = a_3.reshape(3, 1, W_ROW_BLK, D)
        return (a_s[META_ROW:META_ROW + N_META], a_8[0:1], a_in[None], a_s[0:CONV_K][None], a_8[1:2], a_8[2:3],
                a_8[3:4], t3[0], a_8[4:6], a_8[6:7], t3[1], t3[2], a_8[7])

    grad_x = d_xin[ROW0:][None]
    return (loss.reshape(()), grad_x, *unpack(g_in, g_3, gs, g8), *unpack(d_in, d_3, ds, d8),
            *unpack(m_in, m_3, ms, m8), *unpack(v_in, v_3, vs, v8))
```

```python
import functools

import jax
import jax.numpy as jnp
from jax import lax
from jax.experimental import pallas as pl
from jax.experimental.pallas import tpu as pltpu

F32 = jnp.float32
BF16 = jnp.bfloat16

D = 1024
SEQ = 4096
N_META = 16
CHUNK = 64
PAD_FRONT = 48
ROW0 = PAD_FRONT + N_META
TP = ROW0 + SEQ
N_CHUNK = TP // CHUNK
HEADS = 8
HEAD_W = 128
D_IN = 9 * D
N_DEV = 8
W_IN_BLK = D_IN // N_DEV
W_ROW_BLK = D // N_DEV
CONV_K = 31
SMALL_ROWS = 48
META_ROW = 32
EPS = 1e-6
HALO = 32

TM_MAT = 832
TM_ELT = 208
CONV_STRIPS = 2

ADAM_LR = 0.001
ADAM_B1 = 0.9
ADAM_B2 = 0.999
ADAM_EPS = 1e-08
ADAM_WD = 0.01
ADAM_STEP = 10

MESH_ID = pl.DeviceIdType.MESH
ANY = pl.BlockSpec(memory_space=pl.ANY)


def _sigmoid(v):
    return jax.nn.sigmoid(v)


def _dsilu(v, s):
    return s * (1.0 + v * (1.0 - s))


def _dot(a, b):
    return jnp.dot(a, b, preferred_element_type=F32)


def _dot_nt(a, b):
    return lax.dot_general(a, b, (((1,), (1,)), ((), ())), preferred_element_type=F32)


def _dot_tn(a, b):
    return lax.dot_general(a, b, (((0,), (0,)), ((), ())), preferred_element_type=F32)


def _split3(v):
    hi = v.astype(BF16)
    r1 = v - hi.astype(F32)
    mid = r1.astype(BF16)
    lo = (r1 - mid.astype(F32)).astype(BF16)
    return hi, mid, lo


def _tri_matmul(tri, v):
    hi, mid, lo = _split3(v)
    return _dot(tri, hi) + _dot(tri, mid) + _dot(tri, lo)


def _adamw(w, g, m, v):
    m2 = ADAM_B1 * m + (1.0 - ADAM_B1) * g
    v2 = ADAM_B2 * v + (1.0 - ADAM_B2) * jnp.square(g)
    m_hat = m2 / (1.0 - ADAM_B1 ** ADAM_STEP)
    v_hat = v2 / (1.0 - ADAM_B2 ** ADAM_STEP)
    delta = -ADAM_LR * (m_hat / (jnp.sqrt(v_hat) + ADAM_EPS) + ADAM_WD * w)
    return delta, m2, v2


def _my_place():
    return lax.axis_index("x"), lax.axis_index("y"), lax.axis_index("c")


def _dev_index(px, py, pc):
    return 4 * px + 2 * py + pc


def _cast_shards(w_in_s, w3_s):
    def body(a_ref, b_ref, oa_ref, ob_ref):
        oa_ref[...] = a_ref[...].astype(BF16)
        ob_ref[...] = b_ref[...].astype(BF16)

    return pl.pallas_call(
        body, name="cast_shards",
        out_shape=(jax.ShapeDtypeStruct(w_in_s.shape, BF16), jax.ShapeDtypeStruct(w3_s.shape, BF16)),
    )(w_in_s, w3_s)


def _all_gather_weights(w_in_b, w3_b, small_s):
    n_arr = 3

    def body(s0, s1, s2, o0, o1, o2, send_sems, recv_sems, local_sems):
        x, y, c = _my_place()
        me, sibling = (x, y, c), (x, y, 1 - c)
        chips = [(1 - x, y), (x, 1 - y), (1 - x, 1 - y)]
        srcs, outs = (s0, s1, s2), (o0, o1, o2)

        def block(a, place):
            d = _dev_index(*place)
            if a == 1:
                return outs[1].at[:, pl.ds(pl.multiple_of(d * W_ROW_BLK, W_ROW_BLK), W_ROW_BLK), :]
            return outs[a].at[d]

        def copy(a, k, place, to, from_src=False):
            return pltpu.make_async_remote_copy(
                src_ref=srcs[a] if from_src else block(a, place), dst_ref=block(a, place),
                send_sem=send_sems.at[a * 7 + k], recv_sem=recv_sems.at[a * 7 + k],
                device_id=to, device_id_type=MESH_ID)

        mine = [pltpu.make_async_copy(srcs[a], block(a, me), local_sems.at[a]) for a in range(n_arr)]
        for cp in mine:
            cp.start()
        started = []
        for a in range(n_arr):
            cp = copy(a, 0, me, sibling, from_src=True)
            cp.start()
            started.append(cp)
            for j, chip in enumerate(chips):
                cp = copy(a, 1 + j, me, (*chip, c), from_src=True)
                cp.start()
                started.append(cp)
        for j, chip in enumerate(chips):
            for a in range(n_arr):
                copy(a, 1 + j, (*chip, c), me).wait_recv()
                cp = copy(a, 4 + j, (*chip, c), sibling)
                cp.start()
                started.append(cp)
        for a in range(n_arr):
            copy(a, 0, sibling, me).wait_recv()
            for j, chip in enumerate(chips):
                copy(a, 4 + j, (*chip, 1 - c), me).wait_recv()
        for cp in started:
            cp.wait_send()
        for cp in mine:
            cp.wait()

    return pl.pallas_call(
        body, name="all_gather_weights",
        out_shape=(jax.ShapeDtypeStruct((N_DEV,) + w_in_b.shape, BF16),
                   jax.ShapeDtypeStruct((3, D, D), BF16),
                   jax.ShapeDtypeStruct((N_DEV,) + small_s.shape, F32)),
        in_specs=[ANY, ANY, ANY], out_specs=(ANY, ANY, ANY),
        scratch_shapes=[pltpu.SemaphoreType.DMA((7 * n_arr,)), pltpu.SemaphoreType.DMA((7 * n_arr,)),
                        pltpu.SemaphoreType.DMA((n_arr,))],
    )(w_in_b, w3_b, small_s)


def _chip_rel(x, y, r):
    return (jnp.bitwise_xor(x, r >> 1), jnp.bitwise_xor(y, r & 1))


def _exchange_small(pack, srs):
    def body(pk, sr, pk_all, sr_all, send_sems, recv_sems, local_sems):
        x, y, c = _my_place()
        my_id = _dev_index(x, y, c)
        mine = [pltpu.make_async_copy(pk, pk_all.at[my_id], local_sems.at[0]),
                pltpu.make_async_copy(sr.at[my_id], sr_all.at[my_id], local_sems.at[1])]
        for cp in mine:
            cp.start()
        copies = []
        for r in range(1, N_DEV):
            peer = (jnp.bitwise_xor(x, (r >> 2) & 1), jnp.bitwise_xor(y, (r >> 1) & 1), jnp.bitwise_xor(c, r & 1))
            peer_id = _dev_index(*peer)
            for a, (src, dst) in enumerate(((pk, pk_all.at[my_id]), (sr.at[peer_id], sr_all.at[my_id]))):
                cp = pltpu.make_async_remote_copy(
                    src_ref=src, dst_ref=dst, send_sem=send_sems.at[a * 7 + r - 1], recv_sem=recv_sems.at[a * 7 + r - 1],
                    device_id=peer, device_id_type=MESH_ID)
                cp.start()
                copies.append(cp)
        for cp in copies:
            cp.wait_recv()
        for cp in copies:
            cp.wait_send()
        for cp in mine:
            cp.wait()

    return pl.pallas_call(
        body, name="exchange_small",
        out_shape=(jax.ShapeDtypeStruct((N_DEV,) + pack.shape, F32), jax.ShapeDtypeStruct(srs.shape, F32)),
        in_specs=[ANY, ANY], out_specs=(ANY, ANY),
        scratch_shapes=[pltpu.SemaphoreType.DMA((14,)), pltpu.SemaphoreType.DMA((14,)), pltpu.SemaphoreType.DMA((2,))],
    )(pack, srs)


def _rms_h(xin, norm_g):
    tm = TM_MAT

    def body(x_ref, g_ref, h_ref):
        xv = x_ref[...]
        r = lax.rsqrt(jnp.mean(xv * xv, axis=-1, keepdims=True) + EPS)
        h_ref[...] = (xv * r * g_ref[...]).astype(BF16)

    return pl.pallas_call(
        body, name="rms_h", grid=(TP // tm,),
        in_specs=[pl.BlockSpec((tm, D), lambda i: (i, 0)), pl.BlockSpec((1, D), lambda i: (0, 0))],
        out_specs=pl.BlockSpec((tm, D), lambda i: (i, 0)),
        out_shape=jax.ShapeDtypeStruct((TP, D), BF16),
    )(xin, norm_g)


def _proj(h, w_in_full):
    tm = TM_MAT

    def body(h_ref, w_ref, o_ref):
        o_ref[...] = _dot(h_ref[...], w_ref[0]).astype(BF16)

    return pl.pallas_call(
        body, name="proj", grid=(N_DEV, TP // tm),
        in_specs=[pl.BlockSpec((tm, D), lambda n, m: (m, 0)), pl.BlockSpec((1, D, W_IN_BLK), lambda n, m: (n, 0, 0))],
        out_specs=pl.BlockSpec((tm, W_IN_BLK), lambda n, m: (m, n)),
        out_shape=jax.ShapeDtypeStruct((TP, D_IN), BF16),
    )(h, w_in_full)


def _fill_shifts(sh, tm):
    n = tm + HALO - 8
    for s in range(1, 8):
        sh[s, 0:n, :] = sh[0, s:s + n, :]


def _conv_fwd(proj, conv_w, conv_b):
    tm = TM_ELT
    strip = tm // CONV_STRIPS

    def body(p_ref, w_ref, b_ref, c0_ref, sh):
        i = pl.program_id(0)

        @pl.when(i == 0)
        def _():
            sh[0, 0:HALO, :] = jnp.zeros((HALO, D), F32)

        @pl.when(i > 0)
        def _():
            sh[0, 0:HALO, :] = sh[0, tm:tm + HALO, :]

        ga = p_ref[:, 0:D].astype(F32)
        gb = p_ref[:, D:2 * D].astype(F32)
        sh[0, HALO:HALO + tm, :] = ga * _sigmoid(gb)
        _fill_shifts(sh, tm)
        for cb in range(D // HEAD_W):
            cs = slice(cb * HEAD_W, (cb + 1) * HEAD_W)
            for st in range(CONV_STRIPS):
                acc = jnp.broadcast_to(b_ref[:, cs], (strip, HEAD_W))
                for j in range(CONV_K):
                    off = HALO - (CONV_K - 1) + j
                    lo = st * strip + 8 * (off // 8)
                    acc = acc + w_ref[j:j + 1, cs] * sh[off % 8, lo:lo + strip, cs]
                c0_ref[st * strip:(st + 1) * strip, cs] = acc

    return pl.pallas_call(
        body, name="conv_fwd", grid=(TP // tm,),
        in_specs=[pl.BlockSpec((tm, 2 * D), lambda i: (i, 0)), pl.BlockSpec((CONV_K, D), lambda i: (0, 0)),
                  pl.BlockSpec((1, D), lambda i: (0, 0))],
        out_specs=pl.BlockSpec((tm, D), lambda i: (i, 0)),
        out_shape=jax.ShapeDtypeStruct((TP, D), F32),
        scratch_shapes=[pltpu.VMEM((8, HALO + tm, D), F32)],
        compiler_params=pltpu.CompilerParams(dimension_semantics=("arbitrary",)),
    )(proj, conv_w, conv_b)


def _gates(p_ref, lbl_ref, chunk, bsc):
    lb = _sigmoid(lbl_ref[0:1, :] - lbl_ref[1:2, :])
    q_raw = p_ref[:, 0:D].astype(F32)
    f_raw = p_ref[:, D:2 * D].astype(F32)
    sq = _sigmoid(q_raw)
    q = q_raw * sq
    sg = _sigmoid(f_raw)
    f = lb + (1.0 - lb) * sg
    row = lax.broadcasted_iota(jnp.int32, (CHUNK, 1), 0) + chunk * CHUNK
    valid = row >= PAD_FRONT
    lf = jnp.where(valid, jnp.log(f), 0.0)
    k = jnp.where(valid, 1.0 - f, 0.0)
    r_i = lax.broadcasted_iota(jnp.int32, (CHUNK, CHUNK), 0)
    c_i = lax.broadcasted_iota(jnp.int32, (CHUNK, CHUNK), 1)
    causal = r_i >= c_i
    bsc[...] = _tri_matmul(causal.astype(BF16), lf)
    b = bsc[...]
    b_mid = bsc[CHUNK // 2 - 1:CHUNK // 2, :]
    b_last = bsc[CHUNK - 1:CHUNK, :]
    e_q = jnp.exp(b)
    e_qm = jnp.exp(b - b_mid)
    e_km = jnp.exp(b_mid - b)
    e_kh = jnp.exp(b_last - b)
    e_last = jnp.exp(b_last)
    return dict(lb=lb, q_raw=q_raw, sq=sq, q=q, sg=sg, f=f, k=k, valid=valid, causal=causal,
                e_q=e_q, e_qm=e_qm, e_km=e_km, e_kh=e_kh, e_last=e_last)


def _rec_fwd(proj, lb_logits):
    def body(p_ref, lbl_ref, o_ref, s_ref, st, bsc):
        n = pl.program_id(0)

        @pl.when(n == 0)
        def _():
            st[...] = jnp.zeros_like(st)

        s_ref[0] = st[...]
        g = _gates(p_ref, lbl_ref, n, bsc)
        q1 = (g["q"] * g["e_q"]).astype(BF16)
        qm = (g["q"] * g["e_qm"]).astype(BF16)
        km = (g["k"] * g["e_km"]).astype(BF16)
        kh = (g["k"] * g["e_kh"]).astype(BF16)
        for h in range(HEADS):
            sl = slice(h * HEAD_W, (h + 1) * HEAD_W)
            v = p_ref[:, 2 * D + h * HEAD_W:2 * D + (h + 1) * HEAD_W]
            att = jnp.where(g["causal"], _dot_nt(qm[:, sl], km[:, sl]), 0.0).astype(BF16)
            s_h = st[h]
            o_ref[:, sl] = _dot_nt(q1[:, sl], s_h.astype(BF16)) + _dot(att, v)
            st[h] = s_h * g["e_last"][:, sl] + _dot_tn(v, kh[:, sl])

    return pl.pallas_call(
        body, name="rec_fwd", grid=(N_CHUNK,),
        in_specs=[pl.BlockSpec((CHUNK, 3 * D), lambda n: (n, 1)), pl.BlockSpec((2, D), lambda n: (0, 0))],
        out_specs=(pl.BlockSpec((CHUNK, D), lambda n: (n, 0)),
                   pl.BlockSpec((1, HEADS, HEAD_W, HEAD_W), lambda n: (n, 0, 0, 0))),
        out_shape=(jax.ShapeDtypeStruct((TP, D), F32), jax.ShapeDtypeStruct((N_CHUNK, HEADS, HEAD_W, HEAD_W), F32)),
        scratch_shapes=[pltpu.VMEM((HEADS, HEAD_W, HEAD_W), F32), pltpu.VMEM((CHUNK, D), F32)],
        compiler_params=pltpu.CompilerParams(dimension_semantics=("arbitrary",)),
    )(proj, lb_logits)


def _rec_bwd(proj, lb_logits, d_o, s_start, dproj):
    last = N_CHUNK - 1

    def body(p_ref, lbl_ref, do_ref, s_ref, dproj_in, dp_ref, dlb_ref, dst, bsc, dq_sc, dk_sc, g_sc):
        del dproj_in
        n = pl.program_id(0)

        @pl.when(n == 0)
        def _():
            dst[...] = jnp.zeros_like(dst)
            dlb_ref[...] = jnp.zeros_like(dlb_ref)

        g = _gates(p_ref, lbl_ref, last - n, bsc)
        q1 = (g["q"] * g["e_q"]).astype(BF16)
        qm_f = g["q"] * g["e_qm"]
        km_f = g["k"] * g["e_km"]
        qm = qm_f.astype(BF16)
        km = km_f.astype(BF16)
        qm_lo = (qm_f - qm.astype(F32)).astype(BF16)
        km_lo = (km_f - km.astype(F32)).astype(BF16)
        kh_f = g["k"] * g["e_kh"]
        kh = kh_f.astype(BF16)
        for h in range(HEADS):
            sl = slice(h * HEAD_W, (h + 1) * HEAD_W)
            v = p_ref[:, 2 * D + h * HEAD_W:2 * D + (h + 1) * HEAD_W]
            d_oh = do_ref[:, sl].astype(BF16)
            s0 = s_ref[0, h]
            ds_end = dst[h]
            ds_end_b = ds_end.astype(BF16)
            att = jnp.where(g["causal"], _dot_nt(qm[:, sl], km[:, sl]), 0.0).astype(BF16)
            d_att = jnp.where(g["causal"], _dot_nt(d_oh, v), 0.0).astype(BF16)
            d_v = _dot_tn(att, d_oh) + _dot_nt(kh[:, sl], ds_end_b)
            d_qm = _dot(d_att, km[:, sl]) + _dot(d_att, km_lo[:, sl])
            d_q1 = _dot(d_oh, s0.astype(BF16))
            d_km = _dot_tn(d_att, qm[:, sl]) + _dot_tn(d_att, qm_lo[:, sl])
            d_kh = _dot(v, ds_end_b)
            dq_sc[:, sl] = d_qm * g["e_qm"][:, sl] + d_q1 * g["e_q"][:, sl]
            dk_sc[:, sl] = d_km * g["e_km"][:, sl] + d_kh * g["e_kh"][:, sl]
            g_sc[:, sl] = (jnp.sum(kh_f[:, sl] * d_kh, axis=0, keepdims=True)
                           + g["e_last"][:, sl] * jnp.sum(ds_end * s0, axis=0, keepdims=True))
            dst[h] = ds_end * g["e_last"][:, sl] + _dot_tn(d_oh, q1[:, sl])
            dp_ref[:, 2 * D + h * HEAD_W:2 * D + (h + 1) * HEAD_W] = d_v.astype(BF16)
        d_q = dq_sc[...]
        d_k = dk_sc[...]
        d_b = g["q"] * d_q - g["k"] * d_k
        anti = jnp.logical_not(g["causal"]) | (lax.broadcasted_iota(jnp.int32, (CHUNK, CHUNK), 0)
                                                == lax.broadcasted_iota(jnp.int32, (CHUNK, CHUNK), 1))
        d_lf = _tri_matmul(anti.astype(BF16), d_b) + g_sc[...]
        d_f = jnp.where(g["valid"], d_lf / g["f"] - d_k, 0.0)
        sg = g["sg"]
        dlb_ref[0:1, :] += jnp.sum(d_f * (1.0 - sg), axis=0, keepdims=True)
        dp_ref[:, 0:D] = (d_q * _dsilu(g["q_raw"], g["sq"])).astype(BF16)
        dp_ref[:, D:2 * D] = (d_f * (1.0 - g["lb"]) * sg * (1.0 - sg)).astype(BF16)

    return pl.pallas_call(
        body, name="rec_bwd", grid=(N_CHUNK,),
        in_specs=[pl.BlockSpec((CHUNK, 3 * D), lambda n: (last - n, 1)), pl.BlockSpec((2, D), lambda n: (0, 0)),
                  pl.BlockSpec((CHUNK, D), lambda n: (last - n, 0)),
                  pl.BlockSpec((1, HEADS, HEAD_W, HEAD_W), lambda n: (last - n, 0, 0, 0)), ANY],
        out_specs=(pl.BlockSpec((CHUNK, 3 * D), lambda n: (last - n, 1)), pl.BlockSpec((8, D), lambda n: (0, 0))),
        out_shape=(jax.ShapeDtypeStruct((TP, D_IN), BF16), jax.ShapeDtypeStruct((8, D), F32)),
        scratch_shapes=[pltpu.VMEM((HEADS, HEAD_W, HEAD_W), F32), pltpu.VMEM((CHUNK, D), F32),
                        pltpu.VMEM((CHUNK, D), F32), pltpu.VMEM((CHUNK, D), F32), pltpu.VMEM((1, D), F32)],
        input_output_aliases={4: 0},
        compiler_params=pltpu.CompilerParams(dimension_semantics=("arbitrary",)),
    )(proj, lb_logits, d_o, s_start, dproj)


def _mid(xin, tgt, o, c0, proj, w3, ln_g, ln_b, gnorm_g, final_g):
    tm = TM_ELT

    def body(x_ref, t_ref, o_ref, c0_ref, z_ref, gr_ref, mc_ref, mr_ref, w_ref, lng_ref, lnb_ref, gng_ref, fg_ref,
             dres_ref, do_ref, dc0_ref, dz_ref, dp_ref, a3_ref, b3_ref, red_ref, on_sc, don_sc):
        i = pl.program_id(0)

        @pl.when(i == 0)
        def _():
            red_ref[...] = jnp.zeros_like(red_ref)

        w_conv, w_rec, w_out = w_ref[0], w_ref[1], w_ref[2]
        c0v = c0_ref[...]
        mu = jnp.mean(c0v, axis=-1, keepdims=True)
        xc = c0v - mu
        rstd = lax.rsqrt(jnp.mean(xc * xc, axis=-1, keepdims=True) + EPS)
        xh = xc * rstd
        c1 = xh * lng_ref[...] + lnb_ref[...]
        s1 = _sigmoid(c1)
        c2 = c1 * s1
        z = z_ref[...].astype(F32)
        sz = _sigmoid(z)
        silu_z = z * sz
        u_conv = (c2 * silu_z).astype(BF16)
        y_conv = _dot(u_conv, w_conv)
        ov = o_ref[...]
        r3 = []
        for h in range(HEADS):
            sl = slice(h * HEAD_W, (h + 1) * HEAD_W)
            oh = ov[:, sl]
            r_h = lax.rsqrt(jnp.mean(oh * oh, axis=-1, keepdims=True) + EPS)
            r3.append(r_h)
            on_sc[:, sl] = oh * r_h
        o_n = on_sc[...]
        o_g = o_n * gng_ref[...]
        gr = gr_ref[...].astype(F32)
        sgr = _sigmoid(gr)
        silu_g = gr * sgr
        u_rec = (o_g * silu_g).astype(BF16)
        y_rec = _dot(u_rec, w_rec)
        mc = mc_ref[...].astype(F32)
        mr = mr_ref[...].astype(F32)
        smc = _sigmoid(mc)
        smr = _sigmoid(mr)
        merged = (smc * y_conv + smr * y_rec).astype(BF16)
        res = x_ref[...] + _dot(merged, w_out)
        r2 = lax.rsqrt(jnp.mean(res * res, axis=-1, keepdims=True) + EPS)
        xh2 = res * r2
        row = lax.broadcasted_iota(jnp.int32, (tm, 1), 0) + i * tm
        real = row >= ROW0
        diff = jnp.where(real, xh2 * fg_ref[...] - t_ref[...], 0.0)
        d_y = diff * (1.0 / D)
        d_xh2 = d_y * fg_ref[...]
        d_res = r2 * (d_xh2 - xh2 * jnp.mean(d_xh2 * xh2, axis=-1, keepdims=True))
        dres_ref[...] = d_res
        d_res_b = d_res.astype(BF16)
        d_merged = _dot_nt(d_res_b, w_out)
        d_yc = (d_merged * smc).astype(BF16)
        d_yr = (d_merged * smr).astype(BF16)
        dp_ref[:, D:2 * D] = (d_merged * y_conv * smc * (1.0 - smc)).astype(BF16)
        dp_ref[:, 2 * D:3 * D] = (d_merged * y_rec * smr * (1.0 - smr)).astype(BF16)
        d_ur = _dot_nt(d_yr, w_rec)
        d_og = d_ur * silu_g
        dp_ref[:, 0:D] = (d_ur * o_g * _dsilu(gr, sgr)).astype(BF16)
        d_on = d_og * gng_ref[...]
        for h in range(HEADS):
            sl = slice(h * HEAD_W, (h + 1) * HEAD_W)
            d_h = d_on[:, sl]
            n_h = o_n[:, sl]
            don_sc[:, sl] = r3[h] * (d_h - n_h * jnp.mean(d_h * n_h, axis=-1, keepdims=True))
        do_ref[...] = don_sc[...]
        d_uc = _dot_nt(d_yc, w_conv)
        d_c2 = d_uc * silu_z
        dz_ref[...] = (d_uc * c2 * _dsilu(z, sz)).astype(BF16)
        d_c1 = d_c2 * _dsilu(c1, s1)
        d_xh = d_c1 * lng_ref[...]
        d_c0 = rstd * (d_xh - jnp.mean(d_xh, axis=-1, keepdims=True)
                       - xh * jnp.mean(d_xh * xh, axis=-1, keepdims=True))
        dc0_ref[...] = d_c0
        a3_ref[0] = u_conv
        b3_ref[0] = d_yc
        a3_ref[1] = u_rec
        b3_ref[1] = d_yr
        a3_ref[2] = merged
        b3_ref[2] = d_res_b
        def colsum(vv):
            return jnp.sum(vv, axis=0, keepdims=True)

        red_ref[0:1, :] += colsum(d_y * xh2)
        red_ref[1:2, :] += colsum(d_og * o_n)
        red_ref[2:3, :] += colsum(d_c1 * xh)
        red_ref[3:4, :] += colsum(d_c1)
        red_ref[4:5, :] += colsum(d_c0)
        red_ref[5:6, :] += colsum(diff * diff) * (0.5 / D)

    def row_block(width, col):
        return pl.BlockSpec((tm, width), lambda i: (i, col))

    def const_block(shape):
        return pl.BlockSpec(shape, lambda i: (0,) * len(shape))

    stack = jax.ShapeDtypeStruct((3, TP, D), BF16)
    stack_spec = pl.BlockSpec((3, tm, D), lambda i: (0, i, 0))
    return pl.pallas_call(
        body, name="mid", grid=(TP // tm,),
        in_specs=[row_block(D, 0), row_block(D, 0), row_block(D, 0), row_block(D, 0),
                  row_block(D, 2), row_block(D, 6), row_block(D, 7), row_block(D, 8),
                  pl.BlockSpec((3, D, D), lambda i: (0, 0, 0), pipeline_mode=pl.Buffered(1)),
                  const_block((1, D)), const_block((1, D)), const_block((1, D)), const_block((1, D))],
        out_specs=(row_block(D, 0), row_block(D, 0), row_block(D, 0), row_block(D, 0), row_block(3 * D, 2),
                   stack_spec, stack_spec, const_block((8, D))),
        out_shape=(jax.ShapeDtypeStruct((TP, D), F32), jax.ShapeDtypeStruct((TP, D), F32),
                   jax.ShapeDtypeStruct((TP, D), F32), jax.ShapeDtypeStruct((TP, D), BF16),
                   jax.ShapeDtypeStruct((TP, D_IN), BF16), stack, stack, jax.ShapeDtypeStruct((8, D), F32)),
        scratch_shapes=[pltpu.VMEM((tm, D), F32), pltpu.VMEM((tm, D), F32)],
        compiler_params=pltpu.CompilerParams(dimension_semantics=("arbitrary",), vmem_limit_bytes=60 * 1024 * 1024),
    )(xin, tgt, o, c0, proj, proj, proj, proj, w3, ln_g, ln_b, gnorm_g, final_g)


def _conv_bwd(proj, d_c0, d_z, conv_w, dproj):
    tm = TM_ELT
    n_tile = TP // tm
    lastt = n_tile - 1

    strip = tm // CONV_STRIPS

    def body(p_ref, dc_ref, dz_ref, w_ref, dproj_in, dp_ref, dw_ref, dsh, a_sc, da_sc, acc):
        del dproj_in
        i = pl.program_id(0)

        @pl.when(i == 0)
        def _():
            dsh[0, tm:tm + HALO, :] = jnp.zeros((HALO, D), F32)
            acc[...] = jnp.zeros_like(acc)

        @pl.when(i > 0)
        def _():
            dsh[0, tm:tm + HALO, :] = dsh[0, 0:HALO, :]

        dsh[0, 0:tm, :] = dc_ref[...]
        _fill_shifts(dsh, tm)
        ga = p_ref[:, 0:D].astype(F32)
        sb = _sigmoid(p_ref[:, D:2 * D].astype(F32))
        a_sc[...] = ga * sb
        for cb in range(D // HEAD_W):
            cs = slice(cb * HEAD_W, (cb + 1) * HEAD_W)
            for st in range(CONV_STRIPS):
                rows = slice(st * strip, (st + 1) * strip)
                a_s = a_sc[rows, cs]
                d_a = jnp.zeros((strip, HEAD_W), F32)
                for j in range(CONV_K):
                    off = CONV_K - 1 - j
                    lo = st * strip + 8 * (off // 8)
                    slab = dsh[off % 8, lo:lo + strip, cs]
                    d_a = d_a + w_ref[j:j + 1, cs] * slab
                    acc[j, :, cs] += jnp.sum((a_s * slab).reshape(strip // 8, 8, HEAD_W), axis=0)
                da_sc[rows, cs] = d_a
        d_a = da_sc[...]
        dp_ref[:, 0:D] = (d_a * sb).astype(BF16)
        dp_ref[:, D:2 * D] = (d_a * a_sc[...] * (1.0 - sb)).astype(BF16)
        dp_ref[:, 2 * D:3 * D] = dz_ref[...]

        @pl.when(i == lastt)
        def _():
            for j in range(CONV_K):
                dw_ref[j:j + 1, :] = jnp.sum(acc[j], axis=0, keepdims=True)
            dw_ref[CONV_K:CONV_K + 1, :] = jnp.zeros((1, D), F32)

    return pl.pallas_call(
        body, name="conv_bwd", grid=(n_tile,),
        in_specs=[pl.BlockSpec((tm, 2 * D), lambda i: (lastt - i, 0)), pl.BlockSpec((tm, D), lambda i: (lastt - i, 0)),
                  pl.BlockSpec((tm, D), lambda i: (lastt - i, 0)), pl.BlockSpec((CONV_K, D), lambda i: (0, 0)), ANY],
        out_specs=(pl.BlockSpec((tm, 3 * D), lambda i: (lastt - i, 0)), pl.BlockSpec((CONV_K + 1, D), lambda i: (0, 0))),
        out_shape=(jax.ShapeDtypeStruct((TP, D_IN), BF16), jax.ShapeDtypeStruct((CONV_K + 1, D), F32)),
        scratch_shapes=[pltpu.VMEM((8, tm + HALO, D), F32), pltpu.VMEM((tm, D), F32), pltpu.VMEM((tm, D), F32),
                        pltpu.VMEM((CONV_K, 8, D), F32)],
        input_output_aliases={4: 0},
        compiler_params=pltpu.CompilerParams(dimension_semantics=("arbitrary",)),
    )(proj, d_c0, d_z, conv_w, dproj)


def _wgrad3(a3, b3):
    tt = TM_MAT

    def body(a_ref, b_ref, o_ref):
        @pl.when(pl.program_id(1) == 0)
        def _():
            o_ref[...] = jnp.zeros_like(o_ref)

        o_ref[0] += _dot_tn(a_ref[0], b_ref[0])

    return pl.pallas_call(
        body, name="wgrad3", grid=(3, TP // tt),
        in_specs=[pl.BlockSpec((1, tt, D), lambda g, t: (g, t, 0)), pl.BlockSpec((1, tt, D), lambda g, t: (g, t, 0))],
        out_specs=pl.BlockSpec((1, D, D), lambda g, t: (g, 0, 0)),
        out_shape=jax.ShapeDtypeStruct((3, D, D), F32),
        compiler_params=pltpu.CompilerParams(dimension_semantics=("arbitrary", "arbitrary")),
    )(a3, b3)


def _wgrad_in_sibling(h, dproj, ids_sib):
    tt = TM_MAT

    def body(ids_ref, a_ref, b_ref, o_ref):
        del ids_ref

        @pl.when(pl.program_id(1) == 0)
        def _():
            o_ref[...] = jnp.zeros_like(o_ref)

        o_ref[0] += _dot_tn(a_ref[...], b_ref[...])

    return pl.pallas_call(
        body, name="wgrad_in_sibling",
        grid_spec=pltpu.PrefetchScalarGridSpec(
            num_scalar_prefetch=1, grid=(4, TP // tt),
            in_specs=[pl.BlockSpec((tt, D), lambda r, t, ids: (t, 0)),
                      pl.BlockSpec((tt, W_IN_BLK), lambda r, t, ids: (t, ids[r]))],
            out_specs=pl.BlockSpec((1, D, W_IN_BLK), lambda r, t, ids: (r, 0, 0))),
        out_shape=jax.ShapeDtypeStruct((4, D, W_IN_BLK), F32),
        compiler_params=pltpu.CompilerParams(dimension_semantics=("arbitrary", "arbitrary")),
    )(ids_sib, h, dproj)


def _wgrad_in_mine(h, dproj, ids_mine, p_sib, p3):
    tt = TM_MAT
    n_t = TP // tt

    def body(ids_ref, a_ref, b_ref, ps_ref, p3_ref, o_ref, ob_ref, l0_ref, l1_ref, tmp, send_sems, recv_sems, tmp_sem):
        del ids_ref
        r = pl.program_id(0)
        t = pl.program_id(1)
        x, y, c = _my_place()
        sibling = (x, y, 1 - c)

        def to_sibling(a, q):
            if a == 0:
                src, dst = ps_ref.at[q], l0_ref.at[q]
            else:
                d = _dev_index(*_chip_rel(x, y, q), 1 - c)
                src = p3_ref.at[:, pl.ds(pl.multiple_of(d * W_ROW_BLK, W_ROW_BLK), W_ROW_BLK), :]
                dst = l1_ref.at[q]
            return pltpu.make_async_remote_copy(
                src_ref=src, dst_ref=dst, send_sem=send_sems.at[a * 4 + q], recv_sem=recv_sems.at[a * 4 + q],
                device_id=sibling, device_id_type=MESH_ID)

        @pl.when((r == 0) & (t == 0))
        def _():
            for q in range(4):
                for a in range(2):
                    to_sibling(a, q).start()

        @pl.when(t == 0)
        def _():
            o_ref[...] = jnp.zeros_like(o_ref)

        o_ref[0] += _dot_tn(a_ref[...], b_ref[...])
        landed = pltpu.make_async_copy(l0_ref.at[r], tmp, tmp_sem)

        @pl.when(t == n_t - 2)
        def _():
            to_sibling(0, r).wait_recv()
            landed.start()

        @pl.when(t == n_t - 1)
        def _():
            landed.wait()
            tot = o_ref[0] + tmp[...]
            o_ref[0] = tot
            ob_ref[0] = tot.astype(BF16)

        @pl.when((r == 3) & (t == n_t - 1))
        def _():
            for q in range(4):
                to_sibling(1, q).wait_recv()
            for q in range(4):
                for a in range(2):
                    to_sibling(a, q).wait_send()

    blk = pl.BlockSpec((1, D, W_IN_BLK), lambda r, t, ids: (r, 0, 0))
    return pl.pallas_call(
        body, name="wgrad_in_mine",
        grid_spec=pltpu.PrefetchScalarGridSpec(
            num_scalar_prefetch=1, grid=(4, n_t),
            in_specs=[pl.BlockSpec((tt, D), lambda r, t, ids: (t, 0)),
                      pl.BlockSpec((tt, W_IN_BLK), lambda r, t, ids: (t, ids[r])), ANY, ANY],
            out_specs=(blk, blk, ANY, ANY),
            scratch_shapes=[pltpu.VMEM((D, W_IN_BLK), F32), pltpu.SemaphoreType.DMA((8,)),
                            pltpu.SemaphoreType.DMA((8,)), pltpu.SemaphoreType.DMA]),
        out_shape=(jax.ShapeDtypeStruct((4, D, W_IN_BLK), F32), jax.ShapeDtypeStruct((4, D, W_IN_BLK), BF16),
                   jax.ShapeDtypeStruct((4, D, W_IN_BLK), F32), jax.ShapeDtypeStruct((4, 3, W_ROW_BLK, D), F32)),
        compiler_params=pltpu.CompilerParams(dimension_semantics=("arbitrary", "arbitrary")),
    )(ids_mine, h, dproj, p_sib, p3)


def _chip_sum_3(p3, land1, ids_mine):
    def body(ids_ref, p_ref, l_ref, o_ref, ob_ref):
        del ids_ref
        tot = p_ref[...] + l_ref[0]
        o_ref[0] = tot
        ob_ref[0] = tot.astype(BF16)

    blk = pl.BlockSpec((1, 3, W_ROW_BLK, D), lambda r, ids: (r, 0, 0, 0))
    return pl.pallas_call(
        body, name="chip_sum_3",
        grid_spec=pltpu.PrefetchScalarGridSpec(
            num_scalar_prefetch=1, grid=(4,),
            in_specs=[pl.BlockSpec((3, W_ROW_BLK, D), lambda r, ids: (0, ids[r], 0)), blk],
            out_specs=(blk, blk)),
        out_shape=(jax.ShapeDtypeStruct((4, 3, W_ROW_BLK, D), F32), jax.ShapeDtypeStruct((4, 3, W_ROW_BLK, D), BF16)),
    )(ids_mine, p3, land1)


def _dh_and_norm_bwd(dproj, w_in_full, xin, d_res, norm_g, chip0b, chip1b):
    tm = TM_MAT
    n_k = N_DEV
    n_m = TP // tm

    def body(dp_ref, w_ref, x_ref, dr_ref, g_ref, c0_ref, c1_ref, dx_ref, dg_ref, f0_ref, f1_ref, acc,
             send_sems, recv_sems):
        m = pl.program_id(0)
        k = pl.program_id(1)
        x, y, c = _my_place()

        def to_owner(a, q):
            src, dst = ((c0_ref, f0_ref), (c1_ref, f1_ref))[a]
            return pltpu.make_async_remote_copy(
                src_ref=src.at[q], dst_ref=dst.at[q - 1], send_sem=send_sems.at[a * 3 + q - 1],
                recv_sem=recv_sems.at[a * 3 + q - 1], device_id=(*_chip_rel(x, y, q), c), device_id_type=MESH_ID)

        @pl.when((m == 0) & (k == 0))
        def _():
            for q in range(1, 4):
                for a in range(2):
                    to_owner(a, q).start()

        @pl.when(k == 0)
        def _():
            acc[...] = jnp.zeros_like(acc)

        acc[...] += _dot_nt(dp_ref[...], w_ref[0])

        @pl.when((k == n_k - 1) & (m == 0))
        def _():
            dg_ref[...] = jnp.zeros_like(dg_ref)

        @pl.when(k == n_k - 1)
        def _():
            xv = x_ref[...]
            r1 = lax.rsqrt(jnp.mean(xv * xv, axis=-1, keepdims=True) + EPS)
            xh = xv * r1
            d_h = acc[...]
            dg_ref[0:1, :] += jnp.sum(d_h * xh, axis=0, keepdims=True)
            d_xh = d_h * g_ref[...]
            dx_ref[...] = dr_ref[...] + r1 * (d_xh - xh * jnp.mean(d_xh * xh, axis=-1, keepdims=True))

        @pl.when((m == n_m - 1) & (k == n_k - 1))
        def _():
            for q in range(1, 4):
                for a in range(2):
                    to_owner(a, q).wait_recv()
            for q in range(1, 4):
                for a in range(2):
                    to_owner(a, q).wait_send()

    return pl.pallas_call(
        body, name="dh_norm_bwd", grid=(n_m, n_k),
        in_specs=[pl.BlockSpec((tm, W_IN_BLK), lambda m, k: (m, k)), pl.BlockSpec((1, D, W_IN_BLK), lambda m, k: (k, 0, 0)),
                  pl.BlockSpec((tm, D), lambda m, k: (m, 0)), pl.BlockSpec((tm, D), lambda m, k: (m, 0)),
                  pl.BlockSpec((1, D), lambda m, k: (0, 0)), ANY, ANY],
        out_specs=(pl.BlockSpec((tm, D), lambda m, k: (m, 0)), pl.BlockSpec((8, D), lambda m, k: (0, 0)), ANY, ANY),
        out_shape=(jax.ShapeDtypeStruct((TP, D), F32), jax.ShapeDtypeStruct((8, D), F32),
                   jax.ShapeDtypeStruct((3, D, W_IN_BLK), BF16), jax.ShapeDtypeStruct((3, 3, W_ROW_BLK, D), BF16)),
        scratch_shapes=[pltpu.VMEM((tm, D), F32), pltpu.SemaphoreType.DMA((6,)), pltpu.SemaphoreType.DMA((6,))],
        compiler_params=pltpu.CompilerParams(dimension_semantics=("arbitrary", "arbitrary")),
    )(dproj, w_in_full, xin, d_res, norm_g, chip0b, chip1b)


def _sum_adamw(own, landed, w, m, v, tr, name):
    rows, cols = w.shape
    n_t = rows // tr

    def body(o_ref, l1_ref, l2_ref, l3_ref, w_ref, m_ref, v_ref, g_ref, d_ref, m2_ref, v2_ref):
        g = ((o_ref[...] + l1_ref[...].astype(F32)) + l2_ref[...].astype(F32)) + l3_ref[...].astype(F32)
        delta, m2, v2 = _adamw(w_ref[...], g, m_ref[...], v_ref[...])
        g_ref[...] = g
        d_ref[...] = delta
        m2_ref[...] = m2
        v2_ref[...] = v2

    def spec(k):
        return pl.BlockSpec((tr, cols), lambda i: (i + k * n_t, 0))

    out = jax.ShapeDtypeStruct((rows, cols), F32)
    return pl.pallas_call(
        body, name=name, grid=(n_t,),
        in_specs=[spec(0), spec(0), spec(1), spec(2), spec(0), spec(0), spec(0)],
        out_specs=(spec(0),) * 4, out_shape=(out,) * 4,
    )(own, landed, landed, landed, w, m, v)


def _small_update(pack_all, srs_all, lb_logits, p8, m8, v8, ws, ms, vs):
    def body(pk_ref, sr_ref, lbl_ref, p_ref, m_ref, v_ref, ws_ref, ms_ref, vs_ref,
             g8_ref, d8_ref, m8_ref, v8_ref, loss_ref, gs_ref, ds_ref, ms2_ref, vs2_ref):
        tot = pk_ref[0]
        tot_s = sr_ref[0]
        for d in range(1, N_DEV):
            tot = tot + pk_ref[d]
            tot_s = tot_s + sr_ref[d]
        p0 = _sigmoid(lbl_ref[0:1, :] - lbl_ref[1:2, :])
        row = lax.broadcasted_iota(jnp.int32, (8, D), 0)
        d_lb = jnp.sum(jnp.where(row == 4, tot, 0.0), axis=0, keepdims=True)
        d_l0 = d_lb * p0 * (1.0 - p0)
        loss_ref[...] = jnp.sum(jnp.where(row == 5, tot, 0.0), keepdims=True).reshape(1, 1)
        g8 = jnp.where(row == 4, d_l0, jnp.where(row == 5, -d_l0, tot))
        delta, m2, v2 = _adamw(p_ref[...], g8, m_ref[...], v_ref[...])
        g8_ref[...] = g8
        d8_ref[...] = delta
        m8_ref[...] = m2
        v8_ref[...] = v2
        delta, m2, v2 = _adamw(ws_ref[...], tot_s, ms_ref[...], vs_ref[...])
        gs_ref[...] = tot_s
        ds_ref[...] = delta
        ms2_ref[...] = m2
        vs2_ref[...] = v2

    o8 = jax.ShapeDtypeStruct((8, D), F32)
    os_ = jax.ShapeDtypeStruct((SMALL_ROWS, HEAD_W), F32)
    return pl.pallas_call(
        body, name="small_update",
        out_shape=(o8, o8, o8, o8, jax.ShapeDtypeStruct((1, 1), F32), os_, os_, os_, os_),
    )(pack_all, srs_all, lb_logits, p8, m8, v8, ws, ms, vs)


def _local_step(x_seq, target, meta_full, norm_g, w_in_full, conv_w_full, conv_b, ln_g, ln_b, w3_full, lb_logits,
                gnorm_g, final_g):
    xin = jnp.concatenate([jnp.zeros((PAD_FRONT, D), F32), meta_full, x_seq], axis=0)
    tgt = jnp.concatenate([jnp.zeros((ROW0, D), F32), target], axis=0)
    fg = final_g.reshape(1, D)
    h = _rms_h(xin, norm_g)
    proj = _proj(h, w_in_full)
    c0 = _conv_fwd(proj, conv_w_full, conv_b)
    o, s_start = _rec_fwd(proj, lb_logits)
    d_res, d_o, d_c0, d_z, dproj, a3, b3, red = _mid(xin, tgt, o, c0, proj, w3_full, ln_g, ln_b, gnorm_g, fg)
    dproj, dlb = _rec_bwd(proj, lb_logits, d_o, s_start, dproj)
    dproj, d_conv_w = _conv_bwd(proj, d_c0, d_z, conv_w_full, dproj)
    p3 = _wgrad3(a3, b3)
    return xin, h, dproj, d_res, p3, d_conv_w, red, dlb


def kernel(x, meta_tokens, norm_g, w_in, conv_w, conv_b, ln_g, ln_b, w_conv_out, lb_logits, gnorm_g, w_rec_out, w_out, final_g, loss_target, m_meta_tokens, m_norm_g, m_w_in, m_conv_w, m_conv_b, m_ln_g, m_ln_b, m_w_conv_out, m_lb_logits, m_gnorm_g, m_w_rec_out, m_w_out, m_final_g, v_meta_tokens, v_norm_g, v_w_in, v_conv_w, v_conv_b, v_ln_g, v_ln_b, v_w_conv_out, v_lb_logits, v_gnorm_g, v_w_rec_out, v_w_out, v_final_g):
    def small_pack(cw, mt):
        return jnp.concatenate([cw[0], jnp.zeros((1, HEAD_W), F32), mt], axis=0)

    def stack3(a, b, c):
        return jnp.concatenate([a, b, c], axis=0)

    def stack8(ng, cb, lg, lb_, lbl, gg, fg):
        return jnp.concatenate([ng, cb, lg, lb_, lbl, gg, fg.reshape(1, D)], axis=0)

    w3_s = stack3(w_conv_out, w_rec_out, w_out)
    ws_s = small_pack(conv_w, meta_tokens)
    w_in_b, w3_b = _cast_shards(w_in[0], w3_s)
    w_in_full, w3_full, small_full = _all_gather_weights(w_in_b, w3_b, ws_s)
    small_full = jnp.transpose(small_full, (1, 0, 2)).reshape(SMALL_ROWS, D)
    conv_w_full = small_full[0:CONV_K]
    meta_full = small_full[META_ROW:META_ROW + N_META]

    xin, h, dproj, d_res, p3, d_conv_w, red, dlb = _local_step(
        x[0], loss_target[0], meta_full, norm_g, w_in_full, conv_w_full, conv_b, ln_g, ln_b, w3_full, lb_logits,
        gnorm_g, final_g)

    mx, my, mc = _my_place()
    ids_mine = jnp.stack([_dev_index(*_chip_rel(mx, my, r), mc) for r in range(4)]).astype(jnp.int32)
    ids_sib = jnp.stack([_dev_index(*_chip_rel(mx, my, r), 1 - mc) for r in range(4)]).astype(jnp.int32)
    p_sib = _wgrad_in_sibling(h, dproj, ids_sib)
    chip0, chip0b, _, land1 = _wgrad_in_mine(h, dproj, ids_mine, p_sib, p3)
    chip1, chip1b = _chip_sum_3(p3, land1, ids_mine)
    d_xin, dng, far0, far1 = _dh_and_norm_bwd(dproj, w_in_full, xin, d_res, norm_g, chip0b, chip1b)
    pack = jnp.concatenate([dng[0:1], red[4:5], red[2:3], red[3:4], dlb[0:1], red[5:6], red[1:2], red[0:1]], axis=0)
    g_in, d_in, m_in, v_in = _sum_adamw(chip0.reshape(4 * D, W_IN_BLK), far0.reshape(3 * D, W_IN_BLK), w_in[0],
                                        m_w_in[0], v_w_in[0], 256, "adamw_in")
    g_3, d_3, m_3, v_3 = _sum_adamw(
        chip1.reshape(12 * W_ROW_BLK, D), far1.reshape(9 * W_ROW_BLK, D), w3_s.reshape(3 * W_ROW_BLK, D),
        stack3(m_w_conv_out, m_w_rec_out, m_w_out).reshape(3 * W_ROW_BLK, D),
        stack3(v_w_conv_out, v_w_rec_out, v_w_out).reshape(3 * W_ROW_BLK, D), 3 * W_ROW_BLK, "adamw_3")

    srs = jnp.concatenate([d_conv_w, d_xin[PAD_FRONT:ROW0]], axis=0)
    srs = jnp.transpose(srs.reshape(SMALL_ROWS, N_DEV, HEAD_W), (1, 0, 2))
    pack_all, srs_all = _exchange_small(pack, srs)
    g8, d8, m8, v8, loss, gs, ds, ms, vs = _small_update(
        pack_all, srs_all, lb_logits,
        stack8(norm_g, conv_b, ln_g, ln_b, lb_logits, gnorm_g, final_g),
        stack8(m_norm_g, m_conv_b, m_ln_g, m_ln_b, m_lb_logits, m_gnorm_g, m_final_g),
        stack8(v_norm_g, v_conv_b, v_ln_g, v_ln_b, v_lb_logits, v_gnorm_g, v_final_g),
        ws_s, small_pack(m_conv_w, m_meta_tokens), small_pack(v_conv_w, v_meta_tokens))

    def unpack(a_in, a_3, a_s, a_8):
        t3 = a_3.reshape(3, 1, W_ROW_BLK, D)
        return (a_s[META_ROW:META_ROW + N_META], a_8[0:1], a_in[None], a_s[0:CONV_K][None], a_8[1:2], a_8[2:3],
                a_8[3:4], t3[0], a_8[4:6], a_8[6:7], t3[1], t3[2], a_8[7])

    grad_x = d_xin[ROW0:][None]
    return (loss.reshape(()), grad_x, *unpack(g_in, g_3, gs, g8), *unpack(d_in, d_3, ds, d8),
            *unpack(m_in, m_3, ms, m8), *unpack(v_in, v_3, vs, v8))
```

```python
import functools

import jax
import jax.numpy as jnp
from jax import lax
from jax.experimental import pallas as pl
from jax.experimental.pallas import tpu as pltpu

F32 = jnp.float32
BF16 = jnp.bfloat16

D = 1024
SEQ = 4096
N_META = 16
CHUNK = 64
PAD_FRONT = 48
ROW0 = PAD_FRONT + N_META
TP = ROW0 + SEQ
N_CHUNK = TP // CHUNK
HEADS = 8
HEAD_W = 128
D_IN = 9 * D
N_DEV = 8
W_IN_BLK = D_IN // N_DEV
W_ROW_BLK = D // N_DEV
CONV_K = 31
SMALL_ROWS = 48
META_ROW = 32
EPS = 1e-6
HALO = 32

TM_MAT = 832
TM_ELT = 208
CONV_STRIPS = 2

ADAM_LR = 0.001
ADAM_B1 = 0.9
ADAM_B2 = 0.999
ADAM_EPS = 1e-08
ADAM_WD = 0.01
ADAM_STEP = 10

MESH_ID = pl.DeviceIdType.MESH
ANY = pl.BlockSpec(memory_space=pl.ANY)


def _sigmoid(v):
    return jax.nn.sigmoid(v)


def _dsilu(v, s):
    return s * (1.0 + v * (1.0 - s))


def _dot(a, b):
    return jnp.dot(a, b, preferred_element_type=F32)


def _dot_nt(a, b):
    return lax.dot_general(a, b, (((1,), (1,)), ((), ())), preferred_element_type=F32)


def _dot_tn(a, b):
    return lax.dot_general(a, b, (((0,), (0,)), ((), ())), preferred_element_type=F32)


def _split3(v):
    hi = v.astype(BF16)
    r1 = v - hi.astype(F32)
    mid = r1.astype(BF16)
    lo = (r1 - mid.astype(F32)).astype(BF16)
    return hi, mid, lo


def _tri_matmul(tri, v):
    hi, mid, lo = _split3(v)
    return _dot(tri, hi) + _dot(tri, mid) + _dot(tri, lo)


def _adamw(w, g, m, v):
    m2 = ADAM_B1 * m + (1.0 - ADAM_B1) * g
    v2 = ADAM_B2 * v + (1.0 - ADAM_B2) * jnp.square(g)
    m_hat = m2 / (1.0 - ADAM_B1 ** ADAM_STEP)
    v_hat = v2 / (1.0 - ADAM_B2 ** ADAM_STEP)
    delta = -ADAM_LR * (m_hat / (jnp.sqrt(v_hat) + ADAM_EPS) + ADAM_WD * w)
    return delta, m2, v2


def _my_place():
    return lax.axis_index("x"), lax.axis_index("y"), lax.axis_index("c")


def _dev_index(px, py, pc):
    return 4 * px + 2 * py + pc


def _cast_shards(w_in_s, w3_s):
    def body(a_ref, b_ref, oa_ref, ob_ref):
        oa_ref[...] = a_ref[...].astype(BF16)
        ob_ref[...] = b_ref[...].astype(BF16)

    return pl.pallas_call(
        body, name="cast_shards",
        out_shape=(jax.ShapeDtypeStruct(w_in_s.shape, BF16), jax.ShapeDtypeStruct(w3_s.shape, BF16)),
    )(w_in_s, w3_s)


def _peer(x, y, c, r):
    return (jnp.bitwise_xor(x, (r >> 2) & 1), jnp.bitwise_xor(y, (r >> 1) & 1), jnp.bitwise_xor(c, r & 1))


def _gather_small(small_s):
    def body(s_ref, o_ref, send_sems, recv_sems, local_sem):
        x, y, c = _my_place()
        my_id = _dev_index(x, y, c)
        mine = pltpu.make_async_copy(s_ref, o_ref.at[my_id], local_sem)
        mine.start()
        copies = []
        for r in range(1, N_DEV):
            cp = pltpu.make_async_remote_copy(
                src_ref=s_ref, dst_ref=o_ref.at[my_id], send_sem=send_sems.at[r - 1], recv_sem=recv_sems.at[r - 1],
                device_id=_peer(x, y, c, r), device_id_type=MESH_ID)
            cp.start()
            copies.append(cp)
        for cp in copies:
            cp.wait_recv()
        for cp in copies:
            cp.wait_send()
        mine.wait()

    return pl.pallas_call(
        body, name="gather_small", out_shape=jax.ShapeDtypeStruct((N_DEV,) + small_s.shape, F32),
        in_specs=[ANY], out_specs=ANY,
        scratch_shapes=[pltpu.SemaphoreType.DMA((7,)), pltpu.SemaphoreType.DMA((7,)), pltpu.SemaphoreType.DMA],
    )(small_s)


def _gather_and_proj(xin, norm_g, w_in_b, w3_b, order):
    tm = TM_MAT
    n_m = TP // tm
    last_m = n_m - 1

    def body(order_ref, x_ref, g_ref, s0, s1, proj_ref, h_out, o0, o1, hbuf, wbuf, w3buf, send_sems, recv_sems, local_sems):
        del order_ref
        n = pl.program_id(0)
        m = pl.program_id(1)
        x, y, c = _my_place()
        me, sibling = (x, y, c), (x, y, 1 - c)
        chips = [(1 - x, y), (x, 1 - y), (1 - x, 1 - y)]
        srcs, outs = (s0, s1), (o0, o1)

        def block(a, place):
            d = _dev_index(*place)
            if a == 1:
                return o1.at[:, pl.ds(pl.multiple_of(d * W_ROW_BLK, W_ROW_BLK), W_ROW_BLK), :]
            return o0.at[d]

        def copy(a, k, place, to, from_src=False):
            return pltpu.make_async_remote_copy(
                src_ref=srcs[a] if from_src else block(a, place), dst_ref=block(a, place),
                send_sem=send_sems.at[a * 7 + k], recv_sem=recv_sems.at[a * 7 + k],
                device_id=to, device_id_type=MESH_ID)

        def to_vmem(place, slot):
            return pltpu.make_async_copy(block(0, place), wbuf.at[slot], local_sems.at[slot])

        own_out = [pltpu.make_async_copy(wbuf.at[0], block(0, me), local_sems.at[2]),
                   pltpu.make_async_copy(w3buf, block(1, me), local_sems.at[3])]
        h_copy = pltpu.make_async_copy(hbuf, h_out, local_sems.at[4])

        @pl.when((n == 0) & (m == 0))
        def _():
            for a in range(2):
                copy(a, 0, me, sibling, from_src=True).start()
                for j, chip in enumerate(chips):
                    copy(a, 1 + j, me, (*chip, c), from_src=True).start()
            mine = [pltpu.make_async_copy(s0, wbuf.at[0], local_sems.at[0]),
                    pltpu.make_async_copy(s1, w3buf, local_sems.at[1])]
            for cp in mine:
                cp.start()
            for cp in mine:
                cp.wait()
            for cp in own_out:
                cp.start()

        @pl.when(n == 0)
        def _():
            xv = x_ref[...]
            r = lax.rsqrt(jnp.mean(xv * xv, axis=-1, keepdims=True) + EPS)
            hbuf[m] = (xv * r * g_ref[...]).astype(BF16)

        plan = [(sibling, (0, sibling), None)]
        for j, chip in enumerate(chips):
            plan.append(((*chip, c), (1 + j, (*chip, c)), 4 + j))
            plan.append(((*chip, 1 - c), (4 + j, (*chip, 1 - c)), None))

        for s, (place, (k, origin), pass_on) in enumerate(plan, start=1):
            @pl.when((n == s - 1) & (m == last_m))
            def _(s=s, place=place, k=k, origin=origin, pass_on=pass_on):
                copy(0, k, origin, me).wait_recv()
                if pass_on is not None:
                    copy(0, pass_on, place, sibling).start()
                if s == 2:
                    own_out[0].wait()
                to_vmem(place, s % 2).start()

            @pl.when((n == s) & (m == 0))
            def _(s=s, place=place):
                to_vmem(place, s % 2).wait()

        proj_ref[...] = _dot(hbuf[m], wbuf[lax.rem(n, 2)]).astype(BF16)

        @pl.when((n == 0) & (m == last_m))
        def _():
            h_copy.start()

        @pl.when((n == N_DEV - 1) & (m == last_m))
        def _():
            for j, chip in enumerate(chips):
                copy(1, 1 + j, (*chip, c), me).wait_recv()
                copy(1, 4 + j, (*chip, c), sibling).start()
            copy(1, 0, sibling, me).wait_recv()
            for j, chip in enumerate(chips):
                copy(1, 4 + j, (*chip, 1 - c), me).wait_recv()
            for a in range(2):
                for k in range(7):
                    copy(a, k, me, me).wait_send()
            own_out[1].wait()
            h_copy.wait()

    return pl.pallas_call(
        body, name="gather_and_proj",
        grid_spec=pltpu.PrefetchScalarGridSpec(
            num_scalar_prefetch=1, grid=(N_DEV, n_m),
            in_specs=[pl.BlockSpec((tm, D), lambda n, m, o: (jnp.where(n == 0, m, 0), 0)),
                      pl.BlockSpec((1, D), lambda n, m, o: (0, 0)), ANY, ANY],
            out_specs=(pl.BlockSpec((tm, W_IN_BLK), lambda n, m, o: (m, o[n])), ANY, ANY, ANY),
            scratch_shapes=[pltpu.VMEM((n_m, tm, D), BF16), pltpu.VMEM((2, D, W_IN_BLK), BF16),
                            pltpu.VMEM((3, W_ROW_BLK, D), BF16), pltpu.SemaphoreType.DMA((14,)),
                            pltpu.SemaphoreType.DMA((14,)), pltpu.SemaphoreType.DMA((5,))]),
        out_shape=(jax.ShapeDtypeStruct((TP, D_IN), BF16), jax.ShapeDtypeStruct((n_m, tm, D), BF16),
                   jax.ShapeDtypeStruct((N_DEV, D, W_IN_BLK), BF16), jax.ShapeDtypeStruct((3, D, D), BF16)),
        compiler_params=pltpu.CompilerParams(dimension_semantics=("arbitrary", "arbitrary")),
    )(order, xin, norm_g, w_in_b, w3_b)


def _chip_rel(x, y, r):
    return (jnp.bitwise_xor(x, r >> 1), jnp.bitwise_xor(y, r & 1))


def _exchange_small(pack, srs):
    def body(pk, sr, pk_all, sr_all, send_sems, recv_sems, local_sems):
        x, y, c = _my_place()
        my_id = _dev_index(x, y, c)
        mine = [pltpu.make_async_copy(pk, pk_all.at[my_id], local_sems.at[0]),
                pltpu.make_async_copy(sr.at[my_id], sr_all.at[my_id], local_sems.at[1])]
        for cp in mine:
            cp.start()
        copies = []
        for r in range(1, N_DEV):
            peer = (jnp.bitwise_xor(x, (r >> 2) & 1), jnp.bitwise_xor(y, (r >> 1) & 1), jnp.bitwise_xor(c, r & 1))
            peer_id = _dev_index(*peer)
            for a, (src, dst) in enumerate(((pk, pk_all.at[my_id]), (sr.at[peer_id], sr_all.at[my_id]))):
                cp = pltpu.make_async_remote_copy(
                    src_ref=src, dst_ref=dst, send_sem=send_sems.at[a * 7 + r - 1], recv_sem=recv_sems.at[a * 7 + r - 1],
                    device_id=peer, device_id_type=MESH_ID)
                cp.start()
                copies.append(cp)
        for cp in copies:
            cp.wait_recv()
        for cp in copies:
            cp.wait_send()
        for cp in mine:
            cp.wait()

    return pl.pallas_call(
        body, name="exchange_small",
        out_shape=(jax.ShapeDtypeStruct((N_DEV,) + pack.shape, F32), jax.ShapeDtypeStruct(srs.shape, F32)),
        in_specs=[ANY, ANY], out_specs=(ANY, ANY),
        scratch_shapes=[pltpu.SemaphoreType.DMA((14,)), pltpu.SemaphoreType.DMA((14,)), pltpu.SemaphoreType.DMA((2,))],
    )(pack, srs)


def _fill_shifts(sh, tm):
    n = tm + HALO - 8
    for s in range(1, 8):
        sh[s, 0:n, :] = sh[0, s:s + n, :]


def _conv_fwd(proj, conv_w, conv_b):
    tm = TM_ELT
    strip = tm // CONV_STRIPS

    def body(p_ref, w_ref, b_ref, c0_ref, sh):
        i = pl.program_id(0)

        @pl.when(i == 0)
        def _():
            sh[0, 0:HALO, :] = jnp.zeros((HALO, D), F32)

        @pl.when(i > 0)
        def _():
            sh[0, 0:HALO, :] = sh[0, tm:tm + HALO, :]

        ga = p_ref[:, 0:D].astype(F32)
        gb = p_ref[:, D:2 * D].astype(F32)
        sh[0, HALO:HALO + tm, :] = ga * _sigmoid(gb)
        _fill_shifts(sh, tm)
        for cb in range(D // HEAD_W):
            cs = slice(cb * HEAD_W, (cb + 1) * HEAD_W)
            for st in range(CONV_STRIPS):
                acc = jnp.broadcast_to(b_ref[:, cs], (strip, HEAD_W))
                for j in range(CONV_K):
                    off = HALO - (CONV_K - 1) + j
                    lo = st * strip + 8 * (off // 8)
                    acc = acc + w_ref[j:j + 1, cs] * sh[off % 8, lo:lo + strip, cs]
                c0_ref[st * strip:(st + 1) * strip, cs] = acc

    return pl.pallas_call(
        body, name="conv_fwd", grid=(TP // tm,),
        in_specs=[pl.BlockSpec((tm, 2 * D), lambda i: (i, 0)), pl.BlockSpec((CONV_K, D), lambda i: (0, 0)),
                  pl.BlockSpec((1, D), lambda i: (0, 0))],
        out_specs=pl.BlockSpec((tm, D), lambda i: (i, 0)),
        out_shape=jax.ShapeDtypeStruct((TP, D), F32),
        scratch_shapes=[pltpu.VMEM((8, HALO + tm, D), F32)],
        compiler_params=pltpu.CompilerParams(dimension_semantics=("arbitrary",)),
    )(proj, conv_w, conv_b)


def _gates(p_ref, lbl_ref, chunk, bsc):
    lb = _sigmoid(lbl_ref[0:1, :] - lbl_ref[1:2, :])
    q_raw = p_ref[:, 0:D].astype(F32)
    f_raw = p_ref[:, D:2 * D].astype(F32)
    sq = _sigmoid(q_raw)
    q = q_raw * sq
    sg = _sigmoid(f_raw)
    f = lb + (1.0 - lb) * sg
    row = lax.broadcasted_iota(jnp.int32, (CHUNK, 1), 0) + chunk * CHUNK
    valid = row >= PAD_FRONT
    lf = jnp.where(valid, jnp.log(f), 0.0)
    k = jnp.where(valid, 1.0 - f, 0.0)
    r_i = lax.broadcasted_iota(jnp.int32, (CHUNK, CHUNK), 0)
    c_i = lax.broadcasted_iota(jnp.int32, (CHUNK, CHUNK), 1)
    causal = r_i >= c_i
    bsc[...] = _tri_matmul(causal.astype(BF16), lf)
    b = bsc[...]
    b_mid = bsc[CHUNK // 2 - 1:CHUNK // 2, :]
    b_last = bsc[CHUNK - 1:CHUNK, :]
    e_q = jnp.exp(b)
    e_qm = jnp.exp(b - b_mid)
    e_km = jnp.exp(b_mid - b)
    e_kh = jnp.exp(b_last - b)
    e_last = jnp.exp(b_last)
    return dict(lb=lb, q_raw=q_raw, sq=sq, q=q, sg=sg, f=f, k=k, valid=valid, causal=causal,
                e_q=e_q, e_qm=e_qm, e_km=e_km, e_kh=e_kh, e_last=e_last)


def _rec_fwd(proj, lb_logits):
    def body(p_ref, lbl_ref, o_ref, s_ref, st, bsc):
        n = pl.program_id(0)

        @pl.when(n == 0)
        def _():
            st[...] = jnp.zeros_like(st)

        s_ref[0] = st[...]
        g = _gates(p_ref, lbl_ref, n, bsc)
        q1 = (g["q"] * g["e_q"]).astype(BF16)
        qm = (g["q"] * g["e_qm"]).astype(BF16)
        km = (g["k"] * g["e_km"]).astype(BF16)
        kh = (g["k"] * g["e_kh"]).astype(BF16)
        for h in range(HEADS):
            sl = slice(h * HEAD_W, (h + 1) * HEAD_W)
            v = p_ref[:, 2 * D + h * HEAD_W:2 * D + (h + 1) * HEAD_W]
            att = jnp.where(g["causal"], _dot_nt(qm[:, sl], km[:, sl]), 0.0).astype(BF16)
            s_h = st[h]
            o_ref[:, sl] = _dot_nt(q1[:, sl], s_h.astype(BF16)) + _dot(att, v)
            st[h] = s_h * g["e_last"][:, sl] + _dot_tn(v, kh[:, sl])

    return pl.pallas_call(
        body, name="rec_fwd", grid=(N_CHUNK,),
        in_specs=[pl.BlockSpec((CHUNK, 3 * D), lambda n: (n, 1)), pl.BlockSpec((2, D), lambda n: (0, 0))],
        out_specs=(pl.BlockSpec((CHUNK, D), lambda n: (n, 0)),
                   pl.BlockSpec((1, HEADS, HEAD_W, HEAD_W), lambda n: (n, 0, 0, 0))),
        out_shape=(jax.ShapeDtypeStruct((TP, D), F32), jax.ShapeDtypeStruct((N_CHUNK, HEADS, HEAD_W, HEAD_W), F32)),
        scratch_shapes=[pltpu.VMEM((HEADS, HEAD_W, HEAD_W), F32), pltpu.VMEM((CHUNK, D), F32)],
        compiler_params=pltpu.CompilerParams(dimension_semantics=("arbitrary",)),
    )(proj, lb_logits)


def _rec_bwd(proj, lb_logits, d_o, s_start, dproj):
    last = N_CHUNK - 1

    def body(p_ref, lbl_ref, do_ref, s_ref, dproj_in, dp_ref, dlb_ref, dst, bsc, dq_sc, dk_sc, g_sc):
        del dproj_in
        n = pl.program_id(0)

        @pl.when(n == 0)
        def _():
            dst[...] = jnp.zeros_like(dst)
            dlb_ref[...] = jnp.zeros_like(dlb_ref)

        g = _gates(p_ref, lbl_ref, last - n, bsc)
        q1 = (g["q"] * g["e_q"]).astype(BF16)
        qm_f = g["q"] * g["e_qm"]
        km_f = g["k"] * g["e_km"]
        qm = qm_f.astype(BF16)
        km = km_f.astype(BF16)
        qm_lo = (qm_f - qm.astype(F32)).astype(BF16)
        km_lo = (km_f - km.astype(F32)).astype(BF16)
        kh_f = g["k"] * g["e_kh"]
        kh = kh_f.astype(BF16)
        for h in range(HEADS):
            sl = slice(h * HEAD_W, (h + 1) * HEAD_W)
            v = p_ref[:, 2 * D + h * HEAD_W:2 * D + (h + 1) * HEAD_W]
            d_oh = do_ref[:, sl].astype(BF16)
            s0 = s_ref[0, h]
            ds_end = dst[h]
            ds_end_b = ds_end.astype(BF16)
            att = jnp.where(g["causal"], _dot_nt(qm[:, sl], km[:, sl]), 0.0).astype(BF16)
            d_att = jnp.where(g["causal"], _dot_nt(d_oh, v), 0.0).astype(BF16)
            d_v = _dot_tn(att, d_oh) + _dot_nt(kh[:, sl], ds_end_b)
            d_qm = _dot(d_att, km[:, sl]) + _dot(d_att, km_lo[:, sl])
            d_q1 = _dot(d_oh, s0.astype(BF16))
            d_km = _dot_tn(d_att, qm[:, sl]) + _dot_tn(d_att, qm_lo[:, sl])
            d_kh = _dot(v, ds_end_b)
            dq_sc[:, sl] = d_qm * g["e_qm"][:, sl] + d_q1 * g["e_q"][:, sl]
            dk_sc[:, sl] = d_km * g["e_km"][:, sl] + d_kh * g["e_kh"][:, sl]
            g_sc[:, sl] = (jnp.sum(kh_f[:, sl] * d_kh, axis=0, keepdims=True)
                           + g["e_last"][:, sl] * jnp.sum(ds_end * s0, axis=0, keepdims=True))
            dst[h] = ds_end * g["e_last"][:, sl] + _dot_tn(d_oh, q1[:, sl])
            dp_ref[:, 2 * D + h * HEAD_W:2 * D + (h + 1) * HEAD_W] = d_v.astype(BF16)
        d_q = dq_sc[...]
        d_k = dk_sc[...]
        d_b = g["q"] * d_q - g["k"] * d_k
        anti = jnp.logical_not(g["causal"]) | (lax.broadcasted_iota(jnp.int32, (CHUNK, CHUNK), 0)
                                                == lax.broadcasted_iota(jnp.int32, (CHUNK, CHUNK), 1))
        d_lf = _tri_matmul(anti.astype(BF16), d_b) + g_sc[...]
        d_f = jnp.where(g["valid"], d_lf / g["f"] - d_k, 0.0)
        sg = g["sg"]
        dlb_ref[0:1, :] += jnp.sum(d_f * (1.0 - sg), axis=0, keepdims=True)
        dp_ref[:, 0:D] = (d_q * _dsilu(g["q_raw"], g["sq"])).astype(BF16)
        dp_ref[:, D:2 * D] = (d_f * (1.0 - g["lb"]) * sg * (1.0 - sg)).astype(BF16)

    return pl.pallas_call(
        body, name="rec_bwd", grid=(N_CHUNK,),
        in_specs=[pl.BlockSpec((CHUNK, 3 * D), lambda n: (last - n, 1)), pl.BlockSpec((2, D), lambda n: (0, 0)),
                  pl.BlockSpec((CHUNK, D), lambda n: (last - n, 0)),
                  pl.BlockSpec((1, HEADS, HEAD_W, HEAD_W), lambda n: (last - n, 0, 0, 0)), ANY],
        out_specs=(pl.BlockSpec((CHUNK, 3 * D), lambda n: (last - n, 1)), pl.BlockSpec((8, D), lambda n: (0, 0))),
        out_shape=(jax.ShapeDtypeStruct((TP, D_IN), BF16), jax.ShapeDtypeStruct((8, D), F32)),
        scratch_shapes=[pltpu.VMEM((HEADS, HEAD_W, HEAD_W), F32), pltpu.VMEM((CHUNK, D), F32),
                        pltpu.VMEM((CHUNK, D), F32), pltpu.VMEM((CHUNK, D), F32), pltpu.VMEM((1, D), F32)],
        input_output_aliases={4: 0},
        compiler_params=pltpu.CompilerParams(dimension_semantics=("arbitrary",)),
    )(proj, lb_logits, d_o, s_start, dproj)


def _mid(xin, tgt, o, c0, proj, w3, ln_g, ln_b, gnorm_g, final_g):
    tm = TM_ELT

    def body(x_ref, t_ref, o_ref, c0_ref, z_ref, gr_ref, mc_ref, mr_ref, w_ref, lng_ref, lnb_ref, gng_ref, fg_ref,
             dres_ref, do_ref, dc0_ref, dz_ref, dp_ref, a3_ref, b3_ref, red_ref, on_sc, don_sc):
        i = pl.program_id(0)

        @pl.when(i == 0)
        def _():
            red_ref[...] = jnp.zeros_like(red_ref)

        w_conv, w_rec, w_out = w_ref[0], w_ref[1], w_ref[2]
        c0v = c0_ref[...]
        mu = jnp.mean(c0v, axis=-1, keepdims=True)
        xc = c0v - mu
        rstd = lax.rsqrt(jnp.mean(xc * xc, axis=-1, keepdims=True) + EPS)
        xh = xc * rstd
        c1 = xh * lng_ref[...] + lnb_ref[...]
        s1 = _sigmoid(c1)
        c2 = c1 * s1
        z = z_ref[...].astype(F32)
        sz = _sigmoid(z)
        silu_z = z * sz
        u_conv = (c2 * silu_z).astype(BF16)
        y_conv = _dot(u_conv, w_conv)
        ov = o_ref[...]
        r3 = []
        for h in range(HEADS):
            sl = slice(h * HEAD_W, (h + 1) * HEAD_W)
            oh = ov[:, sl]
            r_h = lax.rsqrt(jnp.mean(oh * oh, axis=-1, keepdims=True) + EPS)
            r3.append(r_h)
            on_sc[:, sl] = oh * r_h
        o_n = on_sc[...]
        o_g = o_n * gng_ref[...]
        gr = gr_ref[...].astype(F32)
        sgr = _sigmoid(gr)
        silu_g = gr * sgr
        u_rec = (o_g * silu_g).astype(BF16)
        y_rec = _dot(u_rec, w_rec)
        mc = mc_ref[...].astype(F32)
        mr = mr_ref[...].astype(F32)
        smc = _sigmoid(mc)
        smr = _sigmoid(mr)
        merged = (smc * y_conv + smr * y_rec).astype(BF16)
        res = x_ref[...] + _dot(merged, w_out)
        r2 = lax.rsqrt(jnp.mean(res * res, axis=-1, keepdims=True) + EPS)
        xh2 = res * r2
        row = lax.broadcasted_iota(jnp.int32, (tm, 1), 0) + i * tm
        real = row >= ROW0
        diff = jnp.where(real, xh2 * fg_ref[...] - t_ref[...], 0.0)
        d_y = diff * (1.0 / D)
        d_xh2 = d_y * fg_ref[...]
        d_res = r2 * (d_xh2 - xh2 * jnp.mean(d_xh2 * xh2, axis=-1, keepdims=True))
        dres_ref[...] = d_res
        d_res_b = d_res.astype(BF16)
        d_merged = _dot_nt(d_res_b, w_out)
        d_yc = (d_merged * smc).astype(BF16)
        d_yr = (d_merged * smr).astype(BF16)
        dp_ref[:, D:2 * D] = (d_merged * y_conv * smc * (1.0 - smc)).astype(BF16)
        dp_ref[:, 2 * D:3 * D] = (d_merged * y_rec * smr * (1.0 - smr)).astype(BF16)
        d_ur = _dot_nt(d_yr, w_rec)
        d_og = d_ur * silu_g
        dp_ref[:, 0:D] = (d_ur * o_g * _dsilu(gr, sgr)).astype(BF16)
        d_on = d_og * gng_ref[...]
        for h in range(HEADS):
            sl = slice(h * HEAD_W, (h + 1) * HEAD_W)
            d_h = d_on[:, sl]
            n_h = o_n[:, sl]
            don_sc[:, sl] = r3[h] * (d_h - n_h * jnp.mean(d_h * n_h, axis=-1, keepdims=True))
        do_ref[...] = don_sc[...]
        d_uc = _dot_nt(d_yc, w_conv)
        d_c2 = d_uc * silu_z
        dz_ref[...] = (d_uc * c2 * _dsilu(z, sz)).astype(BF16)
        d_c1 = d_c2 * _dsilu(c1, s1)
        d_xh = d_c1 * lng_ref[...]
        d_c0 = rstd * (d_xh - jnp.mean(d_xh, axis=-1, keepdims=True)
                       - xh * jnp.mean(d_xh * xh, axis=-1, keepdims=True))
        dc0_ref[...] = d_c0
        a3_ref[0] = u_conv
        b3_ref[0] = d_yc
        a3_ref[1] = u_rec
        b3_ref[1] = d_yr
        a3_ref[2] = merged
        b3_ref[2] = d_res_b
        def colsum(vv):
            return jnp.sum(vv, axis=0, keepdims=True)

        red_ref[0:1, :] += colsum(d_y * xh2)
        red_ref[1:2, :] += colsum(d_og * o_n)
        red_ref[2:3, :] += colsum(d_c1 * xh)
        red_ref[3:4, :] += colsum(d_c1)
        red_ref[4:5, :] += colsum(d_c0)
        red_ref[5:6, :] += colsum(diff * diff) * (0.5 / D)

    def row_block(width, col):
        return pl.BlockSpec((tm, width), lambda i: (i, col))

    def const_block(shape):
        return pl.BlockSpec(shape, lambda i: (0,) * len(shape))

    stack = jax.ShapeDtypeStruct((3, TP, D), BF16)
    stack_spec = pl.BlockSpec((3, tm, D), lambda i: (0, i, 0))
    return pl.pallas_call(
        body, name="mid", grid=(TP // tm,),
        in_specs=[row_block(D, 0), row_block(D, 0), row_block(D, 0), row_block(D, 0),
                  row_block(D, 2), row_block(D, 6), row_block(D, 7), row_block(D, 8),
                  pl.BlockSpec((3, D, D), lambda i: (0, 0, 0), pipeline_mode=pl.Buffered(1)),
                  const_block((1, D)), const_block((1, D)), const_block((1, D)), const_block((1, D))],
        out_specs=(row_block(D, 0), row_block(D, 0), row_block(D, 0), row_block(D, 0), row_block(3 * D, 2),
                   stack_spec, stack_spec, const_block((8, D))),
        out_shape=(jax.ShapeDtypeStruct((TP, D), F32), jax.ShapeDtypeStruct((TP, D), F32),
                   jax.ShapeDtypeStruct((TP, D), F32), jax.ShapeDtypeStruct((TP, D), BF16),
                   jax.ShapeDtypeStruct((TP, D_IN), BF16), stack, stack, jax.ShapeDtypeStruct((8, D), F32)),
        scratch_shapes=[pltpu.VMEM((tm, D), F32), pltpu.VMEM((tm, D), F32)],
        compiler_params=pltpu.CompilerParams(dimension_semantics=("arbitrary",), vmem_limit_bytes=60 * 1024 * 1024),
    )(xin, tgt, o, c0, proj, proj, proj, proj, w3, ln_g, ln_b, gnorm_g, final_g)


def _conv_bwd(proj, d_c0, d_z, conv_w, dproj):
    tm = TM_ELT
    n_tile = TP // tm
    lastt = n_tile - 1

    strip = tm // CONV_STRIPS

    def body(p_ref, dc_ref, dz_ref, w_ref, dproj_in, dp_ref, dw_ref, dsh, a_sc, da_sc, acc):
        del dproj_in
        i = pl.program_id(0)

        @pl.when(i == 0)
        def _():
            dsh[0, tm:tm + HALO, :] = jnp.zeros((HALO, D), F32)
            acc[...] = jnp.zeros_like(acc)

        @pl.when(i > 0)
        def _():
            dsh[0, tm:tm + HALO, :] = dsh[0, 0:HALO, :]

        dsh[0, 0:tm, :] = dc_ref[...]
        _fill_shifts(dsh, tm)
        ga = p_ref[:, 0:D].astype(F32)
        sb = _sigmoid(p_ref[:, D:2 * D].astype(F32))
        a_sc[...] = ga * sb
        for cb in range(D // HEAD_W):
            cs = slice(cb * HEAD_W, (cb + 1) * HEAD_W)
            for st in range(CONV_STRIPS):
                rows = slice(st * strip, (st + 1) * strip)
                a_s = a_sc[rows, cs]
                d_a = jnp.zeros((strip, HEAD_W), F32)
                for j in range(CONV_K):
                    off = CONV_K - 1 - j
                    lo = st * strip + 8 * (off // 8)
                    slab = dsh[off % 8, lo:lo + strip, cs]
                    d_a = d_a + w_ref[j:j + 1, cs] * slab
                    acc[j, :, cs] += jnp.sum((a_s * slab).reshape(strip // 8, 8, HEAD_W), axis=0)
                da_sc[rows, cs] = d_a
        d_a = da_sc[...]
        dp_ref[:, 0:D] = (d_a * sb).astype(BF16)
        dp_ref[:, D:2 * D] = (d_a * a_sc[...] * (1.0 - sb)).astype(BF16)
        dp_ref[:, 2 * D:3 * D] = dz_ref[...]

        @pl.when(i == lastt)
        def _():
            for j in range(CONV_K):
                dw_ref[j:j + 1, :] = jnp.sum(acc[j], axis=0, keepdims=True)
            dw_ref[CONV_K:CONV_K + 1, :] = jnp.zeros((1, D), F32)

    return pl.pallas_call(
        body, name="conv_bwd", grid=(n_tile,),
        in_specs=[pl.BlockSpec((tm, 2 * D), lambda i: (lastt - i, 0)), pl.BlockSpec((tm, D), lambda i: (lastt - i, 0)),
                  pl.BlockSpec((tm, D), lambda i: (lastt - i, 0)), pl.BlockSpec((CONV_K, D), lambda i: (0, 0)), ANY],
        out_specs=(pl.BlockSpec((tm, 3 * D), lambda i: (lastt - i, 0)), pl.BlockSpec((CONV_K + 1, D), lambda i: (0, 0))),
        out_shape=(jax.ShapeDtypeStruct((TP, D_IN), BF16), jax.ShapeDtypeStruct((CONV_K + 1, D), F32)),
        scratch_shapes=[pltpu.VMEM((8, tm + HALO, D), F32), pltpu.VMEM((tm, D), F32), pltpu.VMEM((tm, D), F32),
                        pltpu.VMEM((CONV_K, 8, D), F32)],
        input_output_aliases={4: 0},
        compiler_params=pltpu.CompilerParams(dimension_semantics=("arbitrary",)),
    )(proj, d_c0, d_z, conv_w, dproj)


def _wgrad3(a3, b3):
    tt = TM_MAT

    def body(a_ref, b_ref, o_ref):
        @pl.when(pl.program_id(1) == 0)
        def _():
            o_ref[...] = jnp.zeros_like(o_ref)

        o_ref[0] += _dot_tn(a_ref[0], b_ref[0])

    return pl.pallas_call(
        body, name="wgrad3", grid=(3, TP // tt),
        in_specs=[pl.BlockSpec((1, tt, D), lambda g, t: (g, t, 0)), pl.BlockSpec((1, tt, D), lambda g, t: (g, t, 0))],
        out_specs=pl.BlockSpec((1, D, D), lambda g, t: (g, 0, 0)),
        out_shape=jax.ShapeDtypeStruct((3, D, D), F32),
        compiler_params=pltpu.CompilerParams(dimension_semantics=("arbitrary", "arbitrary")),
    )(a3, b3)


def _wgrad_in_sibling(h, dproj, ids_sib):
    tt = TM_MAT

    def body(ids_ref, a_ref, b_ref, o_ref):
        del ids_ref

        @pl.when(pl.program_id(1) == 0)
        def _():
            o_ref[...] = jnp.zeros_like(o_ref)

        o_ref[0] += _dot_tn(a_ref[...], b_ref[...])

    return pl.pallas_call(
        body, name="wgrad_in_sibling",
        grid_spec=pltpu.PrefetchScalarGridSpec(
            num_scalar_prefetch=1, grid=(4, TP // tt),
            in_specs=[pl.BlockSpec((tt, D), lambda r, t, ids: (t, 0)),
                      pl.BlockSpec((tt, W_IN_BLK), lambda r, t, ids: (t, ids[r]))],
            out_specs=pl.BlockSpec((1, D, W_IN_BLK), lambda r, t, ids: (r, 0, 0))),
        out_shape=jax.ShapeDtypeStruct((4, D, W_IN_BLK), F32),
        compiler_params=pltpu.CompilerParams(dimension_semantics=("arbitrary", "arbitrary")),
    )(ids_sib, h, dproj)


def _wgrad_in_mine(h, dproj, ids_mine, p_sib, p3):
    tt = TM_MAT
    n_t = TP // tt

    def body(ids_ref, a_ref, b_ref, ps_ref, p3_ref, o_ref, ob_ref, l0_ref, l1_ref, tmp, send_sems, recv_sems, tmp_sem):
        del ids_ref
        r = pl.program_id(0)
        t = pl.program_id(1)
        x, y, c = _my_place()
        sibling = (x, y, 1 - c)

        def to_sibling(a, q):
            if a == 0:
                src, dst = ps_ref.at[q], l0_ref.at[q]
            else:
                d = _dev_index(*_chip_rel(x, y, q), 1 - c)
                src = p3_ref.at[:, pl.ds(pl.multiple_of(d * W_ROW_BLK, W_ROW_BLK), W_ROW_BLK), :]
                dst = l1_ref.at[q]
            return pltpu.make_async_remote_copy(
                src_ref=src, dst_ref=dst, send_sem=send_sems.at[a * 4 + q], recv_sem=recv_sems.at[a * 4 + q],
                device_id=sibling, device_id_type=MESH_ID)

        @pl.when((r == 0) & (t == 0))
        def _():
            for q in range(4):
                for a in range(2):
                    to_sibling(a, q).start()

        @pl.when(t == 0)
        def _():
            o_ref[...] = jnp.zeros_like(o_ref)

        o_ref[0] += _dot_tn(a_ref[...], b_ref[...])
        landed = pltpu.make_async_copy(l0_ref.at[r], tmp, tmp_sem)

        @pl.when(t == n_t - 2)
        def _():
            to_sibling(0, r).wait_recv()
            landed.start()

        @pl.when(t == n_t - 1)
        def _():
            landed.wait()
            tot = o_ref[0] + tmp[...]
            o_ref[0] = tot
            ob_ref[0] = tot.astype(BF16)

        @pl.when((r == 3) & (t == n_t - 1))
        def _():
            for q in range(4):
                to_sibling(1, q).wait_recv()
            for q in range(4):
                for a in range(2):
                    to_sibling(a, q).wait_send()

    blk = pl.BlockSpec((1, D, W_IN_BLK), lambda r, t, ids: (r, 0, 0))
    return pl.pallas_call(
        body, name="wgrad_in_mine",
        grid_spec=pltpu.PrefetchScalarGridSpec(
            num_scalar_prefetch=1, grid=(4, n_t),
            in_specs=[pl.BlockSpec((tt, D), lambda r, t, ids: (t, 0)),
                      pl.BlockSpec((tt, W_IN_BLK), lambda r, t, ids: (t, ids[r])), ANY, ANY],
            out_specs=(blk, blk, ANY, ANY),
            scratch_shapes=[pltpu.VMEM((D, W_IN_BLK), F32), pltpu.SemaphoreType.DMA((8,)),
                            pltpu.SemaphoreType.DMA((8,)), pltpu.SemaphoreType.DMA]),
        out_shape=(jax.ShapeDtypeStruct((4, D, W_IN_BLK), F32), jax.ShapeDtypeStruct((4, D, W_IN_BLK), BF16),
                   jax.ShapeDtypeStruct((4, D, W_IN_BLK), F32), jax.ShapeDtypeStruct((4, 3, W_ROW_BLK, D), F32)),
        compiler_params=pltpu.CompilerParams(dimension_semantics=("arbitrary", "arbitrary")),
    )(ids_mine, h, dproj, p_sib, p3)


def _chip_sum_3(p3, land1, ids_mine):
    def body(ids_ref, p_ref, l_ref, o_ref, ob_ref):
        del ids_ref
        tot = p_ref[...] + l_ref[0]
        o_ref[0] = tot
        ob_ref[0] = tot.astype(BF16)

    blk = pl.BlockSpec((1, 3, W_ROW_BLK, D), lambda r, ids: (r, 0, 0, 0))
    return pl.pallas_call(
        body, name="chip_sum_3",
        grid_spec=pltpu.PrefetchScalarGridSpec(
            num_scalar_prefetch=1, grid=(4,),
            in_specs=[pl.BlockSpec((3, W_ROW_BLK, D), lambda r, ids: (0, ids[r], 0)), blk],
            out_specs=(blk, blk)),
        out_shape=(jax.ShapeDtypeStruct((4, 3, W_ROW_BLK, D), F32), jax.ShapeDtypeStruct((4, 3, W_ROW_BLK, D), BF16)),
    )(ids_mine, p3, land1)


def _dh_and_norm_bwd(dproj, w_in_full, xin, d_res, norm_g, chip0b, chip1b):
    tm = TM_MAT
    n_k = N_DEV
    n_m = TP // tm

    def body(dp_ref, w_ref, x_ref, dr_ref, g_ref, c0_ref, c1_ref, dx_ref, dg_ref, f0_ref, f1_ref, acc,
             send_sems, recv_sems):
        m = pl.program_id(0)
        k = pl.program_id(1)
        x, y, c = _my_place()

        def to_owner(a, q):
            src, dst = ((c0_ref, f0_ref), (c1_ref, f1_ref))[a]
            return pltpu.make_async_remote_copy(
                src_ref=src.at[q], dst_ref=dst.at[q - 1], send_sem=send_sems.at[a * 3 + q - 1],
                recv_sem=recv_sems.at[a * 3 + q - 1], device_id=(*_chip_rel(x, y, q), c), device_id_type=MESH_ID)

        @pl.when((m == 0) & (k == 0))
        def _():
            for q in range(1, 4):
                for a in range(2):
                    to_owner(a, q).start()

        @pl.when(k == 0)
        def _():
            acc[...] = jnp.zeros_like(acc)

        acc[...] += _dot_nt(dp_ref[...], w_ref[0])

        @pl.when((k == n_k - 1) & (m == 0))
        def _():
            dg_ref[...] = jnp.zeros_like(dg_ref)

        @pl.when(k == n_k - 1)
        def _():
            xv = x_ref[...]
            r1 = lax.rsqrt(jnp.mean(xv * xv, axis=-1, keepdims=True) + EPS)
            xh = xv * r1
            d_h = acc[...]
            dg_ref[0:1, :] += jnp.sum(d_h * xh, axis=0, keepdims=True)
            d_xh = d_h * g_ref[...]
            dx_ref[...] = dr_ref[...] + r1 * (d_xh - xh * jnp.mean(d_xh * xh, axis=-1, keepdims=True))

        @pl.when((m == n_m - 1) & (k == n_k - 1))
        def _():
            for q in range(1, 4):
                for a in range(2):
                    to_owner(a, q).wait_recv()
            for q in range(1, 4):
                for a in range(2):
                    to_owner(a, q).wait_send()

    return pl.pallas_call(
        body, name="dh_norm_bwd", grid=(n_m, n_k),
        in_specs=[pl.BlockSpec((tm, W_IN_BLK), lambda m, k: (m, k)), pl.BlockSpec((1, D, W_IN_BLK), lambda m, k: (k, 0, 0)),
                  pl.BlockSpec((tm, D), lambda m, k: (m, 0)), pl.BlockSpec((tm, D), lambda m, k: (m, 0)),
                  pl.BlockSpec((1, D), lambda m, k: (0, 0)), ANY, ANY],
        out_specs=(pl.BlockSpec((tm, D), lambda m, k: (m, 0)), pl.BlockSpec((8, D), lambda m, k: (0, 0)), ANY, ANY),
        out_shape=(jax.ShapeDtypeStruct((TP, D), F32), jax.ShapeDtypeStruct((8, D), F32),
                   jax.ShapeDtypeStruct((3, D, W_IN_BLK), BF16), jax.ShapeDtypeStruct((3, 3, W_ROW_BLK, D), BF16)),
        scratch_shapes=[pltpu.VMEM((tm, D), F32), pltpu.SemaphoreType.DMA((6,)), pltpu.SemaphoreType.DMA((6,))],
        compiler_params=pltpu.CompilerParams(dimension_semantics=("arbitrary", "arbitrary")),
    )(dproj, w_in_full, xin, d_res, norm_g, chip0b, chip1b)


def _sum_adamw(own, landed, w, m, v, tr, name):
    rows, cols = w.shape
    n_t = rows // tr

    def body(o_ref, l1_ref, l2_ref, l3_ref, w_ref, m_ref, v_ref, g_ref, d_ref, m2_ref, v2_ref):
        g = ((o_ref[...] + l1_ref[...].astype(F32)) + l2_ref[...].astype(F32)) + l3_ref[...].astype(F32)
        delta, m2, v2 = _adamw(w_ref[...], g, m_ref[...], v_ref[...])
        g_ref[...] = g
        d_ref[...] = delta
        m2_ref[...] = m2
        v2_ref[...] = v2

    def spec(k):
        return pl.BlockSpec((tr, cols), lambda i: (i + k * n_t, 0))

    out = jax.ShapeDtypeStruct((rows, cols), F32)
    return pl.pallas_call(
        body, name=name, grid=(n_t,),
        in_specs=[spec(0), spec(0), spec(1), spec(2), spec(0), spec(0), spec(0)],
        out_specs=(spec(0),) * 4, out_shape=(out,) * 4,
    )(own, landed, landed, landed, w, m, v)


def _small_update(pack_all, srs_all, lb_logits, p8, m8, v8, ws, ms, vs):
    def body(pk_ref, sr_ref, lbl_ref, p_ref, m_ref, v_ref, ws_ref, ms_ref, vs_ref,
             g8_ref, d8_ref, m8_ref, v8_ref, loss_ref, gs_ref, ds_ref, ms2_ref, vs2_ref):
        tot = pk_ref[0]
        tot_s = sr_ref[0]
        for d in range(1, N_DEV):
            tot = tot + pk_ref[d]
            tot_s = tot_s + sr_ref[d]
        p0 = _sigmoid(lbl_ref[0:1, :] - lbl_ref[1:2, :])
        row = lax.broadcasted_iota(jnp.int32, (8, D), 0)
        d_lb = jnp.sum(jnp.where(row == 4, tot, 0.0), axis=0, keepdims=True)
        d_l0 = d_lb * p0 * (1.0 - p0)
        loss_ref[...] = jnp.sum(jnp.where(row == 5, tot, 0.0), keepdims=True).reshape(1, 1)
        g8 = jnp.where(row == 4, d_l0, jnp.where(row == 5, -d_l0, tot))
        delta, m2, v2 = _adamw(p_ref[...], g8, m_ref[...], v_ref[...])
        g8_ref[...] = g8
        d8_ref[...] = delta
        m8_ref[...] = m2
        v8_ref[...] = v2
        delta, m2, v2 = _adamw(ws_ref[...], tot_s, ms_ref[...], vs_ref[...])
        gs_ref[...] = tot_s
        ds_ref[...] = delta
        ms2_ref[...] = m2
        vs2_ref[...] = v2

    o8 = jax.ShapeDtypeStruct((8, D), F32)
    os_ = jax.ShapeDtypeStruct((SMALL_ROWS, HEAD_W), F32)
    return pl.pallas_call(
        body, name="small_update",
        out_shape=(o8, o8, o8, o8, jax.ShapeDtypeStruct((1, 1), F32), os_, os_, os_, os_),
    )(pack_all, srs_all, lb_logits, p8, m8, v8, ws, ms, vs)


def _local_step(xin, proj, target, conv_w_full, conv_b, ln_g, ln_b, w3_full, lb_logits, gnorm_g, final_g):
    tgt = jnp.concatenate([jnp.zeros((ROW0, D), F32), target], axis=0)
    fg = final_g.reshape(1, D)
    c0 = _conv_fwd(proj, conv_w_full, conv_b)
    o, s_start = _rec_fwd(proj, lb_logits)
    d_res, d_o, d_c0, d_z, dproj, a3, b3, red = _mid(xin, tgt, o, c0, proj, w3_full, ln_g, ln_b, gnorm_g, fg)
    dproj, dlb = _rec_bwd(proj, lb_logits, d_o, s_start, dproj)
    dproj, d_conv_w = _conv_bwd(proj, d_c0, d_z, conv_w_full, dproj)
    p3 = _wgrad3(a3, b3)
    return dproj, d_res, p3, d_conv_w, red, dlb


def kernel(x, meta_tokens, norm_g, w_in, conv_w, conv_b, ln_g, ln_b, w_conv_out, lb_logits, gnorm_g, w_rec_out, w_out, final_g, loss_target, m_meta_tokens, m_norm_g, m_w_in, m_conv_w, m_conv_b, m_ln_g, m_ln_b, m_w_conv_out, m_lb_logits, m_gnorm_g, m_w_rec_out, m_w_out, m_final_g, v_meta_tokens, v_norm_g, v_w_in, v_conv_w, v_conv_b, v_ln_g, v_ln_b, v_w_conv_out, v_lb_logits, v_gnorm_g, v_w_rec_out, v_w_out, v_final_g):
    def small_pack(cw, mt):
        return jnp.concatenate([cw[0], jnp.zeros((1, HEAD_W), F32), mt], axis=0)

    def stack3(a, b, c):
        return jnp.concatenate([a, b, c], axis=0)

    def stack8(ng, cb, lg, lb_, lbl, gg, fg):
        return jnp.concatenate([ng, cb, lg, lb_, lbl, gg, fg.reshape(1, D)], axis=0)

    mx, my, mc = _my_place()

    w3_s = stack3(w_conv_out, w_rec_out, w_out)
    ws_s = small_pack(conv_w, meta_tokens)
    small_full = jnp.transpose(_gather_small(ws_s), (1, 0, 2)).reshape(SMALL_ROWS, D)
    conv_w_full = small_full[0:CONV_K]
    meta_full = small_full[META_ROW:META_ROW + N_META]
    xin = jnp.concatenate([jnp.zeros((PAD_FRONT, D), F32), meta_full, x[0]], axis=0)
    w_in_b, w3_b = _cast_shards(w_in[0], w3_s)
    use_order = [(mx, my, mc), (mx, my, 1 - mc)]
    for chip in ((1 - mx, my), (mx, 1 - my), (1 - mx, 1 - my)):
        use_order += [(*chip, mc), (*chip, 1 - mc)]
    order = jnp.stack([_dev_index(*p) for p in use_order]).astype(jnp.int32)
    proj, h, w_in_full, w3_full = _gather_and_proj(xin, norm_g, w_in_b, w3_b, order)
    h = h.reshape(TP, D)

    dproj, d_res, p3, d_conv_w, red, dlb = _local_step(
        xin, proj, loss_target[0], conv_w_full, conv_b, ln_g, ln_b, w3_full, lb_logits, gnorm_g, final_g)

    ids_mine = jnp.stack([_dev_index(*_chip_rel(mx, my, r), mc) for r in range(4)]).astype(jnp.int32)
    ids_sib = jnp.stack([_dev_index(*_chip_rel(mx, my, r), 1 - mc) for r in range(4)]).astype(jnp.int32)
    p_sib = _wgrad_in_sibling(h, dproj, ids_sib)
    chip0, chip0b, _, land1 = _wgrad_in_mine(h, dproj, ids_mine, p_sib, p3)
    chip1, chip1b = _chip_sum_3(p3, land1, ids_mine)
    d_xin, dng, far0, far1 = _dh_and_norm_bwd(dproj, w_in_full, xin, d_res, norm_g, chip0b, chip1b)
    pack = jnp.concatenate([dng[0:1], red[4:5], red[2:3], red[3:4], dlb[0:1], red[5:6], red[1:2], red[0:1]], axis=0)
    g_in, d_in, m_in, v_in = _sum_adamw(chip0.reshape(4 * D, W_IN_BLK), far0.reshape(3 * D, W_IN_BLK), w_in[0],
                                        m_w_in[0], v_w_in[0], 256, "adamw_in")
    g_3, d_3, m_3, v_3 = _sum_adamw(
        chip1.reshape(12 * W_ROW_BLK, D), far1.reshape(9 * W_ROW_BLK, D), w3_s.reshape(3 * W_ROW_BLK, D),
        stack3(m_w_conv_out, m_w_rec_out, m_w_out).reshape(3 * W_ROW_BLK, D),
        stack3(v_w_conv_out, v_w_rec_out, v_w_out).reshape(3 * W_ROW_BLK, D), 3 * W_ROW_BLK, "adamw_3")

    srs = jnp.concatenate([d_conv_w, d_xin[PAD_FRONT:ROW0]], axis=0)
    srs = jnp.transpose(srs.reshape(SMALL_ROWS, N_DEV, HEAD_W), (1, 0, 2))
    pack_all, srs_all = _exchange_small(pack, srs)
    g8, d8, m8, v8, loss, gs, ds, ms, vs = _small_update(
        pack_all, srs_all, lb_logits,
        stack8(norm_g, conv_b, ln_g, ln_b, lb_logits, gnorm_g, final_g),
        stack8(m_norm_g, m_conv_b, m_ln_g, m_ln_b, m_lb_logits, m_gnorm_g, m_final_g),
        stack8(v_norm_g, v_conv_b, v_ln_g, v_ln_b, v_lb_logits, v_gnorm_g, v_final_g),
        ws_s, small_pack(m_conv_w, m_meta_tokens), small_pack(v_conv_w, v_meta_tokens))

    def unpack(a_in, a_3, a_s, a_8):
        t3 = a_3.reshape(3, 1, W_ROW_BLK, D)
        return (a_s[META_ROW:META_ROW + N_META], a_8[0:1], a_in[None], a_s[0:CONV_K][None], a_8[1:2], a_8[2:3],
                a_8[3:4], t3[0], a_8[4:6], a_8[6:7], t3[1], t3[2], a_8[7])

    grad_x = d_xin[ROW0:][None]
    return (loss.reshape(()), grad_x, *unpack(g_in, g_3, gs, g8), *unpack(d_in, d_3, ds, d8),
            *unpack(m_in, m_3, ms, m8), *unpack(v_in, v_3, vs, v8))
```

```python
import functools

import jax
import jax.numpy as jnp
from jax import lax
from jax.experimental import pallas as pl
from jax.experimental.pallas import tpu as pltpu

F32 = jnp.float32
BF16 = jnp.bfloat16

D = 1024
SEQ = 4096
N_META = 16
CHUNK = 64
PAD_FRONT = 48
ROW0 = PAD_FRONT + N_META
TP = ROW0 + SEQ
N_CHUNK = TP // CHUNK
HEADS = 8
HEAD_W = 128
D_IN = 9 * D
N_DEV = 8
W_IN_BLK = D_IN // N_DEV
W_ROW_BLK = D // N_DEV
CONV_K = 31
SMALL_ROWS = 48
META_ROW = 32
EPS = 1e-6
HALO = 32

TM_MAT = 832
TM_ELT = 208
CONV_STRIPS = 2

ADAM_LR = 0.001
ADAM_B1 = 0.9
ADAM_B2 = 0.999
ADAM_EPS = 1e-08
ADAM_WD = 0.01
ADAM_STEP = 10

MESH_ID = pl.DeviceIdType.MESH
ANY = pl.BlockSpec(memory_space=pl.ANY)


def _sigmoid(v):
    return jax.nn.sigmoid(v)


def _dsilu(v, s):
    return s * (1.0 + v * (1.0 - s))


def _dot(a, b):
    return jnp.dot(a, b, preferred_element_type=F32)


def _dot_nt(a, b):
    return lax.dot_general(a, b, (((1,), (1,)), ((), ())), preferred_element_type=F32)


def _dot_tn(a, b):
    return lax.dot_general(a, b, (((0,), (0,)), ((), ())), preferred_element_type=F32)


def _split3(v):
    hi = v.astype(BF16)
    r1 = v - hi.astype(F32)
    mid = r1.astype(BF16)
    lo = (r1 - mid.astype(F32)).astype(BF16)
    return hi, mid, lo


def _tri_matmul(tri, v):
    hi, mid, lo = _split3(v)
    return _dot(tri, hi) + _dot(tri, mid) + _dot(tri, lo)


def _adamw(w, g, m, v):
    m2 = ADAM_B1 * m + (1.0 - ADAM_B1) * g
    v2 = ADAM_B2 * v + (1.0 - ADAM_B2) * jnp.square(g)
    m_hat = m2 / (1.0 - ADAM_B1 ** ADAM_STEP)
    v_hat = v2 / (1.0 - ADAM_B2 ** ADAM_STEP)
    delta = -ADAM_LR * (m_hat / (jnp.sqrt(v_hat) + ADAM_EPS) + ADAM_WD * w)
    return delta, m2, v2


def _my_place():
    return lax.axis_index("x"), lax.axis_index("y"), lax.axis_index("c")


def _dev_index(px, py, pc):
    return 4 * px + 2 * py + pc


def _cast_shards(w_in_s, w3_s):
    def body(a_ref, b_ref, oa_ref, ob_ref):
        oa_ref[...] = a_ref[...].astype(BF16)
        ob_ref[...] = b_ref[...].astype(BF16)

    return pl.pallas_call(
        body, name="cast_shards",
        out_shape=(jax.ShapeDtypeStruct(w_in_s.shape, BF16), jax.ShapeDtypeStruct(w3_s.shape, BF16)),
    )(w_in_s, w3_s)


def _peer(x, y, c, r):
    return (jnp.bitwise_xor(x, (r >> 2) & 1), jnp.bitwise_xor(y, (r >> 1) & 1), jnp.bitwise_xor(c, r & 1))


def _gather_small(small_s):
    def body(s_ref, o_ref, send_sems, recv_sems, local_sem):
        x, y, c = _my_place()
        my_id = _dev_index(x, y, c)
        mine = pltpu.make_async_copy(s_ref, o_ref.at[my_id], local_sem)
        mine.start()
        copies = []
        for r in range(1, N_DEV):
            cp = pltpu.make_async_remote_copy(
                src_ref=s_ref, dst_ref=o_ref.at[my_id], send_sem=send_sems.at[r - 1], recv_sem=recv_sems.at[r - 1],
                device_id=_peer(x, y, c, r), device_id_type=MESH_ID)
            cp.start()
            copies.append(cp)
        for cp in copies:
            cp.wait_recv()
        for cp in copies:
            cp.wait_send()
        mine.wait()

    return pl.pallas_call(
        body, name="gather_small", out_shape=jax.ShapeDtypeStruct((N_DEV,) + small_s.shape, F32),
        in_specs=[ANY], out_specs=ANY,
        scratch_shapes=[pltpu.SemaphoreType.DMA((7,)), pltpu.SemaphoreType.DMA((7,)), pltpu.SemaphoreType.DMA],
    )(small_s)


def _gather_and_proj(xin, norm_g, w_in_b, w3_b, order):
    tm = TM_MAT
    n_m = TP // tm
    last_m = n_m - 1

    def body(order_ref, x_ref, g_ref, s0, s1, proj_ref, h_out, o0, o1, hbuf, wbuf, w3buf, send_sems, recv_sems, local_sems):
        del order_ref
        n = pl.program_id(0)
        m = pl.program_id(1)
        x, y, c = _my_place()
        me, sibling = (x, y, c), (x, y, 1 - c)
        chips = [(1 - x, y), (x, 1 - y), (1 - x, 1 - y)]
        srcs, outs = (s0, s1), (o0, o1)

        def block(a, place):
            d = _dev_index(*place)
            if a == 1:
                return o1.at[:, pl.ds(pl.multiple_of(d * W_ROW_BLK, W_ROW_BLK), W_ROW_BLK), :]
            return o0.at[d]

        def copy(a, k, place, to, from_src=False):
            return pltpu.make_async_remote_copy(
                src_ref=srcs[a] if from_src else block(a, place), dst_ref=block(a, place),
                send_sem=send_sems.at[a * 7 + k], recv_sem=recv_sems.at[a * 7 + k],
                device_id=to, device_id_type=MESH_ID)

        def to_vmem(place, slot):
            return pltpu.make_async_copy(block(0, place), wbuf.at[slot], local_sems.at[slot])

        own_out = [pltpu.make_async_copy(wbuf.at[0], block(0, me), local_sems.at[2]),
                   pltpu.make_async_copy(w3buf, block(1, me), local_sems.at[3])]
        h_copy = pltpu.make_async_copy(hbuf, h_out, local_sems.at[4])

        @pl.when((n == 0) & (m == 0))
        def _():
            for a in range(2):
                copy(a, 0, me, sibling, from_src=True).start()
                for j, chip in enumerate(chips):
                    copy(a, 1 + j, me, (*chip, c), from_src=True).start()
            mine = [pltpu.make_async_copy(s0, wbuf.at[0], local_sems.at[0]),
                    pltpu.make_async_copy(s1, w3buf, local_sems.at[1])]
            for cp in mine:
                cp.start()
            for cp in mine:
                cp.wait()
            for cp in own_out:
                cp.start()

        @pl.when(n == 0)
        def _():
            xv = x_ref[...]
            r = lax.rsqrt(jnp.mean(xv * xv, axis=-1, keepdims=True) + EPS)
            hbuf[m] = (xv * r * g_ref[...]).astype(BF16)

        plan = [(sibling, (0, sibling), None)]
        for j, chip in enumerate(chips):
            plan.append(((*chip, c), (1 + j, (*chip, c)), 4 + j))
            plan.append(((*chip, 1 - c), (4 + j, (*chip, 1 - c)), None))

        for s, (place, (k, origin), pass_on) in enumerate(plan, start=1):
            @pl.when((n == s - 1) & (m == last_m))
            def _(s=s, place=place, k=k, origin=origin, pass_on=pass_on):
                copy(0, k, origin, me).wait_recv()
                if pass_on is not None:
                    copy(0, pass_on, place, sibling).start()
                if s == 2:
                    own_out[0].wait()
                to_vmem(place, s % 2).start()

            @pl.when((n == s) & (m == 0))
            def _(s=s, place=place):
                to_vmem(place, s % 2).wait()

        proj_ref[...] = _dot(hbuf[m], wbuf[lax.rem(n, 2)]).astype(BF16)

        @pl.when((n == 0) & (m == last_m))
        def _():
            h_copy.start()

        @pl.when((n == N_DEV - 1) & (m == last_m))
        def _():
            for j, chip in enumerate(chips):
                copy(1, 1 + j, (*chip, c), me).wait_recv()
                copy(1, 4 + j, (*chip, c), sibling).start()
            copy(1, 0, sibling, me).wait_recv()
            for j, chip in enumerate(chips):
                copy(1, 4 + j, (*chip, 1 - c), me).wait_recv()
            for a in range(2):
                for k in range(7):
                    copy(a, k, me, me).wait_send()
            own_out[1].wait()
            h_copy.wait()

    return pl.pallas_call(
        body, name="gather_and_proj",
        grid_spec=pltpu.PrefetchScalarGridSpec(
            num_scalar_prefetch=1, grid=(N_DEV, n_m),
            in_specs=[pl.BlockSpec((tm, D), lambda n, m, o: (jnp.where(n == 0, m, 0), 0)),
                      pl.BlockSpec((1, D), lambda n, m, o: (0, 0)), ANY, ANY],
            out_specs=(pl.BlockSpec((tm, W_IN_BLK), lambda n, m, o: (m, o[n])), ANY, ANY, ANY),
            scratch_shapes=[pltpu.VMEM((n_m, tm, D), BF16), pltpu.VMEM((2, D, W_IN_BLK), BF16),
                            pltpu.VMEM((3, W_ROW_BLK, D), BF16), pltpu.SemaphoreType.DMA((14,)),
                            pltpu.SemaphoreType.DMA((14,)), pltpu.SemaphoreType.DMA((5,))]),
        out_shape=(jax.ShapeDtypeStruct((TP, D_IN), BF16), jax.ShapeDtypeStruct((n_m, tm, D), BF16),
                   jax.ShapeDtypeStruct((N_DEV, D, W_IN_BLK), BF16), jax.ShapeDtypeStruct((3, D, D), BF16)),
        compiler_params=pltpu.CompilerParams(dimension_semantics=("arbitrary", "arbitrary")),
    )(order, xin, norm_g, w_in_b, w3_b)


def _chip_rel(x, y, r):
    return (jnp.bitwise_xor(x, r >> 1), jnp.bitwise_xor(y, r & 1))


def _exchange_small(pack, srs):
    def body(pk, sr, pk_all, sr_all, send_sems, recv_sems, local_sems):
        x, y, c = _my_place()
        my_id = _dev_index(x, y, c)
        mine = [pltpu.make_async_copy(pk, pk_all.at[my_id], local_sems.at[0]),
                pltpu.make_async_copy(sr.at[my_id], sr_all.at[my_id], local_sems.at[1])]
        for cp in mine:
            cp.start()
        copies = []
        for r in range(1, N_DEV):
            peer = (jnp.bitwise_xor(x, (r >> 2) & 1), jnp.bitwise_xor(y, (r >> 1) & 1), jnp.bitwise_xor(c, r & 1))
            peer_id = _dev_index(*peer)
            for a, (src, dst) in enumerate(((pk, pk_all.at[my_id]), (sr.at[peer_id], sr_all.at[my_id]))):
                cp = pltpu.make_async_remote_copy(
                    src_ref=src, dst_ref=dst, send_sem=send_sems.at[a * 7 + r - 1], recv_sem=recv_sems.at[a * 7 + r - 1],
                    device_id=peer, device_id_type=MESH_ID)
                cp.start()
                copies.append(cp)
        for cp in copies:
            cp.wait_recv()
        for cp in copies:
            cp.wait_send()
        for cp in mine:
            cp.wait()

    return pl.pallas_call(
        body, name="exchange_small",
        out_shape=(jax.ShapeDtypeStruct((N_DEV,) + pack.shape, F32), jax.ShapeDtypeStruct(srs.shape, F32)),
        in_specs=[ANY, ANY], out_specs=(ANY, ANY),
        scratch_shapes=[pltpu.SemaphoreType.DMA((14,)), pltpu.SemaphoreType.DMA((14,)), pltpu.SemaphoreType.DMA((2,))],
    )(pack, srs)


N_CB = D // HEAD_W


def _store_by_cb(ref, idx, rows, val):
    for cb in range(N_CB):
        ref[(*idx, cb, rows, slice(None))] = val[:, cb * HEAD_W:(cb + 1) * HEAD_W]


def _fill_shifts(sh, tm):
    n = tm + HALO - 8
    for s in range(1, 8):
        for cb in range(N_CB):
            sh[s, cb, 0:n, :] = sh[0, cb, s:s + n, :]


def _conv_fwd(proj, conv_w, conv_b):
    tm = TM_ELT
    strip = tm // CONV_STRIPS

    def body(p_ref, w_ref, b_ref, c0_ref, sh):
        i = pl.program_id(0)

        @pl.when(i == 0)
        def _():
            sh[0, :, 0:HALO, :] = jnp.zeros((N_CB, HALO, HEAD_W), F32)

        @pl.when(i > 0)
        def _():
            sh[0, :, 0:HALO, :] = sh[0, :, tm:tm + HALO, :]

        ga = p_ref[:, 0:D].astype(F32)
        gb = p_ref[:, D:2 * D].astype(F32)
        _store_by_cb(sh, (0,), slice(HALO, HALO + tm), ga * _sigmoid(gb))
        _fill_shifts(sh, tm)
        for cb in range(N_CB):
            cs = slice(cb * HEAD_W, (cb + 1) * HEAD_W)
            for st in range(CONV_STRIPS):
                acc = jnp.broadcast_to(b_ref[:, cs], (strip, HEAD_W))
                for j in range(CONV_K):
                    off = HALO - (CONV_K - 1) + j
                    lo = st * strip + 8 * (off // 8)
                    acc = acc + w_ref[j:j + 1, cs] * sh[off % 8, cb, lo:lo + strip, :]
                c0_ref[st * strip:(st + 1) * strip, cs] = acc

    return pl.pallas_call(
        body, name="conv_fwd", grid=(TP // tm,),
        in_specs=[pl.BlockSpec((tm, 2 * D), lambda i: (i, 0)), pl.BlockSpec((CONV_K, D), lambda i: (0, 0)),
                  pl.BlockSpec((1, D), lambda i: (0, 0))],
        out_specs=pl.BlockSpec((tm, D), lambda i: (i, 0)),
        out_shape=jax.ShapeDtypeStruct((TP, D), F32),
        scratch_shapes=[pltpu.VMEM((8, N_CB, HALO + tm, HEAD_W), F32)],
        compiler_params=pltpu.CompilerParams(dimension_semantics=("arbitrary",)),
    )(proj, conv_w, conv_b)


def _gates(p_ref, lbl_ref, chunk, bsc):
    lb = _sigmoid(lbl_ref[0:1, :] - lbl_ref[1:2, :])
    q_raw = p_ref[:, 0:D].astype(F32)
    f_raw = p_ref[:, D:2 * D].astype(F32)
    sq = _sigmoid(q_raw)
    q = q_raw * sq
    sg = _sigmoid(f_raw)
    f = lb + (1.0 - lb) * sg
    row = lax.broadcasted_iota(jnp.int32, (CHUNK, 1), 0) + chunk * CHUNK
    valid = row >= PAD_FRONT
    lf = jnp.where(valid, jnp.log(f), 0.0)
    k = jnp.where(valid, 1.0 - f, 0.0)
    r_i = lax.broadcasted_iota(jnp.int32, (CHUNK, CHUNK), 0)
    c_i = lax.broadcasted_iota(jnp.int32, (CHUNK, CHUNK), 1)
    causal = r_i >= c_i
    bsc[...] = _tri_matmul(causal.astype(BF16), lf)
    b = bsc[...]
    b_mid = bsc[CHUNK // 2 - 1:CHUNK // 2, :]
    b_last = bsc[CHUNK - 1:CHUNK, :]
    e_q = jnp.exp(b)
    e_qm = jnp.exp(b - b_mid)
    e_km = jnp.exp(b_mid - b)
    e_kh = jnp.exp(b_last - b)
    e_last = jnp.exp(b_last)
    return dict(lb=lb, q_raw=q_raw, sq=sq, q=q, sg=sg, f=f, k=k, valid=valid, causal=causal,
                e_q=e_q, e_qm=e_qm, e_km=e_km, e_kh=e_kh, e_last=e_last)


def _rec_fwd(proj, lb_logits):
    def body(p_ref, lbl_ref, o_ref, s_ref, st, bsc):
        n = pl.program_id(0)

        @pl.when(n == 0)
        def _():
            st[...] = jnp.zeros_like(st)

        s_ref[0] = st[...]
        g = _gates(p_ref, lbl_ref, n, bsc)
        q1 = (g["q"] * g["e_q"]).astype(BF16)
        qm = (g["q"] * g["e_qm"]).astype(BF16)
        km = (g["k"] * g["e_km"]).astype(BF16)
        kh = (g["k"] * g["e_kh"]).astype(BF16)
        for h in range(HEADS):
            sl = slice(h * HEAD_W, (h + 1) * HEAD_W)
            v = p_ref[:, 2 * D + h * HEAD_W:2 * D + (h + 1) * HEAD_W]
            att = jnp.where(g["causal"], _dot_nt(qm[:, sl], km[:, sl]), 0.0).astype(BF16)
            s_h = st[h]
            o_ref[:, sl] = _dot_nt(q1[:, sl], s_h.astype(BF16)) + _dot(att, v)
            st[h] = s_h * g["e_last"][:, sl] + _dot_tn(v, kh[:, sl])

    return pl.pallas_call(
        body, name="rec_fwd", grid=(N_CHUNK,),
        in_specs=[pl.BlockSpec((CHUNK, 3 * D), lambda n: (n, 1)), pl.BlockSpec((2, D), lambda n: (0, 0))],
        out_specs=(pl.BlockSpec((CHUNK, D), lambda n: (n, 0)),
                   pl.BlockSpec((1, HEADS, HEAD_W, HEAD_W), lambda n: (n, 0, 0, 0))),
        out_shape=(jax.ShapeDtypeStruct((TP, D), F32), jax.ShapeDtypeStruct((N_CHUNK, HEADS, HEAD_W, HEAD_W), F32)),
        scratch_shapes=[pltpu.VMEM((HEADS, HEAD_W, HEAD_W), F32), pltpu.VMEM((CHUNK, D), F32)],
        compiler_params=pltpu.CompilerParams(dimension_semantics=("arbitrary",)),
    )(proj, lb_logits)


def _rec_bwd(proj, lb_logits, d_o, s_start, dproj):
    last = N_CHUNK - 1

    def body(p_ref, lbl_ref, do_ref, s_ref, dproj_in, dp_ref, dlb_ref, dst, bsc, dq_sc, dk_sc, g_sc):
        del dproj_in
        n = pl.program_id(0)

        @pl.when(n == 0)
        def _():
            dst[...] = jnp.zeros_like(dst)
            dlb_ref[...] = jnp.zeros_like(dlb_ref)

        g = _gates(p_ref, lbl_ref, last - n, bsc)
        q1 = (g["q"] * g["e_q"]).astype(BF16)
        qm_f = g["q"] * g["e_qm"]
        km_f = g["k"] * g["e_km"]
        qm = qm_f.astype(BF16)
        km = km_f.astype(BF16)
        qm_lo = (qm_f - qm.astype(F32)).astype(BF16)
        km_lo = (km_f - km.astype(F32)).astype(BF16)
        kh_f = g["k"] * g["e_kh"]
        kh = kh_f.astype(BF16)
        for h in range(HEADS):
            sl = slice(h * HEAD_W, (h + 1) * HEAD_W)
            v = p_ref[:, 2 * D + h * HEAD_W:2 * D + (h + 1) * HEAD_W]
            d_oh = do_ref[:, sl].astype(BF16)
            s0 = s_ref[0, h]
            ds_end = dst[h]
            ds_end_b = ds_end.astype(BF16)
            att = jnp.where(g["causal"], _dot_nt(qm[:, sl], km[:, sl]), 0.0).astype(BF16)
            d_att = jnp.where(g["causal"], _dot_nt(d_oh, v), 0.0).astype(BF16)
            d_v = _dot_tn(att, d_oh) + _dot_nt(kh[:, sl], ds_end_b)
            d_qm = _dot(d_att, km[:, sl]) + _dot(d_att, km_lo[:, sl])
            d_q1 = _dot(d_oh, s0.astype(BF16))
            d_km = _dot_tn(d_att, qm[:, sl]) + _dot_tn(d_att, qm_lo[:, sl])
            d_kh = _dot(v, ds_end_b)
            dq_sc[:, sl] = d_qm * g["e_qm"][:, sl] + d_q1 * g["e_q"][:, sl]
            dk_sc[:, sl] = d_km * g["e_km"][:, sl] + d_kh * g["e_kh"][:, sl]
            g_sc[:, sl] = (jnp.sum(kh_f[:, sl] * d_kh, axis=0, keepdims=True)
                           + g["e_last"][:, sl] * jnp.sum(ds_end * s0, axis=0, keepdims=True))
            dst[h] = ds_end * g["e_last"][:, sl] + _dot_tn(d_oh, q1[:, sl])
            dp_ref[:, 2 * D + h * HEAD_W:2 * D + (h + 1) * HEAD_W] = d_v.astype(BF16)
        d_q = dq_sc[...]
        d_k = dk_sc[...]
        d_b = g["q"] * d_q - g["k"] * d_k
        anti = jnp.logical_not(g["causal"]) | (lax.broadcasted_iota(jnp.int32, (CHUNK, CHUNK), 0)
                                                == lax.broadcasted_iota(jnp.int32, (CHUNK, CHUNK), 1))
        d_lf = _tri_matmul(anti.astype(BF16), d_b) + g_sc[...]
        d_f = jnp.where(g["valid"], d_lf / g["f"] - d_k, 0.0)
        sg = g["sg"]
        dlb_ref[0:1, :] += jnp.sum(d_f * (1.0 - sg), axis=0, keepdims=True)
        dp_ref[:, 0:D] = (d_q * _dsilu(g["q_raw"], g["sq"])).astype(BF16)
        dp_ref[:, D:2 * D] = (d_f * (1.0 - g["lb"]) * sg * (1.0 - sg)).astype(BF16)

    return pl.pallas_call(
        body, name="rec_bwd", grid=(N_CHUNK,),
        in_specs=[pl.BlockSpec((CHUNK, 3 * D), lambda n: (last - n, 1)), pl.BlockSpec((2, D), lambda n: (0, 0)),
                  pl.BlockSpec((CHUNK, D), lambda n: (last - n, 0)),
                  pl.BlockSpec((1, HEADS, HEAD_W, HEAD_W), lambda n: (last - n, 0, 0, 0)), ANY],
        out_specs=(pl.BlockSpec((CHUNK, 3 * D), lambda n: (last - n, 1)), pl.BlockSpec((8, D), lambda n: (0, 0))),
        out_shape=(jax.ShapeDtypeStruct((TP, D_IN), BF16), jax.ShapeDtypeStruct((8, D), F32)),
        scratch_shapes=[pltpu.VMEM((HEADS, HEAD_W, HEAD_W), F32), pltpu.VMEM((CHUNK, D), F32),
                        pltpu.VMEM((CHUNK, D), F32), pltpu.VMEM((CHUNK, D), F32), pltpu.VMEM((1, D), F32)],
        input_output_aliases={4: 0},
        compiler_params=pltpu.CompilerParams(dimension_semantics=("arbitrary",)),
    )(proj, lb_logits, d_o, s_start, dproj)


def _mid(xin, tgt, o, c0, proj, w3, ln_g, ln_b, gnorm_g, final_g):
    tm = TM_ELT

    def body(x_ref, t_ref, o_ref, c0_ref, z_ref, gr_ref, mc_ref, mr_ref, w_ref, lng_ref, lnb_ref, gng_ref, fg_ref,
             dres_ref, do_ref, dc0_ref, dz_ref, dp_ref, a3_ref, b3_ref, red_ref, on_sc, don_sc):
        i = pl.program_id(0)

        @pl.when(i == 0)
        def _():
            red_ref[...] = jnp.zeros_like(red_ref)

        w_conv, w_rec, w_out = w_ref[0], w_ref[1], w_ref[2]
        c0v = c0_ref[...]
        mu = jnp.mean(c0v, axis=-1, keepdims=True)
        xc = c0v - mu
        rstd = lax.rsqrt(jnp.mean(xc * xc, axis=-1, keepdims=True) + EPS)
        xh = xc * rstd
        c1 = xh * lng_ref[...] + lnb_ref[...]
        s1 = _sigmoid(c1)
        c2 = c1 * s1
        z = z_ref[...].astype(F32)
        sz = _sigmoid(z)
        silu_z = z * sz
        u_conv = (c2 * silu_z).astype(BF16)
        y_conv = _dot(u_conv, w_conv)
        ov = o_ref[...]
        r3 = []
        for h in range(HEADS):
            sl = slice(h * HEAD_W, (h + 1) * HEAD_W)
            oh = ov[:, sl]
            r_h = lax.rsqrt(jnp.mean(oh * oh, axis=-1, keepdims=True) + EPS)
            r3.append(r_h)
            on_sc[:, sl] = oh * r_h
        o_n = on_sc[...]
        o_g = o_n * gng_ref[...]
        gr = gr_ref[...].astype(F32)
        sgr = _sigmoid(gr)
        silu_g = gr * sgr
        u_rec = (o_g * silu_g).astype(BF16)
        y_rec = _dot(u_rec, w_rec)
        mc = mc_ref[...].astype(F32)
        mr = mr_ref[...].astype(F32)
        smc = _sigmoid(mc)
        smr = _sigmoid(mr)
        merged = (smc * y_conv + smr * y_rec).astype(BF16)
        res = x_ref[...] + _dot(merged, w_out)
        r2 = lax.rsqrt(jnp.mean(res * res, axis=-1, keepdims=True) + EPS)
        xh2 = res * r2
        row = lax.broadcasted_iota(jnp.int32, (tm, 1), 0) + i * tm
        real = row >= ROW0
        diff = jnp.where(real, xh2 * fg_ref[...] - t_ref[...], 0.0)
        d_y = diff * (1.0 / D)
        d_xh2 = d_y * fg_ref[...]
        d_res = r2 * (d_xh2 - xh2 * jnp.mean(d_xh2 * xh2, axis=-1, keepdims=True))
        dres_ref[...] = d_res
        d_res_b = d_res.astype(BF16)
        d_merged = _dot_nt(d_res_b, w_out)
        d_yc = (d_merged * smc).astype(BF16)
        d_yr = (d_merged * smr).astype(BF16)
        dp_ref[:, D:2 * D] = (d_merged * y_conv * smc * (1.0 - smc)).astype(BF16)
        dp_ref[:, 2 * D:3 * D] = (d_merged * y_rec * smr * (1.0 - smr)).astype(BF16)
        d_ur = _dot_nt(d_yr, w_rec)
        d_og = d_ur * silu_g
        dp_ref[:, 0:D] = (d_ur * o_g * _dsilu(gr, sgr)).astype(BF16)
        d_on = d_og * gng_ref[...]
        for h in range(HEADS):
            sl = slice(h * HEAD_W, (h + 1) * HEAD_W)
            d_h = d_on[:, sl]
            n_h = o_n[:, sl]
            don_sc[:, sl] = r3[h] * (d_h - n_h * jnp.mean(d_h * n_h, axis=-1, keepdims=True))
        do_ref[...] = don_sc[...]
        d_uc = _dot_nt(d_yc, w_conv)
        d_c2 = d_uc * silu_z
        dz_ref[...] = (d_uc * c2 * _dsilu(z, sz)).astype(BF16)
        d_c1 = d_c2 * _dsilu(c1, s1)
        d_xh = d_c1 * lng_ref[...]
        d_c0 = rstd * (d_xh - jnp.mean(d_xh, axis=-1, keepdims=True)
                       - xh * jnp.mean(d_xh * xh, axis=-1, keepdims=True))
        dc0_ref[...] = d_c0
        a3_ref[0] = u_conv
        b3_ref[0] = d_yc
        a3_ref[1] = u_rec
        b3_ref[1] = d_yr
        a3_ref[2] = merged
        b3_ref[2] = d_res_b
        def colsum(vv):
            return jnp.sum(vv, axis=0, keepdims=True)

        red_ref[0:1, :] += colsum(d_y * xh2)
        red_ref[1:2, :] += colsum(d_og * o_n)
        red_ref[2:3, :] += colsum(d_c1 * xh)
        red_ref[3:4, :] += colsum(d_c1)
        red_ref[4:5, :] += colsum(d_c0)
        red_ref[5:6, :] += colsum(diff * diff) * (0.5 / D)

    def row_block(width, col):
        return pl.BlockSpec((tm, width), lambda i: (i, col))

    def const_block(shape):
        return pl.BlockSpec(shape, lambda i: (0,) * len(shape))

    stack = jax.ShapeDtypeStruct((3, TP, D), BF16)
    stack_spec = pl.BlockSpec((3, tm, D), lambda i: (0, i, 0))
    return pl.pallas_call(
        body, name="mid", grid=(TP // tm,),
        in_specs=[row_block(D, 0), row_block(D, 0), row_block(D, 0), row_block(D, 0),
                  row_block(D, 2), row_block(D, 6), row_block(D, 7), row_block(D, 8),
                  pl.BlockSpec((3, D, D), lambda i: (0, 0, 0), pipeline_mode=pl.Buffered(1)),
                  const_block((1, D)), const_block((1, D)), const_block((1, D)), const_block((1, D))],
        out_specs=(row_block(D, 0), row_block(D, 0), row_block(D, 0), row_block(D, 0), row_block(3 * D, 2),
                   stack_spec, stack_spec, const_block((8, D))),
        out_shape=(jax.ShapeDtypeStruct((TP, D), F32), jax.ShapeDtypeStruct((TP, D), F32),
                   jax.ShapeDtypeStruct((TP, D), F32), jax.ShapeDtypeStruct((TP, D), BF16),
                   jax.ShapeDtypeStruct((TP, D_IN), BF16), stack, stack, jax.ShapeDtypeStruct((8, D), F32)),
        scratch_shapes=[pltpu.VMEM((tm, D), F32), pltpu.VMEM((tm, D), F32)],
        compiler_params=pltpu.CompilerParams(dimension_semantics=("arbitrary",), vmem_limit_bytes=60 * 1024 * 1024),
    )(xin, tgt, o, c0, proj, proj, proj, proj, w3, ln_g, ln_b, gnorm_g, final_g)


def _conv_bwd(proj, d_c0, d_z, conv_w, dproj):
    tm = TM_ELT
    n_tile = TP // tm
    lastt = n_tile - 1

    strip = tm // CONV_STRIPS

    def body(p_ref, dc_ref, dz_ref, w_ref, dproj_in, dp_ref, dw_ref, dsh, a_sc, da_sc, acc):
        del dproj_in
        i = pl.program_id(0)

        @pl.when(i == 0)
        def _():
            dsh[0, :, tm:tm + HALO, :] = jnp.zeros((N_CB, HALO, HEAD_W), F32)
            acc[...] = jnp.zeros_like(acc)

        @pl.when(i > 0)
        def _():
            dsh[0, :, tm:tm + HALO, :] = dsh[0, :, 0:HALO, :]

        _store_by_cb(dsh, (0,), slice(0, tm), dc_ref[...])
        _fill_shifts(dsh, tm)
        ga = p_ref[:, 0:D].astype(F32)
        sb = _sigmoid(p_ref[:, D:2 * D].astype(F32))
        a = ga * sb
        _store_by_cb(a_sc, (), slice(0, tm), a)
        for cb in range(N_CB):
            cs = slice(cb * HEAD_W, (cb + 1) * HEAD_W)
            for st in range(CONV_STRIPS):
                rows = slice(st * strip, (st + 1) * strip)
                a_s = a_sc[cb, rows, :]
                d_a = jnp.zeros((strip, HEAD_W), F32)
                for j in range(CONV_K):
                    off = CONV_K - 1 - j
                    lo = st * strip + 8 * (off // 8)
                    slab = dsh[off % 8, cb, lo:lo + strip, :]
                    d_a = d_a + w_ref[j:j + 1, cs] * slab
                    acc[j, :, cs] += jnp.sum((a_s * slab).reshape(strip // 8, 8, HEAD_W), axis=0)
                da_sc[rows, cs] = d_a
        d_a = da_sc[...]
        dp_ref[:, 0:D] = (d_a * sb).astype(BF16)
        dp_ref[:, D:2 * D] = (d_a * a * (1.0 - sb)).astype(BF16)
        dp_ref[:, 2 * D:3 * D] = dz_ref[...]

        @pl.when(i == lastt)
        def _():
            for j in range(CONV_K):
                dw_ref[j:j + 1, :] = jnp.sum(acc[j], axis=0, keepdims=True)
            dw_ref[CONV_K:CONV_K + 1, :] = jnp.zeros((1, D), F32)

    return pl.pallas_call(
        body, name="conv_bwd", grid=(n_tile,),
        in_specs=[pl.BlockSpec((tm, 2 * D), lambda i: (lastt - i, 0)), pl.BlockSpec((tm, D), lambda i: (lastt - i, 0)),
                  pl.BlockSpec((tm, D), lambda i: (lastt - i, 0)), pl.BlockSpec((CONV_K, D), lambda i: (0, 0)), ANY],
        out_specs=(pl.BlockSpec((tm, 3 * D), lambda i: (lastt - i, 0)), pl.BlockSpec((CONV_K + 1, D), lambda i: (0, 0))),
        out_shape=(jax.ShapeDtypeStruct((TP, D_IN), BF16), jax.ShapeDtypeStruct((CONV_K + 1, D), F32)),
        scratch_shapes=[pltpu.VMEM((8, N_CB, tm + HALO, HEAD_W), F32), pltpu.VMEM((N_CB, tm, HEAD_W), F32),
                        pltpu.VMEM((tm, D), F32), pltpu.VMEM((CONV_K, 8, D), F32)],
        input_output_aliases={4: 0},
        compiler_params=pltpu.CompilerParams(dimension_semantics=("arbitrary",)),
    )(proj, d_c0, d_z, conv_w, dproj)


def _wgrad3(a3, b3):
    tt = TM_MAT

    def body(a_ref, b_ref, o_ref):
        @pl.when(pl.program_id(1) == 0)
        def _():
            o_ref[...] = jnp.zeros_like(o_ref)

        o_ref[0] += _dot_tn(a_ref[0], b_ref[0])

    return pl.pallas_call(
        body, name="wgrad3", grid=(3, TP // tt),
        in_specs=[pl.BlockSpec((1, tt, D), lambda g, t: (g, t, 0)), pl.BlockSpec((1, tt, D), lambda g, t: (g, t, 0))],
        out_specs=pl.BlockSpec((1, D, D), lambda g, t: (g, 0, 0)),
        out_shape=jax.ShapeDtypeStruct((3, D, D), F32),
        compiler_params=pltpu.CompilerParams(dimension_semantics=("arbitrary", "arbitrary")),
    )(a3, b3)


def _wgrad_in(h, dproj, ids, p3):
    tt = TM_MAT
    n_t = TP // tt

    def body(ids_ref, a_ref, b_ref, p3_ref, o_ref, ob_ref, l0_ref, l1_ref, acc, tmp, send_sems, recv_sems, tmp_sem):
        del ids_ref
        r = pl.program_id(0)
        t = pl.program_id(1)
        x, y, c = _my_place()
        sibling = (x, y, 1 - c)
        slot = lax.rem(r, 2)

        def send_in(q):
            return pltpu.make_async_remote_copy(
                src_ref=acc.at[q % 2], dst_ref=l0_ref.at[q], send_sem=send_sems.at[q], recv_sem=recv_sems.at[q],
                device_id=sibling, device_id_type=MESH_ID)

        def send_3(q):
            d = _dev_index(*_chip_rel(x, y, q), 1 - c)
            return pltpu.make_async_remote_copy(
                src_ref=p3_ref.at[:, pl.ds(pl.multiple_of(d * W_ROW_BLK, W_ROW_BLK), W_ROW_BLK), :],
                dst_ref=l1_ref.at[q], send_sem=send_sems.at[4 + q], recv_sem=recv_sems.at[4 + q],
                device_id=sibling, device_id_type=MESH_ID)

        def landed(q):
            return pltpu.make_async_copy(l0_ref.at[q], tmp, tmp_sem)

        @pl.when((r == 0) & (t == 0))
        def _():
            for q in range(4):
                send_3(q).start()

        @pl.when(t == 0)
        def _():
            acc[slot] = jnp.zeros((D, W_IN_BLK), F32)

        acc[slot] += _dot_tn(a_ref[...], b_ref[...])

        for q in range(4):
            @pl.when((r == q) & (t == n_t - 1))
            def _(q=q):
                if q >= 1:
                    send_in(q - 1).wait_send()
                send_in(q).start()

            @pl.when((r == 4 + q) & (t == n_t - 2))
            def _(q=q):
                if q == 0:
                    send_in(3).wait_send()
                send_in(q).wait_recv()
                landed(q).start()

            @pl.when((r == 4 + q) & (t == n_t - 1))
            def _(q=q):
                landed(q).wait()
                tot = acc[q % 2] + tmp[...]
                o_ref[0] = tot
                ob_ref[0] = tot.astype(BF16)

        @pl.when((r == 7) & (t == n_t - 1))
        def _():
            for q in range(4):
                send_3(q).wait_recv()
            for q in range(4):
                send_3(q).wait_send()

    blk = pl.BlockSpec((1, D, W_IN_BLK), lambda r, t, ids: (jnp.maximum(r - 4, 0), 0, 0))
    return pl.pallas_call(
        body, name="wgrad_in",
        grid_spec=pltpu.PrefetchScalarGridSpec(
            num_scalar_prefetch=1, grid=(N_DEV, n_t),
            in_specs=[pl.BlockSpec((tt, D), lambda r, t, ids: (t, 0)),
                      pl.BlockSpec((tt, W_IN_BLK), lambda r, t, ids: (t, ids[r])), ANY],
            out_specs=(blk, blk, ANY, ANY),
            scratch_shapes=[pltpu.VMEM((2, D, W_IN_BLK), F32), pltpu.VMEM((D, W_IN_BLK), F32),
                            pltpu.SemaphoreType.DMA((8,)), pltpu.SemaphoreType.DMA((8,)), pltpu.SemaphoreType.DMA]),
        out_shape=(jax.ShapeDtypeStruct((4, D, W_IN_BLK), F32), jax.ShapeDtypeStruct((4, D, W_IN_BLK), BF16),
                   jax.ShapeDtypeStruct((4, D, W_IN_BLK), F32), jax.ShapeDtypeStruct((4, 3, W_ROW_BLK, D), F32)),
        compiler_params=pltpu.CompilerParams(dimension_semantics=("arbitrary", "arbitrary")),
    )(ids, h, dproj, p3)


def _chip_sum_3(p3, land1, ids_mine):
    def body(ids_ref, p_ref, l_ref, o_ref, ob_ref):
        del ids_ref
        tot = p_ref[...] + l_ref[0]
        o_ref[0] = tot
        ob_ref[0] = tot.astype(BF16)

    blk = pl.BlockSpec((1, 3, W_ROW_BLK, D), lambda r, ids: (r, 0, 0, 0))
    return pl.pallas_call(
        body, name="chip_sum_3",
        grid_spec=pltpu.PrefetchScalarGridSpec(
            num_scalar_prefetch=1, grid=(4,),
            in_specs=[pl.BlockSpec((3, W_ROW_BLK, D), lambda r, ids: (0, ids[r], 0)), blk],
            out_specs=(blk, blk)),
        out_shape=(jax.ShapeDtypeStruct((4, 3, W_ROW_BLK, D), F32), jax.ShapeDtypeStruct((4, 3, W_ROW_BLK, D), BF16)),
    )(ids_mine, p3, land1)


def _dh_and_norm_bwd(dproj, w_in_full, xin, d_res, norm_g, chip0b, chip1b):
    tm = TM_MAT
    n_k = N_DEV
    n_m = TP // tm

    def body(dp_ref, w_ref, x_ref, dr_ref, g_ref, c0_ref, c1_ref, dx_ref, dg_ref, f0_ref, f1_ref, acc,
             send_sems, recv_sems):
        m = pl.program_id(0)
        k = pl.program_id(1)
        x, y, c = _my_place()

        def to_owner(a, q):
            src, dst = ((c0_ref, f0_ref), (c1_ref, f1_ref))[a]
            return pltpu.make_async_remote_copy(
                src_ref=src.at[q], dst_ref=dst.at[q - 1], send_sem=send_sems.at[a * 3 + q - 1],
                recv_sem=recv_sems.at[a * 3 + q - 1], device_id=(*_chip_rel(x, y, q), c), device_id_type=MESH_ID)

        @pl.when((m == 0) & (k == 0))
        def _():
            for q in range(1, 4):
                for a in range(2):
                    to_owner(a, q).start()

        @pl.when(k == 0)
        def _():
            acc[...] = jnp.zeros_like(acc)

        acc[...] += _dot_nt(dp_ref[...], w_ref[0])

        @pl.when((k == n_k - 1) & (m == 0))
        def _():
            dg_ref[...] = jnp.zeros_like(dg_ref)

        @pl.when(k == n_k - 1)
        def _():
            xv = x_ref[...]
            r1 = lax.rsqrt(jnp.mean(xv * xv, axis=-1, keepdims=True) + EPS)
            xh = xv * r1
            d_h = acc[...]
            dg_ref[0:1, :] += jnp.sum(d_h * xh, axis=0, keepdims=True)
            d_xh = d_h * g_ref[...]
            dx_ref[...] = dr_ref[...] + r1 * (d_xh - xh * jnp.mean(d_xh * xh, axis=-1, keepdims=True))

        @pl.when((m == n_m - 1) & (k == n_k - 1))
        def _():
            for q in range(1, 4):
                for a in range(2):
                    to_owner(a, q).wait_recv()
            for q in range(1, 4):
                for a in range(2):
                    to_owner(a, q).wait_send()

    return pl.pallas_call(
        body, name="dh_norm_bwd", grid=(n_m, n_k),
        in_specs=[pl.BlockSpec((tm, W_IN_BLK), lambda m, k: (m, k)), pl.BlockSpec((1, D, W_IN_BLK), lambda m, k: (k, 0, 0)),
                  pl.BlockSpec((tm, D), lambda m, k: (m, 0)), pl.BlockSpec((tm, D), lambda m, k: (m, 0)),
                  pl.BlockSpec((1, D), lambda m, k: (0, 0)), ANY, ANY],
        out_specs=(pl.BlockSpec((tm, D), lambda m, k: (m, 0)), pl.BlockSpec((8, D), lambda m, k: (0, 0)), ANY, ANY),
        out_shape=(jax.ShapeDtypeStruct((TP, D), F32), jax.ShapeDtypeStruct((8, D), F32),
                   jax.ShapeDtypeStruct((3, D, W_IN_BLK), BF16), jax.ShapeDtypeStruct((3, 3, W_ROW_BLK, D), BF16)),
        scratch_shapes=[pltpu.VMEM((tm, D), F32), pltpu.SemaphoreType.DMA((6,)), pltpu.SemaphoreType.DMA((6,))],
        compiler_params=pltpu.CompilerParams(dimension_semantics=("arbitrary", "arbitrary")),
    )(dproj, w_in_full, xin, d_res, norm_g, chip0b, chip1b)


def _sum_adamw(own, landed, w, m, v, tr, name):
    rows, cols = w.shape
    n_t = rows // tr

    def body(o_ref, l1_ref, l2_ref, l3_ref, w_ref, m_ref, v_ref, g_ref, d_ref, m2_ref, v2_ref):
        g = ((o_ref[...] + l1_ref[...].astype(F32)) + l2_ref[...].astype(F32)) + l3_ref[...].astype(F32)
        delta, m2, v2 = _adamw(w_ref[...], g, m_ref[...], v_ref[...])
        g_ref[...] = g
        d_ref[...] = delta
        m2_ref[...] = m2
        v2_ref[...] = v2

    def spec(k):
        return pl.BlockSpec((tr, cols), lambda i: (i + k * n_t, 0))

    out = jax.ShapeDtypeStruct((rows, cols), F32)
    return pl.pallas_call(
        body, name=name, grid=(n_t,),
        in_specs=[spec(0), spec(0), spec(1), spec(2), spec(0), spec(0), spec(0)],
        out_specs=(spec(0),) * 4, out_shape=(out,) * 4,
    )(own, landed, landed, landed, w, m, v)


def _small_update(pack_all, srs_all, lb_logits, p8, m8, v8, ws, ms, vs):
    def body(pk_ref, sr_ref, lbl_ref, p_ref, m_ref, v_ref, ws_ref, ms_ref, vs_ref,
             g8_ref, d8_ref, m8_ref, v8_ref, loss_ref, gs_ref, ds_ref, ms2_ref, vs2_ref):
        tot = pk_ref[0]
        tot_s = sr_ref[0]
        for d in range(1, N_DEV):
            tot = tot + pk_ref[d]
            tot_s = tot_s + sr_ref[d]
        p0 = _sigmoid(lbl_ref[0:1, :] - lbl_ref[1:2, :])
        row = lax.broadcasted_iota(jnp.int32, (8, D), 0)
        d_lb = jnp.sum(jnp.where(row == 4, tot, 0.0), axis=0, keepdims=True)
        d_l0 = d_lb * p0 * (1.0 - p0)
        loss_ref[...] = jnp.sum(jnp.where(row == 5, tot, 0.0), keepdims=True).reshape(1, 1)
        g8 = jnp.where(row == 4, d_l0, jnp.where(row == 5, -d_l0, tot))
        delta, m2, v2 = _adamw(p_ref[...], g8, m_ref[...], v_ref[...])
        g8_ref[...] = g8
        d8_ref[...] = delta
        m8_ref[...] = m2
        v8_ref[...] = v2
        delta, m2, v2 = _adamw(ws_ref[...], tot_s, ms_ref[...], vs_ref[...])
        gs_ref[...] = tot_s
        ds_ref[...] = delta
        ms2_ref[...] = m2
        vs2_ref[...] = v2

    o8 = jax.ShapeDtypeStruct((8, D), F32)
    os_ = jax.ShapeDtypeStruct((SMALL_ROWS, HEAD_W), F32)
    return pl.pallas_call(
        body, name="small_update",
        out_shape=(o8, o8, o8, o8, jax.ShapeDtypeStruct((1, 1), F32), os_, os_, os_, os_),
    )(pack_all, srs_all, lb_logits, p8, m8, v8, ws, ms, vs)


def _local_step(xin, proj, target, conv_w_full, conv_b, ln_g, ln_b, w3_full, lb_logits, gnorm_g, final_g):
    tgt = jnp.concatenate([jnp.zeros((ROW0, D), F32), target], axis=0)
    fg = final_g.reshape(1, D)
    c0 = _conv_fwd(proj, conv_w_full, conv_b)
    o, s_start = _rec_fwd(proj, lb_logits)
    d_res, d_o, d_c0, d_z, dproj, a3, b3, red = _mid(xin, tgt, o, c0, proj, w3_full, ln_g, ln_b, gnorm_g, fg)
    dproj, dlb = _rec_bwd(proj, lb_logits, d_o, s_start, dproj)
    dproj, d_conv_w = _conv_bwd(proj, d_c0, d_z, conv_w_full, dproj)
    p3 = _wgrad3(a3, b3)
    return dproj, d_res, p3, d_conv_w, red, dlb


def kernel(x, meta_tokens, norm_g, w_in, conv_w, conv_b, ln_g, ln_b, w_conv_out, lb_logits, gnorm_g, w_rec_out, w_out, final_g, loss_target, m_meta_tokens, m_norm_g, m_w_in, m_conv_w, m_conv_b, m_ln_g, m_ln_b, m_w_conv_out, m_lb_logits, m_gnorm_g, m_w_rec_out, m_w_out, m_final_g, v_meta_tokens, v_norm_g, v_w_in, v_conv_w, v_conv_b, v_ln_g, v_ln_b, v_w_conv_out, v_lb_logits, v_gnorm_g, v_w_rec_out, v_w_out, v_final_g):
    def small_pack(cw, mt):
        return jnp.concatenate([cw[0], jnp.zeros((1, HEAD_W), F32), mt], axis=0)

    def stack3(a, b, c):
        return jnp.concatenate([a, b, c], axis=0)

    def stack8(ng, cb, lg, lb_, lbl, gg, fg):
        return jnp.concatenate([ng, cb, lg, lb_, lbl, gg, fg.reshape(1, D)], axis=0)

    mx, my, mc = _my_place()

    w3_s = stack3(w_conv_out, w_rec_out, w_out)
    ws_s = small_pack(conv_w, meta_tokens)
    small_full = jnp.transpose(_gather_small(ws_s), (1, 0, 2)).reshape(SMALL_ROWS, D)
    conv_w_full = small_full[0:CONV_K]
    meta_full = small_full[META_ROW:META_ROW + N_META]
    xin = jnp.concatenate([jnp.zeros((PAD_FRONT, D), F32), meta_full, x[0]], axis=0)
    w_in_b, w3_b = _cast_shards(w_in[0], w3_s)
    use_order = [(mx, my, mc), (mx, my, 1 - mc)]
    for chip in ((1 - mx, my), (mx, 1 - my), (1 - mx, 1 - my)):
        use_order += [(*chip, mc), (*chip, 1 - mc)]
    order = jnp.stack([_dev_index(*p) for p in use_order]).astype(jnp.int32)
    proj, h, w_in_full, w3_full = _gather_and_proj(xin, norm_g, w_in_b, w3_b, order)
    h = h.reshape(TP, D)

    dproj, d_res, p3, d_conv_w, red, dlb = _local_step(
        xin, proj, loss_target[0], conv_w_full, conv_b, ln_g, ln_b, w3_full, lb_logits, gnorm_g, final_g)

    ids_mine = jnp.stack([_dev_index(*_chip_rel(mx, my, r), mc) for r in range(4)]).astype(jnp.int32)
    ids_sib = jnp.stack([_dev_index(*_chip_rel(mx, my, r), 1 - mc) for r in range(4)]).astype(jnp.int32)
    chip0, chip0b, _, land1 = _wgrad_in(h, dproj, jnp.concatenate([ids_sib, ids_mine]), p3)
    chip1, chip1b = _chip_sum_3(p3, land1, ids_mine)
    d_xin, dng, far0, far1 = _dh_and_norm_bwd(dproj, w_in_full, xin, d_res, norm_g, chip0b, chip1b)
    pack = jnp.concatenate([dng[0:1], red[4:5], red[2:3], red[3:4], dlb[0:1], red[5:6], red[1:2], red[0:1]], axis=0)
    g_in, d_in, m_in, v_in = _sum_adamw(chip0.reshape(4 * D, W_IN_BLK), far0.reshape(3 * D, W_IN_BLK), w_in[0],
                                        m_w_in[0], v_w_in[0], 256, "adamw_in")
    g_3, d_3, m_3, v_3 = _sum_adamw(
        chip1.reshape(12 * W_ROW_BLK, D), far1.reshape(9 * W_ROW_BLK, D), w3_s.reshape(3 * W_ROW_BLK, D),
        stack3(m_w_conv_out, m_w_rec_out, m_w_out).reshape(3 * W_ROW_BLK, D),
        stack3(v_w_conv_out, v_w_rec_out, v_w_out).reshape(3 * W_ROW_BLK, D), 3 * W_ROW_BLK, "adamw_3")

    srs = jnp.concatenate([d_conv_w, d_xin[PAD_FRONT:ROW0]], axis=0)
    srs = jnp.transpose(srs.reshape(SMALL_ROWS, N_DEV, HEAD_W), (1, 0, 2))
    pack_all, srs_all = _exchange_small(pack, srs)
    g8, d8, m8, v8, loss, gs, ds, ms, vs = _small_update(
        pack_all, srs_all, lb_logits,
        stack8(norm_g, conv_b, ln_g, ln_b, lb_logits, gnorm_g, final_g),
        stack8(m_norm_g, m_conv_b, m_ln_g, m_ln_b, m_lb_logits, m_gnorm_g, m_final_g),
        stack8(v_norm_g, v_conv_b, v_ln_g, v_ln_b, v_lb_logits, v_gnorm_g, v_final_g),
        ws_s, small_pack(m_conv_w, m_meta_tokens), small_pack(v_conv_w, v_meta_tokens))

    def unpack(a_in, a_3, a_s, a_8):
        t3 = a_3.reshape(3, 1, W_ROW_BLK, D)
        return (a_s[META_ROW:META_ROW + N_META], a_8[0:1], a_in[None], a_s[0:CONV_K][None], a_8[1:2], a_8[2:3],
                a_8[3:4], t3[0], a_8[4:6], a_8[6:7], t3[1], t3[2], a_8[7])

    grad_x = d_xin[ROW0:][None]
    return (loss.reshape(()), grad_x, *unpack(g_in, g_3, gs, g8), *unpack(d_in, d_3, ds, d8),
            *unpack(m_in, m_3, ms, m8), *unpack(v_in, v_3, vs, v8))
```

```python
import functools

import jax
import jax.numpy as jnp
from jax import lax
from jax.experimental import pallas as pl
from jax.experimental.pallas import tpu as pltpu

F32 = jnp.float32
BF16 = jnp.bfloat16

D = 1024
SEQ = 4096
N_META = 16
CHUNK = 64
PAD_FRONT = 48
ROW0 = PAD_FRONT + N_META
TP = ROW0 + SEQ
N_CHUNK = TP // CHUNK
HEADS = 8
HEAD_W = 128
D_IN = 9 * D
N_DEV = 8
W_IN_BLK = D_IN // N_DEV
W_ROW_BLK = D // N_DEV
CONV_K = 31
SMALL_ROWS = 48
META_ROW = 32
EPS = 1e-6
HALO = 32

TM_MAT = 832
TM_ELT = 208
CHUNKS_PER_STEP = 5
CONV_STRIPS = 2

ADAM_LR = 0.001
ADAM_B1 = 0.9
ADAM_B2 = 0.999
ADAM_EPS = 1e-08
ADAM_WD = 0.01
ADAM_STEP = 10

MESH_ID = pl.DeviceIdType.MESH
ANY = pl.BlockSpec(memory_space=pl.ANY)


def _sigmoid(v):
    return jax.nn.sigmoid(v)


def _dsilu(v, s):
    return s * (1.0 + v * (1.0 - s))


def _dot(a, b):
    return jnp.dot(a, b, preferred_element_type=F32)


def _dot_nt(a, b):
    return lax.dot_general(a, b, (((1,), (1,)), ((), ())), preferred_element_type=F32)


def _dot_tn(a, b):
    return lax.dot_general(a, b, (((0,), (0,)), ((), ())), preferred_element_type=F32)


def _split3(v):
    hi = v.astype(BF16)
    r1 = v - hi.astype(F32)
    mid = r1.astype(BF16)
    lo = (r1 - mid.astype(F32)).astype(BF16)
    return hi, mid, lo


def _tri_matmul(tri, v):
    hi, mid, lo = _split3(v)
    return _dot(tri, hi) + _dot(tri, mid) + _dot(tri, lo)


def _adamw(w, g, m, v):
    m2 = ADAM_B1 * m + (1.0 - ADAM_B1) * g
    v2 = ADAM_B2 * v + (1.0 - ADAM_B2) * jnp.square(g)
    m_hat = m2 / (1.0 - ADAM_B1 ** ADAM_STEP)
    v_hat = v2 / (1.0 - ADAM_B2 ** ADAM_STEP)
    delta = -ADAM_LR * (m_hat / (jnp.sqrt(v_hat) + ADAM_EPS) + ADAM_WD * w)
    return delta, m2, v2


def _my_place():
    return lax.axis_index("x"), lax.axis_index("y"), lax.axis_index("c")


def _dev_index(px, py, pc):
    return 4 * px + 2 * py + pc


def _cast_shards(w_in_s, w3_s):
    def body(a_ref, b_ref, oa_ref, ob_ref):
        oa_ref[...] = a_ref[...].astype(BF16)
        ob_ref[...] = b_ref[...].astype(BF16)

    return pl.pallas_call(
        body, name="cast_shards",
        out_shape=(jax.ShapeDtypeStruct(w_in_s.shape, BF16), jax.ShapeDtypeStruct(w3_s.shape, BF16)),
    )(w_in_s, w3_s)


def _peer(x, y, c, r):
    return (jnp.bitwise_xor(x, (r >> 2) & 1), jnp.bitwise_xor(y, (r >> 1) & 1), jnp.bitwise_xor(c, r & 1))


def _gather_small(small_s):
    def body(s_ref, o_ref, send_sems, recv_sems, local_sem):
        x, y, c = _my_place()
        my_id = _dev_index(x, y, c)
        mine = pltpu.make_async_copy(s_ref, o_ref.at[my_id], local_sem)
        mine.start()
        copies = []
        for r in range(1, N_DEV):
            cp = pltpu.make_async_remote_copy(
                src_ref=s_ref, dst_ref=o_ref.at[my_id], send_sem=send_sems.at[r - 1], recv_sem=recv_sems.at[r - 1],
                device_id=_peer(x, y, c, r), device_id_type=MESH_ID)
            cp.start()
            copies.append(cp)
        for cp in copies:
            cp.wait_recv()
        for cp in copies:
            cp.wait_send()
        mine.wait()

    return pl.pallas_call(
        body, name="gather_small", out_shape=jax.ShapeDtypeStruct((N_DEV,) + small_s.shape, F32),
        in_specs=[ANY], out_specs=ANY,
        scratch_shapes=[pltpu.SemaphoreType.DMA((7,)), pltpu.SemaphoreType.DMA((7,)), pltpu.SemaphoreType.DMA],
    )(small_s)


def _w3_gather(src, out, stage, send_sems, recv_sems, local_sems):
    x, y, c = _my_place()
    me, sibling = (x, y, c), (x, y, 1 - c)
    chips = [(1 - x, y), (x, 1 - y), (1 - x, 1 - y)]

    def block(place):
        d = _dev_index(*place)
        return out.at[:, pl.ds(pl.multiple_of(d * W_ROW_BLK, W_ROW_BLK), W_ROW_BLK), :]

    def copy(k, place, to, from_src=False):
        return pltpu.make_async_remote_copy(
            src_ref=src if from_src else block(place), dst_ref=block(place),
            send_sem=send_sems.at[k], recv_sem=recv_sems.at[k], device_id=to, device_id_type=MESH_ID)

    own_in = pltpu.make_async_copy(src, stage, local_sems.at[0])
    own_out = pltpu.make_async_copy(stage, block(me), local_sems.at[1])

    def start():
        copy(0, me, sibling, from_src=True).start()
        for j, chip in enumerate(chips):
            copy(1 + j, me, (*chip, c), from_src=True).start()
        own_in.start()
        own_in.wait()
        own_out.start()

    def finish():
        for j, chip in enumerate(chips):
            copy(1 + j, (*chip, c), me).wait_recv()
            copy(4 + j, (*chip, c), sibling).start()
        copy(0, sibling, me).wait_recv()
        for j, chip in enumerate(chips):
            copy(4 + j, (*chip, 1 - c), me).wait_recv()
        for k in range(7):
            copy(k, me, me).wait_send()
        own_out.wait()

    return start, finish


def _gather_and_proj(xin, norm_g, w_in_b, order):
    tm = TM_MAT
    n_m = TP // tm
    last_m = n_m - 1

    def body(order_ref, x_ref, g_ref, s0, proj_ref, h_out, o0, hbuf, wbuf, send_sems, recv_sems, local_sems):
        del order_ref
        n = pl.program_id(0)
        m = pl.program_id(1)
        x, y, c = _my_place()
        me, sibling = (x, y, c), (x, y, 1 - c)
        chips = [(1 - x, y), (x, 1 - y), (1 - x, 1 - y)]

        def block(place):
            return o0.at[_dev_index(*place)]

        def copy(k, place, to, from_src=False):
            return pltpu.make_async_remote_copy(
                src_ref=s0 if from_src else block(place), dst_ref=block(place),
                send_sem=send_sems.at[k], recv_sem=recv_sems.at[k], device_id=to, device_id_type=MESH_ID)

        def to_vmem(place, slot):
            return pltpu.make_async_copy(block(place), wbuf.at[slot], local_sems.at[slot])

        own_out = pltpu.make_async_copy(wbuf.at[0], block(me), local_sems.at[2])
        h_copy = pltpu.make_async_copy(hbuf, h_out, local_sems.at[3])

        @pl.when((n == 0) & (m == 0))
        def _():
            copy(0, me, sibling, from_src=True).start()
            for j, chip in enumerate(chips):
                copy(1 + j, me, (*chip, c), from_src=True).start()
            mine = pltpu.make_async_copy(s0, wbuf.at[0], local_sems.at[0])
            mine.start()
            mine.wait()
            own_out.start()

        @pl.when(n == 0)
        def _():
            xv = x_ref[...]
            r = lax.rsqrt(jnp.mean(xv * xv, axis=-1, keepdims=True) + EPS)
            hbuf[m] = (xv * r * g_ref[...]).astype(BF16)

        plan = [(sibling, (0, sibling), None)]
        for j, chip in enumerate(chips):
            plan.append(((*chip, c), (1 + j, (*chip, c)), 4 + j))
            plan.append(((*chip, 1 - c), (4 + j, (*chip, 1 - c)), None))

        for s, (place, (k, origin), pass_on) in enumerate(plan, start=1):
            @pl.when((n == s - 1) & (m == last_m))
            def _(s=s, place=place, k=k, origin=origin, pass_on=pass_on):
                copy(k, origin, me).wait_recv()
                if pass_on is not None:
                    copy(pass_on, place, sibling).start()
                if s == 2:
                    own_out.wait()
                to_vmem(place, s % 2).start()

            @pl.when((n == s) & (m == 0))
            def _(s=s, place=place):
                to_vmem(place, s % 2).wait()

        proj_ref[...] = _dot(hbuf[m], wbuf[lax.rem(n, 2)]).astype(BF16)

        @pl.when((n == 0) & (m == last_m))
        def _():
            h_copy.start()

        @pl.when((n == N_DEV - 1) & (m == last_m))
        def _():
            for k in range(7):
                copy(k, me, me).wait_send()
            h_copy.wait()

    return pl.pallas_call(
        body, name="gather_and_proj",
        grid_spec=pltpu.PrefetchScalarGridSpec(
            num_scalar_prefetch=1, grid=(N_DEV, n_m),
            in_specs=[pl.BlockSpec((tm, D), lambda n, m, o: (jnp.where(n == 0, m, 0), 0)),
                      pl.BlockSpec((1, D), lambda n, m, o: (0, 0)), ANY],
            out_specs=(pl.BlockSpec((tm, W_IN_BLK), lambda n, m, o: (m, o[n])), ANY, ANY),
            scratch_shapes=[pltpu.VMEM((n_m, tm, D), BF16), pltpu.VMEM((2, D, W_IN_BLK), BF16),
                            pltpu.SemaphoreType.DMA((7,)), pltpu.SemaphoreType.DMA((7,)),
                            pltpu.SemaphoreType.DMA((4,))]),
        out_shape=(jax.ShapeDtypeStruct((TP, D_IN), BF16), jax.ShapeDtypeStruct((n_m, tm, D), BF16),
                   jax.ShapeDtypeStruct((N_DEV, D, W_IN_BLK), BF16)),
        compiler_params=pltpu.CompilerParams(dimension_semantics=("arbitrary", "arbitrary")),
    )(order, xin, norm_g, w_in_b)


def _chip_rel(x, y, r):
    return (jnp.bitwise_xor(x, r >> 1), jnp.bitwise_xor(y, r & 1))


def _exchange_small(pack, srs):
    def body(pk, sr, pk_all, sr_all, send_sems, recv_sems, local_sems):
        x, y, c = _my_place()
        my_id = _dev_index(x, y, c)
        mine = [pltpu.make_async_copy(pk, pk_all.at[my_id], local_sems.at[0]),
                pltpu.make_async_copy(sr.at[my_id], sr_all.at[my_id], local_sems.at[1])]
        for cp in mine:
            cp.start()
        copies = []
        for r in range(1, N_DEV):
            peer = (jnp.bitwise_xor(x, (r >> 2) & 1), jnp.bitwise_xor(y, (r >> 1) & 1), jnp.bitwise_xor(c, r & 1))
            peer_id = _dev_index(*peer)
            for a, (src, dst) in enumerate(((pk, pk_all.at[my_id]), (sr.at[peer_id], sr_all.at[my_id]))):
                cp = pltpu.make_async_remote_copy(
                    src_ref=src, dst_ref=dst, send_sem=send_sems.at[a * 7 + r - 1], recv_sem=recv_sems.at[a * 7 + r - 1],
                    device_id=peer, device_id_type=MESH_ID)
                cp.start()
                copies.append(cp)
        for cp in copies:
            cp.wait_recv()
        for cp in copies:
            cp.wait_send()
        for cp in mine:
            cp.wait()

    return pl.pallas_call(
        body, name="exchange_small",
        out_shape=(jax.ShapeDtypeStruct((N_DEV,) + pack.shape, F32), jax.ShapeDtypeStruct(srs.shape, F32)),
        in_specs=[ANY, ANY], out_specs=(ANY, ANY),
        scratch_shapes=[pltpu.SemaphoreType.DMA((14,)), pltpu.SemaphoreType.DMA((14,)), pltpu.SemaphoreType.DMA((2,))],
    )(pack, srs)


N_CB = D // HEAD_W


def _store_by_cb(ref, idx, rows, val):
    for cb in range(N_CB):
        ref[(*idx, cb, rows, slice(None))] = val[:, cb * HEAD_W:(cb + 1) * HEAD_W]


def _fill_shifts(sh, tm):
    n = tm + HALO - 8
    for s in range(1, 8):
        for cb in range(N_CB):
            sh[s, cb, 0:n, :] = sh[0, cb, s:s + n, :]


def _conv_fwd(proj, conv_w, conv_b):
    tm = TM_ELT
    strip = tm // CONV_STRIPS

    def body(p_ref, w_ref, b_ref, c0_ref, sh):
        i = pl.program_id(0)

        @pl.when(i == 0)
        def _():
            sh[0, :, 0:HALO, :] = jnp.zeros((N_CB, HALO, HEAD_W), F32)

        @pl.when(i > 0)
        def _():
            sh[0, :, 0:HALO, :] = sh[0, :, tm:tm + HALO, :]

        ga = p_ref[:, 0:D].astype(F32)
        gb = p_ref[:, D:2 * D].astype(F32)
        _store_by_cb(sh, (0,), slice(HALO, HALO + tm), ga * _sigmoid(gb))
        _fill_shifts(sh, tm)
        for cb in range(N_CB):
            cs = slice(cb * HEAD_W, (cb + 1) * HEAD_W)
            for st in range(CONV_STRIPS):
                acc = jnp.broadcast_to(b_ref[:, cs], (strip, HEAD_W))
                for j in range(CONV_K):
                    off = HALO - (CONV_K - 1) + j
                    lo = st * strip + 8 * (off // 8)
                    acc = acc + w_ref[j:j + 1, cs] * sh[off % 8, cb, lo:lo + strip, :]
                c0_ref[st * strip:(st + 1) * strip, cs] = acc

    return pl.pallas_call(
        body, name="conv_fwd", grid=(TP // tm,),
        in_specs=[pl.BlockSpec((tm, 2 * D), lambda i: (i, 0)), pl.BlockSpec((CONV_K, D), lambda i: (0, 0)),
                  pl.BlockSpec((1, D), lambda i: (0, 0))],
        out_specs=pl.BlockSpec((tm, D), lambda i: (i, 0)),
        out_shape=jax.ShapeDtypeStruct((TP, D), F32),
        scratch_shapes=[pltpu.VMEM((8, N_CB, HALO + tm, HEAD_W), F32)],
        compiler_params=pltpu.CompilerParams(dimension_semantics=("arbitrary",)),
    )(proj, conv_w, conv_b)


def _gates(p_ref, lbl_ref, chunk, bsc):
    lb = _sigmoid(lbl_ref[0:1, :] - lbl_ref[1:2, :])
    q_raw = p_ref[:, 0:D].astype(F32)
    f_raw = p_ref[:, D:2 * D].astype(F32)
    sq = _sigmoid(q_raw)
    q = q_raw * sq
    sg = _sigmoid(f_raw)
    f = lb + (1.0 - lb) * sg
    row = lax.broadcasted_iota(jnp.int32, (CHUNK, 1), 0) + chunk * CHUNK
    valid = row >= PAD_FRONT
    lf = jnp.where(valid, jnp.log(f), 0.0)
    k = jnp.where(valid, 1.0 - f, 0.0)
    r_i = lax.broadcasted_iota(jnp.int32, (CHUNK, CHUNK), 0)
    c_i = lax.broadcasted_iota(jnp.int32, (CHUNK, CHUNK), 1)
    causal = r_i >= c_i
    bsc[...] = _tri_matmul(causal.astype(BF16), lf)
    b = bsc[...]
    b_mid = bsc[CHUNK // 2 - 1:CHUNK // 2, :]
    b_last = bsc[CHUNK - 1:CHUNK, :]
    e_q = jnp.exp(b)
    e_qm = jnp.exp(b - b_mid)
    e_km = jnp.exp(b_mid - b)
    e_kh = jnp.exp(b_last - b)
    e_last = jnp.exp(b_last)
    return dict(lb=lb, q_raw=q_raw, sq=sq, q=q, sg=sg, f=f, k=k, valid=valid, causal=causal,
                e_q=e_q, e_qm=e_qm, e_km=e_km, e_kh=e_kh, e_last=e_last)


def _rec_fwd(proj, lb_logits, w3_b):
    cps = CHUNKS_PER_STEP
    rows = cps * CHUNK

    def body(p_ref, lbl_ref, w3s_ref, o_ref, s_ref, w3o_ref, st, bsc, w3buf, send_sems, recv_sems, local_sems):
        n = pl.program_id(0)
        gather_start, gather_finish = _w3_gather(w3s_ref, w3o_ref, w3buf, send_sems, recv_sems, local_sems)

        @pl.when(n == 0)
        def _():
            st[...] = jnp.zeros_like(st)
            gather_start()

        def prep(ci):
            g = _gates(p_ref.at[pl.ds(ci * CHUNK, CHUNK)], lbl_ref, n * cps + ci, bsc.at[ci])
            g["q1"] = (g["q"] * g["e_q"]).astype(BF16)
            g["qm"] = (g["q"] * g["e_qm"]).astype(BF16)
            g["km"] = (g["k"] * g["e_km"]).astype(BF16)
            g["kh"] = (g["k"] * g["e_kh"]).astype(BF16)
            return g

        def heads(ci, g):
            rs = pl.ds(ci * CHUNK, CHUNK)
            pv = p_ref.at[rs]
            s_ref[ci] = st[...]
            for h in range(HEADS):
                sl = slice(h * HEAD_W, (h + 1) * HEAD_W)
                v = pv[:, 2 * D + h * HEAD_W:2 * D + (h + 1) * HEAD_W]
                att = jnp.where(g["causal"], _dot_nt(g["qm"][:, sl], g["km"][:, sl]), 0.0).astype(BF16)
                s_h = st[h]
                o_ref[rs, sl] = _dot_nt(g["q1"][:, sl], s_h.astype(BF16)) + _dot(att, v)
                st[h] = s_h * g["e_last"][:, sl] + _dot_tn(v, g["kh"][:, sl])

        ready = prep(0)
        for ci in range(cps):
            coming = prep(ci + 1) if ci + 1 < cps else None
            heads(ci, ready)
            ready = coming

        @pl.when(n == N_CHUNK // cps - 1)
        def _():
            gather_finish()

    return pl.pallas_call(
        body, name="rec_fwd", grid=(N_CHUNK // cps,),
        in_specs=[pl.BlockSpec((rows, 3 * D), lambda n: (n, 1)), pl.BlockSpec((2, D), lambda n: (0, 0)), ANY],
        out_specs=(pl.BlockSpec((rows, D), lambda n: (n, 0)),
                   pl.BlockSpec((cps, HEADS, HEAD_W, HEAD_W), lambda n: (n, 0, 0, 0)), ANY),
        out_shape=(jax.ShapeDtypeStruct((TP, D), F32), jax.ShapeDtypeStruct((N_CHUNK, HEADS, HEAD_W, HEAD_W), F32),
                   jax.ShapeDtypeStruct((3, D, D), BF16)),
        scratch_shapes=[pltpu.VMEM((HEADS, HEAD_W, HEAD_W), F32), pltpu.VMEM((cps, CHUNK, D), F32),
                        pltpu.VMEM((3, W_ROW_BLK, D), BF16), pltpu.SemaphoreType.DMA((7,)),
                        pltpu.SemaphoreType.DMA((7,)), pltpu.SemaphoreType.DMA((2,))],
        compiler_params=pltpu.CompilerParams(dimension_semantics=("arbitrary",)),
    )(proj, lb_logits, w3_b)


def _rec_bwd(proj, lb_logits, d_o, s_start, dproj):
    cps = CHUNKS_PER_STEP
    rows = cps * CHUNK
    last = N_CHUNK // cps - 1

    def body(p_ref, lbl_ref, do_ref, s_ref, dproj_in, dp_ref, dlb_ref, dst, bsc, dq_sc, dk_sc, g_sc):
        del dproj_in
        n = pl.program_id(0)

        @pl.when(n == 0)
        def _():
            dst[...] = jnp.zeros_like(dst)
            dlb_ref[...] = jnp.zeros_like(dlb_ref)

        def prep(ci):
            g = _gates(p_ref.at[pl.ds(ci * CHUNK, CHUNK)], lbl_ref, (last - n) * cps + ci, bsc.at[ci])
            g["q1"] = (g["q"] * g["e_q"]).astype(BF16)
            qm_f = g["q"] * g["e_qm"]
            km_f = g["k"] * g["e_km"]
            g["qm"] = qm_f.astype(BF16)
            g["km"] = km_f.astype(BF16)
            g["qm_lo"] = (qm_f - g["qm"].astype(F32)).astype(BF16)
            g["km_lo"] = (km_f - g["km"].astype(F32)).astype(BF16)
            g["kh_f"] = g["k"] * g["e_kh"]
            g["kh"] = g["kh_f"].astype(BF16)
            return g

        def heads_and_post(ci, g):
            rs = pl.ds(ci * CHUNK, CHUNK)
            pv = p_ref.at[rs]
            dpv = dp_ref.at[rs]
            q1, qm, km, qm_lo, km_lo, kh_f, kh = (g[k] for k in ("q1", "qm", "km", "qm_lo", "km_lo", "kh_f", "kh"))
            for h in range(HEADS):
                sl = slice(h * HEAD_W, (h + 1) * HEAD_W)
                v = pv[:, 2 * D + h * HEAD_W:2 * D + (h + 1) * HEAD_W]
                d_oh = do_ref[rs, sl].astype(BF16)
                s0 = s_ref[ci, h]
                ds_end = dst[h]
                ds_end_b = ds_end.astype(BF16)
                att = jnp.where(g["causal"], _dot_nt(qm[:, sl], km[:, sl]), 0.0).astype(BF16)
                d_att = jnp.where(g["causal"], _dot_nt(d_oh, v), 0.0).astype(BF16)
                d_v = _dot_tn(att, d_oh) + _dot_nt(kh[:, sl], ds_end_b)
                d_qm = _dot(d_att, km[:, sl]) + _dot(d_att, km_lo[:, sl])
                d_q1 = _dot(d_oh, s0.astype(BF16))
                d_km = _dot_tn(d_att, qm[:, sl]) + _dot_tn(d_att, qm_lo[:, sl])
                d_kh = _dot(v, ds_end_b)
                dq_sc[ci, :, sl] = d_qm * g["e_qm"][:, sl] + d_q1 * g["e_q"][:, sl]
                dk_sc[ci, :, sl] = d_km * g["e_km"][:, sl] + d_kh * g["e_kh"][:, sl]
                g_sc[ci, :, sl] = (jnp.sum(kh_f[:, sl] * d_kh, axis=0, keepdims=True)
                                   + g["e_last"][:, sl] * jnp.sum(ds_end * s0, axis=0, keepdims=True))
                dst[h] = ds_end * g["e_last"][:, sl] + _dot_tn(d_oh, q1[:, sl])
                dpv[:, 2 * D + h * HEAD_W:2 * D + (h + 1) * HEAD_W] = d_v.astype(BF16)
            d_q = dq_sc[ci]
            d_k = dk_sc[ci]
            d_b = g["q"] * d_q - g["k"] * d_k
            anti = jnp.logical_not(g["causal"]) | (lax.broadcasted_iota(jnp.int32, (CHUNK, CHUNK), 0)
                                                    == lax.broadcasted_iota(jnp.int32, (CHUNK, CHUNK), 1))
            d_lf = _tri_matmul(anti.astype(BF16), d_b) + g_sc[ci]
            d_f = jnp.where(g["valid"], d_lf / g["f"] - d_k, 0.0)
            sg = g["sg"]
            dlb_ref[0:1, :] += jnp.sum(d_f * (1.0 - sg), axis=0, keepdims=True)
            dpv[:, 0:D] = (d_q * _dsilu(g["q_raw"], g["sq"])).astype(BF16)
            dpv[:, D:2 * D] = (d_f * (1.0 - g["lb"]) * sg * (1.0 - sg)).astype(BF16)

        ready = prep(cps - 1)
        for ci in reversed(range(cps)):
            coming = prep(ci - 1) if ci > 0 else None
            heads_and_post(ci, ready)
            ready = coming

    return pl.pallas_call(
        body, name="rec_bwd", grid=(N_CHUNK // cps,),
        in_specs=[pl.BlockSpec((rows, 3 * D), lambda n: (last - n, 1)), pl.BlockSpec((2, D), lambda n: (0, 0)),
                  pl.BlockSpec((rows, D), lambda n: (last - n, 0)),
                  pl.BlockSpec((cps, HEADS, HEAD_W, HEAD_W), lambda n: (last - n, 0, 0, 0)), ANY],
        out_specs=(pl.BlockSpec((rows, 3 * D), lambda n: (last - n, 1)), pl.BlockSpec((8, D), lambda n: (0, 0))),
        out_shape=(jax.ShapeDtypeStruct((TP, D_IN), BF16), jax.ShapeDtypeStruct((8, D), F32)),
        scratch_shapes=[pltpu.VMEM((HEADS, HEAD_W, HEAD_W), F32), pltpu.VMEM((cps, CHUNK, D), F32),
                        pltpu.VMEM((cps, CHUNK, D), F32), pltpu.VMEM((cps, CHUNK, D), F32),
                        pltpu.VMEM((cps, 1, D), F32)],
        input_output_aliases={4: 0},
        compiler_params=pltpu.CompilerParams(dimension_semantics=("arbitrary",)),
    )(proj, lb_logits, d_o, s_start, dproj)


def _mid(xin, tgt, o, c0, proj, w3, ln_g, ln_b, gnorm_g, final_g):
    tm = TM_ELT

    def body(x_ref, t_ref, o_ref, c0_ref, z_ref, gr_ref, mc_ref, mr_ref, w_ref, lng_ref, lnb_ref, gng_ref, fg_ref,
             dres_ref, do_ref, dc0_ref, dz_ref, dp_ref, a3_ref, b3_ref, red_ref, on_sc, don_sc):
        i = pl.program_id(0)

        @pl.when(i == 0)
        def _():
            red_ref[...] = jnp.zeros_like(red_ref)

        w_conv, w_rec, w_out = w_ref[0], w_ref[1], w_ref[2]
        c0v = c0_ref[...]
        mu = jnp.mean(c0v, axis=-1, keepdims=True)
        xc = c0v - mu
        rstd = lax.rsqrt(jnp.mean(xc * xc, axis=-1, keepdims=True) + EPS)
        xh = xc * rstd
        c1 = xh * lng_ref[...] + lnb_ref[...]
        s1 = _sigmoid(c1)
        c2 = c1 * s1
        z = z_ref[...].astype(F32)
        sz = _sigmoid(z)
        silu_z = z * sz
        u_conv = (c2 * silu_z).astype(BF16)
        y_conv = _dot(u_conv, w_conv)
        ov = o_ref[...]
        r3 = []
        for h in range(HEADS):
            sl = slice(h * HEAD_W, (h + 1) * HEAD_W)
            oh = ov[:, sl]
            r_h = lax.rsqrt(jnp.mean(oh * oh, axis=-1, keepdims=True) + EPS)
            r3.append(r_h)
            on_sc[:, sl] = oh * r_h
        o_n = on_sc[...]
        o_g = o_n * gng_ref[...]
        gr = gr_ref[...].astype(F32)
        sgr = _sigmoid(gr)
        silu_g = gr * sgr
        u_rec = (o_g * silu_g).astype(BF16)
        y_rec = _dot(u_rec, w_rec)
        mc = mc_ref[...].astype(F32)
        mr = mr_ref[...].astype(F32)
        smc = _sigmoid(mc)
        smr = _sigmoid(mr)
        merged = (smc * y_conv + smr * y_rec).astype(BF16)
        res = x_ref[...] + _dot(merged, w_out)
        r2 = lax.rsqrt(jnp.mean(res * res, axis=-1, keepdims=True) + EPS)
        xh2 = res * r2
        row = lax.broadcasted_iota(jnp.int32, (tm, 1), 0) + i * tm
        real = row >= ROW0
        diff = jnp.where(real, xh2 * fg_ref[...] - t_ref[...], 0.0)
        d_y = diff * (1.0 / D)
        d_xh2 = d_y * fg_ref[...]
        d_res = r2 * (d_xh2 - xh2 * jnp.mean(d_xh2 * xh2, axis=-1, keepdims=True))
        dres_ref[...] = d_res
        d_res_b = d_res.astype(BF16)
        d_merged = _dot_nt(d_res_b, w_out)
        d_yc = (d_merged * smc).astype(BF16)
        d_yr = (d_merged * smr).astype(BF16)
        dp_ref[:, D:2 * D] = (d_merged * y_conv * smc * (1.0 - smc)).astype(BF16)
        dp_ref[:, 2 * D:3 * D] = (d_merged * y_rec * smr * (1.0 - smr)).astype(BF16)
        d_ur = _dot_nt(d_yr, w_rec)
        d_og = d_ur * silu_g
        dp_ref[:, 0:D] = (d_ur * o_g * _dsilu(gr, sgr)).astype(BF16)
        d_on = d_og * gng_ref[...]
        for h in range(HEADS):
            sl = slice(h * HEAD_W, (h + 1) * HEAD_W)
            d_h = d_on[:, sl]
            n_h = o_n[:, sl]
            don_sc[:, sl] = r3[h] * (d_h - n_h * jnp.mean(d_h * n_h, axis=-1, keepdims=True))
        do_ref[...] = don_sc[...]
        d_uc = _dot_nt(d_yc, w_conv)
        d_c2 = d_uc * silu_z
        dz_ref[...] = (d_uc * c2 * _dsilu(z, sz)).astype(BF16)
        d_c1 = d_c2 * _dsilu(c1, s1)
        d_xh = d_c1 * lng_ref[...]
        d_c0 = rstd * (d_xh - jnp.mean(d_xh, axis=-1, keepdims=True)
                       - xh * jnp.mean(d_xh * xh, axis=-1, keepdims=True))
        dc0_ref[...] = d_c0
        a3_ref[0] = u_conv
        b3_ref[0] = d_yc
        a3_ref[1] = u_rec
        b3_ref[1] = d_yr
        a3_ref[2] = merged
        b3_ref[2] = d_res_b
        def colsum(vv):
            return jnp.sum(vv, axis=0, keepdims=True)

        red_ref[0:1, :] += colsum(d_y * xh2)
        red_ref[1:2, :] += colsum(d_og * o_n)
        red_ref[2:3, :] += colsum(d_c1 * xh)
        red_ref[3:4, :] += colsum(d_c1)
        red_ref[4:5, :] += colsum(d_c0)
        red_ref[5:6, :] += colsum(diff * diff) * (0.5 / D)

    def row_block(width, col):
        return pl.BlockSpec((tm, width), lambda i: (i, col))

    def const_block(shape):
        return pl.BlockSpec(shape, lambda i: (0,) * len(shape))

    stack = jax.ShapeDtypeStruct((3, TP, D), BF16)
    stack_spec = pl.BlockSpec((3, tm, D), lambda i: (0, i, 0))
    return pl.pallas_call(
        body, name="mid", grid=(TP // tm,),
        in_specs=[row_block(D, 0), row_block(D, 0), row_block(D, 0), row_block(D, 0),
                  row_block(D, 2), row_block(D, 6), row_block(D, 7), row_block(D, 8),
                  pl.BlockSpec((3, D, D), lambda i: (0, 0, 0), pipeline_mode=pl.Buffered(1)),
                  const_block((1, D)), const_block((1, D)), const_block((1, D)), const_block((1, D))],
        out_specs=(row_block(D, 0), row_block(D, 0), row_block(D, 0), row_block(D, 0), row_block(3 * D, 2),
                   stack_spec, stack_spec, const_block((8, D))),
        out_shape=(jax.ShapeDtypeStruct((TP, D), F32), jax.ShapeDtypeStruct((TP, D), F32),
                   jax.ShapeDtypeStruct((TP, D), F32), jax.ShapeDtypeStruct((TP, D), BF16),
                   jax.ShapeDtypeStruct((TP, D_IN), BF16), stack, stack, jax.ShapeDtypeStruct((8, D), F32)),
        scratch_shapes=[pltpu.VMEM((tm, D), F32), pltpu.VMEM((tm, D), F32)],
        compiler_params=pltpu.CompilerParams(dimension_semantics=("arbitrary",), vmem_limit_bytes=60 * 1024 * 1024),
    )(xin, tgt, o, c0, proj, proj, proj, proj, w3, ln_g, ln_b, gnorm_g, final_g)


def _conv_bwd(proj, d_c0, d_z, conv_w, dproj):
    tm = TM_ELT
    n_tile = TP // tm
    lastt = n_tile - 1

    strip = tm // CONV_STRIPS

    def body(p_ref, dc_ref, dz_ref, w_ref, dproj_in, dp_ref, dw_ref, dsh, a_sc, da_sc, acc):
        del dproj_in
        i = pl.program_id(0)

        @pl.when(i == 0)
        def _():
            dsh[0, :, tm:tm + HALO, :] = jnp.zeros((N_CB, HALO, HEAD_W), F32)
            acc[...] = jnp.zeros_like(acc)

        @pl.when(i > 0)
        def _():
            dsh[0, :, tm:tm + HALO, :] = dsh[0, :, 0:HALO, :]

        _store_by_cb(dsh, (0,), slice(0, tm), dc_ref[...])
        _fill_shifts(dsh, tm)
        ga = p_ref[:, 0:D].astype(F32)
        sb = _sigmoid(p_ref[:, D:2 * D].astype(F32))
        a = ga * sb
        _store_by_cb(a_sc, (), slice(0, tm), a)
        for cb in range(N_CB):
            cs = slice(cb * HEAD_W, (cb + 1) * HEAD_W)
            for st in range(CONV_STRIPS):
                rows = slice(st * strip, (st + 1) * strip)
                a_s = a_sc[cb, rows, :]
                d_a = jnp.zeros((strip, HEAD_W), F32)
                for j in range(CONV_K):
                    off = CONV_K - 1 - j
                    lo = st * strip + 8 * (off // 8)
                    slab = dsh[off % 8, cb, lo:lo + strip, :]
                    d_a = d_a + w_ref[j:j + 1, cs] * slab
                    acc[j, :, cs] += jnp.sum((a_s * slab).reshape(strip // 8, 8, HEAD_W), axis=0)
                da_sc[rows, cs] = d_a
        d_a = da_sc[...]
        dp_ref[:, 0:D] = (d_a * sb).astype(BF16)
        dp_ref[:, D:2 * D] = (d_a * a * (1.0 - sb)).astype(BF16)
        dp_ref[:, 2 * D:3 * D] = dz_ref[...]

        @pl.when(i == lastt)
        def _():
            for j in range(CONV_K):
                dw_ref[j:j + 1, :] = jnp.sum(acc[j], axis=0, keepdims=True)
            dw_ref[CONV_K:CONV_K + 1, :] = jnp.zeros((1, D), F32)

    return pl.pallas_call(
        body, name="conv_bwd", grid=(n_tile,),
        in_specs=[pl.BlockSpec((tm, 2 * D), lambda i: (lastt - i, 0)), pl.BlockSpec((tm, D), lambda i: (lastt - i, 0)),
                  pl.BlockSpec((tm, D), lambda i: (lastt - i, 0)), pl.BlockSpec((CONV_K, D), lambda i: (0, 0)), ANY],
        out_specs=(pl.BlockSpec((tm, 3 * D), lambda i: (lastt - i, 0)), pl.BlockSpec((CONV_K + 1, D), lambda i: (0, 0))),
        out_shape=(jax.ShapeDtypeStruct((TP, D_IN), BF16), jax.ShapeDtypeStruct((CONV_K + 1, D), F32)),
        scratch_shapes=[pltpu.VMEM((8, N_CB, tm + HALO, HEAD_W), F32), pltpu.VMEM((N_CB, tm, HEAD_W), F32),
                        pltpu.VMEM((tm, D), F32), pltpu.VMEM((CONV_K, 8, D), F32)],
        input_output_aliases={4: 0},
        compiler_params=pltpu.CompilerParams(dimension_semantics=("arbitrary",)),
    )(proj, d_c0, d_z, conv_w, dproj)


def _wgrad3(a3, b3):
    tt = TM_MAT

    def body(a_ref, b_ref, o_ref):
        @pl.when(pl.program_id(1) == 0)
        def _():
            o_ref[...] = jnp.zeros_like(o_ref)

        o_ref[0] += _dot_tn(a_ref[0], b_ref[0])

    return pl.pallas_call(
        body, name="wgrad3", grid=(3, TP // tt),
        in_specs=[pl.BlockSpec((1, tt, D), lambda g, t: (g, t, 0)), pl.BlockSpec((1, tt, D), lambda g, t: (g, t, 0))],
        out_specs=pl.BlockSpec((1, D, D), lambda g, t: (g, 0, 0)),
        out_shape=jax.ShapeDtypeStruct((3, D, D), F32),
        compiler_params=pltpu.CompilerParams(dimension_semantics=("arbitrary", "arbitrary")),
    )(a3, b3)


def _wgrad_in(h, dproj, ids, p3):
    tt = TM_MAT
    n_t = TP // tt

    def body(ids_ref, a_ref, b_ref, p3_ref, o_ref, ob_ref, l0_ref, l1_ref, acc, tmp, send_sems, recv_sems, tmp_sem):
        del ids_ref
        r = pl.program_id(0)
        t = pl.program_id(1)
        x, y, c = _my_place()
        sibling = (x, y, 1 - c)
        slot = lax.rem(r, 2)

        def send_in(q):
            return pltpu.make_async_remote_copy(
                src_ref=acc.at[q % 2], dst_ref=l0_ref.at[q], send_sem=send_sems.at[q], recv_sem=recv_sems.at[q],
                device_id=sibling, device_id_type=MESH_ID)

        def send_3(q):
            d = _dev_index(*_chip_rel(x, y, q), 1 - c)
            return pltpu.make_async_remote_copy(
                src_ref=p3_ref.at[:, pl.ds(pl.multiple_of(d * W_ROW_BLK, W_ROW_BLK), W_ROW_BLK), :],
                dst_ref=l1_ref.at[q], send_sem=send_sems.at[4 + q], recv_sem=recv_sems.at[4 + q],
                device_id=sibling, device_id_type=MESH_ID)

        def landed(q):
            return pltpu.make_async_copy(l0_ref.at[q], tmp, tmp_sem)

        @pl.when((r == 0) & (t == 0))
        def _():
            for q in range(4):
                send_3(q).start()

        @pl.when(t == 0)
        def _():
            acc[slot] = jnp.zeros((D, W_IN_BLK), F32)

        acc[slot] += _dot_tn(a_ref[...], b_ref[...])

        for q in range(4):
            @pl.when((r == q) & (t == n_t - 1))
            def _(q=q):
                if q >= 1:
                    send_in(q - 1).wait_send()
                send_in(q).start()

            @pl.when((r == 4 + q) & (t == n_t - 2))
            def _(q=q):
                if q == 0:
                    send_in(3).wait_send()
                send_in(q).wait_recv()
                landed(q).start()

            @pl.when((r == 4 + q) & (t == n_t - 1))
            def _(q=q):
                landed(q).wait()
                tot = acc[q % 2] + tmp[...]
                o_ref[0] = tot
                ob_ref[0] = tot.astype(BF16)

        @pl.when((r == 7) & (t == n_t - 1))
        def _():
            for q in range(4):
                send_3(q).wait_recv()
            for q in range(4):
                send_3(q).wait_send()

    blk = pl.BlockSpec((1, D, W_IN_BLK), lambda r, t, ids: (jnp.maximum(r - 4, 0), 0, 0))
    return pl.pallas_call(
        body, name="wgrad_in",
        grid_spec=pltpu.PrefetchScalarGridSpec(
            num_scalar_prefetch=1, grid=(N_DEV, n_t),
            in_specs=[pl.BlockSpec((tt, D), lambda r, t, ids: (t, 0)),
                      pl.BlockSpec((tt, W_IN_BLK), lambda r, t, ids: (t, ids[r])), ANY],
            out_specs=(blk, blk, ANY, ANY),
            scratch_shapes=[pltpu.VMEM((2, D, W_IN_BLK), F32), pltpu.VMEM((D, W_IN_BLK), F32),
                            pltpu.SemaphoreType.DMA((8,)), pltpu.SemaphoreType.DMA((8,)), pltpu.SemaphoreType.DMA]),
        out_shape=(jax.ShapeDtypeStruct((4, D, W_IN_BLK), F32), jax.ShapeDtypeStruct((4, D, W_IN_BLK), BF16),
                   jax.ShapeDtypeStruct((4, D, W_IN_BLK), F32), jax.ShapeDtypeStruct((4, 3, W_ROW_BLK, D), F32)),
        compiler_params=pltpu.CompilerParams(dimension_semantics=("arbitrary", "arbitrary")),
    )(ids, h, dproj, p3)


def _chip_sum_3(p3, land1, ids_mine):
    def body(ids_ref, p_ref, l_ref, o_ref, ob_ref):
        del ids_ref
        tot = p_ref[...] + l_ref[0]
        o_ref[0] = tot
        ob_ref[0] = tot.astype(BF16)

    blk = pl.BlockSpec((1, 3, W_ROW_BLK, D), lambda r, ids: (r, 0, 0, 0))
    return pl.pallas_call(
        body, name="chip_sum_3",
        grid_spec=pltpu.PrefetchScalarGridSpec(
            num_scalar_prefetch=1, grid=(4,),
            in_specs=[pl.BlockSpec((3, W_ROW_BLK, D), lambda r, ids: (0, ids[r], 0)), blk],
            out_specs=(blk, blk)),
        out_shape=(jax.ShapeDtypeStruct((4, 3, W_ROW_BLK, D), F32), jax.ShapeDtypeStruct((4, 3, W_ROW_BLK, D), BF16)),
    )(ids_mine, p3, land1)


def _dh_and_norm_bwd(dproj, w_in_full, xin, d_res, norm_g, chip0b, chip1b):
    tm = TM_MAT
    n_k = N_DEV
    n_m = TP // tm

    def body(dp_ref, w_ref, x_ref, dr_ref, g_ref, c0_ref, c1_ref, dx_ref, dg_ref, f0_ref, f1_ref, acc,
             send_sems, recv_sems):
        m = pl.program_id(0)
        k = pl.program_id(1)
        x, y, c = _my_place()

        def to_owner(a, q):
            src, dst = ((c0_ref, f0_ref), (c1_ref, f1_ref))[a]
            return pltpu.make_async_remote_copy(
                src_ref=src.at[q], dst_ref=dst.at[q - 1], send_sem=send_sems.at[a * 3 + q - 1],
                recv_sem=recv_sems.at[a * 3 + q - 1], device_id=(*_chip_rel(x, y, q), c), device_id_type=MESH_ID)

        @pl.when((m == 0) & (k == 0))
        def _():
            for q in range(1, 4):
                for a in range(2):
                    to_owner(a, q).start()

        @pl.when(k == 0)
        def _():
            acc[...] = jnp.zeros_like(acc)

        acc[...] += _dot_nt(dp_ref[...], w_ref[0])

        @pl.when((k == n_k - 1) & (m == 0))
        def _():
            dg_ref[...] = jnp.zeros_like(dg_ref)

        @pl.when(k == n_k - 1)
        def _():
            xv = x_ref[...]
            r1 = lax.rsqrt(jnp.mean(xv * xv, axis=-1, keepdims=True) + EPS)
            xh = xv * r1
            d_h = acc[...]
            dg_ref[0:1, :] += jnp.sum(d_h * xh, axis=0, keepdims=True)
            d_xh = d_h * g_ref[...]
            dx_ref[...] = dr_ref[...] + r1 * (d_xh - xh * jnp.mean(d_xh * xh, axis=-1, keepdims=True))

        @pl.when((m == n_m - 1) & (k == n_k - 1))
        def _():
            for q in range(1, 4):
                for a in range(2):
                    to_owner(a, q).wait_recv()
            for q in range(1, 4):
                for a in range(2):
                    to_owner(a, q).wait_send()

    return pl.pallas_call(
        body, name="dh_norm_bwd", grid=(n_m, n_k),
        in_specs=[pl.BlockSpec((tm, W_IN_BLK), lambda m, k: (m, k)), pl.BlockSpec((1, D, W_IN_BLK), lambda m, k: (k, 0, 0)),
                  pl.BlockSpec((tm, D), lambda m, k: (m, 0)), pl.BlockSpec((tm, D), lambda m, k: (m, 0)),
                  pl.BlockSpec((1, D), lambda m, k: (0, 0)), ANY, ANY],
        out_specs=(pl.BlockSpec((tm, D), lambda m, k: (m, 0)), pl.BlockSpec((8, D), lambda m, k: (0, 0)), ANY, ANY),
        out_shape=(jax.ShapeDtypeStruct((TP, D), F32), jax.ShapeDtypeStruct((8, D), F32),
                   jax.ShapeDtypeStruct((3, D, W_IN_BLK), BF16), jax.ShapeDtypeStruct((3, 3, W_ROW_BLK, D), BF16)),
        scratch_shapes=[pltpu.VMEM((tm, D), F32), pltpu.SemaphoreType.DMA((6,)), pltpu.SemaphoreType.DMA((6,))],
        compiler_params=pltpu.CompilerParams(dimension_semantics=("arbitrary", "arbitrary")),
    )(dproj, w_in_full, xin, d_res, norm_g, chip0b, chip1b)


def _sum_adamw(own, landed, w, m, v, tr, name):
    rows, cols = w.shape
    n_t = rows // tr

    def body(o_ref, l1_ref, l2_ref, l3_ref, w_ref, m_ref, v_ref, g_ref, d_ref, m2_ref, v2_ref):
        g = ((o_ref[...] + l1_ref[...].astype(F32)) + l2_ref[...].astype(F32)) + l3_ref[...].astype(F32)
        delta, m2, v2 = _adamw(w_ref[...], g, m_ref[...], v_ref[...])
        g_ref[...] = g
        d_ref[...] = delta
        m2_ref[...] = m2
        v2_ref[...] = v2

    def spec(k):
        return pl.BlockSpec((tr, cols), lambda i: (i + k * n_t, 0))

    out = jax.ShapeDtypeStruct((rows, cols), F32)
    return pl.pallas_call(
        body, name=name, grid=(n_t,),
        in_specs=[spec(0), spec(0), spec(1), spec(2), spec(0), spec(0), spec(0)],
        out_specs=(spec(0),) * 4, out_shape=(out,) * 4,
    )(own, landed, landed, landed, w, m, v)


def _small_update(pack_all, srs_all, lb_logits, p8, m8, v8, ws, ms, vs):
    def body(pk_ref, sr_ref, lbl_ref, p_ref, m_ref, v_ref, ws_ref, ms_ref, vs_ref,
             g8_ref, d8_ref, m8_ref, v8_ref, loss_ref, gs_ref, ds_ref, ms2_ref, vs2_ref):
        tot = pk_ref[0]
        tot_s = sr_ref[0]
        for d in range(1, N_DEV):
            tot = tot + pk_ref[d]
            tot_s = tot_s + sr_ref[d]
        p0 = _sigmoid(lbl_ref[0:1, :] - lbl_ref[1:2, :])
        row = lax.broadcasted_iota(jnp.int32, (8, D), 0)
        d_lb = jnp.sum(jnp.where(row == 4, tot, 0.0), axis=0, keepdims=True)
        d_l0 = d_lb * p0 * (1.0 - p0)
        loss_ref[...] = jnp.sum(jnp.where(row == 5, tot, 0.0), keepdims=True).reshape(1, 1)
        g8 = jnp.where(row == 4, d_l0, jnp.where(row == 5, -d_l0, tot))
        delta, m2, v2 = _adamw(p_ref[...], g8, m_ref[...], v_ref[...])
        g8_ref[...] = g8
        d8_ref[...] = delta
        m8_ref[...] = m2
        v8_ref[...] = v2
        delta, m2, v2 = _adamw(ws_ref[...], tot_s, ms_ref[...], vs_ref[...])
        gs_ref[...] = tot_s
        ds_ref[...] = delta
        ms2_ref[...] = m2
        vs2_ref[...] = v2

    o8 = jax.ShapeDtypeStruct((8, D), F32)
    os_ = jax.ShapeDtypeStruct((SMALL_ROWS, HEAD_W), F32)
    return pl.pallas_call(
        body, name="small_update",
        out_shape=(o8, o8, o8, o8, jax.ShapeDtypeStruct((1, 1), F32), os_, os_, os_, os_),
    )(pack_all, srs_all, lb_logits, p8, m8, v8, ws, ms, vs)


def _local_step(xin, proj, target, conv_w_full, conv_b, ln_g, ln_b, w3_b, lb_logits, gnorm_g, final_g):
    tgt = jnp.concatenate([jnp.zeros((ROW0, D), F32), target], axis=0)
    fg = final_g.reshape(1, D)
    c0 = _conv_fwd(proj, conv_w_full, conv_b)
    o, s_start, w3_full = _rec_fwd(proj, lb_logits, w3_b)
    d_res, d_o, d_c0, d_z, dproj, a3, b3, red = _mid(xin, tgt, o, c0, proj, w3_full, ln_g, ln_b, gnorm_g, fg)
    dproj, dlb = _rec_bwd(proj, lb_logits, d_o, s_start, dproj)
    dproj, d_conv_w = _conv_bwd(proj, d_c0, d_z, conv_w_full, dproj)
    p3 = _wgrad3(a3, b3)
    return dproj, d_res, p3, d_conv_w, red, dlb


def kernel(x, meta_tokens, norm_g, w_in, conv_w, conv_b, ln_g, ln_b, w_conv_out, lb_logits, gnorm_g, w_rec_out, w_out, final_g, loss_target, m_meta_tokens, m_norm_g, m_w_in, m_conv_w, m_conv_b, m_ln_g, m_ln_b, m_w_conv_out, m_lb_logits, m_gnorm_g, m_w_rec_out, m_w_out, m_final_g, v_meta_tokens, v_norm_g, v_w_in, v_conv_w, v_conv_b, v_ln_g, v_ln_b, v_w_conv_out, v_lb_logits, v_gnorm_g, v_w_rec_out, v_w_out, v_final_g):
    def small_pack(cw, mt):
        return jnp.concatenate([cw[0], jnp.zeros((1, HEAD_W), F32), mt], axis=0)

    def stack3(a, b, c):
        return jnp.concatenate([a, b, c], axis=0)

    def stack8(ng, cb, lg, lb_, lbl, gg, fg):
        return jnp.concatenate([ng, cb, lg, lb_, lbl, gg, fg.reshape(1, D)], axis=0)

    mx, my, mc = _my_place()

    w3_s = stack3(w_conv_out, w_rec_out, w_out)
    ws_s = small_pack(conv_w, meta_tokens)
    small_full = jnp.transpose(_gather_small(ws_s), (1, 0, 2)).reshape(SMALL_ROWS, D)
    conv_w_full = small_full[0:CONV_K]
    meta_full = small_full[META_ROW:META_ROW + N_META]
    xin = jnp.concatenate([jnp.zeros((PAD_FRONT, D), F32), meta_full, x[0]], axis=0)
    w_in_b, w3_b = _cast_shards(w_in[0], w3_s)
    use_order = [(mx, my, mc), (mx, my, 1 - mc)]
    for chip in ((1 - mx, my), (mx, 1 - my), (1 - mx, 1 - my)):
        use_order += [(*chip, mc), (*chip, 1 - mc)]
    order = jnp.stack([_dev_index(*p) for p in use_order]).astype(jnp.int32)
    proj, h, w_in_full = _gather_and_proj(xin, norm_g, w_in_b, order)
    h = h.reshape(TP, D)

    dproj, d_res, p3, d_conv_w, red, dlb = _local_step(
        xin, proj, loss_target[0], conv_w_full, conv_b, ln_g, ln_b, w3_b, lb_logits, gnorm_g, final_g)

    ids_mine = jnp.stack([_dev_index(*_chip_rel(mx, my, r), mc) for r in range(4)]).astype(jnp.int32)
    ids_sib = jnp.stack([_dev_index(*_chip_rel(mx, my, r), 1 - mc) for r in range(4)]).astype(jnp.int32)
    chip0, chip0b, _, land1 = _wgrad_in(h, dproj, jnp.concatenate([ids_sib, ids_mine]), p3)
    chip1, chip1b = _chip_sum_3(p3, land1, ids_mine)
    d_xin, dng, far0, far1 = _dh_and_norm_bwd(dproj, w_in_full, xin, d_res, norm_g, chip0b, chip1b)
    pack = jnp.concatenate([dng[0:1], red[4:5], red[2:3], red[3:4], dlb[0:1], red[5:6], red[1:2], red[0:1]], axis=0)
    g_in, d_in, m_in, v_in = _sum_adamw(chip0.reshape(4 * D, W_IN_BLK), far0.reshape(3 * D, W_IN_BLK), w_in[0],
                                        m_w_in[0], v_w_in[0], 256, "adamw_in")
    g_3, d_3, m_3, v_3 = _sum_adamw(
        chip1.reshape(12 * W_ROW_BLK, D), far1.reshape(9 * W_ROW_BLK, D), w3_s.reshape(3 * W_ROW_BLK, D),
        stack3(m_w_conv_out, m_w_rec_out, m_w_out).reshape(3 * W_ROW_BLK, D),
        stack3(v_w_conv_out, v_w_rec_out, v_w_out).reshape(3 * W_ROW_BLK, D), 3 * W_ROW_BLK, "adamw_3")

    srs = jnp.concatenate([d_conv_w, d_xin[PAD_FRONT:ROW0]], axis=0)
    srs = jnp.transpose(srs.reshape(SMALL_ROWS, N_DEV, HEAD_W), (1, 0, 2))
    pack_all, srs_all = _exchange_small(pack, srs)
    g8, d8, m8, v8, loss, gs, ds, ms, vs = _small_update(
        pack_all, srs_all, lb_logits,
        stack8(norm_g, conv_b, ln_g, ln_b, lb_logits, gnorm_g, final_g),
        stack8(m_norm_g, m_conv_b, m_ln_g, m_ln_b, m_lb_logits, m_gnorm_g, m_final_g),
        stack8(v_norm_g, v_conv_b, v_ln_g, v_ln_b, v_lb_logits, v_gnorm_g, v_final_g),
        ws_s, small_pack(m_conv_w, m_meta_tokens), small_pack(v_conv_w, v_meta_tokens))

    def unpack(a_in, a_3, a_s, a_8):
        t3 = a_3.reshape(3, 1, W_ROW_BLK, D)
        return (a_s[META_ROW:META_ROW + N_META], a_8[0:1], a_in[None], a_s[0:CONV_K][None], a_8[1:2], a_8[2:3],
                a_8[3:4], t3[0], a_8[4:6], a_8[6:7], t3[1], t3[2], a_8[7])

    grad_x = d_xin[ROW0:][None]
    return (loss.reshape(()), grad_x, *unpack(g_in, g_3, gs, g8), *unpack(d_in, d_3, ds, d8),
            *unpack(m_in, m_3, ms, m8), *unpack(v_in, v_3, vs, v8))
```

```python
import functools

import jax
import jax.numpy as jnp
from jax import lax
from jax.experimental import pallas as pl
from jax.experimental.pallas import tpu as pltpu

F32 = jnp.float32
BF16 = jnp.bfloat16

D = 1024
SEQ = 4096
N_META = 16
CHUNK = 64
PAD_FRONT = 48
ROW0 = PAD_FRONT + N_META
TP = ROW0 + SEQ
N_CHUNK = TP // CHUNK
HEADS = 8
HEAD_W = 128
D_IN = 9 * D
N_DEV = 8
W_IN_BLK = D_IN // N_DEV
W_ROW_BLK = D // N_DEV
CONV_K = 31
SMALL_ROWS = 48
META_ROW = 32
EPS = 1e-6
HALO = 32

TM_MAT = 832
TM_ELT = 208
CHUNKS_PER_STEP = 5
CONV_STRIPS = 2

ADAM_LR = 0.001
ADAM_B1 = 0.9
ADAM_B2 = 0.999
ADAM_EPS = 1e-08
ADAM_WD = 0.01
ADAM_STEP = 10

MESH_ID = pl.DeviceIdType.MESH
ANY = pl.BlockSpec(memory_space=pl.ANY)


def _sigmoid(v):
    return jax.nn.sigmoid(v)


def _dsilu(v, s):
    return s * (1.0 + v * (1.0 - s))


def _dot(a, b):
    return jnp.dot(a, b, preferred_element_type=F32)


def _dot_nt(a, b):
    return lax.dot_general(a, b, (((1,), (1,)), ((), ())), preferred_element_type=F32)


def _dot_tn(a, b):
    return lax.dot_general(a, b, (((0,), (0,)), ((), ())), preferred_element_type=F32)


def _split3(v):
    hi = v.astype(BF16)
    r1 = v - hi.astype(F32)
    mid = r1.astype(BF16)
    lo = (r1 - mid.astype(F32)).astype(BF16)
    return hi, mid, lo


def _tri_matmul(tri, v):
    hi, mid, lo = _split3(v)
    return _dot(tri, hi) + _dot(tri, mid) + _dot(tri, lo)


def _adamw(w, g, m, v):
    m2 = ADAM_B1 * m + (1.0 - ADAM_B1) * g
    v2 = ADAM_B2 * v + (1.0 - ADAM_B2) * jnp.square(g)
    m_hat = m2 / (1.0 - ADAM_B1 ** ADAM_STEP)
    v_hat = v2 / (1.0 - ADAM_B2 ** ADAM_STEP)
    delta = -ADAM_LR * (m_hat / (jnp.sqrt(v_hat) + ADAM_EPS) + ADAM_WD * w)
    return delta, m2, v2


def _window_start(i, tm):
    assert tm % 16 == 0 and ROW0 % 16 == 0
    return pl.multiple_of(16 * jnp.maximum((tm // 16) * i - ROW0 // 16, 0), 16)


def _my_place():
    return lax.axis_index("x"), lax.axis_index("y"), lax.axis_index("c")


def _dev_index(px, py, pc):
    return 4 * px + 2 * py + pc


def _cast_shards(w_in_s, w3_s):
    def body(a_ref, b_ref, oa_ref, ob_ref):
        oa_ref[...] = a_ref[...].astype(BF16)
        ob_ref[...] = b_ref[...].astype(BF16)

    return pl.pallas_call(
        body, name="cast_shards",
        out_shape=(jax.ShapeDtypeStruct(w_in_s.shape, BF16), jax.ShapeDtypeStruct(w3_s.shape, BF16)),
    )(w_in_s, w3_s)


def _peer(x, y, c, r):
    return (jnp.bitwise_xor(x, (r >> 2) & 1), jnp.bitwise_xor(y, (r >> 1) & 1), jnp.bitwise_xor(c, r & 1))


def _gather_small(small_s):
    def body(s_ref, o_ref, send_sems, recv_sems, local_sem):
        x, y, c = _my_place()
        my_id = _dev_index(x, y, c)
        mine = pltpu.make_async_copy(s_ref, o_ref.at[my_id], local_sem)
        mine.start()
        copies = []
        for r in range(1, N_DEV):
            cp = pltpu.make_async_remote_copy(
                src_ref=s_ref, dst_ref=o_ref.at[my_id], send_sem=send_sems.at[r - 1], recv_sem=recv_sems.at[r - 1],
                device_id=_peer(x, y, c, r), device_id_type=MESH_ID)
            cp.start()
            copies.append(cp)
        for cp in copies:
            cp.wait_recv()
        for cp in copies:
            cp.wait_send()
        mine.wait()

    return pl.pallas_call(
        body, name="gather_small", out_shape=jax.ShapeDtypeStruct((N_DEV,) + small_s.shape, F32),
        in_specs=[ANY], out_specs=ANY,
        scratch_shapes=[pltpu.SemaphoreType.DMA((7,)), pltpu.SemaphoreType.DMA((7,)), pltpu.SemaphoreType.DMA],
    )(small_s)


def _w3_gather(src, out, stage, send_sems, recv_sems, local_sems):
    x, y, c = _my_place()
    me, sibling = (x, y, c), (x, y, 1 - c)
    chips = [(1 - x, y), (x, 1 - y), (1 - x, 1 - y)]

    def block(place):
        d = _dev_index(*place)
        return out.at[:, pl.ds(pl.multiple_of(d * W_ROW_BLK, W_ROW_BLK), W_ROW_BLK), :]

    def copy(k, place, to, from_src=False):
        return pltpu.make_async_remote_copy(
            src_ref=src if from_src else block(place), dst_ref=block(place),
            send_sem=send_sems.at[k], recv_sem=recv_sems.at[k], device_id=to, device_id_type=MESH_ID)

    own_in = pltpu.make_async_copy(src, stage, local_sems.at[0])
    own_out = pltpu.make_async_copy(stage, block(me), local_sems.at[1])

    def start():
        copy(0, me, sibling, from_src=True).start()
        for j, chip in enumerate(chips):
            copy(1 + j, me, (*chip, c), from_src=True).start()
        own_in.start()
        own_in.wait()
        own_out.start()

    def finish():
        for j, chip in enumerate(chips):
            copy(1 + j, (*chip, c), me).wait_recv()
            copy(4 + j, (*chip, c), sibling).start()
        copy(0, sibling, me).wait_recv()
        for j, chip in enumerate(chips):
            copy(4 + j, (*chip, 1 - c), me).wait_recv()
        for k in range(7):
            copy(k, me, me).wait_send()
        own_out.wait()

    return start, finish


def _gather_and_proj(x_seq, meta_full, norm_g, w_in_b, order):
    tm = TM_MAT
    n_m = TP // tm
    last_m = n_m - 1

    def body(order_ref, x_ref, meta_ref, g_ref, s0, proj_ref, xin_ref, h_out, o0, hbuf, wbuf, send_sems, recv_sems,
             local_sems):
        del order_ref
        n = pl.program_id(0)
        m = pl.program_id(1)
        x, y, c = _my_place()
        me, sibling = (x, y, c), (x, y, 1 - c)
        chips = [(1 - x, y), (x, 1 - y), (1 - x, 1 - y)]

        def block(place):
            return o0.at[_dev_index(*place)]

        def copy(k, place, to, from_src=False):
            return pltpu.make_async_remote_copy(
                src_ref=s0 if from_src else block(place), dst_ref=block(place),
                send_sem=send_sems.at[k], recv_sem=recv_sems.at[k], device_id=to, device_id_type=MESH_ID)

        def to_vmem(place, slot):
            return pltpu.make_async_copy(block(place), wbuf.at[slot], local_sems.at[slot])

        own_out = pltpu.make_async_copy(wbuf.at[0], block(me), local_sems.at[2])
        h_copy = pltpu.make_async_copy(hbuf, h_out, local_sems.at[3])

        @pl.when((n == 0) & (m == 0))
        def _():
            copy(0, me, sibling, from_src=True).start()
            for j, chip in enumerate(chips):
                copy(1 + j, me, (*chip, c), from_src=True).start()
            mine = pltpu.make_async_copy(s0, wbuf.at[0], local_sems.at[0])
            mine.start()
            mine.wait()
            own_out.start()

        @pl.when(n == 0)
        def _():
            xv = x_ref[...]
            xin_ref[...] = jnp.where(m == 0, pltpu.roll(xv, ROW0, 0), xv)

            @pl.when(m == 0)
            def _():
                xin_ref[0:PAD_FRONT, :] = jnp.zeros((PAD_FRONT, D), F32)
                xin_ref[PAD_FRONT:ROW0, :] = meta_ref[...]

            xv = xin_ref[...]
            r = lax.rsqrt(jnp.mean(xv * xv, axis=-1, keepdims=True) + EPS)
            hbuf[m] = (xv * r * g_ref[...]).astype(BF16)

        plan = [(sibling, (0, sibling), None)]
        for j, chip in enumerate(chips):
            plan.append(((*chip, c), (1 + j, (*chip, c)), 4 + j))
            plan.append(((*chip, 1 - c), (4 + j, (*chip, 1 - c)), None))

        for s, (place, (k, origin), pass_on) in enumerate(plan, start=1):
            @pl.when((n == s - 1) & (m == last_m))
            def _(s=s, place=place, k=k, origin=origin, pass_on=pass_on):
                copy(k, origin, me).wait_recv()
                if pass_on is not None:
                    copy(pass_on, place, sibling).start()
                if s == 2:
                    own_out.wait()
                to_vmem(place, s % 2).start()

            @pl.when((n == s) & (m == 0))
            def _(s=s, place=place):
                to_vmem(place, s % 2).wait()

        proj_ref[...] = _dot(hbuf[m], wbuf[lax.rem(n, 2)]).astype(BF16)

        @pl.when((n == 0) & (m == last_m))
        def _():
            h_copy.start()

        @pl.when((n == N_DEV - 1) & (m == last_m))
        def _():
            for k in range(7):
                copy(k, me, me).wait_send()
            h_copy.wait()

    return pl.pallas_call(
        body, name="gather_and_proj",
        grid_spec=pltpu.PrefetchScalarGridSpec(
            num_scalar_prefetch=1, grid=(N_DEV, n_m),
            in_specs=[pl.BlockSpec((pl.Element(tm), pl.Element(D)),
                                   lambda n, m, o: (_window_start(jnp.where(n == 0, m, 0), tm), 0)),
                      pl.BlockSpec((N_META, D), lambda n, m, o: (0, 0)),
                      pl.BlockSpec((1, D), lambda n, m, o: (0, 0)), ANY],
            out_specs=(pl.BlockSpec((tm, W_IN_BLK), lambda n, m, o: (m, o[n])),
                       pl.BlockSpec((tm, D), lambda n, m, o: (jnp.where(n == 0, m, last_m), 0)), ANY, ANY),
            scratch_shapes=[pltpu.VMEM((n_m, tm, D), BF16), pltpu.VMEM((2, D, W_IN_BLK), BF16),
                            pltpu.SemaphoreType.DMA((7,)), pltpu.SemaphoreType.DMA((7,)),
                            pltpu.SemaphoreType.DMA((4,))]),
        out_shape=(jax.ShapeDtypeStruct((TP, D_IN), BF16), jax.ShapeDtypeStruct((TP, D), F32),
                   jax.ShapeDtypeStruct((n_m, tm, D), BF16), jax.ShapeDtypeStruct((N_DEV, D, W_IN_BLK), BF16)),
        compiler_params=pltpu.CompilerParams(dimension_semantics=("arbitrary", "arbitrary")),
    )(order, x_seq, meta_full, norm_g, w_in_b)


def _chip_rel(x, y, r):
    return (jnp.bitwise_xor(x, r >> 1), jnp.bitwise_xor(y, r & 1))


def _exchange_small(pack, srs):
    def body(pk, sr, pk_all, sr_all, send_sems, recv_sems, local_sems):
        x, y, c = _my_place()
        my_id = _dev_index(x, y, c)
        mine = [pltpu.make_async_copy(pk, pk_all.at[my_id], local_sems.at[0]),
                pltpu.make_async_copy(sr.at[my_id], sr_all.at[my_id], local_sems.at[1])]
        for cp in mine:
            cp.start()
        copies = []
        for r in range(1, N_DEV):
            peer = (jnp.bitwise_xor(x, (r >> 2) & 1), jnp.bitwise_xor(y, (r >> 1) & 1), jnp.bitwise_xor(c, r & 1))
            peer_id = _dev_index(*peer)
            for a, (src, dst) in enumerate(((pk, pk_all.at[my_id]), (sr.at[peer_id], sr_all.at[my_id]))):
                cp = pltpu.make_async_remote_copy(
                    src_ref=src, dst_ref=dst, send_sem=send_sems.at[a * 7 + r - 1], recv_sem=recv_sems.at[a * 7 + r - 1],
                    device_id=peer, device_id_type=MESH_ID)
                cp.start()
                copies.append(cp)
        for cp in copies:
            cp.wait_recv()
        for cp in copies:
            cp.wait_send()
        for cp in mine:
            cp.wait()

    return pl.pallas_call(
        body, name="exchange_small",
        out_shape=(jax.ShapeDtypeStruct((N_DEV,) + pack.shape, F32), jax.ShapeDtypeStruct(srs.shape, F32)),
        in_specs=[ANY, ANY], out_specs=(ANY, ANY),
        scratch_shapes=[pltpu.SemaphoreType.DMA((14,)), pltpu.SemaphoreType.DMA((14,)), pltpu.SemaphoreType.DMA((2,))],
    )(pack, srs)


N_CB = D // HEAD_W


def _store_by_cb(ref, idx, rows, val):
    for cb in range(N_CB):
        ref[(*idx, cb, rows, slice(None))] = val[:, cb * HEAD_W:(cb + 1) * HEAD_W]


def _fill_shifts(sh, tm):
    n = tm + HALO - 8
    for s in range(1, 8):
        for cb in range(N_CB):
            sh[s, cb, 0:n, :] = sh[0, cb, s:s + n, :]


def _conv_fwd(proj, conv_w, conv_b):
    tm = TM_ELT
    strip = tm // CONV_STRIPS

    def body(p_ref, w_ref, b_ref, c0_ref, sh):
        i = pl.program_id(0)

        @pl.when(i == 0)
        def _():
            sh[0, :, 0:HALO, :] = jnp.zeros((N_CB, HALO, HEAD_W), F32)

        @pl.when(i > 0)
        def _():
            sh[0, :, 0:HALO, :] = sh[0, :, tm:tm + HALO, :]

        ga = p_ref[:, 0:D].astype(F32)
        gb = p_ref[:, D:2 * D].astype(F32)
        _store_by_cb(sh, (0,), slice(HALO, HALO + tm), ga * _sigmoid(gb))
        _fill_shifts(sh, tm)
        for cb in range(N_CB):
            cs = slice(cb * HEAD_W, (cb + 1) * HEAD_W)
            for st in range(CONV_STRIPS):
                acc = jnp.broadcast_to(b_ref[:, cs], (strip, HEAD_W))
                for j in range(CONV_K):
                    off = HALO - (CONV_K - 1) + j
                    lo = st * strip + 8 * (off // 8)
                    acc = acc + w_ref[j:j + 1, cs] * sh[off % 8, cb, lo:lo + strip, :]
                c0_ref[st * strip:(st + 1) * strip, cs] = acc

    return pl.pallas_call(
        body, name="conv_fwd", grid=(TP // tm,),
        in_specs=[pl.BlockSpec((tm, 2 * D), lambda i: (i, 0)), pl.BlockSpec((CONV_K, D), lambda i: (0, 0)),
                  pl.BlockSpec((1, D), lambda i: (0, 0))],
        out_specs=pl.BlockSpec((tm, D), lambda i: (i, 0)),
        out_shape=jax.ShapeDtypeStruct((TP, D), F32),
        scratch_shapes=[pltpu.VMEM((8, N_CB, HALO + tm, HEAD_W), F32)],
        compiler_params=pltpu.CompilerParams(dimension_semantics=("arbitrary",)),
    )(proj, conv_w, conv_b)


def _gates(p_ref, lbl_ref, chunk, bsc):
    lb = _sigmoid(lbl_ref[0:1, :] - lbl_ref[1:2, :])
    q_raw = p_ref[:, 0:D].astype(F32)
    f_raw = p_ref[:, D:2 * D].astype(F32)
    sq = _sigmoid(q_raw)
    q = q_raw * sq
    sg = _sigmoid(f_raw)
    f = lb + (1.0 - lb) * sg
    row = lax.broadcasted_iota(jnp.int32, (CHUNK, 1), 0) + chunk * CHUNK
    valid = row >= PAD_FRONT
    lf = jnp.where(valid, jnp.log(f), 0.0)
    k = jnp.where(valid, 1.0 - f, 0.0)
    r_i = lax.broadcasted_iota(jnp.int32, (CHUNK, CHUNK), 0)
    c_i = lax.broadcasted_iota(jnp.int32, (CHUNK, CHUNK), 1)
    causal = r_i >= c_i
    bsc[...] = _tri_matmul(causal.astype(BF16), lf)
    b = bsc[...]
    b_mid = bsc[CHUNK // 2 - 1:CHUNK // 2, :]
    b_last = bsc[CHUNK - 1:CHUNK, :]
    e_q = jnp.exp(b)
    e_qm = jnp.exp(b - b_mid)
    e_km = jnp.exp(b_mid - b)
    e_kh = jnp.exp(b_last - b)
    e_last = jnp.exp(b_last)
    return dict(lb=lb, q_raw=q_raw, sq=sq, q=q, sg=sg, f=f, k=k, valid=valid, causal=causal,
                e_q=e_q, e_qm=e_qm, e_km=e_km, e_kh=e_kh, e_last=e_last)


def _rec_fwd(proj, lb_logits, w3_b):
    cps = CHUNKS_PER_STEP
    rows = cps * CHUNK

    def body(p_ref, lbl_ref, w3s_ref, o_ref, s_ref, w3o_ref, st, bsc, w3buf, send_sems, recv_sems, local_sems):
        n = pl.program_id(0)
        gather_start, gather_finish = _w3_gather(w3s_ref, w3o_ref, w3buf, send_sems, recv_sems, local_sems)

        @pl.when(n == 0)
        def _():
            st[...] = jnp.zeros_like(st)
            gather_start()

        def prep(ci):
            g = _gates(p_ref.at[pl.ds(ci * CHUNK, CHUNK)], lbl_ref, n * cps + ci, bsc.at[ci])
            g["q1"] = (g["q"] * g["e_q"]).astype(BF16)
            g["qm"] = (g["q"] * g["e_qm"]).astype(BF16)
            g["km"] = (g["k"] * g["e_km"]).astype(BF16)
            g["kh"] = (g["k"] * g["e_kh"]).astype(BF16)
            return g

        def heads(ci, g):
            rs = pl.ds(ci * CHUNK, CHUNK)
            pv = p_ref.at[rs]
            s_ref[ci] = st[...]
            for h in range(HEADS):
                sl = slice(h * HEAD_W, (h + 1) * HEAD_W)
                v = pv[:, 2 * D + h * HEAD_W:2 * D + (h + 1) * HEAD_W]
                att = jnp.where(g["causal"], _dot_nt(g["qm"][:, sl], g["km"][:, sl]), 0.0).astype(BF16)
                s_h = st[h]
                o_ref[rs, sl] = _dot_nt(g["q1"][:, sl], s_h.astype(BF16)) + _dot(att, v)
                st[h] = s_h * g["e_last"][:, sl] + _dot_tn(v, g["kh"][:, sl])

        ready = prep(0)
        for ci in range(cps):
            coming = prep(ci + 1) if ci + 1 < cps else None
            heads(ci, ready)
            ready = coming

        @pl.when(n == N_CHUNK // cps - 1)
        def _():
            gather_finish()

    return pl.pallas_call(
        body, name="rec_fwd", grid=(N_CHUNK // cps,),
        in_specs=[pl.BlockSpec((rows, 3 * D), lambda n: (n, 1)), pl.BlockSpec((2, D), lambda n: (0, 0)), ANY],
        out_specs=(pl.BlockSpec((rows, D), lambda n: (n, 0)),
                   pl.BlockSpec((cps, HEADS, HEAD_W, HEAD_W), lambda n: (n, 0, 0, 0)), ANY),
        out_shape=(jax.ShapeDtypeStruct((TP, D), F32), jax.ShapeDtypeStruct((N_CHUNK, HEADS, HEAD_W, HEAD_W), F32),
                   jax.ShapeDtypeStruct((3, D, D), BF16)),
        scratch_shapes=[pltpu.VMEM((HEADS, HEAD_W, HEAD_W), F32), pltpu.VMEM((cps, CHUNK, D), F32),
                        pltpu.VMEM((3, W_ROW_BLK, D), BF16), pltpu.SemaphoreType.DMA((7,)),
                        pltpu.SemaphoreType.DMA((7,)), pltpu.SemaphoreType.DMA((2,))],
        compiler_params=pltpu.CompilerParams(dimension_semantics=("arbitrary",)),
    )(proj, lb_logits, w3_b)


def _rec_bwd(proj, lb_logits, d_o, s_start, dproj):
    cps = CHUNKS_PER_STEP
    rows = cps * CHUNK
    last = N_CHUNK // cps - 1

    def body(p_ref, lbl_ref, do_ref, s_ref, dproj_in, dp_ref, dlb_ref, dst, bsc, dq_sc, dk_sc, g_sc):
        del dproj_in
        n = pl.program_id(0)

        @pl.when(n == 0)
        def _():
            dst[...] = jnp.zeros_like(dst)
            dlb_ref[...] = jnp.zeros_like(dlb_ref)

        def prep(ci):
            g = _gates(p_ref.at[pl.ds(ci * CHUNK, CHUNK)], lbl_ref, (last - n) * cps + ci, bsc.at[ci])
            g["q1"] = (g["q"] * g["e_q"]).astype(BF16)
            qm_f = g["q"] * g["e_qm"]
            km_f = g["k"] * g["e_km"]
            g["qm"] = qm_f.astype(BF16)
            g["km"] = km_f.astype(BF16)
            g["qm_lo"] = (qm_f - g["qm"].astype(F32)).astype(BF16)
            g["km_lo"] = (km_f - g["km"].astype(F32)).astype(BF16)
            g["kh_f"] = g["k"] * g["e_kh"]
            g["kh"] = g["kh_f"].astype(BF16)
            return g

        def heads_and_post(ci, g):
            rs = pl.ds(ci * CHUNK, CHUNK)
            pv = p_ref.at[rs]
            dpv = dp_ref.at[rs]
            q1, qm, km, qm_lo, km_lo, kh_f, kh = (g[k] for k in ("q1", "qm", "km", "qm_lo", "km_lo", "kh_f", "kh"))
            for h in range(HEADS):
                sl = slice(h * HEAD_W, (h + 1) * HEAD_W)
                v = pv[:, 2 * D + h * HEAD_W:2 * D + (h + 1) * HEAD_W]
                d_oh = do_ref[rs, sl].astype(BF16)
                s0 = s_ref[ci, h]
                ds_end = dst[h]
                ds_end_b = ds_end.astype(BF16)
                att = jnp.where(g["causal"], _dot_nt(qm[:, sl], km[:, sl]), 0.0).astype(BF16)
                d_att = jnp.where(g["causal"], _dot_nt(d_oh, v), 0.0).astype(BF16)
                d_v = _dot_tn(att, d_oh) + _dot_nt(kh[:, sl], ds_end_b)
                d_qm = _dot(d_att, km[:, sl]) + _dot(d_att, km_lo[:, sl])
                d_q1 = _dot(d_oh, s0.astype(BF16))
                d_km = _dot_tn(d_att, qm[:, sl]) + _dot_tn(d_att, qm_lo[:, sl])
                d_kh = _dot(v, ds_end_b)
                dq_sc[ci, :, sl] = d_qm * g["e_qm"][:, sl] + d_q1 * g["e_q"][:, sl]
                dk_sc[ci, :, sl] = d_km * g["e_km"][:, sl] + d_kh * g["e_kh"][:, sl]
                g_sc[ci, :, sl] = (jnp.sum(kh_f[:, sl] * d_kh, axis=0, keepdims=True)
                                   + g["e_last"][:, sl] * jnp.sum(ds_end * s0, axis=0, keepdims=True))
                dst[h] = ds_end * g["e_last"][:, sl] + _dot_tn(d_oh, q1[:, sl])
                dpv[:, 2 * D + h * HEAD_W:2 * D + (h + 1) * HEAD_W] = d_v.astype(BF16)
            d_q = dq_sc[ci]
            d_k = dk_sc[ci]
            d_b = g["q"] * d_q - g["k"] * d_k
            anti = jnp.logical_not(g["causal"]) | (lax.broadcasted_iota(jnp.int32, (CHUNK, CHUNK), 0)
                                                    == lax.broadcasted_iota(jnp.int32, (CHUNK, CHUNK), 1))
            d_lf = _tri_matmul(anti.astype(BF16), d_b) + g_sc[ci]
            d_f = jnp.where(g["valid"], d_lf / g["f"] - d_k, 0.0)
            sg = g["sg"]
            dlb_ref[0:1, :] += jnp.sum(d_f * (1.0 - sg), axis=0, keepdims=True)
            dpv[:, 0:D] = (d_q * _dsilu(g["q_raw"], g["sq"])).astype(BF16)
            dpv[:, D:2 * D] = (d_f * (1.0 - g["lb"]) * sg * (1.0 - sg)).astype(BF16)

        ready = prep(cps - 1)
        for ci in reversed(range(cps)):
            coming = prep(ci - 1) if ci > 0 else None
            heads_and_post(ci, ready)
            ready = coming

    return pl.pallas_call(
        body, name="rec_bwd", grid=(N_CHUNK // cps,),
        in_specs=[pl.BlockSpec((rows, 3 * D), lambda n: (last - n, 1)), pl.BlockSpec((2, D), lambda n: (0, 0)),
                  pl.BlockSpec((rows, D), lambda n: (last - n, 0)),
                  pl.BlockSpec((cps, HEADS, HEAD_W, HEAD_W), lambda n: (last - n, 0, 0, 0)), ANY],
        out_specs=(pl.BlockSpec((rows, 3 * D), lambda n: (last - n, 1)), pl.BlockSpec((8, D), lambda n: (0, 0))),
        out_shape=(jax.ShapeDtypeStruct((TP, D_IN), BF16), jax.ShapeDtypeStruct((8, D), F32)),
        scratch_shapes=[pltpu.VMEM((HEADS, HEAD_W, HEAD_W), F32), pltpu.VMEM((cps, CHUNK, D), F32),
                        pltpu.VMEM((cps, CHUNK, D), F32), pltpu.VMEM((cps, CHUNK, D), F32),
                        pltpu.VMEM((cps, 1, D), F32)],
        input_output_aliases={4: 0},
        compiler_params=pltpu.CompilerParams(dimension_semantics=("arbitrary",)),
    )(proj, lb_logits, d_o, s_start, dproj)


def _mid(xin, tgt, o, c0, proj, w3, ln_g, ln_b, gnorm_g, final_g):
    tm = TM_ELT

    def body(x_ref, t_ref, o_ref, c0_ref, z_ref, gr_ref, mc_ref, mr_ref, w_ref, lng_ref, lnb_ref, gng_ref, fg_ref,
             dres_ref, do_ref, dc0_ref, dz_ref, dp_ref, a3_ref, b3_ref, red_ref, on_sc, don_sc):
        i = pl.program_id(0)

        @pl.when(i == 0)
        def _():
            red_ref[...] = jnp.zeros_like(red_ref)

        w_conv, w_rec, w_out = w_ref[0], w_ref[1], w_ref[2]
        c0v = c0_ref[...]
        mu = jnp.mean(c0v, axis=-1, keepdims=True)
        xc = c0v - mu
        rstd = lax.rsqrt(jnp.mean(xc * xc, axis=-1, keepdims=True) + EPS)
        xh = xc * rstd
        c1 = xh * lng_ref[...] + lnb_ref[...]
        s1 = _sigmoid(c1)
        c2 = c1 * s1
        z = z_ref[...].astype(F32)
        sz = _sigmoid(z)
        silu_z = z * sz
        u_conv = (c2 * silu_z).astype(BF16)
        y_conv = _dot(u_conv, w_conv)
        ov = o_ref[...]
        r3 = []
        for h in range(HEADS):
            sl = slice(h * HEAD_W, (h + 1) * HEAD_W)
            oh = ov[:, sl]
            r_h = lax.rsqrt(jnp.mean(oh * oh, axis=-1, keepdims=True) + EPS)
            r3.append(r_h)
            on_sc[:, sl] = oh * r_h
        o_n = on_sc[...]
        o_g = o_n * gng_ref[...]
        gr = gr_ref[...].astype(F32)
        sgr = _sigmoid(gr)
        silu_g = gr * sgr
        u_rec = (o_g * silu_g).astype(BF16)
        y_rec = _dot(u_rec, w_rec)
        mc = mc_ref[...].astype(F32)
        mr = mr_ref[...].astype(F32)
        smc = _sigmoid(mc)
        smr = _sigmoid(mr)
        merged = (smc * y_conv + smr * y_rec).astype(BF16)
        res = x_ref[...] + _dot(merged, w_out)
        r2 = lax.rsqrt(jnp.mean(res * res, axis=-1, keepdims=True) + EPS)
        xh2 = res * r2
        row = lax.broadcasted_iota(jnp.int32, (tm, 1), 0) + i * tm
        real = row >= ROW0
        tgt = t_ref[...]
        tgt = jnp.where(i == 0, pltpu.roll(tgt, ROW0, 0), tgt)
        diff = jnp.where(real, xh2 * fg_ref[...] - tgt, 0.0)
        d_y = diff * (1.0 / D)
        d_xh2 = d_y * fg_ref[...]
        d_res = r2 * (d_xh2 - xh2 * jnp.mean(d_xh2 * xh2, axis=-1, keepdims=True))
        dres_ref[...] = d_res
        d_res_b = d_res.astype(BF16)
        d_merged = _dot_nt(d_res_b, w_out)
        d_yc = (d_merged * smc).astype(BF16)
        d_yr = (d_merged * smr).astype(BF16)
        dp_ref[:, D:2 * D] = (d_merged * y_conv * smc * (1.0 - smc)).astype(BF16)
        dp_ref[:, 2 * D:3 * D] = (d_merged * y_rec * smr * (1.0 - smr)).astype(BF16)
        d_ur = _dot_nt(d_yr, w_rec)
        d_og = d_ur * silu_g
        dp_ref[:, 0:D] = (d_ur * o_g * _dsilu(gr, sgr)).astype(BF16)
        d_on = d_og * gng_ref[...]
        for h in range(HEADS):
            sl = slice(h * HEAD_W, (h + 1) * HEAD_W)
            d_h = d_on[:, sl]
            n_h = o_n[:, sl]
            don_sc[:, sl] = r3[h] * (d_h - n_h * jnp.mean(d_h * n_h, axis=-1, keepdims=True))
        do_ref[...] = don_sc[...]
        d_uc = _dot_nt(d_yc, w_conv)
        d_c2 = d_uc * silu_z
        dz_ref[...] = (d_uc * c2 * _dsilu(z, sz)).astype(BF16)
        d_c1 = d_c2 * _dsilu(c1, s1)
        d_xh = d_c1 * lng_ref[...]
        d_c0 = rstd * (d_xh - jnp.mean(d_xh, axis=-1, keepdims=True)
                       - xh * jnp.mean(d_xh * xh, axis=-1, keepdims=True))
        dc0_ref[...] = d_c0
        a3_ref[0] = u_conv
        b3_ref[0] = d_yc
        a3_ref[1] = u_rec
        b3_ref[1] = d_yr
        a3_ref[2] = merged
        b3_ref[2] = d_res_b
        def colsum(vv):
            return jnp.sum(vv, axis=0, keepdims=True)

        red_ref[0:1, :] += colsum(d_y * xh2)
        red_ref[1:2, :] += colsum(d_og * o_n)
        red_ref[2:3, :] += colsum(d_c1 * xh)
        red_ref[3:4, :] += colsum(d_c1)
        red_ref[4:5, :] += colsum(d_c0)
        red_ref[5:6, :] += colsum(diff * diff) * (0.5 / D)

    def row_block(width, col):
        return pl.BlockSpec((tm, width), lambda i: (i, col))

    def const_block(shape):
        return pl.BlockSpec(shape, lambda i: (0,) * len(shape))

    stack = jax.ShapeDtypeStruct((3, TP, D), BF16)
    stack_spec = pl.BlockSpec((3, tm, D), lambda i: (0, i, 0))
    return pl.pallas_call(
        body, name="mid", grid=(TP // tm,),
        in_specs=[row_block(D, 0),
                  pl.BlockSpec((pl.Element(tm), pl.Element(D)), lambda i: (_window_start(i, tm), 0)),
                  row_block(D, 0), row_block(D, 0),
                  row_block(D, 2), row_block(D, 6), row_block(D, 7), row_block(D, 8),
                  pl.BlockSpec((3, D, D), lambda i: (0, 0, 0), pipeline_mode=pl.Buffered(1)),
                  const_block((1, D)), const_block((1, D)), const_block((1, D)), const_block((1, D))],
        out_specs=(row_block(D, 0), row_block(D, 0), row_block(D, 0), row_block(D, 0), row_block(3 * D, 2),
                   stack_spec, stack_spec, const_block((8, D))),
        out_shape=(jax.ShapeDtypeStruct((TP, D), F32), jax.ShapeDtypeStruct((TP, D), F32),
                   jax.ShapeDtypeStruct((TP, D), F32), jax.ShapeDtypeStruct((TP, D), BF16),
                   jax.ShapeDtypeStruct((TP, D_IN), BF16), stack, stack, jax.ShapeDtypeStruct((8, D), F32)),
        scratch_shapes=[pltpu.VMEM((tm, D), F32), pltpu.VMEM((tm, D), F32)],
        compiler_params=pltpu.CompilerParams(dimension_semantics=("arbitrary",), vmem_limit_bytes=60 * 1024 * 1024),
    )(xin, tgt, o, c0, proj, proj, proj, proj, w3, ln_g, ln_b, gnorm_g, final_g)


def _conv_bwd(proj, d_c0, d_z, conv_w, dproj):
    tm = TM_ELT
    n_tile = TP // tm
    lastt = n_tile - 1

    strip = tm // CONV_STRIPS

    def body(p_ref, dc_ref, dz_ref, w_ref, dproj_in, dp_ref, dw_ref, dsh, a_sc, da_sc, acc):
        del dproj_in
        i = pl.program_id(0)

        @pl.when(i == 0)
        def _():
            dsh[0, :, tm:tm + HALO, :] = jnp.zeros((N_CB, HALO, HEAD_W), F32)
            acc[...] = jnp.zeros_like(acc)

        @pl.when(i > 0)
        def _():
            dsh[0, :, tm:tm + HALO, :] = dsh[0, :, 0:HALO, :]

        _store_by_cb(dsh, (0,), slice(0, tm), dc_ref[...])
        _fill_shifts(dsh, tm)
        ga = p_ref[:, 0:D].astype(F32)
        sb = _sigmoid(p_ref[:, D:2 * D].astype(F32))
        a = ga * sb
        _store_by_cb(a_sc, (), slice(0, tm), a)
        for cb in range(N_CB):
            cs = slice(cb * HEAD_W, (cb + 1) * HEAD_W)
            for st in range(CONV_STRIPS):
                rows = slice(st * strip, (st + 1) * strip)
                a_s = a_sc[cb, rows, :]
                d_a = jnp.zeros((strip, HEAD_W), F32)
                for j in range(CONV_K):
                    off = CONV_K - 1 - j
                    lo = st * strip + 8 * (off // 8)
                    slab = dsh[off % 8, cb, lo:lo + strip, :]
                    d_a = d_a + w_ref[j:j + 1, cs] * slab
                    acc[j, :, cs] += jnp.sum((a_s * slab).reshape(strip // 8, 8, HEAD_W), axis=0)
                da_sc[rows, cs] = d_a
        d_a = da_sc[...]
        dp_ref[:, 0:D] = (d_a * sb).astype(BF16)
        dp_ref[:, D:2 * D] = (d_a * a * (1.0 - sb)).astype(BF16)
        dp_ref[:, 2 * D:3 * D] = dz_ref[...]

        @pl.when(i == lastt)
        def _():
            for j in range(CONV_K):
                dw_ref[j:j + 1, :] = jnp.sum(acc[j], axis=0, keepdims=True)
            dw_ref[CONV_K:CONV_K + 1, :] = jnp.zeros((1, D), F32)

    return pl.pallas_call(
        body, name="conv_bwd", grid=(n_tile,),
        in_specs=[pl.BlockSpec((tm, 2 * D), lambda i: (lastt - i, 0)), pl.BlockSpec((tm, D), lambda i: (lastt - i, 0)),
                  pl.BlockSpec((tm, D), lambda i: (lastt - i, 0)), pl.BlockSpec((CONV_K, D), lambda i: (0, 0)), ANY],
        out_specs=(pl.BlockSpec((tm, 3 * D), lambda i: (lastt - i, 0)), pl.BlockSpec((CONV_K + 1, D), lambda i: (0, 0))),
        out_shape=(jax.ShapeDtypeStruct((TP, D_IN), BF16), jax.ShapeDtypeStruct((CONV_K + 1, D), F32)),
        scratch_shapes=[pltpu.VMEM((8, N_CB, tm + HALO, HEAD_W), F32), pltpu.VMEM((N_CB, tm, HEAD_W), F32),
                        pltpu.VMEM((tm, D), F32), pltpu.VMEM((CONV_K, 8, D), F32)],
        input_output_aliases={4: 0},
        compiler_params=pltpu.CompilerParams(dimension_semantics=("arbitrary",)),
    )(proj, d_c0, d_z, conv_w, dproj)


def _wgrad3(a3, b3):
    tt = TM_MAT

    def body(a_ref, b_ref, o_ref):
        @pl.when(pl.program_id(1) == 0)
        def _():
            o_ref[...] = jnp.zeros_like(o_ref)

        o_ref[0] += _dot_tn(a_ref[0], b_ref[0])

    return pl.pallas_call(
        body, name="wgrad3", grid=(3, TP // tt),
        in_specs=[pl.BlockSpec((1, tt, D), lambda g, t: (g, t, 0)), pl.BlockSpec((1, tt, D), lambda g, t: (g, t, 0))],
        out_specs=pl.BlockSpec((1, D, D), lambda g, t: (g, 0, 0)),
        out_shape=jax.ShapeDtypeStruct((3, D, D), F32),
        compiler_params=pltpu.CompilerParams(dimension_semantics=("arbitrary", "arbitrary")),
    )(a3, b3)


def _wgrad_in(h, dproj, ids, p3):
    tt = TM_MAT
    n_t = TP // tt

    def body(ids_ref, a_ref, b_ref, p3_ref, o_ref, ob_ref, l0_ref, l1_ref, acc, tmp, send_sems, recv_sems, tmp_sem):
        del ids_ref
        r = pl.program_id(0)
        t = pl.program_id(1)
        x, y, c = _my_place()
        sibling = (x, y, 1 - c)
        slot = lax.rem(r, 2)

        def send_in(q):
            return pltpu.make_async_remote_copy(
                src_ref=acc.at[q % 2], dst_ref=l0_ref.at[q], send_sem=send_sems.at[q], recv_sem=recv_sems.at[q],
                device_id=sibling, device_id_type=MESH_ID)

        def send_3(q):
            d = _dev_index(*_chip_rel(x, y, q), 1 - c)
            return pltpu.make_async_remote_copy(
                src_ref=p3_ref.at[:, pl.ds(pl.multiple_of(d * W_ROW_BLK, W_ROW_BLK), W_ROW_BLK), :],
                dst_ref=l1_ref.at[q], send_sem=send_sems.at[4 + q], recv_sem=recv_sems.at[4 + q],
                device_id=sibling, device_id_type=MESH_ID)

        def landed(q):
            return pltpu.make_async_copy(l0_ref.at[q], tmp, tmp_sem)

        @pl.when((r == 0) & (t == 0))
        def _():
            for q in range(4):
                send_3(q).start()

        @pl.when(t == 0)
        def _():
            acc[slot] = jnp.zeros((D, W_IN_BLK), F32)

        acc[slot] += _dot_tn(a_ref[...], b_ref[...])

        for q in range(4):
            @pl.when((r == q) & (t == n_t - 1))
            def _(q=q):
                if q >= 1:
                    send_in(q - 1).wait_send()
                send_in(q).start()

            @pl.when((r == 4 + q) & (t == n_t - 2))
            def _(q=q):
                if q == 0:
                    send_in(3).wait_send()
                send_in(q).wait_recv()
                landed(q).start()

            @pl.when((r == 4 + q) & (t == n_t - 1))
            def _(q=q):
                landed(q).wait()
                tot = acc[q % 2] + tmp[...]
                o_ref[0] = tot
                ob_ref[0] = tot.astype(BF16)

        @pl.when((r == 7) & (t == n_t - 1))
        def _():
            for q in range(4):
                send_3(q).wait_recv()
            for q in range(4):
                send_3(q).wait_send()

    blk = pl.BlockSpec((1, D, W_IN_BLK), lambda r, t, ids: (jnp.maximum(r - 4, 0), 0, 0))
    return pl.pallas_call(
        body, name="wgrad_in",
        grid_spec=pltpu.PrefetchScalarGridSpec(
            num_scalar_prefetch=1, grid=(N_DEV, n_t),
            in_specs=[pl.BlockSpec((tt, D), lambda r, t, ids: (t, 0)),
                      pl.BlockSpec((tt, W_IN_BLK), lambda r, t, ids: (t, ids[r])), ANY],
            out_specs=(blk, blk, ANY, ANY),
            scratch_shapes=[pltpu.VMEM((2, D, W_IN_BLK), F32), pltpu.VMEM((D, W_IN_BLK), F32),
                            pltpu.SemaphoreType.DMA((8,)), pltpu.SemaphoreType.DMA((8,)), pltpu.SemaphoreType.DMA]),
        out_shape=(jax.ShapeDtypeStruct((4, D, W_IN_BLK), F32), jax.ShapeDtypeStruct((4, D, W_IN_BLK), BF16),
                   jax.ShapeDtypeStruct((4, D, W_IN_BLK), F32), jax.ShapeDtypeStruct((4, 3, W_ROW_BLK, D), F32)),
        compiler_params=pltpu.CompilerParams(dimension_semantics=("arbitrary", "arbitrary")),
    )(ids, h, dproj, p3)


def _chip_sum_3(p3, land1, ids_mine):
    def body(ids_ref, p_ref, l_ref, o_ref, ob_ref):
        del ids_ref
        tot = p_ref[...] + l_ref[0]
        o_ref[0] = tot
        ob_ref[0] = tot.astype(BF16)

    blk = pl.BlockSpec((1, 3, W_ROW_BLK, D), lambda r, ids: (r, 0, 0, 0))
    return pl.pallas_call(
        body, name="chip_sum_3",
        grid_spec=pltpu.PrefetchScalarGridSpec(
            num_scalar_prefetch=1, grid=(4,),
            in_specs=[pl.BlockSpec((3, W_ROW_BLK, D), lambda r, ids: (0, ids[r], 0)), blk],
            out_specs=(blk, blk)),
        out_shape=(jax.ShapeDtypeStruct((4, 3, W_ROW_BLK, D), F32), jax.ShapeDtypeStruct((4, 3, W_ROW_BLK, D), BF16)),
    )(ids_mine, p3, land1)


def _dh_and_norm_bwd(dproj, w_in_full, xin, d_res, norm_g, chip0b, chip1b):
    tm = TM_MAT
    n_k = N_DEV
    n_m = TP // tm

    def body(dp_ref, w_ref, x_ref, dr_ref, g_ref, c0_ref, c1_ref, dx_ref, dg_ref, f0_ref, f1_ref, acc,
             send_sems, recv_sems):
        m = pl.program_id(0)
        k = pl.program_id(1)
        x, y, c = _my_place()

        def to_owner(a, q):
            src, dst = ((c0_ref, f0_ref), (c1_ref, f1_ref))[a]
            return pltpu.make_async_remote_copy(
                src_ref=src.at[q], dst_ref=dst.at[q - 1], send_sem=send_sems.at[a * 3 + q - 1],
                recv_sem=recv_sems.at[a * 3 + q - 1], device_id=(*_chip_rel(x, y, q), c), device_id_type=MESH_ID)

        @pl.when((m == 0) & (k == 0))
        def _():
            for q in range(1, 4):
                for a in range(2):
                    to_owner(a, q).start()

        @pl.when(k == 0)
        def _():
            acc[...] = jnp.zeros_like(acc)

        acc[...] += _dot_nt(dp_ref[...], w_ref[0])

        @pl.when((k == n_k - 1) & (m == 0))
        def _():
            dg_ref[...] = jnp.zeros_like(dg_ref)

        @pl.when(k == n_k - 1)
        def _():
            xv = x_ref[...]
            r1 = lax.rsqrt(jnp.mean(xv * xv, axis=-1, keepdims=True) + EPS)
            xh = xv * r1
            d_h = acc[...]
            dg_ref[0:1, :] += jnp.sum(d_h * xh, axis=0, keepdims=True)
            d_xh = d_h * g_ref[...]
            dx_ref[...] = dr_ref[...] + r1 * (d_xh - xh * jnp.mean(d_xh * xh, axis=-1, keepdims=True))

        @pl.when((m == n_m - 1) & (k == n_k - 1))
        def _():
            for q in range(1, 4):
                for a in range(2):
                    to_owner(a, q).wait_recv()
            for q in range(1, 4):
                for a in range(2):
                    to_owner(a, q).wait_send()

    return pl.pallas_call(
        body, name="dh_norm_bwd", grid=(n_m, n_k),
        in_specs=[pl.BlockSpec((tm, W_IN_BLK), lambda m, k: (m, k)), pl.BlockSpec((1, D, W_IN_BLK), lambda m, k: (k, 0, 0)),
                  pl.BlockSpec((tm, D), lambda m, k: (m, 0)), pl.BlockSpec((tm, D), lambda m, k: (m, 0)),
                  pl.BlockSpec((1, D), lambda m, k: (0, 0)), ANY, ANY],
        out_specs=(pl.BlockSpec((tm, D), lambda m, k: (m, 0)), pl.BlockSpec((8, D), lambda m, k: (0, 0)), ANY, ANY),
        out_shape=(jax.ShapeDtypeStruct((TP, D), F32), jax.ShapeDtypeStruct((8, D), F32),
                   jax.ShapeDtypeStruct((3, D, W_IN_BLK), BF16), jax.ShapeDtypeStruct((3, 3, W_ROW_BLK, D), BF16)),
        scratch_shapes=[pltpu.VMEM((tm, D), F32), pltpu.SemaphoreType.DMA((6,)), pltpu.SemaphoreType.DMA((6,))],
        compiler_params=pltpu.CompilerParams(dimension_semantics=("arbitrary", "arbitrary")),
    )(dproj, w_in_full, xin, d_res, norm_g, chip0b, chip1b)


def _sum_adamw(own, landed, w, m, v, tr, name):
    rows, cols = w.shape
    n_t = rows // tr

    def body(o_ref, l1_ref, l2_ref, l3_ref, w_ref, m_ref, v_ref, g_ref, d_ref, m2_ref, v2_ref):
        g = ((o_ref[...] + l1_ref[...].astype(F32)) + l2_ref[...].astype(F32)) + l3_ref[...].astype(F32)
        delta, m2, v2 = _adamw(w_ref[...], g, m_ref[...], v_ref[...])
        g_ref[...] = g
        d_ref[...] = delta
        m2_ref[...] = m2
        v2_ref[...] = v2

    def spec(k):
        return pl.BlockSpec((tr, cols), lambda i: (i + k * n_t, 0))

    out = jax.ShapeDtypeStruct((rows, cols), F32)
    return pl.pallas_call(
        body, name=name, grid=(n_t,),
        in_specs=[spec(0), spec(0), spec(1), spec(2), spec(0), spec(0), spec(0)],
        out_specs=(spec(0),) * 4, out_shape=(out,) * 4,
    )(own, landed, landed, landed, w, m, v)


def _small_update(pack_all, srs_all, lb_logits, p8, m8, v8, ws, ms, vs):
    def body(pk_ref, sr_ref, lbl_ref, p_ref, m_ref, v_ref, ws_ref, ms_ref, vs_ref,
             g8_ref, d8_ref, m8_ref, v8_ref, loss_ref, gs_ref, ds_ref, ms2_ref, vs2_ref):
        tot = pk_ref[0]
        tot_s = sr_ref[0]
        for d in range(1, N_DEV):
            tot = tot + pk_ref[d]
            tot_s = tot_s + sr_ref[d]
        p0 = _sigmoid(lbl_ref[0:1, :] - lbl_ref[1:2, :])
        row = lax.broadcasted_iota(jnp.int32, (8, D), 0)
        d_lb = jnp.sum(jnp.where(row == 4, tot, 0.0), axis=0, keepdims=True)
        d_l0 = d_lb * p0 * (1.0 - p0)
        loss_ref[...] = jnp.sum(jnp.where(row == 5, tot, 0.0), keepdims=True).reshape(1, 1)
        g8 = jnp.where(row == 4, d_l0, jnp.where(row == 5, -d_l0, tot))
        delta, m2, v2 = _adamw(p_ref[...], g8, m_ref[...], v_ref[...])
        g8_ref[...] = g8
        d8_ref[...] = delta
        m8_ref[...] = m2
        v8_ref[...] = v2
        delta, m2, v2 = _adamw(ws_ref[...], tot_s, ms_ref[...], vs_ref[...])
        gs_ref[...] = tot_s
        ds_ref[...] = delta
        ms2_ref[...] = m2
        vs2_ref[...] = v2

    o8 = jax.ShapeDtypeStruct((8, D), F32)
    os_ = jax.ShapeDtypeStruct((SMALL_ROWS, HEAD_W), F32)
    return pl.pallas_call(
        body, name="small_update",
        out_shape=(o8, o8, o8, o8, jax.ShapeDtypeStruct((1, 1), F32), os_, os_, os_, os_),
    )(pack_all, srs_all, lb_logits, p8, m8, v8, ws, ms, vs)


def _local_step(xin, proj, target, conv_w_full, conv_b, ln_g, ln_b, w3_b, lb_logits, gnorm_g, final_g):
    fg = final_g.reshape(1, D)
    c0 = _conv_fwd(proj, conv_w_full, conv_b)
    o, s_start, w3_full = _rec_fwd(proj, lb_logits, w3_b)
    d_res, d_o, d_c0, d_z, dproj, a3, b3, red = _mid(xin, target, o, c0, proj, w3_full, ln_g, ln_b, gnorm_g, fg)
    dproj, dlb = _rec_bwd(proj, lb_logits, d_o, s_start, dproj)
    dproj, d_conv_w = _conv_bwd(proj, d_c0, d_z, conv_w_full, dproj)
    p3 = _wgrad3(a3, b3)
    return dproj, d_res, p3, d_conv_w, red, dlb


def kernel(x, meta_tokens, norm_g, w_in, conv_w, conv_b, ln_g, ln_b, w_conv_out, lb_logits, gnorm_g, w_rec_out, w_out, final_g, loss_target, m_meta_tokens, m_norm_g, m_w_in, m_conv_w, m_conv_b, m_ln_g, m_ln_b, m_w_conv_out, m_lb_logits, m_gnorm_g, m_w_rec_out, m_w_out, m_final_g, v_meta_tokens, v_norm_g, v_w_in, v_conv_w, v_conv_b, v_ln_g, v_ln_b, v_w_conv_out, v_lb_logits, v_gnorm_g, v_w_rec_out, v_w_out, v_final_g):
    def small_pack(cw, mt):
        return jnp.concatenate([cw[0], jnp.zeros((1, HEAD_W), F32), mt], axis=0)

    def stack3(a, b, c):
        return jnp.concatenate([a, b, c], axis=0)

    def stack8(ng, cb, lg, lb_, lbl, gg, fg):
        return jnp.concatenate([ng, cb, lg, lb_, lbl, gg, fg.reshape(1, D)], axis=0)

    mx, my, mc = _my_place()

    w3_s = stack3(w_conv_out, w_rec_out, w_out)
    ws_s = small_pack(conv_w, meta_tokens)
    small_full = jnp.transpose(_gather_small(ws_s), (1, 0, 2)).reshape(SMALL_ROWS, D)
    conv_w_full = small_full[0:CONV_K]
    meta_full = small_full[META_ROW:META_ROW + N_META]
    w_in_b, w3_b = _cast_shards(w_in[0], w3_s)
    use_order = [(mx, my, mc), (mx, my, 1 - mc)]
    for chip in ((1 - mx, my), (mx, 1 - my), (1 - mx, 1 - my)):
        use_order += [(*chip, mc), (*chip, 1 - mc)]
    order = jnp.stack([_dev_index(*p) for p in use_order]).astype(jnp.int32)
    proj, xin, h, w_in_full = _gather_and_proj(x[0], meta_full, norm_g, w_in_b, order)
    h = h.reshape(TP, D)

    dproj, d_res, p3, d_conv_w, red, dlb = _local_step(
        xin, proj, loss_target[0], conv_w_full, conv_b, ln_g, ln_b, w3_b, lb_logits, gnorm_g, final_g)

    ids_mine = jnp.stack([_dev_index(*_chip_rel(mx, my, r), mc) for r in range(4)]).astype(jnp.int32)
    ids_sib = jnp.stack([_dev_index(*_chip_rel(mx, my, r), 1 - mc) for r in range(4)]).astype(jnp.int32)
    chip0, chip0b, _, land1 = _wgrad_in(h, dproj, jnp.concatenate([ids_sib, ids_mine]), p3)
    chip1, chip1b = _chip_sum_3(p3, land1, ids_mine)
    d_xin, dng, far0, far1 = _dh_and_norm_bwd(dproj, w_in_full, xin, d_res, norm_g, chip0b, chip1b)
    pack = jnp.concatenate([dng[0:1], red[4:5], red[2:3], red[3:4], dlb[0:1], red[5:6], red[1:2], red[0:1]], axis=0)
    g_in, d_in, m_in, v_in = _sum_adamw(chip0.reshape(4 * D, W_IN_BLK), far0.reshape(3 * D, W_IN_BLK), w_in[0],
                                        m_w_in[0], v_w_in[0], 256, "adamw_in")
    g_3, d_3, m_3, v_3 = _sum_adamw(
        chip1.reshape(12 * W_ROW_BLK, D), far1.reshape(9 * W_ROW_BLK, D), w3_s.reshape(3 * W_ROW_BLK, D),
        stack3(m_w_conv_out, m_w_rec_out, m_w_out).reshape(3 * W_ROW_BLK, D),
        stack3(v_w_conv_out, v_w_rec_out, v_w_out).reshape(3 * W_ROW_BLK, D), 3 * W_ROW_BLK, "adamw_3")

    srs = jnp.concatenate([d_conv_w, d_xin[PAD_FRONT:ROW0]], axis=0)
    srs = jnp.transpose(srs.reshape(SMALL_ROWS, N_DEV, HEAD_W), (1, 0, 2))
    pack_all, srs_all = _exchange_small(pack, srs)
    g8, d8, m8, v8, loss, gs, ds, ms, vs = _small_update(
        pack_all, srs_all, lb_logits,
        stack8(norm_g, conv_b, ln_g, ln_b, lb_logits, gnorm_g, final_g),
        stack8(m_norm_g, m_conv_b, m_ln_g, m_ln_b, m_lb_logits, m_gnorm_g, m_final_g),
        stack8(v_norm_g, v_conv_b, v_ln_g, v_ln_b, v_lb_logits, v_gnorm_g, v_final_g),
        ws_s, small_pack(m_conv_w, m_meta_tokens), small_pack(v_conv_w, v_meta_tokens))

    def unpack(a_in, a_3, a_s, a_8):
        t3 = a_3.reshape(3, 1, W_ROW_BLK, D)
        return (a_s[META_ROW:META_ROW + N_META], a_8[0:1], a_in[None], a_s[0:CONV_K][None], a_8[1:2], a_8[2:3],
                a_8[3:4], t3[0], a_8[4:6], a_8[6:7], t3[1], t3[2], a_8[7])

    grad_x = d_xin[ROW0:][None]
    return (loss.reshape(()), grad_x, *unpack(g_in, g_3, gs, g8), *unpack(d_in, d_3, ds, d8),
            *unpack(m_in, m_3, ms, m8), *unpack(v_in, v_3, vs, v8))
```

```python
import functools

import jax
import jax.numpy as jnp
from jax import lax
from jax.experimental import pallas as pl
from jax.experimental.pallas import tpu as pltpu

F32 = jnp.float32
BF16 = jnp.bfloat16

D = 1024
SEQ = 4096
N_META = 16
CHUNK = 64
PAD_FRONT = 48
ROW0 = PAD_FRONT + N_META
TP = ROW0 + SEQ
N_CHUNK = TP // CHUNK
HEADS = 8
HEAD_W = 128
D_IN = 9 * D
N_DEV = 8
W_IN_BLK = D_IN // N_DEV
W_ROW_BLK = D // N_DEV
CONV_K = 31
SMALL_ROWS = 48
META_ROW = 32
EPS = 1e-6
HALO = 32

TM_MAT = 832
TT_WGRAD = 2080
DH_K_BLKS = 2
TM_ELT = 208
CHUNKS_PER_STEP = 5
CONV_STRIPS = 2

ADAM_LR = 0.001
ADAM_B1 = 0.9
ADAM_B2 = 0.999
ADAM_EPS = 1e-08
ADAM_WD = 0.01
ADAM_STEP = 10

MESH_ID = pl.DeviceIdType.MESH
ANY = pl.BlockSpec(memory_space=pl.ANY)


def _sigmoid(v):
    return jax.nn.sigmoid(v)


def _dsilu(v, s):
    return s * (1.0 + v * (1.0 - s))


def _dot(a, b):
    return jnp.dot(a, b, preferred_element_type=F32)


def _dot_nt(a, b):
    return lax.dot_general(a, b, (((1,), (1,)), ((), ())), preferred_element_type=F32)


def _dot_tn(a, b):
    return lax.dot_general(a, b, (((0,), (0,)), ((), ())), preferred_element_type=F32)


def _split3(v):
    hi = v.astype(BF16)
    r1 = v - hi.astype(F32)
    mid = r1.astype(BF16)
    lo = (r1 - mid.astype(F32)).astype(BF16)
    return hi, mid, lo


def _tri_matmul(tri, v):
    hi, mid, lo = _split3(v)
    return _dot(tri, hi) + _dot(tri, mid) + _dot(tri, lo)


def _adamw(w, g, m, v):
    m2 = ADAM_B1 * m + (1.0 - ADAM_B1) * g
    v2 = ADAM_B2 * v + (1.0 - ADAM_B2) * jnp.square(g)
    m_hat = m2 / (1.0 - ADAM_B1 ** ADAM_STEP)
    v_hat = v2 / (1.0 - ADAM_B2 ** ADAM_STEP)
    delta = -ADAM_LR * (m_hat / (jnp.sqrt(v_hat) + ADAM_EPS) + ADAM_WD * w)
    return delta, m2, v2


def _window_start(i, tm):
    assert tm % 16 == 0 and ROW0 % 16 == 0
    return pl.multiple_of(16 * jnp.maximum((tm // 16) * i - ROW0 // 16, 0), 16)


def _my_place():
    return lax.axis_index("x"), lax.axis_index("y"), lax.axis_index("c")


def _dev_index(px, py, pc):
    return 4 * px + 2 * py + pc


def _cast_shards(w_in_s, w3_s):
    def body(a_ref, b_ref, oa_ref, ob_ref):
        oa_ref[...] = a_ref[...].astype(BF16)
        ob_ref[...] = b_ref[...].astype(BF16)

    return pl.pallas_call(
        body, name="cast_shards",
        out_shape=(jax.ShapeDtypeStruct(w_in_s.shape, BF16), jax.ShapeDtypeStruct(w3_s.shape, BF16)),
    )(w_in_s, w3_s)


def _peer(x, y, c, r):
    return (jnp.bitwise_xor(x, (r >> 2) & 1), jnp.bitwise_xor(y, (r >> 1) & 1), jnp.bitwise_xor(c, r & 1))


def _gather_small(small_s):
    def body(s_ref, o_ref, send_sems, recv_sems, local_sem):
        x, y, c = _my_place()
        my_id = _dev_index(x, y, c)
        mine = pltpu.make_async_copy(s_ref, o_ref.at[my_id], local_sem)
        mine.start()
        copies = []
        for r in range(1, N_DEV):
            cp = pltpu.make_async_remote_copy(
                src_ref=s_ref, dst_ref=o_ref.at[my_id], send_sem=send_sems.at[r - 1], recv_sem=recv_sems.at[r - 1],
                device_id=_peer(x, y, c, r), device_id_type=MESH_ID)
            cp.start()
            copies.append(cp)
        for cp in copies:
            cp.wait_recv()
        for cp in copies:
            cp.wait_send()
        mine.wait()

    return pl.pallas_call(
        body, name="gather_small", out_shape=jax.ShapeDtypeStruct((N_DEV,) + small_s.shape, F32),
        in_specs=[ANY], out_specs=ANY,
        scratch_shapes=[pltpu.SemaphoreType.DMA((7,)), pltpu.SemaphoreType.DMA((7,)), pltpu.SemaphoreType.DMA],
    )(small_s)


def _w3_gather(src, out, stage, send_sems, recv_sems, local_sems):
    x, y, c = _my_place()
    me, sibling = (x, y, c), (x, y, 1 - c)
    chips = [(1 - x, y), (x, 1 - y), (1 - x, 1 - y)]

    def block(place):
        d = _dev_index(*place)
        return out.at[:, pl.ds(pl.multiple_of(d * W_ROW_BLK, W_ROW_BLK), W_ROW_BLK), :]

    def copy(k, place, to, from_src=False):
        return pltpu.make_async_remote_copy(
            src_ref=src if from_src else block(place), dst_ref=block(place),
            send_sem=send_sems.at[k], recv_sem=recv_sems.at[k], device_id=to, device_id_type=MESH_ID)

    own_in = pltpu.make_async_copy(src, stage, local_sems.at[0])
    own_out = pltpu.make_async_copy(stage, block(me), local_sems.at[1])

    def start():
        copy(0, me, sibling, from_src=True).start()
        for j, chip in enumerate(chips):
            copy(1 + j, me, (*chip, c), from_src=True).start()
        own_in.start()
        own_in.wait()
        own_out.start()

    def finish():
        for j, chip in enumerate(chips):
            copy(1 + j, (*chip, c), me).wait_recv()
            copy(4 + j, (*chip, c), sibling).start()
        copy(0, sibling, me).wait_recv()
        for j, chip in enumerate(chips):
            copy(4 + j, (*chip, 1 - c), me).wait_recv()
        for k in range(7):
            copy(k, me, me).wait_send()
        own_out.wait()

    return start, finish


def _gather_and_proj(x_seq, meta_full, norm_g, w_in_b, order):
    tm = TM_MAT
    n_m = TP // tm
    last_m = n_m - 1

    def body(order_ref, x_ref, meta_ref, g_ref, s0, proj_ref, xin_ref, h_out, o0, hbuf, wbuf, send_sems, recv_sems,
             local_sems):
        del order_ref
        n = pl.program_id(0)
        m = pl.program_id(1)
        x, y, c = _my_place()
        me, sibling = (x, y, c), (x, y, 1 - c)
        chips = [(1 - x, y), (x, 1 - y), (1 - x, 1 - y)]

        def block(place):
            return o0.at[_dev_index(*place)]

        def copy(k, place, to, from_src=False):
            return pltpu.make_async_remote_copy(
                src_ref=s0 if from_src else block(place), dst_ref=block(place),
                send_sem=send_sems.at[k], recv_sem=recv_sems.at[k], device_id=to, device_id_type=MESH_ID)

        def to_vmem(place, slot):
            return pltpu.make_async_copy(block(place), wbuf.at[slot], local_sems.at[slot])

        own_out = pltpu.make_async_copy(wbuf.at[0], block(me), local_sems.at[2])
        h_copy = pltpu.make_async_copy(hbuf, h_out, local_sems.at[3])

        @pl.when((n == 0) & (m == 0))
        def _():
            copy(0, me, sibling, from_src=True).start()
            for j, chip in enumerate(chips):
                copy(1 + j, me, (*chip, c), from_src=True).start()
            mine = pltpu.make_async_copy(s0, wbuf.at[0], local_sems.at[0])
            mine.start()
            mine.wait()
            own_out.start()

        @pl.when(n == 0)
        def _():
            xv = x_ref[...]
            xin_ref[...] = jnp.where(m == 0, pltpu.roll(xv, ROW0, 0), xv)

            @pl.when(m == 0)
            def _():
                xin_ref[0:PAD_FRONT, :] = jnp.zeros((PAD_FRONT, D), F32)
                xin_ref[PAD_FRONT:ROW0, :] = meta_ref[...]

            xv = xin_ref[...]
            r = lax.rsqrt(jnp.mean(xv * xv, axis=-1, keepdims=True) + EPS)
            hbuf[m] = (xv * r * g_ref[...]).astype(BF16)

        plan = [(sibling, (0, sibling), None)]
        for j, chip in enumerate(chips):
            plan.append(((*chip, c), (1 + j, (*chip, c)), 4 + j))
            plan.append(((*chip, 1 - c), (4 + j, (*chip, 1 - c)), None))

        for s, (place, (k, origin), pass_on) in enumerate(plan, start=1):
            @pl.when((n == s - 1) & (m == last_m))
            def _(s=s, place=place, k=k, origin=origin, pass_on=pass_on):
                copy(k, origin, me).wait_recv()
                if pass_on is not None:
                    copy(pass_on, place, sibling).start()
                if s == 2:
                    own_out.wait()
                to_vmem(place, s % 2).start()

            @pl.when((n == s) & (m == 0))
            def _(s=s, place=place):
                to_vmem(place, s % 2).wait()

        proj_ref[...] = _dot(hbuf[m], wbuf[lax.rem(n, 2)]).astype(BF16)

        @pl.when((n == 0) & (m == last_m))
        def _():
            h_copy.start()

        @pl.when((n == N_DEV - 1) & (m == last_m))
        def _():
            for k in range(7):
                copy(k, me, me).wait_send()
            h_copy.wait()

    return pl.pallas_call(
        body, name="gather_and_proj",
        grid_spec=pltpu.PrefetchScalarGridSpec(
            num_scalar_prefetch=1, grid=(N_DEV, n_m),
            in_specs=[pl.BlockSpec((pl.Element(tm), pl.Element(D)),
                                   lambda n, m, o: (_window_start(jnp.where(n == 0, m, 0), tm), 0)),
                      pl.BlockSpec((N_META, D), lambda n, m, o: (0, 0)),
                      pl.BlockSpec((1, D), lambda n, m, o: (0, 0)), ANY],
            out_specs=(pl.BlockSpec((tm, W_IN_BLK), lambda n, m, o: (m, o[n])),
                       pl.BlockSpec((tm, D), lambda n, m, o: (jnp.where(n == 0, m, last_m), 0)), ANY, ANY),
            scratch_shapes=[pltpu.VMEM((n_m, tm, D), BF16), pltpu.VMEM((2, D, W_IN_BLK), BF16),
                            pltpu.SemaphoreType.DMA((7,)), pltpu.SemaphoreType.DMA((7,)),
                            pltpu.SemaphoreType.DMA((4,))]),
        out_shape=(jax.ShapeDtypeStruct((TP, D_IN), BF16), jax.ShapeDtypeStruct((TP, D), F32),
                   jax.ShapeDtypeStruct((n_m, tm, D), BF16), jax.ShapeDtypeStruct((N_DEV, D, W_IN_BLK), BF16)),
        compiler_params=pltpu.CompilerParams(dimension_semantics=("arbitrary", "arbitrary")),
    )(order, x_seq, meta_full, norm_g, w_in_b)


def _chip_rel(x, y, r):
    return (jnp.bitwise_xor(x, r >> 1), jnp.bitwise_xor(y, r & 1))


def _exchange_small(pack, srs):
    def body(pk, sr, pk_all, sr_all, send_sems, recv_sems, local_sems):
        x, y, c = _my_place()
        my_id = _dev_index(x, y, c)
        mine = [pltpu.make_async_copy(pk, pk_all.at[my_id], local_sems.at[0]),
                pltpu.make_async_copy(sr.at[my_id], sr_all.at[my_id], local_sems.at[1])]
        for cp in mine:
            cp.start()
        copies = []
        for r in range(1, N_DEV):
            peer = (jnp.bitwise_xor(x, (r >> 2) & 1), jnp.bitwise_xor(y, (r >> 1) & 1), jnp.bitwise_xor(c, r & 1))
            peer_id = _dev_index(*peer)
            for a, (src, dst) in enumerate(((pk, pk_all.at[my_id]), (sr.at[peer_id], sr_all.at[my_id]))):
                cp = pltpu.make_async_remote_copy(
                    src_ref=src, dst_ref=dst, send_sem=send_sems.at[a * 7 + r - 1], recv_sem=recv_sems.at[a * 7 + r - 1],
                    device_id=peer, device_id_type=MESH_ID)
                cp.start()
                copies.append(cp)
        for cp in copies:
            cp.wait_recv()
        for cp in copies:
            cp.wait_send()
        for cp in mine:
            cp.wait()

    return pl.pallas_call(
        body, name="exchange_small",
        out_shape=(jax.ShapeDtypeStruct((N_DEV,) + pack.shape, F32), jax.ShapeDtypeStruct(srs.shape, F32)),
        in_specs=[ANY, ANY], out_specs=(ANY, ANY),
        scratch_shapes=[pltpu.SemaphoreType.DMA((14,)), pltpu.SemaphoreType.DMA((14,)), pltpu.SemaphoreType.DMA((2,))],
    )(pack, srs)


N_CB = D // HEAD_W


def _store_by_cb(ref, idx, rows, val):
    for cb in range(N_CB):
        ref[(*idx, cb, rows, slice(None))] = val[:, cb * HEAD_W:(cb + 1) * HEAD_W]


def _fill_shifts(sh, tm):
    n = tm + HALO - 8
    for s in range(1, 8):
        for cb in range(N_CB):
            sh[s, cb, 0:n, :] = sh[0, cb, s:s + n, :]


def _conv_fwd(proj, conv_w, conv_b):
    tm = TM_ELT
    strip = tm // CONV_STRIPS

    def body(p_ref, w_ref, b_ref, c0_ref, sh):
        i = pl.program_id(0)

        @pl.when(i == 0)
        def _():
            sh[0, :, 0:HALO, :] = jnp.zeros((N_CB, HALO, HEAD_W), F32)

        @pl.when(i > 0)
        def _():
            sh[0, :, 0:HALO, :] = sh[0, :, tm:tm + HALO, :]

        ga = p_ref[:, 0:D].astype(F32)
        gb = p_ref[:, D:2 * D].astype(F32)
        _store_by_cb(sh, (0,), slice(HALO, HALO + tm), ga * _sigmoid(gb))
        _fill_shifts(sh, tm)
        for cb in range(N_CB):
            cs = slice(cb * HEAD_W, (cb + 1) * HEAD_W)
            for st in range(CONV_STRIPS):
                acc = jnp.broadcast_to(b_ref[:, cs], (strip, HEAD_W))
                for j in range(CONV_K):
                    off = HALO - (CONV_K - 1) + j
                    lo = st * strip + 8 * (off // 8)
                    acc = acc + w_ref[j:j + 1, cs] * sh[off % 8, cb, lo:lo + strip, :]
                c0_ref[st * strip:(st + 1) * strip, cs] = acc

    return pl.pallas_call(
        body, name="conv_fwd", grid=(TP // tm,),
        in_specs=[pl.BlockSpec((tm, 2 * D), lambda i: (i, 0)), pl.BlockSpec((CONV_K, D), lambda i: (0, 0)),
                  pl.BlockSpec((1, D), lambda i: (0, 0))],
        out_specs=pl.BlockSpec((tm, D), lambda i: (i, 0)),
        out_shape=jax.ShapeDtypeStruct((TP, D), F32),
        scratch_shapes=[pltpu.VMEM((8, N_CB, HALO + tm, HEAD_W), F32)],
        compiler_params=pltpu.CompilerParams(dimension_semantics=("arbitrary",)),
    )(proj, conv_w, conv_b)


def _gates(p_ref, lbl_ref, chunk, bsc):
    lb = _sigmoid(lbl_ref[0:1, :] - lbl_ref[1:2, :])
    q_raw = p_ref[:, 0:D].astype(F32)
    f_raw = p_ref[:, D:2 * D].astype(F32)
    sq = _sigmoid(q_raw)
    q = q_raw * sq
    sg = _sigmoid(f_raw)
    f = lb + (1.0 - lb) * sg
    row = lax.broadcasted_iota(jnp.int32, (CHUNK, 1), 0) + chunk * CHUNK
    valid = row >= PAD_FRONT
    lf = jnp.where(valid, jnp.log(f), 0.0)
    k = jnp.where(valid, 1.0 - f, 0.0)
    r_i = lax.broadcasted_iota(jnp.int32, (CHUNK, CHUNK), 0)
    c_i = lax.broadcasted_iota(jnp.int32, (CHUNK, CHUNK), 1)
    causal = r_i >= c_i
    bsc[...] = _tri_matmul(causal.astype(BF16), lf)
    b = bsc[...]
    b_mid = bsc[CHUNK // 2 - 1:CHUNK // 2, :]
    b_last = bsc[CHUNK - 1:CHUNK, :]
    e_q = jnp.exp(b)
    e_qm = jnp.exp(b - b_mid)
    e_km = jnp.exp(b_mid - b)
    e_kh = jnp.exp(b_last - b)
    e_last = jnp.exp(b_last)
    return dict(lb=lb, q_raw=q_raw, sq=sq, q=q, sg=sg, f=f, k=k, valid=valid, causal=causal,
                e_q=e_q, e_qm=e_qm, e_km=e_km, e_kh=e_kh, e_last=e_last)


def _rec_fwd(proj, lb_logits, w3_b):
    cps = CHUNKS_PER_STEP
    rows = cps * CHUNK

    def body(p_ref, lbl_ref, w3s_ref, o_ref, s_ref, w3o_ref, st, bsc, w3buf, send_sems, recv_sems, local_sems):
        n = pl.program_id(0)
        gather_start, gather_finish = _w3_gather(w3s_ref, w3o_ref, w3buf, send_sems, recv_sems, local_sems)

        @pl.when(n == 0)
        def _():
            st[...] = jnp.zeros_like(st)
            gather_start()

        def prep(ci):
            g = _gates(p_ref.at[pl.ds(ci * CHUNK, CHUNK)], lbl_ref, n * cps + ci, bsc.at[ci])
            g["q1"] = (g["q"] * g["e_q"]).astype(BF16)
            g["qm"] = (g["q"] * g["e_qm"]).astype(BF16)
            g["km"] = (g["k"] * g["e_km"]).astype(BF16)
            g["kh"] = (g["k"] * g["e_kh"]).astype(BF16)
            return g

        def heads(ci, g):
            rs = pl.ds(ci * CHUNK, CHUNK)
            pv = p_ref.at[rs]
            s_ref[ci] = st[...]
            for h in range(HEADS):
                sl = slice(h * HEAD_W, (h + 1) * HEAD_W)
                v = pv[:, 2 * D + h * HEAD_W:2 * D + (h + 1) * HEAD_W]
                att = jnp.where(g["causal"], _dot_nt(g["qm"][:, sl], g["km"][:, sl]), 0.0).astype(BF16)
                s_h = st[h]
                o_ref[rs, sl] = _dot_nt(g["q1"][:, sl], s_h.astype(BF16)) + _dot(att, v)
                st[h] = s_h * g["e_last"][:, sl] + _dot_tn(v, g["kh"][:, sl])

        ready = prep(0)
        for ci in range(cps):
            coming = prep(ci + 1) if ci + 1 < cps else None
            heads(ci, ready)
            ready = coming

        @pl.when(n == N_CHUNK // cps - 1)
        def _():
            gather_finish()

    return pl.pallas_call(
        body, name="rec_fwd", grid=(N_CHUNK // cps,),
        in_specs=[pl.BlockSpec((rows, 3 * D), lambda n: (n, 1)), pl.BlockSpec((2, D), lambda n: (0, 0)), ANY],
        out_specs=(pl.BlockSpec((rows, D), lambda n: (n, 0)),
                   pl.BlockSpec((cps, HEADS, HEAD_W, HEAD_W), lambda n: (n, 0, 0, 0)), ANY),
        out_shape=(jax.ShapeDtypeStruct((TP, D), F32), jax.ShapeDtypeStruct((N_CHUNK, HEADS, HEAD_W, HEAD_W), F32),
                   jax.ShapeDtypeStruct((3, D, D), BF16)),
        scratch_shapes=[pltpu.VMEM((HEADS, HEAD_W, HEAD_W), F32), pltpu.VMEM((cps, CHUNK, D), F32),
                        pltpu.VMEM((3, W_ROW_BLK, D), BF16), pltpu.SemaphoreType.DMA((7,)),
                        pltpu.SemaphoreType.DMA((7,)), pltpu.SemaphoreType.DMA((2,))],
        compiler_params=pltpu.CompilerParams(dimension_semantics=("arbitrary",)),
    )(proj, lb_logits, w3_b)


def _rec_bwd(proj, lb_logits, d_o, s_start, dproj):
    cps = CHUNKS_PER_STEP
    rows = cps * CHUNK
    last = N_CHUNK // cps - 1

    def body(p_ref, lbl_ref, do_ref, s_ref, dproj_in, dp_ref, dlb_ref, dst, bsc, dq_sc, dk_sc, g_sc):
        del dproj_in
        n = pl.program_id(0)

        @pl.when(n == 0)
        def _():
            dst[...] = jnp.zeros_like(dst)
            dlb_ref[...] = jnp.zeros_like(dlb_ref)

        def prep(ci):
            g = _gates(p_ref.at[pl.ds(ci * CHUNK, CHUNK)], lbl_ref, (last - n) * cps + ci, bsc.at[ci])
            g["q1"] = (g["q"] * g["e_q"]).astype(BF16)
            qm_f = g["q"] * g["e_qm"]
            km_f = g["k"] * g["e_km"]
            g["qm"] = qm_f.astype(BF16)
            g["km"] = km_f.astype(BF16)
            g["qm_lo"] = (qm_f - g["qm"].astype(F32)).astype(BF16)
            g["km_lo"] = (km_f - g["km"].astype(F32)).astype(BF16)
            g["kh_f"] = g["k"] * g["e_kh"]
            g["kh"] = g["kh_f"].astype(BF16)
            return g

        def heads_and_post(ci, g):
            rs = pl.ds(ci * CHUNK, CHUNK)
            pv = p_ref.at[rs]
            dpv = dp_ref.at[rs]
            q1, qm, km, qm_lo, km_lo, kh_f, kh = (g[k] for k in ("q1", "qm", "km", "qm_lo", "km_lo", "kh_f", "kh"))
            for h in range(HEADS):
                sl = slice(h * HEAD_W, (h + 1) * HEAD_W)
                v = pv[:, 2 * D + h * HEAD_W:2 * D + (h + 1) * HEAD_W]
                d_oh = do_ref[rs, sl].astype(BF16)
                s0 = s_ref[ci, h]
                ds_end = dst[h]
                ds_end_b = ds_end.astype(BF16)
                att = jnp.where(g["causal"], _dot_nt(qm[:, sl], km[:, sl]), 0.0).astype(BF16)
                d_att = jnp.where(g["causal"], _dot_nt(d_oh, v), 0.0).astype(BF16)
                d_v = _dot_tn(att, d_oh) + _dot_nt(kh[:, sl], ds_end_b)
                d_qm2 = _dot(d_att, jnp.concatenate([km[:, sl], km_lo[:, sl]], axis=1))
                d_qm = d_qm2[:, 0:HEAD_W] + d_qm2[:, HEAD_W:2 * HEAD_W]
                d_q1 = _dot(d_oh, s0.astype(BF16))
                d_km2 = _dot_tn(d_att, jnp.concatenate([qm[:, sl], qm_lo[:, sl]], axis=1))
                d_km = d_km2[:, 0:HEAD_W] + d_km2[:, HEAD_W:2 * HEAD_W]
                d_kh = _dot(v, ds_end_b)
                dq_sc[ci, :, sl] = d_qm * g["e_qm"][:, sl] + d_q1 * g["e_q"][:, sl]
                dk_sc[ci, :, sl] = d_km * g["e_km"][:, sl] + d_kh * g["e_kh"][:, sl]
                g_sc[ci, :, sl] = (jnp.sum(kh_f[:, sl] * d_kh, axis=0, keepdims=True)
                                   + g["e_last"][:, sl] * jnp.sum(ds_end * s0, axis=0, keepdims=True))
                dst[h] = ds_end * g["e_last"][:, sl] + _dot_tn(d_oh, q1[:, sl])
                dpv[:, 2 * D + h * HEAD_W:2 * D + (h + 1) * HEAD_W] = d_v.astype(BF16)
            d_q = dq_sc[ci]
            d_k = dk_sc[ci]
            d_b = g["q"] * d_q - g["k"] * d_k
            anti = jnp.logical_not(g["causal"]) | (lax.broadcasted_iota(jnp.int32, (CHUNK, CHUNK), 0)
                                                    == lax.broadcasted_iota(jnp.int32, (CHUNK, CHUNK), 1))
            d_lf = _tri_matmul(anti.astype(BF16), d_b) + g_sc[ci]
            d_f = jnp.where(g["valid"], d_lf / g["f"] - d_k, 0.0)
            sg = g["sg"]
            dlb_ref[0:1, :] += jnp.sum(d_f * (1.0 - sg), axis=0, keepdims=True)
            dpv[:, 0:D] = (d_q * _dsilu(g["q_raw"], g["sq"])).astype(BF16)
            dpv[:, D:2 * D] = (d_f * (1.0 - g["lb"]) * sg * (1.0 - sg)).astype(BF16)

        ready = prep(cps - 1)
        for ci in reversed(range(cps)):
            coming = prep(ci - 1) if ci > 0 else None
            heads_and_post(ci, ready)
            ready = coming

    return pl.pallas_call(
        body, name="rec_bwd", grid=(N_CHUNK // cps,),
        in_specs=[pl.BlockSpec((rows, 3 * D), lambda n: (last - n, 1)), pl.BlockSpec((2, D), lambda n: (0, 0)),
                  pl.BlockSpec((rows, D), lambda n: (last - n, 0)),
                  pl.BlockSpec((cps, HEADS, HEAD_W, HEAD_W), lambda n: (last - n, 0, 0, 0)), ANY],
        out_specs=(pl.BlockSpec((rows, 3 * D), lambda n: (last - n, 1)), pl.BlockSpec((8, D), lambda n: (0, 0))),
        out_shape=(jax.ShapeDtypeStruct((TP, D_IN), BF16), jax.ShapeDtypeStruct((8, D), F32)),
        scratch_shapes=[pltpu.VMEM((HEADS, HEAD_W, HEAD_W), F32), pltpu.VMEM((cps, CHUNK, D), F32),
                        pltpu.VMEM((cps, CHUNK, D), F32), pltpu.VMEM((cps, CHUNK, D), F32),
                        pltpu.VMEM((cps, 1, D), F32)],
        input_output_aliases={4: 0},
        compiler_params=pltpu.CompilerParams(dimension_semantics=("arbitrary",)),
    )(proj, lb_logits, d_o, s_start, dproj)


def _mid(xin, tgt, o, c0, proj, w3, ln_g, ln_b, gnorm_g, final_g):
    tm = TM_ELT

    def body(x_ref, t_ref, o_ref, c0_ref, z_ref, gr_ref, mc_ref, mr_ref, w_ref, lng_ref, lnb_ref, gng_ref, fg_ref,
             dres_ref, do_ref, dc0_ref, dz_ref, dp_ref, a3_ref, b3_ref, red_ref, on_sc, don_sc):
        i = pl.program_id(0)

        @pl.when(i == 0)
        def _():
            red_ref[...] = jnp.zeros_like(red_ref)

        w_conv, w_rec, w_out = w_ref[0], w_ref[1], w_ref[2]
        c0v = c0_ref[...]
        mu = jnp.mean(c0v, axis=-1, keepdims=True)
        xc = c0v - mu
        rstd = lax.rsqrt(jnp.mean(xc * xc, axis=-1, keepdims=True) + EPS)
        xh = xc * rstd
        c1 = xh * lng_ref[...] + lnb_ref[...]
        s1 = _sigmoid(c1)
        c2 = c1 * s1
        z = z_ref[...].astype(F32)
        sz = _sigmoid(z)
        silu_z = z * sz
        u_conv = (c2 * silu_z).astype(BF16)
        y_conv = _dot(u_conv, w_conv)
        ov = o_ref[...]
        r3 = []
        for h in range(HEADS):
            sl = slice(h * HEAD_W, (h + 1) * HEAD_W)
            oh = ov[:, sl]
            r_h = lax.rsqrt(jnp.mean(oh * oh, axis=-1, keepdims=True) + EPS)
            r3.append(r_h)
            on_sc[:, sl] = oh * r_h
        o_n = on_sc[...]
        o_g = o_n * gng_ref[...]
        gr = gr_ref[...].astype(F32)
        sgr = _sigmoid(gr)
        silu_g = gr * sgr
        u_rec = (o_g * silu_g).astype(BF16)
        y_rec = _dot(u_rec, w_rec)
        mc = mc_ref[...].astype(F32)
        mr = mr_ref[...].astype(F32)
        smc = _sigmoid(mc)
        smr = _sigmoid(mr)
        merged = (smc * y_conv + smr * y_rec).astype(BF16)
        res = x_ref[...] + _dot(merged, w_out)
        r2 = lax.rsqrt(jnp.mean(res * res, axis=-1, keepdims=True) + EPS)
        xh2 = res * r2
        row = lax.broadcasted_iota(jnp.int32, (tm, 1), 0) + i * tm
        real = row >= ROW0
        tgt = t_ref[...]
        tgt = jnp.where(i == 0, pltpu.roll(tgt, ROW0, 0), tgt)
        diff = jnp.where(real, xh2 * fg_ref[...] - tgt, 0.0)
        d_y = diff * (1.0 / D)
        d_xh2 = d_y * fg_ref[...]
        d_res = r2 * (d_xh2 - xh2 * jnp.mean(d_xh2 * xh2, axis=-1, keepdims=True))
        dres_ref[...] = d_res
        d_res_b = d_res.astype(BF16)
        d_merged = _dot_nt(d_res_b, w_out)
        d_yc = (d_merged * smc).astype(BF16)
        d_yr = (d_merged * smr).astype(BF16)
        dp_ref[:, D:2 * D] = (d_merged * y_conv * smc * (1.0 - smc)).astype(BF16)
        dp_ref[:, 2 * D:3 * D] = (d_merged * y_rec * smr * (1.0 - smr)).astype(BF16)
        d_ur = _dot_nt(d_yr, w_rec)
        d_og = d_ur * silu_g
        dp_ref[:, 0:D] = (d_ur * o_g * _dsilu(gr, sgr)).astype(BF16)
        d_on = d_og * gng_ref[...]
        for h in range(HEADS):
            sl = slice(h * HEAD_W, (h + 1) * HEAD_W)
            d_h = d_on[:, sl]
            n_h = o_n[:, sl]
            don_sc[:, sl] = r3[h] * (d_h - n_h * jnp.mean(d_h * n_h, axis=-1, keepdims=True))
        do_ref[...] = don_sc[...]
        d_uc = _dot_nt(d_yc, w_conv)
        d_c2 = d_uc * silu_z
        dz_ref[...] = (d_uc * c2 * _dsilu(z, sz)).astype(BF16)
        d_c1 = d_c2 * _dsilu(c1, s1)
        d_xh = d_c1 * lng_ref[...]
        d_c0 = rstd * (d_xh - jnp.mean(d_xh, axis=-1, keepdims=True)
                       - xh * jnp.mean(d_xh * xh, axis=-1, keepdims=True))
        dc0_ref[...] = d_c0
        a3_ref[0] = u_conv
        b3_ref[0] = d_yc
        a3_ref[1] = u_rec
        b3_ref[1] = d_yr
        a3_ref[2] = merged
        b3_ref[2] = d_res_b
        def colsum(vv):
            return jnp.sum(vv, axis=0, keepdims=True)

        red_ref[0:1, :] += colsum(d_y * xh2)
        red_ref[1:2, :] += colsum(d_og * o_n)
        red_ref[2:3, :] += colsum(d_c1 * xh)
        red_ref[3:4, :] += colsum(d_c1)
        red_ref[4:5, :] += colsum(d_c0)
        red_ref[5:6, :] += colsum(diff * diff) * (0.5 / D)

    def row_block(width, col):
        return pl.BlockSpec((tm, width), lambda i: (i, col))

    def const_block(shape):
        return pl.BlockSpec(shape, lambda i: (0,) * len(shape))

    stack = jax.ShapeDtypeStruct((3, TP, D), BF16)
    stack_spec = pl.BlockSpec((3, tm, D), lambda i: (0, i, 0))
    return pl.pallas_call(
        body, name="mid", grid=(TP // tm,),
        in_specs=[row_block(D, 0),
                  pl.BlockSpec((pl.Element(tm), pl.Element(D)), lambda i: (_window_start(i, tm), 0)),
                  row_block(D, 0), row_block(D, 0),
                  row_block(D, 2), row_block(D, 6), row_block(D, 7), row_block(D, 8),
                  pl.BlockSpec((3, D, D), lambda i: (0, 0, 0), pipeline_mode=pl.Buffered(1)),
                  const_block((1, D)), const_block((1, D)), const_block((1, D)), const_block((1, D))],
        out_specs=(row_block(D, 0), row_block(D, 0), row_block(D, 0), row_block(D, 0), row_block(3 * D, 2),
                   stack_spec, stack_spec, const_block((8, D))),
        out_shape=(jax.ShapeDtypeStruct((TP, D), F32), jax.ShapeDtypeStruct((TP, D), F32),
                   jax.ShapeDtypeStruct((TP, D), F32), jax.ShapeDtypeStruct((TP, D), BF16),
                   jax.ShapeDtypeStruct((TP, D_IN), BF16), stack, stack, jax.ShapeDtypeStruct((8, D), F32)),
        scratch_shapes=[pltpu.VMEM((tm, D), F32), pltpu.VMEM((tm, D), F32)],
        compiler_params=pltpu.CompilerParams(dimension_semantics=("arbitrary",), vmem_limit_bytes=60 * 1024 * 1024),
    )(xin, tgt, o, c0, proj, proj, proj, proj, w3, ln_g, ln_b, gnorm_g, final_g)


def _conv_bwd(proj, d_c0, d_z, conv_w, dproj):
    tm = TM_ELT
    n_tile = TP // tm
    lastt = n_tile - 1

    strip = tm // CONV_STRIPS

    def body(p_ref, dc_ref, dz_ref, w_ref, dproj_in, dp_ref, dw_ref, dsh, a_sc, da_sc, acc):
        del dproj_in
        i = pl.program_id(0)

        @pl.when(i == 0)
        def _():
            dsh[0, :, tm:tm + HALO, :] = jnp.zeros((N_CB, HALO, HEAD_W), F32)
            acc[...] = jnp.zeros_like(acc)

        @pl.when(i > 0)
        def _():
            dsh[0, :, tm:tm + HALO, :] = dsh[0, :, 0:HALO, :]

        _store_by_cb(dsh, (0,), slice(0, tm), dc_ref[...])
        _fill_shifts(dsh, tm)
        ga = p_ref[:, 0:D].astype(F32)
        sb = _sigmoid(p_ref[:, D:2 * D].astype(F32))
        a = ga * sb
        _store_by_cb(a_sc, (), slice(0, tm), a)
        for cb in range(N_CB):
            cs = slice(cb * HEAD_W, (cb + 1) * HEAD_W)
            for st in range(CONV_STRIPS):
                rows = slice(st * strip, (st + 1) * strip)
                a_s = a_sc[cb, rows, :]
                d_a = jnp.zeros((strip, HEAD_W), F32)
                for j in range(CONV_K):
                    off = CONV_K - 1 - j
                    lo = st * strip + 8 * (off // 8)
                    slab = dsh[off % 8, cb, lo:lo + strip, :]
                    d_a = d_a + w_ref[j:j + 1, cs] * slab
                    acc[j, :, cs] += jnp.sum((a_s * slab).reshape(strip // 8, 8, HEAD_W), axis=0)
                da_sc[rows, cs] = d_a
        d_a = da_sc[...]
        dp_ref[:, 0:D] = (d_a * sb).astype(BF16)
        dp_ref[:, D:2 * D] = (d_a * a * (1.0 - sb)).astype(BF16)
        dp_ref[:, 2 * D:3 * D] = dz_ref[...]

        @pl.when(i == lastt)
        def _():
            for j in range(CONV_K):
                dw_ref[j:j + 1, :] = jnp.sum(acc[j], axis=0, keepdims=True)
            dw_ref[CONV_K:CONV_K + 1, :] = jnp.zeros((1, D), F32)

    return pl.pallas_call(
        body, name="conv_bwd", grid=(n_tile,),
        in_specs=[pl.BlockSpec((tm, 2 * D), lambda i: (lastt - i, 0)), pl.BlockSpec((tm, D), lambda i: (lastt - i, 0)),
                  pl.BlockSpec((tm, D), lambda i: (lastt - i, 0)), pl.BlockSpec((CONV_K, D), lambda i: (0, 0)), ANY],
        out_specs=(pl.BlockSpec((tm, 3 * D), lambda i: (lastt - i, 0)), pl.BlockSpec((CONV_K + 1, D), lambda i: (0, 0))),
        out_shape=(jax.ShapeDtypeStruct((TP, D_IN), BF16), jax.ShapeDtypeStruct((CONV_K + 1, D), F32)),
        scratch_shapes=[pltpu.VMEM((8, N_CB, tm + HALO, HEAD_W), F32), pltpu.VMEM((N_CB, tm, HEAD_W), F32),
                        pltpu.VMEM((tm, D), F32), pltpu.VMEM((CONV_K, 8, D), F32)],
        input_output_aliases={4: 0},
        compiler_params=pltpu.CompilerParams(dimension_semantics=("arbitrary",)),
    )(proj, d_c0, d_z, conv_w, dproj)


def _wgrad3(a3, b3):
    tt = TT_WGRAD

    def body(a_ref, b_ref, o_ref):
        @pl.when(pl.program_id(1) == 0)
        def _():
            o_ref[...] = jnp.zeros_like(o_ref)

        o_ref[0] += _dot_tn(a_ref[0], b_ref[0])

    return pl.pallas_call(
        body, name="wgrad3", grid=(3, TP // tt),
        in_specs=[pl.BlockSpec((1, tt, D), lambda g, t: (g, t, 0)), pl.BlockSpec((1, tt, D), lambda g, t: (g, t, 0))],
        out_specs=pl.BlockSpec((1, D, D), lambda g, t: (g, 0, 0)),
        out_shape=jax.ShapeDtypeStruct((3, D, D), F32),
        compiler_params=pltpu.CompilerParams(dimension_semantics=("arbitrary", "arbitrary")),
    )(a3, b3)


def _wgrad_in(h, dproj, ids, p3):
    tt = TT_WGRAD
    n_t = TP // tt

    def body(ids_ref, a_ref, b_ref, p3_ref, o_ref, ob_ref, l0_ref, l1_ref, acc, tmp, send_sems, recv_sems, tmp_sem):
        del ids_ref
        r = pl.program_id(0)
        t = pl.program_id(1)
        x, y, c = _my_place()
        sibling = (x, y, 1 - c)
        slot = lax.rem(r, 2)

        def send_in(q):
            return pltpu.make_async_remote_copy(
                src_ref=acc.at[q % 2], dst_ref=l0_ref.at[q], send_sem=send_sems.at[q], recv_sem=recv_sems.at[q],
                device_id=sibling, device_id_type=MESH_ID)

        def send_3(q):
            d = _dev_index(*_chip_rel(x, y, q), 1 - c)
            return pltpu.make_async_remote_copy(
                src_ref=p3_ref.at[:, pl.ds(pl.multiple_of(d * W_ROW_BLK, W_ROW_BLK), W_ROW_BLK), :],
                dst_ref=l1_ref.at[q], send_sem=send_sems.at[4 + q], recv_sem=recv_sems.at[4 + q],
                device_id=sibling, device_id_type=MESH_ID)

        def landed(q):
            return pltpu.make_async_copy(l0_ref.at[q], tmp, tmp_sem)

        @pl.when((r == 0) & (t == 0))
        def _():
            for q in range(4):
                send_3(q).start()

        @pl.when(t == 0)
        def _():
            acc[slot] = jnp.zeros((D, W_IN_BLK), F32)

        acc[slot] += _dot_tn(a_ref[...], b_ref[...])

        for q in range(4):
            @pl.when((r == q) & (t == n_t - 1))
            def _(q=q):
                if q >= 1:
                    send_in(q - 1).wait_send()
                send_in(q).start()

            @pl.when((r == 4 + q) & (t == n_t - 2))
            def _(q=q):
                if q == 0:
                    send_in(3).wait_send()
                send_in(q).wait_recv()
                landed(q).start()

            @pl.when((r == 4 + q) & (t == n_t - 1))
            def _(q=q):
                landed(q).wait()
                tot = acc[q % 2] + tmp[...]
                o_ref[0] = tot
                ob_ref[0] = tot.astype(BF16)

        @pl.when((r == 7) & (t == n_t - 1))
        def _():
            for q in range(4):
                send_3(q).wait_recv()
            for q in range(4):
                send_3(q).wait_send()

    blk = pl.BlockSpec((1, D, W_IN_BLK), lambda r, t, ids: (jnp.maximum(r - 4, 0), 0, 0))
    return pl.pallas_call(
        body, name="wgrad_in",
        grid_spec=pltpu.PrefetchScalarGridSpec(
            num_scalar_prefetch=1, grid=(N_DEV, n_t),
            in_specs=[pl.BlockSpec((tt, D), lambda r, t, ids: (t, 0)),
                      pl.BlockSpec((tt, W_IN_BLK), lambda r, t, ids: (t, ids[r])), ANY],
            out_specs=(blk, blk, ANY, ANY),
            scratch_shapes=[pltpu.VMEM((2, D, W_IN_BLK), F32), pltpu.VMEM((D, W_IN_BLK), F32),
                            pltpu.SemaphoreType.DMA((8,)), pltpu.SemaphoreType.DMA((8,)), pltpu.SemaphoreType.DMA]),
        out_shape=(jax.ShapeDtypeStruct((4, D, W_IN_BLK), F32), jax.ShapeDtypeStruct((4, D, W_IN_BLK), BF16),
                   jax.ShapeDtypeStruct((4, D, W_IN_BLK), F32), jax.ShapeDtypeStruct((4, 3, W_ROW_BLK, D), F32)),
        compiler_params=pltpu.CompilerParams(dimension_semantics=("arbitrary", "arbitrary")),
    )(ids, h, dproj, p3)


def _chip_sum_3(p3, land1, ids_mine):
    def body(ids_ref, p_ref, l_ref, o_ref, ob_ref):
        del ids_ref
        tot = p_ref[...] + l_ref[0]
        o_ref[0] = tot
        ob_ref[0] = tot.astype(BF16)

    blk = pl.BlockSpec((1, 3, W_ROW_BLK, D), lambda r, ids: (r, 0, 0, 0))
    return pl.pallas_call(
        body, name="chip_sum_3",
        grid_spec=pltpu.PrefetchScalarGridSpec(
            num_scalar_prefetch=1, grid=(4,),
            in_specs=[pl.BlockSpec((3, W_ROW_BLK, D), lambda r, ids: (0, ids[r], 0)), blk],
            out_specs=(blk, blk)),
        out_shape=(jax.ShapeDtypeStruct((4, 3, W_ROW_BLK, D), F32), jax.ShapeDtypeStruct((4, 3, W_ROW_BLK, D), BF16)),
    )(ids_mine, p3, land1)


def _dh_and_norm_bwd(dproj, w_in_full, xin, d_res, norm_g, chip0b, chip1b):
    tm = TM_MAT
    n_k = N_DEV // DH_K_BLKS
    n_m = TP // tm

    def body(dp_ref, w_ref, x_ref, dr_ref, g_ref, c0_ref, c1_ref, dx_ref, dg_ref, f0_ref, f1_ref, acc,
             send_sems, recv_sems):
        m = pl.program_id(0)
        k = pl.program_id(1)
        x, y, c = _my_place()

        def to_owner(a, q):
            src, dst = ((c0_ref, f0_ref), (c1_ref, f1_ref))[a]
            return pltpu.make_async_remote_copy(
                src_ref=src.at[q], dst_ref=dst.at[q - 1], send_sem=send_sems.at[a * 3 + q - 1],
                recv_sem=recv_sems.at[a * 3 + q - 1], device_id=(*_chip_rel(x, y, q), c), device_id_type=MESH_ID)

        @pl.when((m == 0) & (k == 0))
        def _():
            for q in range(1, 4):
                for a in range(2):
                    to_owner(a, q).start()

        @pl.when(k == 0)
        def _():
            acc[...] = jnp.zeros_like(acc)

        part = _dot_nt(dp_ref[:, 0:W_IN_BLK], w_ref[0])
        for j in range(1, DH_K_BLKS):
            part = part + _dot_nt(dp_ref[:, j * W_IN_BLK:(j + 1) * W_IN_BLK], w_ref[j])
        acc[...] += part

        @pl.when((k == n_k - 1) & (m == 0))
        def _():
            dg_ref[...] = jnp.zeros_like(dg_ref)

        @pl.when(k == n_k - 1)
        def _():
            xv = x_ref[...]
            r1 = lax.rsqrt(jnp.mean(xv * xv, axis=-1, keepdims=True) + EPS)
            xh = xv * r1
            d_h = acc[...]
            dg_ref[0:1, :] += jnp.sum(d_h * xh, axis=0, keepdims=True)
            d_xh = d_h * g_ref[...]
            dx_ref[...] = dr_ref[...] + r1 * (d_xh - xh * jnp.mean(d_xh * xh, axis=-1, keepdims=True))

        @pl.when((m == n_m - 1) & (k == n_k - 1))
        def _():
            for q in range(1, 4):
                for a in range(2):
                    to_owner(a, q).wait_recv()
            for q in range(1, 4):
                for a in range(2):
                    to_owner(a, q).wait_send()

    return pl.pallas_call(
        body, name="dh_norm_bwd", grid=(n_m, n_k),
        in_specs=[pl.BlockSpec((tm, DH_K_BLKS * W_IN_BLK), lambda m, k: (m, k)),
                  pl.BlockSpec((DH_K_BLKS, D, W_IN_BLK), lambda m, k: (k, 0, 0)),
                  pl.BlockSpec((tm, D), lambda m, k: (m, 0)), pl.BlockSpec((tm, D), lambda m, k: (m, 0)),
                  pl.BlockSpec((1, D), lambda m, k: (0, 0)), ANY, ANY],
        out_specs=(pl.BlockSpec((tm, D), lambda m, k: (m, 0)), pl.BlockSpec((8, D), lambda m, k: (0, 0)), ANY, ANY),
        out_shape=(jax.ShapeDtypeStruct((TP, D), F32), jax.ShapeDtypeStruct((8, D), F32),
                   jax.ShapeDtypeStruct((3, D, W_IN_BLK), BF16), jax.ShapeDtypeStruct((3, 3, W_ROW_BLK, D), BF16)),
        scratch_shapes=[pltpu.VMEM((tm, D), F32), pltpu.SemaphoreType.DMA((6,)), pltpu.SemaphoreType.DMA((6,))],
        compiler_params=pltpu.CompilerParams(dimension_semantics=("arbitrary", "arbitrary")),
    )(dproj, w_in_full, xin, d_res, norm_g, chip0b, chip1b)


def _sum_adamw(own, landed, w, m, v, tr, name):
    rows, cols = w.shape
    n_t = rows // tr

    def body(o_ref, l1_ref, l2_ref, l3_ref, w_ref, m_ref, v_ref, g_ref, d_ref, m2_ref, v2_ref):
        g = ((o_ref[...] + l1_ref[...].astype(F32)) + l2_ref[...].astype(F32)) + l3_ref[...].astype(F32)
        delta, m2, v2 = _adamw(w_ref[...], g, m_ref[...], v_ref[...])
        g_ref[...] = g
        d_ref[...] = delta
        m2_ref[...] = m2
        v2_ref[...] = v2

    def spec(k):
        return pl.BlockSpec((tr, cols), lambda i: (i + k * n_t, 0))

    out = jax.ShapeDtypeStruct((rows, cols), F32)
    return pl.pallas_call(
        body, name=name, grid=(n_t,),
        in_specs=[spec(0), spec(0), spec(1), spec(2), spec(0), spec(0), spec(0)],
        out_specs=(spec(0),) * 4, out_shape=(out,) * 4,
    )(own, landed, landed, landed, w, m, v)


def _small_update(pack_all, srs_all, lb_logits, p8, m8, v8, ws, ms, vs):
    def body(pk_ref, sr_ref, lbl_ref, p_ref, m_ref, v_ref, ws_ref, ms_ref, vs_ref,
             g8_ref, d8_ref, m8_ref, v8_ref, loss_ref, gs_ref, ds_ref, ms2_ref, vs2_ref):
        tot = pk_ref[0]
        tot_s = sr_ref[0]
        for d in range(1, N_DEV):
            tot = tot + pk_ref[d]
            tot_s = tot_s + sr_ref[d]
        p0 = _sigmoid(lbl_ref[0:1, :] - lbl_ref[1:2, :])
        row = lax.broadcasted_iota(jnp.int32, (8, D), 0)
        d_lb = jnp.sum(jnp.where(row == 4, tot, 0.0), axis=0, keepdims=True)
        d_l0 = d_lb * p0 * (1.0 - p0)
        loss_ref[...] = jnp.sum(jnp.where(row == 5, tot, 0.0), keepdims=True).reshape(1, 1)
        g8 = jnp.where(row == 4, d_l0, jnp.where(row == 5, -d_l0, tot))
        delta, m2, v2 = _adamw(p_ref[...], g8, m_ref[...], v_ref[...])
        g8_ref[...] = g8
        d8_ref[...] = delta
        m8_ref[...] = m2
        v8_ref[...] = v2
        delta, m2, v2 = _adamw(ws_ref[...], tot_s, ms_ref[...], vs_ref[...])
        gs_ref[...] = tot_s
        ds_ref[...] = delta
        ms2_ref[...] = m2
        vs2_ref[...] = v2

    o8 = jax.ShapeDtypeStruct((8, D), F32)
    os_ = jax.ShapeDtypeStruct((SMALL_ROWS, HEAD_W), F32)
    return pl.pallas_call(
        body, name="small_update",
        out_shape=(o8, o8, o8, o8, jax.ShapeDtypeStruct((1, 1), F32), os_, os_, os_, os_),
    )(pack_all, srs_all, lb_logits, p8, m8, v8, ws, ms, vs)


def _local_step(xin, proj, target, conv_w_full, conv_b, ln_g, ln_b, w3_b, lb_logits, gnorm_g, final_g):
    fg = final_g.reshape(1, D)
    c0 = _conv_fwd(proj, conv_w_full, conv_b)
    o, s_start, w3_full = _rec_fwd(proj, lb_logits, w3_b)
    d_res, d_o, d_c0, d_z, dproj, a3, b3, red = _mid(xin, target, o, c0, proj, w3_full, ln_g, ln_b, gnorm_g, fg)
    dproj, dlb = _rec_bwd(proj, lb_logits, d_o, s_start, dproj)
    dproj, d_conv_w = _conv_bwd(proj, d_c0, d_z, conv_w_full, dproj)
    p3 = _wgrad3(a3, b3)
    return dproj, d_res, p3, d_conv_w, red, dlb


def kernel(x, meta_tokens, norm_g, w_in, conv_w, conv_b, ln_g, ln_b, w_conv_out, lb_logits, gnorm_g, w_rec_out, w_out, final_g, loss_target, m_meta_tokens, m_norm_g, m_w_in, m_conv_w, m_conv_b, m_ln_g, m_ln_b, m_w_conv_out, m_lb_logits, m_gnorm_g, m_w_rec_out, m_w_out, m_final_g, v_meta_tokens, v_norm_g, v_w_in, v_conv_w, v_conv_b, v_ln_g, v_ln_b, v_w_conv_out, v_lb_logits, v_gnorm_g, v_w_rec_out, v_w_out, v_final_g):
    def small_pack(cw, mt):
        return jnp.concatenate([cw[0], jnp.zeros((1, HEAD_W), F32), mt], axis=0)

    def stack3(a, b, c):
        return jnp.concatenate([a, b, c], axis=0)

    def stack8(ng, cb, lg, lb_, lbl, gg, fg):
        return jnp.concatenate([ng, cb, lg, lb_, lbl, gg, fg.reshape(1, D)], axis=0)

    mx, my, mc = _my_place()

    w3_s = stack3(w_conv_out, w_rec_out, w_out)
    ws_s = small_pack(conv_w, meta_tokens)
    small_full = jnp.transpose(_gather_small(ws_s), (1, 0, 2)).reshape(SMALL_ROWS, D)
    conv_w_full = small_full[0:CONV_K]
    meta_full = small_full[META_ROW:META_ROW + N_META]
    w_in_b, w3_b = _cast_shards(w_in[0], w3_s)
    use_order = [(mx, my, mc), (mx, my, 1 - mc)]
    for chip in ((1 - mx, my), (mx, 1 - my), (1 - mx, 1 - my)):
        use_order += [(*chip, mc), (*chip, 1 - mc)]
    order = jnp.stack([_dev_index(*p) for p in use_order]).astype(jnp.int32)
    proj, xin, h, w_in_full = _gather_and_proj(x[0], meta_full, norm_g, w_in_b, order)
    h = h.reshape(TP, D)

    dproj, d_res, p3, d_conv_w, red, dlb = _local_step(
        xin, proj, loss_target[0], conv_w_full, conv_b, ln_g, ln_b, w3_b, lb_logits, gnorm_g, final_g)

    ids_mine = jnp.stack([_dev_index(*_chip_rel(mx, my, r), mc) for r in range(4)]).astype(jnp.int32)
    ids_sib = jnp.stack([_dev_index(*_chip_rel(mx, my, r), 1 - mc) for r in range(4)]).astype(jnp.int32)
    chip0, chip0b, _, land1 = _wgrad_in(h, dproj, jnp.concatenate([ids_sib, ids_mine]), p3)
    chip1, chip1b = _chip_sum_3(p3, land1, ids_mine)
    d_xin, dng, far0, far1 = _dh_and_norm_bwd(dproj, w_in_full, xin, d_res, norm_g, chip0b, chip1b)
    pack = jnp.concatenate([dng[0:1], red[4:5], red[2:3], red[3:4], dlb[0:1], red[5:6], red[1:2], red[0:1]], axis=0)
    g_in, d_in, m_in, v_in = _sum_adamw(chip0.reshape(4 * D, W_IN_BLK), far0.reshape(3 * D, W_IN_BLK), w_in[0],
                                        m_w_in[0], v_w_in[0], 256, "adamw_in")
    g_3, d_3, m_3, v_3 = _sum_adamw(
        chip1.reshape(12 * W_ROW_BLK, D), far1.reshape(9 * W_ROW_BLK, D), w3_s.reshape(3 * W_ROW_BLK, D),
        stack3(m_w_conv_out, m_w_rec_out, m_w_out).reshape(3 * W_ROW_BLK, D),
        stack3(v_w_conv_out, v_w_rec_out, v_w_out).reshape(3 * W_ROW_BLK, D), 3 * W_ROW_BLK, "adamw_3")

    srs = jnp.concatenate([d_conv_w, d_xin[PAD_FRONT:ROW0]], axis=0)
    srs = jnp.transpose(srs.reshape(SMALL_ROWS, N_DEV, HEAD_W), (1, 0, 2))
    pack_all, srs_all = _exchange_small(pack, srs)
    g8, d8, m8, v8, loss, gs, ds, ms, vs = _small_update(
        pack_all, srs_all, lb_logits,
        stack8(norm_g, conv_b, ln_g, ln_b, lb_logits, gnorm_g, final_g),
        stack8(m_norm_g, m_conv_b, m_ln_g, m_ln_b, m_lb_logits, m_gnorm_g, m_final_g),
        stack8(v_norm_g, v_conv_b, v_ln_g, v_ln_b, v_lb_logits, v_gnorm_g, v_final_g),
        ws_s, small_pack(m_conv_w, m_meta_tokens), small_pack(v_conv_w, v_meta_tokens))

    def unpack(a_in, a_3, a_s, a_8):
        t3 = a_3.reshape(3, 1, W_ROW_BLK, D)
        return (a_s[META_ROW:META_ROW + N_META], a_8[0:1], a_in[None], a_s[0:CONV_K][None], a_8[1:2], a_8[2:3],
                a_8[3:4], t3[0], a_8[4:6], a_8[6:7], t3[1], t3[2], a_8[7])

    grad_x = d_xin[ROW0:][None]
    return (loss.reshape(()), grad_x, *unpack(g_in, g_3, gs, g8), *unpack(d_in, d_3, ds, d8),
            *unpack(m_in, m_3, ms, m8), *unpack(v_in, v_3, vs, v8))
```

```python
import functools

import jax
import jax.numpy as jnp
from jax import lax
from jax.experimental import pallas as pl
from jax.experimental.pallas import tpu as pltpu

F32 = jnp.float32
BF16 = jnp.bfloat16

D = 1024
SEQ = 4096
N_META = 16
CHUNK = 64
PAD_FRONT = 48
ROW0 = PAD_FRONT + N_META
TP = ROW0 + SEQ
N_CHUNK = TP // CHUNK
HEADS = 8
HEAD_W = 128
D_IN = 9 * D
N_DEV = 8
W_IN_BLK = D_IN // N_DEV
W_ROW_BLK = D // N_DEV
CONV_K = 31
SMALL_ROWS = 48
META_ROW = 32
EPS = 1e-6
HALO = 32

TM_MAT = 832
TT_WGRAD = 2080
DH_K_BLKS = 2
TM_ELT = 208
CHUNKS_PER_STEP = 5
CONV_STRIPS = 2

ADAM_LR = 0.001
ADAM_B1 = 0.9
ADAM_B2 = 0.999
ADAM_EPS = 1e-08
ADAM_WD = 0.01
ADAM_STEP = 10

MESH_ID = pl.DeviceIdType.MESH
ANY = pl.BlockSpec(memory_space=pl.ANY)


def _sigmoid(v):
    return jax.nn.sigmoid(v)


def _dsilu(v, s):
    return s * (1.0 + v * (1.0 - s))


def _dot(a, b):
    return jnp.dot(a, b, preferred_element_type=F32)


def _dot_nt(a, b):
    return lax.dot_general(a, b, (((1,), (1,)), ((), ())), preferred_element_type=F32)


def _dot_tn(a, b):
    return lax.dot_general(a, b, (((0,), (0,)), ((), ())), preferred_element_type=F32)


def _split3(v):
    hi = v.astype(BF16)
    r1 = v - hi.astype(F32)
    mid = r1.astype(BF16)
    lo = (r1 - mid.astype(F32)).astype(BF16)
    return hi, mid, lo


def _tri_matmul(tri, v):
    hi, mid, lo = _split3(v)
    return _dot(tri, hi) + _dot(tri, mid) + _dot(tri, lo)


def _adamw(w, g, m, v):
    m2 = ADAM_B1 * m + (1.0 - ADAM_B1) * g
    v2 = ADAM_B2 * v + (1.0 - ADAM_B2) * jnp.square(g)
    m_hat = m2 / (1.0 - ADAM_B1 ** ADAM_STEP)
    v_hat = v2 / (1.0 - ADAM_B2 ** ADAM_STEP)
    delta = -ADAM_LR * (m_hat / (jnp.sqrt(v_hat) + ADAM_EPS) + ADAM_WD * w)
    return delta, m2, v2


def _window_start(i, tm):
    assert tm % 16 == 0 and ROW0 % 16 == 0
    return pl.multiple_of(16 * jnp.maximum((tm // 16) * i - ROW0 // 16, 0), 16)


def _my_place():
    return lax.axis_index("x"), lax.axis_index("y"), lax.axis_index("c")


def _dev_index(px, py, pc):
    return 4 * px + 2 * py + pc


def _cast_shards(w_in_s, w3_s):
    def body(a_ref, b_ref, oa_ref, ob_ref):
        oa_ref[...] = a_ref[...].astype(BF16)
        ob_ref[...] = b_ref[...].astype(BF16)

    return pl.pallas_call(
        body, name="cast_shards",
        out_shape=(jax.ShapeDtypeStruct(w_in_s.shape, BF16), jax.ShapeDtypeStruct(w3_s.shape, BF16)),
    )(w_in_s, w3_s)


def _peer(x, y, c, r):
    return (jnp.bitwise_xor(x, (r >> 2) & 1), jnp.bitwise_xor(y, (r >> 1) & 1), jnp.bitwise_xor(c, r & 1))


def _gather_small(small_s):
    def body(s_ref, o_ref, send_sems, recv_sems, local_sem):
        x, y, c = _my_place()
        my_id = _dev_index(x, y, c)
        mine = pltpu.make_async_copy(s_ref, o_ref.at[my_id], local_sem)
        mine.start()
        copies = []
        for r in range(1, N_DEV):
            cp = pltpu.make_async_remote_copy(
                src_ref=s_ref, dst_ref=o_ref.at[my_id], send_sem=send_sems.at[r - 1], recv_sem=recv_sems.at[r - 1],
                device_id=_peer(x, y, c, r), device_id_type=MESH_ID)
            cp.start()
            copies.append(cp)
        for cp in copies:
            cp.wait_recv()
        for cp in copies:
            cp.wait_send()
        mine.wait()

    return pl.pallas_call(
        body, name="gather_small", out_shape=jax.ShapeDtypeStruct((N_DEV,) + small_s.shape, F32),
        in_specs=[ANY], out_specs=ANY,
        scratch_shapes=[pltpu.SemaphoreType.DMA((7,)), pltpu.SemaphoreType.DMA((7,)), pltpu.SemaphoreType.DMA],
    )(small_s)


def _w3_gather(src, out, stage, send_sems, recv_sems, local_sems):
    x, y, c = _my_place()
    me, sibling = (x, y, c), (x, y, 1 - c)
    chips = [(1 - x, y), (x, 1 - y), (1 - x, 1 - y)]

    def block(place):
        d = _dev_index(*place)
        return out.at[:, pl.ds(pl.multiple_of(d * W_ROW_BLK, W_ROW_BLK), W_ROW_BLK), :]

    def copy(k, place, to, from_src=False):
        return pltpu.make_async_remote_copy(
            src_ref=src if from_src else block(place), dst_ref=block(place),
            send_sem=send_sems.at[k], recv_sem=recv_sems.at[k], device_id=to, device_id_type=MESH_ID)

    own_in = pltpu.make_async_copy(src, stage, local_sems.at[0])
    own_out = pltpu.make_async_copy(stage, block(me), local_sems.at[1])

    def start():
        copy(0, me, sibling, from_src=True).start()
        for j, chip in enumerate(chips):
            copy(1 + j, me, (*chip, c), from_src=True).start()
        own_in.start()
        own_in.wait()
        own_out.start()

    def finish():
        for j, chip in enumerate(chips):
            copy(1 + j, (*chip, c), me).wait_recv()
            copy(4 + j, (*chip, c), sibling).start()
        copy(0, sibling, me).wait_recv()
        for j, chip in enumerate(chips):
            copy(4 + j, (*chip, 1 - c), me).wait_recv()
        for k in range(7):
            copy(k, me, me).wait_send()
        own_out.wait()

    return start, finish


def _p3_to_sibling(p3_ref, land_ref, send_sems, recv_sems):
    x, y, c = _my_place()

    def cp(q):
        d = _dev_index(*_chip_rel(x, y, q), 1 - c)
        return pltpu.make_async_remote_copy(
            src_ref=p3_ref.at[:, pl.ds(pl.multiple_of(d * W_ROW_BLK, W_ROW_BLK), W_ROW_BLK), :],
            dst_ref=land_ref.at[q], send_sem=send_sems.at[q], recv_sem=recv_sems.at[q],
            device_id=(x, y, 1 - c), device_id_type=MESH_ID)

    def start():
        for q in range(4):
            cp(q).start()

    def finish():
        for q in range(4):
            cp(q).wait_recv()
        for q in range(4):
            cp(q).wait_send()

    return start, finish


def _partials_to_owners(src_ref, far_ref, send_sems, recv_sems):
    x, y, c = _my_place()

    def cp(q):
        return pltpu.make_async_remote_copy(
            src_ref=src_ref.at[q], dst_ref=far_ref.at[q - 1], send_sem=send_sems.at[q - 1],
            recv_sem=recv_sems.at[q - 1], device_id=(*_chip_rel(x, y, q), c), device_id_type=MESH_ID)

    def start():
        for q in range(1, 4):
            cp(q).start()

    def finish():
        for q in range(1, 4):
            cp(q).wait_recv()
        for q in range(1, 4):
            cp(q).wait_send()

    return start, finish


def _gather_and_proj(x_seq, meta_full, norm_g, w_in_b, order):
    tm = TM_MAT
    n_m = TP // tm
    last_m = n_m - 1

    def body(order_ref, x_ref, meta_ref, g_ref, s0, proj_ref, xin_ref, h_out, o0, hbuf, wbuf, send_sems, recv_sems,
             local_sems):
        del order_ref
        n = pl.program_id(0)
        m = pl.program_id(1)
        x, y, c = _my_place()
        me, sibling = (x, y, c), (x, y, 1 - c)
        chips = [(1 - x, y), (x, 1 - y), (1 - x, 1 - y)]

        def block(place):
            return o0.at[_dev_index(*place)]

        def copy(k, place, to, from_src=False):
            return pltpu.make_async_remote_copy(
                src_ref=s0 if from_src else block(place), dst_ref=block(place),
                send_sem=send_sems.at[k], recv_sem=recv_sems.at[k], device_id=to, device_id_type=MESH_ID)

        def to_vmem(place, slot):
            return pltpu.make_async_copy(block(place), wbuf.at[slot], local_sems.at[slot])

        own_out = pltpu.make_async_copy(wbuf.at[0], block(me), local_sems.at[2])
        h_copy = pltpu.make_async_copy(hbuf, h_out, local_sems.at[3])

        @pl.when((n == 0) & (m == 0))
        def _():
            copy(0, me, sibling, from_src=True).start()
            for j, chip in enumerate(chips):
                copy(1 + j, me, (*chip, c), from_src=True).start()
            mine = pltpu.make_async_copy(s0, wbuf.at[0], local_sems.at[0])
            mine.start()
            mine.wait()
            own_out.start()

        @pl.when(n == 0)
        def _():
            xv = x_ref[...]
            xin_ref[...] = jnp.where(m == 0, pltpu.roll(xv, ROW0, 0), xv)

            @pl.when(m == 0)
            def _():
                xin_ref[0:PAD_FRONT, :] = jnp.zeros((PAD_FRONT, D), F32)
                xin_ref[PAD_FRONT:ROW0, :] = meta_ref[...]

            xv = xin_ref[...]
            r = lax.rsqrt(jnp.mean(xv * xv, axis=-1, keepdims=True) + EPS)
            hbuf[m] = (xv * r * g_ref[...]).astype(BF16)

        plan = [(sibling, (0, sibling), None)]
        for j, chip in enumerate(chips):
            plan.append(((*chip, c), (1 + j, (*chip, c)), 4 + j))
            plan.append(((*chip, 1 - c), (4 + j, (*chip, 1 - c)), None))

        for s, (place, (k, origin), pass_on) in enumerate(plan, start=1):
            @pl.when((n == s - 1) & (m == last_m))
            def _(s=s, place=place, k=k, origin=origin, pass_on=pass_on):
                copy(k, origin, me).wait_recv()
                if pass_on is not None:
                    copy(pass_on, place, sibling).start()
                if s == 2:
                    own_out.wait()
                to_vmem(place, s % 2).start()

            @pl.when((n == s) & (m == 0))
            def _(s=s, place=place):
                to_vmem(place, s % 2).wait()

        proj_ref[...] = _dot(hbuf[m], wbuf[lax.rem(n, 2)]).astype(BF16)

        @pl.when((n == 0) & (m == last_m))
        def _():
            h_copy.start()

        @pl.when((n == N_DEV - 1) & (m == last_m))
        def _():
            for k in range(7):
                copy(k, me, me).wait_send()
            h_copy.wait()

    return pl.pallas_call(
        body, name="gather_and_proj",
        grid_spec=pltpu.PrefetchScalarGridSpec(
            num_scalar_prefetch=1, grid=(N_DEV, n_m),
            in_specs=[pl.BlockSpec((pl.Element(tm), pl.Element(D)),
                                   lambda n, m, o: (_window_start(jnp.where(n == 0, m, 0), tm), 0)),
                      pl.BlockSpec((N_META, D), lambda n, m, o: (0, 0)),
                      pl.BlockSpec((1, D), lambda n, m, o: (0, 0)), ANY],
            out_specs=(pl.BlockSpec((tm, W_IN_BLK), lambda n, m, o: (m, o[n])),
                       pl.BlockSpec((tm, D), lambda n, m, o: (jnp.where(n == 0, m, last_m), 0)), ANY, ANY),
            scratch_shapes=[pltpu.VMEM((n_m, tm, D), BF16), pltpu.VMEM((2, D, W_IN_BLK), BF16),
                            pltpu.SemaphoreType.DMA((7,)), pltpu.SemaphoreType.DMA((7,)),
                            pltpu.SemaphoreType.DMA((4,))]),
        out_shape=(jax.ShapeDtypeStruct((TP, D_IN), BF16), jax.ShapeDtypeStruct((TP, D), F32),
                   jax.ShapeDtypeStruct((n_m, tm, D), BF16), jax.ShapeDtypeStruct((N_DEV, D, W_IN_BLK), BF16)),
        compiler_params=pltpu.CompilerParams(dimension_semantics=("arbitrary", "arbitrary")),
    )(order, x_seq, meta_full, norm_g, w_in_b)


def _chip_rel(x, y, r):
    return (jnp.bitwise_xor(x, r >> 1), jnp.bitwise_xor(y, r & 1))


def _exchange_small(pack, srs):
    def body(pk, sr, pk_all, sr_all, send_sems, recv_sems, local_sems):
        x, y, c = _my_place()
        my_id = _dev_index(x, y, c)
        mine = [pltpu.make_async_copy(pk, pk_all.at[my_id], local_sems.at[0]),
                pltpu.make_async_copy(sr.at[my_id], sr_all.at[my_id], local_sems.at[1])]
        for cp in mine:
            cp.start()
        copies = []
        for r in range(1, N_DEV):
            peer = (jnp.bitwise_xor(x, (r >> 2) & 1), jnp.bitwise_xor(y, (r >> 1) & 1), jnp.bitwise_xor(c, r & 1))
            peer_id = _dev_index(*peer)
            for a, (src, dst) in enumerate(((pk, pk_all.at[my_id]), (sr.at[peer_id], sr_all.at[my_id]))):
                cp = pltpu.make_async_remote_copy(
                    src_ref=src, dst_ref=dst, send_sem=send_sems.at[a * 7 + r - 1], recv_sem=recv_sems.at[a * 7 + r - 1],
                    device_id=peer, device_id_type=MESH_ID)
                cp.start()
                copies.append(cp)
        for cp in copies:
            cp.wait_recv()
        for cp in copies:
            cp.wait_send()
        for cp in mine:
            cp.wait()

    return pl.pallas_call(
        body, name="exchange_small",
        out_shape=(jax.ShapeDtypeStruct((N_DEV,) + pack.shape, F32), jax.ShapeDtypeStruct(srs.shape, F32)),
        in_specs=[ANY, ANY], out_specs=(ANY, ANY),
        scratch_shapes=[pltpu.SemaphoreType.DMA((14,)), pltpu.SemaphoreType.DMA((14,)), pltpu.SemaphoreType.DMA((2,))],
    )(pack, srs)


N_CB = D // HEAD_W


def _store_by_cb(ref, idx, rows, val):
    for cb in range(N_CB):
        ref[(*idx, cb, rows, slice(None))] = val[:, cb * HEAD_W:(cb + 1) * HEAD_W]


def _fill_shifts(sh, tm):
    n = tm + HALO - 8
    for s in range(1, 8):
        for cb in range(N_CB):
            sh[s, cb, 0:n, :] = sh[0, cb, s:s + n, :]


def _conv_fwd(proj, conv_w, conv_b):
    tm = TM_ELT
    strip = tm // CONV_STRIPS

    def body(p_ref, w_ref, b_ref, c0_ref, sh):
        i = pl.program_id(0)

        @pl.when(i == 0)
        def _():
            sh[0, :, 0:HALO, :] = jnp.zeros((N_CB, HALO, HEAD_W), F32)

        @pl.when(i > 0)
        def _():
            sh[0, :, 0:HALO, :] = sh[0, :, tm:tm + HALO, :]

        ga = p_ref[:, 0:D].astype(F32)
        gb = p_ref[:, D:2 * D].astype(F32)
        _store_by_cb(sh, (0,), slice(HALO, HALO + tm), ga * _sigmoid(gb))
        _fill_shifts(sh, tm)
        for cb in range(N_CB):
            cs = slice(cb * HEAD_W, (cb + 1) * HEAD_W)
            for st in range(CONV_STRIPS):
                acc = jnp.broadcast_to(b_ref[:, cs], (strip, HEAD_W))
                for j in range(CONV_K):
                    off = HALO - (CONV_K - 1) + j
                    lo = st * strip + 8 * (off // 8)
                    acc = acc + w_ref[j:j + 1, cs] * sh[off % 8, cb, lo:lo + strip, :]
                c0_ref[st * strip:(st + 1) * strip, cs] = acc

    return pl.pallas_call(
        body, name="conv_fwd", grid=(TP // tm,),
        in_specs=[pl.BlockSpec((tm, 2 * D), lambda i: (i, 0)), pl.BlockSpec((CONV_K, D), lambda i: (0, 0)),
                  pl.BlockSpec((1, D), lambda i: (0, 0))],
        out_specs=pl.BlockSpec((tm, D), lambda i: (i, 0)),
        out_shape=jax.ShapeDtypeStruct((TP, D), F32),
        scratch_shapes=[pltpu.VMEM((8, N_CB, HALO + tm, HEAD_W), F32)],
        compiler_params=pltpu.CompilerParams(dimension_semantics=("arbitrary",)),
    )(proj, conv_w, conv_b)


def _gates(p_ref, lbl_ref, chunk, bsc):
    lb = _sigmoid(lbl_ref[0:1, :] - lbl_ref[1:2, :])
    q_raw = p_ref[:, 0:D].astype(F32)
    f_raw = p_ref[:, D:2 * D].astype(F32)
    sq = _sigmoid(q_raw)
    q = q_raw * sq
    sg = _sigmoid(f_raw)
    f = lb + (1.0 - lb) * sg
    row = lax.broadcasted_iota(jnp.int32, (CHUNK, 1), 0) + chunk * CHUNK
    valid = row >= PAD_FRONT
    lf = jnp.where(valid, jnp.log(f), 0.0)
    k = jnp.where(valid, 1.0 - f, 0.0)
    r_i = lax.broadcasted_iota(jnp.int32, (CHUNK, CHUNK), 0)
    c_i = lax.broadcasted_iota(jnp.int32, (CHUNK, CHUNK), 1)
    causal = r_i >= c_i
    bsc[...] = _tri_matmul(causal.astype(BF16), lf)
    b = bsc[...]
    b_mid = bsc[CHUNK // 2 - 1:CHUNK // 2, :]
    b_last = bsc[CHUNK - 1:CHUNK, :]
    e_q = jnp.exp(b)
    e_qm = jnp.exp(b - b_mid)
    e_km = jnp.exp(b_mid - b)
    e_kh = jnp.exp(b_last - b)
    e_last = jnp.exp(b_last)
    return dict(lb=lb, q_raw=q_raw, sq=sq, q=q, sg=sg, f=f, k=k, valid=valid, causal=causal,
                e_q=e_q, e_qm=e_qm, e_km=e_km, e_kh=e_kh, e_last=e_last)


def _rec_fwd(proj, lb_logits, w3_b):
    cps = CHUNKS_PER_STEP
    rows = cps * CHUNK

    def body(p_ref, lbl_ref, w3s_ref, o_ref, s_ref, w3o_ref, st, bsc, w3buf, send_sems, recv_sems, local_sems):
        n = pl.program_id(0)
        gather_start, gather_finish = _w3_gather(w3s_ref, w3o_ref, w3buf, send_sems, recv_sems, local_sems)

        @pl.when(n == 0)
        def _():
            st[...] = jnp.zeros_like(st)
            gather_start()

        def prep(ci):
            g = _gates(p_ref.at[pl.ds(ci * CHUNK, CHUNK)], lbl_ref, n * cps + ci, bsc.at[ci])
            g["q1"] = (g["q"] * g["e_q"]).astype(BF16)
            g["qm"] = (g["q"] * g["e_qm"]).astype(BF16)
            g["km"] = (g["k"] * g["e_km"]).astype(BF16)
            g["kh"] = (g["k"] * g["e_kh"]).astype(BF16)
            return g

        def heads(ci, g):
            rs = pl.ds(ci * CHUNK, CHUNK)
            pv = p_ref.at[rs]
            s_ref[ci] = st[...]
            for h in range(HEADS):
                sl = slice(h * HEAD_W, (h + 1) * HEAD_W)
                v = pv[:, 2 * D + h * HEAD_W:2 * D + (h + 1) * HEAD_W]
                att = jnp.where(g["causal"], _dot_nt(g["qm"][:, sl], g["km"][:, sl]), 0.0).astype(BF16)
                s_h = st[h]
                o_ref[rs, sl] = _dot_nt(g["q1"][:, sl], s_h.astype(BF16)) + _dot(att, v)
                st[h] = s_h * g["e_last"][:, sl] + _dot_tn(v, g["kh"][:, sl])

        ready = prep(0)
        for ci in range(cps):
            coming = prep(ci + 1) if ci + 1 < cps else None
            heads(ci, ready)
            ready = coming

        @pl.when(n == N_CHUNK // cps - 1)
        def _():
            gather_finish()

    return pl.pallas_call(
        body, name="rec_fwd", grid=(N_CHUNK // cps,),
        in_specs=[pl.BlockSpec((rows, 3 * D), lambda n: (n, 1)), pl.BlockSpec((2, D), lambda n: (0, 0)), ANY],
        out_specs=(pl.BlockSpec((rows, D), lambda n: (n, 0)),
                   pl.BlockSpec((cps, HEADS, HEAD_W, HEAD_W), lambda n: (n, 0, 0, 0)), ANY),
        out_shape=(jax.ShapeDtypeStruct((TP, D), F32), jax.ShapeDtypeStruct((N_CHUNK, HEADS, HEAD_W, HEAD_W), F32),
                   jax.ShapeDtypeStruct((3, D, D), BF16)),
        scratch_shapes=[pltpu.VMEM((HEADS, HEAD_W, HEAD_W), F32), pltpu.VMEM((cps, CHUNK, D), F32),
                        pltpu.VMEM((3, W_ROW_BLK, D), BF16), pltpu.SemaphoreType.DMA((7,)),
                        pltpu.SemaphoreType.DMA((7,)), pltpu.SemaphoreType.DMA((2,))],
        compiler_params=pltpu.CompilerParams(dimension_semantics=("arbitrary",)),
    )(proj, lb_logits, w3_b)


def _rec_bwd(proj, lb_logits, d_o, s_start, dproj, p3):
    cps = CHUNKS_PER_STEP
    rows = cps * CHUNK
    last = N_CHUNK // cps - 1

    def body(p_ref, lbl_ref, do_ref, s_ref, dproj_in, p3_ref, dp_ref, dlb_ref, land_ref, dst, bsc, dq_sc, dk_sc, g_sc,
             send_sems, recv_sems):
        del dproj_in
        n = pl.program_id(0)
        ride_start, ride_finish = _p3_to_sibling(p3_ref, land_ref, send_sems, recv_sems)

        @pl.when(n == 0)
        def _():
            ride_start()
            dst[...] = jnp.zeros_like(dst)
            dlb_ref[...] = jnp.zeros_like(dlb_ref)

        def prep(ci):
            g = _gates(p_ref.at[pl.ds(ci * CHUNK, CHUNK)], lbl_ref, (last - n) * cps + ci, bsc.at[ci])
            g["q1"] = (g["q"] * g["e_q"]).astype(BF16)
            qm_f = g["q"] * g["e_qm"]
            km_f = g["k"] * g["e_km"]
            g["qm"] = qm_f.astype(BF16)
            g["km"] = km_f.astype(BF16)
            g["qm_lo"] = (qm_f - g["qm"].astype(F32)).astype(BF16)
            g["km_lo"] = (km_f - g["km"].astype(F32)).astype(BF16)
            g["kh_f"] = g["k"] * g["e_kh"]
            g["kh"] = g["kh_f"].astype(BF16)
            return g

        def heads_and_post(ci, g):
            rs = pl.ds(ci * CHUNK, CHUNK)
            pv = p_ref.at[rs]
            dpv = dp_ref.at[rs]
            q1, qm, km, qm_lo, km_lo, kh_f, kh = (g[k] for k in ("q1", "qm", "km", "qm_lo", "km_lo", "kh_f", "kh"))
            for h in range(HEADS):
                sl = slice(h * HEAD_W, (h + 1) * HEAD_W)
                v = pv[:, 2 * D + h * HEAD_W:2 * D + (h + 1) * HEAD_W]
                d_oh = do_ref[rs, sl].astype(BF16)
                s0 = s_ref[ci, h]
                ds_end = dst[h]
                ds_end_b = ds_end.astype(BF16)
                att = jnp.where(g["causal"], _dot_nt(qm[:, sl], km[:, sl]), 0.0).astype(BF16)
                d_att = jnp.where(g["causal"], _dot_nt(d_oh, v), 0.0).astype(BF16)
                d_v = _dot_tn(att, d_oh) + _dot_nt(kh[:, sl], ds_end_b)
                d_qm2 = _dot(d_att, jnp.concatenate([km[:, sl], km_lo[:, sl]], axis=1))
                d_qm = d_qm2[:, 0:HEAD_W] + d_qm2[:, HEAD_W:2 * HEAD_W]
                d_q1 = _dot(d_oh, s0.astype(BF16))
                d_km2 = _dot_tn(d_att, jnp.concatenate([qm[:, sl], qm_lo[:, sl]], axis=1))
                d_km = d_km2[:, 0:HEAD_W] + d_km2[:, HEAD_W:2 * HEAD_W]
                d_kh = _dot(v, ds_end_b)
                dq_sc[ci, :, sl] = d_qm * g["e_qm"][:, sl] + d_q1 * g["e_q"][:, sl]
                dk_sc[ci, :, sl] = d_km * g["e_km"][:, sl] + d_kh * g["e_kh"][:, sl]
                g_sc[ci, :, sl] = (jnp.sum(kh_f[:, sl] * d_kh, axis=0, keepdims=True)
                                   + g["e_last"][:, sl] * jnp.sum(ds_end * s0, axis=0, keepdims=True))
                dst[h] = ds_end * g["e_last"][:, sl] + _dot_tn(d_oh, q1[:, sl])
                dpv[:, 2 * D + h * HEAD_W:2 * D + (h + 1) * HEAD_W] = d_v.astype(BF16)
            d_q = dq_sc[ci]
            d_k = dk_sc[ci]
            d_b = g["q"] * d_q - g["k"] * d_k
            anti = jnp.logical_not(g["causal"]) | (lax.broadcasted_iota(jnp.int32, (CHUNK, CHUNK), 0)
                                                    == lax.broadcasted_iota(jnp.int32, (CHUNK, CHUNK), 1))
            d_lf = _tri_matmul(anti.astype(BF16), d_b) + g_sc[ci]
            d_f = jnp.where(g["valid"], d_lf / g["f"] - d_k, 0.0)
            sg = g["sg"]
            dlb_ref[0:1, :] += jnp.sum(d_f * (1.0 - sg), axis=0, keepdims=True)
            dpv[:, 0:D] = (d_q * _dsilu(g["q_raw"], g["sq"])).astype(BF16)
            dpv[:, D:2 * D] = (d_f * (1.0 - g["lb"]) * sg * (1.0 - sg)).astype(BF16)

        ready = prep(cps - 1)
        for ci in reversed(range(cps)):
            coming = prep(ci - 1) if ci > 0 else None
            heads_and_post(ci, ready)
            ready = coming

        @pl.when(n == last)
        def _():
            ride_finish()

    return pl.pallas_call(
        body, name="rec_bwd", grid=(N_CHUNK // cps,),
        in_specs=[pl.BlockSpec((rows, 3 * D), lambda n: (last - n, 1)), pl.BlockSpec((2, D), lambda n: (0, 0)),
                  pl.BlockSpec((rows, D), lambda n: (last - n, 0)),
                  pl.BlockSpec((cps, HEADS, HEAD_W, HEAD_W), lambda n: (last - n, 0, 0, 0)), ANY, ANY],
        out_specs=(pl.BlockSpec((rows, 3 * D), lambda n: (last - n, 1)), pl.BlockSpec((8, D), lambda n: (0, 0)), ANY),
        out_shape=(jax.ShapeDtypeStruct((TP, D_IN), BF16), jax.ShapeDtypeStruct((8, D), F32),
                   jax.ShapeDtypeStruct((4, 3, W_ROW_BLK, D), F32)),
        scratch_shapes=[pltpu.VMEM((HEADS, HEAD_W, HEAD_W), F32), pltpu.VMEM((cps, CHUNK, D), F32),
                        pltpu.VMEM((cps, CHUNK, D), F32), pltpu.VMEM((cps, CHUNK, D), F32),
                        pltpu.VMEM((cps, 1, D), F32), pltpu.SemaphoreType.DMA((4,)), pltpu.SemaphoreType.DMA((4,))],
        input_output_aliases={4: 0},
        compiler_params=pltpu.CompilerParams(dimension_semantics=("arbitrary",)),
    )(proj, lb_logits, d_o, s_start, dproj, p3)


def _mid(xin, tgt, o, c0, proj, w3, ln_g, ln_b, gnorm_g, final_g):
    tm = TM_ELT

    def body(x_ref, t_ref, o_ref, c0_ref, z_ref, gr_ref, mc_ref, mr_ref, w_ref, lng_ref, lnb_ref, gng_ref, fg_ref,
             dres_ref, do_ref, dc0_ref, dz_ref, dp_ref, a3_ref, b3_ref, red_ref, on_sc, don_sc):
        i = pl.program_id(0)

        @pl.when(i == 0)
        def _():
            red_ref[...] = jnp.zeros_like(red_ref)

        w_conv, w_rec, w_out = w_ref[0], w_ref[1], w_ref[2]
        c0v = c0_ref[...]
        mu = jnp.mean(c0v, axis=-1, keepdims=True)
        xc = c0v - mu
        rstd = lax.rsqrt(jnp.mean(xc * xc, axis=-1, keepdims=True) + EPS)
        xh = xc * rstd
        c1 = xh * lng_ref[...] + lnb_ref[...]
        s1 = _sigmoid(c1)
        c2 = c1 * s1
        z = z_ref[...].astype(F32)
        sz = _sigmoid(z)
        silu_z = z * sz
        u_conv = (c2 * silu_z).astype(BF16)
        y_conv = _dot(u_conv, w_conv)
        ov = o_ref[...]
        r3 = []
        for h in range(HEADS):
            sl = slice(h * HEAD_W, (h + 1) * HEAD_W)
            oh = ov[:, sl]
            r_h = lax.rsqrt(jnp.mean(oh * oh, axis=-1, keepdims=True) + EPS)
            r3.append(r_h)
            on_sc[:, sl] = oh * r_h
        o_n = on_sc[...]
        o_g = o_n * gng_ref[...]
        gr = gr_ref[...].astype(F32)
        sgr = _sigmoid(gr)
        silu_g = gr * sgr
        u_rec = (o_g * silu_g).astype(BF16)
        y_rec = _dot(u_rec, w_rec)
        mc = mc_ref[...].astype(F32)
        mr = mr_ref[...].astype(F32)
        smc = _sigmoid(mc)
        smr = _sigmoid(mr)
        merged = (smc * y_conv + smr * y_rec).astype(BF16)
        res = x_ref[...] + _dot(merged, w_out)
        r2 = lax.rsqrt(jnp.mean(res * res, axis=-1, keepdims=True) + EPS)
        xh2 = res * r2
        row = lax.broadcasted_iota(jnp.int32, (tm, 1), 0) + i * tm
        real = row >= ROW0
        tgt = t_ref[...]
        tgt = jnp.where(i == 0, pltpu.roll(tgt, ROW0, 0), tgt)
        diff = jnp.where(real, xh2 * fg_ref[...] - tgt, 0.0)
        d_y = diff * (1.0 / D)
        d_xh2 = d_y * fg_ref[...]
        d_res = r2 * (d_xh2 - xh2 * jnp.mean(d_xh2 * xh2, axis=-1, keepdims=True))
        dres_ref[...] = d_res
        d_res_b = d_res.astype(BF16)
        d_merged = _dot_nt(d_res_b, w_out)
        d_yc = (d_merged * smc).astype(BF16)
        d_yr = (d_merged * smr).astype(BF16)
        dp_ref[:, D:2 * D] = (d_merged * y_conv * smc * (1.0 - smc)).astype(BF16)
        dp_ref[:, 2 * D:3 * D] = (d_merged * y_rec * smr * (1.0 - smr)).astype(BF16)
        d_ur = _dot_nt(d_yr, w_rec)
        d_og = d_ur * silu_g
        dp_ref[:, 0:D] = (d_ur * o_g * _dsilu(gr, sgr)).astype(BF16)
        d_on = d_og * gng_ref[...]
        for h in range(HEADS):
            sl = slice(h * HEAD_W, (h + 1) * HEAD_W)
            d_h = d_on[:, sl]
            n_h = o_n[:, sl]
            don_sc[:, sl] = r3[h] * (d_h - n_h * jnp.mean(d_h * n_h, axis=-1, keepdims=True))
        do_ref[...] = don_sc[...]
        d_uc = _dot_nt(d_yc, w_conv)
        d_c2 = d_uc * silu_z
        dz_ref[...] = (d_uc * c2 * _dsilu(z, sz)).astype(BF16)
        d_c1 = d_c2 * _dsilu(c1, s1)
        d_xh = d_c1 * lng_ref[...]
        d_c0 = rstd * (d_xh - jnp.mean(d_xh, axis=-1, keepdims=True)
                       - xh * jnp.mean(d_xh * xh, axis=-1, keepdims=True))
        dc0_ref[...] = d_c0
        a3_ref[0] = u_conv
        b3_ref[0] = d_yc
        a3_ref[1] = u_rec
        b3_ref[1] = d_yr
        a3_ref[2] = merged
        b3_ref[2] = d_res_b
        def colsum(vv):
            return jnp.sum(vv, axis=0, keepdims=True)

        red_ref[0:1, :] += colsum(d_y * xh2)
        red_ref[1:2, :] += colsum(d_og * o_n)
        red_ref[2:3, :] += colsum(d_c1 * xh)
        red_ref[3:4, :] += colsum(d_c1)
        red_ref[4:5, :] += colsum(d_c0)
        red_ref[5:6, :] += colsum(diff * diff) * (0.5 / D)

    def row_block(width, col):
        return pl.BlockSpec((tm, width), lambda i: (i, col))

    def const_block(shape):
        return pl.BlockSpec(shape, lambda i: (0,) * len(shape))

    stack = jax.ShapeDtypeStruct((3, TP, D), BF16)
    stack_spec = pl.BlockSpec((3, tm, D), lambda i: (0, i, 0))
    return pl.pallas_call(
        body, name="mid", grid=(TP // tm,),
        in_specs=[row_block(D, 0),
                  pl.BlockSpec((pl.Element(tm), pl.Element(D)), lambda i: (_window_start(i, tm), 0)),
                  row_block(D, 0), row_block(D, 0),
                  row_block(D, 2), row_block(D, 6), row_block(D, 7), row_block(D, 8),
                  pl.BlockSpec((3, D, D), lambda i: (0, 0, 0), pipeline_mode=pl.Buffered(1)),
                  const_block((1, D)), const_block((1, D)), const_block((1, D)), const_block((1, D))],
        out_specs=(row_block(D, 0), row_block(D, 0), row_block(D, 0), row_block(D, 0), row_block(3 * D, 2),
                   stack_spec, stack_spec, const_block((8, D))),
        out_shape=(jax.ShapeDtypeStruct((TP, D), F32), jax.ShapeDtypeStruct((TP, D), F32),
                   jax.ShapeDtypeStruct((TP, D), F32), jax.ShapeDtypeStruct((TP, D), BF16),
                   jax.ShapeDtypeStruct((TP, D_IN), BF16), stack, stack, jax.ShapeDtypeStruct((8, D), F32)),
        scratch_shapes=[pltpu.VMEM((tm, D), F32), pltpu.VMEM((tm, D), F32)],
        compiler_params=pltpu.CompilerParams(dimension_semantics=("arbitrary",), vmem_limit_bytes=60 * 1024 * 1024),
    )(xin, tgt, o, c0, proj, proj, proj, proj, w3, ln_g, ln_b, gnorm_g, final_g)


def _conv_bwd(proj, d_c0, d_z, conv_w, dproj, chip1b):
    tm = TM_ELT
    n_tile = TP // tm
    lastt = n_tile - 1

    strip = tm // CONV_STRIPS

    def body(p_ref, dc_ref, dz_ref, w_ref, dproj_in, c1_ref, dp_ref, dw_ref, far_ref, dsh, a_sc, da_sc, acc,
             send_sems, recv_sems):
        del dproj_in
        i = pl.program_id(0)
        ride_start, ride_finish = _partials_to_owners(c1_ref, far_ref, send_sems, recv_sems)

        @pl.when(i == 0)
        def _():
            ride_start()
            dsh[0, :, tm:tm + HALO, :] = jnp.zeros((N_CB, HALO, HEAD_W), F32)
            acc[...] = jnp.zeros_like(acc)

        @pl.when(i > 0)
        def _():
            dsh[0, :, tm:tm + HALO, :] = dsh[0, :, 0:HALO, :]

        _store_by_cb(dsh, (0,), slice(0, tm), dc_ref[...])
        _fill_shifts(dsh, tm)
        ga = p_ref[:, 0:D].astype(F32)
        sb = _sigmoid(p_ref[:, D:2 * D].astype(F32))
        a = ga * sb
        _store_by_cb(a_sc, (), slice(0, tm), a)
        for cb in range(N_CB):
            cs = slice(cb * HEAD_W, (cb + 1) * HEAD_W)
            for st in range(CONV_STRIPS):
                rows = slice(st * strip, (st + 1) * strip)
                a_s = a_sc[cb, rows, :]
                d_a = jnp.zeros((strip, HEAD_W), F32)
                for j in range(CONV_K):
                    off = CONV_K - 1 - j
                    lo = st * strip + 8 * (off // 8)
                    slab = dsh[off % 8, cb, lo:lo + strip, :]
                    d_a = d_a + w_ref[j:j + 1, cs] * slab
                    acc[j, :, cs] += jnp.sum((a_s * slab).reshape(strip // 8, 8, HEAD_W), axis=0)
                da_sc[rows, cs] = d_a
        d_a = da_sc[...]
        dp_ref[:, 0:D] = (d_a * sb).astype(BF16)
        dp_ref[:, D:2 * D] = (d_a * a * (1.0 - sb)).astype(BF16)
        dp_ref[:, 2 * D:3 * D] = dz_ref[...]

        @pl.when(i == lastt)
        def _():
            for j in range(CONV_K):
                dw_ref[j:j + 1, :] = jnp.sum(acc[j], axis=0, keepdims=True)
            dw_ref[CONV_K:CONV_K + 1, :] = jnp.zeros((1, D), F32)
            ride_finish()

    return pl.pallas_call(
        body, name="conv_bwd", grid=(n_tile,),
        in_specs=[pl.BlockSpec((tm, 2 * D), lambda i: (lastt - i, 0)), pl.BlockSpec((tm, D), lambda i: (lastt - i, 0)),
                  pl.BlockSpec((tm, D), lambda i: (lastt - i, 0)), pl.BlockSpec((CONV_K, D), lambda i: (0, 0)), ANY, ANY],
        out_specs=(pl.BlockSpec((tm, 3 * D), lambda i: (lastt - i, 0)), pl.BlockSpec((CONV_K + 1, D), lambda i: (0, 0)),
                   ANY),
        out_shape=(jax.ShapeDtypeStruct((TP, D_IN), BF16), jax.ShapeDtypeStruct((CONV_K + 1, D), F32),
                   jax.ShapeDtypeStruct((3, 3, W_ROW_BLK, D), BF16)),
        scratch_shapes=[pltpu.VMEM((8, N_CB, tm + HALO, HEAD_W), F32), pltpu.VMEM((N_CB, tm, HEAD_W), F32),
                        pltpu.VMEM((tm, D), F32), pltpu.VMEM((CONV_K, 8, D), F32),
                        pltpu.SemaphoreType.DMA((3,)), pltpu.SemaphoreType.DMA((3,))],
        input_output_aliases={4: 0},
        compiler_params=pltpu.CompilerParams(dimension_semantics=("arbitrary",)),
    )(proj, d_c0, d_z, conv_w, dproj, chip1b)


def _wgrad3(a3, b3):
    tt = TT_WGRAD

    def body(a_ref, b_ref, o_ref):
        @pl.when(pl.program_id(1) == 0)
        def _():
            o_ref[...] = jnp.zeros_like(o_ref)

        o_ref[0] += _dot_tn(a_ref[0], b_ref[0])

    return pl.pallas_call(
        body, name="wgrad3", grid=(3, TP // tt),
        in_specs=[pl.BlockSpec((1, tt, D), lambda g, t: (g, t, 0)), pl.BlockSpec((1, tt, D), lambda g, t: (g, t, 0))],
        out_specs=pl.BlockSpec((1, D, D), lambda g, t: (g, 0, 0)),
        out_shape=jax.ShapeDtypeStruct((3, D, D), F32),
        compiler_params=pltpu.CompilerParams(dimension_semantics=("arbitrary", "arbitrary")),
    )(a3, b3)


def _wgrad_in(h, dproj, ids):
    tt = TT_WGRAD
    n_t = TP // tt

    def body(ids_ref, a_ref, b_ref, o_ref, ob_ref, l0_ref, acc, tmp, send_sems, recv_sems, tmp_sem):
        del ids_ref
        r = pl.program_id(0)
        t = pl.program_id(1)
        x, y, c = _my_place()
        sibling = (x, y, 1 - c)
        slot = lax.rem(r, 2)

        def send_in(q):
            return pltpu.make_async_remote_copy(
                src_ref=acc.at[q % 2], dst_ref=l0_ref.at[q], send_sem=send_sems.at[q], recv_sem=recv_sems.at[q],
                device_id=sibling, device_id_type=MESH_ID)

        def landed(q):
            return pltpu.make_async_copy(l0_ref.at[q], tmp, tmp_sem)

        @pl.when(t == 0)
        def _():
            acc[slot] = jnp.zeros((D, W_IN_BLK), F32)

        acc[slot] += _dot_tn(a_ref[...], b_ref[...])

        for q in range(4):
            @pl.when((r == q) & (t == n_t - 1))
            def _(q=q):
                if q >= 1:
                    send_in(q - 1).wait_send()
                send_in(q).start()

            @pl.when((r == 4 + q) & (t == n_t - 2))
            def _(q=q):
                if q == 0:
                    send_in(3).wait_send()
                send_in(q).wait_recv()
                landed(q).start()

            @pl.when((r == 4 + q) & (t == n_t - 1))
            def _(q=q):
                landed(q).wait()
                tot = acc[q % 2] + tmp[...]
                o_ref[0] = tot
                ob_ref[0] = tot.astype(BF16)

    blk = pl.BlockSpec((1, D, W_IN_BLK), lambda r, t, ids: (jnp.maximum(r - 4, 0), 0, 0))
    return pl.pallas_call(
        body, name="wgrad_in",
        grid_spec=pltpu.PrefetchScalarGridSpec(
            num_scalar_prefetch=1, grid=(N_DEV, n_t),
            in_specs=[pl.BlockSpec((tt, D), lambda r, t, ids: (t, 0)),
                      pl.BlockSpec((tt, W_IN_BLK), lambda r, t, ids: (t, ids[r]))],
            out_specs=(blk, blk, ANY),
            scratch_shapes=[pltpu.VMEM((2, D, W_IN_BLK), F32), pltpu.VMEM((D, W_IN_BLK), F32),
                            pltpu.SemaphoreType.DMA((4,)), pltpu.SemaphoreType.DMA((4,)), pltpu.SemaphoreType.DMA]),
        out_shape=(jax.ShapeDtypeStruct((4, D, W_IN_BLK), F32), jax.ShapeDtypeStruct((4, D, W_IN_BLK), BF16),
                   jax.ShapeDtypeStruct((4, D, W_IN_BLK), F32)),
        compiler_params=pltpu.CompilerParams(dimension_semantics=("arbitrary", "arbitrary")),
    )(ids, h, dproj)


def _chip_sum_3(p3, land1, ids_mine):
    def body(ids_ref, p_ref, l_ref, o_ref, ob_ref):
        del ids_ref
        tot = p_ref[...] + l_ref[0]
        o_ref[0] = tot
        ob_ref[0] = tot.astype(BF16)

    blk = pl.BlockSpec((1, 3, W_ROW_BLK, D), lambda r, ids: (r, 0, 0, 0))
    return pl.pallas_call(
        body, name="chip_sum_3",
        grid_spec=pltpu.PrefetchScalarGridSpec(
            num_scalar_prefetch=1, grid=(4,),
            in_specs=[pl.BlockSpec((3, W_ROW_BLK, D), lambda r, ids: (0, ids[r], 0)), blk],
            out_specs=(blk, blk)),
        out_shape=(jax.ShapeDtypeStruct((4, 3, W_ROW_BLK, D), F32), jax.ShapeDtypeStruct((4, 3, W_ROW_BLK, D), BF16)),
    )(ids_mine, p3, land1)


def _dh_and_norm_bwd(dproj, w_in_full, xin, d_res, norm_g, chip0b):
    tm = TM_MAT
    n_k = N_DEV // DH_K_BLKS
    n_m = TP // tm

    def body(dp_ref, w_ref, x_ref, dr_ref, g_ref, c0_ref, dx_ref, dg_ref, f0_ref, acc, send_sems, recv_sems):
        m = pl.program_id(0)
        k = pl.program_id(1)
        ride_start, ride_finish = _partials_to_owners(c0_ref, f0_ref, send_sems, recv_sems)

        @pl.when((m == 0) & (k == 0))
        def _():
            ride_start()

        @pl.when(k == 0)
        def _():
            acc[...] = jnp.zeros_like(acc)

        part = _dot_nt(dp_ref[:, 0:W_IN_BLK], w_ref[0])
        for j in range(1, DH_K_BLKS):
            part = part + _dot_nt(dp_ref[:, j * W_IN_BLK:(j + 1) * W_IN_BLK], w_ref[j])
        acc[...] += part

        @pl.when((k == n_k - 1) & (m == 0))
        def _():
            dg_ref[...] = jnp.zeros_like(dg_ref)

        @pl.when(k == n_k - 1)
        def _():
            xv = x_ref[...]
            r1 = lax.rsqrt(jnp.mean(xv * xv, axis=-1, keepdims=True) + EPS)
            xh = xv * r1
            d_h = acc[...]
            dg_ref[0:1, :] += jnp.sum(d_h * xh, axis=0, keepdims=True)
            d_xh = d_h * g_ref[...]
            dx_ref[...] = dr_ref[...] + r1 * (d_xh - xh * jnp.mean(d_xh * xh, axis=-1, keepdims=True))

        @pl.when((m == n_m - 1) & (k == n_k - 1))
        def _():
            ride_finish()

    return pl.pallas_call(
        body, name="dh_norm_bwd", grid=(n_m, n_k),
        in_specs=[pl.BlockSpec((tm, DH_K_BLKS * W_IN_BLK), lambda m, k: (m, k)),
                  pl.BlockSpec((DH_K_BLKS, D, W_IN_BLK), lambda m, k: (k, 0, 0)),
                  pl.BlockSpec((tm, D), lambda m, k: (m, 0)), pl.BlockSpec((tm, D), lambda m, k: (m, 0)),
                  pl.BlockSpec((1, D), lambda m, k: (0, 0)), ANY],
        out_specs=(pl.BlockSpec((tm, D), lambda m, k: (m, 0)), pl.BlockSpec((8, D), lambda m, k: (0, 0)), ANY),
        out_shape=(jax.ShapeDtypeStruct((TP, D), F32), jax.ShapeDtypeStruct((8, D), F32),
                   jax.ShapeDtypeStruct((3, D, W_IN_BLK), BF16)),
        scratch_shapes=[pltpu.VMEM((tm, D), F32), pltpu.SemaphoreType.DMA((3,)), pltpu.SemaphoreType.DMA((3,))],
        compiler_params=pltpu.CompilerParams(dimension_semantics=("arbitrary", "arbitrary")),
    )(dproj, w_in_full, xin, d_res, norm_g, chip0b)


def _sum_adamw(own, landed, w, m, v, tr, name):
    rows, cols = w.shape
    n_t = rows // tr

    def body(o_ref, l1_ref, l2_ref, l3_ref, w_ref, m_ref, v_ref, g_ref, d_ref, m2_ref, v2_ref):
        g = ((o_ref[...] + l1_ref[...].astype(F32)) + l2_ref[...].astype(F32)) + l3_ref[...].astype(F32)
        delta, m2, v2 = _adamw(w_ref[...], g, m_ref[...], v_ref[...])
        g_ref[...] = g
        d_ref[...] = delta
        m2_ref[...] = m2
        v2_ref[...] = v2

    def spec(k):
        return pl.BlockSpec((tr, cols), lambda i: (i + k * n_t, 0))

    out = jax.ShapeDtypeStruct((rows, cols), F32)
    return pl.pallas_call(
        body, name=name, grid=(n_t,),
        in_specs=[spec(0), spec(0), spec(1), spec(2), spec(0), spec(0), spec(0)],
        out_specs=(spec(0),) * 4, out_shape=(out,) * 4,
    )(own, landed, landed, landed, w, m, v)


def _small_update(pack_all, srs_all, lb_logits, p8, m8, v8, ws, ms, vs):
    def body(pk_ref, sr_ref, lbl_ref, p_ref, m_ref, v_ref, ws_ref, ms_ref, vs_ref,
             g8_ref, d8_ref, m8_ref, v8_ref, loss_ref, gs_ref, ds_ref, ms2_ref, vs2_ref):
        tot = pk_ref[0]
        tot_s = sr_ref[0]
        for d in range(1, N_DEV):
            tot = tot + pk_ref[d]
            tot_s = tot_s + sr_ref[d]
        p0 = _sigmoid(lbl_ref[0:1, :] - lbl_ref[1:2, :])
        row = lax.broadcasted_iota(jnp.int32, (8, D), 0)
        d_lb = jnp.sum(jnp.where(row == 4, tot, 0.0), axis=0, keepdims=True)
        d_l0 = d_lb * p0 * (1.0 - p0)
        loss_ref[...] = jnp.sum(jnp.where(row == 5, tot, 0.0), keepdims=True).reshape(1, 1)
        g8 = jnp.where(row == 4, d_l0, jnp.where(row == 5, -d_l0, tot))
        delta, m2, v2 = _adamw(p_ref[...], g8, m_ref[...], v_ref[...])
        g8_ref[...] = g8
        d8_ref[...] = delta
        m8_ref[...] = m2
        v8_ref[...] = v2
        delta, m2, v2 = _adamw(ws_ref[...], tot_s, ms_ref[...], vs_ref[...])
        gs_ref[...] = tot_s
        ds_ref[...] = delta
        ms2_ref[...] = m2
        vs2_ref[...] = v2

    o8 = jax.ShapeDtypeStruct((8, D), F32)
    os_ = jax.ShapeDtypeStruct((SMALL_ROWS, HEAD_W), F32)
    return pl.pallas_call(
        body, name="small_update",
        out_shape=(o8, o8, o8, o8, jax.ShapeDtypeStruct((1, 1), F32), os_, os_, os_, os_),
    )(pack_all, srs_all, lb_logits, p8, m8, v8, ws, ms, vs)


def _local_step(xin, proj, target, conv_w_full, conv_b, ln_g, ln_b, w3_b, lb_logits, gnorm_g, final_g, ids_mine):
    fg = final_g.reshape(1, D)
    c0 = _conv_fwd(proj, conv_w_full, conv_b)
    o, s_start, w3_full = _rec_fwd(proj, lb_logits, w3_b)
    d_res, d_o, d_c0, d_z, dproj, a3, b3, red = _mid(xin, target, o, c0, proj, w3_full, ln_g, ln_b, gnorm_g, fg)
    p3 = _wgrad3(a3, b3)
    dproj, dlb, land1 = _rec_bwd(proj, lb_logits, d_o, s_start, dproj, p3)
    chip1, chip1b = _chip_sum_3(p3, land1, ids_mine)
    dproj, d_conv_w, far1 = _conv_bwd(proj, d_c0, d_z, conv_w_full, dproj, chip1b)
    return dproj, d_res, p3, chip1, far1, d_conv_w, red, dlb


def kernel(x, meta_tokens, norm_g, w_in, conv_w, conv_b, ln_g, ln_b, w_conv_out, lb_logits, gnorm_g, w_rec_out, w_out, final_g, loss_target, m_meta_tokens, m_norm_g, m_w_in, m_conv_w, m_conv_b, m_ln_g, m_ln_b, m_w_conv_out, m_lb_logits, m_gnorm_g, m_w_rec_out, m_w_out, m_final_g, v_meta_tokens, v_norm_g, v_w_in, v_conv_w, v_conv_b, v_ln_g, v_ln_b, v_w_conv_out, v_lb_logits, v_gnorm_g, v_w_rec_out, v_w_out, v_final_g):
    def small_pack(cw, mt):
        return jnp.concatenate([cw[0], jnp.zeros((1, HEAD_W), F32), mt], axis=0)

    def stack3(a, b, c):
        return jnp.concatenate([a, b, c], axis=0)

    def stack8(ng, cb, lg, lb_, lbl, gg, fg):
        return jnp.concatenate([ng, cb, lg, lb_, lbl, gg, fg.reshape(1, D)], axis=0)

    mx, my, mc = _my_place()

    w3_s = stack3(w_conv_out, w_rec_out, w_out)
    ws_s = small_pack(conv_w, meta_tokens)
    small_full = jnp.transpose(_gather_small(ws_s), (1, 0, 2)).reshape(SMALL_ROWS, D)
    conv_w_full = small_full[0:CONV_K]
    meta_full = small_full[META_ROW:META_ROW + N_META]
    w_in_b, w3_b = _cast_shards(w_in[0], w3_s)
    use_order = [(mx, my, mc), (mx, my, 1 - mc)]
    for chip in ((1 - mx, my), (mx, 1 - my), (1 - mx, 1 - my)):
        use_order += [(*chip, mc), (*chip, 1 - mc)]
    order = jnp.stack([_dev_index(*p) for p in use_order]).astype(jnp.int32)
    proj, xin, h, w_in_full = _gather_and_proj(x[0], meta_full, norm_g, w_in_b, order)
    h = h.reshape(TP, D)

    ids_mine = jnp.stack([_dev_index(*_chip_rel(mx, my, r), mc) for r in range(4)]).astype(jnp.int32)
    ids_sib = jnp.stack([_dev_index(*_chip_rel(mx, my, r), 1 - mc) for r in range(4)]).astype(jnp.int32)
    dproj, d_res, _, chip1, far1, d_conv_w, red, dlb = _local_step(
        xin, proj, loss_target[0], conv_w_full, conv_b, ln_g, ln_b, w3_b, lb_logits, gnorm_g, final_g, ids_mine)

    chip0, chip0b, _ = _wgrad_in(h, dproj, jnp.concatenate([ids_sib, ids_mine]))
    d_xin, dng, far0 = _dh_and_norm_bwd(dproj, w_in_full, xin, d_res, norm_g, chip0b)
    pack = jnp.concatenate([dng[0:1], red[4:5], red[2:3], red[3:4], dlb[0:1], red[5:6], red[1:2], red[0:1]], axis=0)
    g_in, d_in, m_in, v_in = _sum_adamw(chip0.reshape(4 * D, W_IN_BLK), far0.reshape(3 * D, W_IN_BLK), w_in[0],
                                        m_w_in[0], v_w_in[0], 256, "adamw_in")
    g_3, d_3, m_3, v_3 = _sum_adamw(
        chip1.reshape(12 * W_ROW_BLK, D), far1.reshape(9 * W_ROW_BLK, D), w3_s.reshape(3 * W_ROW_BLK, D),
        stack3(m_w_conv_out, m_w_rec_out, m_w_out).reshape(3 * W_ROW_BLK, D),
        stack3(v_w_conv_out, v_w_rec_out, v_w_out).reshape(3 * W_ROW_BLK, D), 3 * W_ROW_BLK, "adamw_3")

    srs = jnp.concatenate([d_conv_w, d_xin[PAD_FRONT:ROW0]], axis=0)
    srs = jnp.transpose(srs.reshape(SMALL_ROWS, N_DEV, HEAD_W), (1, 0, 2))
    pack_all, srs_all = _exchange_small(pack, srs)
    g8, d8, m8, v8, loss, gs, ds, ms, vs = _small_update(
        pack_all, srs_all, lb_logits,
        stack8(norm_g, conv_b, ln_g, ln_b, lb_logits, gnorm_g, final_g),
        stack8(m_norm_g, m_conv_b, m_ln_g, m_ln_b, m_lb_logits, m_gnorm_g, m_final_g),
        stack8(v_norm_g, v_conv_b, v_ln_g, v_ln_b, v_lb_logits, v_gnorm_g, v_final_g),
        ws_s, small_pack(m_conv_w, m_meta_tokens), small_pack(v_conv_w, v_meta_tokens))

    def unpack(a_in, a_3, a_s, a_8):
        t3 = a_3.reshape(3, 1, W_ROW_BLK, D)
        return (a_s[META_ROW:META_ROW + N_META], a_8[0:1], a_in[None], a_s[0:CONV_K][None], a_8[1:2], a_8[2:3],
                a_8[3:4], t3[0], a_8[4:6], a_8[6:7], t3[1], t3[2], a_8[7])

    grad_x = d_xin[ROW0:][None]
    return (loss.reshape(()), grad_x, *unpack(g_in, g_3, gs, g8), *unpack(d_in, d_3, ds, d8),
            *unpack(m_in, m_3, ms, m8), *unpack(v_in, v_3, vs, v8))
```

```python
import functools

import jax
import jax.numpy as jnp
from jax import lax
from jax.experimental import pallas as pl
from jax.experimental.pallas import tpu as pltpu

F32 = jnp.float32
BF16 = jnp.bfloat16
ACT = BF16

D = 1024
SEQ = 4096
N_META = 16
CHUNK = 64
PAD_FRONT = 48
ROW0 = PAD_FRONT + N_META
TP = ROW0 + SEQ
N_CHUNK = TP // CHUNK
HEADS = 8
HEAD_W = 128
D_IN = 9 * D
N_DEV = 8
W_IN_BLK = D_IN // N_DEV
W_ROW_BLK = D // N_DEV
CONV_K = 31
SMALL_ROWS = 48
META_ROW = 32
EPS = 1e-6
HALO = 32

TM_MAT = 832
TT_WGRAD = 2080
DH_K_BLKS = 2
TM_ELT = 208
CHUNKS_PER_STEP = 5
CONV_STRIPS = 2

ADAM_LR = 0.001
ADAM_B1 = 0.9
ADAM_B2 = 0.999
ADAM_EPS = 1e-08
ADAM_WD = 0.01
ADAM_STEP = 10

MESH_ID = pl.DeviceIdType.MESH
ANY = pl.BlockSpec(memory_space=pl.ANY)


def _sigmoid(v):
    return jax.nn.sigmoid(v)


def _dsilu(v, s):
    return s * (1.0 + v * (1.0 - s))


def _dot(a, b):
    return jnp.dot(a, b, preferred_element_type=F32)


def _dot_nt(a, b):
    return lax.dot_general(a, b, (((1,), (1,)), ((), ())), preferred_element_type=F32)


def _dot_tn(a, b):
    return lax.dot_general(a, b, (((0,), (0,)), ((), ())), preferred_element_type=F32)


def _split3(v):
    hi = v.astype(BF16)
    r1 = v - hi.astype(F32)
    mid = r1.astype(BF16)
    lo = (r1 - mid.astype(F32)).astype(BF16)
    return hi, mid, lo


def _tri_matmul(tri, v):
    hi, mid, lo = _split3(v)
    return _dot(tri, hi) + _dot(tri, mid) + _dot(tri, lo)


def _adamw(w, g, m, v):
    m2 = ADAM_B1 * m + (1.0 - ADAM_B1) * g
    v2 = ADAM_B2 * v + (1.0 - ADAM_B2) * jnp.square(g)
    m_hat = m2 / (1.0 - ADAM_B1 ** ADAM_STEP)
    v_hat = v2 / (1.0 - ADAM_B2 ** ADAM_STEP)
    delta = -ADAM_LR * (m_hat / (jnp.sqrt(v_hat) + ADAM_EPS) + ADAM_WD * w)
    return delta, m2, v2


def _window_start(i, tm):
    assert tm % 16 == 0 and ROW0 % 16 == 0
    return pl.multiple_of(16 * jnp.maximum((tm // 16) * i - ROW0 // 16, 0), 16)


def _my_place():
    return lax.axis_index("x"), lax.axis_index("y"), lax.axis_index("c")


def _dev_index(px, py, pc):
    return 4 * px + 2 * py + pc


def _cast_shards(w_in_s, w3_s):
    def body(a_ref, b_ref, oa_ref, ob_ref):
        oa_ref[...] = a_ref[...].astype(BF16)
        ob_ref[...] = b_ref[...].astype(BF16)

    return pl.pallas_call(
        body, name="cast_shards",
        out_shape=(jax.ShapeDtypeStruct(w_in_s.shape, BF16), jax.ShapeDtypeStruct(w3_s.shape, BF16)),
    )(w_in_s, w3_s)


def _peer(x, y, c, r):
    return (jnp.bitwise_xor(x, (r >> 2) & 1), jnp.bitwise_xor(y, (r >> 1) & 1), jnp.bitwise_xor(c, r & 1))


def _gather_small(small_s):
    def body(s_ref, o_ref, send_sems, recv_sems, local_sem):
        x, y, c = _my_place()
        my_id = _dev_index(x, y, c)
        mine = pltpu.make_async_copy(s_ref, o_ref.at[my_id], local_sem)
        mine.start()
        copies = []
        for r in range(1, N_DEV):
            cp = pltpu.make_async_remote_copy(
                src_ref=s_ref, dst_ref=o_ref.at[my_id], send_sem=send_sems.at[r - 1], recv_sem=recv_sems.at[r - 1],
                device_id=_peer(x, y, c, r), device_id_type=MESH_ID)
            cp.start()
            copies.append(cp)
        for cp in copies:
            cp.wait_recv()
        for cp in copies:
            cp.wait_send()
        mine.wait()

    return pl.pallas_call(
        body, name="gather_small", out_shape=jax.ShapeDtypeStruct((N_DEV,) + small_s.shape, F32),
        in_specs=[ANY], out_specs=ANY,
        scratch_shapes=[pltpu.SemaphoreType.DMA((7,)), pltpu.SemaphoreType.DMA((7,)), pltpu.SemaphoreType.DMA],
    )(small_s)


def _w3_gather(src, out, stage, send_sems, recv_sems, local_sems):
    x, y, c = _my_place()
    me, sibling = (x, y, c), (x, y, 1 - c)
    chips = [(1 - x, y), (x, 1 - y), (1 - x, 1 - y)]

    def block(place):
        d = _dev_index(*place)
        return out.at[:, pl.ds(pl.multiple_of(d * W_ROW_BLK, W_ROW_BLK), W_ROW_BLK), :]

    def copy(k, place, to, from_src=False):
        return pltpu.make_async_remote_copy(
            src_ref=src if from_src else block(place), dst_ref=block(place),
            send_sem=send_sems.at[k], recv_sem=recv_sems.at[k], device_id=to, device_id_type=MESH_ID)

    own_in = pltpu.make_async_copy(src, stage, local_sems.at[0])
    own_out = pltpu.make_async_copy(stage, block(me), local_sems.at[1])

    def start():
        copy(0, me, sibling, from_src=True).start()
        for j, chip in enumerate(chips):
            copy(1 + j, me, (*chip, c), from_src=True).start()
        own_in.start()
        own_in.wait()
        own_out.start()

    def finish():
        for j, chip in enumerate(chips):
            copy(1 + j, (*chip, c), me).wait_recv()
            copy(4 + j, (*chip, c), sibling).start()
        copy(0, sibling, me).wait_recv()
        for j, chip in enumerate(chips):
            copy(4 + j, (*chip, 1 - c), me).wait_recv()
        for k in range(7):
            copy(k, me, me).wait_send()
        own_out.wait()

    return start, finish


def _p3_to_sibling(p3_ref, land_ref, send_sems, recv_sems):
    x, y, c = _my_place()

    def cp(q):
        d = _dev_index(*_chip_rel(x, y, q), 1 - c)
        return pltpu.make_async_remote_copy(
            src_ref=p3_ref.at[:, pl.ds(pl.multiple_of(d * W_ROW_BLK, W_ROW_BLK), W_ROW_BLK), :],
            dst_ref=land_ref.at[q], send_sem=send_sems.at[q], recv_sem=recv_sems.at[q],
            device_id=(x, y, 1 - c), device_id_type=MESH_ID)

    def start():
        for q in range(4):
            cp(q).start()

    def finish():
        for q in range(4):
            cp(q).wait_recv()
        for q in range(4):
            cp(q).wait_send()

    return start, finish


def _partials_to_owners(src_ref, far_ref, send_sems, recv_sems):
    x, y, c = _my_place()

    def cp(q):
        return pltpu.make_async_remote_copy(
            src_ref=src_ref.at[q], dst_ref=far_ref.at[q - 1], send_sem=send_sems.at[q - 1],
            recv_sem=recv_sems.at[q - 1], device_id=(*_chip_rel(x, y, q), c), device_id_type=MESH_ID)

    def start():
        for q in range(1, 4):
            cp(q).start()

    def finish():
        for q in range(1, 4):
            cp(q).wait_recv()
        for q in range(1, 4):
            cp(q).wait_send()

    return start, finish


def _gather_and_proj(x_seq, meta_full, norm_g, w_in_b, order):
    tm = TM_MAT
    n_m = TP // tm
    last_m = n_m - 1

    def body(order_ref, x_ref, meta_ref, g_ref, s0, proj_ref, xin_ref, h_out, o0, hbuf, wbuf, send_sems, recv_sems,
             local_sems):
        del order_ref
        n = pl.program_id(0)
        m = pl.program_id(1)
        x, y, c = _my_place()
        me, sibling = (x, y, c), (x, y, 1 - c)
        chips = [(1 - x, y), (x, 1 - y), (1 - x, 1 - y)]

        def block(place):
            return o0.at[_dev_index(*place)]

        def copy(k, place, to, from_src=False):
            return pltpu.make_async_remote_copy(
                src_ref=s0 if from_src else block(place), dst_ref=block(place),
                send_sem=send_sems.at[k], recv_sem=recv_sems.at[k], device_id=to, device_id_type=MESH_ID)

        def to_vmem(place, slot):
            return pltpu.make_async_copy(block(place), wbuf.at[slot], local_sems.at[slot])

        own_out = pltpu.make_async_copy(wbuf.at[0], block(me), local_sems.at[2])
        h_copy = pltpu.make_async_copy(hbuf, h_out, local_sems.at[3])

        @pl.when((n == 0) & (m == 0))
        def _():
            copy(0, me, sibling, from_src=True).start()
            for j, chip in enumerate(chips):
                copy(1 + j, me, (*chip, c), from_src=True).start()
            mine = pltpu.make_async_copy(s0, wbuf.at[0], local_sems.at[0])
            mine.start()
            mine.wait()
            own_out.start()

        @pl.when(n == 0)
        def _():
            xv = x_ref[...]
            xin_ref[...] = jnp.where(m == 0, pltpu.roll(xv, ROW0, 0), xv)

            @pl.when(m == 0)
            def _():
                xin_ref[0:PAD_FRONT, :] = jnp.zeros((PAD_FRONT, D), F32)
                xin_ref[PAD_FRONT:ROW0, :] = meta_ref[...]

            xv = xin_ref[...]
            r = lax.rsqrt(jnp.mean(xv * xv, axis=-1, keepdims=True) + EPS)
            hbuf[m] = (xv * r * g_ref[...]).astype(BF16)

        plan = [(sibling, (0, sibling), None)]
        for j, chip in enumerate(chips):
            plan.append(((*chip, c), (1 + j, (*chip, c)), 4 + j))
            plan.append(((*chip, 1 - c), (4 + j, (*chip, 1 - c)), None))

        for s, (place, (k, origin), pass_on) in enumerate(plan, start=1):
            @pl.when((n == s - 1) & (m == last_m))
            def _(s=s, place=place, k=k, origin=origin, pass_on=pass_on):
                copy(k, origin, me).wait_recv()
                if pass_on is not None:
                    copy(pass_on, place, sibling).start()
                if s == 2:
                    own_out.wait()
                to_vmem(place, s % 2).start()

            @pl.when((n == s) & (m == 0))
            def _(s=s, place=place):
                to_vmem(place, s % 2).wait()

        proj_ref[...] = _dot(hbuf[m], wbuf[lax.rem(n, 2)]).astype(BF16)

        @pl.when((n == 0) & (m == last_m))
        def _():
            h_copy.start()

        @pl.when((n == N_DEV - 1) & (m == last_m))
        def _():
            for k in range(7):
                copy(k, me, me).wait_send()
            h_copy.wait()

    return pl.pallas_call(
        body, name="gather_and_proj",
        grid_spec=pltpu.PrefetchScalarGridSpec(
            num_scalar_prefetch=1, grid=(N_DEV, n_m),
            in_specs=[pl.BlockSpec((pl.Element(tm), pl.Element(D)),
                                   lambda n, m, o: (_window_start(jnp.where(n == 0, m, 0), tm), 0)),
                      pl.BlockSpec((N_META, D), lambda n, m, o: (0, 0)),
                      pl.BlockSpec((1, D), lambda n, m, o: (0, 0)), ANY],
            out_specs=(pl.BlockSpec((tm, W_IN_BLK), lambda n, m, o: (m, o[n])),
                       pl.BlockSpec((tm, D), lambda n, m, o: (jnp.where(n == 0, m, last_m), 0)), ANY, ANY),
            scratch_shapes=[pltpu.VMEM((n_m, tm, D), BF16), pltpu.VMEM((2, D, W_IN_BLK), BF16),
                            pltpu.SemaphoreType.DMA((7,)), pltpu.SemaphoreType.DMA((7,)),
                            pltpu.SemaphoreType.DMA((4,))]),
        out_shape=(jax.ShapeDtypeStruct((TP, D_IN), BF16), jax.ShapeDtypeStruct((TP, D), F32),
                   jax.ShapeDtypeStruct((n_m, tm, D), BF16), jax.ShapeDtypeStruct((N_DEV, D, W_IN_BLK), BF16)),
        compiler_params=pltpu.CompilerParams(dimension_semantics=("arbitrary", "arbitrary")),
    )(order, x_seq, meta_full, norm_g, w_in_b)


def _chip_rel(x, y, r):
    return (jnp.bitwise_xor(x, r >> 1), jnp.bitwise_xor(y, r & 1))


def _exchange_small(pack, srs):
    def body(pk, sr, pk_all, sr_all, send_sems, recv_sems, local_sems):
        x, y, c = _my_place()
        my_id = _dev_index(x, y, c)
        mine = [pltpu.make_async_copy(pk, pk_all.at[my_id], local_sems.at[0]),
                pltpu.make_async_copy(sr.at[my_id], sr_all.at[my_id], local_sems.at[1])]
        for cp in mine:
            cp.start()
        copies = []
        for r in range(1, N_DEV):
            peer = (jnp.bitwise_xor(x, (r >> 2) & 1), jnp.bitwise_xor(y, (r >> 1) & 1), jnp.bitwise_xor(c, r & 1))
            peer_id = _dev_index(*peer)
            for a, (src, dst) in enumerate(((pk, pk_all.at[my_id]), (sr.at[peer_id], sr_all.at[my_id]))):
                cp = pltpu.make_async_remote_copy(
                    src_ref=src, dst_ref=dst, send_sem=send_sems.at[a * 7 + r - 1], recv_sem=recv_sems.at[a * 7 + r - 1],
                    device_id=peer, device_id_type=MESH_ID)
                cp.start()
                copies.append(cp)
        for cp in copies:
            cp.wait_recv()
        for cp in copies:
            cp.wait_send()
        for cp in mine:
            cp.wait()

    return pl.pallas_call(
        body, name="exchange_small",
        out_shape=(jax.ShapeDtypeStruct((N_DEV,) + pack.shape, F32), jax.ShapeDtypeStruct(srs.shape, F32)),
        in_specs=[ANY, ANY], out_specs=(ANY, ANY),
        scratch_shapes=[pltpu.SemaphoreType.DMA((14,)), pltpu.SemaphoreType.DMA((14,)), pltpu.SemaphoreType.DMA((2,))],
    )(pack, srs)


N_CB = D // HEAD_W


def _store_by_cb(ref, idx, rows, val):
    for cb in range(N_CB):
        ref[(*idx, cb, rows, slice(None))] = val[:, cb * HEAD_W:(cb + 1) * HEAD_W]


def _fill_shifts(sh, tm):
    n = tm + HALO - 8
    for s in range(1, 8):
        for cb in range(N_CB):
            sh[s, cb, 0:n, :] = sh[0, cb, s:s + n, :]


def _conv_fwd(proj, conv_w, conv_b):
    tm = TM_ELT
    strip = tm // CONV_STRIPS

    def body(p_ref, w_ref, b_ref, c0_ref, sh, c0_sc):
        i = pl.program_id(0)

        @pl.when(i == 0)
        def _():
            sh[0, :, 0:HALO, :] = jnp.zeros((N_CB, HALO, HEAD_W), F32)

        @pl.when(i > 0)
        def _():
            sh[0, :, 0:HALO, :] = sh[0, :, tm:tm + HALO, :]

        ga = p_ref[:, 0:D].astype(F32)
        gb = p_ref[:, D:2 * D].astype(F32)
        _store_by_cb(sh, (0,), slice(HALO, HALO + tm), ga * _sigmoid(gb))
        _fill_shifts(sh, tm)
        for cb in range(N_CB):
            cs = slice(cb * HEAD_W, (cb + 1) * HEAD_W)
            for st in range(CONV_STRIPS):
                acc = jnp.broadcast_to(b_ref[:, cs], (strip, HEAD_W))
                for j in range(CONV_K):
                    off = HALO - (CONV_K - 1) + j
                    lo = st * strip + 8 * (off // 8)
                    acc = acc + w_ref[j:j + 1, cs] * sh[off % 8, cb, lo:lo + strip, :]
                c0_sc[st * strip:(st + 1) * strip, cs] = acc
        c0_ref[...] = c0_sc[...].astype(ACT)

    return pl.pallas_call(
        body, name="conv_fwd", grid=(TP // tm,),
        in_specs=[pl.BlockSpec((tm, 2 * D), lambda i: (i, 0)), pl.BlockSpec((CONV_K, D), lambda i: (0, 0)),
                  pl.BlockSpec((1, D), lambda i: (0, 0))],
        out_specs=pl.BlockSpec((tm, D), lambda i: (i, 0)),
        out_shape=jax.ShapeDtypeStruct((TP, D), ACT),
        scratch_shapes=[pltpu.VMEM((8, N_CB, HALO + tm, HEAD_W), F32), pltpu.VMEM((tm, D), F32)],
        compiler_params=pltpu.CompilerParams(dimension_semantics=("arbitrary",)),
    )(proj, conv_w, conv_b)


def _gates(p_ref, lbl_ref, chunk, bsc):
    lb = _sigmoid(lbl_ref[0:1, :] - lbl_ref[1:2, :])
    q_raw = p_ref[:, 0:D].astype(F32)
    f_raw = p_ref[:, D:2 * D].astype(F32)
    sq = _sigmoid(q_raw)
    q = q_raw * sq
    sg = _sigmoid(f_raw)
    f = lb + (1.0 - lb) * sg
    row = lax.broadcasted_iota(jnp.int32, (CHUNK, 1), 0) + chunk * CHUNK
    valid = row >= PAD_FRONT
    lf = jnp.where(valid, jnp.log(f), 0.0)
    k = jnp.where(valid, 1.0 - f, 0.0)
    r_i = lax.broadcasted_iota(jnp.int32, (CHUNK, CHUNK), 0)
    c_i = lax.broadcasted_iota(jnp.int32, (CHUNK, CHUNK), 1)
    causal = r_i >= c_i
    bsc[...] = _tri_matmul(causal.astype(BF16), lf)
    b = bsc[...]
    b_mid = bsc[CHUNK // 2 - 1:CHUNK // 2, :]
    b_last = bsc[CHUNK - 1:CHUNK, :]
    e_q = jnp.exp(b)
    e_qm = jnp.exp(b - b_mid)
    e_km = jnp.exp(b_mid - b)
    e_kh = jnp.exp(b_last - b)
    e_last = jnp.exp(b_last)
    return dict(lb=lb, q_raw=q_raw, sq=sq, q=q, sg=sg, f=f, k=k, valid=valid, causal=causal,
                e_q=e_q, e_qm=e_qm, e_km=e_km, e_kh=e_kh, e_last=e_last)


def _rec_fwd(proj, lb_logits, w3_b):
    cps = CHUNKS_PER_STEP
    rows = cps * CHUNK

    def body(p_ref, lbl_ref, w3s_ref, o_ref, s_ref, w3o_ref, st, bsc, w3buf, send_sems, recv_sems, local_sems):
        n = pl.program_id(0)
        gather_start, gather_finish = _w3_gather(w3s_ref, w3o_ref, w3buf, send_sems, recv_sems, local_sems)

        @pl.when(n == 0)
        def _():
            st[...] = jnp.zeros_like(st)
            gather_start()

        def prep(ci):
            g = _gates(p_ref.at[pl.ds(ci * CHUNK, CHUNK)], lbl_ref, n * cps + ci, bsc.at[ci])
            g["q1"] = (g["q"] * g["e_q"]).astype(BF16)
            g["qm"] = (g["q"] * g["e_qm"]).astype(BF16)
            g["km"] = (g["k"] * g["e_km"]).astype(BF16)
            g["kh"] = (g["k"] * g["e_kh"]).astype(BF16)
            return g

        def heads(ci, g):
            rs = pl.ds(ci * CHUNK, CHUNK)
            pv = p_ref.at[rs]
            s_ref[ci] = st[...]
            for h in range(HEADS):
                sl = slice(h * HEAD_W, (h + 1) * HEAD_W)
                v = pv[:, 2 * D + h * HEAD_W:2 * D + (h + 1) * HEAD_W]
                att = jnp.where(g["causal"], _dot_nt(g["qm"][:, sl], g["km"][:, sl]), 0.0).astype(BF16)
                s_h = st[h]
                o_ref[rs, sl] = (_dot_nt(g["q1"][:, sl], s_h.astype(BF16)) + _dot(att, v)).astype(ACT)
                st[h] = s_h * g["e_last"][:, sl] + _dot_tn(v, g["kh"][:, sl])

        ready = prep(0)
        for ci in range(cps):
            coming = prep(ci + 1) if ci + 1 < cps else None
            heads(ci, ready)
            ready = coming

        @pl.when(n == N_CHUNK // cps - 1)
        def _():
            gather_finish()

    return pl.pallas_call(
        body, name="rec_fwd", grid=(N_CHUNK // cps,),
        in_specs=[pl.BlockSpec((rows, 3 * D), lambda n: (n, 1)), pl.BlockSpec((2, D), lambda n: (0, 0)), ANY],
        out_specs=(pl.BlockSpec((rows, D), lambda n: (n, 0)),
                   pl.BlockSpec((cps, HEADS, HEAD_W, HEAD_W), lambda n: (n, 0, 0, 0)), ANY),
        out_shape=(jax.ShapeDtypeStruct((TP, D), ACT), jax.ShapeDtypeStruct((N_CHUNK, HEADS, HEAD_W, HEAD_W), F32),
                   jax.ShapeDtypeStruct((3, D, D), BF16)),
        scratch_shapes=[pltpu.VMEM((HEADS, HEAD_W, HEAD_W), F32), pltpu.VMEM((cps, CHUNK, D), F32),
                        pltpu.VMEM((3, W_ROW_BLK, D), BF16), pltpu.SemaphoreType.DMA((7,)),
                        pltpu.SemaphoreType.DMA((7,)), pltpu.SemaphoreType.DMA((2,))],
        compiler_params=pltpu.CompilerParams(dimension_semantics=("arbitrary",)),
    )(proj, lb_logits, w3_b)


def _rec_bwd(proj, lb_logits, d_o, s_start, dproj, p3):
    cps = CHUNKS_PER_STEP
    rows = cps * CHUNK
    last = N_CHUNK // cps - 1

    def body(p_ref, lbl_ref, do_ref, s_ref, dproj_in, p3_ref, dp_ref, dlb_ref, land_ref, dst, bsc, dq_sc, dk_sc, g_sc,
             send_sems, recv_sems):
        del dproj_in
        n = pl.program_id(0)
        ride_start, ride_finish = _p3_to_sibling(p3_ref, land_ref, send_sems, recv_sems)

        @pl.when(n == 0)
        def _():
            ride_start()
            dst[...] = jnp.zeros_like(dst)
            dlb_ref[...] = jnp.zeros_like(dlb_ref)

        def prep(ci):
            g = _gates(p_ref.at[pl.ds(ci * CHUNK, CHUNK)], lbl_ref, (last - n) * cps + ci, bsc.at[ci])
            g["q1"] = (g["q"] * g["e_q"]).astype(BF16)
            qm_f = g["q"] * g["e_qm"]
            km_f = g["k"] * g["e_km"]
            g["qm"] = qm_f.astype(BF16)
            g["km"] = km_f.astype(BF16)
            g["qm_lo"] = (qm_f - g["qm"].astype(F32)).astype(BF16)
            g["km_lo"] = (km_f - g["km"].astype(F32)).astype(BF16)
            g["kh_f"] = g["k"] * g["e_kh"]
            g["kh"] = g["kh_f"].astype(BF16)
            return g

        def heads_and_post(ci, g):
            rs = pl.ds(ci * CHUNK, CHUNK)
            pv = p_ref.at[rs]
            dpv = dp_ref.at[rs]
            q1, qm, km, qm_lo, km_lo, kh_f, kh = (g[k] for k in ("q1", "qm", "km", "qm_lo", "km_lo", "kh_f", "kh"))
            for h in range(HEADS):
                sl = slice(h * HEAD_W, (h + 1) * HEAD_W)
                v = pv[:, 2 * D + h * HEAD_W:2 * D + (h + 1) * HEAD_W]
                d_oh = do_ref[rs, sl].astype(BF16)
                s0 = s_ref[ci, h]
                ds_end = dst[h]
                ds_end_b = ds_end.astype(BF16)
                att = jnp.where(g["causal"], _dot_nt(qm[:, sl], km[:, sl]), 0.0).astype(BF16)
                d_att = jnp.where(g["causal"], _dot_nt(d_oh, v), 0.0).astype(BF16)
                d_v = _dot_tn(att, d_oh) + _dot_nt(kh[:, sl], ds_end_b)
                d_qm2 = _dot(d_att, jnp.concatenate([km[:, sl], km_lo[:, sl]], axis=1))
                d_qm = d_qm2[:, 0:HEAD_W] + d_qm2[:, HEAD_W:2 * HEAD_W]
                d_q1 = _dot(d_oh, s0.astype(BF16))
                d_km2 = _dot_tn(d_att, jnp.concatenate([qm[:, sl], qm_lo[:, sl]], axis=1))
                d_km = d_km2[:, 0:HEAD_W] + d_km2[:, HEAD_W:2 * HEAD_W]
                d_kh = _dot(v, ds_end_b)
                dq_sc[ci, :, sl] = d_qm * g["e_qm"][:, sl] + d_q1 * g["e_q"][:, sl]
                dk_sc[ci, :, sl] = d_km * g["e_km"][:, sl] + d_kh * g["e_kh"][:, sl]
                g_sc[ci, :, sl] = (jnp.sum(kh_f[:, sl] * d_kh, axis=0, keepdims=True)
                                   + g["e_last"][:, sl] * jnp.sum(ds_end * s0, axis=0, keepdims=True))
                dst[h] = ds_end * g["e_last"][:, sl] + _dot_tn(d_oh, q1[:, sl])
                dpv[:, 2 * D + h * HEAD_W:2 * D + (h + 1) * HEAD_W] = d_v.astype(BF16)
            d_q = dq_sc[ci]
            d_k = dk_sc[ci]
            d_b = g["q"] * d_q - g["k"] * d_k
            anti = jnp.logical_not(g["causal"]) | (lax.broadcasted_iota(jnp.int32, (CHUNK, CHUNK), 0)
                                                    == lax.broadcasted_iota(jnp.int32, (CHUNK, CHUNK), 1))
            d_lf = _tri_matmul(anti.astype(BF16), d_b) + g_sc[ci]
            d_f = jnp.where(g["valid"], d_lf / g["f"] - d_k, 0.0)
            sg = g["sg"]
            dlb_ref[0:1, :] += jnp.sum(d_f * (1.0 - sg), axis=0, keepdims=True)
            dpv[:, 0:D] = (d_q * _dsilu(g["q_raw"], g["sq"])).astype(BF16)
            dpv[:, D:2 * D] = (d_f * (1.0 - g["lb"]) * sg * (1.0 - sg)).astype(BF16)

        ready = prep(cps - 1)
        for ci in reversed(range(cps)):
            coming = prep(ci - 1) if ci > 0 else None
            heads_and_post(ci, ready)
            ready = coming

        @pl.when(n == last)
        def _():
            ride_finish()

    return pl.pallas_call(
        body, name="rec_bwd", grid=(N_CHUNK // cps,),
        in_specs=[pl.BlockSpec((rows, 3 * D), lambda n: (last - n, 1)), pl.BlockSpec((2, D), lambda n: (0, 0)),
                  pl.BlockSpec((rows, D), lambda n: (last - n, 0)),
                  pl.BlockSpec((cps, HEADS, HEAD_W, HEAD_W), lambda n: (last - n, 0, 0, 0)), ANY, ANY],
        out_specs=(pl.BlockSpec((rows, 3 * D), lambda n: (last - n, 1)), pl.BlockSpec((8, D), lambda n: (0, 0)), ANY),
        out_shape=(jax.ShapeDtypeStruct((TP, D_IN), BF16), jax.ShapeDtypeStruct((8, D), F32),
                   jax.ShapeDtypeStruct((4, 3, W_ROW_BLK, D), F32)),
        scratch_shapes=[pltpu.VMEM((HEADS, HEAD_W, HEAD_W), F32), pltpu.VMEM((cps, CHUNK, D), F32),
                        pltpu.VMEM((cps, CHUNK, D), F32), pltpu.VMEM((cps, CHUNK, D), F32),
                        pltpu.VMEM((cps, 1, D), F32), pltpu.SemaphoreType.DMA((4,)), pltpu.SemaphoreType.DMA((4,))],
        input_output_aliases={4: 0},
        compiler_params=pltpu.CompilerParams(dimension_semantics=("arbitrary",)),
    )(proj, lb_logits, d_o, s_start, dproj, p3)


def _mid(xin, tgt, o, c0, proj, w3, ln_g, ln_b, gnorm_g, final_g):
    tm = TM_ELT

    def body(x_ref, t_ref, o_ref, c0_ref, z_ref, gr_ref, mc_ref, mr_ref, w_ref, lng_ref, lnb_ref, gng_ref, fg_ref,
             do_ref, dc0_ref, dz_ref, dp_ref, a3_ref, b3_ref, red_ref, on_sc, don_sc):
        i = pl.program_id(0)

        @pl.when(i == 0)
        def _():
            red_ref[...] = jnp.zeros_like(red_ref)

        w_conv, w_rec, w_out = w_ref[0], w_ref[1], w_ref[2]
        c0v = c0_ref[...].astype(F32)
        mu = jnp.mean(c0v, axis=-1, keepdims=True)
        xc = c0v - mu
        rstd = lax.rsqrt(jnp.mean(xc * xc, axis=-1, keepdims=True) + EPS)
        xh = xc * rstd
        c1 = xh * lng_ref[...] + lnb_ref[...]
        s1 = _sigmoid(c1)
        c2 = c1 * s1
        z = z_ref[...].astype(F32)
        sz = _sigmoid(z)
        silu_z = z * sz
        u_conv = (c2 * silu_z).astype(BF16)
        y_conv = _dot(u_conv, w_conv)
        ov = o_ref[...].astype(F32)
        r3 = []
        for h in range(HEADS):
            sl = slice(h * HEAD_W, (h + 1) * HEAD_W)
            oh = ov[:, sl]
            r_h = lax.rsqrt(jnp.mean(oh * oh, axis=-1, keepdims=True) + EPS)
            r3.append(r_h)
            on_sc[:, sl] = oh * r_h
        o_n = on_sc[...]
        o_g = o_n * gng_ref[...]
        gr = gr_ref[...].astype(F32)
        sgr = _sigmoid(gr)
        silu_g = gr * sgr
        u_rec = (o_g * silu_g).astype(BF16)
        y_rec = _dot(u_rec, w_rec)
        mc = mc_ref[...].astype(F32)
        mr = mr_ref[...].astype(F32)
        smc = _sigmoid(mc)
        smr = _sigmoid(mr)
        merged = (smc * y_conv + smr * y_rec).astype(BF16)
        res = x_ref[...] + _dot(merged, w_out)
        r2 = lax.rsqrt(jnp.mean(res * res, axis=-1, keepdims=True) + EPS)
        xh2 = res * r2
        row = lax.broadcasted_iota(jnp.int32, (tm, 1), 0) + i * tm
        real = row >= ROW0
        tgt = t_ref[...]
        tgt = jnp.where(i == 0, pltpu.roll(tgt, ROW0, 0), tgt)
        diff = jnp.where(real, xh2 * fg_ref[...] - tgt, 0.0)
        d_y = diff * (1.0 / D)
        d_xh2 = d_y * fg_ref[...]
        d_res = r2 * (d_xh2 - xh2 * jnp.mean(d_xh2 * xh2, axis=-1, keepdims=True))
        d_res_b = d_res.astype(BF16)
        d_merged = _dot_nt(d_res_b, w_out)
        d_yc = (d_merged * smc).astype(BF16)
        d_yr = (d_merged * smr).astype(BF16)
        dp_ref[:, D:2 * D] = (d_merged * y_conv * smc * (1.0 - smc)).astype(BF16)
        dp_ref[:, 2 * D:3 * D] = (d_merged * y_rec * smr * (1.0 - smr)).astype(BF16)
        d_ur = _dot_nt(d_yr, w_rec)
        d_og = d_ur * silu_g
        dp_ref[:, 0:D] = (d_ur * o_g * _dsilu(gr, sgr)).astype(BF16)
        d_on = d_og * gng_ref[...]
        for h in range(HEADS):
            sl = slice(h * HEAD_W, (h + 1) * HEAD_W)
            d_h = d_on[:, sl]
            n_h = o_n[:, sl]
            don_sc[:, sl] = r3[h] * (d_h - n_h * jnp.mean(d_h * n_h, axis=-1, keepdims=True))
        do_ref[...] = don_sc[...].astype(ACT)
        d_uc = _dot_nt(d_yc, w_conv)
        d_c2 = d_uc * silu_z
        dz_ref[...] = (d_uc * c2 * _dsilu(z, sz)).astype(BF16)
        d_c1 = d_c2 * _dsilu(c1, s1)
        d_xh = d_c1 * lng_ref[...]
        d_c0 = rstd * (d_xh - jnp.mean(d_xh, axis=-1, keepdims=True)
                       - xh * jnp.mean(d_xh * xh, axis=-1, keepdims=True))
        dc0_ref[...] = d_c0.astype(ACT)
        a3_ref[0] = u_conv
        b3_ref[0] = d_yc
        a3_ref[1] = u_rec
        b3_ref[1] = d_yr
        a3_ref[2] = merged
        b3_ref[2] = d_res_b
        def colsum(vv):
            return jnp.sum(vv, axis=0, keepdims=True)

        red_ref[0:1, :] += colsum(d_y * xh2)
        red_ref[1:2, :] += colsum(d_og * o_n)
        red_ref[2:3, :] += colsum(d_c1 * xh)
        red_ref[3:4, :] += colsum(d_c1)
        red_ref[4:5, :] += colsum(d_c0)
        red_ref[5:6, :] += colsum(diff * diff) * (0.5 / D)

    def row_block(width, col):
        return pl.BlockSpec((tm, width), lambda i: (i, col))

    def const_block(shape):
        return pl.BlockSpec(shape, lambda i: (0,) * len(shape))

    stack = jax.ShapeDtypeStruct((3, TP, D), BF16)
    stack_spec = pl.BlockSpec((3, tm, D), lambda i: (0, i, 0))
    return pl.pallas_call(
        body, name="mid", grid=(TP // tm,),
        in_specs=[row_block(D, 0),
                  pl.BlockSpec((pl.Element(tm), pl.Element(D)), lambda i: (_window_start(i, tm), 0)),
                  row_block(D, 0), row_block(D, 0),
                  row_block(D, 2), row_block(D, 6), row_block(D, 7), row_block(D, 8),
                  pl.BlockSpec((3, D, D), lambda i: (0, 0, 0), pipeline_mode=pl.Buffered(1)),
                  const_block((1, D)), const_block((1, D)), const_block((1, D)), const_block((1, D))],
        out_specs=(row_block(D, 0), row_block(D, 0), row_block(D, 0), row_block(3 * D, 2),
                   stack_spec, stack_spec, const_block((8, D))),
        out_shape=(jax.ShapeDtypeStruct((TP, D), ACT), jax.ShapeDtypeStruct((TP, D), ACT),
                   jax.ShapeDtypeStruct((TP, D), BF16),
                   jax.ShapeDtypeStruct((TP, D_IN), BF16), stack, stack, jax.ShapeDtypeStruct((8, D), F32)),
        scratch_shapes=[pltpu.VMEM((tm, D), F32), pltpu.VMEM((tm, D), F32)],
        compiler_params=pltpu.CompilerParams(dimension_semantics=("arbitrary",), vmem_limit_bytes=60 * 1024 * 1024),
    )(xin, tgt, o, c0, proj, proj, proj, proj, w3, ln_g, ln_b, gnorm_g, final_g)


def _conv_bwd(proj, d_c0, d_z, conv_w, dproj, chip1b):
    tm = TM_ELT
    n_tile = TP // tm
    lastt = n_tile - 1

    strip = tm // CONV_STRIPS

    def body(p_ref, dc_ref, dz_ref, w_ref, dproj_in, c1_ref, dp_ref, dw_ref, far_ref, dsh, a_sc, da_sc, acc,
             send_sems, recv_sems):
        del dproj_in
        i = pl.program_id(0)
        ride_start, ride_finish = _partials_to_owners(c1_ref, far_ref, send_sems, recv_sems)

        @pl.when(i == 0)
        def _():
            ride_start()
            dsh[0, :, tm:tm + HALO, :] = jnp.zeros((N_CB, HALO, HEAD_W), F32)
            acc[...] = jnp.zeros_like(acc)

        @pl.when(i > 0)
        def _():
            dsh[0, :, tm:tm + HALO, :] = dsh[0, :, 0:HALO, :]

        _store_by_cb(dsh, (0,), slice(0, tm), dc_ref[...].astype(F32))
        _fill_shifts(dsh, tm)
        ga = p_ref[:, 0:D].astype(F32)
        sb = _sigmoid(p_ref[:, D:2 * D].astype(F32))
        a = ga * sb
        _store_by_cb(a_sc, (), slice(0, tm), a)
        for cb in range(N_CB):
            cs = slice(cb * HEAD_W, (cb + 1) * HEAD_W)
            for st in range(CONV_STRIPS):
                rows = slice(st * strip, (st + 1) * strip)
                a_s = a_sc[cb, rows, :]
                d_a = jnp.zeros((strip, HEAD_W), F32)
                for j in range(CONV_K):
                    off = CONV_K - 1 - j
                    lo = st * strip + 8 * (off // 8)
                    slab = dsh[off % 8, cb, lo:lo + strip, :]
                    d_a = d_a + w_ref[j:j + 1, cs] * slab
                    acc[j, :, cs] += jnp.sum((a_s * slab).reshape(strip // 8, 8, HEAD_W), axis=0)
                da_sc[rows, cs] = d_a
        d_a = da_sc[...]
        dp_ref[:, 0:D] = (d_a * sb).astype(BF16)
        dp_ref[:, D:2 * D] = (d_a * a * (1.0 - sb)).astype(BF16)
        dp_ref[:, 2 * D:3 * D] = dz_ref[...]

        @pl.when(i == lastt)
        def _():
            for j in range(CONV_K):
                dw_ref[j:j + 1, :] = jnp.sum(acc[j], axis=0, keepdims=True)
            dw_ref[CONV_K:CONV_K + 1, :] = jnp.zeros((1, D), F32)
            ride_finish()

    return pl.pallas_call(
        body, name="conv_bwd", grid=(n_tile,),
        in_specs=[pl.BlockSpec((tm, 2 * D), lambda i: (lastt - i, 0)), pl.BlockSpec((tm, D), lambda i: (lastt - i, 0)),
                  pl.BlockSpec((tm, D), lambda i: (lastt - i, 0)), pl.BlockSpec((CONV_K, D), lambda i: (0, 0)), ANY, ANY],
        out_specs=(pl.BlockSpec((tm, 3 * D), lambda i: (lastt - i, 0)), pl.BlockSpec((CONV_K + 1, D), lambda i: (0, 0)),
                   ANY),
        out_shape=(jax.ShapeDtypeStruct((TP, D_IN), BF16), jax.ShapeDtypeStruct((CONV_K + 1, D), F32),
                   jax.ShapeDtypeStruct((3, 3, W_ROW_BLK, D), BF16)),
        scratch_shapes=[pltpu.VMEM((8, N_CB, tm + HALO, HEAD_W), F32), pltpu.VMEM((N_CB, tm, HEAD_W), F32),
                        pltpu.VMEM((tm, D), F32), pltpu.VMEM((CONV_K, 8, D), F32),
                        pltpu.SemaphoreType.DMA((3,)), pltpu.SemaphoreType.DMA((3,))],
        input_output_aliases={4: 0},
        compiler_params=pltpu.CompilerParams(dimension_semantics=("arbitrary",)),
    )(proj, d_c0, d_z, conv_w, dproj, chip1b)


def _wgrad3(a3, b3):
    tt = TT_WGRAD

    def body(a_ref, b_ref, o_ref):
        @pl.when(pl.program_id(1) == 0)
        def _():
            o_ref[...] = jnp.zeros_like(o_ref)

        o_ref[0] += _dot_tn(a_ref[0], b_ref[0])

    return pl.pallas_call(
        body, name="wgrad3", grid=(3, TP // tt),
        in_specs=[pl.BlockSpec((1, tt, D), lambda g, t: (g, t, 0)), pl.BlockSpec((1, tt, D), lambda g, t: (g, t, 0))],
        out_specs=pl.BlockSpec((1, D, D), lambda g, t: (g, 0, 0)),
        out_shape=jax.ShapeDtypeStruct((3, D, D), F32),
        compiler_params=pltpu.CompilerParams(dimension_semantics=("arbitrary", "arbitrary")),
    )(a3, b3)


def _wgrad_in(h, dproj, ids):
    tt = TT_WGRAD
    n_t = TP // tt

    def body(ids_ref, a_ref, b_ref, o_ref, ob_ref, l0_ref, acc, tmp, send_sems, recv_sems, tmp_sem):
        del ids_ref
        r = pl.program_id(0)
        t = pl.program_id(1)
        x, y, c = _my_place()
        sibling = (x, y, 1 - c)
        slot = lax.rem(r, 2)

        def send_in(q):
            return pltpu.make_async_remote_copy(
                src_ref=acc.at[q % 2], dst_ref=l0_ref.at[q], send_sem=send_sems.at[q], recv_sem=recv_sems.at[q],
                device_id=sibling, device_id_type=MESH_ID)

        def landed(q):
            return pltpu.make_async_copy(l0_ref.at[q], tmp, tmp_sem)

        @pl.when(t == 0)
        def _():
            acc[slot] = jnp.zeros((D, W_IN_BLK), F32)

        acc[slot] += _dot_tn(a_ref[...], b_ref[...])

        for q in range(4):
            @pl.when((r == q) & (t == n_t - 1))
            def _(q=q):
                if q >= 1:
                    send_in(q - 1).wait_send()
                send_in(q).start()

            @pl.when((r == 4 + q) & (t == n_t - 2))
            def _(q=q):
                if q == 0:
                    send_in(3).wait_send()
                send_in(q).wait_recv()
                landed(q).start()

            @pl.when((r == 4 + q) & (t == n_t - 1))
            def _(q=q):
                landed(q).wait()
                tot = acc[q % 2] + tmp[...]
                o_ref[0] = tot
                ob_ref[0] = tot.astype(BF16)

    blk = pl.BlockSpec((1, D, W_IN_BLK), lambda r, t, ids: (jnp.maximum(r - 4, 0), 0, 0))
    return pl.pallas_call(
        body, name="wgrad_in",
        grid_spec=pltpu.PrefetchScalarGridSpec(
            num_scalar_prefetch=1, grid=(N_DEV, n_t),
            in_specs=[pl.BlockSpec((tt, D), lambda r, t, ids: (t, 0)),
                      pl.BlockSpec((tt, W_IN_BLK), lambda r, t, ids: (t, ids[r]))],
            out_specs=(blk, blk, ANY),
            scratch_shapes=[pltpu.VMEM((2, D, W_IN_BLK), F32), pltpu.VMEM((D, W_IN_BLK), F32),
                            pltpu.SemaphoreType.DMA((4,)), pltpu.SemaphoreType.DMA((4,)), pltpu.SemaphoreType.DMA]),
        out_shape=(jax.ShapeDtypeStruct((4, D, W_IN_BLK), F32), jax.ShapeDtypeStruct((4, D, W_IN_BLK), BF16),
                   jax.ShapeDtypeStruct((4, D, W_IN_BLK), F32)),
        compiler_params=pltpu.CompilerParams(dimension_semantics=("arbitrary", "arbitrary")),
    )(ids, h, dproj)


def _chip_sum_3(p3, land1, ids_mine):
    def body(ids_ref, p_ref, l_ref, o_ref, ob_ref):
        del ids_ref
        tot = p_ref[...] + l_ref[0]
        o_ref[0] = tot
        ob_ref[0] = tot.astype(BF16)

    blk = pl.BlockSpec((1, 3, W_ROW_BLK, D), lambda r, ids: (r, 0, 0, 0))
    return pl.pallas_call(
        body, name="chip_sum_3",
        grid_spec=pltpu.PrefetchScalarGridSpec(
            num_scalar_prefetch=1, grid=(4,),
            in_specs=[pl.BlockSpec((3, W_ROW_BLK, D), lambda r, ids: (0, ids[r], 0)), blk],
            out_specs=(blk, blk)),
        out_shape=(jax.ShapeDtypeStruct((4, 3, W_ROW_BLK, D), F32), jax.ShapeDtypeStruct((4, 3, W_ROW_BLK, D), BF16)),
    )(ids_mine, p3, land1)


def _dh_and_norm_bwd(dproj, w_in_full, xin, b3, norm_g, chip0b):
    tm = TM_MAT
    n_k = N_DEV // DH_K_BLKS
    n_m = TP // tm

    def body(dp_ref, w_ref, x_ref, dr_ref, g_ref, c0_ref, dx_ref, dg_ref, f0_ref, acc, send_sems, recv_sems):
        m = pl.program_id(0)
        k = pl.program_id(1)
        ride_start, ride_finish = _partials_to_owners(c0_ref, f0_ref, send_sems, recv_sems)

        @pl.when((m == 0) & (k == 0))
        def _():
            ride_start()

        @pl.when(k == 0)
        def _():
            acc[...] = jnp.zeros_like(acc)

        part = _dot_nt(dp_ref[:, 0:W_IN_BLK], w_ref[0])
        for j in range(1, DH_K_BLKS):
            part = part + _dot_nt(dp_ref[:, j * W_IN_BLK:(j + 1) * W_IN_BLK], w_ref[j])
        acc[...] += part

        @pl.when((k == n_k - 1) & (m == 0))
        def _():
            dg_ref[...] = jnp.zeros_like(dg_ref)

        @pl.when(k == n_k - 1)
        def _():
            xv = x_ref[...]
            r1 = lax.rsqrt(jnp.mean(xv * xv, axis=-1, keepdims=True) + EPS)
            xh = xv * r1
            d_h = acc[...]
            dg_ref[0:1, :] += jnp.sum(d_h * xh, axis=0, keepdims=True)
            d_xh = d_h * g_ref[...]
            dx_ref[...] = dr_ref[0].astype(F32) + r1 * (d_xh - xh * jnp.mean(d_xh * xh, axis=-1, keepdims=True))

        @pl.when((m == n_m - 1) & (k == n_k - 1))
        def _():
            ride_finish()

    return pl.pallas_call(
        body, name="dh_norm_bwd", grid=(n_m, n_k),
        in_specs=[pl.BlockSpec((tm, DH_K_BLKS * W_IN_BLK), lambda m, k: (m, k)),
                  pl.BlockSpec((DH_K_BLKS, D, W_IN_BLK), lambda m, k: (k, 0, 0)),
                  pl.BlockSpec((tm, D), lambda m, k: (m, 0)), pl.BlockSpec((1, tm, D), lambda m, k: (2, m, 0)),
                  pl.BlockSpec((1, D), lambda m, k: (0, 0)), ANY],
        out_specs=(pl.BlockSpec((tm, D), lambda m, k: (m, 0)), pl.BlockSpec((8, D), lambda m, k: (0, 0)), ANY),
        out_shape=(jax.ShapeDtypeStruct((TP, D), F32), jax.ShapeDtypeStruct((8, D), F32),
                   jax.ShapeDtypeStruct((3, D, W_IN_BLK), BF16)),
        scratch_shapes=[pltpu.VMEM((tm, D), F32), pltpu.SemaphoreType.DMA((3,)), pltpu.SemaphoreType.DMA((3,))],
        compiler_params=pltpu.CompilerParams(dimension_semantics=("arbitrary", "arbitrary")),
    )(dproj, w_in_full, xin, b3, norm_g, chip0b)


def _sum_adamw(own, landed, w, m, v, tr, name):
    rows, cols = w.shape
    n_t = rows // tr

    def body(o_ref, l1_ref, l2_ref, l3_ref, w_ref, m_ref, v_ref, g_ref, d_ref, m2_ref, v2_ref):
        g = ((o_ref[...] + l1_ref[...].astype(F32)) + l2_ref[...].astype(F32)) + l3_ref[...].astype(F32)
        delta, m2, v2 = _adamw(w_ref[...], g, m_ref[...], v_ref[...])
        g_ref[...] = g
        d_ref[...] = delta
        m2_ref[...] = m2
        v2_ref[...] = v2

    def spec(k):
        return pl.BlockSpec((tr, cols), lambda i: (i + k * n_t, 0))

    out = jax.ShapeDtypeStruct((rows, cols), F32)
    return pl.pallas_call(
        body, name=name, grid=(n_t,),
        in_specs=[spec(0), spec(0), spec(1), spec(2), spec(0), spec(0), spec(0)],
        out_specs=(spec(0),) * 4, out_shape=(out,) * 4,
    )(own, landed, landed, landed, w, m, v)


def _small_update(pack_all, srs_all, lb_logits, p8, m8, v8, ws, ms, vs):
    def body(pk_ref, sr_ref, lbl_ref, p_ref, m_ref, v_ref, ws_ref, ms_ref, vs_ref,
             g8_ref, d8_ref, m8_ref, v8_ref, loss_ref, gs_ref, ds_ref, ms2_ref, vs2_ref):
        tot = pk_ref[0]
        tot_s = sr_ref[0]
        for d in range(1, N_DEV):
            tot = tot + pk_ref[d]
            tot_s = tot_s + sr_ref[d]
        p0 = _sigmoid(lbl_ref[0:1, :] - lbl_ref[1:2, :])
        row = lax.broadcasted_iota(jnp.int32, (8, D), 0)
        d_lb = jnp.sum(jnp.where(row == 4, tot, 0.0), axis=0, keepdims=True)
        d_l0 = d_lb * p0 * (1.0 - p0)
        loss_ref[...] = jnp.sum(jnp.where(row == 5, tot, 0.0), keepdims=True).reshape(1, 1)
        g8 = jnp.where(row == 4, d_l0, jnp.where(row == 5, -d_l0, tot))
        delta, m2, v2 = _adamw(p_ref[...], g8, m_ref[...], v_ref[...])
        g8_ref[...] = g8
        d8_ref[...] = delta
        m8_ref[...] = m2
        v8_ref[...] = v2
        delta, m2, v2 = _adamw(ws_ref[...], tot_s, ms_ref[...], vs_ref[...])
        gs_ref[...] = tot_s
        ds_ref[...] = delta
        ms2_ref[...] = m2
        vs2_ref[...] = v2

    o8 = jax.ShapeDtypeStruct((8, D), F32)
    os_ = jax.ShapeDtypeStruct((SMALL_ROWS, HEAD_W), F32)
    return pl.pallas_call(
        body, name="small_update",
        out_shape=(o8, o8, o8, o8, jax.ShapeDtypeStruct((1, 1), F32), os_, os_, os_, os_),
    )(pack_all, srs_all, lb_logits, p8, m8, v8, ws, ms, vs)


def _local_step(xin, proj, target, conv_w_full, conv_b, ln_g, ln_b, w3_b, lb_logits, gnorm_g, final_g, ids_mine):
    fg = final_g.reshape(1, D)
    c0 = _conv_fwd(proj, conv_w_full, conv_b)
    o, s_start, w3_full = _rec_fwd(proj, lb_logits, w3_b)
    d_o, d_c0, d_z, dproj, a3, b3, red = _mid(xin, target, o, c0, proj, w3_full, ln_g, ln_b, gnorm_g, fg)
    p3 = _wgrad3(a3, b3)
    dproj, dlb, land1 = _rec_bwd(proj, lb_logits, d_o, s_start, dproj, p3)
    chip1, chip1b = _chip_sum_3(p3, land1, ids_mine)
    dproj, d_conv_w, far1 = _conv_bwd(proj, d_c0, d_z, conv_w_full, dproj, chip1b)
    return dproj, b3, p3, chip1, far1, d_conv_w, red, dlb


def kernel(x, meta_tokens, norm_g, w_in, conv_w, conv_b, ln_g, ln_b, w_conv_out, lb_logits, gnorm_g, w_rec_out, w_out, final_g, loss_target, m_meta_tokens, m_norm_g, m_w_in, m_conv_w, m_conv_b, m_ln_g, m_ln_b, m_w_conv_out, m_lb_logits, m_gnorm_g, m_w_rec_out, m_w_out, m_final_g, v_meta_tokens, v_norm_g, v_w_in, v_conv_w, v_conv_b, v_ln_g, v_ln_b, v_w_conv_out, v_lb_logits, v_gnorm_g, v_w_rec_out, v_w_out, v_final_g):
    def small_pack(cw, mt):
        return jnp.concatenate([cw[0], jnp.zeros((1, HEAD_W), F32), mt], axis=0)

    def stack3(a, b, c):
        return jnp.concatenate([a, b, c], axis=0)

    def stack8(ng, cb, lg, lb_, lbl, gg, fg):
        return jnp.concatenate([ng, cb, lg, lb_, lbl, gg, fg.reshape(1, D)], axis=0)

    mx, my, mc = _my_place()

    w3_s = stack3(w_conv_out, w_rec_out, w_out)
    ws_s = small_pack(conv_w, meta_tokens)
    small_full = jnp.transpose(_gather_small(ws_s), (1, 0, 2)).reshape(SMALL_ROWS, D)
    conv_w_full = small_full[0:CONV_K]
    meta_full = small_full[META_ROW:META_ROW + N_META]
    w_in_b, w3_b = _cast_shards(w_in[0], w3_s)
    use_order = [(mx, my, mc), (mx, my, 1 - mc)]
    for chip in ((1 - mx, my), (mx, 1 - my), (1 - mx, 1 - my)):
        use_order += [(*chip, mc), (*chip, 1 - mc)]
    order = jnp.stack([_dev_index(*p) for p in use_order]).astype(jnp.int32)
    proj, xin, h, w_in_full = _gather_and_proj(x[0], meta_full, norm_g, w_in_b, order)
    h = h.reshape(TP, D)

    ids_mine = jnp.stack([_dev_index(*_chip_rel(mx, my, r), mc) for r in range(4)]).astype(jnp.int32)
    ids_sib = jnp.stack([_dev_index(*_chip_rel(mx, my, r), 1 - mc) for r in range(4)]).astype(jnp.int32)
    dproj, b3, _, chip1, far1, d_conv_w, red, dlb = _local_step(
        xin, proj, loss_target[0], conv_w_full, conv_b, ln_g, ln_b, w3_b, lb_logits, gnorm_g, final_g, ids_mine)

    chip0, chip0b, _ = _wgrad_in(h, dproj, jnp.concatenate([ids_sib, ids_mine]))
    d_xin, dng, far0 = _dh_and_norm_bwd(dproj, w_in_full, xin, b3, norm_g, chip0b)
    pack = jnp.concatenate([dng[0:1], red[4:5], red[2:3], red[3:4], dlb[0:1], red[5:6], red[1:2], red[0:1]], axis=0)
    g_in, d_in, m_in, v_in = _sum_adamw(chip0.reshape(4 * D, W_IN_BLK), far0.reshape(3 * D, W_IN_BLK), w_in[0],
                                        m_w_in[0], v_w_in[0], 256, "adamw_in")
    g_3, d_3, m_3, v_3 = _sum_adamw(
        chip1.reshape(12 * W_ROW_BLK, D), far1.reshape(9 * W_ROW_BLK, D), w3_s.reshape(3 * W_ROW_BLK, D),
        stack3(m_w_conv_out, m_w_rec_out, m_w_out).reshape(3 * W_ROW_BLK, D),
        stack3(v_w_conv_out, v_w_rec_out, v_w_out).reshape(3 * W_ROW_BLK, D), 3 * W_ROW_BLK, "adamw_3")

    srs = jnp.concatenate([d_conv_w, d_xin[PAD_FRONT:ROW0]], axis=0)
    srs = jnp.transpose(srs.reshape(SMALL_ROWS, N_DEV, HEAD_W), (1, 0, 2))
    pack_all, srs_all = _exchange_small(pack, srs)
    g8, d8, m8, v8, loss, gs, ds, ms, vs = _small_update(
        pack_all, srs_all, lb_logits,
        stack8(norm_g, conv_b, ln_g, ln_b, lb_logits, gnorm_g, final_g),
        stack8(m_norm_g, m_conv_b, m_ln_g, m_ln_b, m_lb_logits, m_gnorm_g, m_final_g),
        stack8(v_norm_g, v_conv_b, v_ln_g, v_ln_b, v_lb_logits, v_gnorm_g, v_final_g),
        ws_s, small_pack(m_conv_w, m_meta_tokens), small_pack(v_conv_w, v_meta_tokens))

    def unpack(a_in, a_3, a_s, a_8):
        t3 = a_3.reshape(3, 1, W_ROW_BLK, D)
        return (a_s[META_ROW:META_ROW + N_META], a_8[0:1], a_in[None], a_s[0:CONV_K][None], a_8[1:2], a_8[2:3],
                a_8[3:4], t3[0], a_8[4:6], a_8[6:7], t3[1], t3[2], a_8[7])

    grad_x = d_xin[ROW0:][None]
    return (loss.reshape(()), grad_x, *unpack(g_in, g_3, gs, g8), *unpack(d_in, d_3, ds, d8),
            *unpack(m_in, m_3, ms, m8), *unpack(v_in, v_3, vs, v8))
```

```python
import functools

import jax
import jax.numpy as jnp
from jax import lax
from jax.experimental import pallas as pl
from jax.experimental.pallas import tpu as pltpu

F32 = jnp.float32
BF16 = jnp.bfloat16
ACT = BF16

D = 1024
SEQ = 4096
N_META = 16
CHUNK = 64
PAD_FRONT = 48
ROW0 = PAD_FRONT + N_META
TP = ROW0 + SEQ
N_CHUNK = TP // CHUNK
HEADS = 8
HEAD_W = 128
D_IN = 9 * D
N_DEV = 8
W_IN_BLK = D_IN // N_DEV
W_ROW_BLK = D // N_DEV
CONV_K = 31
SMALL_ROWS = 48
META_ROW = 32
EPS = 1e-6
HALO = 32

TM_MAT = 832
TT_WGRAD = 2080
DH_K_BLKS = 2
TM_ELT = 208
CHUNKS_PER_STEP = 5
CONV_STRIPS = 2

ADAM_LR = 0.001
ADAM_B1 = 0.9
ADAM_B2 = 0.999
ADAM_EPS = 1e-08
ADAM_WD = 0.01
ADAM_STEP = 10

MESH_ID = pl.DeviceIdType.MESH
ANY = pl.BlockSpec(memory_space=pl.ANY)


def _sigmoid(v):
    return jax.nn.sigmoid(v)


def _dsilu(v, s):
    return s * (1.0 + v * (1.0 - s))


def _dot(a, b):
    return jnp.dot(a, b, preferred_element_type=F32)


def _dot_nt(a, b):
    return lax.dot_general(a, b, (((1,), (1,)), ((), ())), preferred_element_type=F32)


def _dot_tn(a, b):
    return lax.dot_general(a, b, (((0,), (0,)), ((), ())), preferred_element_type=F32)


def _split3(v):
    hi = v.astype(BF16)
    r1 = v - hi.astype(F32)
    mid = r1.astype(BF16)
    lo = (r1 - mid.astype(F32)).astype(BF16)
    return hi, mid, lo


def _tri_matmul(tri, v):
    hi, mid, lo = _split3(v)
    return _dot(tri, hi) + _dot(tri, mid) + _dot(tri, lo)


def _adamw(w, g, m, v):
    m2 = ADAM_B1 * m + (1.0 - ADAM_B1) * g
    v2 = ADAM_B2 * v + (1.0 - ADAM_B2) * jnp.square(g)
    m_hat = m2 / (1.0 - ADAM_B1 ** ADAM_STEP)
    v_hat = v2 / (1.0 - ADAM_B2 ** ADAM_STEP)
    delta = -ADAM_LR * (m_hat / (jnp.sqrt(v_hat) + ADAM_EPS) + ADAM_WD * w)
    return delta, m2, v2


def _window_start(i, tm):
    assert tm % 16 == 0 and ROW0 % 16 == 0
    return pl.multiple_of(16 * jnp.maximum((tm // 16) * i - ROW0 // 16, 0), 16)


def _my_place():
    return lax.axis_index("x"), lax.axis_index("y"), lax.axis_index("c")


def _dev_index(px, py, pc):
    return 4 * px + 2 * py + pc


def _cast_shards(w_in_s, w_conv_s, w_rec_s, w_out_s):
    def body(a_ref, c_ref, r_ref, o_ref, oa_ref, ob_ref):
        oa_ref[...] = a_ref[...].astype(BF16)
        for k, ref in enumerate((c_ref, r_ref, o_ref)):
            ob_ref[k] = ref[0].astype(BF16)

    return pl.pallas_call(
        body, name="cast_shards",
        out_shape=(jax.ShapeDtypeStruct(w_in_s.shape, BF16), jax.ShapeDtypeStruct((3, W_ROW_BLK, D), BF16)),
    )(w_in_s, w_conv_s, w_rec_s, w_out_s)


def _peer(x, y, c, r):
    return (jnp.bitwise_xor(x, (r >> 2) & 1), jnp.bitwise_xor(y, (r >> 1) & 1), jnp.bitwise_xor(c, r & 1))


def _gather_small(small_s):
    def body(s_ref, o_ref, send_sems, recv_sems, local_sem):
        x, y, c = _my_place()
        my_id = _dev_index(x, y, c)
        mine = pltpu.make_async_copy(s_ref, o_ref.at[my_id], local_sem)
        mine.start()
        copies = []
        for r in range(1, N_DEV):
            cp = pltpu.make_async_remote_copy(
                src_ref=s_ref, dst_ref=o_ref.at[my_id], send_sem=send_sems.at[r - 1], recv_sem=recv_sems.at[r - 1],
                device_id=_peer(x, y, c, r), device_id_type=MESH_ID)
            cp.start()
            copies.append(cp)
        for cp in copies:
            cp.wait_recv()
        for cp in copies:
            cp.wait_send()
        mine.wait()

    return pl.pallas_call(
        body, name="gather_small", out_shape=jax.ShapeDtypeStruct((N_DEV,) + small_s.shape, F32),
        in_specs=[ANY], out_specs=ANY,
        scratch_shapes=[pltpu.SemaphoreType.DMA((7,)), pltpu.SemaphoreType.DMA((7,)), pltpu.SemaphoreType.DMA],
    )(small_s)


def _w3_gather(src, out, stage, send_sems, recv_sems, local_sems):
    x, y, c = _my_place()
    me, sibling = (x, y, c), (x, y, 1 - c)
    chips = [(1 - x, y), (x, 1 - y), (1 - x, 1 - y)]

    def block(place):
        d = _dev_index(*place)
        return out.at[:, pl.ds(pl.multiple_of(d * W_ROW_BLK, W_ROW_BLK), W_ROW_BLK), :]

    def copy(k, place, to, from_src=False):
        return pltpu.make_async_remote_copy(
            src_ref=src if from_src else block(place), dst_ref=block(place),
            send_sem=send_sems.at[k], recv_sem=recv_sems.at[k], device_id=to, device_id_type=MESH_ID)

    own_in = pltpu.make_async_copy(src, stage, local_sems.at[0])
    own_out = pltpu.make_async_copy(stage, block(me), local_sems.at[1])

    def start():
        copy(0, me, sibling, from_src=True).start()
        for j, chip in enumerate(chips):
            copy(1 + j, me, (*chip, c), from_src=True).start()
        own_in.start()
        own_in.wait()
        own_out.start()

    def finish():
        for j, chip in enumerate(chips):
            copy(1 + j, (*chip, c), me).wait_recv()
            copy(4 + j, (*chip, c), sibling).start()
        copy(0, sibling, me).wait_recv()
        for j, chip in enumerate(chips):
            copy(4 + j, (*chip, 1 - c), me).wait_recv()
        for k in range(7):
            copy(k, me, me).wait_send()
        own_out.wait()

    return start, finish


def _p3_to_sibling(p3_ref, land_ref, send_sems, recv_sems):
    x, y, c = _my_place()

    def cp(q):
        d = _dev_index(*_chip_rel(x, y, q), 1 - c)
        return pltpu.make_async_remote_copy(
            src_ref=p3_ref.at[:, pl.ds(pl.multiple_of(d * W_ROW_BLK, W_ROW_BLK), W_ROW_BLK), :],
            dst_ref=land_ref.at[q], send_sem=send_sems.at[q], recv_sem=recv_sems.at[q],
            device_id=(x, y, 1 - c), device_id_type=MESH_ID)

    def start():
        for q in range(4):
            cp(q).start()

    def finish():
        for q in range(4):
            cp(q).wait_recv()
        for q in range(4):
            cp(q).wait_send()

    return start, finish


def _partials_to_owners(src_ref, far_ref, send_sems, recv_sems):
    x, y, c = _my_place()

    def cp(q):
        return pltpu.make_async_remote_copy(
            src_ref=src_ref.at[q], dst_ref=far_ref.at[q - 1], send_sem=send_sems.at[q - 1],
            recv_sem=recv_sems.at[q - 1], device_id=(*_chip_rel(x, y, q), c), device_id_type=MESH_ID)

    def start():
        for q in range(1, 4):
            cp(q).start()

    def finish():
        for q in range(1, 4):
            cp(q).wait_recv()
        for q in range(1, 4):
            cp(q).wait_send()

    return start, finish


def _gather_and_proj(x_seq, meta_full, norm_g, w_in_b, order):
    tm = TM_MAT
    n_m = TP // tm
    last_m = n_m - 1

    def body(order_ref, x_ref, meta_ref, g_ref, s0, proj_ref, xin_ref, h_out, o0, hbuf, wbuf, send_sems, recv_sems,
             local_sems):
        del order_ref
        n = pl.program_id(0)
        m = pl.program_id(1)
        x, y, c = _my_place()
        me, sibling = (x, y, c), (x, y, 1 - c)
        chips = [(1 - x, y), (x, 1 - y), (1 - x, 1 - y)]

        def block(place):
            return o0.at[_dev_index(*place)]

        def copy(k, place, to, from_src=False):
            return pltpu.make_async_remote_copy(
                src_ref=s0 if from_src else block(place), dst_ref=block(place),
                send_sem=send_sems.at[k], recv_sem=recv_sems.at[k], device_id=to, device_id_type=MESH_ID)

        def to_vmem(place, slot):
            return pltpu.make_async_copy(block(place), wbuf.at[slot], local_sems.at[slot])

        own_out = pltpu.make_async_copy(wbuf.at[0], block(me), local_sems.at[2])
        h_copy = pltpu.make_async_copy(hbuf, h_out, local_sems.at[3])

        @pl.when((n == 0) & (m == 0))
        def _():
            copy(0, me, sibling, from_src=True).start()
            for j, chip in enumerate(chips):
                copy(1 + j, me, (*chip, c), from_src=True).start()
            mine = pltpu.make_async_copy(s0, wbuf.at[0], local_sems.at[0])
            mine.start()
            mine.wait()
            own_out.start()

        @pl.when(n == 0)
        def _():
            xv = x_ref[...]
            xin_ref[...] = jnp.where(m == 0, pltpu.roll(xv, ROW0, 0), xv)

            @pl.when(m == 0)
            def _():
                xin_ref[0:PAD_FRONT, :] = jnp.zeros((PAD_FRONT, D), F32)
                xin_ref[PAD_FRONT:ROW0, :] = meta_ref[...]

            xv = xin_ref[...]
            r = lax.rsqrt(jnp.mean(xv * xv, axis=-1, keepdims=True) + EPS)
            hbuf[m] = (xv * r * g_ref[...]).astype(BF16)

        plan = [(sibling, (0, sibling), None)]
        for j, chip in enumerate(chips):
            plan.append(((*chip, c), (1 + j, (*chip, c)), 4 + j))
            plan.append(((*chip, 1 - c), (4 + j, (*chip, 1 - c)), None))

        for s, (place, (k, origin), pass_on) in enumerate(plan, start=1):
            @pl.when((n == s - 1) & (m == last_m))
            def _(s=s, place=place, k=k, origin=origin, pass_on=pass_on):
                copy(k, origin, me).wait_recv()
                if pass_on is not None:
                    copy(pass_on, place, sibling).start()
                if s == 2:
                    own_out.wait()
                to_vmem(place, s % 2).start()

            @pl.when((n == s) & (m == 0))
            def _(s=s, place=place):
                to_vmem(place, s % 2).wait()

        proj_ref[...] = _dot(hbuf[m], wbuf[lax.rem(n, 2)]).astype(BF16)

        @pl.when((n == 0) & (m == last_m))
        def _():
            h_copy.start()

        @pl.when((n == N_DEV - 1) & (m == last_m))
        def _():
            for k in range(7):
                copy(k, me, me).wait_send()
            h_copy.wait()

    return pl.pallas_call(
        body, name="gather_and_proj",
        grid_spec=pltpu.PrefetchScalarGridSpec(
            num_scalar_prefetch=1, grid=(N_DEV, n_m),
            in_specs=[pl.BlockSpec((pl.Element(tm), pl.Element(D)),
                                   lambda n, m, o: (_window_start(jnp.where(n == 0, m, 0), tm), 0)),
                      pl.BlockSpec((N_META, D), lambda n, m, o: (0, 0)),
                      pl.BlockSpec((1, D), lambda n, m, o: (0, 0)), ANY],
            out_specs=(pl.BlockSpec((tm, W_IN_BLK), lambda n, m, o: (m, o[n])),
                       pl.BlockSpec((tm, D), lambda n, m, o: (jnp.where(n == 0, m, last_m), 0)), ANY, ANY),
            scratch_shapes=[pltpu.VMEM((n_m, tm, D), BF16), pltpu.VMEM((2, D, W_IN_BLK), BF16),
                            pltpu.SemaphoreType.DMA((7,)), pltpu.SemaphoreType.DMA((7,)),
                            pltpu.SemaphoreType.DMA((4,))]),
        out_shape=(jax.ShapeDtypeStruct((TP, D_IN), BF16), jax.ShapeDtypeStruct((TP, D), F32),
                   jax.ShapeDtypeStruct((n_m, tm, D), BF16), jax.ShapeDtypeStruct((N_DEV, D, W_IN_BLK), BF16)),
        compiler_params=pltpu.CompilerParams(dimension_semantics=("arbitrary", "arbitrary")),
    )(order, x_seq, meta_full, norm_g, w_in_b)


def _chip_rel(x, y, r):
    return (jnp.bitwise_xor(x, r >> 1), jnp.bitwise_xor(y, r & 1))


def _exchange_small(pack, srs):
    def body(pk, sr, pk_all, sr_all, send_sems, recv_sems, local_sems):
        x, y, c = _my_place()
        my_id = _dev_index(x, y, c)
        mine = [pltpu.make_async_copy(pk, pk_all.at[my_id], local_sems.at[0]),
                pltpu.make_async_copy(sr.at[my_id], sr_all.at[my_id], local_sems.at[1])]
        for cp in mine:
            cp.start()
        copies = []
        for r in range(1, N_DEV):
            peer = (jnp.bitwise_xor(x, (r >> 2) & 1), jnp.bitwise_xor(y, (r >> 1) & 1), jnp.bitwise_xor(c, r & 1))
            peer_id = _dev_index(*peer)
            for a, (src, dst) in enumerate(((pk, pk_all.at[my_id]), (sr.at[peer_id], sr_all.at[my_id]))):
                cp = pltpu.make_async_remote_copy(
                    src_ref=src, dst_ref=dst, send_sem=send_sems.at[a * 7 + r - 1], recv_sem=recv_sems.at[a * 7 + r - 1],
                    device_id=peer, device_id_type=MESH_ID)
                cp.start()
                copies.append(cp)
        for cp in copies:
            cp.wait_recv()
        for cp in copies:
            cp.wait_send()
        for cp in mine:
            cp.wait()

    return pl.pallas_call(
        body, name="exchange_small",
        out_shape=(jax.ShapeDtypeStruct((N_DEV,) + pack.shape, F32), jax.ShapeDtypeStruct(srs.shape, F32)),
        in_specs=[ANY, ANY], out_specs=(ANY, ANY),
        scratch_shapes=[pltpu.SemaphoreType.DMA((14,)), pltpu.SemaphoreType.DMA((14,)), pltpu.SemaphoreType.DMA((2,))],
    )(pack, srs)


N_CB = D // HEAD_W


def _store_by_cb(ref, idx, rows, val):
    for cb in range(N_CB):
        ref[(*idx, cb, rows, slice(None))] = val[:, cb * HEAD_W:(cb + 1) * HEAD_W]


def _fill_shifts(sh, tm):
    n = tm + HALO - 8
    for s in range(1, 8):
        for cb in range(N_CB):
            sh[s, cb, 0:n, :] = sh[0, cb, s:s + n, :]


def _conv_fwd(proj, conv_w, conv_b):
    tm = TM_ELT
    strip = tm // CONV_STRIPS

    def body(p_ref, w_ref, b_ref, c0_ref, sh, c0_sc):
        i = pl.program_id(0)

        @pl.when(i == 0)
        def _():
            sh[0, :, 0:HALO, :] = jnp.zeros((N_CB, HALO, HEAD_W), F32)

        @pl.when(i > 0)
        def _():
            sh[0, :, 0:HALO, :] = sh[0, :, tm:tm + HALO, :]

        ga = p_ref[:, 0:D].astype(F32)
        gb = p_ref[:, D:2 * D].astype(F32)
        _store_by_cb(sh, (0,), slice(HALO, HALO + tm), ga * _sigmoid(gb))
        _fill_shifts(sh, tm)
        for cb in range(N_CB):
            cs = slice(cb * HEAD_W, (cb + 1) * HEAD_W)
            for st in range(CONV_STRIPS):
                acc = jnp.broadcast_to(b_ref[:, cs], (strip, HEAD_W))
                for j in range(CONV_K):
                    off = HALO - (CONV_K - 1) + j
                    lo = st * strip + 8 * (off // 8)
                    acc = acc + w_ref[j:j + 1, cs] * sh[off % 8, cb, lo:lo + strip, :]
                c0_sc[st * strip:(st + 1) * strip, cs] = acc
        c0_ref[...] = c0_sc[...].astype(ACT)

    return pl.pallas_call(
        body, name="conv_fwd", grid=(TP // tm,),
        in_specs=[pl.BlockSpec((tm, 2 * D), lambda i: (i, 0)), pl.BlockSpec((CONV_K, D), lambda i: (0, 0)),
                  pl.BlockSpec((1, D), lambda i: (0, 0))],
        out_specs=pl.BlockSpec((tm, D), lambda i: (i, 0)),
        out_shape=jax.ShapeDtypeStruct((TP, D), ACT),
        scratch_shapes=[pltpu.VMEM((8, N_CB, HALO + tm, HEAD_W), F32), pltpu.VMEM((tm, D), F32)],
        compiler_params=pltpu.CompilerParams(dimension_semantics=("arbitrary",)),
    )(proj, conv_w, conv_b)


def _gates(p_ref, lbl_ref, chunk, bsc):
    lb = _sigmoid(lbl_ref[0:1, :] - lbl_ref[1:2, :])
    q_raw = p_ref[:, 0:D].astype(F32)
    f_raw = p_ref[:, D:2 * D].astype(F32)
    sq = _sigmoid(q_raw)
    q = q_raw * sq
    sg = _sigmoid(f_raw)
    f = lb + (1.0 - lb) * sg
    row = lax.broadcasted_iota(jnp.int32, (CHUNK, 1), 0) + chunk * CHUNK
    valid = row >= PAD_FRONT
    lf = jnp.where(valid, jnp.log(f), 0.0)
    k = jnp.where(valid, 1.0 - f, 0.0)
    r_i = lax.broadcasted_iota(jnp.int32, (CHUNK, CHUNK), 0)
    c_i = lax.broadcasted_iota(jnp.int32, (CHUNK, CHUNK), 1)
    causal = r_i >= c_i
    bsc[...] = _tri_matmul(causal.astype(BF16), lf)
    b = bsc[...]
    b_mid = bsc[CHUNK // 2 - 1:CHUNK // 2, :]
    b_last = bsc[CHUNK - 1:CHUNK, :]
    e_q = jnp.exp(b)
    e_qm = jnp.exp(b - b_mid)
    e_km = jnp.exp(b_mid - b)
    e_kh = jnp.exp(b_last - b)
    e_last = jnp.exp(b_last)
    return dict(lb=lb, q_raw=q_raw, sq=sq, q=q, sg=sg, f=f, k=k, valid=valid, causal=causal,
                e_q=e_q, e_qm=e_qm, e_km=e_km, e_kh=e_kh, e_last=e_last)


def _rec_fwd(proj, lb_logits, w3_b):
    cps = CHUNKS_PER_STEP
    rows = cps * CHUNK

    def body(p_ref, lbl_ref, w3s_ref, o_ref, s_ref, w3o_ref, st, bsc, w3buf, send_sems, recv_sems, local_sems):
        n = pl.program_id(0)
        gather_start, gather_finish = _w3_gather(w3s_ref, w3o_ref, w3buf, send_sems, recv_sems, local_sems)

        @pl.when(n == 0)
        def _():
            st[...] = jnp.zeros_like(st)
            gather_start()

        def prep(ci):
            g = _gates(p_ref.at[pl.ds(ci * CHUNK, CHUNK)], lbl_ref, n * cps + ci, bsc.at[ci])
            g["q1"] = (g["q"] * g["e_q"]).astype(BF16)
            g["qm"] = (g["q"] * g["e_qm"]).astype(BF16)
            g["km"] = (g["k"] * g["e_km"]).astype(BF16)
            g["kh"] = (g["k"] * g["e_kh"]).astype(BF16)
            return g

        def heads(ci, g):
            rs = pl.ds(ci * CHUNK, CHUNK)
            pv = p_ref.at[rs]
            s_ref[ci] = st[...]
            for h in range(HEADS):
                sl = slice(h * HEAD_W, (h + 1) * HEAD_W)
                v = pv[:, 2 * D + h * HEAD_W:2 * D + (h + 1) * HEAD_W]
                att = jnp.where(g["causal"], _dot_nt(g["qm"][:, sl], g["km"][:, sl]), 0.0).astype(BF16)
                s_h = st[h]
                o_ref[rs, sl] = (_dot_nt(g["q1"][:, sl], s_h.astype(BF16)) + _dot(att, v)).astype(ACT)
                st[h] = s_h * g["e_last"][:, sl] + _dot_tn(v, g["kh"][:, sl])

        ready = prep(0)
        for ci in range(cps):
            coming = prep(ci + 1) if ci + 1 < cps else None
            heads(ci, ready)
            ready = coming

        @pl.when(n == N_CHUNK // cps - 1)
        def _():
            gather_finish()

    return pl.pallas_call(
        body, name="rec_fwd", grid=(N_CHUNK // cps,),
        in_specs=[pl.BlockSpec((rows, 3 * D), lambda n: (n, 1)), pl.BlockSpec((2, D), lambda n: (0, 0)), ANY],
        out_specs=(pl.BlockSpec((rows, D), lambda n: (n, 0)),
                   pl.BlockSpec((cps, HEADS, HEAD_W, HEAD_W), lambda n: (n, 0, 0, 0)), ANY),
        out_shape=(jax.ShapeDtypeStruct((TP, D), ACT), jax.ShapeDtypeStruct((N_CHUNK, HEADS, HEAD_W, HEAD_W), F32),
                   jax.ShapeDtypeStruct((3, D, D), BF16)),
        scratch_shapes=[pltpu.VMEM((HEADS, HEAD_W, HEAD_W), F32), pltpu.VMEM((cps, CHUNK, D), F32),
                        pltpu.VMEM((3, W_ROW_BLK, D), BF16), pltpu.SemaphoreType.DMA((7,)),
                        pltpu.SemaphoreType.DMA((7,)), pltpu.SemaphoreType.DMA((2,))],
        compiler_params=pltpu.CompilerParams(dimension_semantics=("arbitrary",)),
    )(proj, lb_logits, w3_b)


def _rec_bwd(proj, lb_logits, d_o, s_start, dproj, p3):
    cps = CHUNKS_PER_STEP
    rows = cps * CHUNK
    last = N_CHUNK // cps - 1

    def body(p_ref, lbl_ref, do_ref, s_ref, dproj_in, p3_ref, dp_ref, dlb_ref, land_ref, dst, bsc, dq_sc, dk_sc, g_sc,
             send_sems, recv_sems):
        del dproj_in
        n = pl.program_id(0)
        ride_start, ride_finish = _p3_to_sibling(p3_ref, land_ref, send_sems, recv_sems)

        @pl.when(n == 0)
        def _():
            ride_start()
            dst[...] = jnp.zeros_like(dst)
            dlb_ref[...] = jnp.zeros_like(dlb_ref)

        def prep(ci):
            g = _gates(p_ref.at[pl.ds(ci * CHUNK, CHUNK)], lbl_ref, (last - n) * cps + ci, bsc.at[ci])
            g["q1"] = (g["q"] * g["e_q"]).astype(BF16)
            qm_f = g["q"] * g["e_qm"]
            km_f = g["k"] * g["e_km"]
            g["qm"] = qm_f.astype(BF16)
            g["km"] = km_f.astype(BF16)
            g["qm_lo"] = (qm_f - g["qm"].astype(F32)).astype(BF16)
            g["km_lo"] = (km_f - g["km"].astype(F32)).astype(BF16)
            g["kh_f"] = g["k"] * g["e_kh"]
            g["kh"] = g["kh_f"].astype(BF16)
            return g

        def heads_and_post(ci, g):
            rs = pl.ds(ci * CHUNK, CHUNK)
            pv = p_ref.at[rs]
            dpv = dp_ref.at[rs]
            q1, qm, km, qm_lo, km_lo, kh_f, kh = (g[k] for k in ("q1", "qm", "km", "qm_lo", "km_lo", "kh_f", "kh"))
            for h in range(HEADS):
                sl = slice(h * HEAD_W, (h + 1) * HEAD_W)
                v = pv[:, 2 * D + h * HEAD_W:2 * D + (h + 1) * HEAD_W]
                d_oh = do_ref[rs, sl].astype(BF16)
                s0 = s_ref[ci, h]
                ds_end = dst[h]
                ds_end_b = ds_end.astype(BF16)
                att = jnp.where(g["causal"], _dot_nt(qm[:, sl], km[:, sl]), 0.0).astype(BF16)
                d_att = jnp.where(g["causal"], _dot_nt(d_oh, v), 0.0).astype(BF16)
                d_v = _dot_tn(att, d_oh) + _dot_nt(kh[:, sl], ds_end_b)
                d_qm2 = _dot(d_att, jnp.concatenate([km[:, sl], km_lo[:, sl]], axis=1))
                d_qm = d_qm2[:, 0:HEAD_W] + d_qm2[:, HEAD_W:2 * HEAD_W]
                d_q1 = _dot(d_oh, s0.astype(BF16))
                d_km2 = _dot_tn(d_att, jnp.concatenate([qm[:, sl], qm_lo[:, sl]], axis=1))
                d_km = d_km2[:, 0:HEAD_W] + d_km2[:, HEAD_W:2 * HEAD_W]
                d_kh = _dot(v, ds_end_b)
                dq_sc[ci, :, sl] = d_qm * g["e_qm"][:, sl] + d_q1 * g["e_q"][:, sl]
                dk_sc[ci, :, sl] = d_km * g["e_km"][:, sl] + d_kh * g["e_kh"][:, sl]
                g_sc[ci, :, sl] = (jnp.sum(kh_f[:, sl] * d_kh, axis=0, keepdims=True)
                                   + g["e_last"][:, sl] * jnp.sum(ds_end * s0, axis=0, keepdims=True))
                dst[h] = ds_end * g["e_last"][:, sl] + _dot_tn(d_oh, q1[:, sl])
                dpv[:, 2 * D + h * HEAD_W:2 * D + (h + 1) * HEAD_W] = d_v.astype(BF16)
            d_q = dq_sc[ci]
            d_k = dk_sc[ci]
            d_b = g["q"] * d_q - g["k"] * d_k
            anti = jnp.logical_not(g["causal"]) | (lax.broadcasted_iota(jnp.int32, (CHUNK, CHUNK), 0)
                                                    == lax.broadcasted_iota(jnp.int32, (CHUNK, CHUNK), 1))
            d_lf = _tri_matmul(anti.astype(BF16), d_b) + g_sc[ci]
            d_f = jnp.where(g["valid"], d_lf / g["f"] - d_k, 0.0)
            sg = g["sg"]
            dlb_ref[0:1, :] += jnp.sum(d_f * (1.0 - sg), axis=0, keepdims=True)
            dpv[:, 0:D] = (d_q * _dsilu(g["q_raw"], g["sq"])).astype(BF16)
            dpv[:, D:2 * D] = (d_f * (1.0 - g["lb"]) * sg * (1.0 - sg)).astype(BF16)

        ready = prep(cps - 1)
        for ci in reversed(range(cps)):
            coming = prep(ci - 1) if ci > 0 else None
            heads_and_post(ci, ready)
            ready = coming

        @pl.when(n == last)
        def _():
            ride_finish()

    return pl.pallas_call(
        body, name="rec_bwd", grid=(N_CHUNK // cps,),
        in_specs=[pl.BlockSpec((rows, 3 * D), lambda n: (last - n, 1)), pl.BlockSpec((2, D), lambda n: (0, 0)),
                  pl.BlockSpec((rows, D), lambda n: (last - n, 0)),
                  pl.BlockSpec((cps, HEADS, HEAD_W, HEAD_W), lambda n: (last - n, 0, 0, 0)), ANY, ANY],
        out_specs=(pl.BlockSpec((rows, 3 * D), lambda n: (last - n, 1)), pl.BlockSpec((8, D), lambda n: (0, 0)), ANY),
        out_shape=(jax.ShapeDtypeStruct((TP, D_IN), BF16), jax.ShapeDtypeStruct((8, D), F32),
                   jax.ShapeDtypeStruct((4, 3, W_ROW_BLK, D), F32)),
        scratch_shapes=[pltpu.VMEM((HEADS, HEAD_W, HEAD_W), F32), pltpu.VMEM((cps, CHUNK, D), F32),
                        pltpu.VMEM((cps, CHUNK, D), F32), pltpu.VMEM((cps, CHUNK, D), F32),
                        pltpu.VMEM((cps, 1, D), F32), pltpu.SemaphoreType.DMA((4,)), pltpu.SemaphoreType.DMA((4,))],
        input_output_aliases={4: 0},
        compiler_params=pltpu.CompilerParams(dimension_semantics=("arbitrary",)),
    )(proj, lb_logits, d_o, s_start, dproj, p3)


def _mid(xin, tgt, o, c0, proj, w3, ln_g, ln_b, gnorm_g, final_g):
    tm = TM_ELT

    def body(x_ref, t_ref, o_ref, c0_ref, z_ref, gr_ref, mc_ref, mr_ref, w_ref, lng_ref, lnb_ref, gng_ref, fg_ref,
             do_ref, dc0_ref, dz_ref, dp_ref, a3_ref, b3_ref, red_ref, on_sc, don_sc):
        i = pl.program_id(0)

        @pl.when(i == 0)
        def _():
            red_ref[...] = jnp.zeros_like(red_ref)

        w_conv, w_rec, w_out = w_ref[0], w_ref[1], w_ref[2]
        c0v = c0_ref[...].astype(F32)
        mu = jnp.mean(c0v, axis=-1, keepdims=True)
        xc = c0v - mu
        rstd = lax.rsqrt(jnp.mean(xc * xc, axis=-1, keepdims=True) + EPS)
        xh = xc * rstd
        c1 = xh * lng_ref[...] + lnb_ref[...]
        s1 = _sigmoid(c1)
        c2 = c1 * s1
        z = z_ref[...].astype(F32)
        sz = _sigmoid(z)
        silu_z = z * sz
        u_conv = (c2 * silu_z).astype(BF16)
        y_conv = _dot(u_conv, w_conv)
        ov = o_ref[...].astype(F32)
        r3 = []
        for h in range(HEADS):
            sl = slice(h * HEAD_W, (h + 1) * HEAD_W)
            oh = ov[:, sl]
            r_h = lax.rsqrt(jnp.mean(oh * oh, axis=-1, keepdims=True) + EPS)
            r3.append(r_h)
            on_sc[:, sl] = oh * r_h
        o_n = on_sc[...]
        o_g = o_n * gng_ref[...]
        gr = gr_ref[...].astype(F32)
        sgr = _sigmoid(gr)
        silu_g = gr * sgr
        u_rec = (o_g * silu_g).astype(BF16)
        y_rec = _dot(u_rec, w_rec)
        mc = mc_ref[...].astype(F32)
        mr = mr_ref[...].astype(F32)
        smc = _sigmoid(mc)
        smr = _sigmoid(mr)
        merged = (smc * y_conv + smr * y_rec).astype(BF16)
        res = x_ref[...] + _dot(merged, w_out)
        r2 = lax.rsqrt(jnp.mean(res * res, axis=-1, keepdims=True) + EPS)
        xh2 = res * r2
        row = lax.broadcasted_iota(jnp.int32, (tm, 1), 0) + i * tm
        real = row >= ROW0
        tgt = t_ref[...]
        tgt = jnp.where(i == 0, pltpu.roll(tgt, ROW0, 0), tgt)
        diff = jnp.where(real, xh2 * fg_ref[...] - tgt, 0.0)
        d_y = diff * (1.0 / D)
        d_xh2 = d_y * fg_ref[...]
        d_res = r2 * (d_xh2 - xh2 * jnp.mean(d_xh2 * xh2, axis=-1, keepdims=True))
        d_res_b = d_res.astype(BF16)
        d_merged = _dot_nt(d_res_b, w_out)
        d_yc = (d_merged * smc).astype(BF16)
        d_yr = (d_merged * smr).astype(BF16)
        dp_ref[:, D:2 * D] = (d_merged * y_conv * smc * (1.0 - smc)).astype(BF16)
        dp_ref[:, 2 * D:3 * D] = (d_merged * y_rec * smr * (1.0 - smr)).astype(BF16)
        d_ur = _dot_nt(d_yr, w_rec)
        d_og = d_ur * silu_g
        dp_ref[:, 0:D] = (d_ur * o_g * _dsilu(gr, sgr)).astype(BF16)
        d_on = d_og * gng_ref[...]
        for h in range(HEADS):
            sl = slice(h * HEAD_W, (h + 1) * HEAD_W)
            d_h = d_on[:, sl]
            n_h = o_n[:, sl]
            don_sc[:, sl] = r3[h] * (d_h - n_h * jnp.mean(d_h * n_h, axis=-1, keepdims=True))
        do_ref[...] = don_sc[...].astype(ACT)
        d_uc = _dot_nt(d_yc, w_conv)
        d_c2 = d_uc * silu_z
        dz_ref[...] = (d_uc * c2 * _dsilu(z, sz)).astype(BF16)
        d_c1 = d_c2 * _dsilu(c1, s1)
        d_xh = d_c1 * lng_ref[...]
        d_c0 = rstd * (d_xh - jnp.mean(d_xh, axis=-1, keepdims=True)
                       - xh * jnp.mean(d_xh * xh, axis=-1, keepdims=True))
        dc0_ref[...] = d_c0.astype(ACT)
        a3_ref[0] = u_conv
        b3_ref[0] = d_yc
        a3_ref[1] = u_rec
        b3_ref[1] = d_yr
        a3_ref[2] = merged
        b3_ref[2] = d_res_b
        def colsum(vv):
            return jnp.sum(vv, axis=0, keepdims=True)

        red_ref[0:1, :] += colsum(d_y * xh2)
        red_ref[1:2, :] += colsum(d_og * o_n)
        red_ref[2:3, :] += colsum(d_c1 * xh)
        red_ref[3:4, :] += colsum(d_c1)
        red_ref[4:5, :] += colsum(d_c0)
        red_ref[5:6, :] += colsum(diff * diff) * (0.5 / D)

    def row_block(width, col):
        return pl.BlockSpec((tm, width), lambda i: (i, col))

    def const_block(shape):
        return pl.BlockSpec(shape, lambda i: (0,) * len(shape))

    stack = jax.ShapeDtypeStruct((3, TP, D), BF16)
    stack_spec = pl.BlockSpec((3, tm, D), lambda i: (0, i, 0))
    return pl.pallas_call(
        body, name="mid", grid=(TP // tm,),
        in_specs=[row_block(D, 0),
                  pl.BlockSpec((pl.Element(tm), pl.Element(D)), lambda i: (_window_start(i, tm), 0)),
                  row_block(D, 0), row_block(D, 0),
                  row_block(D, 2), row_block(D, 6), row_block(D, 7), row_block(D, 8),
                  pl.BlockSpec((3, D, D), lambda i: (0, 0, 0), pipeline_mode=pl.Buffered(1)),
                  const_block((1, D)), const_block((1, D)), const_block((1, D)), const_block((1, D))],
        out_specs=(row_block(D, 0), row_block(D, 0), row_block(D, 0), row_block(3 * D, 2),
                   stack_spec, stack_spec, const_block((8, D))),
        out_shape=(jax.ShapeDtypeStruct((TP, D), ACT), jax.ShapeDtypeStruct((TP, D), ACT),
                   jax.ShapeDtypeStruct((TP, D), BF16),
                   jax.ShapeDtypeStruct((TP, D_IN), BF16), stack, stack, jax.ShapeDtypeStruct((8, D), F32)),
        scratch_shapes=[pltpu.VMEM((tm, D), F32), pltpu.VMEM((tm, D), F32)],
        compiler_params=pltpu.CompilerParams(dimension_semantics=("arbitrary",), vmem_limit_bytes=60 * 1024 * 1024),
    )(xin, tgt, o, c0, proj, proj, proj, proj, w3, ln_g, ln_b, gnorm_g, final_g)


def _conv_bwd(proj, d_c0, d_z, conv_w, dproj, chip1b):
    tm = TM_ELT
    n_tile = TP // tm
    lastt = n_tile - 1

    strip = tm // CONV_STRIPS

    def body(p_ref, dc_ref, dz_ref, w_ref, dproj_in, c1_ref, dp_ref, dw_ref, far_ref, dsh, a_sc, da_sc, acc,
             send_sems, recv_sems):
        del dproj_in
        i = pl.program_id(0)
        ride_start, ride_finish = _partials_to_owners(c1_ref, far_ref, send_sems, recv_sems)

        @pl.when(i == 0)
        def _():
            ride_start()
            dsh[0, :, tm:tm + HALO, :] = jnp.zeros((N_CB, HALO, HEAD_W), F32)
            acc[...] = jnp.zeros_like(acc)

        @pl.when(i > 0)
        def _():
            dsh[0, :, tm:tm + HALO, :] = dsh[0, :, 0:HALO, :]

        _store_by_cb(dsh, (0,), slice(0, tm), dc_ref[...].astype(F32))
        _fill_shifts(dsh, tm)
        ga = p_ref[:, 0:D].astype(F32)
        sb = _sigmoid(p_ref[:, D:2 * D].astype(F32))
        a = ga * sb
        _store_by_cb(a_sc, (), slice(0, tm), a)
        for cb in range(N_CB):
            cs = slice(cb * HEAD_W, (cb + 1) * HEAD_W)
            for st in range(CONV_STRIPS):
                rows = slice(st * strip, (st + 1) * strip)
                a_s = a_sc[cb, rows, :]
                d_a = jnp.zeros((strip, HEAD_W), F32)
                for j in range(CONV_K):
                    off = CONV_K - 1 - j
                    lo = st * strip + 8 * (off // 8)
                    slab = dsh[off % 8, cb, lo:lo + strip, :]
                    d_a = d_a + w_ref[j:j + 1, cs] * slab
                    acc[j, :, cs] += jnp.sum((a_s * slab).reshape(strip // 8, 8, HEAD_W), axis=0)
                da_sc[rows, cs] = d_a
        d_a = da_sc[...]
        dp_ref[:, 0:D] = (d_a * sb).astype(BF16)
        dp_ref[:, D:2 * D] = (d_a * a * (1.0 - sb)).astype(BF16)
        dp_ref[:, 2 * D:3 * D] = dz_ref[...]

        @pl.when(i == lastt)
        def _():
            for j in range(CONV_K):
                dw_ref[j:j + 1, :] = jnp.sum(acc[j], axis=0, keepdims=True)
            dw_ref[CONV_K:CONV_K + 1, :] = jnp.zeros((1, D), F32)
            ride_finish()

    return pl.pallas_call(
        body, name="conv_bwd", grid=(n_tile,),
        in_specs=[pl.BlockSpec((tm, 2 * D), lambda i: (lastt - i, 0)), pl.BlockSpec((tm, D), lambda i: (lastt - i, 0)),
                  pl.BlockSpec((tm, D), lambda i: (lastt - i, 0)), pl.BlockSpec((CONV_K, D), lambda i: (0, 0)), ANY, ANY],
        out_specs=(pl.BlockSpec((tm, 3 * D), lambda i: (lastt - i, 0)), pl.BlockSpec((CONV_K + 1, D), lambda i: (0, 0)),
                   ANY),
        out_shape=(jax.ShapeDtypeStruct((TP, D_IN), BF16), jax.ShapeDtypeStruct((CONV_K + 1, D), F32),
                   jax.ShapeDtypeStruct((3, 3, W_ROW_BLK, D), BF16)),
        scratch_shapes=[pltpu.VMEM((8, N_CB, tm + HALO, HEAD_W), F32), pltpu.VMEM((N_CB, tm, HEAD_W), F32),
                        pltpu.VMEM((tm, D), F32), pltpu.VMEM((CONV_K, 8, D), F32),
                        pltpu.SemaphoreType.DMA((3,)), pltpu.SemaphoreType.DMA((3,))],
        input_output_aliases={4: 0},
        compiler_params=pltpu.CompilerParams(dimension_semantics=("arbitrary",)),
    )(proj, d_c0, d_z, conv_w, dproj, chip1b)


def _wgrad3(a3, b3):
    tt = TT_WGRAD

    def body(a_ref, b_ref, o_ref):
        @pl.when(pl.program_id(1) == 0)
        def _():
            o_ref[...] = jnp.zeros_like(o_ref)

        o_ref[0] += _dot_tn(a_ref[0], b_ref[0])

    return pl.pallas_call(
        body, name="wgrad3", grid=(3, TP // tt),
        in_specs=[pl.BlockSpec((1, tt, D), lambda g, t: (g, t, 0)), pl.BlockSpec((1, tt, D), lambda g, t: (g, t, 0))],
        out_specs=pl.BlockSpec((1, D, D), lambda g, t: (g, 0, 0)),
        out_shape=jax.ShapeDtypeStruct((3, D, D), F32),
        compiler_params=pltpu.CompilerParams(dimension_semantics=("arbitrary", "arbitrary")),
    )(a3, b3)


def _wgrad_in(h, dproj, ids):
    tt = TT_WGRAD
    n_t = TP // tt

    def body(ids_ref, a_ref, b_ref, o_ref, ob_ref, l0_ref, acc, tmp, send_sems, recv_sems, tmp_sem):
        del ids_ref
        r = pl.program_id(0)
        t = pl.program_id(1)
        x, y, c = _my_place()
        sibling = (x, y, 1 - c)
        slot = lax.rem(r, 2)

        def send_in(q):
            return pltpu.make_async_remote_copy(
                src_ref=acc.at[q % 2], dst_ref=l0_ref.at[q], send_sem=send_sems.at[q], recv_sem=recv_sems.at[q],
                device_id=sibling, device_id_type=MESH_ID)

        def landed(q):
            return pltpu.make_async_copy(l0_ref.at[q], tmp, tmp_sem)

        @pl.when(t == 0)
        def _():
            acc[slot] = jnp.zeros((D, W_IN_BLK), F32)

        acc[slot] += _dot_tn(a_ref[...], b_ref[...])

        for q in range(4):
            @pl.when((r == q) & (t == n_t - 1))
            def _(q=q):
                if q >= 1:
                    send_in(q - 1).wait_send()
                send_in(q).start()

            @pl.when((r == 4 + q) & (t == n_t - 2))
            def _(q=q):
                if q == 0:
                    send_in(3).wait_send()
                send_in(q).wait_recv()
                landed(q).start()

            @pl.when((r == 4 + q) & (t == n_t - 1))
            def _(q=q):
                landed(q).wait()
                tot = acc[q % 2] + tmp[...]
                o_ref[0] = tot
                ob_ref[0] = tot.astype(BF16)

    blk = pl.BlockSpec((1, D, W_IN_BLK), lambda r, t, ids: (jnp.maximum(r - 4, 0), 0, 0))
    return pl.pallas_call(
        body, name="wgrad_in",
        grid_spec=pltpu.PrefetchScalarGridSpec(
            num_scalar_prefetch=1, grid=(N_DEV, n_t),
            in_specs=[pl.BlockSpec((tt, D), lambda r, t, ids: (t, 0)),
                      pl.BlockSpec((tt, W_IN_BLK), lambda r, t, ids: (t, ids[r]))],
            out_specs=(blk, blk, ANY),
            scratch_shapes=[pltpu.VMEM((2, D, W_IN_BLK), F32), pltpu.VMEM((D, W_IN_BLK), F32),
                            pltpu.SemaphoreType.DMA((4,)), pltpu.SemaphoreType.DMA((4,)), pltpu.SemaphoreType.DMA]),
        out_shape=(jax.ShapeDtypeStruct((4, D, W_IN_BLK), F32), jax.ShapeDtypeStruct((4, D, W_IN_BLK), BF16),
                   jax.ShapeDtypeStruct((4, D, W_IN_BLK), F32)),
        compiler_params=pltpu.CompilerParams(dimension_semantics=("arbitrary", "arbitrary")),
    )(ids, h, dproj)


def _chip_sum_3(p3, land1, ids_mine):
    def body(ids_ref, p_ref, l_ref, o_ref, ob_ref):
        del ids_ref
        tot = p_ref[...] + l_ref[0]
        o_ref[0] = tot
        ob_ref[0] = tot.astype(BF16)

    blk = pl.BlockSpec((1, 3, W_ROW_BLK, D), lambda r, ids: (r, 0, 0, 0))
    return pl.pallas_call(
        body, name="chip_sum_3",
        grid_spec=pltpu.PrefetchScalarGridSpec(
            num_scalar_prefetch=1, grid=(4,),
            in_specs=[pl.BlockSpec((3, W_ROW_BLK, D), lambda r, ids: (0, ids[r], 0)), blk],
            out_specs=(blk, blk)),
        out_shape=(jax.ShapeDtypeStruct((4, 3, W_ROW_BLK, D), F32), jax.ShapeDtypeStruct((4, 3, W_ROW_BLK, D), BF16)),
    )(ids_mine, p3, land1)


def _dh_and_norm_bwd(dproj, w_in_full, xin, b3, norm_g, chip0b):
    tm = TM_MAT
    n_k = N_DEV // DH_K_BLKS
    n_m = TP // tm

    def body(dp_ref, w_ref, x_ref, dr_ref, g_ref, c0_ref, dx_ref, dg_ref, f0_ref, acc, send_sems, recv_sems):
        m = pl.program_id(0)
        k = pl.program_id(1)
        ride_start, ride_finish = _partials_to_owners(c0_ref, f0_ref, send_sems, recv_sems)

        @pl.when((m == 0) & (k == 0))
        def _():
            ride_start()

        @pl.when(k == 0)
        def _():
            acc[...] = jnp.zeros_like(acc)

        part = _dot_nt(dp_ref[:, 0:W_IN_BLK], w_ref[0])
        for j in range(1, DH_K_BLKS):
            part = part + _dot_nt(dp_ref[:, j * W_IN_BLK:(j + 1) * W_IN_BLK], w_ref[j])
        acc[...] += part

        @pl.when((k == n_k - 1) & (m == 0))
        def _():
            dg_ref[...] = jnp.zeros_like(dg_ref)

        @pl.when(k == n_k - 1)
        def _():
            xv = x_ref[...]
            r1 = lax.rsqrt(jnp.mean(xv * xv, axis=-1, keepdims=True) + EPS)
            xh = xv * r1
            d_h = acc[...]
            dg_ref[0:1, :] += jnp.sum(d_h * xh, axis=0, keepdims=True)
            d_xh = d_h * g_ref[...]
            dx_ref[...] = dr_ref[0].astype(F32) + r1 * (d_xh - xh * jnp.mean(d_xh * xh, axis=-1, keepdims=True))

        @pl.when((m == n_m - 1) & (k == n_k - 1))
        def _():
            ride_finish()

    return pl.pallas_call(
        body, name="dh_norm_bwd", grid=(n_m, n_k),
        in_specs=[pl.BlockSpec((tm, DH_K_BLKS * W_IN_BLK), lambda m, k: (m, k)),
                  pl.BlockSpec((DH_K_BLKS, D, W_IN_BLK), lambda m, k: (k, 0, 0)),
                  pl.BlockSpec((tm, D), lambda m, k: (m, 0)), pl.BlockSpec((1, tm, D), lambda m, k: (2, m, 0)),
                  pl.BlockSpec((1, D), lambda m, k: (0, 0)), ANY],
        out_specs=(pl.BlockSpec((tm, D), lambda m, k: (m, 0)), pl.BlockSpec((8, D), lambda m, k: (0, 0)), ANY),
        out_shape=(jax.ShapeDtypeStruct((TP, D), F32), jax.ShapeDtypeStruct((8, D), F32),
                   jax.ShapeDtypeStruct((3, D, W_IN_BLK), BF16)),
        scratch_shapes=[pltpu.VMEM((tm, D), F32), pltpu.SemaphoreType.DMA((3,)), pltpu.SemaphoreType.DMA((3,))],
        compiler_params=pltpu.CompilerParams(dimension_semantics=("arbitrary", "arbitrary")),
    )(dproj, w_in_full, xin, b3, norm_g, chip0b)


def _sum_adamw(own, landed, w, m, v, tr, name):
    rows, cols = w.shape
    n_t = rows // tr

    def body(o_ref, l1_ref, l2_ref, l3_ref, w_ref, m_ref, v_ref, g_ref, d_ref, m2_ref, v2_ref):
        g = ((o_ref[...] + l1_ref[...].astype(F32)) + l2_ref[...].astype(F32)) + l3_ref[...].astype(F32)
        delta, m2, v2 = _adamw(w_ref[...], g, m_ref[...], v_ref[...])
        g_ref[...] = g
        d_ref[...] = delta
        m2_ref[...] = m2
        v2_ref[...] = v2

    def spec(k):
        return pl.BlockSpec((tr, cols), lambda i: (i + k * n_t, 0))

    out = jax.ShapeDtypeStruct((rows, cols), F32)
    return pl.pallas_call(
        body, name=name, grid=(n_t,),
        in_specs=[spec(0), spec(0), spec(1), spec(2), spec(0), spec(0), spec(0)],
        out_specs=(spec(0),) * 4, out_shape=(out,) * 4,
    )(own, landed, landed, landed, w, m, v)


def _adamw_3(chip1, far1, ws, ms, vs):
    def body(c_ref, f_ref, *refs):
        w_refs, m_refs, v_refs, outs = refs[0:3], refs[3:6], refs[6:9], refs[9:21]
        for k in range(3):
            g = ((c_ref[0, k] + f_ref[0, k].astype(F32)) + f_ref[1, k].astype(F32)) + f_ref[2, k].astype(F32)
            delta, m2, v2 = _adamw(w_refs[k][0], g, m_refs[k][0], v_refs[k][0])
            for kind, val in enumerate((g, delta, m2, v2)):
                outs[3 * kind + k][0] = val

    full = pl.BlockSpec((1, W_ROW_BLK, D), lambda i: (0, 0, 0))
    out = jax.ShapeDtypeStruct((1, W_ROW_BLK, D), F32)
    res = pl.pallas_call(
        body, name="adamw_3", grid=(1,),
        in_specs=[pl.BlockSpec((1, 3, W_ROW_BLK, D), lambda i: (0, 0, 0, 0)),
                  pl.BlockSpec((3, 3, W_ROW_BLK, D), lambda i: (0, 0, 0, 0))] + [full] * 9,
        out_specs=(full,) * 12, out_shape=(out,) * 12,
    )(chip1, far1, *ws, *ms, *vs)
    return tuple(res[3 * kind:3 * kind + 3] for kind in range(4))


N_SMALL = 9


def _small_update(pack_all, srs_all, ws, ms, vs):
    def body(pk_ref, sr_ref, *refs):
        w_refs, m_refs, v_refs = refs[0:N_SMALL], refs[N_SMALL:2 * N_SMALL], refs[2 * N_SMALL:3 * N_SMALL]
        loss_ref = refs[3 * N_SMALL]
        outs = refs[3 * N_SMALL + 1:7 * N_SMALL + 1]
        tot_sc, tots_sc = refs[7 * N_SMALL + 1:]
        tot = pk_ref[0]
        tot_s = sr_ref[0]
        for d in range(1, N_DEV):
            tot = tot + pk_ref[d]
            tot_s = tot_s + sr_ref[d]
        tot_sc[...] = tot
        tots_sc[...] = tot_s
        loss_ref[...] = jnp.sum(tot_sc[5:6, :], axis=1, keepdims=True)
        lbl = w_refs[4]
        p0 = _sigmoid(lbl[0:1, :] - lbl[1:2, :])
        d_l0 = tot_sc[4:5, :] * p0 * (1.0 - p0)

        def update(k, sel, g):
            delta, m2, v2 = _adamw(w_refs[k][sel], g, m_refs[k][sel], v_refs[k][sel])
            for kind, val in enumerate((g, delta, m2, v2)):
                outs[N_SMALL * kind + k][sel] = val

        everything = (slice(None), slice(None))
        for k, row in ((0, 0), (1, 1), (2, 2), (3, 3), (5, 6), (6, 7)):
            update(k, everything, tot_sc[row:row + 1, :])
        update(4, (slice(0, 1), slice(None)), d_l0)
        update(4, (slice(1, 2), slice(None)), -d_l0)
        update(7, (0, slice(None), slice(None)), tots_sc[0:CONV_K, :])
        update(8, everything, tots_sc[META_ROW:META_ROW + N_META, :])

    shapes = [jax.ShapeDtypeStruct(w.shape, F32) for w in ws]
    res = pl.pallas_call(
        body, name="small_update",
        out_shape=(jax.ShapeDtypeStruct((1, 1), F32), *(shapes * 4)),
        scratch_shapes=[pltpu.VMEM((8, D), F32), pltpu.VMEM((SMALL_ROWS, HEAD_W), F32)],
    )(pack_all, srs_all, *ws, *ms, *vs)
    return res[0], tuple(res[1 + N_SMALL * kind:1 + N_SMALL * (kind + 1)] for kind in range(4))


def _local_step(xin, proj, target, conv_w_full, conv_b, ln_g, ln_b, w3_b, lb_logits, gnorm_g, final_g, ids_mine):
    fg = final_g.reshape(1, D)
    c0 = _conv_fwd(proj, conv_w_full, conv_b)
    o, s_start, w3_full = _rec_fwd(proj, lb_logits, w3_b)
    d_o, d_c0, d_z, dproj, a3, b3, red = _mid(xin, target, o, c0, proj, w3_full, ln_g, ln_b, gnorm_g, fg)
    p3 = _wgrad3(a3, b3)
    dproj, dlb, land1 = _rec_bwd(proj, lb_logits, d_o, s_start, dproj, p3)
    chip1, chip1b = _chip_sum_3(p3, land1, ids_mine)
    dproj, d_conv_w, far1 = _conv_bwd(proj, d_c0, d_z, conv_w_full, dproj, chip1b)
    return dproj, b3, p3, chip1, far1, d_conv_w, red, dlb


def kernel(x, meta_tokens, norm_g, w_in, conv_w, conv_b, ln_g, ln_b, w_conv_out, lb_logits, gnorm_g, w_rec_out, w_out, final_g, loss_target, m_meta_tokens, m_norm_g, m_w_in, m_conv_w, m_conv_b, m_ln_g, m_ln_b, m_w_conv_out, m_lb_logits, m_gnorm_g, m_w_rec_out, m_w_out, m_final_g, v_meta_tokens, v_norm_g, v_w_in, v_conv_w, v_conv_b, v_ln_g, v_ln_b, v_w_conv_out, v_lb_logits, v_gnorm_g, v_w_rec_out, v_w_out, v_final_g):
    mx, my, mc = _my_place()

    ws_s = jnp.concatenate([conv_w[0], jnp.zeros((1, HEAD_W), F32), meta_tokens], axis=0)
    small_full = jnp.transpose(_gather_small(ws_s), (1, 0, 2)).reshape(SMALL_ROWS, D)
    conv_w_full = small_full[0:CONV_K]
    meta_full = small_full[META_ROW:META_ROW + N_META]
    w_in_b, w3_b = _cast_shards(w_in[0], w_conv_out, w_rec_out, w_out)
    use_order = [(mx, my, mc), (mx, my, 1 - mc)]
    for chip in ((1 - mx, my), (mx, 1 - my), (1 - mx, 1 - my)):
        use_order += [(*chip, mc), (*chip, 1 - mc)]
    order = jnp.stack([_dev_index(*p) for p in use_order]).astype(jnp.int32)
    proj, xin, h, w_in_full = _gather_and_proj(x[0], meta_full, norm_g, w_in_b, order)
    h = h.reshape(TP, D)

    ids_mine = jnp.stack([_dev_index(*_chip_rel(mx, my, r), mc) for r in range(4)]).astype(jnp.int32)
    ids_sib = jnp.stack([_dev_index(*_chip_rel(mx, my, r), 1 - mc) for r in range(4)]).astype(jnp.int32)
    dproj, b3, _, chip1, far1, d_conv_w, red, dlb = _local_step(
        xin, proj, loss_target[0], conv_w_full, conv_b, ln_g, ln_b, w3_b, lb_logits, gnorm_g, final_g, ids_mine)

    chip0, chip0b, _ = _wgrad_in(h, dproj, jnp.concatenate([ids_sib, ids_mine]))
    d_xin, dng, far0 = _dh_and_norm_bwd(dproj, w_in_full, xin, b3, norm_g, chip0b)
    pack = jnp.concatenate([dng[0:1], red[4:5], red[2:3], red[3:4], dlb[0:1], red[5:6], red[1:2], red[0:1]], axis=0)
    g_in, d_in, m_in, v_in = _sum_adamw(chip0.reshape(4 * D, W_IN_BLK), far0.reshape(3 * D, W_IN_BLK), w_in[0],
                                        m_w_in[0], v_w_in[0], 256, "adamw_in")
    big3 = _adamw_3(chip1, far1, (w_conv_out, w_rec_out, w_out), (m_w_conv_out, m_w_rec_out, m_w_out),
                    (v_w_conv_out, v_w_rec_out, v_w_out))

    srs = jnp.concatenate([d_conv_w, d_xin[PAD_FRONT:ROW0]], axis=0)
    srs = jnp.transpose(srs.reshape(SMALL_ROWS, N_DEV, HEAD_W), (1, 0, 2))
    pack_all, srs_all = _exchange_small(pack, srs)
    loss, small = _small_update(
        pack_all, srs_all,
        (norm_g, conv_b, ln_g, ln_b, lb_logits, gnorm_g, final_g.reshape(1, D), conv_w, meta_tokens),
        (m_norm_g, m_conv_b, m_ln_g, m_ln_b, m_lb_logits, m_gnorm_g, m_final_g.reshape(1, D), m_conv_w, m_meta_tokens),
        (v_norm_g, v_conv_b, v_ln_g, v_ln_b, v_lb_logits, v_gnorm_g, v_final_g.reshape(1, D), v_conv_w, v_meta_tokens))

    outs = [loss.reshape(()), d_xin[ROW0:][None]]
    for kind, a_in in enumerate((g_in, d_in, m_in, v_in)):
        ng, cb, lg, lb_, lbl, gg, fg, cw, mt = small[kind]
        a_3 = big3[kind]
        outs += [mt, ng, a_in[None], cw, cb, lg, lb_, a_3[0], lbl, gg, a_3[1], a_3[2], fg.reshape(D)]
    return tuple(outs)
```

```python
import functools

import jax
import jax.numpy as jnp
from jax import lax
from jax.experimental import pallas as pl
from jax.experimental.pallas import tpu as pltpu

F32 = jnp.float32
BF16 = jnp.bfloat16
ACT = BF16

D = 1024
SEQ = 4096
N_META = 16
CHUNK = 64
PAD_FRONT = 48
ROW0 = PAD_FRONT + N_META
TP = ROW0 + SEQ
N_CHUNK = TP // CHUNK
HEADS = 8
HEAD_W = 128
D_IN = 9 * D
N_DEV = 8
W_IN_BLK = D_IN // N_DEV
W_ROW_BLK = D // N_DEV
CONV_K = 31
SMALL_ROWS = 48
META_ROW = 32
EPS = 1e-6
HALO = 32

TM_MAT = 832
TT_WGRAD = 2080
DH_K_BLKS = 2
TM_ELT = 208
CHUNKS_PER_STEP = 5
CONV_STRIPS = 2

ADAM_LR = 0.001
ADAM_B1 = 0.9
ADAM_B2 = 0.999
ADAM_EPS = 1e-08
ADAM_WD = 0.01
ADAM_STEP = 10

MESH_ID = pl.DeviceIdType.MESH
ANY = pl.BlockSpec(memory_space=pl.ANY)


def _sigmoid(v):
    return jax.nn.sigmoid(v)


def _dsilu(v, s):
    return s * (1.0 + v * (1.0 - s))


def _dot(a, b):
    return jnp.dot(a, b, preferred_element_type=F32)


def _dot_nt(a, b):
    return lax.dot_general(a, b, (((1,), (1,)), ((), ())), preferred_element_type=F32)


def _dot_tn(a, b):
    return lax.dot_general(a, b, (((0,), (0,)), ((), ())), preferred_element_type=F32)


def _split3(v):
    hi = v.astype(BF16)
    r1 = v - hi.astype(F32)
    mid = r1.astype(BF16)
    lo = (r1 - mid.astype(F32)).astype(BF16)
    return hi, mid, lo


def _tri_matmul(tri, v):
    hi, mid, lo = _split3(v)
    return _dot(tri, hi) + _dot(tri, mid) + _dot(tri, lo)


def _adamw(w, g, m, v):
    m2 = ADAM_B1 * m + (1.0 - ADAM_B1) * g
    v2 = ADAM_B2 * v + (1.0 - ADAM_B2) * jnp.square(g)
    m_hat = m2 / (1.0 - ADAM_B1 ** ADAM_STEP)
    v_hat = v2 / (1.0 - ADAM_B2 ** ADAM_STEP)
    delta = -ADAM_LR * (m_hat / (jnp.sqrt(v_hat) + ADAM_EPS) + ADAM_WD * w)
    return delta, m2, v2


def _window_start(i, tm):
    assert tm % 16 == 0 and ROW0 % 16 == 0
    return pl.multiple_of(16 * jnp.maximum((tm // 16) * i - ROW0 // 16, 0), 16)


def _my_place():
    return lax.axis_index("x"), lax.axis_index("y"), lax.axis_index("c")


def _dev_index(px, py, pc):
    return 4 * px + 2 * py + pc


def _cast_shards(w_in_s, w_conv_s, w_rec_s, w_out_s):
    def body(a_ref, c_ref, r_ref, o_ref, oa_ref, ob_ref):
        oa_ref[...] = a_ref[...].astype(BF16)
        for k, ref in enumerate((c_ref, r_ref, o_ref)):
            ob_ref[k] = ref[0].astype(BF16)

    return pl.pallas_call(
        body, name="cast_shards",
        out_shape=(jax.ShapeDtypeStruct(w_in_s.shape, BF16), jax.ShapeDtypeStruct((3, W_ROW_BLK, D), BF16)),
    )(w_in_s, w_conv_s, w_rec_s, w_out_s)


def _peer(x, y, c, r):
    return (jnp.bitwise_xor(x, (r >> 2) & 1), jnp.bitwise_xor(y, (r >> 1) & 1), jnp.bitwise_xor(c, r & 1))


def _gather_small(small_s):
    def body(s_ref, o_ref, send_sems, recv_sems, local_sem):
        x, y, c = _my_place()
        my_id = _dev_index(x, y, c)
        mine = pltpu.make_async_copy(s_ref, o_ref.at[my_id], local_sem)
        mine.start()
        copies = []
        for r in range(1, N_DEV):
            cp = pltpu.make_async_remote_copy(
                src_ref=s_ref, dst_ref=o_ref.at[my_id], send_sem=send_sems.at[r - 1], recv_sem=recv_sems.at[r - 1],
                device_id=_peer(x, y, c, r), device_id_type=MESH_ID)
            cp.start()
            copies.append(cp)
        for cp in copies:
            cp.wait_recv()
        for cp in copies:
            cp.wait_send()
        mine.wait()

    return pl.pallas_call(
        body, name="gather_small", out_shape=jax.ShapeDtypeStruct((N_DEV,) + small_s.shape, F32),
        in_specs=[ANY], out_specs=ANY,
        scratch_shapes=[pltpu.SemaphoreType.DMA((7,)), pltpu.SemaphoreType.DMA((7,)), pltpu.SemaphoreType.DMA],
    )(small_s)


def _w3_gather(src, out, stage, send_sems, recv_sems, local_sems):
    x, y, c = _my_place()
    me, sibling = (x, y, c), (x, y, 1 - c)
    chips = [(1 - x, y), (x, 1 - y), (1 - x, 1 - y)]

    def block(place):
        d = _dev_index(*place)
        return out.at[:, pl.ds(pl.multiple_of(d * W_ROW_BLK, W_ROW_BLK), W_ROW_BLK), :]

    def copy(k, place, to, from_src=False):
        return pltpu.make_async_remote_copy(
            src_ref=src if from_src else block(place), dst_ref=block(place),
            send_sem=send_sems.at[k], recv_sem=recv_sems.at[k], device_id=to, device_id_type=MESH_ID)

    own_in = pltpu.make_async_copy(src, stage, local_sems.at[0])
    own_out = pltpu.make_async_copy(stage, block(me), local_sems.at[1])

    def start():
        copy(0, me, sibling, from_src=True).start()
        for j, chip in enumerate(chips):
            copy(1 + j, me, (*chip, c), from_src=True).start()
        own_in.start()
        own_in.wait()
        own_out.start()

    def finish():
        for j, chip in enumerate(chips):
            copy(1 + j, (*chip, c), me).wait_recv()
            copy(4 + j, (*chip, c), sibling).start()
        copy(0, sibling, me).wait_recv()
        for j, chip in enumerate(chips):
            copy(4 + j, (*chip, 1 - c), me).wait_recv()
        for k in range(7):
            copy(k, me, me).wait_send()
        own_out.wait()

    return start, finish


def _p3_to_sibling(p3_ref, land_ref, send_sems, recv_sems):
    x, y, c = _my_place()

    def cp(q):
        d = _dev_index(*_chip_rel(x, y, q), 1 - c)
        return pltpu.make_async_remote_copy(
            src_ref=p3_ref.at[:, pl.ds(pl.multiple_of(d * W_ROW_BLK, W_ROW_BLK), W_ROW_BLK), :],
            dst_ref=land_ref.at[q], send_sem=send_sems.at[q], recv_sem=recv_sems.at[q],
            device_id=(x, y, 1 - c), device_id_type=MESH_ID)

    def start():
        for q in range(4):
            cp(q).start()

    def finish():
        for q in range(4):
            cp(q).wait_recv()
        for q in range(4):
            cp(q).wait_send()

    return start, finish


def _partials_to_owners(src_ref, far_ref, send_sems, recv_sems):
    x, y, c = _my_place()

    def cp(q):
        return pltpu.make_async_remote_copy(
            src_ref=src_ref.at[q], dst_ref=far_ref.at[q - 1], send_sem=send_sems.at[q - 1],
            recv_sem=recv_sems.at[q - 1], device_id=(*_chip_rel(x, y, q), c), device_id_type=MESH_ID)

    def start():
        for q in range(1, 4):
            cp(q).start()

    def finish():
        for q in range(1, 4):
            cp(q).wait_recv()
        for q in range(1, 4):
            cp(q).wait_send()

    return start, finish


def _gather_chips(x, y, c):
    first = (jnp.bitwise_xor(x, 1 - c), jnp.bitwise_xor(y, c))
    second = (jnp.bitwise_xor(x, c), jnp.bitwise_xor(y, 1 - c))
    return [first, second, (1 - x, 1 - y)]


def _gather_and_proj(x_seq, meta_full, norm_g, w_in_b, order):
    tm = TM_MAT
    n_m = TP // tm
    last_m = n_m - 1

    def body(order_ref, x_ref, meta_ref, g_ref, s0, proj_ref, xin_ref, h_out, o0, hbuf, wbuf, send_sems, recv_sems,
             local_sems):
        del order_ref
        n = pl.program_id(0)
        m = pl.program_id(1)
        x, y, c = _my_place()
        me, sibling = (x, y, c), (x, y, 1 - c)
        chips = _gather_chips(x, y, c)

        def block(place):
            return o0.at[_dev_index(*place)]

        def copy(k, place, to, from_src=False):
            return pltpu.make_async_remote_copy(
                src_ref=s0 if from_src else block(place), dst_ref=block(place),
                send_sem=send_sems.at[k], recv_sem=recv_sems.at[k], device_id=to, device_id_type=MESH_ID)

        def to_vmem(place, slot):
            return pltpu.make_async_copy(block(place), wbuf.at[slot], local_sems.at[slot])

        own_out = pltpu.make_async_copy(wbuf.at[0], block(me), local_sems.at[2])
        h_copy = pltpu.make_async_copy(hbuf, h_out, local_sems.at[3])

        @pl.when((n == 0) & (m == 0))
        def _():
            copy(0, me, sibling, from_src=True).start()
            for j, chip in enumerate(chips[0:2]):
                copy(1 + j, me, (*chip, c), from_src=True).start()
            mine = pltpu.make_async_copy(s0, wbuf.at[0], local_sems.at[0])
            mine.start()
            mine.wait()
            own_out.start()

        @pl.when(n == 0)
        def _():
            xv = x_ref[...]
            xin_ref[...] = jnp.where(m == 0, pltpu.roll(xv, ROW0, 0), xv)

            @pl.when(m == 0)
            def _():
                xin_ref[0:PAD_FRONT, :] = jnp.zeros((PAD_FRONT, D), F32)
                xin_ref[PAD_FRONT:ROW0, :] = meta_ref[...]

            xv = xin_ref[...]
            r = lax.rsqrt(jnp.mean(xv * xv, axis=-1, keepdims=True) + EPS)
            hbuf[m] = (xv * r * g_ref[...]).astype(BF16)

        between = [4 + c, 5 - c, 6]
        first, second, diag = chips
        plan = [(sibling, (0, sibling), None),
                ((*first, c), (1, (*first, c)), between[0]),
                ((*second, 1 - c), (between[1], (*second, 1 - c)), None),
                ((*second, c), (2, (*second, c)), between[1]),
                ((*first, 1 - c), (between[0], (*first, 1 - c)), None),
                ((*diag, c), (3, (*diag, c)), between[2]),
                ((*diag, 1 - c), (between[2], (*diag, 1 - c)), None)]

        for s, (place, (k, origin), pass_on) in enumerate(plan, start=1):
            @pl.when((n == s - 1) & (m == last_m))
            def _(s=s, place=place, k=k, origin=origin, pass_on=pass_on):
                copy(k, origin, me).wait_recv()
                if pass_on is not None:
                    copy(pass_on, place, sibling).start()
                if s == 2:
                    copy(3, place, (*chips[1], c)).start()
                    own_out.wait()
                to_vmem(place, s % 2).start()

            @pl.when((n == s) & (m == 0))
            def _(s=s, place=place):
                to_vmem(place, s % 2).wait()

        proj_ref[...] = _dot(hbuf[m], wbuf[lax.rem(n, 2)]).astype(BF16)

        @pl.when((n == 0) & (m == last_m))
        def _():
            h_copy.start()

        @pl.when((n == N_DEV - 1) & (m == last_m))
        def _():
            for k in range(7):
                copy(k, me, me).wait_send()
            h_copy.wait()

    return pl.pallas_call(
        body, name="gather_and_proj",
        grid_spec=pltpu.PrefetchScalarGridSpec(
            num_scalar_prefetch=1, grid=(N_DEV, n_m),
            in_specs=[pl.BlockSpec((pl.Element(tm), pl.Element(D)),
                                   lambda n, m, o: (_window_start(jnp.where(n == 0, m, 0), tm), 0)),
                      pl.BlockSpec((N_META, D), lambda n, m, o: (0, 0)),
                      pl.BlockSpec((1, D), lambda n, m, o: (0, 0)), ANY],
            out_specs=(pl.BlockSpec((tm, W_IN_BLK), lambda n, m, o: (m, o[n])),
                       pl.BlockSpec((tm, D), lambda n, m, o: (jnp.where(n == 0, m, last_m), 0)), ANY, ANY),
            scratch_shapes=[pltpu.VMEM((n_m, tm, D), BF16), pltpu.VMEM((2, D, W_IN_BLK), BF16),
                            pltpu.SemaphoreType.DMA((7,)), pltpu.SemaphoreType.DMA((7,)),
                            pltpu.SemaphoreType.DMA((4,))]),
        out_shape=(jax.ShapeDtypeStruct((TP, D_IN), BF16), jax.ShapeDtypeStruct((TP, D), F32),
                   jax.ShapeDtypeStruct((n_m, tm, D), BF16), jax.ShapeDtypeStruct((N_DEV, D, W_IN_BLK), BF16)),
        compiler_params=pltpu.CompilerParams(dimension_semantics=("arbitrary", "arbitrary")),
    )(order, x_seq, meta_full, norm_g, w_in_b)


def _chip_rel(x, y, r):
    return (jnp.bitwise_xor(x, r >> 1), jnp.bitwise_xor(y, r & 1))


def _exchange_small(pack, srs):
    def body(pk, sr, pk_all, sr_all, send_sems, recv_sems, local_sems):
        x, y, c = _my_place()
        my_id = _dev_index(x, y, c)
        mine = [pltpu.make_async_copy(pk, pk_all.at[my_id], local_sems.at[0]),
                pltpu.make_async_copy(sr.at[my_id], sr_all.at[my_id], local_sems.at[1])]
        for cp in mine:
            cp.start()
        copies = []
        for r in range(1, N_DEV):
            peer = (jnp.bitwise_xor(x, (r >> 2) & 1), jnp.bitwise_xor(y, (r >> 1) & 1), jnp.bitwise_xor(c, r & 1))
            peer_id = _dev_index(*peer)
            for a, (src, dst) in enumerate(((pk, pk_all.at[my_id]), (sr.at[peer_id], sr_all.at[my_id]))):
                cp = pltpu.make_async_remote_copy(
                    src_ref=src, dst_ref=dst, send_sem=send_sems.at[a * 7 + r - 1], recv_sem=recv_sems.at[a * 7 + r - 1],
                    device_id=peer, device_id_type=MESH_ID)
                cp.start()
                copies.append(cp)
        for cp in copies:
            cp.wait_recv()
        for cp in copies:
            cp.wait_send()
        for cp in mine:
            cp.wait()

    return pl.pallas_call(
        body, name="exchange_small",
        out_shape=(jax.ShapeDtypeStruct((N_DEV,) + pack.shape, F32), jax.ShapeDtypeStruct(srs.shape, F32)),
        in_specs=[ANY, ANY], out_specs=(ANY, ANY),
        scratch_shapes=[pltpu.SemaphoreType.DMA((14,)), pltpu.SemaphoreType.DMA((14,)), pltpu.SemaphoreType.DMA((2,))],
    )(pack, srs)


N_CB = D // HEAD_W


def _store_by_cb(ref, idx, rows, val):
    for cb in range(N_CB):
        ref[(*idx, cb, rows, slice(None))] = val[:, cb * HEAD_W:(cb + 1) * HEAD_W]


def _fill_shifts(sh, tm):
    n = tm + HALO - 8
    for s in range(1, 8):
        for cb in range(N_CB):
            sh[s, cb, 0:n, :] = sh[0, cb, s:s + n, :]


def _conv_fwd(proj, conv_w, conv_b):
    tm = TM_ELT
    strip = tm // CONV_STRIPS

    def body(p_ref, w_ref, b_ref, c0_ref, sh, c0_sc):
        i = pl.program_id(0)

        @pl.when(i == 0)
        def _():
            sh[0, :, 0:HALO, :] = jnp.zeros((N_CB, HALO, HEAD_W), F32)

        @pl.when(i > 0)
        def _():
            sh[0, :, 0:HALO, :] = sh[0, :, tm:tm + HALO, :]

        ga = p_ref[:, 0:D].astype(F32)
        gb = p_ref[:, D:2 * D].astype(F32)
        _store_by_cb(sh, (0,), slice(HALO, HALO + tm), ga * _sigmoid(gb))
        _fill_shifts(sh, tm)
        for cb in range(N_CB):
            cs = slice(cb * HEAD_W, (cb + 1) * HEAD_W)
            for st in range(CONV_STRIPS):
                acc = jnp.broadcast_to(b_ref[:, cs], (strip, HEAD_W))
                for j in range(CONV_K):
                    off = HALO - (CONV_K - 1) + j
                    lo = st * strip + 8 * (off // 8)
                    acc = acc + w_ref[j:j + 1, cs] * sh[off % 8, cb, lo:lo + strip, :]
                c0_sc[st * strip:(st + 1) * strip, cs] = acc
        c0_ref[...] = c0_sc[...].astype(ACT)

    return pl.pallas_call(
        body, name="conv_fwd", grid=(TP // tm,),
        in_specs=[pl.BlockSpec((tm, 2 * D), lambda i: (i, 0)), pl.BlockSpec((CONV_K, D), lambda i: (0, 0)),
                  pl.BlockSpec((1, D), lambda i: (0, 0))],
        out_specs=pl.BlockSpec((tm, D), lambda i: (i, 0)),
        out_shape=jax.ShapeDtypeStruct((TP, D), ACT),
        scratch_shapes=[pltpu.VMEM((8, N_CB, HALO + tm, HEAD_W), F32), pltpu.VMEM((tm, D), F32)],
        compiler_params=pltpu.CompilerParams(dimension_semantics=("arbitrary",)),
    )(proj, conv_w, conv_b)


def _gates(p_ref, lbl_ref, chunk, bsc):
    lb = _sigmoid(lbl_ref[0:1, :] - lbl_ref[1:2, :])
    q_raw = p_ref[:, 0:D].astype(F32)
    f_raw = p_ref[:, D:2 * D].astype(F32)
    sq = _sigmoid(q_raw)
    q = q_raw * sq
    sg = _sigmoid(f_raw)
    f = lb + (1.0 - lb) * sg
    row = lax.broadcasted_iota(jnp.int32, (CHUNK, 1), 0) + chunk * CHUNK
    valid = row >= PAD_FRONT
    lf = jnp.where(valid, jnp.log(f), 0.0)
    k = jnp.where(valid, 1.0 - f, 0.0)
    r_i = lax.broadcasted_iota(jnp.int32, (CHUNK, CHUNK), 0)
    c_i = lax.broadcasted_iota(jnp.int32, (CHUNK, CHUNK), 1)
    causal = r_i >= c_i
    bsc[...] = _tri_matmul(causal.astype(BF16), lf)
    b = bsc[...]
    b_mid = bsc[CHUNK // 2 - 1:CHUNK // 2, :]
    b_last = bsc[CHUNK - 1:CHUNK, :]
    e_q = jnp.exp(b)
    e_qm = jnp.exp(b - b_mid)
    e_km = jnp.exp(b_mid - b)
    e_kh = jnp.exp(b_last - b)
    e_last = jnp.exp(b_last)
    return dict(lb=lb, q_raw=q_raw, sq=sq, q=q, sg=sg, f=f, k=k, valid=valid, causal=causal,
                e_q=e_q, e_qm=e_qm, e_km=e_km, e_kh=e_kh, e_last=e_last)


def _rec_fwd(proj, lb_logits, w3_b):
    cps = CHUNKS_PER_STEP
    rows = cps * CHUNK

    def body(p_ref, lbl_ref, w3s_ref, o_ref, s_ref, w3o_ref, st, bsc, w3buf, send_sems, recv_sems, local_sems):
        n = pl.program_id(0)
        gather_start, gather_finish = _w3_gather(w3s_ref, w3o_ref, w3buf, send_sems, recv_sems, local_sems)

        @pl.when(n == 0)
        def _():
            st[...] = jnp.zeros_like(st)
            gather_start()

        def prep(ci):
            g = _gates(p_ref.at[pl.ds(ci * CHUNK, CHUNK)], lbl_ref, n * cps + ci, bsc.at[ci])
            g["q1"] = (g["q"] * g["e_q"]).astype(BF16)
            g["qm"] = (g["q"] * g["e_qm"]).astype(BF16)
            g["km"] = (g["k"] * g["e_km"]).astype(BF16)
            g["kh"] = (g["k"] * g["e_kh"]).astype(BF16)
            return g

        def heads(ci, g):
            rs = pl.ds(ci * CHUNK, CHUNK)
            pv = p_ref.at[rs]
            s_ref[ci] = st[...]
            for h in range(HEADS):
                sl = slice(h * HEAD_W, (h + 1) * HEAD_W)
                v = pv[:, 2 * D + h * HEAD_W:2 * D + (h + 1) * HEAD_W]
                att = jnp.where(g["causal"], _dot_nt(g["qm"][:, sl], g["km"][:, sl]), 0.0).astype(BF16)
                s_h = st[h]
                o_ref[rs, sl] = (_dot_nt(g["q1"][:, sl], s_h.astype(BF16)) + _dot(att, v)).astype(ACT)
                st[h] = s_h * g["e_last"][:, sl] + _dot_tn(v, g["kh"][:, sl])

        ready = prep(0)
        for ci in range(cps):
            coming = prep(ci + 1) if ci + 1 < cps else None
            heads(ci, ready)
            ready = coming

        @pl.when(n == N_CHUNK // cps - 1)
        def _():
            gather_finish()

    return pl.pallas_call(
        body, name="rec_fwd", grid=(N_CHUNK // cps,),
        in_specs=[pl.BlockSpec((rows, 3 * D), lambda n: (n, 1)), pl.BlockSpec((2, D), lambda n: (0, 0)), ANY],
        out_specs=(pl.BlockSpec((rows, D), lambda n: (n, 0)),
                   pl.BlockSpec((cps, HEADS, HEAD_W, HEAD_W), lambda n: (n, 0, 0, 0)), ANY),
        out_shape=(jax.ShapeDtypeStruct((TP, D), ACT), jax.ShapeDtypeStruct((N_CHUNK, HEADS, HEAD_W, HEAD_W), F32),
                   jax.ShapeDtypeStruct((3, D, D), BF16)),
        scratch_shapes=[pltpu.VMEM((HEADS, HEAD_W, HEAD_W), F32), pltpu.VMEM((cps, CHUNK, D), F32),
                        pltpu.VMEM((3, W_ROW_BLK, D), BF16), pltpu.SemaphoreType.DMA((7,)),
                        pltpu.SemaphoreType.DMA((7,)), pltpu.SemaphoreType.DMA((2,))],
        compiler_params=pltpu.CompilerParams(dimension_semantics=("arbitrary",)),
    )(proj, lb_logits, w3_b)


def _rec_bwd(proj, lb_logits, d_o, s_start, dproj, p3):
    cps = CHUNKS_PER_STEP
    rows = cps * CHUNK
    last = N_CHUNK // cps - 1

    def body(p_ref, lbl_ref, do_ref, s_ref, dproj_in, p3_ref, dp_ref, dlb_ref, land_ref, dst, bsc, dq_sc, dk_sc, g_sc,
             send_sems, recv_sems):
        del dproj_in
        n = pl.program_id(0)
        ride_start, ride_finish = _p3_to_sibling(p3_ref, land_ref, send_sems, recv_sems)

        @pl.when(n == 0)
        def _():
            ride_start()
            dst[...] = jnp.zeros_like(dst)
            dlb_ref[...] = jnp.zeros_like(dlb_ref)

        def prep(ci):
            g = _gates(p_ref.at[pl.ds(ci * CHUNK, CHUNK)], lbl_ref, (last - n) * cps + ci, bsc.at[ci])
            g["q1"] = (g["q"] * g["e_q"]).astype(BF16)
            qm_f = g["q"] * g["e_qm"]
            km_f = g["k"] * g["e_km"]
            g["qm"] = qm_f.astype(BF16)
            g["km"] = km_f.astype(BF16)
            g["qm_lo"] = (qm_f - g["qm"].astype(F32)).astype(BF16)
            g["km_lo"] = (km_f - g["km"].astype(F32)).astype(BF16)
            g["kh_f"] = g["k"] * g["e_kh"]
            g["kh"] = g["kh_f"].astype(BF16)
            return g

        def heads_and_post(ci, g):
            rs = pl.ds(ci * CHUNK, CHUNK)
            pv = p_ref.at[rs]
            dpv = dp_ref.at[rs]
            q1, qm, km, qm_lo, km_lo, kh_f, kh = (g[k] for k in ("q1", "qm", "km", "qm_lo", "km_lo", "kh_f", "kh"))
            for h in range(HEADS):
                sl = slice(h * HEAD_W, (h + 1) * HEAD_W)
                v = pv[:, 2 * D + h * HEAD_W:2 * D + (h + 1) * HEAD_W]
                d_oh = do_ref[rs, sl].astype(BF16)
                s0 = s_ref[ci, h]
                ds_end = dst[h]
                ds_end_b = ds_end.astype(BF16)
                att = jnp.where(g["causal"], _dot_nt(qm[:, sl], km[:, sl]), 0.0).astype(BF16)
                d_att = jnp.where(g["causal"], _dot_nt(d_oh, v), 0.0).astype(BF16)
                d_v = _dot_tn(att, d_oh) + _dot_nt(kh[:, sl], ds_end_b)
                d_qm2 = _dot(d_att, jnp.concatenate([km[:, sl], km_lo[:, sl]], axis=1))
                d_qm = d_qm2[:, 0:HEAD_W] + d_qm2[:, HEAD_W:2 * HEAD_W]
                d_q1 = _dot(d_oh, s0.astype(BF16))
                d_km2 = _dot_tn(d_att, jnp.concatenate([qm[:, sl], qm_lo[:, sl]], axis=1))
                d_km = d_km2[:, 0:HEAD_W] + d_km2[:, HEAD_W:2 * HEAD_W]
                d_kh = _dot(v, ds_end_b)
                dq_sc[ci, :, sl] = d_qm * g["e_qm"][:, sl] + d_q1 * g["e_q"][:, sl]
                dk_sc[ci, :, sl] = d_km * g["e_km"][:, sl] + d_kh * g["e_kh"][:, sl]
                g_sc[ci, :, sl] = (jnp.sum(kh_f[:, sl] * d_kh, axis=0, keepdims=True)
                                   + g["e_last"][:, sl] * jnp.sum(ds_end * s0, axis=0, keepdims=True))
                dst[h] = ds_end * g["e_last"][:, sl] + _dot_tn(d_oh, q1[:, sl])
                dpv[:, 2 * D + h * HEAD_W:2 * D + (h + 1) * HEAD_W] = d_v.astype(BF16)
            d_q = dq_sc[ci]
            d_k = dk_sc[ci]
            d_b = g["q"] * d_q - g["k"] * d_k
            anti = jnp.logical_not(g["causal"]) | (lax.broadcasted_iota(jnp.int32, (CHUNK, CHUNK), 0)
                                                    == lax.broadcasted_iota(jnp.int32, (CHUNK, CHUNK), 1))
            d_lf = _tri_matmul(anti.astype(BF16), d_b) + g_sc[ci]
            d_f = jnp.where(g["valid"], d_lf / g["f"] - d_k, 0.0)
            sg = g["sg"]
            dlb_ref[0:1, :] += jnp.sum(d_f * (1.0 - sg), axis=0, keepdims=True)
            dpv[:, 0:D] = (d_q * _dsilu(g["q_raw"], g["sq"])).astype(BF16)
            dpv[:, D:2 * D] = (d_f * (1.0 - g["lb"]) * sg * (1.0 - sg)).astype(BF16)

        ready = prep(cps - 1)
        for ci in reversed(range(cps)):
            coming = prep(ci - 1) if ci > 0 else None
            heads_and_post(ci, ready)
            ready = coming

        @pl.when(n == last)
        def _():
            ride_finish()

    return pl.pallas_call(
        body, name="rec_bwd", grid=(N_CHUNK // cps,),
        in_specs=[pl.BlockSpec((rows, 3 * D), lambda n: (last - n, 1)), pl.BlockSpec((2, D), lambda n: (0, 0)),
                  pl.BlockSpec((rows, D), lambda n: (last - n, 0)),
                  pl.BlockSpec((cps, HEADS, HEAD_W, HEAD_W), lambda n: (last - n, 0, 0, 0)), ANY, ANY],
        out_specs=(pl.BlockSpec((rows, 3 * D), lambda n: (last - n, 1)), pl.BlockSpec((8, D), lambda n: (0, 0)), ANY),
        out_shape=(jax.ShapeDtypeStruct((TP, D_IN), BF16), jax.ShapeDtypeStruct((8, D), F32),
                   jax.ShapeDtypeStruct((4, 3, W_ROW_BLK, D), F32)),
        scratch_shapes=[pltpu.VMEM((HEADS, HEAD_W, HEAD_W), F32), pltpu.VMEM((cps, CHUNK, D), F32),
                        pltpu.VMEM((cps, CHUNK, D), F32), pltpu.VMEM((cps, CHUNK, D), F32),
                        pltpu.VMEM((cps, 1, D), F32), pltpu.SemaphoreType.DMA((4,)), pltpu.SemaphoreType.DMA((4,))],
        input_output_aliases={4: 0},
        compiler_params=pltpu.CompilerParams(dimension_semantics=("arbitrary",)),
    )(proj, lb_logits, d_o, s_start, dproj, p3)


def _mid(xin, tgt, o, c0, proj, w3, ln_g, ln_b, gnorm_g, final_g):
    tm = TM_ELT

    def body(x_ref, t_ref, o_ref, c0_ref, z_ref, gr_ref, mc_ref, mr_ref, w_ref, lng_ref, lnb_ref, gng_ref, fg_ref,
             do_ref, dc0_ref, dz_ref, dp_ref, a3_ref, b3_ref, red_ref, on_sc, don_sc):
        i = pl.program_id(0)

        @pl.when(i == 0)
        def _():
            red_ref[...] = jnp.zeros_like(red_ref)

        w_conv, w_rec, w_out = w_ref[0], w_ref[1], w_ref[2]
        c0v = c0_ref[...].astype(F32)
        mu = jnp.mean(c0v, axis=-1, keepdims=True)
        xc = c0v - mu
        rstd = lax.rsqrt(jnp.mean(xc * xc, axis=-1, keepdims=True) + EPS)
        xh = xc * rstd
        c1 = xh * lng_ref[...] + lnb_ref[...]
        s1 = _sigmoid(c1)
        c2 = c1 * s1
        z = z_ref[...].astype(F32)
        sz = _sigmoid(z)
        silu_z = z * sz
        u_conv = (c2 * silu_z).astype(BF16)
        y_conv = _dot(u_conv, w_conv)
        ov = o_ref[...].astype(F32)
        r3 = []
        for h in range(HEADS):
            sl = slice(h * HEAD_W, (h + 1) * HEAD_W)
            oh = ov[:, sl]
            r_h = lax.rsqrt(jnp.mean(oh * oh, axis=-1, keepdims=True) + EPS)
            r3.append(r_h)
            on_sc[:, sl] = oh * r_h
        o_n = on_sc[...]
        o_g = o_n * gng_ref[...]
        gr = gr_ref[...].astype(F32)
        sgr = _sigmoid(gr)
        silu_g = gr * sgr
        u_rec = (o_g * silu_g).astype(BF16)
        y_rec = _dot(u_rec, w_rec)
        mc = mc_ref[...].astype(F32)
        mr = mr_ref[...].astype(F32)
        smc = _sigmoid(mc)
        smr = _sigmoid(mr)
        merged = (smc * y_conv + smr * y_rec).astype(BF16)
        res = x_ref[...] + _dot(merged, w_out)
        r2 = lax.rsqrt(jnp.mean(res * res, axis=-1, keepdims=True) + EPS)
        xh2 = res * r2
        row = lax.broadcasted_iota(jnp.int32, (tm, 1), 0) + i * tm
        real = row >= ROW0
        tgt = t_ref[...]
        tgt = jnp.where(i == 0, pltpu.roll(tgt, ROW0, 0), tgt)
        diff = jnp.where(real, xh2 * fg_ref[...] - tgt, 0.0)
        d_y = diff * (1.0 / D)
        d_xh2 = d_y * fg_ref[...]
        d_res = r2 * (d_xh2 - xh2 * jnp.mean(d_xh2 * xh2, axis=-1, keepdims=True))
        d_res_b = d_res.astype(BF16)
        d_merged = _dot_nt(d_res_b, w_out)
        d_yc = (d_merged * smc).astype(BF16)
        d_yr = (d_merged * smr).astype(BF16)
        dp_ref[:, D:2 * D] = (d_merged * y_conv * smc * (1.0 - smc)).astype(BF16)
        dp_ref[:, 2 * D:3 * D] = (d_merged * y_rec * smr * (1.0 - smr)).astype(BF16)
        d_ur = _dot_nt(d_yr, w_rec)
        d_og = d_ur * silu_g
        dp_ref[:, 0:D] = (d_ur * o_g * _dsilu(gr, sgr)).astype(BF16)
        d_on = d_og * gng_ref[...]
        for h in range(HEADS):
            sl = slice(h * HEAD_W, (h + 1) * HEAD_W)
            d_h = d_on[:, sl]
            n_h = o_n[:, sl]
            don_sc[:, sl] = r3[h] * (d_h - n_h * jnp.mean(d_h * n_h, axis=-1, keepdims=True))
        do_ref[...] = don_sc[...].astype(ACT)
        d_uc = _dot_nt(d_yc, w_conv)
        d_c2 = d_uc * silu_z
        dz_ref[...] = (d_uc * c2 * _dsilu(z, sz)).astype(BF16)
        d_c1 = d_c2 * _dsilu(c1, s1)
        d_xh = d_c1 * lng_ref[...]
        d_c0 = rstd * (d_xh - jnp.mean(d_xh, axis=-1, keepdims=True)
                       - xh * jnp.mean(d_xh * xh, axis=-1, keepdims=True))
        dc0_ref[...] = d_c0.astype(ACT)
        a3_ref[0] = u_conv
        b3_ref[0] = d_yc
        a3_ref[1] = u_rec
        b3_ref[1] = d_yr
        a3_ref[2] = merged
        b3_ref[2] = d_res_b
        def colsum(vv):
            return jnp.sum(vv, axis=0, keepdims=True)

        red_ref[0:1, :] += colsum(d_y * xh2)
        red_ref[1:2, :] += colsum(d_og * o_n)
        red_ref[2:3, :] += colsum(d_c1 * xh)
        red_ref[3:4, :] += colsum(d_c1)
        red_ref[4:5, :] += colsum(d_c0)
        red_ref[5:6, :] += colsum(diff * diff) * (0.5 / D)

    def row_block(width, col):
        return pl.BlockSpec((tm, width), lambda i: (i, col))

    def const_block(shape):
        return pl.BlockSpec(shape, lambda i: (0,) * len(shape))

    stack = jax.ShapeDtypeStruct((3, TP, D), BF16)
    stack_spec = pl.BlockSpec((3, tm, D), lambda i: (0, i, 0))
    return pl.pallas_call(
        body, name="mid", grid=(TP // tm,),
        in_specs=[row_block(D, 0),
                  pl.BlockSpec((pl.Element(tm), pl.Element(D)), lambda i: (_window_start(i, tm), 0)),
                  row_block(D, 0), row_block(D, 0),
                  row_block(D, 2), row_block(D, 6), row_block(D, 7), row_block(D, 8),
                  pl.BlockSpec((3, D, D), lambda i: (0, 0, 0), pipeline_mode=pl.Buffered(1)),
                  const_block((1, D)), const_block((1, D)), const_block((1, D)), const_block((1, D))],
        out_specs=(row_block(D, 0), row_block(D, 0), row_block(D, 0), row_block(3 * D, 2),
                   stack_spec, stack_spec, const_block((8, D))),
        out_shape=(jax.ShapeDtypeStruct((TP, D), ACT), jax.ShapeDtypeStruct((TP, D), ACT),
                   jax.ShapeDtypeStruct((TP, D), BF16),
                   jax.ShapeDtypeStruct((TP, D_IN), BF16), stack, stack, jax.ShapeDtypeStruct((8, D), F32)),
        scratch_shapes=[pltpu.VMEM((tm, D), F32), pltpu.VMEM((tm, D), F32)],
        compiler_params=pltpu.CompilerParams(dimension_semantics=("arbitrary",), vmem_limit_bytes=60 * 1024 * 1024),
    )(xin, tgt, o, c0, proj, proj, proj, proj, w3, ln_g, ln_b, gnorm_g, final_g)


def _conv_bwd(proj, d_c0, d_z, conv_w, dproj, chip1b):
    tm = TM_ELT
    n_tile = TP // tm
    lastt = n_tile - 1

    strip = tm // CONV_STRIPS

    def body(p_ref, dc_ref, dz_ref, w_ref, dproj_in, c1_ref, dp_ref, dw_ref, far_ref, dsh, a_sc, da_sc, acc,
             send_sems, recv_sems):
        del dproj_in
        i = pl.program_id(0)
        ride_start, ride_finish = _partials_to_owners(c1_ref, far_ref, send_sems, recv_sems)

        @pl.when(i == 0)
        def _():
            ride_start()
            dsh[0, :, tm:tm + HALO, :] = jnp.zeros((N_CB, HALO, HEAD_W), F32)
            acc[...] = jnp.zeros_like(acc)

        @pl.when(i > 0)
        def _():
            dsh[0, :, tm:tm + HALO, :] = dsh[0, :, 0:HALO, :]

        _store_by_cb(dsh, (0,), slice(0, tm), dc_ref[...].astype(F32))
        _fill_shifts(dsh, tm)
        ga = p_ref[:, 0:D].astype(F32)
        sb = _sigmoid(p_ref[:, D:2 * D].astype(F32))
        a = ga * sb
        _store_by_cb(a_sc, (), slice(0, tm), a)
        for cb in range(N_CB):
            cs = slice(cb * HEAD_W, (cb + 1) * HEAD_W)
            for st in range(CONV_STRIPS):
                rows = slice(st * strip, (st + 1) * strip)
                a_s = a_sc[cb, rows, :]
                d_a = jnp.zeros((strip, HEAD_W), F32)
                for j in range(CONV_K):
                    off = CONV_K - 1 - j
                    lo = st * strip + 8 * (off // 8)
                    slab = dsh[off % 8, cb, lo:lo + strip, :]
                    d_a = d_a + w_ref[j:j + 1, cs] * slab
                    acc[j, :, cs] += jnp.sum((a_s * slab).reshape(strip // 8, 8, HEAD_W), axis=0)
                da_sc[rows, cs] = d_a
        d_a = da_sc[...]
        dp_ref[:, 0:D] = (d_a * sb).astype(BF16)
        dp_ref[:, D:2 * D] = (d_a * a * (1.0 - sb)).astype(BF16)
        dp_ref[:, 2 * D:3 * D] = dz_ref[...]

        @pl.when(i == lastt)
        def _():
            for j in range(CONV_K):
                dw_ref[j:j + 1, :] = jnp.sum(acc[j], axis=0, keepdims=True)
            dw_ref[CONV_K:CONV_K + 1, :] = jnp.zeros((1, D), F32)
            ride_finish()

    return pl.pallas_call(
        body, name="conv_bwd", grid=(n_tile,),
        in_specs=[pl.BlockSpec((tm, 2 * D), lambda i: (lastt - i, 0)), pl.BlockSpec((tm, D), lambda i: (lastt - i, 0)),
                  pl.BlockSpec((tm, D), lambda i: (lastt - i, 0)), pl.BlockSpec((CONV_K, D), lambda i: (0, 0)), ANY, ANY],
        out_specs=(pl.BlockSpec((tm, 3 * D), lambda i: (lastt - i, 0)), pl.BlockSpec((CONV_K + 1, D), lambda i: (0, 0)),
                   ANY),
        out_shape=(jax.ShapeDtypeStruct((TP, D_IN), BF16), jax.ShapeDtypeStruct((CONV_K + 1, D), F32),
                   jax.ShapeDtypeStruct((3, 3, W_ROW_BLK, D), BF16)),
        scratch_shapes=[pltpu.VMEM((8, N_CB, tm + HALO, HEAD_W), F32), pltpu.VMEM((N_CB, tm, HEAD_W), F32),
                        pltpu.VMEM((tm, D), F32), pltpu.VMEM((CONV_K, 8, D), F32),
                        pltpu.SemaphoreType.DMA((3,)), pltpu.SemaphoreType.DMA((3,))],
        input_output_aliases={4: 0},
        compiler_params=pltpu.CompilerParams(dimension_semantics=("arbitrary",)),
    )(proj, d_c0, d_z, conv_w, dproj, chip1b)


def _wgrad3(a3, b3):
    tt = TT_WGRAD

    def body(a_ref, b_ref, o_ref):
        @pl.when(pl.program_id(1) == 0)
        def _():
            o_ref[...] = jnp.zeros_like(o_ref)

        o_ref[0] += _dot_tn(a_ref[0], b_ref[0])

    return pl.pallas_call(
        body, name="wgrad3", grid=(3, TP // tt),
        in_specs=[pl.BlockSpec((1, tt, D), lambda g, t: (g, t, 0)), pl.BlockSpec((1, tt, D), lambda g, t: (g, t, 0))],
        out_specs=pl.BlockSpec((1, D, D), lambda g, t: (g, 0, 0)),
        out_shape=jax.ShapeDtypeStruct((3, D, D), F32),
        compiler_params=pltpu.CompilerParams(dimension_semantics=("arbitrary", "arbitrary")),
    )(a3, b3)


def _wgrad_in(h, dproj, ids):
    tt = TT_WGRAD
    n_t = TP // tt

    def body(ids_ref, a_ref, b_ref, o_ref, ob_ref, l0_ref, acc, tmp, send_sems, recv_sems, tmp_sem):
        del ids_ref
        r = pl.program_id(0)
        t = pl.program_id(1)
        x, y, c = _my_place()
        sibling = (x, y, 1 - c)
        slot = lax.rem(r, 2)

        def send_in(q):
            return pltpu.make_async_remote_copy(
                src_ref=acc.at[q % 2], dst_ref=l0_ref.at[q], send_sem=send_sems.at[q], recv_sem=recv_sems.at[q],
                device_id=sibling, device_id_type=MESH_ID)

        def landed(q):
            return pltpu.make_async_copy(l0_ref.at[q], tmp, tmp_sem)

        @pl.when(t == 0)
        def _():
            acc[slot] = jnp.zeros((D, W_IN_BLK), F32)

        acc[slot] += _dot_tn(a_ref[...], b_ref[...])

        for q in range(4):
            @pl.when((r == q) & (t == n_t - 1))
            def _(q=q):
                if q >= 1:
                    send_in(q - 1).wait_send()
                send_in(q).start()

            @pl.when((r == 4 + q) & (t == n_t - 2))
            def _(q=q):
                if q == 0:
                    send_in(3).wait_send()
                send_in(q).wait_recv()
                landed(q).start()

            @pl.when((r == 4 + q) & (t == n_t - 1))
            def _(q=q):
                landed(q).wait()
                tot = acc[q % 2] + tmp[...]
                o_ref[0] = tot
                ob_ref[0] = tot.astype(BF16)

    blk = pl.BlockSpec((1, D, W_IN_BLK), lambda r, t, ids: (jnp.maximum(r - 4, 0), 0, 0))
    return pl.pallas_call(
        body, name="wgrad_in",
        grid_spec=pltpu.PrefetchScalarGridSpec(
            num_scalar_prefetch=1, grid=(N_DEV, n_t),
            in_specs=[pl.BlockSpec((tt, D), lambda r, t, ids: (t, 0)),
                      pl.BlockSpec((tt, W_IN_BLK), lambda r, t, ids: (t, ids[r]))],
            out_specs=(blk, blk, ANY),
            scratch_shapes=[pltpu.VMEM((2, D, W_IN_BLK), F32), pltpu.VMEM((D, W_IN_BLK), F32),
                            pltpu.SemaphoreType.DMA((4,)), pltpu.SemaphoreType.DMA((4,)), pltpu.SemaphoreType.DMA]),
        out_shape=(jax.ShapeDtypeStruct((4, D, W_IN_BLK), F32), jax.ShapeDtypeStruct((4, D, W_IN_BLK), BF16),
                   jax.ShapeDtypeStruct((4, D, W_IN_BLK), F32)),
        compiler_params=pltpu.CompilerParams(dimension_semantics=("arbitrary", "arbitrary")),
    )(ids, h, dproj)


def _chip_sum_3(p3, land1, ids_mine):
    def body(ids_ref, p_ref, l_ref, o_ref, ob_ref):
        del ids_ref
        tot = p_ref[...] + l_ref[0]
        o_ref[0] = tot
        ob_ref[0] = tot.astype(BF16)

    blk = pl.BlockSpec((1, 3, W_ROW_BLK, D), lambda r, ids: (r, 0, 0, 0))
    return pl.pallas_call(
        body, name="chip_sum_3",
        grid_spec=pltpu.PrefetchScalarGridSpec(
            num_scalar_prefetch=1, grid=(4,),
            in_specs=[pl.BlockSpec((3, W_ROW_BLK, D), lambda r, ids: (0, ids[r], 0)), blk],
            out_specs=(blk, blk)),
        out_shape=(jax.ShapeDtypeStruct((4, 3, W_ROW_BLK, D), F32), jax.ShapeDtypeStruct((4, 3, W_ROW_BLK, D), BF16)),
    )(ids_mine, p3, land1)


def _dh_and_norm_bwd(dproj, w_in_full, xin, b3, norm_g, chip0b):
    tm = TM_MAT
    n_k = N_DEV // DH_K_BLKS
    n_m = TP // tm

    def body(dp_ref, w_ref, x_ref, dr_ref, g_ref, c0_ref, dx_ref, dg_ref, f0_ref, acc, send_sems, recv_sems):
        m = pl.program_id(0)
        k = pl.program_id(1)
        ride_start, ride_finish = _partials_to_owners(c0_ref, f0_ref, send_sems, recv_sems)

        @pl.when((m == 0) & (k == 0))
        def _():
            ride_start()

        @pl.when(k == 0)
        def _():
            acc[...] = jnp.zeros_like(acc)

        part = _dot_nt(dp_ref[:, 0:W_IN_BLK], w_ref[0])
        for j in range(1, DH_K_BLKS):
            part = part + _dot_nt(dp_ref[:, j * W_IN_BLK:(j + 1) * W_IN_BLK], w_ref[j])
        acc[...] += part

        @pl.when((k == n_k - 1) & (m == 0))
        def _():
            dg_ref[...] = jnp.zeros_like(dg_ref)

        @pl.when(k == n_k - 1)
        def _():
            xv = x_ref[...]
            r1 = lax.rsqrt(jnp.mean(xv * xv, axis=-1, keepdims=True) + EPS)
            xh = xv * r1
            d_h = acc[...]
            dg_ref[0:1, :] += jnp.sum(d_h * xh, axis=0, keepdims=True)
            d_xh = d_h * g_ref[...]
            dx_ref[...] = dr_ref[0].astype(F32) + r1 * (d_xh - xh * jnp.mean(d_xh * xh, axis=-1, keepdims=True))

        @pl.when((m == n_m - 1) & (k == n_k - 1))
        def _():
            ride_finish()

    return pl.pallas_call(
        body, name="dh_norm_bwd", grid=(n_m, n_k),
        in_specs=[pl.BlockSpec((tm, DH_K_BLKS * W_IN_BLK), lambda m, k: (m, k)),
                  pl.BlockSpec((DH_K_BLKS, D, W_IN_BLK), lambda m, k: (k, 0, 0)),
                  pl.BlockSpec((tm, D), lambda m, k: (m, 0)), pl.BlockSpec((1, tm, D), lambda m, k: (2, m, 0)),
                  pl.BlockSpec((1, D), lambda m, k: (0, 0)), ANY],
        out_specs=(pl.BlockSpec((tm, D), lambda m, k: (m, 0)), pl.BlockSpec((8, D), lambda m, k: (0, 0)), ANY),
        out_shape=(jax.ShapeDtypeStruct((TP, D), F32), jax.ShapeDtypeStruct((8, D), F32),
                   jax.ShapeDtypeStruct((3, D, W_IN_BLK), BF16)),
        scratch_shapes=[pltpu.VMEM((tm, D), F32), pltpu.SemaphoreType.DMA((3,)), pltpu.SemaphoreType.DMA((3,))],
        compiler_params=pltpu.CompilerParams(dimension_semantics=("arbitrary", "arbitrary")),
    )(dproj, w_in_full, xin, b3, norm_g, chip0b)


def _sum_adamw(own, landed, w, m, v, tr, name):
    rows, cols = w.shape
    n_t = rows // tr

    def body(o_ref, l1_ref, l2_ref, l3_ref, w_ref, m_ref, v_ref, g_ref, d_ref, m2_ref, v2_ref):
        g = ((o_ref[...] + l1_ref[...].astype(F32)) + l2_ref[...].astype(F32)) + l3_ref[...].astype(F32)
        delta, m2, v2 = _adamw(w_ref[...], g, m_ref[...], v_ref[...])
        g_ref[...] = g
        d_ref[...] = delta
        m2_ref[...] = m2
        v2_ref[...] = v2

    def spec(k):
        return pl.BlockSpec((tr, cols), lambda i: (i + k * n_t, 0))

    out = jax.ShapeDtypeStruct((rows, cols), F32)
    return pl.pallas_call(
        body, name=name, grid=(n_t,),
        in_specs=[spec(0), spec(0), spec(1), spec(2), spec(0), spec(0), spec(0)],
        out_specs=(spec(0),) * 4, out_shape=(out,) * 4,
    )(own, landed, landed, landed, w, m, v)


def _adamw_3(chip1, far1, ws, ms, vs):
    def body(c_ref, f_ref, *refs):
        w_refs, m_refs, v_refs, outs = refs[0:3], refs[3:6], refs[6:9], refs[9:21]
        for k in range(3):
            g = ((c_ref[0, k] + f_ref[0, k].astype(F32)) + f_ref[1, k].astype(F32)) + f_ref[2, k].astype(F32)
            delta, m2, v2 = _adamw(w_refs[k][0], g, m_refs[k][0], v_refs[k][0])
            for kind, val in enumerate((g, delta, m2, v2)):
                outs[3 * kind + k][0] = val

    full = pl.BlockSpec((1, W_ROW_BLK, D), lambda i: (0, 0, 0))
    out = jax.ShapeDtypeStruct((1, W_ROW_BLK, D), F32)
    res = pl.pallas_call(
        body, name="adamw_3", grid=(1,),
        in_specs=[pl.BlockSpec((1, 3, W_ROW_BLK, D), lambda i: (0, 0, 0, 0)),
                  pl.BlockSpec((3, 3, W_ROW_BLK, D), lambda i: (0, 0, 0, 0))] + [full] * 9,
        out_specs=(full,) * 12, out_shape=(out,) * 12,
    )(chip1, far1, *ws, *ms, *vs)
    return tuple(res[3 * kind:3 * kind + 3] for kind in range(4))


N_SMALL = 9


def _small_update(pack_all, srs_all, ws, ms, vs):
    def body(pk_ref, sr_ref, *refs):
        w_refs, m_refs, v_refs = refs[0:N_SMALL], refs[N_SMALL:2 * N_SMALL], refs[2 * N_SMALL:3 * N_SMALL]
        loss_ref = refs[3 * N_SMALL]
        outs = refs[3 * N_SMALL + 1:7 * N_SMALL + 1]
        tot_sc, tots_sc = refs[7 * N_SMALL + 1:]
        tot = pk_ref[0]
        tot_s = sr_ref[0]
        for d in range(1, N_DEV):
            tot = tot + pk_ref[d]
            tot_s = tot_s + sr_ref[d]
        tot_sc[...] = tot
        tots_sc[...] = tot_s
        loss_ref[...] = jnp.sum(tot_sc[5:6, :], axis=1, keepdims=True)
        lbl = w_refs[4]
        p0 = _sigmoid(lbl[0:1, :] - lbl[1:2, :])
        d_l0 = tot_sc[4:5, :] * p0 * (1.0 - p0)

        def update(k, sel, g):
            delta, m2, v2 = _adamw(w_refs[k][sel], g, m_refs[k][sel], v_refs[k][sel])
            for kind, val in enumerate((g, delta, m2, v2)):
                outs[N_SMALL * kind + k][sel] = val

        everything = (slice(None), slice(None))
        for k, row in ((0, 0), (1, 1), (2, 2), (3, 3), (5, 6), (6, 7)):
            update(k, everything, tot_sc[row:row + 1, :])
        update(4, (slice(0, 1), slice(None)), d_l0)
        update(4, (slice(1, 2), slice(None)), -d_l0)
        update(7, (0, slice(None), slice(None)), tots_sc[0:CONV_K, :])
        update(8, everything, tots_sc[META_ROW:META_ROW + N_META, :])

    shapes = [jax.ShapeDtypeStruct(w.shape, F32) for w in ws]
    res = pl.pallas_call(
        body, name="small_update",
        out_shape=(jax.ShapeDtypeStruct((1, 1), F32), *(shapes * 4)),
        scratch_shapes=[pltpu.VMEM((8, D), F32), pltpu.VMEM((SMALL_ROWS, HEAD_W), F32)],
    )(pack_all, srs_all, *ws, *ms, *vs)
    return res[0], tuple(res[1 + N_SMALL * kind:1 + N_SMALL * (kind + 1)] for kind in range(4))


def _local_step(xin, proj, target, conv_w_full, conv_b, ln_g, ln_b, w3_b, lb_logits, gnorm_g, final_g, ids_mine):
    fg = final_g.reshape(1, D)
    c0 = _conv_fwd(proj, conv_w_full, conv_b)
    o, s_start, w3_full = _rec_fwd(proj, lb_logits, w3_b)
    d_o, d_c0, d_z, dproj, a3, b3, red = _mid(xin, target, o, c0, proj, w3_full, ln_g, ln_b, gnorm_g, fg)
    p3 = _wgrad3(a3, b3)
    dproj, dlb, land1 = _rec_bwd(proj, lb_logits, d_o, s_start, dproj, p3)
    chip1, chip1b = _chip_sum_3(p3, land1, ids_mine)
    dproj, d_conv_w, far1 = _conv_bwd(proj, d_c0, d_z, conv_w_full, dproj, chip1b)
    return dproj, b3, p3, chip1, far1, d_conv_w, red, dlb


def kernel(x, meta_tokens, norm_g, w_in, conv_w, conv_b, ln_g, ln_b, w_conv_out, lb_logits, gnorm_g, w_rec_out, w_out, final_g, loss_target, m_meta_tokens, m_norm_g, m_w_in, m_conv_w, m_conv_b, m_ln_g, m_ln_b, m_w_conv_out, m_lb_logits, m_gnorm_g, m_w_rec_out, m_w_out, m_final_g, v_meta_tokens, v_norm_g, v_w_in, v_conv_w, v_conv_b, v_ln_g, v_ln_b, v_w_conv_out, v_lb_logits, v_gnorm_g, v_w_rec_out, v_w_out, v_final_g):
    mx, my, mc = _my_place()

    ws_s = jnp.concatenate([conv_w[0], jnp.zeros((1, HEAD_W), F32), meta_tokens], axis=0)
    small_full = jnp.transpose(_gather_small(ws_s), (1, 0, 2)).reshape(SMALL_ROWS, D)
    conv_w_full = small_full[0:CONV_K]
    meta_full = small_full[META_ROW:META_ROW + N_META]
    w_in_b, w3_b = _cast_shards(w_in[0], w_conv_out, w_rec_out, w_out)
    first, second, diag = _gather_chips(mx, my, mc)
    use_order = [(mx, my, mc), (mx, my, 1 - mc), (*first, mc), (*second, 1 - mc), (*second, mc), (*first, 1 - mc),
                 (*diag, mc), (*diag, 1 - mc)]
    order = jnp.stack([_dev_index(*p) for p in use_order]).astype(jnp.int32)
    proj, xin, h, w_in_full = _gather_and_proj(x[0], meta_full, norm_g, w_in_b, order)
    h = h.reshape(TP, D)

    ids_mine = jnp.stack([_dev_index(*_chip_rel(mx, my, r), mc) for r in range(4)]).astype(jnp.int32)
    ids_sib = jnp.stack([_dev_index(*_chip_rel(mx, my, r), 1 - mc) for r in range(4)]).astype(jnp.int32)
    dproj, b3, _, chip1, far1, d_conv_w, red, dlb = _local_step(
        xin, proj, loss_target[0], conv_w_full, conv_b, ln_g, ln_b, w3_b, lb_logits, gnorm_g, final_g, ids_mine)

    chip0, chip0b, _ = _wgrad_in(h, dproj, jnp.concatenate([ids_sib, ids_mine]))
    d_xin, dng, far0 = _dh_and_norm_bwd(dproj, w_in_full, xin, b3, norm_g, chip0b)
    pack = jnp.concatenate([dng[0:1], red[4:5], red[2:3], red[3:4], dlb[0:1], red[5:6], red[1:2], red[0:1]], axis=0)
    g_in, d_in, m_in, v_in = _sum_adamw(chip0.reshape(4 * D, W_IN_BLK), far0.reshape(3 * D, W_IN_BLK), w_in[0],
                                        m_w_in[0], v_w_in[0], 256, "adamw_in")
    big3 = _adamw_3(chip1, far1, (w_conv_out, w_rec_out, w_out), (m_w_conv_out, m_w_rec_out, m_w_out),
                    (v_w_conv_out, v_w_rec_out, v_w_out))

    srs = jnp.concatenate([d_conv_w, d_xin[PAD_FRONT:ROW0]], axis=0)
    srs = jnp.transpose(srs.reshape(SMALL_ROWS, N_DEV, HEAD_W), (1, 0, 2))
    pack_all, srs_all = _exchange_small(pack, srs)
    loss, small = _small_update(
        pack_all, srs_all,
        (norm_g, conv_b, ln_g, ln_b, lb_logits, gnorm_g, final_g.reshape(1, D), conv_w, meta_tokens),
        (m_norm_g, m_conv_b, m_ln_g, m_ln_b, m_lb_logits, m_gnorm_g, m_final_g.reshape(1, D), m_conv_w, m_meta_tokens),
        (v_norm_g, v_conv_b, v_ln_g, v_ln_b, v_lb_logits, v_gnorm_g, v_final_g.reshape(1, D), v_conv_w, v_meta_tokens))

    outs = [loss.reshape(()), d_xin[ROW0:][None]]
    for kind, a_in in enumerate((g_in, d_in, m_in, v_in)):
        ng, cb, lg, lb_, lbl, gg, fg, cw, mt = small[kind]
        a_3 = big3[kind]
        outs += [mt, ng, a_in[None], cw, cb, lg, lb_, a_3[0], lbl, gg, a_3[1], a_3[2], fg.reshape(D)]
    return tuple(outs)
```

```python
import functools

import jax
import jax.numpy as jnp
from jax import lax
from jax.experimental import pallas as pl
from jax.experimental.pallas import tpu as pltpu

F32 = jnp.float32
BF16 = jnp.bfloat16
ACT = BF16

D = 1024
SEQ = 4096
N_META = 16
CHUNK = 64
PAD_FRONT = 48
ROW0 = PAD_FRONT + N_META
TP = ROW0 + SEQ
N_CHUNK = TP // CHUNK
HEADS = 8
HEAD_W = 128
D_IN = 9 * D
N_DEV = 8
W_IN_BLK = D_IN // N_DEV
W_ROW_BLK = D // N_DEV
CONV_K = 31
SMALL_ROWS = 48
META_ROW = 32
EPS = 1e-6
HALO = 32

TM_MAT = 832
TT_WGRAD = 2080
DH_K_BLKS = 2
TM_ELT = 208
CHUNKS_PER_STEP = 5
CONV_STRIPS = 2

ADAM_LR = 0.001
ADAM_B1 = 0.9
ADAM_B2 = 0.999
ADAM_EPS = 1e-08
ADAM_WD = 0.01
ADAM_STEP = 10

MESH_ID = pl.DeviceIdType.MESH
ANY = pl.BlockSpec(memory_space=pl.ANY)


def _sigmoid(v):
    return jax.nn.sigmoid(v)


def _dsilu(v, s):
    return s * (1.0 + v * (1.0 - s))


def _dot(a, b):
    return jnp.dot(a, b, preferred_element_type=F32)


def _dot_nt(a, b):
    return lax.dot_general(a, b, (((1,), (1,)), ((), ())), preferred_element_type=F32)


def _dot_tn(a, b):
    return lax.dot_general(a, b, (((0,), (0,)), ((), ())), preferred_element_type=F32)


def _split3(v):
    hi = v.astype(BF16)
    r1 = v - hi.astype(F32)
    mid = r1.astype(BF16)
    lo = (r1 - mid.astype(F32)).astype(BF16)
    return hi, mid, lo


def _tri_matmul(tri, v):
    hi, mid, lo = _split3(v)
    return _dot(tri, hi) + _dot(tri, mid) + _dot(tri, lo)


def _adamw(w, g, m, v):
    m2 = ADAM_B1 * m + (1.0 - ADAM_B1) * g
    v2 = ADAM_B2 * v + (1.0 - ADAM_B2) * jnp.square(g)
    m_hat = m2 / (1.0 - ADAM_B1 ** ADAM_STEP)
    v_hat = v2 / (1.0 - ADAM_B2 ** ADAM_STEP)
    delta = -ADAM_LR * (m_hat / (jnp.sqrt(v_hat) + ADAM_EPS) + ADAM_WD * w)
    return delta, m2, v2


def _window_start(i, tm):
    assert tm % 16 == 0 and ROW0 % 16 == 0
    return pl.multiple_of(16 * jnp.maximum((tm // 16) * i - ROW0 // 16, 0), 16)


def _my_place():
    return lax.axis_index("x"), lax.axis_index("y"), lax.axis_index("c")


def _dev_index(px, py, pc):
    return 4 * px + 2 * py + pc


def _cast_shards(w_in_s, w_conv_s, w_rec_s, w_out_s):
    def body(a_ref, c_ref, r_ref, o_ref, oa_ref, ob_ref):
        oa_ref[...] = a_ref[...].astype(BF16)
        for k, ref in enumerate((c_ref, r_ref, o_ref)):
            ob_ref[k] = ref[0].astype(BF16)

    return pl.pallas_call(
        body, name="cast_shards",
        out_shape=(jax.ShapeDtypeStruct(w_in_s.shape, BF16), jax.ShapeDtypeStruct((3, W_ROW_BLK, D), BF16)),
    )(w_in_s, w_conv_s, w_rec_s, w_out_s)


def _peer(x, y, c, r):
    return (jnp.bitwise_xor(x, (r >> 2) & 1), jnp.bitwise_xor(y, (r >> 1) & 1), jnp.bitwise_xor(c, r & 1))


def _gather_small(small_s):
    def body(s_ref, o_ref, send_sems, recv_sems, local_sem):
        x, y, c = _my_place()
        my_id = _dev_index(x, y, c)
        mine = pltpu.make_async_copy(s_ref, o_ref.at[my_id], local_sem)
        mine.start()
        copies = []
        for r in range(1, N_DEV):
            cp = pltpu.make_async_remote_copy(
                src_ref=s_ref, dst_ref=o_ref.at[my_id], send_sem=send_sems.at[r - 1], recv_sem=recv_sems.at[r - 1],
                device_id=_peer(x, y, c, r), device_id_type=MESH_ID)
            cp.start()
            copies.append(cp)
        for cp in copies:
            cp.wait_recv()
        for cp in copies:
            cp.wait_send()
        mine.wait()

    return pl.pallas_call(
        body, name="gather_small", out_shape=jax.ShapeDtypeStruct((N_DEV,) + small_s.shape, F32),
        in_specs=[ANY], out_specs=ANY,
        scratch_shapes=[pltpu.SemaphoreType.DMA((7,)), pltpu.SemaphoreType.DMA((7,)), pltpu.SemaphoreType.DMA],
    )(small_s)


def _w3_gather(src, out, stage, send_sems, recv_sems, local_sems):
    x, y, c = _my_place()
    me, sibling = (x, y, c), (x, y, 1 - c)
    chips = [(1 - x, y), (x, 1 - y), (1 - x, 1 - y)]

    def block(place):
        d = _dev_index(*place)
        return out.at[:, pl.ds(pl.multiple_of(d * W_ROW_BLK, W_ROW_BLK), W_ROW_BLK), :]

    def copy(k, place, to, from_src=False):
        return pltpu.make_async_remote_copy(
            src_ref=src if from_src else block(place), dst_ref=block(place),
            send_sem=send_sems.at[k], recv_sem=recv_sems.at[k], device_id=to, device_id_type=MESH_ID)

    own_in = pltpu.make_async_copy(src, stage, local_sems.at[0])
    own_out = pltpu.make_async_copy(stage, block(me), local_sems.at[1])

    def start():
        copy(0, me, sibling, from_src=True).start()
        for j, chip in enumerate(chips):
            copy(1 + j, me, (*chip, c), from_src=True).start()
        own_in.start()
        own_in.wait()
        own_out.start()

    def finish():
        for j, chip in enumerate(chips):
            copy(1 + j, (*chip, c), me).wait_recv()
            copy(4 + j, (*chip, c), sibling).start()
        copy(0, sibling, me).wait_recv()
        for j, chip in enumerate(chips):
            copy(4 + j, (*chip, 1 - c), me).wait_recv()
        for k in range(7):
            copy(k, me, me).wait_send()
        own_out.wait()

    return start, finish


def _p3_to_sibling(p3_ref, land_ref, send_sems, recv_sems):
    x, y, c = _my_place()

    def cp(q):
        d = _dev_index(*_chip_rel(x, y, q), 1 - c)
        return pltpu.make_async_remote_copy(
            src_ref=p3_ref.at[:, pl.ds(pl.multiple_of(d * W_ROW_BLK, W_ROW_BLK), W_ROW_BLK), :],
            dst_ref=land_ref.at[q], send_sem=send_sems.at[q], recv_sem=recv_sems.at[q],
            device_id=(x, y, 1 - c), device_id_type=MESH_ID)

    def start():
        for q in range(4):
            cp(q).start()

    def finish():
        for q in range(4):
            cp(q).wait_recv()
        for q in range(4):
            cp(q).wait_send()

    return start, finish


def _partials_to_owners(src_ref, far_ref, send_sems, recv_sems):
    x, y, c = _my_place()

    def cp(q):
        return pltpu.make_async_remote_copy(
            src_ref=src_ref.at[q], dst_ref=far_ref.at[q - 1], send_sem=send_sems.at[q - 1],
            recv_sem=recv_sems.at[q - 1], device_id=(*_chip_rel(x, y, q), c), device_id_type=MESH_ID)

    def start():
        for q in range(1, 4):
            cp(q).start()

    def finish():
        for q in range(1, 4):
            cp(q).wait_recv()
        for q in range(1, 4):
            cp(q).wait_send()

    return start, finish


def _gather_chips(x, y, c):
    first = (jnp.bitwise_xor(x, 1 - c), jnp.bitwise_xor(y, c))
    second = (jnp.bitwise_xor(x, c), jnp.bitwise_xor(y, 1 - c))
    return [first, second, (1 - x, 1 - y)]


def _gather_and_proj(x_seq, meta_full, norm_g, w_in_b, order):
    tm = TM_MAT
    n_m = TP // tm
    last_m = n_m - 1

    def body(order_ref, x_ref, meta_ref, g_ref, s0, proj_ref, xin_ref, h_out, o0, hbuf, wbuf, send_sems, recv_sems,
             local_sems):
        del order_ref
        n = pl.program_id(0)
        m = pl.program_id(1)
        x, y, c = _my_place()
        me, sibling = (x, y, c), (x, y, 1 - c)
        chips = _gather_chips(x, y, c)

        def block(place):
            return o0.at[_dev_index(*place)]

        def copy(k, place, to, from_src=False):
            return pltpu.make_async_remote_copy(
                src_ref=s0 if from_src else block(place), dst_ref=block(place),
                send_sem=send_sems.at[k], recv_sem=recv_sems.at[k], device_id=to, device_id_type=MESH_ID)

        def to_vmem(place, slot):
            return pltpu.make_async_copy(block(place), wbuf.at[slot], local_sems.at[slot])

        own_out = pltpu.make_async_copy(wbuf.at[0], block(me), local_sems.at[2])
        h_copy = pltpu.make_async_copy(hbuf, h_out, local_sems.at[3])

        @pl.when((n == 0) & (m == 0))
        def _():
            copy(0, me, sibling, from_src=True).start()
            for j, chip in enumerate(chips[0:2]):
                copy(1 + j, me, (*chip, c), from_src=True).start()
            mine = pltpu.make_async_copy(s0, wbuf.at[0], local_sems.at[0])
            mine.start()
            mine.wait()
            own_out.start()

        @pl.when(n == 0)
        def _():
            xv = x_ref[...]
            xin_ref[...] = jnp.where(m == 0, pltpu.roll(xv, ROW0, 0), xv)

            @pl.when(m == 0)
            def _():
                xin_ref[0:PAD_FRONT, :] = jnp.zeros((PAD_FRONT, D), F32)
                xin_ref[PAD_FRONT:ROW0, :] = meta_ref[...]

            xv = xin_ref[...]
            r = lax.rsqrt(jnp.mean(xv * xv, axis=-1, keepdims=True) + EPS)
            hbuf[m] = (xv * r * g_ref[...]).astype(BF16)

        between = [4 + c, 5 - c, 6]
        first, second, diag = chips
        plan = [(sibling, (0, sibling), None),
                ((*first, c), (1, (*first, c)), between[0]),
                ((*second, 1 - c), (between[1], (*second, 1 - c)), None),
                ((*second, c), (2, (*second, c)), between[1]),
                ((*first, 1 - c), (between[0], (*first, 1 - c)), None),
                ((*diag, c), (3, (*diag, c)), between[2]),
                ((*diag, 1 - c), (between[2], (*diag, 1 - c)), None)]

        for s, (place, (k, origin), pass_on) in enumerate(plan, start=1):
            @pl.when((n == s - 1) & (m == last_m))
            def _(s=s, place=place, k=k, origin=origin, pass_on=pass_on):
                copy(k, origin, me).wait_recv()
                if pass_on is not None:
                    copy(pass_on, place, sibling).start()
                if s == 2:
                    copy(3, place, (*chips[1], c)).start()
                    own_out.wait()
                to_vmem(place, s % 2).start()

            @pl.when((n == s) & (m == 0))
            def _(s=s, place=place):
                to_vmem(place, s % 2).wait()

        proj_ref[...] = _dot(hbuf[m], wbuf[lax.rem(n, 2)]).astype(BF16)

        @pl.when((n == 0) & (m == last_m))
        def _():
            h_copy.start()

        @pl.when((n == N_DEV - 1) & (m == last_m))
        def _():
            for k in range(7):
                copy(k, me, me).wait_send()
            h_copy.wait()

    return pl.pallas_call(
        body, name="gather_and_proj",
        grid_spec=pltpu.PrefetchScalarGridSpec(
            num_scalar_prefetch=1, grid=(N_DEV, n_m),
            in_specs=[pl.BlockSpec((pl.Element(tm), pl.Element(D)),
                                   lambda n, m, o: (_window_start(jnp.where(n == 0, m, 0), tm), 0)),
                      pl.BlockSpec((N_META, D), lambda n, m, o: (0, 0)),
                      pl.BlockSpec((1, D), lambda n, m, o: (0, 0)), ANY],
            out_specs=(pl.BlockSpec((tm, W_IN_BLK), lambda n, m, o: (m, o[n])),
                       pl.BlockSpec((tm, D), lambda n, m, o: (jnp.where(n == 0, m, last_m), 0)), ANY, ANY),
            scratch_shapes=[pltpu.VMEM((n_m, tm, D), BF16), pltpu.VMEM((2, D, W_IN_BLK), BF16),
                            pltpu.SemaphoreType.DMA((7,)), pltpu.SemaphoreType.DMA((7,)),
                            pltpu.SemaphoreType.DMA((4,))]),
        out_shape=(jax.ShapeDtypeStruct((TP, D_IN), BF16), jax.ShapeDtypeStruct((TP, D), F32),
                   jax.ShapeDtypeStruct((n_m, tm, D), BF16), jax.ShapeDtypeStruct((N_DEV, D, W_IN_BLK), BF16)),
        compiler_params=pltpu.CompilerParams(dimension_semantics=("arbitrary", "arbitrary")),
    )(order, x_seq, meta_full, norm_g, w_in_b)


def _chip_rel(x, y, r):
    return (jnp.bitwise_xor(x, r >> 1), jnp.bitwise_xor(y, r & 1))


def _exchange_small(pack, srs):
    def body(pk, sr, pk_all, sr_all, send_sems, recv_sems, local_sems):
        x, y, c = _my_place()
        my_id = _dev_index(x, y, c)
        mine = [pltpu.make_async_copy(pk, pk_all.at[my_id], local_sems.at[0]),
                pltpu.make_async_copy(sr.at[my_id], sr_all.at[my_id], local_sems.at[1])]
        for cp in mine:
            cp.start()
        copies = []
        for r in range(1, N_DEV):
            peer = (jnp.bitwise_xor(x, (r >> 2) & 1), jnp.bitwise_xor(y, (r >> 1) & 1), jnp.bitwise_xor(c, r & 1))
            peer_id = _dev_index(*peer)
            for a, (src, dst) in enumerate(((pk, pk_all.at[my_id]), (sr.at[peer_id], sr_all.at[my_id]))):
                cp = pltpu.make_async_remote_copy(
                    src_ref=src, dst_ref=dst, send_sem=send_sems.at[a * 7 + r - 1], recv_sem=recv_sems.at[a * 7 + r - 1],
                    device_id=peer, device_id_type=MESH_ID)
                cp.start()
                copies.append(cp)
        for cp in copies:
            cp.wait_recv()
        for cp in copies:
            cp.wait_send()
        for cp in mine:
            cp.wait()

    return pl.pallas_call(
        body, name="exchange_small",
        out_shape=(jax.ShapeDtypeStruct((N_DEV,) + pack.shape, F32), jax.ShapeDtypeStruct(srs.shape, F32)),
        in_specs=[ANY, ANY], out_specs=(ANY, ANY),
        scratch_shapes=[pltpu.SemaphoreType.DMA((14,)), pltpu.SemaphoreType.DMA((14,)), pltpu.SemaphoreType.DMA((2,))],
    )(pack, srs)


N_CB = D // HEAD_W


def _store_by_cb(ref, idx, rows, val):
    for cb in range(N_CB):
        ref[(*idx, cb, rows, slice(None))] = val[:, cb * HEAD_W:(cb + 1) * HEAD_W]


def _fill_shifts(sh, tm):
    n = tm + HALO - 8
    for s in range(1, 8):
        for cb in range(N_CB):
            sh[s, cb, 0:n, :] = sh[0, cb, s:s + n, :]


def _conv_fwd(proj, conv_w, conv_b):
    tm = TM_ELT
    strip = tm // CONV_STRIPS

    def body(p_ref, w_ref, b_ref, c0_ref, sh, c0_sc):
        i = pl.program_id(0)

        @pl.when(i == 0)
        def _():
            sh[0, :, 0:HALO, :] = jnp.zeros((N_CB, HALO, HEAD_W), F32)

        @pl.when(i > 0)
        def _():
            sh[0, :, 0:HALO, :] = sh[0, :, tm:tm + HALO, :]

        ga = p_ref[:, 0:D].astype(F32)
        gb = p_ref[:, D:2 * D].astype(F32)
        _store_by_cb(sh, (0,), slice(HALO, HALO + tm), ga * _sigmoid(gb))
        _fill_shifts(sh, tm)
        for cb in range(N_CB):
            cs = slice(cb * HEAD_W, (cb + 1) * HEAD_W)
            for st in range(CONV_STRIPS):
                acc = jnp.broadcast_to(b_ref[:, cs], (strip, HEAD_W))
                for j in range(CONV_K):
                    off = HALO - (CONV_K - 1) + j
                    lo = st * strip + 8 * (off // 8)
                    acc = acc + w_ref[j:j + 1, cs] * sh[off % 8, cb, lo:lo + strip, :]
                c0_sc[st * strip:(st + 1) * strip, cs] = acc
        c0_ref[...] = c0_sc[...].astype(ACT)

    return pl.pallas_call(
        body, name="conv_fwd", grid=(TP // tm,),
        in_specs=[pl.BlockSpec((tm, 2 * D), lambda i: (i, 0)), pl.BlockSpec((CONV_K, D), lambda i: (0, 0)),
                  pl.BlockSpec((1, D), lambda i: (0, 0))],
        out_specs=pl.BlockSpec((tm, D), lambda i: (i, 0)),
        out_shape=jax.ShapeDtypeStruct((TP, D), ACT),
        scratch_shapes=[pltpu.VMEM((8, N_CB, HALO + tm, HEAD_W), F32), pltpu.VMEM((tm, D), F32)],
        compiler_params=pltpu.CompilerParams(dimension_semantics=("arbitrary",)),
    )(proj, conv_w, conv_b)


def _gates(p_ref, lbl_ref, chunk, bsc):
    lb = _sigmoid(lbl_ref[0:1, :] - lbl_ref[1:2, :])
    q_raw = p_ref[:, 0:D].astype(F32)
    f_raw = p_ref[:, D:2 * D].astype(F32)
    sq = _sigmoid(q_raw)
    q = q_raw * sq
    sg = _sigmoid(f_raw)
    f = lb + (1.0 - lb) * sg
    row = lax.broadcasted_iota(jnp.int32, (CHUNK, 1), 0) + chunk * CHUNK
    valid = row >= PAD_FRONT
    lf = jnp.where(valid, jnp.log(f), 0.0)
    k = jnp.where(valid, 1.0 - f, 0.0)
    r_i = lax.broadcasted_iota(jnp.int32, (CHUNK, CHUNK), 0)
    c_i = lax.broadcasted_iota(jnp.int32, (CHUNK, CHUNK), 1)
    causal = r_i >= c_i
    bsc[...] = _tri_matmul(causal.astype(BF16), lf)
    b = bsc[...]
    b_mid = bsc[CHUNK // 2 - 1:CHUNK // 2, :]
    b_last = bsc[CHUNK - 1:CHUNK, :]
    e_q = jnp.exp(b)
    e_qm = jnp.exp(b - b_mid)
    e_km = jnp.exp(b_mid - b)
    e_kh = jnp.exp(b_last - b)
    e_last = jnp.exp(b_last)
    return dict(lb=lb, q_raw=q_raw, sq=sq, q=q, sg=sg, f=f, k=k, valid=valid, causal=causal,
                e_q=e_q, e_qm=e_qm, e_km=e_km, e_kh=e_kh, e_last=e_last)


def _rec_fwd(proj, lb_logits, w3_b):
    cps = CHUNKS_PER_STEP
    rows = cps * CHUNK

    def body(p_ref, lbl_ref, w3s_ref, o_ref, s_ref, w3o_ref, st, bsc, w3buf, send_sems, recv_sems, local_sems):
        n = pl.program_id(0)
        gather_start, gather_finish = _w3_gather(w3s_ref, w3o_ref, w3buf, send_sems, recv_sems, local_sems)

        @pl.when(n == 0)
        def _():
            st[...] = jnp.zeros_like(st)
            gather_start()

        def prep(ci):
            g = _gates(p_ref.at[pl.ds(ci * CHUNK, CHUNK)], lbl_ref, n * cps + ci, bsc.at[ci])
            g["q1"] = (g["q"] * g["e_q"]).astype(BF16)
            g["qm"] = (g["q"] * g["e_qm"]).astype(BF16)
            g["km"] = (g["k"] * g["e_km"]).astype(BF16)
            g["kh"] = (g["k"] * g["e_kh"]).astype(BF16)
            return g

        def heads(ci, g):
            rs = pl.ds(ci * CHUNK, CHUNK)
            pv = p_ref.at[rs]
            s_ref[ci] = st[...]
            for h in range(HEADS):
                sl = slice(h * HEAD_W, (h + 1) * HEAD_W)
                v = pv[:, 2 * D + h * HEAD_W:2 * D + (h + 1) * HEAD_W]
                att = jnp.where(g["causal"], _dot_nt(g["qm"][:, sl], g["km"][:, sl]), 0.0).astype(BF16)
                s_h = st[h]
                o_ref[rs, sl] = (_dot_nt(g["q1"][:, sl], s_h.astype(BF16)) + _dot(att, v)).astype(ACT)
                st[h] = s_h * g["e_last"][:, sl] + _dot_tn(v, g["kh"][:, sl])

        ready = prep(0)
        for ci in range(cps):
            coming = prep(ci + 1) if ci + 1 < cps else None
            heads(ci, ready)
            ready = coming

        @pl.when(n == N_CHUNK // cps - 1)
        def _():
            gather_finish()

    return pl.pallas_call(
        body, name="rec_fwd", grid=(N_CHUNK // cps,),
        in_specs=[pl.BlockSpec((rows, 3 * D), lambda n: (n, 1)), pl.BlockSpec((2, D), lambda n: (0, 0)), ANY],
        out_specs=(pl.BlockSpec((rows, D), lambda n: (n, 0)),
                   pl.BlockSpec((cps, HEADS, HEAD_W, HEAD_W), lambda n: (n, 0, 0, 0)), ANY),
        out_shape=(jax.ShapeDtypeStruct((TP, D), ACT), jax.ShapeDtypeStruct((N_CHUNK, HEADS, HEAD_W, HEAD_W), F32),
                   jax.ShapeDtypeStruct((3, D, D), BF16)),
        scratch_shapes=[pltpu.VMEM((HEADS, HEAD_W, HEAD_W), F32), pltpu.VMEM((cps, CHUNK, D), F32),
                        pltpu.VMEM((3, W_ROW_BLK, D), BF16), pltpu.SemaphoreType.DMA((7,)),
                        pltpu.SemaphoreType.DMA((7,)), pltpu.SemaphoreType.DMA((2,))],
        compiler_params=pltpu.CompilerParams(dimension_semantics=("arbitrary",)),
    )(proj, lb_logits, w3_b)


def _rec_bwd(proj, lb_logits, d_o, s_start, dproj, p3):
    cps = CHUNKS_PER_STEP
    rows = cps * CHUNK
    last = N_CHUNK // cps - 1

    def body(p_ref, lbl_ref, do_ref, s_ref, dproj_in, p3_ref, dp_ref, dlb_ref, land_ref, dst, bsc, dq_sc, dk_sc, g_sc,
             send_sems, recv_sems):
        del dproj_in
        n = pl.program_id(0)
        ride_start, ride_finish = _p3_to_sibling(p3_ref, land_ref, send_sems, recv_sems)

        @pl.when(n == 0)
        def _():
            ride_start()
            dst[...] = jnp.zeros_like(dst)
            dlb_ref[...] = jnp.zeros_like(dlb_ref)

        def prep(ci):
            g = _gates(p_ref.at[pl.ds(ci * CHUNK, CHUNK)], lbl_ref, (last - n) * cps + ci, bsc.at[ci])
            g["q1"] = (g["q"] * g["e_q"]).astype(BF16)
            qm_f = g["q"] * g["e_qm"]
            km_f = g["k"] * g["e_km"]
            g["qm"] = qm_f.astype(BF16)
            g["km"] = km_f.astype(BF16)
            g["qm_lo"] = (qm_f - g["qm"].astype(F32)).astype(BF16)
            g["km_lo"] = (km_f - g["km"].astype(F32)).astype(BF16)
            g["kh_f"] = g["k"] * g["e_kh"]
            g["kh"] = g["kh_f"].astype(BF16)
            return g

        def heads_and_post(ci, g):
            rs = pl.ds(ci * CHUNK, CHUNK)
            pv = p_ref.at[rs]
            dpv = dp_ref.at[rs]
            q1, qm, km, qm_lo, km_lo, kh_f, kh = (g[k] for k in ("q1", "qm", "km", "qm_lo", "km_lo", "kh_f", "kh"))
            for h in range(HEADS):
                sl = slice(h * HEAD_W, (h + 1) * HEAD_W)
                v = pv[:, 2 * D + h * HEAD_W:2 * D + (h + 1) * HEAD_W]
                d_oh = do_ref[rs, sl].astype(BF16)
                s0 = s_ref[ci, h]
                ds_end = dst[h]
                ds_end_b = ds_end.astype(BF16)
                att = jnp.where(g["causal"], _dot_nt(qm[:, sl], km[:, sl]), 0.0).astype(BF16)
                d_att = jnp.where(g["causal"], _dot_nt(d_oh, v), 0.0).astype(BF16)
                d_v = _dot_tn(att, d_oh) + _dot_nt(kh[:, sl], ds_end_b)
                d_qm2 = _dot(d_att, jnp.concatenate([km[:, sl], km_lo[:, sl]], axis=1))
                d_qm = d_qm2[:, 0:HEAD_W] + d_qm2[:, HEAD_W:2 * HEAD_W]
                d_q1 = _dot(d_oh, s0.astype(BF16))
                d_km2 = _dot_tn(d_att, jnp.concatenate([qm[:, sl], qm_lo[:, sl]], axis=1))
                d_km = d_km2[:, 0:HEAD_W] + d_km2[:, HEAD_W:2 * HEAD_W]
                d_kh = _dot(v, ds_end_b)
                dq_sc[ci, :, sl] = d_qm * g["e_qm"][:, sl] + d_q1 * g["e_q"][:, sl]
                dk_sc[ci, :, sl] = d_km * g["e_km"][:, sl] + d_kh * g["e_kh"][:, sl]
                g_sc[ci, :, sl] = (jnp.sum(kh_f[:, sl] * d_kh, axis=0, keepdims=True)
                                   + g["e_last"][:, sl] * jnp.sum(ds_end * s0, axis=0, keepdims=True))
                dst[h] = ds_end * g["e_last"][:, sl] + _dot_tn(d_oh, q1[:, sl])
                dpv[:, 2 * D + h * HEAD_W:2 * D + (h + 1) * HEAD_W] = d_v.astype(BF16)
            d_q = dq_sc[ci]
            d_k = dk_sc[ci]
            d_b = g["q"] * d_q - g["k"] * d_k
            anti = jnp.logical_not(g["causal"]) | (lax.broadcasted_iota(jnp.int32, (CHUNK, CHUNK), 0)
                                                    == lax.broadcasted_iota(jnp.int32, (CHUNK, CHUNK), 1))
            d_lf = _tri_matmul(anti.astype(BF16), d_b) + g_sc[ci]
            d_f = jnp.where(g["valid"], d_lf / g["f"] - d_k, 0.0)
            sg = g["sg"]
            dlb_ref[0:1, :] += jnp.sum(d_f * (1.0 - sg), axis=0, keepdims=True)
            dpv[:, 0:D] = (d_q * _dsilu(g["q_raw"], g["sq"])).astype(BF16)
            dpv[:, D:2 * D] = (d_f * (1.0 - g["lb"]) * sg * (1.0 - sg)).astype(BF16)

        ready = prep(cps - 1)
        for ci in reversed(range(cps)):
            coming = prep(ci - 1) if ci > 0 else None
            heads_and_post(ci, ready)
            ready = coming

        @pl.when(n == last)
        def _():
            ride_finish()

    return pl.pallas_call(
        body, name="rec_bwd", grid=(N_CHUNK // cps,),
        in_specs=[pl.BlockSpec((rows, 3 * D), lambda n: (last - n, 1)), pl.BlockSpec((2, D), lambda n: (0, 0)),
                  pl.BlockSpec((rows, D), lambda n: (last - n, 0)),
                  pl.BlockSpec((cps, HEADS, HEAD_W, HEAD_W), lambda n: (last - n, 0, 0, 0)), ANY, ANY],
        out_specs=(pl.BlockSpec((rows, 3 * D), lambda n: (last - n, 1)), pl.BlockSpec((8, D), lambda n: (0, 0)), ANY),
        out_shape=(jax.ShapeDtypeStruct((TP, D_IN), BF16), jax.ShapeDtypeStruct((8, D), F32),
                   jax.ShapeDtypeStruct((4, 3, W_ROW_BLK, D), F32)),
        scratch_shapes=[pltpu.VMEM((HEADS, HEAD_W, HEAD_W), F32), pltpu.VMEM((cps, CHUNK, D), F32),
                        pltpu.VMEM((cps, CHUNK, D), F32), pltpu.VMEM((cps, CHUNK, D), F32),
                        pltpu.VMEM((cps, 1, D), F32), pltpu.SemaphoreType.DMA((4,)), pltpu.SemaphoreType.DMA((4,))],
        input_output_aliases={4: 0},
        compiler_params=pltpu.CompilerParams(dimension_semantics=("arbitrary",)),
    )(proj, lb_logits, d_o, s_start, dproj, p3)


def _mid(xin, tgt, o, c0, proj, w3, ln_g, ln_b, gnorm_g, final_g):
    tm = TM_ELT

    def body(x_ref, t_ref, o_ref, c0_ref, z_ref, gr_ref, mc_ref, mr_ref, w_ref, lng_ref, lnb_ref, gng_ref, fg_ref,
             do_ref, dc0_ref, dz_ref, dp_ref, a3_ref, b3_ref, red_ref, on_sc, don_sc):
        i = pl.program_id(0)

        @pl.when(i == 0)
        def _():
            red_ref[...] = jnp.zeros_like(red_ref)

        w_conv, w_rec, w_out = w_ref[0], w_ref[1], w_ref[2]
        c0v = c0_ref[...].astype(F32)
        mu = jnp.mean(c0v, axis=-1, keepdims=True)
        xc = c0v - mu
        rstd = lax.rsqrt(jnp.mean(xc * xc, axis=-1, keepdims=True) + EPS)
        xh = xc * rstd
        c1 = xh * lng_ref[...] + lnb_ref[...]
        s1 = _sigmoid(c1)
        c2 = c1 * s1
        z = z_ref[...].astype(F32)
        sz = _sigmoid(z)
        silu_z = z * sz
        u_conv = (c2 * silu_z).astype(BF16)
        y_conv = _dot(u_conv, w_conv)
        ov = o_ref[...].astype(F32)
        r3 = []
        for h in range(HEADS):
            sl = slice(h * HEAD_W, (h + 1) * HEAD_W)
            oh = ov[:, sl]
            r_h = lax.rsqrt(jnp.mean(oh * oh, axis=-1, keepdims=True) + EPS)
            r3.append(r_h)
            on_sc[:, sl] = oh * r_h
        o_n = on_sc[...]
        o_g = o_n * gng_ref[...]
        gr = gr_ref[...].astype(F32)
        sgr = _sigmoid(gr)
        silu_g = gr * sgr
        u_rec = (o_g * silu_g).astype(BF16)
        y_rec = _dot(u_rec, w_rec)
        mc = mc_ref[...].astype(F32)
        mr = mr_ref[...].astype(F32)
        smc = _sigmoid(mc)
        smr = _sigmoid(mr)
        merged = (smc * y_conv + smr * y_rec).astype(BF16)
        res = x_ref[...] + _dot(merged, w_out)
        r2 = lax.rsqrt(jnp.mean(res * res, axis=-1, keepdims=True) + EPS)
        xh2 = res * r2
        row = lax.broadcasted_iota(jnp.int32, (tm, 1), 0) + i * tm
        real = row >= ROW0
        tgt = t_ref[...]
        tgt = jnp.where(i == 0, pltpu.roll(tgt, ROW0, 0), tgt)
        diff = jnp.where(real, xh2 * fg_ref[...] - tgt, 0.0)
        d_y = diff * (1.0 / D)
        d_xh2 = d_y * fg_ref[...]
        d_res = r2 * (d_xh2 - xh2 * jnp.mean(d_xh2 * xh2, axis=-1, keepdims=True))
        d_res_b = d_res.astype(BF16)
        d_merged = _dot_nt(d_res_b, w_out)
        d_yc = (d_merged * smc).astype(BF16)
        d_yr = (d_merged * smr).astype(BF16)
        dp_ref[:, D:2 * D] = (d_merged * y_conv * smc * (1.0 - smc)).astype(BF16)
        dp_ref[:, 2 * D:3 * D] = (d_merged * y_rec * smr * (1.0 - smr)).astype(BF16)
        d_ur = _dot_nt(d_yr, w_rec)
        d_og = d_ur * silu_g
        dp_ref[:, 0:D] = (d_ur * o_g * _dsilu(gr, sgr)).astype(BF16)
        d_on = d_og * gng_ref[...]
        for h in range(HEADS):
            sl = slice(h * HEAD_W, (h + 1) * HEAD_W)
            d_h = d_on[:, sl]
            n_h = o_n[:, sl]
            don_sc[:, sl] = r3[h] * (d_h - n_h * jnp.mean(d_h * n_h, axis=-1, keepdims=True))
        do_ref[...] = don_sc[...].astype(ACT)
        d_uc = _dot_nt(d_yc, w_conv)
        d_c2 = d_uc * silu_z
        dz_ref[...] = (d_uc * c2 * _dsilu(z, sz)).astype(BF16)
        d_c1 = d_c2 * _dsilu(c1, s1)
        d_xh = d_c1 * lng_ref[...]
        d_c0 = rstd * (d_xh - jnp.mean(d_xh, axis=-1, keepdims=True)
                       - xh * jnp.mean(d_xh * xh, axis=-1, keepdims=True))
        dc0_ref[...] = d_c0.astype(ACT)
        a3_ref[0] = u_conv
        b3_ref[0] = d_yc
        a3_ref[1] = u_rec
        b3_ref[1] = d_yr
        a3_ref[2] = merged
        b3_ref[2] = d_res_b
        def colsum(vv):
            return jnp.sum(vv, axis=0, keepdims=True)

        red_ref[0:1, :] += colsum(d_y * xh2)
        red_ref[1:2, :] += colsum(d_og * o_n)
        red_ref[2:3, :] += colsum(d_c1 * xh)
        red_ref[3:4, :] += colsum(d_c1)
        red_ref[4:5, :] += colsum(d_c0)
        red_ref[5:6, :] += colsum(diff * diff) * (0.5 / D)

    def row_block(width, col):
        return pl.BlockSpec((tm, width), lambda i: (i, col))

    def const_block(shape):
        return pl.BlockSpec(shape, lambda i: (0,) * len(shape))

    stack = jax.ShapeDtypeStruct((3, TP, D), BF16)
    stack_spec = pl.BlockSpec((3, tm, D), lambda i: (0, i, 0))
    return pl.pallas_call(
        body, name="mid", grid=(TP // tm,),
        in_specs=[row_block(D, 0),
                  pl.BlockSpec((pl.Element(tm), pl.Element(D)), lambda i: (_window_start(i, tm), 0)),
                  row_block(D, 0), row_block(D, 0),
                  row_block(D, 2), row_block(D, 6), row_block(D, 7), row_block(D, 8),
                  pl.BlockSpec((3, D, D), lambda i: (0, 0, 0), pipeline_mode=pl.Buffered(1)),
                  const_block((1, D)), const_block((1, D)), const_block((1, D)), const_block((1, D))],
        out_specs=(row_block(D, 0), row_block(D, 0), row_block(D, 0), row_block(3 * D, 2),
                   stack_spec, stack_spec, const_block((8, D))),
        out_shape=(jax.ShapeDtypeStruct((TP, D), ACT), jax.ShapeDtypeStruct((TP, D), ACT),
                   jax.ShapeDtypeStruct((TP, D), BF16),
                   jax.ShapeDtypeStruct((TP, D_IN), BF16), stack, stack, jax.ShapeDtypeStruct((8, D), F32)),
        scratch_shapes=[pltpu.VMEM((tm, D), F32), pltpu.VMEM((tm, D), F32)],
        compiler_params=pltpu.CompilerParams(dimension_semantics=("arbitrary",), vmem_limit_bytes=60 * 1024 * 1024),
    )(xin, tgt, o, c0, proj, proj, proj, proj, w3, ln_g, ln_b, gnorm_g, final_g)


def _conv_bwd(proj, d_c0, d_z, conv_w, dproj, chip1b):
    tm = TM_ELT
    n_tile = TP // tm
    lastt = n_tile - 1

    strip = tm // CONV_STRIPS

    def body(p_ref, dc_ref, dz_ref, w_ref, dproj_in, c1_ref, dp_ref, dw_ref, far_ref, dsh, a_sc, da_sc, acc,
             send_sems, recv_sems):
        del dproj_in
        i = pl.program_id(0)
        ride_start, ride_finish = _partials_to_owners(c1_ref, far_ref, send_sems, recv_sems)

        @pl.when(i == 0)
        def _():
            ride_start()
            dsh[0, :, tm:tm + HALO, :] = jnp.zeros((N_CB, HALO, HEAD_W), F32)
            acc[...] = jnp.zeros_like(acc)

        @pl.when(i > 0)
        def _():
            dsh[0, :, tm:tm + HALO, :] = dsh[0, :, 0:HALO, :]

        _store_by_cb(dsh, (0,), slice(0, tm), dc_ref[...].astype(F32))
        _fill_shifts(dsh, tm)
        ga = p_ref[:, 0:D].astype(F32)
        sb = _sigmoid(p_ref[:, D:2 * D].astype(F32))
        a = ga * sb
        _store_by_cb(a_sc, (), slice(0, tm), a)
        for cb in range(N_CB):
            cs = slice(cb * HEAD_W, (cb + 1) * HEAD_W)
            for st in range(CONV_STRIPS):
                rows = slice(st * strip, (st + 1) * strip)
                a_s = a_sc[cb, rows, :]
                d_a = jnp.zeros((strip, HEAD_W), F32)
                for j in range(CONV_K):
                    off = CONV_K - 1 - j
                    lo = st * strip + 8 * (off // 8)
                    slab = dsh[off % 8, cb, lo:lo + strip, :]
                    d_a = d_a + w_ref[j:j + 1, cs] * slab
                    acc[j, :, cs] += jnp.sum((a_s * slab).reshape(strip // 8, 8, HEAD_W), axis=0)
                da_sc[rows, cs] = d_a
        d_a = da_sc[...]
        dp_ref[:, 0:D] = (d_a * sb).astype(BF16)
        dp_ref[:, D:2 * D] = (d_a * a * (1.0 - sb)).astype(BF16)
        dp_ref[:, 2 * D:3 * D] = dz_ref[...]

        @pl.when(i == lastt)
        def _():
            for j in range(CONV_K):
                dw_ref[j:j + 1, :] = jnp.sum(acc[j], axis=0, keepdims=True)
            dw_ref[CONV_K:CONV_K + 1, :] = jnp.zeros((1, D), F32)
            ride_finish()

    return pl.pallas_call(
        body, name="conv_bwd", grid=(n_tile,),
        in_specs=[pl.BlockSpec((tm, 2 * D), lambda i: (lastt - i, 0)), pl.BlockSpec((tm, D), lambda i: (lastt - i, 0)),
                  pl.BlockSpec((tm, D), lambda i: (lastt - i, 0)), pl.BlockSpec((CONV_K, D), lambda i: (0, 0)), ANY, ANY],
        out_specs=(pl.BlockSpec((tm, 3 * D), lambda i: (lastt - i, 0)), pl.BlockSpec((CONV_K + 1, D), lambda i: (0, 0)),
                   ANY),
        out_shape=(jax.ShapeDtypeStruct((TP, D_IN), BF16), jax.ShapeDtypeStruct((CONV_K + 1, D), F32),
                   jax.ShapeDtypeStruct((3, 3, W_ROW_BLK, D), BF16)),
        scratch_shapes=[pltpu.VMEM((8, N_CB, tm + HALO, HEAD_W), F32), pltpu.VMEM((N_CB, tm, HEAD_W), F32),
                        pltpu.VMEM((tm, D), F32), pltpu.VMEM((CONV_K, 8, D), F32),
                        pltpu.SemaphoreType.DMA((3,)), pltpu.SemaphoreType.DMA((3,))],
        input_output_aliases={4: 0},
        compiler_params=pltpu.CompilerParams(dimension_semantics=("arbitrary",)),
    )(proj, d_c0, d_z, conv_w, dproj, chip1b)


def _rec_conv_bwd(proj, lb_logits, d_o, s_start, d_c0, d_z, conv_w, dproj, p3):
    cps = CHUNKS_PER_STEP
    tm = cps * CHUNK
    last = N_CHUNK // cps - 1
    n_strip = 4
    strip = tm // n_strip

    def body(p_ref, lbl_ref, do_ref, s_ref, pg_ref, dc_ref, dz_ref, w_ref, dproj_in, p3_ref,
             dp_ref, dlb_ref, dw_ref, land_ref,
             dst, bsc, dq_sc, dk_sc, g_sc, dsh, a_sc, da_sc, acc, send_sems, recv_sems):
        del dproj_in
        n = pl.program_id(0)
        ride_start, ride_finish = _p3_to_sibling(p3_ref, land_ref, send_sems, recv_sems)

        @pl.when(n == 0)
        def _():
            ride_start()
            dst[...] = jnp.zeros_like(dst)
            dlb_ref[...] = jnp.zeros_like(dlb_ref)
            dsh[0, :, tm:tm + HALO, :] = jnp.zeros((N_CB, HALO, HEAD_W), F32)
            acc[...] = jnp.zeros_like(acc)

        @pl.when(n > 0)
        def _():
            dsh[0, :, tm:tm + HALO, :] = dsh[0, :, 0:HALO, :]

        _store_by_cb(dsh, (0,), slice(0, tm), dc_ref[...].astype(F32))
        _fill_shifts(dsh, tm)
        ga = pg_ref[:, 0:D].astype(F32)
        sb = _sigmoid(pg_ref[:, D:2 * D].astype(F32))
        a = ga * sb
        _store_by_cb(a_sc, (), slice(0, tm), a)

        def conv_unit(cb, st):
            cs = slice(cb * HEAD_W, (cb + 1) * HEAD_W)
            rows = slice(st * strip, (st + 1) * strip)
            a_s = a_sc[cb, rows, :]
            d_a = jnp.zeros((strip, HEAD_W), F32)
            for j in range(CONV_K):
                off = CONV_K - 1 - j
                lo = st * strip + 8 * (off // 8)
                slab = dsh[off % 8, cb, lo:lo + strip, :]
                d_a = d_a + w_ref[j:j + 1, cs] * slab
                acc[j, :, cs] += jnp.sum((a_s * slab).reshape(strip // 8, 8, HEAD_W), axis=0)
            da_sc[rows, cs] = d_a

        units = [(cb, st) for cb in range(N_CB) for st in range(n_strip)]

        def prep(ci):
            g = _gates(p_ref.at[pl.ds(ci * CHUNK, CHUNK)], lbl_ref, (last - n) * cps + ci, bsc.at[ci])
            g["q1"] = (g["q"] * g["e_q"]).astype(BF16)
            qm_f = g["q"] * g["e_qm"]
            km_f = g["k"] * g["e_km"]
            g["qm"] = qm_f.astype(BF16)
            g["km"] = km_f.astype(BF16)
            g["qm_lo"] = (qm_f - g["qm"].astype(F32)).astype(BF16)
            g["km_lo"] = (km_f - g["km"].astype(F32)).astype(BF16)
            g["kh_f"] = g["k"] * g["e_kh"]
            g["kh"] = g["kh_f"].astype(BF16)
            return g

        def heads_and_post(ci, g):
            rs = pl.ds(ci * CHUNK, CHUNK)
            pv = p_ref.at[rs]
            dpv = dp_ref.at[rs]
            q1, qm, km, qm_lo, km_lo, kh_f, kh = (g[k] for k in ("q1", "qm", "km", "qm_lo", "km_lo", "kh_f", "kh"))
            for h in range(HEADS):
                if units:
                    conv_unit(*units.pop(0))
                sl = slice(h * HEAD_W, (h + 1) * HEAD_W)
                v = pv[:, 2 * D + h * HEAD_W:2 * D + (h + 1) * HEAD_W]
                d_oh = do_ref[rs, sl].astype(BF16)
                s0 = s_ref[ci, h]
                ds_end = dst[h]
                ds_end_b = ds_end.astype(BF16)
                att = jnp.where(g["causal"], _dot_nt(qm[:, sl], km[:, sl]), 0.0).astype(BF16)
                d_att = jnp.where(g["causal"], _dot_nt(d_oh, v), 0.0).astype(BF16)
                d_v = _dot_tn(att, d_oh) + _dot_nt(kh[:, sl], ds_end_b)
                d_qm2 = _dot(d_att, jnp.concatenate([km[:, sl], km_lo[:, sl]], axis=1))
                d_qm = d_qm2[:, 0:HEAD_W] + d_qm2[:, HEAD_W:2 * HEAD_W]
                d_q1 = _dot(d_oh, s0.astype(BF16))
                d_km2 = _dot_tn(d_att, jnp.concatenate([qm[:, sl], qm_lo[:, sl]], axis=1))
                d_km = d_km2[:, 0:HEAD_W] + d_km2[:, HEAD_W:2 * HEAD_W]
                d_kh = _dot(v, ds_end_b)
                dq_sc[ci, :, sl] = d_qm * g["e_qm"][:, sl] + d_q1 * g["e_q"][:, sl]
                dk_sc[ci, :, sl] = d_km * g["e_km"][:, sl] + d_kh * g["e_kh"][:, sl]
                g_sc[ci, :, sl] = (jnp.sum(kh_f[:, sl] * d_kh, axis=0, keepdims=True)
                                   + g["e_last"][:, sl] * jnp.sum(ds_end * s0, axis=0, keepdims=True))
                dst[h] = ds_end * g["e_last"][:, sl] + _dot_tn(d_oh, q1[:, sl])
                dpv[:, 5 * D + h * HEAD_W:5 * D + (h + 1) * HEAD_W] = d_v.astype(BF16)
            d_q = dq_sc[ci]
            d_k = dk_sc[ci]
            d_b = g["q"] * d_q - g["k"] * d_k
            anti = jnp.logical_not(g["causal"]) | (lax.broadcasted_iota(jnp.int32, (CHUNK, CHUNK), 0)
                                                    == lax.broadcasted_iota(jnp.int32, (CHUNK, CHUNK), 1))
            d_lf = _tri_matmul(anti.astype(BF16), d_b) + g_sc[ci]
            d_f = jnp.where(g["valid"], d_lf / g["f"] - d_k, 0.0)
            sg = g["sg"]
            dlb_ref[0:1, :] += jnp.sum(d_f * (1.0 - sg), axis=0, keepdims=True)
            dpv[:, 3 * D:4 * D] = (d_q * _dsilu(g["q_raw"], g["sq"])).astype(BF16)
            dpv[:, 4 * D:5 * D] = (d_f * (1.0 - g["lb"]) * sg * (1.0 - sg)).astype(BF16)

        ready = prep(cps - 1)
        for ci in reversed(range(cps)):
            coming = prep(ci - 1) if ci > 0 else None
            heads_and_post(ci, ready)
            ready = coming
        while units:
            conv_unit(*units.pop(0))

        d_a = da_sc[...]
        dp_ref[:, 0:D] = (d_a * sb).astype(BF16)
        dp_ref[:, D:2 * D] = (d_a * a * (1.0 - sb)).astype(BF16)
        dp_ref[:, 2 * D:3 * D] = dz_ref[...]

        @pl.when(n == last)
        def _():
            for j in range(CONV_K):
                dw_ref[j:j + 1, :] = jnp.sum(acc[j], axis=0, keepdims=True)
            dw_ref[CONV_K:CONV_K + 1, :] = jnp.zeros((1, D), F32)
            ride_finish()

    def rows_of(width, col):
        return pl.BlockSpec((tm, width), lambda n: (last - n, col))

    return pl.pallas_call(
        body, name="rec_conv_bwd", grid=(N_CHUNK // cps,),
        in_specs=[rows_of(3 * D, 1), pl.BlockSpec((2, D), lambda n: (0, 0)), rows_of(D, 0),
                  pl.BlockSpec((cps, HEADS, HEAD_W, HEAD_W), lambda n: (last - n, 0, 0, 0)),
                  rows_of(2 * D, 0), rows_of(D, 0), rows_of(D, 0), pl.BlockSpec((CONV_K, D), lambda n: (0, 0)), ANY, ANY],
        out_specs=(rows_of(6 * D, 0), pl.BlockSpec((8, D), lambda n: (0, 0)),
                   pl.BlockSpec((CONV_K + 1, D), lambda n: (0, 0)), ANY),
        out_shape=(jax.ShapeDtypeStruct((TP, D_IN), BF16), jax.ShapeDtypeStruct((8, D), F32),
                   jax.ShapeDtypeStruct((CONV_K + 1, D), F32), jax.ShapeDtypeStruct((4, 3, W_ROW_BLK, D), F32)),
        scratch_shapes=[pltpu.VMEM((HEADS, HEAD_W, HEAD_W), F32), pltpu.VMEM((cps, CHUNK, D), F32),
                        pltpu.VMEM((cps, CHUNK, D), F32), pltpu.VMEM((cps, CHUNK, D), F32),
                        pltpu.VMEM((cps, 1, D), F32),
                        pltpu.VMEM((8, N_CB, tm + HALO, HEAD_W), F32), pltpu.VMEM((N_CB, tm, HEAD_W), F32),
                        pltpu.VMEM((tm, D), F32), pltpu.VMEM((CONV_K, 8, D), F32),
                        pltpu.SemaphoreType.DMA((4,)), pltpu.SemaphoreType.DMA((4,))],
        input_output_aliases={8: 0},
        compiler_params=pltpu.CompilerParams(dimension_semantics=("arbitrary",)),
    )(proj, lb_logits, d_o, s_start, proj, d_c0, d_z, conv_w, dproj, p3)


def _wgrad3(a3, b3):
    tt = TT_WGRAD

    def body(a_ref, b_ref, o_ref):
        @pl.when(pl.program_id(1) == 0)
        def _():
            o_ref[...] = jnp.zeros_like(o_ref)

        o_ref[0] += _dot_tn(a_ref[0], b_ref[0])

    return pl.pallas_call(
        body, name="wgrad3", grid=(3, TP // tt),
        in_specs=[pl.BlockSpec((1, tt, D), lambda g, t: (g, t, 0)), pl.BlockSpec((1, tt, D), lambda g, t: (g, t, 0))],
        out_specs=pl.BlockSpec((1, D, D), lambda g, t: (g, 0, 0)),
        out_shape=jax.ShapeDtypeStruct((3, D, D), F32),
        compiler_params=pltpu.CompilerParams(dimension_semantics=("arbitrary", "arbitrary")),
    )(a3, b3)


def _wgrad_in(h, dproj, ids, chip1b):
    tt = TT_WGRAD
    n_t = TP // tt

    def body(ids_ref, a_ref, b_ref, c1_ref, o_ref, ob_ref, l0_ref, far_ref, acc, tmp, send_sems, recv_sems, tmp_sem,
             far_send_sems, far_recv_sems):
        del ids_ref
        r = pl.program_id(0)
        t = pl.program_id(1)
        x, y, c = _my_place()
        sibling = (x, y, 1 - c)
        slot = lax.rem(r, 2)
        ride_start, ride_finish = _partials_to_owners(c1_ref, far_ref, far_send_sems, far_recv_sems)

        @pl.when((r == 0) & (t == 0))
        def _():
            ride_start()

        def send_in(q):
            return pltpu.make_async_remote_copy(
                src_ref=acc.at[q % 2], dst_ref=l0_ref.at[q], send_sem=send_sems.at[q], recv_sem=recv_sems.at[q],
                device_id=sibling, device_id_type=MESH_ID)

        def landed(q):
            return pltpu.make_async_copy(l0_ref.at[q], tmp, tmp_sem)

        @pl.when(t == 0)
        def _():
            acc[slot] = jnp.zeros((D, W_IN_BLK), F32)

        acc[slot] += _dot_tn(a_ref[...], b_ref[...])

        for q in range(4):
            @pl.when((r == q) & (t == n_t - 1))
            def _(q=q):
                if q >= 1:
                    send_in(q - 1).wait_send()
                send_in(q).start()

            @pl.when((r == 4 + q) & (t == n_t - 2))
            def _(q=q):
                if q == 0:
                    send_in(3).wait_send()
                send_in(q).wait_recv()
                landed(q).start()

            @pl.when((r == 4 + q) & (t == n_t - 1))
            def _(q=q):
                landed(q).wait()
                tot = acc[q % 2] + tmp[...]
                o_ref[0] = tot
                ob_ref[0] = tot.astype(BF16)

        @pl.when((r == N_DEV - 1) & (t == n_t - 1))
        def _():
            ride_finish()

    blk = pl.BlockSpec((1, D, W_IN_BLK), lambda r, t, ids: (jnp.maximum(r - 4, 0), 0, 0))
    return pl.pallas_call(
        body, name="wgrad_in",
        grid_spec=pltpu.PrefetchScalarGridSpec(
            num_scalar_prefetch=1, grid=(N_DEV, n_t),
            in_specs=[pl.BlockSpec((tt, D), lambda r, t, ids: (t, 0)),
                      pl.BlockSpec((tt, W_IN_BLK), lambda r, t, ids: (t, ids[r])), ANY],
            out_specs=(blk, blk, ANY, ANY),
            scratch_shapes=[pltpu.VMEM((2, D, W_IN_BLK), F32), pltpu.VMEM((D, W_IN_BLK), F32),
                            pltpu.SemaphoreType.DMA((4,)), pltpu.SemaphoreType.DMA((4,)), pltpu.SemaphoreType.DMA,
                            pltpu.SemaphoreType.DMA((3,)), pltpu.SemaphoreType.DMA((3,))]),
        out_shape=(jax.ShapeDtypeStruct((4, D, W_IN_BLK), F32), jax.ShapeDtypeStruct((4, D, W_IN_BLK), BF16),
                   jax.ShapeDtypeStruct((4, D, W_IN_BLK), F32), jax.ShapeDtypeStruct((3, 3, W_ROW_BLK, D), BF16)),
        compiler_params=pltpu.CompilerParams(dimension_semantics=("arbitrary", "arbitrary")),
    )(ids, h, dproj, chip1b)


def _chip_sum_3(p3, land1, ids_mine):
    def body(ids_ref, p_ref, l_ref, o_ref, ob_ref):
        del ids_ref
        tot = p_ref[...] + l_ref[0]
        o_ref[0] = tot
        ob_ref[0] = tot.astype(BF16)

    blk = pl.BlockSpec((1, 3, W_ROW_BLK, D), lambda r, ids: (r, 0, 0, 0))
    return pl.pallas_call(
        body, name="chip_sum_3",
        grid_spec=pltpu.PrefetchScalarGridSpec(
            num_scalar_prefetch=1, grid=(4,),
            in_specs=[pl.BlockSpec((3, W_ROW_BLK, D), lambda r, ids: (0, ids[r], 0)), blk],
            out_specs=(blk, blk)),
        out_shape=(jax.ShapeDtypeStruct((4, 3, W_ROW_BLK, D), F32), jax.ShapeDtypeStruct((4, 3, W_ROW_BLK, D), BF16)),
    )(ids_mine, p3, land1)


def _dh_and_norm_bwd(dproj, w_in_full, xin, b3, norm_g, chip0b):
    tm = TM_MAT
    n_k = N_DEV // DH_K_BLKS
    n_m = TP // tm

    def body(dp_ref, w_ref, x_ref, dr_ref, g_ref, c0_ref, dx_ref, dg_ref, f0_ref, acc, send_sems, recv_sems):
        m = pl.program_id(0)
        k = pl.program_id(1)
        ride_start, ride_finish = _partials_to_owners(c0_ref, f0_ref, send_sems, recv_sems)

        @pl.when((m == 0) & (k == 0))
        def _():
            ride_start()

        @pl.when(k == 0)
        def _():
            acc[...] = jnp.zeros_like(acc)

        part = _dot_nt(dp_ref[:, 0:W_IN_BLK], w_ref[0])
        for j in range(1, DH_K_BLKS):
            part = part + _dot_nt(dp_ref[:, j * W_IN_BLK:(j + 1) * W_IN_BLK], w_ref[j])
        acc[...] += part

        @pl.when((k == n_k - 1) & (m == 0))
        def _():
            dg_ref[...] = jnp.zeros_like(dg_ref)

        @pl.when(k == n_k - 1)
        def _():
            xv = x_ref[...]
            r1 = lax.rsqrt(jnp.mean(xv * xv, axis=-1, keepdims=True) + EPS)
            xh = xv * r1
            d_h = acc[...]
            dg_ref[0:1, :] += jnp.sum(d_h * xh, axis=0, keepdims=True)
            d_xh = d_h * g_ref[...]
            dx_ref[...] = dr_ref[0].astype(F32) + r1 * (d_xh - xh * jnp.mean(d_xh * xh, axis=-1, keepdims=True))

        @pl.when((m == n_m - 1) & (k == n_k - 1))
        def _():
            ride_finish()

    return pl.pallas_call(
        body, name="dh_norm_bwd", grid=(n_m, n_k),
        in_specs=[pl.BlockSpec((tm, DH_K_BLKS * W_IN_BLK), lambda m, k: (m, k)),
                  pl.BlockSpec((DH_K_BLKS, D, W_IN_BLK), lambda m, k: (k, 0, 0)),
                  pl.BlockSpec((tm, D), lambda m, k: (m, 0)), pl.BlockSpec((1, tm, D), lambda m, k: (2, m, 0)),
                  pl.BlockSpec((1, D), lambda m, k: (0, 0)), ANY],
        out_specs=(pl.BlockSpec((tm, D), lambda m, k: (m, 0)), pl.BlockSpec((8, D), lambda m, k: (0, 0)), ANY),
        out_shape=(jax.ShapeDtypeStruct((TP, D), F32), jax.ShapeDtypeStruct((8, D), F32),
                   jax.ShapeDtypeStruct((3, D, W_IN_BLK), BF16)),
        scratch_shapes=[pltpu.VMEM((tm, D), F32), pltpu.SemaphoreType.DMA((3,)), pltpu.SemaphoreType.DMA((3,))],
        compiler_params=pltpu.CompilerParams(dimension_semantics=("arbitrary", "arbitrary")),
    )(dproj, w_in_full, xin, b3, norm_g, chip0b)


def _sum_adamw(own, landed, w, m, v, tr, name):
    rows, cols = w.shape
    n_t = rows // tr

    def body(o_ref, l1_ref, l2_ref, l3_ref, w_ref, m_ref, v_ref, g_ref, d_ref, m2_ref, v2_ref):
        g = ((o_ref[...] + l1_ref[...].astype(F32)) + l2_ref[...].astype(F32)) + l3_ref[...].astype(F32)
        delta, m2, v2 = _adamw(w_ref[...], g, m_ref[...], v_ref[...])
        g_ref[...] = g
        d_ref[...] = delta
        m2_ref[...] = m2
        v2_ref[...] = v2

    def spec(k):
        return pl.BlockSpec((tr, cols), lambda i: (i + k * n_t, 0))

    out = jax.ShapeDtypeStruct((rows, cols), F32)
    return pl.pallas_call(
        body, name=name, grid=(n_t,),
        in_specs=[spec(0), spec(0), spec(1), spec(2), spec(0), spec(0), spec(0)],
        out_specs=(spec(0),) * 4, out_shape=(out,) * 4,
    )(own, landed, landed, landed, w, m, v)


def _adamw_3(chip1, far1, ws, ms, vs):
    def body(c_ref, f_ref, *refs):
        w_refs, m_refs, v_refs, outs = refs[0:3], refs[3:6], refs[6:9], refs[9:21]
        for k in range(3):
            g = ((c_ref[0, k] + f_ref[0, k].astype(F32)) + f_ref[1, k].astype(F32)) + f_ref[2, k].astype(F32)
            delta, m2, v2 = _adamw(w_refs[k][0], g, m_refs[k][0], v_refs[k][0])
            for kind, val in enumerate((g, delta, m2, v2)):
                outs[3 * kind + k][0] = val

    full = pl.BlockSpec((1, W_ROW_BLK, D), lambda i: (0, 0, 0))
    out = jax.ShapeDtypeStruct((1, W_ROW_BLK, D), F32)
    res = pl.pallas_call(
        body, name="adamw_3", grid=(1,),
        in_specs=[pl.BlockSpec((1, 3, W_ROW_BLK, D), lambda i: (0, 0, 0, 0)),
                  pl.BlockSpec((3, 3, W_ROW_BLK, D), lambda i: (0, 0, 0, 0))] + [full] * 9,
        out_specs=(full,) * 12, out_shape=(out,) * 12,
    )(chip1, far1, *ws, *ms, *vs)
    return tuple(res[3 * kind:3 * kind + 3] for kind in range(4))


N_SMALL = 9


def _small_update(pack_all, srs_all, ws, ms, vs):
    def body(pk_ref, sr_ref, *refs):
        w_refs, m_refs, v_refs = refs[0:N_SMALL], refs[N_SMALL:2 * N_SMALL], refs[2 * N_SMALL:3 * N_SMALL]
        loss_ref = refs[3 * N_SMALL]
        outs = refs[3 * N_SMALL + 1:7 * N_SMALL + 1]
        tot_sc, tots_sc = refs[7 * N_SMALL + 1:]
        tot = pk_ref[0]
        tot_s = sr_ref[0]
        for d in range(1, N_DEV):
            tot = tot + pk_ref[d]
            tot_s = tot_s + sr_ref[d]
        tot_sc[...] = tot
        tots_sc[...] = tot_s
        loss_ref[...] = jnp.sum(tot_sc[5:6, :], axis=1, keepdims=True)
        lbl = w_refs[4]
        p0 = _sigmoid(lbl[0:1, :] - lbl[1:2, :])
        d_l0 = tot_sc[4:5, :] * p0 * (1.0 - p0)

        def update(k, sel, g):
            delta, m2, v2 = _adamw(w_refs[k][sel], g, m_refs[k][sel], v_refs[k][sel])
            for kind, val in enumerate((g, delta, m2, v2)):
                outs[N_SMALL * kind + k][sel] = val

        everything = (slice(None), slice(None))
        for k, row in ((0, 0), (1, 1), (2, 2), (3, 3), (5, 6), (6, 7)):
            update(k, everything, tot_sc[row:row + 1, :])
        update(4, (slice(0, 1), slice(None)), d_l0)
        update(4, (slice(1, 2), slice(None)), -d_l0)
        update(7, (0, slice(None), slice(None)), tots_sc[0:CONV_K, :])
        update(8, everything, tots_sc[META_ROW:META_ROW + N_META, :])

    shapes = [jax.ShapeDtypeStruct(w.shape, F32) for w in ws]
    res = pl.pallas_call(
        body, name="small_update",
        out_shape=(jax.ShapeDtypeStruct((1, 1), F32), *(shapes * 4)),
        scratch_shapes=[pltpu.VMEM((8, D), F32), pltpu.VMEM((SMALL_ROWS, HEAD_W), F32)],
    )(pack_all, srs_all, *ws, *ms, *vs)
    return res[0], tuple(res[1 + N_SMALL * kind:1 + N_SMALL * (kind + 1)] for kind in range(4))


def _local_step(xin, proj, target, conv_w_full, conv_b, ln_g, ln_b, w3_b, lb_logits, gnorm_g, final_g, ids_mine):
    fg = final_g.reshape(1, D)
    c0 = _conv_fwd(proj, conv_w_full, conv_b)
    o, s_start, w3_full = _rec_fwd(proj, lb_logits, w3_b)
    d_o, d_c0, d_z, dproj, a3, b3, red = _mid(xin, target, o, c0, proj, w3_full, ln_g, ln_b, gnorm_g, fg)
    p3 = _wgrad3(a3, b3)
    dproj, dlb, d_conv_w, land1 = _rec_conv_bwd(proj, lb_logits, d_o, s_start, d_c0, d_z, conv_w_full, dproj, p3)
    chip1, chip1b = _chip_sum_3(p3, land1, ids_mine)
    return dproj, b3, p3, chip1, chip1b, d_conv_w, red, dlb


def kernel(x, meta_tokens, norm_g, w_in, conv_w, conv_b, ln_g, ln_b, w_conv_out, lb_logits, gnorm_g, w_rec_out, w_out, final_g, loss_target, m_meta_tokens, m_norm_g, m_w_in, m_conv_w, m_conv_b, m_ln_g, m_ln_b, m_w_conv_out, m_lb_logits, m_gnorm_g, m_w_rec_out, m_w_out, m_final_g, v_meta_tokens, v_norm_g, v_w_in, v_conv_w, v_conv_b, v_ln_g, v_ln_b, v_w_conv_out, v_lb_logits, v_gnorm_g, v_w_rec_out, v_w_out, v_final_g):
    mx, my, mc = _my_place()

    ws_s = jnp.concatenate([conv_w[0], jnp.zeros((1, HEAD_W), F32), meta_tokens], axis=0)
    small_full = jnp.transpose(_gather_small(ws_s), (1, 0, 2)).reshape(SMALL_ROWS, D)
    conv_w_full = small_full[0:CONV_K]
    meta_full = small_full[META_ROW:META_ROW + N_META]
    w_in_b, w3_b = _cast_shards(w_in[0], w_conv_out, w_rec_out, w_out)
    first, second, diag = _gather_chips(mx, my, mc)
    use_order = [(mx, my, mc), (mx, my, 1 - mc), (*first, mc), (*second, 1 - mc), (*second, mc), (*first, 1 - mc),
                 (*diag, mc), (*diag, 1 - mc)]
    order = jnp.stack([_dev_index(*p) for p in use_order]).astype(jnp.int32)
    proj, xin, h, w_in_full = _gather_and_proj(x[0], meta_full, norm_g, w_in_b, order)
    h = h.reshape(TP, D)

    ids_mine = jnp.stack([_dev_index(*_chip_rel(mx, my, r), mc) for r in range(4)]).astype(jnp.int32)
    ids_sib = jnp.stack([_dev_index(*_chip_rel(mx, my, r), 1 - mc) for r in range(4)]).astype(jnp.int32)
    dproj, b3, _, chip1, chip1b, d_conv_w, red, dlb = _local_step(
        xin, proj, loss_target[0], conv_w_full, conv_b, ln_g, ln_b, w3_b, lb_logits, gnorm_g, final_g, ids_mine)

    chip0, chip0b, _, far1 = _wgrad_in(h, dproj, jnp.concatenate([ids_sib, ids_mine]), chip1b)
    d_xin, dng, far0 = _dh_and_norm_bwd(dproj, w_in_full, xin, b3, norm_g, chip0b)
    pack = jnp.concatenate([dng[0:1], red[4:5], red[2:3], red[3:4], dlb[0:1], red[5:6], red[1:2], red[0:1]], axis=0)
    g_in, d_in, m_in, v_in = _sum_adamw(chip0.reshape(4 * D, W_IN_BLK), far0.reshape(3 * D, W_IN_BLK), w_in[0],
                                        m_w_in[0], v_w_in[0], 256, "adamw_in")
    big3 = _adamw_3(chip1, far1, (w_conv_out, w_rec_out, w_out), (m_w_conv_out, m_w_rec_out, m_w_out),
                    (v_w_conv_out, v_w_rec_out, v_w_out))

    srs = jnp.concatenate([d_conv_w, d_xin[PAD_FRONT:ROW0]], axis=0)
    srs = jnp.transpose(srs.reshape(SMALL_ROWS, N_DEV, HEAD_W), (1, 0, 2))
    pack_all, srs_all = _exchange_small(pack, srs)
    loss, small = _small_update(
        pack_all, srs_all,
        (norm_g, conv_b, ln_g, ln_b, lb_logits, gnorm_g, final_g.reshape(1, D), conv_w, meta_tokens),
        (m_norm_g, m_conv_b, m_ln_g, m_ln_b, m_lb_logits, m_gnorm_g, m_final_g.reshape(1, D), m_conv_w, m_meta_tokens),
        (v_norm_g, v_conv_b, v_ln_g, v_ln_b, v_lb_logits, v_gnorm_g, v_final_g.reshape(1, D), v_conv_w, v_meta_tokens))

    outs = [loss.reshape(()), d_xin[ROW0:][None]]
    for kind, a_in in enumerate((g_in, d_in, m_in, v_in)):
        ng, cb, lg, lb_, lbl, gg, fg, cw, mt = small[kind]
        a_3 = big3[kind]
        outs += [mt, ng, a_in[None], cw, cb, lg, lb_, a_3[0], lbl, gg, a_3[1], a_3[2], fg.reshape(D)]
    return tuple(outs)
```

```python
import functools

import jax
import jax.numpy as jnp
from jax import lax
from jax.experimental import pallas as pl
from jax.experimental.pallas import tpu as pltpu

F32 = jnp.float32
BF16 = jnp.bfloat16
ACT = BF16

D = 1024
SEQ = 4096
N_META = 16
CHUNK = 64
PAD_FRONT = 48
ROW0 = PAD_FRONT + N_META
TP = ROW0 + SEQ
N_CHUNK = TP // CHUNK
HEADS = 8
HEAD_W = 128
D_IN = 9 * D
N_DEV = 8
W_IN_BLK = D_IN // N_DEV
W_ROW_BLK = D // N_DEV
CONV_K = 31
SMALL_ROWS = 48
META_ROW = 32
EPS = 1e-6
HALO = 32

TM_MAT = 832
TT_WGRAD = 2080
DH_K_BLKS = 2
TM_ELT = 208
CHUNKS_PER_STEP = 5
CONV_STRIPS = 2

ADAM_LR = 0.001
ADAM_B1 = 0.9
ADAM_B2 = 0.999
ADAM_EPS = 1e-08
ADAM_WD = 0.01
ADAM_STEP = 10

MESH_ID = pl.DeviceIdType.MESH
ANY = pl.BlockSpec(memory_space=pl.ANY)


def _sigmoid(v):
    return jax.nn.sigmoid(v)


def _dsilu(v, s):
    return s * (1.0 + v * (1.0 - s))


def _dot(a, b):
    return jnp.dot(a, b, preferred_element_type=F32)


def _dot_nt(a, b):
    return lax.dot_general(a, b, (((1,), (1,)), ((), ())), preferred_element_type=F32)


def _dot_tn(a, b):
    return lax.dot_general(a, b, (((0,), (0,)), ((), ())), preferred_element_type=F32)


def _split3(v):
    hi = v.astype(BF16)
    r1 = v - hi.astype(F32)
    mid = r1.astype(BF16)
    lo = (r1 - mid.astype(F32)).astype(BF16)
    return hi, mid, lo


def _tri_matmul(tri, v):
    hi, mid, lo = _split3(v)
    return _dot(tri, hi) + _dot(tri, mid) + _dot(tri, lo)


def _adamw(w, g, m, v):
    m2 = ADAM_B1 * m + (1.0 - ADAM_B1) * g
    v2 = ADAM_B2 * v + (1.0 - ADAM_B2) * jnp.square(g)
    m_hat = m2 / (1.0 - ADAM_B1 ** ADAM_STEP)
    v_hat = v2 / (1.0 - ADAM_B2 ** ADAM_STEP)
    delta = -ADAM_LR * (m_hat / (jnp.sqrt(v_hat) + ADAM_EPS) + ADAM_WD * w)
    return delta, m2, v2


def _window_start(i, tm):
    assert tm % 16 == 0 and ROW0 % 16 == 0
    return pl.multiple_of(16 * jnp.maximum((tm // 16) * i - ROW0 // 16, 0), 16)


def _my_place():
    return lax.axis_index("x"), lax.axis_index("y"), lax.axis_index("c")


def _dev_index(px, py, pc):
    return 4 * px + 2 * py + pc


def _cast_shards(w_in_s, w_conv_s, w_rec_s, w_out_s):
    def body(a_ref, c_ref, r_ref, o_ref, oa_ref, ob_ref):
        oa_ref[...] = a_ref[...].astype(BF16)
        for k, ref in enumerate((c_ref, r_ref, o_ref)):
            ob_ref[k] = ref[0].astype(BF16)

    return pl.pallas_call(
        body, name="cast_shards",
        out_shape=(jax.ShapeDtypeStruct(w_in_s.shape, BF16), jax.ShapeDtypeStruct((3, W_ROW_BLK, D), BF16)),
    )(w_in_s, w_conv_s, w_rec_s, w_out_s)


def _peer(x, y, c, r):
    return (jnp.bitwise_xor(x, (r >> 2) & 1), jnp.bitwise_xor(y, (r >> 1) & 1), jnp.bitwise_xor(c, r & 1))


def _gather_small(small_s):
    def body(s_ref, o_ref, send_sems, recv_sems, local_sem):
        x, y, c = _my_place()
        my_id = _dev_index(x, y, c)
        mine = pltpu.make_async_copy(s_ref, o_ref.at[my_id], local_sem)
        mine.start()
        copies = []
        for r in range(1, N_DEV):
            cp = pltpu.make_async_remote_copy(
                src_ref=s_ref, dst_ref=o_ref.at[my_id], send_sem=send_sems.at[r - 1], recv_sem=recv_sems.at[r - 1],
                device_id=_peer(x, y, c, r), device_id_type=MESH_ID)
            cp.start()
            copies.append(cp)
        for cp in copies:
            cp.wait_recv()
        for cp in copies:
            cp.wait_send()
        mine.wait()

    return pl.pallas_call(
        body, name="gather_small", out_shape=jax.ShapeDtypeStruct((N_DEV,) + small_s.shape, F32),
        in_specs=[ANY], out_specs=ANY,
        scratch_shapes=[pltpu.SemaphoreType.DMA((7,)), pltpu.SemaphoreType.DMA((7,)), pltpu.SemaphoreType.DMA],
    )(small_s)


def _w3_gather(src, out, stage, send_sems, recv_sems, local_sems):
    x, y, c = _my_place()
    me, sibling = (x, y, c), (x, y, 1 - c)
    chips = [(1 - x, y), (x, 1 - y), (1 - x, 1 - y)]

    def block(place):
        d = _dev_index(*place)
        return out.at[:, pl.ds(pl.multiple_of(d * W_ROW_BLK, W_ROW_BLK), W_ROW_BLK), :]

    def copy(k, place, to, from_src=False):
        return pltpu.make_async_remote_copy(
            src_ref=src if from_src else block(place), dst_ref=block(place),
            send_sem=send_sems.at[k], recv_sem=recv_sems.at[k], device_id=to, device_id_type=MESH_ID)

    own_in = pltpu.make_async_copy(src, stage, local_sems.at[0])
    own_out = pltpu.make_async_copy(stage, block(me), local_sems.at[1])

    def start():
        copy(0, me, sibling, from_src=True).start()
        for j, chip in enumerate(chips):
            copy(1 + j, me, (*chip, c), from_src=True).start()
        own_in.start()
        own_in.wait()
        own_out.start()

    def finish():
        for j, chip in enumerate(chips):
            copy(1 + j, (*chip, c), me).wait_recv()
            copy(4 + j, (*chip, c), sibling).start()
        copy(0, sibling, me).wait_recv()
        for j, chip in enumerate(chips):
            copy(4 + j, (*chip, 1 - c), me).wait_recv()
        for k in range(7):
            copy(k, me, me).wait_send()
        own_out.wait()

    return start, finish


def _p3_to_sibling(p3_ref, land_ref, send_sems, recv_sems):
    x, y, c = _my_place()

    def cp(q):
        d = _dev_index(*_chip_rel(x, y, q), 1 - c)
        return pltpu.make_async_remote_copy(
            src_ref=p3_ref.at[:, pl.ds(pl.multiple_of(d * W_ROW_BLK, W_ROW_BLK), W_ROW_BLK), :],
            dst_ref=land_ref.at[q], send_sem=send_sems.at[q], recv_sem=recv_sems.at[q],
            device_id=(x, y, 1 - c), device_id_type=MESH_ID)

    def start():
        for q in range(4):
            cp(q).start()

    def finish():
        for q in range(4):
            cp(q).wait_recv()
        for q in range(4):
            cp(q).wait_send()

    return start, finish


def _partials_to_owners(src_ref, far_ref, send_sems, recv_sems):
    x, y, c = _my_place()

    def cp(q):
        return pltpu.make_async_remote_copy(
            src_ref=src_ref.at[q], dst_ref=far_ref.at[q - 1], send_sem=send_sems.at[q - 1],
            recv_sem=recv_sems.at[q - 1], device_id=(*_chip_rel(x, y, q), c), device_id_type=MESH_ID)

    def start():
        for q in range(1, 4):
            cp(q).start()

    def finish():
        for q in range(1, 4):
            cp(q).wait_recv()
        for q in range(1, 4):
            cp(q).wait_send()

    return start, finish


def _gather_chips(x, y, c):
    first = (jnp.bitwise_xor(x, 1 - c), jnp.bitwise_xor(y, c))
    second = (jnp.bitwise_xor(x, c), jnp.bitwise_xor(y, 1 - c))
    return [first, second, (1 - x, 1 - y)]


def _gather_and_proj(x_seq, meta_full, norm_g, w_in_b, order):
    tm = TM_MAT
    n_m = TP // tm
    last_m = n_m - 1

    def body(order_ref, x_ref, meta_ref, g_ref, s0, proj_ref, xin_ref, h_out, o0, hbuf, wbuf, send_sems, recv_sems,
             local_sems):
        del order_ref
        n = pl.program_id(0)
        m = pl.program_id(1)
        x, y, c = _my_place()
        me, sibling = (x, y, c), (x, y, 1 - c)
        chips = _gather_chips(x, y, c)

        def block(place):
            return o0.at[_dev_index(*place)]

        def copy(k, place, to, from_src=False):
            return pltpu.make_async_remote_copy(
                src_ref=s0 if from_src else block(place), dst_ref=block(place),
                send_sem=send_sems.at[k], recv_sem=recv_sems.at[k], device_id=to, device_id_type=MESH_ID)

        def to_vmem(place, slot):
            return pltpu.make_async_copy(block(place), wbuf.at[slot], local_sems.at[slot])

        own_out = pltpu.make_async_copy(wbuf.at[0], block(me), local_sems.at[2])
        h_copy = pltpu.make_async_copy(hbuf, h_out, local_sems.at[3])

        @pl.when((n == 0) & (m == 0))
        def _():
            copy(0, me, sibling, from_src=True).start()
            for j, chip in enumerate(chips[0:2]):
                copy(1 + j, me, (*chip, c), from_src=True).start()
            mine = pltpu.make_async_copy(s0, wbuf.at[0], local_sems.at[0])
            mine.start()
            mine.wait()
            own_out.start()

        @pl.when(n == 0)
        def _():
            xv = x_ref[...]
            xin_ref[...] = jnp.where(m == 0, pltpu.roll(xv, ROW0, 0), xv)

            @pl.when(m == 0)
            def _():
                xin_ref[0:PAD_FRONT, :] = jnp.zeros((PAD_FRONT, D), F32)
                xin_ref[PAD_FRONT:ROW0, :] = meta_ref[...]

            xv = xin_ref[...]
            r = lax.rsqrt(jnp.mean(xv * xv, axis=-1, keepdims=True) + EPS)
            hbuf[m] = (xv * r * g_ref[...]).astype(BF16)

        between = [4 + c, 5 - c, 6]
        first, second, diag = chips
        plan = [(sibling, (0, sibling), None),
                ((*first, c), (1, (*first, c)), between[0]),
                ((*second, 1 - c), (between[1], (*second, 1 - c)), None),
                ((*second, c), (2, (*second, c)), between[1]),
                ((*first, 1 - c), (between[0], (*first, 1 - c)), None),
                ((*diag, c), (3, (*diag, c)), between[2]),
                ((*diag, 1 - c), (between[2], (*diag, 1 - c)), None)]

        for s, (place, (k, origin), pass_on) in enumerate(plan, start=1):
            @pl.when((n == s - 1) & (m == last_m))
            def _(s=s, place=place, k=k, origin=origin, pass_on=pass_on):
                copy(k, origin, me).wait_recv()
                if pass_on is not None:
                    copy(pass_on, place, sibling).start()
                if s == 2:
                    copy(3, place, (*chips[1], c)).start()
                    own_out.wait()
                to_vmem(place, s % 2).start()

            @pl.when((n == s) & (m == 0))
            def _(s=s, place=place):
                to_vmem(place, s % 2).wait()

        proj_ref[...] = _dot(hbuf[m], wbuf[lax.rem(n, 2)]).astype(BF16)

        @pl.when((n == 0) & (m == last_m))
        def _():
            h_copy.start()

        @pl.when((n == N_DEV - 1) & (m == last_m))
        def _():
            for k in range(7):
                copy(k, me, me).wait_send()
            h_copy.wait()

    return pl.pallas_call(
        body, name="gather_and_proj",
        grid_spec=pltpu.PrefetchScalarGridSpec(
            num_scalar_prefetch=1, grid=(N_DEV, n_m),
            in_specs=[pl.BlockSpec((pl.Element(tm), pl.Element(D)),
                                   lambda n, m, o: (_window_start(jnp.where(n == 0, m, 0), tm), 0)),
                      pl.BlockSpec((N_META, D), lambda n, m, o: (0, 0)),
                      pl.BlockSpec((1, D), lambda n, m, o: (0, 0)), ANY],
            out_specs=(pl.BlockSpec((tm, W_IN_BLK), lambda n, m, o: (m, o[n])),
                       pl.BlockSpec((tm, D), lambda n, m, o: (jnp.where(n == 0, m, last_m), 0)), ANY, ANY),
            scratch_shapes=[pltpu.VMEM((n_m, tm, D), BF16), pltpu.VMEM((2, D, W_IN_BLK), BF16),
                            pltpu.SemaphoreType.DMA((7,)), pltpu.SemaphoreType.DMA((7,)),
                            pltpu.SemaphoreType.DMA((4,))]),
        out_shape=(jax.ShapeDtypeStruct((TP, D_IN), BF16), jax.ShapeDtypeStruct((TP, D), F32),
                   jax.ShapeDtypeStruct((n_m, tm, D), BF16), jax.ShapeDtypeStruct((N_DEV, D, W_IN_BLK), BF16)),
        compiler_params=pltpu.CompilerParams(dimension_semantics=("arbitrary", "arbitrary")),
    )(order, x_seq, meta_full, norm_g, w_in_b)


def _chip_rel(x, y, r):
    return (jnp.bitwise_xor(x, r >> 1), jnp.bitwise_xor(y, r & 1))


def _exchange_small(pack, srs):
    def body(pk, sr, pk_all, sr_all, send_sems, recv_sems, local_sems):
        x, y, c = _my_place()
        my_id = _dev_index(x, y, c)
        mine = [pltpu.make_async_copy(pk, pk_all.at[my_id], local_sems.at[0]),
                pltpu.make_async_copy(sr.at[my_id], sr_all.at[my_id], local_sems.at[1])]
        for cp in mine:
            cp.start()
        copies = []
        for r in range(1, N_DEV):
            peer = (jnp.bitwise_xor(x, (r >> 2) & 1), jnp.bitwise_xor(y, (r >> 1) & 1), jnp.bitwise_xor(c, r & 1))
            peer_id = _dev_index(*peer)
            for a, (src, dst) in enumerate(((pk, pk_all.at[my_id]), (sr.at[peer_id], sr_all.at[my_id]))):
                cp = pltpu.make_async_remote_copy(
                    src_ref=src, dst_ref=dst, send_sem=send_sems.at[a * 7 + r - 1], recv_sem=recv_sems.at[a * 7 + r - 1],
                    device_id=peer, device_id_type=MESH_ID)
                cp.start()
                copies.append(cp)
        for cp in copies:
            cp.wait_recv()
        for cp in copies:
            cp.wait_send()
        for cp in mine:
            cp.wait()

    return pl.pallas_call(
        body, name="exchange_small",
        out_shape=(jax.ShapeDtypeStruct((N_DEV,) + pack.shape, F32), jax.ShapeDtypeStruct(srs.shape, F32)),
        in_specs=[ANY, ANY], out_specs=(ANY, ANY),
        scratch_shapes=[pltpu.SemaphoreType.DMA((14,)), pltpu.SemaphoreType.DMA((14,)), pltpu.SemaphoreType.DMA((2,))],
    )(pack, srs)


N_CB = D // HEAD_W


def _store_by_cb(ref, idx, rows, val):
    for cb in range(N_CB):
        ref[(*idx, cb, rows, slice(None))] = val[:, cb * HEAD_W:(cb + 1) * HEAD_W]


def _fill_shifts(sh, tm):
    n = tm + HALO - 8
    for s in range(1, 8):
        for cb in range(N_CB):
            sh[s, cb, 0:n, :] = sh[0, cb, s:s + n, :]


def _conv_fwd(proj, conv_w, conv_b):
    tm = TM_ELT
    strip = tm // CONV_STRIPS

    def body(p_ref, w_ref, b_ref, c0_ref, sh, c0_sc):
        i = pl.program_id(0)

        @pl.when(i == 0)
        def _():
            sh[0, :, 0:HALO, :] = jnp.zeros((N_CB, HALO, HEAD_W), F32)

        @pl.when(i > 0)
        def _():
            sh[0, :, 0:HALO, :] = sh[0, :, tm:tm + HALO, :]

        ga = p_ref[:, 0:D].astype(F32)
        gb = p_ref[:, D:2 * D].astype(F32)
        _store_by_cb(sh, (0,), slice(HALO, HALO + tm), ga * _sigmoid(gb))
        _fill_shifts(sh, tm)
        for cb in range(N_CB):
            cs = slice(cb * HEAD_W, (cb + 1) * HEAD_W)
            for st in range(CONV_STRIPS):
                acc = jnp.broadcast_to(b_ref[:, cs], (strip, HEAD_W))
                for j in range(CONV_K):
                    off = HALO - (CONV_K - 1) + j
                    lo = st * strip + 8 * (off // 8)
                    acc = acc + w_ref[j:j + 1, cs] * sh[off % 8, cb, lo:lo + strip, :]
                c0_sc[st * strip:(st + 1) * strip, cs] = acc
        c0_ref[...] = c0_sc[...].astype(ACT)

    return pl.pallas_call(
        body, name="conv_fwd", grid=(TP // tm,),
        in_specs=[pl.BlockSpec((tm, 2 * D), lambda i: (i, 0)), pl.BlockSpec((CONV_K, D), lambda i: (0, 0)),
                  pl.BlockSpec((1, D), lambda i: (0, 0))],
        out_specs=pl.BlockSpec((tm, D), lambda i: (i, 0)),
        out_shape=jax.ShapeDtypeStruct((TP, D), ACT),
        scratch_shapes=[pltpu.VMEM((8, N_CB, HALO + tm, HEAD_W), F32), pltpu.VMEM((tm, D), F32)],
        compiler_params=pltpu.CompilerParams(dimension_semantics=("arbitrary",)),
    )(proj, conv_w, conv_b)


def _gates(p_ref, lbl_ref, chunk, bsc):
    lb = _sigmoid(lbl_ref[0:1, :] - lbl_ref[1:2, :])
    q_raw = p_ref[:, 0:D].astype(F32)
    f_raw = p_ref[:, D:2 * D].astype(F32)
    sq = _sigmoid(q_raw)
    q = q_raw * sq
    sg = _sigmoid(f_raw)
    f = lb + (1.0 - lb) * sg
    row = lax.broadcasted_iota(jnp.int32, (CHUNK, 1), 0) + chunk * CHUNK
    valid = row >= PAD_FRONT
    lf = jnp.where(valid, jnp.log(f), 0.0)
    k = jnp.where(valid, 1.0 - f, 0.0)
    r_i = lax.broadcasted_iota(jnp.int32, (CHUNK, CHUNK), 0)
    c_i = lax.broadcasted_iota(jnp.int32, (CHUNK, CHUNK), 1)
    causal = r_i >= c_i
    bsc[...] = _tri_matmul(causal.astype(BF16), lf)
    b = bsc[...]
    b_mid = bsc[CHUNK // 2 - 1:CHUNK // 2, :]
    b_last = bsc[CHUNK - 1:CHUNK, :]
    e_q = jnp.exp(b)
    e_qm = jnp.exp(b - b_mid)
    e_km = jnp.exp(b_mid - b)
    e_kh = jnp.exp(b_last - b)
    e_last = jnp.exp(b_last)
    return dict(lb=lb, q_raw=q_raw, sq=sq, q=q, sg=sg, f=f, k=k, valid=valid, causal=causal,
                e_q=e_q, e_qm=e_qm, e_km=e_km, e_kh=e_kh, e_last=e_last)


def _rec_fwd(proj, lb_logits, w3_b):
    cps = CHUNKS_PER_STEP
    rows = cps * CHUNK

    def body(p_ref, lbl_ref, w3s_ref, o_ref, s_ref, w3o_ref, st, bsc, w3buf, send_sems, recv_sems, local_sems):
        n = pl.program_id(0)
        gather_start, gather_finish = _w3_gather(w3s_ref, w3o_ref, w3buf, send_sems, recv_sems, local_sems)

        @pl.when(n == 0)
        def _():
            st[...] = jnp.zeros_like(st)
            gather_start()

        def prep(ci):
            g = _gates(p_ref.at[pl.ds(ci * CHUNK, CHUNK)], lbl_ref, n * cps + ci, bsc.at[ci])
            g["q1"] = (g["q"] * g["e_q"]).astype(BF16)
            g["qm"] = (g["q"] * g["e_qm"]).astype(BF16)
            g["km"] = (g["k"] * g["e_km"]).astype(BF16)
            g["kh"] = (g["k"] * g["e_kh"]).astype(BF16)
            return g

        def heads(ci, g):
            rs = pl.ds(ci * CHUNK, CHUNK)
            pv = p_ref.at[rs]
            s_ref[ci] = st[...]
            for h in range(HEADS):
                sl = slice(h * HEAD_W, (h + 1) * HEAD_W)
                v = pv[:, 2 * D + h * HEAD_W:2 * D + (h + 1) * HEAD_W]
                att = jnp.where(g["causal"], _dot_nt(g["qm"][:, sl], g["km"][:, sl]), 0.0).astype(BF16)
                s_h = st[h]
                o_ref[rs, sl] = (_dot_nt(g["q1"][:, sl], s_h.astype(BF16)) + _dot(att, v)).astype(ACT)
                st[h] = s_h * g["e_last"][:, sl] + _dot_tn(v, g["kh"][:, sl])

        ready = prep(0)
        for ci in range(cps):
            coming = prep(ci + 1) if ci + 1 < cps else None
            heads(ci, ready)
            ready = coming

        @pl.when(n == N_CHUNK // cps - 1)
        def _():
            gather_finish()

    return pl.pallas_call(
        body, name="rec_fwd", grid=(N_CHUNK // cps,),
        in_specs=[pl.BlockSpec((rows, 3 * D), lambda n: (n, 1)), pl.BlockSpec((2, D), lambda n: (0, 0)), ANY],
        out_specs=(pl.BlockSpec((rows, D), lambda n: (n, 0)),
                   pl.BlockSpec((cps, HEADS, HEAD_W, HEAD_W), lambda n: (n, 0, 0, 0)), ANY),
        out_shape=(jax.ShapeDtypeStruct((TP, D), ACT), jax.ShapeDtypeStruct((N_CHUNK, HEADS, HEAD_W, HEAD_W), F32),
                   jax.ShapeDtypeStruct((3, D, D), BF16)),
        scratch_shapes=[pltpu.VMEM((HEADS, HEAD_W, HEAD_W), F32), pltpu.VMEM((cps, CHUNK, D), F32),
                        pltpu.VMEM((3, W_ROW_BLK, D), BF16), pltpu.SemaphoreType.DMA((7,)),
                        pltpu.SemaphoreType.DMA((7,)), pltpu.SemaphoreType.DMA((2,))],
        compiler_params=pltpu.CompilerParams(dimension_semantics=("arbitrary",)),
    )(proj, lb_logits, w3_b)


def _rec_conv_fwd(proj, lb_logits, conv_w, conv_b, w3_b):
    cps = CHUNKS_PER_STEP
    tm = cps * CHUNK
    n_strip = 4
    strip = tm // n_strip

    def body(p_ref, lbl_ref, pg_ref, w_ref, b_ref, w3s_ref, c0_ref, o_ref, s_ref, w3o_ref,
             st, bsc, sh, c0_sc, w3buf, send_sems, recv_sems, local_sems):
        n = pl.program_id(0)
        gather_start, gather_finish = _w3_gather(w3s_ref, w3o_ref, w3buf, send_sems, recv_sems, local_sems)

        @pl.when(n == 0)
        def _():
            st[...] = jnp.zeros_like(st)
            sh[0, :, 0:HALO, :] = jnp.zeros((N_CB, HALO, HEAD_W), F32)
            gather_start()

        @pl.when(n > 0)
        def _():
            sh[0, :, 0:HALO, :] = sh[0, :, tm:tm + HALO, :]

        ga = pg_ref[:, 0:D].astype(F32)
        gb = pg_ref[:, D:2 * D].astype(F32)
        _store_by_cb(sh, (0,), slice(HALO, HALO + tm), ga * _sigmoid(gb))
        _fill_shifts(sh, tm)

        def conv_unit(cb, s_i):
            cs = slice(cb * HEAD_W, (cb + 1) * HEAD_W)
            acc = jnp.broadcast_to(b_ref[:, cs], (strip, HEAD_W))
            for j in range(CONV_K):
                off = HALO - (CONV_K - 1) + j
                lo = s_i * strip + 8 * (off // 8)
                acc = acc + w_ref[j:j + 1, cs] * sh[off % 8, cb, lo:lo + strip, :]
            c0_sc[s_i * strip:(s_i + 1) * strip, cs] = acc

        units = [(cb, s_i) for cb in range(N_CB) for s_i in range(n_strip)]

        def prep(ci):
            g = _gates(p_ref.at[pl.ds(ci * CHUNK, CHUNK)], lbl_ref, n * cps + ci, bsc.at[ci])
            g["q1"] = (g["q"] * g["e_q"]).astype(BF16)
            g["qm"] = (g["q"] * g["e_qm"]).astype(BF16)
            g["km"] = (g["k"] * g["e_km"]).astype(BF16)
            g["kh"] = (g["k"] * g["e_kh"]).astype(BF16)
            return g

        def heads(ci, g):
            rs = pl.ds(ci * CHUNK, CHUNK)
            pv = p_ref.at[rs]
            s_ref[ci] = st[...]
            for h in range(HEADS):
                if units:
                    conv_unit(*units.pop(0))
                sl = slice(h * HEAD_W, (h + 1) * HEAD_W)
                v = pv[:, 2 * D + h * HEAD_W:2 * D + (h + 1) * HEAD_W]
                att = jnp.where(g["causal"], _dot_nt(g["qm"][:, sl], g["km"][:, sl]), 0.0).astype(BF16)
                s_h = st[h]
                o_ref[rs, sl] = (_dot_nt(g["q1"][:, sl], s_h.astype(BF16)) + _dot(att, v)).astype(ACT)
                st[h] = s_h * g["e_last"][:, sl] + _dot_tn(v, g["kh"][:, sl])

        ready = prep(0)
        for ci in range(cps):
            coming = prep(ci + 1) if ci + 1 < cps else None
            heads(ci, ready)
            ready = coming
        while units:
            conv_unit(*units.pop(0))
        c0_ref[...] = c0_sc[...].astype(ACT)

        @pl.when(n == N_CHUNK // cps - 1)
        def _():
            gather_finish()

    def rows_of(width, col):
        return pl.BlockSpec((tm, width), lambda n: (n, col))

    return pl.pallas_call(
        body, name="rec_conv_fwd", grid=(N_CHUNK // cps,),
        in_specs=[rows_of(3 * D, 1), pl.BlockSpec((2, D), lambda n: (0, 0)), rows_of(2 * D, 0),
                  pl.BlockSpec((CONV_K, D), lambda n: (0, 0)), pl.BlockSpec((1, D), lambda n: (0, 0)), ANY],
        out_specs=(rows_of(D, 0), rows_of(D, 0), pl.BlockSpec((cps, HEADS, HEAD_W, HEAD_W), lambda n: (n, 0, 0, 0)), ANY),
        out_shape=(jax.ShapeDtypeStruct((TP, D), ACT), jax.ShapeDtypeStruct((TP, D), ACT),
                   jax.ShapeDtypeStruct((N_CHUNK, HEADS, HEAD_W, HEAD_W), F32), jax.ShapeDtypeStruct((3, D, D), BF16)),
        scratch_shapes=[pltpu.VMEM((HEADS, HEAD_W, HEAD_W), F32), pltpu.VMEM((cps, CHUNK, D), F32),
                        pltpu.VMEM((8, N_CB, HALO + tm, HEAD_W), F32), pltpu.VMEM((tm, D), F32),
                        pltpu.VMEM((3, W_ROW_BLK, D), BF16), pltpu.SemaphoreType.DMA((7,)),
                        pltpu.SemaphoreType.DMA((7,)), pltpu.SemaphoreType.DMA((2,))],
        compiler_params=pltpu.CompilerParams(dimension_semantics=("arbitrary",)),
    )(proj, lb_logits, proj, conv_w, conv_b, w3_b)


def _rec_bwd(proj, lb_logits, d_o, s_start, dproj, p3):
    cps = CHUNKS_PER_STEP
    rows = cps * CHUNK
    last = N_CHUNK // cps - 1

    def body(p_ref, lbl_ref, do_ref, s_ref, dproj_in, p3_ref, dp_ref, dlb_ref, land_ref, dst, bsc, dq_sc, dk_sc, g_sc,
             send_sems, recv_sems):
        del dproj_in
        n = pl.program_id(0)
        ride_start, ride_finish = _p3_to_sibling(p3_ref, land_ref, send_sems, recv_sems)

        @pl.when(n == 0)
        def _():
            ride_start()
            dst[...] = jnp.zeros_like(dst)
            dlb_ref[...] = jnp.zeros_like(dlb_ref)

        def prep(ci):
            g = _gates(p_ref.at[pl.ds(ci * CHUNK, CHUNK)], lbl_ref, (last - n) * cps + ci, bsc.at[ci])
            g["q1"] = (g["q"] * g["e_q"]).astype(BF16)
            qm_f = g["q"] * g["e_qm"]
            km_f = g["k"] * g["e_km"]
            g["qm"] = qm_f.astype(BF16)
            g["km"] = km_f.astype(BF16)
            g["qm_lo"] = (qm_f - g["qm"].astype(F32)).astype(BF16)
            g["km_lo"] = (km_f - g["km"].astype(F32)).astype(BF16)
            g["kh_f"] = g["k"] * g["e_kh"]
            g["kh"] = g["kh_f"].astype(BF16)
            return g

        def heads_and_post(ci, g):
            rs = pl.ds(ci * CHUNK, CHUNK)
            pv = p_ref.at[rs]
            dpv = dp_ref.at[rs]
            q1, qm, km, qm_lo, km_lo, kh_f, kh = (g[k] for k in ("q1", "qm", "km", "qm_lo", "km_lo", "kh_f", "kh"))
            for h in range(HEADS):
                sl = slice(h * HEAD_W, (h + 1) * HEAD_W)
                v = pv[:, 2 * D + h * HEAD_W:2 * D + (h + 1) * HEAD_W]
                d_oh = do_ref[rs, sl].astype(BF16)
                s0 = s_ref[ci, h]
                ds_end = dst[h]
                ds_end_b = ds_end.astype(BF16)
                att = jnp.where(g["causal"], _dot_nt(qm[:, sl], km[:, sl]), 0.0).astype(BF16)
                d_att = jnp.where(g["causal"], _dot_nt(d_oh, v), 0.0).astype(BF16)
                d_v = _dot_tn(att, d_oh) + _dot_nt(kh[:, sl], ds_end_b)
                d_qm2 = _dot(d_att, jnp.concatenate([km[:, sl], km_lo[:, sl]], axis=1))
                d_qm = d_qm2[:, 0:HEAD_W] + d_qm2[:, HEAD_W:2 * HEAD_W]
                d_q1 = _dot(d_oh, s0.astype(BF16))
                d_km2 = _dot_tn(d_att, jnp.concatenate([qm[:, sl], qm_lo[:, sl]], axis=1))
                d_km = d_km2[:, 0:HEAD_W] + d_km2[:, HEAD_W:2 * HEAD_W]
                d_kh = _dot(v, ds_end_b)
                dq_sc[ci, :, sl] = d_qm * g["e_qm"][:, sl] + d_q1 * g["e_q"][:, sl]
                dk_sc[ci, :, sl] = d_km * g["e_km"][:, sl] + d_kh * g["e_kh"][:, sl]
                g_sc[ci, :, sl] = (jnp.sum(kh_f[:, sl] * d_kh, axis=0, keepdims=True)
                                   + g["e_last"][:, sl] * jnp.sum(ds_end * s0, axis=0, keepdims=True))
                dst[h] = ds_end * g["e_last"][:, sl] + _dot_tn(d_oh, q1[:, sl])
                dpv[:, 2 * D + h * HEAD_W:2 * D + (h + 1) * HEAD_W] = d_v.astype(BF16)
            d_q = dq_sc[ci]
            d_k = dk_sc[ci]
            d_b = g["q"] * d_q - g["k"] * d_k
            anti = jnp.logical_not(g["causal"]) | (lax.broadcasted_iota(jnp.int32, (CHUNK, CHUNK), 0)
                                                    == lax.broadcasted_iota(jnp.int32, (CHUNK, CHUNK), 1))
            d_lf = _tri_matmul(anti.astype(BF16), d_b) + g_sc[ci]
            d_f = jnp.where(g["valid"], d_lf / g["f"] - d_k, 0.0)
            sg = g["sg"]
            dlb_ref[0:1, :] += jnp.sum(d_f * (1.0 - sg), axis=0, keepdims=True)
            dpv[:, 0:D] = (d_q * _dsilu(g["q_raw"], g["sq"])).astype(BF16)
            dpv[:, D:2 * D] = (d_f * (1.0 - g["lb"]) * sg * (1.0 - sg)).astype(BF16)

        ready = prep(cps - 1)
        for ci in reversed(range(cps)):
            coming = prep(ci - 1) if ci > 0 else None
            heads_and_post(ci, ready)
            ready = coming

        @pl.when(n == last)
        def _():
            ride_finish()

    return pl.pallas_call(
        body, name="rec_bwd", grid=(N_CHUNK // cps,),
        in_specs=[pl.BlockSpec((rows, 3 * D), lambda n: (last - n, 1)), pl.BlockSpec((2, D), lambda n: (0, 0)),
                  pl.BlockSpec((rows, D), lambda n: (last - n, 0)),
                  pl.BlockSpec((cps, HEADS, HEAD_W, HEAD_W), lambda n: (last - n, 0, 0, 0)), ANY, ANY],
        out_specs=(pl.BlockSpec((rows, 3 * D), lambda n: (last - n, 1)), pl.BlockSpec((8, D), lambda n: (0, 0)), ANY),
        out_shape=(jax.ShapeDtypeStruct((TP, D_IN), BF16), jax.ShapeDtypeStruct((8, D), F32),
                   jax.ShapeDtypeStruct((4, 3, W_ROW_BLK, D), F32)),
        scratch_shapes=[pltpu.VMEM((HEADS, HEAD_W, HEAD_W), F32), pltpu.VMEM((cps, CHUNK, D), F32),
                        pltpu.VMEM((cps, CHUNK, D), F32), pltpu.VMEM((cps, CHUNK, D), F32),
                        pltpu.VMEM((cps, 1, D), F32), pltpu.SemaphoreType.DMA((4,)), pltpu.SemaphoreType.DMA((4,))],
        input_output_aliases={4: 0},
        compiler_params=pltpu.CompilerParams(dimension_semantics=("arbitrary",)),
    )(proj, lb_logits, d_o, s_start, dproj, p3)


def _mid(xin, tgt, o, c0, proj, w3, ln_g, ln_b, gnorm_g, final_g):
    tm = TM_ELT

    def body(x_ref, t_ref, o_ref, c0_ref, z_ref, gr_ref, mc_ref, mr_ref, w_ref, lng_ref, lnb_ref, gng_ref, fg_ref,
             do_ref, dc0_ref, dz_ref, dp_ref, a3_ref, b3_ref, red_ref, on_sc, don_sc):
        i = pl.program_id(0)

        @pl.when(i == 0)
        def _():
            red_ref[...] = jnp.zeros_like(red_ref)

        w_conv, w_rec, w_out = w_ref[0], w_ref[1], w_ref[2]
        c0v = c0_ref[...].astype(F32)
        mu = jnp.mean(c0v, axis=-1, keepdims=True)
        xc = c0v - mu
        rstd = lax.rsqrt(jnp.mean(xc * xc, axis=-1, keepdims=True) + EPS)
        xh = xc * rstd
        c1 = xh * lng_ref[...] + lnb_ref[...]
        s1 = _sigmoid(c1)
        c2 = c1 * s1
        z = z_ref[...].astype(F32)
        sz = _sigmoid(z)
        silu_z = z * sz
        u_conv = (c2 * silu_z).astype(BF16)
        y_conv = _dot(u_conv, w_conv)
        ov = o_ref[...].astype(F32)
        r3 = []
        for h in range(HEADS):
            sl = slice(h * HEAD_W, (h + 1) * HEAD_W)
            oh = ov[:, sl]
            r_h = lax.rsqrt(jnp.mean(oh * oh, axis=-1, keepdims=True) + EPS)
            r3.append(r_h)
            on_sc[:, sl] = oh * r_h
        o_n = on_sc[...]
        o_g = o_n * gng_ref[...]
        gr = gr_ref[...].astype(F32)
        sgr = _sigmoid(gr)
        silu_g = gr * sgr
        u_rec = (o_g * silu_g).astype(BF16)
        y_rec = _dot(u_rec, w_rec)
        mc = mc_ref[...].astype(F32)
        mr = mr_ref[...].astype(F32)
        smc = _sigmoid(mc)
        smr = _sigmoid(mr)
        merged = (smc * y_conv + smr * y_rec).astype(BF16)
        res = x_ref[...] + _dot(merged, w_out)
        r2 = lax.rsqrt(jnp.mean(res * res, axis=-1, keepdims=True) + EPS)
        xh2 = res * r2
        row = lax.broadcasted_iota(jnp.int32, (tm, 1), 0) + i * tm
        real = row >= ROW0
        tgt = t_ref[...]
        tgt = jnp.where(i == 0, pltpu.roll(tgt, ROW0, 0), tgt)
        diff = jnp.where(real, xh2 * fg_ref[...] - tgt, 0.0)
        d_y = diff * (1.0 / D)
        d_xh2 = d_y * fg_ref[...]
        d_res = r2 * (d_xh2 - xh2 * jnp.mean(d_xh2 * xh2, axis=-1, keepdims=True))
        d_res_b = d_res.astype(BF16)
        d_merged = _dot_nt(d_res_b, w_out)
        d_yc = (d_merged * smc).astype(BF16)
        d_yr = (d_merged * smr).astype(BF16)
        dp_ref[:, D:2 * D] = (d_merged * y_conv * smc * (1.0 - smc)).astype(BF16)
        dp_ref[:, 2 * D:3 * D] = (d_merged * y_rec * smr * (1.0 - smr)).astype(BF16)
        d_ur = _dot_nt(d_yr, w_rec)
        d_og = d_ur * silu_g
        dp_ref[:, 0:D] = (d_ur * o_g * _dsilu(gr, sgr)).astype(BF16)
        d_on = d_og * gng_ref[...]
        for h in range(HEADS):
            sl = slice(h * HEAD_W, (h + 1) * HEAD_W)
            d_h = d_on[:, sl]
            n_h = o_n[:, sl]
            don_sc[:, sl] = r3[h] * (d_h - n_h * jnp.mean(d_h * n_h, axis=-1, keepdims=True))
        do_ref[...] = don_sc[...].astype(ACT)
        d_uc = _dot_nt(d_yc, w_conv)
        d_c2 = d_uc * silu_z
        dz_ref[...] = (d_uc * c2 * _dsilu(z, sz)).astype(BF16)
        d_c1 = d_c2 * _dsilu(c1, s1)
        d_xh = d_c1 * lng_ref[...]
        d_c0 = rstd * (d_xh - jnp.mean(d_xh, axis=-1, keepdims=True)
                       - xh * jnp.mean(d_xh * xh, axis=-1, keepdims=True))
        dc0_ref[...] = d_c0.astype(ACT)
        a3_ref[0] = u_conv
        b3_ref[0] = d_yc
        a3_ref[1] = u_rec
        b3_ref[1] = d_yr
        a3_ref[2] = merged
        b3_ref[2] = d_res_b
        def colsum(vv):
            return jnp.sum(vv, axis=0, keepdims=True)

        red_ref[0:1, :] += colsum(d_y * xh2)
        red_ref[1:2, :] += colsum(d_og * o_n)
        red_ref[2:3, :] += colsum(d_c1 * xh)
        red_ref[3:4, :] += colsum(d_c1)
        red_ref[4:5, :] += colsum(d_c0)
        red_ref[5:6, :] += colsum(diff * diff) * (0.5 / D)

    def row_block(width, col):
        return pl.BlockSpec((tm, width), lambda i: (i, col))

    def const_block(shape):
        return pl.BlockSpec(shape, lambda i: (0,) * len(shape))

    stack = jax.ShapeDtypeStruct((3, TP, D), BF16)
    stack_spec = pl.BlockSpec((3, tm, D), lambda i: (0, i, 0))
    return pl.pallas_call(
        body, name="mid", grid=(TP // tm,),
        in_specs=[row_block(D, 0),
                  pl.BlockSpec((pl.Element(tm), pl.Element(D)), lambda i: (_window_start(i, tm), 0)),
                  row_block(D, 0), row_block(D, 0),
                  row_block(D, 2), row_block(D, 6), row_block(D, 7), row_block(D, 8),
                  pl.BlockSpec((3, D, D), lambda i: (0, 0, 0), pipeline_mode=pl.Buffered(1)),
                  const_block((1, D)), const_block((1, D)), const_block((1, D)), const_block((1, D))],
        out_specs=(row_block(D, 0), row_block(D, 0), row_block(D, 0), row_block(3 * D, 2),
                   stack_spec, stack_spec, const_block((8, D))),
        out_shape=(jax.ShapeDtypeStruct((TP, D), ACT), jax.ShapeDtypeStruct((TP, D), ACT),
                   jax.ShapeDtypeStruct((TP, D), BF16),
                   jax.ShapeDtypeStruct((TP, D_IN), BF16), stack, stack, jax.ShapeDtypeStruct((8, D), F32)),
        scratch_shapes=[pltpu.VMEM((tm, D), F32), pltpu.VMEM((tm, D), F32)],
        compiler_params=pltpu.CompilerParams(dimension_semantics=("arbitrary",), vmem_limit_bytes=60 * 1024 * 1024),
    )(xin, tgt, o, c0, proj, proj, proj, proj, w3, ln_g, ln_b, gnorm_g, final_g)


def _conv_bwd(proj, d_c0, d_z, conv_w, dproj, chip1b):
    tm = TM_ELT
    n_tile = TP // tm
    lastt = n_tile - 1

    strip = tm // CONV_STRIPS

    def body(p_ref, dc_ref, dz_ref, w_ref, dproj_in, c1_ref, dp_ref, dw_ref, far_ref, dsh, a_sc, da_sc, acc,
             send_sems, recv_sems):
        del dproj_in
        i = pl.program_id(0)
        ride_start, ride_finish = _partials_to_owners(c1_ref, far_ref, send_sems, recv_sems)

        @pl.when(i == 0)
        def _():
            ride_start()
            dsh[0, :, tm:tm + HALO, :] = jnp.zeros((N_CB, HALO, HEAD_W), F32)
            acc[...] = jnp.zeros_like(acc)

        @pl.when(i > 0)
        def _():
            dsh[0, :, tm:tm + HALO, :] = dsh[0, :, 0:HALO, :]

        _store_by_cb(dsh, (0,), slice(0, tm), dc_ref[...].astype(F32))
        _fill_shifts(dsh, tm)
        ga = p_ref[:, 0:D].astype(F32)
        sb = _sigmoid(p_ref[:, D:2 * D].astype(F32))
        a = ga * sb
        _store_by_cb(a_sc, (), slice(0, tm), a)
        for cb in range(N_CB):
            cs = slice(cb * HEAD_W, (cb + 1) * HEAD_W)
            for st in range(CONV_STRIPS):
                rows = slice(st * strip, (st + 1) * strip)
                a_s = a_sc[cb, rows, :]
                d_a = jnp.zeros((strip, HEAD_W), F32)
                for j in range(CONV_K):
                    off = CONV_K - 1 - j
                    lo = st * strip + 8 * (off // 8)
                    slab = dsh[off % 8, cb, lo:lo + strip, :]
                    d_a = d_a + w_ref[j:j + 1, cs] * slab
                    acc[j, :, cs] += jnp.sum((a_s * slab).reshape(strip // 8, 8, HEAD_W), axis=0)
                da_sc[rows, cs] = d_a
        d_a = da_sc[...]
        dp_ref[:, 0:D] = (d_a * sb).astype(BF16)
        dp_ref[:, D:2 * D] = (d_a * a * (1.0 - sb)).astype(BF16)
        dp_ref[:, 2 * D:3 * D] = dz_ref[...]

        @pl.when(i == lastt)
        def _():
            for j in range(CONV_K):
                dw_ref[j:j + 1, :] = jnp.sum(acc[j], axis=0, keepdims=True)
            dw_ref[CONV_K:CONV_K + 1, :] = jnp.zeros((1, D), F32)
            ride_finish()

    return pl.pallas_call(
        body, name="conv_bwd", grid=(n_tile,),
        in_specs=[pl.BlockSpec((tm, 2 * D), lambda i: (lastt - i, 0)), pl.BlockSpec((tm, D), lambda i: (lastt - i, 0)),
                  pl.BlockSpec((tm, D), lambda i: (lastt - i, 0)), pl.BlockSpec((CONV_K, D), lambda i: (0, 0)), ANY, ANY],
        out_specs=(pl.BlockSpec((tm, 3 * D), lambda i: (lastt - i, 0)), pl.BlockSpec((CONV_K + 1, D), lambda i: (0, 0)),
                   ANY),
        out_shape=(jax.ShapeDtypeStruct((TP, D_IN), BF16), jax.ShapeDtypeStruct((CONV_K + 1, D), F32),
                   jax.ShapeDtypeStruct((3, 3, W_ROW_BLK, D), BF16)),
        scratch_shapes=[pltpu.VMEM((8, N_CB, tm + HALO, HEAD_W), F32), pltpu.VMEM((N_CB, tm, HEAD_W), F32),
                        pltpu.VMEM((tm, D), F32), pltpu.VMEM((CONV_K, 8, D), F32),
                        pltpu.SemaphoreType.DMA((3,)), pltpu.SemaphoreType.DMA((3,))],
        input_output_aliases={4: 0},
        compiler_params=pltpu.CompilerParams(dimension_semantics=("arbitrary",)),
    )(proj, d_c0, d_z, conv_w, dproj, chip1b)


def _rec_conv_bwd(proj, lb_logits, d_o, s_start, d_c0, d_z, conv_w, dproj, p3):
    cps = CHUNKS_PER_STEP
    tm = cps * CHUNK
    last = N_CHUNK // cps - 1
    n_strip = 4
    strip = tm // n_strip

    def body(p_ref, lbl_ref, do_ref, s_ref, pg_ref, dc_ref, dz_ref, w_ref, dproj_in, p3_ref,
             dp_ref, dlb_ref, dw_ref, land_ref,
             dst, bsc, dq_sc, dk_sc, g_sc, dsh, a_sc, da_sc, acc, send_sems, recv_sems):
        del dproj_in
        n = pl.program_id(0)
        ride_start, ride_finish = _p3_to_sibling(p3_ref, land_ref, send_sems, recv_sems)

        @pl.when(n == 0)
        def _():
            ride_start()
            dst[...] = jnp.zeros_like(dst)
            dlb_ref[...] = jnp.zeros_like(dlb_ref)
            dsh[0, :, tm:tm + HALO, :] = jnp.zeros((N_CB, HALO, HEAD_W), F32)
            acc[...] = jnp.zeros_like(acc)

        @pl.when(n > 0)
        def _():
            dsh[0, :, tm:tm + HALO, :] = dsh[0, :, 0:HALO, :]

        _store_by_cb(dsh, (0,), slice(0, tm), dc_ref[...].astype(F32))
        _fill_shifts(dsh, tm)
        ga = pg_ref[:, 0:D].astype(F32)
        sb = _sigmoid(pg_ref[:, D:2 * D].astype(F32))
        a = ga * sb
        _store_by_cb(a_sc, (), slice(0, tm), a)

        def conv_unit(cb, st):
            cs = slice(cb * HEAD_W, (cb + 1) * HEAD_W)
            rows = slice(st * strip, (st + 1) * strip)
            a_s = a_sc[cb, rows, :]
            d_a = jnp.zeros((strip, HEAD_W), F32)
            for j in range(CONV_K):
                off = CONV_K - 1 - j
                lo = st * strip + 8 * (off // 8)
                slab = dsh[off % 8, cb, lo:lo + strip, :]
                d_a = d_a + w_ref[j:j + 1, cs] * slab
                acc[j, :, cs] += jnp.sum((a_s * slab).reshape(strip // 8, 8, HEAD_W), axis=0)
            da_sc[rows, cs] = d_a

        units = [(cb, st) for cb in range(N_CB) for st in range(n_strip)]

        def prep(ci):
            g = _gates(p_ref.at[pl.ds(ci * CHUNK, CHUNK)], lbl_ref, (last - n) * cps + ci, bsc.at[ci])
            g["q1"] = (g["q"] * g["e_q"]).astype(BF16)
            qm_f = g["q"] * g["e_qm"]
            km_f = g["k"] * g["e_km"]
            g["qm"] = qm_f.astype(BF16)
            g["km"] = km_f.astype(BF16)
            g["qm_lo"] = (qm_f - g["qm"].astype(F32)).astype(BF16)
            g["km_lo"] = (km_f - g["km"].astype(F32)).astype(BF16)
            g["kh_f"] = g["k"] * g["e_kh"]
            g["kh"] = g["kh_f"].astype(BF16)
            return g

        def heads_and_post(ci, g):
            rs = pl.ds(ci * CHUNK, CHUNK)
            pv = p_ref.at[rs]
            dpv = dp_ref.at[rs]
            q1, qm, km, qm_lo, km_lo, kh_f, kh = (g[k] for k in ("q1", "qm", "km", "qm_lo", "km_lo", "kh_f", "kh"))
            for h in range(HEADS):
                if units:
                    conv_unit(*units.pop(0))
                sl = slice(h * HEAD_W, (h + 1) * HEAD_W)
                v = pv[:, 2 * D + h * HEAD_W:2 * D + (h + 1) * HEAD_W]
                d_oh = do_ref[rs, sl].astype(BF16)
                s0 = s_ref[ci, h]
                ds_end = dst[h]
                ds_end_b = ds_end.astype(BF16)
                att = jnp.where(g["causal"], _dot_nt(qm[:, sl], km[:, sl]), 0.0).astype(BF16)
                d_att = jnp.where(g["causal"], _dot_nt(d_oh, v), 0.0).astype(BF16)
                d_v = _dot_tn(att, d_oh) + _dot_nt(kh[:, sl], ds_end_b)
                d_qm2 = _dot(d_att, jnp.concatenate([km[:, sl], km_lo[:, sl]], axis=1))
                d_qm = d_qm2[:, 0:HEAD_W] + d_qm2[:, HEAD_W:2 * HEAD_W]
                d_q1 = _dot(d_oh, s0.astype(BF16))
                d_km2 = _dot_tn(d_att, jnp.concatenate([qm[:, sl], qm_lo[:, sl]], axis=1))
                d_km = d_km2[:, 0:HEAD_W] + d_km2[:, HEAD_W:2 * HEAD_W]
                d_kh = _dot(v, ds_end_b)
                dq_sc[ci, :, sl] = d_qm * g["e_qm"][:, sl] + d_q1 * g["e_q"][:, sl]
                dk_sc[ci, :, sl] = d_km * g["e_km"][:, sl] + d_kh * g["e_kh"][:, sl]
                g_sc[ci, :, sl] = (jnp.sum(kh_f[:, sl] * d_kh, axis=0, keepdims=True)
                                   + g["e_last"][:, sl] * jnp.sum(ds_end * s0, axis=0, keepdims=True))
                dst[h] = ds_end * g["e_last"][:, sl] + _dot_tn(d_oh, q1[:, sl])
                dpv[:, 5 * D + h * HEAD_W:5 * D + (h + 1) * HEAD_W] = d_v.astype(BF16)
            d_q = dq_sc[ci]
            d_k = dk_sc[ci]
            d_b = g["q"] * d_q - g["k"] * d_k
            anti = jnp.logical_not(g["causal"]) | (lax.broadcasted_iota(jnp.int32, (CHUNK, CHUNK), 0)
                                                    == lax.broadcasted_iota(jnp.int32, (CHUNK, CHUNK), 1))
            d_lf = _tri_matmul(anti.astype(BF16), d_b) + g_sc[ci]
            d_f = jnp.where(g["valid"], d_lf / g["f"] - d_k, 0.0)
            sg = g["sg"]
            dlb_ref[0:1, :] += jnp.sum(d_f * (1.0 - sg), axis=0, keepdims=True)
            dpv[:, 3 * D:4 * D] = (d_q * _dsilu(g["q_raw"], g["sq"])).astype(BF16)
            dpv[:, 4 * D:5 * D] = (d_f * (1.0 - g["lb"]) * sg * (1.0 - sg)).astype(BF16)

        ready = prep(cps - 1)
        for ci in reversed(range(cps)):
            coming = prep(ci - 1) if ci > 0 else None
            heads_and_post(ci, ready)
            ready = coming
        while units:
            conv_unit(*units.pop(0))

        d_a = da_sc[...]
        dp_ref[:, 0:D] = (d_a * sb).astype(BF16)
        dp_ref[:, D:2 * D] = (d_a * a * (1.0 - sb)).astype(BF16)
        dp_ref[:, 2 * D:3 * D] = dz_ref[...]

        @pl.when(n == last)
        def _():
            for j in range(CONV_K):
                dw_ref[j:j + 1, :] = jnp.sum(acc[j], axis=0, keepdims=True)
            dw_ref[CONV_K:CONV_K + 1, :] = jnp.zeros((1, D), F32)
            ride_finish()

    def rows_of(width, col):
        return pl.BlockSpec((tm, width), lambda n: (last - n, col))

    return pl.pallas_call(
        body, name="rec_conv_bwd", grid=(N_CHUNK // cps,),
        in_specs=[rows_of(3 * D, 1), pl.BlockSpec((2, D), lambda n: (0, 0)), rows_of(D, 0),
                  pl.BlockSpec((cps, HEADS, HEAD_W, HEAD_W), lambda n: (last - n, 0, 0, 0)),
                  rows_of(2 * D, 0), rows_of(D, 0), rows_of(D, 0), pl.BlockSpec((CONV_K, D), lambda n: (0, 0)), ANY, ANY],
        out_specs=(rows_of(6 * D, 0), pl.BlockSpec((8, D), lambda n: (0, 0)),
                   pl.BlockSpec((CONV_K + 1, D), lambda n: (0, 0)), ANY),
        out_shape=(jax.ShapeDtypeStruct((TP, D_IN), BF16), jax.ShapeDtypeStruct((8, D), F32),
                   jax.ShapeDtypeStruct((CONV_K + 1, D), F32), jax.ShapeDtypeStruct((4, 3, W_ROW_BLK, D), F32)),
        scratch_shapes=[pltpu.VMEM((HEADS, HEAD_W, HEAD_W), F32), pltpu.VMEM((cps, CHUNK, D), F32),
                        pltpu.VMEM((cps, CHUNK, D), F32), pltpu.VMEM((cps, CHUNK, D), F32),
                        pltpu.VMEM((cps, 1, D), F32),
                        pltpu.VMEM((8, N_CB, tm + HALO, HEAD_W), F32), pltpu.VMEM((N_CB, tm, HEAD_W), F32),
                        pltpu.VMEM((tm, D), F32), pltpu.VMEM((CONV_K, 8, D), F32),
                        pltpu.SemaphoreType.DMA((4,)), pltpu.SemaphoreType.DMA((4,))],
        input_output_aliases={8: 0},
        compiler_params=pltpu.CompilerParams(dimension_semantics=("arbitrary",)),
    )(proj, lb_logits, d_o, s_start, proj, d_c0, d_z, conv_w, dproj, p3)


def _wgrad3(a3, b3):
    tt = TT_WGRAD

    def body(a_ref, b_ref, o_ref):
        @pl.when(pl.program_id(1) == 0)
        def _():
            o_ref[...] = jnp.zeros_like(o_ref)

        o_ref[0] += _dot_tn(a_ref[0], b_ref[0])

    return pl.pallas_call(
        body, name="wgrad3", grid=(3, TP // tt),
        in_specs=[pl.BlockSpec((1, tt, D), lambda g, t: (g, t, 0)), pl.BlockSpec((1, tt, D), lambda g, t: (g, t, 0))],
        out_specs=pl.BlockSpec((1, D, D), lambda g, t: (g, 0, 0)),
        out_shape=jax.ShapeDtypeStruct((3, D, D), F32),
        compiler_params=pltpu.CompilerParams(dimension_semantics=("arbitrary", "arbitrary")),
    )(a3, b3)


def _wgrad_in(h, dproj, ids, chip1b):
    tt = TT_WGRAD
    n_t = TP // tt

    def body(ids_ref, a_ref, b_ref, c1_ref, o_ref, ob_ref, l0_ref, far_ref, acc, tmp, send_sems, recv_sems, tmp_sem,
             far_send_sems, far_recv_sems):
        del ids_ref
        r = pl.program_id(0)
        t = pl.program_id(1)
        x, y, c = _my_place()
        sibling = (x, y, 1 - c)
        slot = lax.rem(r, 2)
        ride_start, ride_finish = _partials_to_owners(c1_ref, far_ref, far_send_sems, far_recv_sems)

        @pl.when((r == 0) & (t == 0))
        def _():
            ride_start()

        def send_in(q):
            return pltpu.make_async_remote_copy(
                src_ref=acc.at[q % 2], dst_ref=l0_ref.at[q], send_sem=send_sems.at[q], recv_sem=recv_sems.at[q],
                device_id=sibling, device_id_type=MESH_ID)

        def landed(q):
            return pltpu.make_async_copy(l0_ref.at[q], tmp, tmp_sem)

        @pl.when(t == 0)
        def _():
            acc[slot] = jnp.zeros((D, W_IN_BLK), F32)

        acc[slot] += _dot_tn(a_ref[...], b_ref[...])

        for q in range(4):
            @pl.when((r == q) & (t == n_t - 1))
            def _(q=q):
                if q >= 1:
                    send_in(q - 1).wait_send()
                send_in(q).start()

            @pl.when((r == 4 + q) & (t == n_t - 2))
            def _(q=q):
                if q == 0:
                    send_in(3).wait_send()
                send_in(q).wait_recv()
                landed(q).start()

            @pl.when((r == 4 + q) & (t == n_t - 1))
            def _(q=q):
                landed(q).wait()
                tot = acc[q % 2] + tmp[...]
                o_ref[0] = tot
                ob_ref[0] = tot.astype(BF16)

        @pl.when((r == N_DEV - 1) & (t == n_t - 1))
        def _():
            ride_finish()

    blk = pl.BlockSpec((1, D, W_IN_BLK), lambda r, t, ids: (jnp.maximum(r - 4, 0), 0, 0))
    return pl.pallas_call(
        body, name="wgrad_in",
        grid_spec=pltpu.PrefetchScalarGridSpec(
            num_scalar_prefetch=1, grid=(N_DEV, n_t),
            in_specs=[pl.BlockSpec((tt, D), lambda r, t, ids: (t, 0)),
                      pl.BlockSpec((tt, W_IN_BLK), lambda r, t, ids: (t, ids[r])), ANY],
            out_specs=(blk, blk, ANY, ANY),
            scratch_shapes=[pltpu.VMEM((2, D, W_IN_BLK), F32), pltpu.VMEM((D, W_IN_BLK), F32),
                            pltpu.SemaphoreType.DMA((4,)), pltpu.SemaphoreType.DMA((4,)), pltpu.SemaphoreType.DMA,
                            pltpu.SemaphoreType.DMA((3,)), pltpu.SemaphoreType.DMA((3,))]),
        out_shape=(jax.ShapeDtypeStruct((4, D, W_IN_BLK), F32), jax.ShapeDtypeStruct((4, D, W_IN_BLK), BF16),
                   jax.ShapeDtypeStruct((4, D, W_IN_BLK), F32), jax.ShapeDtypeStruct((3, 3, W_ROW_BLK, D), BF16)),
        compiler_params=pltpu.CompilerParams(dimension_semantics=("arbitrary", "arbitrary")),
    )(ids, h, dproj, chip1b)


def _chip_sum_3(p3, land1, ids_mine):
    def body(ids_ref, p_ref, l_ref, o_ref, ob_ref):
        del ids_ref
        tot = p_ref[...] + l_ref[0]
        o_ref[0] = tot
        ob_ref[0] = tot.astype(BF16)

    blk = pl.BlockSpec((1, 3, W_ROW_BLK, D), lambda r, ids: (r, 0, 0, 0))
    return pl.pallas_call(
        body, name="chip_sum_3",
        grid_spec=pltpu.PrefetchScalarGridSpec(
            num_scalar_prefetch=1, grid=(4,),
            in_specs=[pl.BlockSpec((3, W_ROW_BLK, D), lambda r, ids: (0, ids[r], 0)), blk],
            out_specs=(blk, blk)),
        out_shape=(jax.ShapeDtypeStruct((4, 3, W_ROW_BLK, D), F32), jax.ShapeDtypeStruct((4, 3, W_ROW_BLK, D), BF16)),
    )(ids_mine, p3, land1)


def _dh_and_norm_bwd(dproj, w_in_full, xin, b3, norm_g, chip0b):
    tm = TM_MAT
    n_k = N_DEV // DH_K_BLKS
    n_m = TP // tm

    def body(dp_ref, w_ref, x_ref, dr_ref, g_ref, c0_ref, dx_ref, dg_ref, f0_ref, acc, send_sems, recv_sems):
        m = pl.program_id(0)
        k = pl.program_id(1)
        ride_start, ride_finish = _partials_to_owners(c0_ref, f0_ref, send_sems, recv_sems)

        @pl.when((m == 0) & (k == 0))
        def _():
            ride_start()

        @pl.when(k == 0)
        def _():
            acc[...] = jnp.zeros_like(acc)

        part = _dot_nt(dp_ref[:, 0:W_IN_BLK], w_ref[0])
        for j in range(1, DH_K_BLKS):
            part = part + _dot_nt(dp_ref[:, j * W_IN_BLK:(j + 1) * W_IN_BLK], w_ref[j])
        acc[...] += part

        @pl.when((k == n_k - 1) & (m == 0))
        def _():
            dg_ref[...] = jnp.zeros_like(dg_ref)

        @pl.when(k == n_k - 1)
        def _():
            xv = x_ref[...]
            r1 = lax.rsqrt(jnp.mean(xv * xv, axis=-1, keepdims=True) + EPS)
            xh = xv * r1
            d_h = acc[...]
            dg_ref[0:1, :] += jnp.sum(d_h * xh, axis=0, keepdims=True)
            d_xh = d_h * g_ref[...]
            dx_ref[...] = dr_ref[0].astype(F32) + r1 * (d_xh - xh * jnp.mean(d_xh * xh, axis=-1, keepdims=True))

        @pl.when((m == n_m - 1) & (k == n_k - 1))
        def _():
            ride_finish()

    return pl.pallas_call(
        body, name="dh_norm_bwd", grid=(n_m, n_k),
        in_specs=[pl.BlockSpec((tm, DH_K_BLKS * W_IN_BLK), lambda m, k: (m, k)),
                  pl.BlockSpec((DH_K_BLKS, D, W_IN_BLK), lambda m, k: (k, 0, 0)),
                  pl.BlockSpec((tm, D), lambda m, k: (m, 0)), pl.BlockSpec((1, tm, D), lambda m, k: (2, m, 0)),
                  pl.BlockSpec((1, D), lambda m, k: (0, 0)), ANY],
        out_specs=(pl.BlockSpec((tm, D), lambda m, k: (m, 0)), pl.BlockSpec((8, D), lambda m, k: (0, 0)), ANY),
        out_shape=(jax.ShapeDtypeStruct((TP, D), F32), jax.ShapeDtypeStruct((8, D), F32),
                   jax.ShapeDtypeStruct((3, D, W_IN_BLK), BF16)),
        scratch_shapes=[pltpu.VMEM((tm, D), F32), pltpu.SemaphoreType.DMA((3,)), pltpu.SemaphoreType.DMA((3,))],
        compiler_params=pltpu.CompilerParams(dimension_semantics=("arbitrary", "arbitrary")),
    )(dproj, w_in_full, xin, b3, norm_g, chip0b)


def _sum_adamw(own, landed, w, m, v, tr, name):
    rows, cols = w.shape
    n_t = rows // tr

    def body(o_ref, l1_ref, l2_ref, l3_ref, w_ref, m_ref, v_ref, g_ref, d_ref, m2_ref, v2_ref):
        g = ((o_ref[...] + l1_ref[...].astype(F32)) + l2_ref[...].astype(F32)) + l3_ref[...].astype(F32)
        delta, m2, v2 = _adamw(w_ref[...], g, m_ref[...], v_ref[...])
        g_ref[...] = g
        d_ref[...] = delta
        m2_ref[...] = m2
        v2_ref[...] = v2

    def spec(k):
        return pl.BlockSpec((tr, cols), lambda i: (i + k * n_t, 0))

    out = jax.ShapeDtypeStruct((rows, cols), F32)
    return pl.pallas_call(
        body, name=name, grid=(n_t,),
        in_specs=[spec(0), spec(0), spec(1), spec(2), spec(0), spec(0), spec(0)],
        out_specs=(spec(0),) * 4, out_shape=(out,) * 4,
    )(own, landed, landed, landed, w, m, v)


def _adamw_3(chip1, far1, ws, ms, vs):
    def body(c_ref, f_ref, *refs):
        w_refs, m_refs, v_refs, outs = refs[0:3], refs[3:6], refs[6:9], refs[9:21]
        for k in range(3):
            g = ((c_ref[0, k] + f_ref[0, k].astype(F32)) + f_ref[1, k].astype(F32)) + f_ref[2, k].astype(F32)
            delta, m2, v2 = _adamw(w_refs[k][0], g, m_refs[k][0], v_refs[k][0])
            for kind, val in enumerate((g, delta, m2, v2)):
                outs[3 * kind + k][0] = val

    full = pl.BlockSpec((1, W_ROW_BLK, D), lambda i: (0, 0, 0))
    out = jax.ShapeDtypeStruct((1, W_ROW_BLK, D), F32)
    res = pl.pallas_call(
        body, name="adamw_3", grid=(1,),
        in_specs=[pl.BlockSpec((1, 3, W_ROW_BLK, D), lambda i: (0, 0, 0, 0)),
                  pl.BlockSpec((3, 3, W_ROW_BLK, D), lambda i: (0, 0, 0, 0))] + [full] * 9,
        out_specs=(full,) * 12, out_shape=(out,) * 12,
    )(chip1, far1, *ws, *ms, *vs)
    return tuple(res[3 * kind:3 * kind + 3] for kind in range(4))


N_SMALL = 9


def _small_update(pack_all, srs_all, ws, ms, vs):
    def body(pk_ref, sr_ref, *refs):
        w_refs, m_refs, v_refs = refs[0:N_SMALL], refs[N_SMALL:2 * N_SMALL], refs[2 * N_SMALL:3 * N_SMALL]
        loss_ref = refs[3 * N_SMALL]
        outs = refs[3 * N_SMALL + 1:7 * N_SMALL + 1]
        tot_sc, tots_sc = refs[7 * N_SMALL + 1:]
        tot = pk_ref[0]
        tot_s = sr_ref[0]
        for d in range(1, N_DEV):
            tot = tot + pk_ref[d]
            tot_s = tot_s + sr_ref[d]
        tot_sc[...] = tot
        tots_sc[...] = tot_s
        loss_ref[...] = jnp.sum(tot_sc[5:6, :], axis=1, keepdims=True)
        lbl = w_refs[4]
        p0 = _sigmoid(lbl[0:1, :] - lbl[1:2, :])
        d_l0 = tot_sc[4:5, :] * p0 * (1.0 - p0)

        def update(k, sel, g):
            delta, m2, v2 = _adamw(w_refs[k][sel], g, m_refs[k][sel], v_refs[k][sel])
            for kind, val in enumerate((g, delta, m2, v2)):
                outs[N_SMALL * kind + k][sel] = val

        everything = (slice(None), slice(None))
        for k, row in ((0, 0), (1, 1), (2, 2), (3, 3), (5, 6), (6, 7)):
            update(k, everything, tot_sc[row:row + 1, :])
        update(4, (slice(0, 1), slice(None)), d_l0)
        update(4, (slice(1, 2), slice(None)), -d_l0)
        update(7, (0, slice(None), slice(None)), tots_sc[0:CONV_K, :])
        update(8, everything, tots_sc[META_ROW:META_ROW + N_META, :])

    shapes = [jax.ShapeDtypeStruct(w.shape, F32) for w in ws]
    res = pl.pallas_call(
        body, name="small_update",
        out_shape=(jax.ShapeDtypeStruct((1, 1), F32), *(shapes * 4)),
        scratch_shapes=[pltpu.VMEM((8, D), F32), pltpu.VMEM((SMALL_ROWS, HEAD_W), F32)],
    )(pack_all, srs_all, *ws, *ms, *vs)
    return res[0], tuple(res[1 + N_SMALL * kind:1 + N_SMALL * (kind + 1)] for kind in range(4))


def _local_step(xin, proj, target, conv_w_full, conv_b, ln_g, ln_b, w3_b, lb_logits, gnorm_g, final_g, ids_mine):
    fg = final_g.reshape(1, D)
    c0, o, s_start, w3_full = _rec_conv_fwd(proj, lb_logits, conv_w_full, conv_b, w3_b)
    d_o, d_c0, d_z, dproj, a3, b3, red = _mid(xin, target, o, c0, proj, w3_full, ln_g, ln_b, gnorm_g, fg)
    p3 = _wgrad3(a3, b3)
    dproj, dlb, d_conv_w, land1 = _rec_conv_bwd(proj, lb_logits, d_o, s_start, d_c0, d_z, conv_w_full, dproj, p3)
    chip1, chip1b = _chip_sum_3(p3, land1, ids_mine)
    return dproj, b3, p3, chip1, chip1b, d_conv_w, red, dlb


def kernel(x, meta_tokens, norm_g, w_in, conv_w, conv_b, ln_g, ln_b, w_conv_out, lb_logits, gnorm_g, w_rec_out, w_out, final_g, loss_target, m_meta_tokens, m_norm_g, m_w_in, m_conv_w, m_conv_b, m_ln_g, m_ln_b, m_w_conv_out, m_lb_logits, m_gnorm_g, m_w_rec_out, m_w_out, m_final_g, v_meta_tokens, v_norm_g, v_w_in, v_conv_w, v_conv_b, v_ln_g, v_ln_b, v_w_conv_out, v_lb_logits, v_gnorm_g, v_w_rec_out, v_w_out, v_final_g):
    mx, my, mc = _my_place()

    ws_s = jnp.concatenate([conv_w[0], jnp.zeros((1, HEAD_W), F32), meta_tokens], axis=0)
    small_full = jnp.transpose(_gather_small(ws_s), (1, 0, 2)).reshape(SMALL_ROWS, D)
    conv_w_full = small_full[0:CONV_K]
    meta_full = small_full[META_ROW:META_ROW + N_META]
    w_in_b, w3_b = _cast_shards(w_in[0], w_conv_out, w_rec_out, w_out)
    first, second, diag = _gather_chips(mx, my, mc)
    use_order = [(mx, my, mc), (mx, my, 1 - mc), (*first, mc), (*second, 1 - mc), (*second, mc), (*first, 1 - mc),
                 (*diag, mc), (*diag, 1 - mc)]
    order = jnp.stack([_dev_index(*p) for p in use_order]).astype(jnp.int32)
    proj, xin, h, w_in_full = _gather_and_proj(x[0], meta_full, norm_g, w_in_b, order)
    h = h.reshape(TP, D)

    ids_mine = jnp.stack([_dev_index(*_chip_rel(mx, my, r), mc) for r in range(4)]).astype(jnp.int32)
    ids_sib = jnp.stack([_dev_index(*_chip_rel(mx, my, r), 1 - mc) for r in range(4)]).astype(jnp.int32)
    dproj, b3, _, chip1, chip1b, d_conv_w, red, dlb = _local_step(
        xin, proj, loss_target[0], conv_w_full, conv_b, ln_g, ln_b, w3_b, lb_logits, gnorm_g, final_g, ids_mine)

    chip0, chip0b, _, far1 = _wgrad_in(h, dproj, jnp.concatenate([ids_sib, ids_mine]), chip1b)
    d_xin, dng, far0 = _dh_and_norm_bwd(dproj, w_in_full, xin, b3, norm_g, chip0b)
    pack = jnp.concatenate([dng[0:1], red[4:5], red[2:3], red[3:4], dlb[0:1], red[5:6], red[1:2], red[0:1]], axis=0)
    g_in, d_in, m_in, v_in = _sum_adamw(chip0.reshape(4 * D, W_IN_BLK), far0.reshape(3 * D, W_IN_BLK), w_in[0],
                                        m_w_in[0], v_w_in[0], 256, "adamw_in")
    big3 = _adamw_3(chip1, far1, (w_conv_out, w_rec_out, w_out), (m_w_conv_out, m_w_rec_out, m_w_out),
                    (v_w_conv_out, v_w_rec_out, v_w_out))

    srs = jnp.concatenate([d_conv_w, d_xin[PAD_FRONT:ROW0]], axis=0)
    srs = jnp.transpose(srs.reshape(SMALL_ROWS, N_DEV, HEAD_W), (1, 0, 2))
    pack_all, srs_all = _exchange_small(pack, srs)
    loss, small = _small_update(
        pack_all, srs_all,
        (norm_g, conv_b, ln_g, ln_b, lb_logits, gnorm_g, final_g.reshape(1, D), conv_w, meta_tokens),
        (m_norm_g, m_conv_b, m_ln_g, m_ln_b, m_lb_logits, m_gnorm_g, m_final_g.reshape(1, D), m_conv_w, m_meta_tokens),
        (v_norm_g, v_conv_b, v_ln_g, v_ln_b, v_lb_logits, v_gnorm_g, v_final_g.reshape(1, D), v_conv_w, v_meta_tokens))

    outs = [loss.reshape(()), d_xin[ROW0:][None]]
    for kind, a_in in enumerate((g_in, d_in, m_in, v_in)):
        ng, cb, lg, lb_, lbl, gg, fg, cw, mt = small[kind]
        a_3 = big3[kind]
        outs += [mt, ng, a_in[None], cw, cb, lg, lb_, a_3[0], lbl, gg, a_3[1], a_3[2], fg.reshape(D)]
    return tuple(outs)
```

```python
import jax
import jax.numpy as jnp
from jax import lax
from jax.experimental import pallas as pl
from jax.experimental.pallas import tpu as pltpu

F32 = jnp.float32
BF16 = jnp.bfloat16
ACT = BF16

D = 1024
SEQ = 4096
N_META = 16
CHUNK = 64
PAD_FRONT = 48
ROW0 = PAD_FRONT + N_META
TP = ROW0 + SEQ
N_CHUNK = TP // CHUNK
HEADS = 8
HEAD_W = 128
D_IN = 9 * D
N_DEV = 8
W_IN_BLK = D_IN // N_DEV
W_ROW_BLK = D // N_DEV
CONV_K = 31
SMALL_ROWS = 48
META_ROW = 32
EPS = 1e-6
HALO = 32

TM_MAT = 1040
TT_WGRAD = 2080
DH_K_BLKS = 2
TM_ELT = 208
CHUNKS_PER_STEP = 5
CONV_STRIPS = 4

ADAM_LR = 0.001
ADAM_B1 = 0.9
ADAM_B2 = 0.999
ADAM_EPS = 1e-08
ADAM_WD = 0.01
ADAM_STEP = 10

MESH_ID = pl.DeviceIdType.MESH
ANY = pl.BlockSpec(memory_space=pl.ANY)


def _sigmoid(v):
    return jax.nn.sigmoid(v)


def _dsilu(v, s):
    return s * (1.0 + v * (1.0 - s))


def _dot(a, b):
    return jnp.dot(a, b, preferred_element_type=F32)


def _dot_nt(a, b):
    return lax.dot_general(a, b, (((1,), (1,)), ((), ())), preferred_element_type=F32)


def _dot_tn(a, b):
    return lax.dot_general(a, b, (((0,), (0,)), ((), ())), preferred_element_type=F32)


def _split3(v):
    hi = v.astype(BF16)
    r1 = v - hi.astype(F32)
    mid = r1.astype(BF16)
    lo = (r1 - mid.astype(F32)).astype(BF16)
    return hi, mid, lo


def _tri_matmul(tri, v):
    hi, mid, lo = _split3(v)
    return _dot(tri, hi) + _dot(tri, mid) + _dot(tri, lo)


def _adamw(w, g, m, v):
    m2 = ADAM_B1 * m + (1.0 - ADAM_B1) * g
    v2 = ADAM_B2 * v + (1.0 - ADAM_B2) * jnp.square(g)
    m_hat = m2 / (1.0 - ADAM_B1 ** ADAM_STEP)
    v_hat = v2 / (1.0 - ADAM_B2 ** ADAM_STEP)
    delta = -ADAM_LR * (m_hat / (jnp.sqrt(v_hat) + ADAM_EPS) + ADAM_WD * w)
    return delta, m2, v2


def _window_start(i, tm):
    assert tm % 16 == 0 and ROW0 % 16 == 0
    return pl.multiple_of(16 * jnp.maximum((tm // 16) * i - ROW0 // 16, 0), 16)


def _my_place():
    return lax.axis_index("x"), lax.axis_index("y"), lax.axis_index("c")


def _dev_index(px, py, pc):
    return 4 * px + 2 * py + pc


def _cast_shards(w_in_s, w_conv_s, w_rec_s, w_out_s):
    def body(a_ref, c_ref, r_ref, o_ref, oa_ref, ob_ref):
        oa_ref[...] = a_ref[...].astype(BF16)
        for k, ref in enumerate((c_ref, r_ref, o_ref)):
            ob_ref[k] = ref[0].astype(BF16)

    return pl.pallas_call(
        body, name="cast_shards",
        out_shape=(jax.ShapeDtypeStruct(w_in_s.shape, BF16), jax.ShapeDtypeStruct((3, W_ROW_BLK, D), BF16)),
    )(w_in_s, w_conv_s, w_rec_s, w_out_s)


def _peer(x, y, c, r):
    return (jnp.bitwise_xor(x, (r >> 2) & 1), jnp.bitwise_xor(y, (r >> 1) & 1), jnp.bitwise_xor(c, r & 1))


def _gather_small(small_s):
    def body(s_ref, o_ref, send_sems, recv_sems, local_sem):
        x, y, c = _my_place()
        my_id = _dev_index(x, y, c)
        mine = pltpu.make_async_copy(s_ref, o_ref.at[my_id], local_sem)
        mine.start()
        copies = []
        for r in range(1, N_DEV):
            cp = pltpu.make_async_remote_copy(
                src_ref=s_ref, dst_ref=o_ref.at[my_id], send_sem=send_sems.at[r - 1], recv_sem=recv_sems.at[r - 1],
                device_id=_peer(x, y, c, r), device_id_type=MESH_ID)
            cp.start()
            copies.append(cp)
        for cp in copies:
            cp.wait_recv()
        for cp in copies:
            cp.wait_send()
        mine.wait()

    return pl.pallas_call(
        body, name="gather_small", out_shape=jax.ShapeDtypeStruct((N_DEV,) + small_s.shape, F32),
        in_specs=[ANY], out_specs=ANY,
        scratch_shapes=[pltpu.SemaphoreType.DMA((7,)), pltpu.SemaphoreType.DMA((7,)), pltpu.SemaphoreType.DMA],
    )(small_s)


def _w3_gather(src, out, stage, send_sems, recv_sems, local_sems):
    x, y, c = _my_place()
    me, sibling = (x, y, c), (x, y, 1 - c)
    chips = [(1 - x, y), (x, 1 - y), (1 - x, 1 - y)]

    def block(place):
        d = _dev_index(*place)
        return out.at[:, pl.ds(pl.multiple_of(d * W_ROW_BLK, W_ROW_BLK), W_ROW_BLK), :]

    def copy(k, place, to, from_src=False):
        return pltpu.make_async_remote_copy(
            src_ref=src if from_src else block(place), dst_ref=block(place),
            send_sem=send_sems.at[k], recv_sem=recv_sems.at[k], device_id=to, device_id_type=MESH_ID)

    own_in = pltpu.make_async_copy(src, stage, local_sems.at[0])
    own_out = pltpu.make_async_copy(stage, block(me), local_sems.at[1])

    def start():
        copy(0, me, sibling, from_src=True).start()
        for j, chip in enumerate(chips):
            copy(1 + j, me, (*chip, c), from_src=True).start()
        own_in.start()
        own_in.wait()
        own_out.start()

    def finish():
        for j, chip in enumerate(chips):
            copy(1 + j, (*chip, c), me).wait_recv()
            copy(4 + j, (*chip, c), sibling).start()
        copy(0, sibling, me).wait_recv()
        for j, chip in enumerate(chips):
            copy(4 + j, (*chip, 1 - c), me).wait_recv()
        for k in range(7):
            copy(k, me, me).wait_send()
        own_out.wait()

    return start, finish


def _p3_to_sibling(p3_ref, land_ref, send_sems, recv_sems):
    x, y, c = _my_place()

    def cp(q):
        d = _dev_index(*_chip_rel(x, y, q), 1 - c)
        return pltpu.make_async_remote_copy(
            src_ref=p3_ref.at[:, pl.ds(pl.multiple_of(d * W_ROW_BLK, W_ROW_BLK), W_ROW_BLK), :],
            dst_ref=land_ref.at[q], send_sem=send_sems.at[q], recv_sem=recv_sems.at[q],
            device_id=(x, y, 1 - c), device_id_type=MESH_ID)

    def start():
        for q in range(4):
            cp(q).start()

    def finish():
        for q in range(4):
            cp(q).wait_recv()
        for q in range(4):
            cp(q).wait_send()

    return start, finish


def _partials_to_owners(src_ref, far_ref, send_sems, recv_sems):
    x, y, c = _my_place()

    def cp(q):
        return pltpu.make_async_remote_copy(
            src_ref=src_ref.at[q], dst_ref=far_ref.at[q - 1], send_sem=send_sems.at[q - 1],
            recv_sem=recv_sems.at[q - 1], device_id=(*_chip_rel(x, y, q), c), device_id_type=MESH_ID)

    def start():
        for q in range(1, 4):
            cp(q).start()

    def finish():
        for q in range(1, 4):
            cp(q).wait_recv()
        for q in range(1, 4):
            cp(q).wait_send()

    return start, finish


def _gather_chips(x, y, c):
    first = (jnp.bitwise_xor(x, 1 - c), jnp.bitwise_xor(y, c))
    second = (jnp.bitwise_xor(x, c), jnp.bitwise_xor(y, 1 - c))
    return [first, second, (1 - x, 1 - y)]


def _gather_and_proj(x_seq, meta_full, norm_g, w_in_b, order):
    tm = TM_MAT
    n_m = TP // tm
    last_m = n_m - 1

    def body(order_ref, x_ref, meta_ref, g_ref, s0, proj_ref, xin_ref, h_out, o0, hbuf, wbuf, send_sems, recv_sems,
             local_sems):
        del order_ref
        n = pl.program_id(0)
        m = pl.program_id(1)
        x, y, c = _my_place()
        me, sibling = (x, y, c), (x, y, 1 - c)
        chips = _gather_chips(x, y, c)

        def block(place):
            return o0.at[_dev_index(*place)]

        def copy(k, place, to, from_src=False):
            return pltpu.make_async_remote_copy(
                src_ref=s0 if from_src else block(place), dst_ref=block(place),
                send_sem=send_sems.at[k], recv_sem=recv_sems.at[k], device_id=to, device_id_type=MESH_ID)

        def to_vmem(place, slot):
            return pltpu.make_async_copy(block(place), wbuf.at[slot], local_sems.at[slot])

        own_out = pltpu.make_async_copy(wbuf.at[0], block(me), local_sems.at[2])
        h_copy = pltpu.make_async_copy(hbuf, h_out, local_sems.at[3])

        @pl.when((n == 0) & (m == 0))
        def _():
            copy(0, me, sibling, from_src=True).start()
            for j, chip in enumerate(chips[0:2]):
                copy(1 + j, me, (*chip, c), from_src=True).start()
            mine = pltpu.make_async_copy(s0, wbuf.at[0], local_sems.at[0])
            mine.start()
            mine.wait()
            own_out.start()

        @pl.when(n == 0)
        def _():
            xv = x_ref[...]
            xin_ref[...] = jnp.where(m == 0, pltpu.roll(xv, ROW0, 0), xv)

            @pl.when(m == 0)
            def _():
                xin_ref[0:PAD_FRONT, :] = jnp.zeros((PAD_FRONT, D), F32)
                xin_ref[PAD_FRONT:ROW0, :] = meta_ref[...]

            xv = xin_ref[...]
            r = lax.rsqrt(jnp.mean(xv * xv, axis=-1, keepdims=True) + EPS)
            hbuf[m] = (xv * r * g_ref[...]).astype(BF16)

        between = [4 + c, 5 - c, 6]
        first, second, diag = chips
        plan = [(sibling, (0, sibling), None),
                ((*first, c), (1, (*first, c)), between[0]),
                ((*second, 1 - c), (between[1], (*second, 1 - c)), None),
                ((*second, c), (2, (*second, c)), between[1]),
                ((*first, 1 - c), (between[0], (*first, 1 - c)), None),
                ((*diag, c), (3, (*diag, c)), between[2]),
                ((*diag, 1 - c), (between[2], (*diag, 1 - c)), None)]

        for s, (place, (k, origin), pass_on) in enumerate(plan, start=1):
            @pl.when((n == s - 1) & (m == last_m))
            def _(s=s, place=place, k=k, origin=origin, pass_on=pass_on):
                copy(k, origin, me).wait_recv()
                if pass_on is not None:
                    copy(pass_on, place, sibling).start()
                if s == 2:
                    copy(3, place, (*chips[1], c)).start()
                    own_out.wait()
                to_vmem(place, s % 2).start()

            @pl.when((n == s) & (m == 0))
            def _(s=s, place=place):
                to_vmem(place, s % 2).wait()

        proj_ref[...] = _dot(hbuf[m], wbuf[lax.rem(n, 2)]).astype(BF16)

        @pl.when((n == 0) & (m == last_m))
        def _():
            h_copy.start()

        @pl.when((n == N_DEV - 1) & (m == last_m))
        def _():
            for k in range(7):
                copy(k, me, me).wait_send()
            h_copy.wait()

    return pl.pallas_call(
        body, name="gather_and_proj",
        grid_spec=pltpu.PrefetchScalarGridSpec(
            num_scalar_prefetch=1, grid=(N_DEV, n_m),
            in_specs=[pl.BlockSpec((pl.Element(tm), pl.Element(D)),
                                   lambda n, m, o: (_window_start(jnp.where(n == 0, m, 0), tm), 0)),
                      pl.BlockSpec((N_META, D), lambda n, m, o: (0, 0)),
                      pl.BlockSpec((1, D), lambda n, m, o: (0, 0)), ANY],
            out_specs=(pl.BlockSpec((tm, W_IN_BLK), lambda n, m, o: (m, o[n])),
                       pl.BlockSpec((tm, D), lambda n, m, o: (jnp.where(n == 0, m, last_m), 0)), ANY, ANY),
            scratch_shapes=[pltpu.VMEM((n_m, tm, D), BF16), pltpu.VMEM((2, D, W_IN_BLK), BF16),
                            pltpu.SemaphoreType.DMA((7,)), pltpu.SemaphoreType.DMA((7,)),
                            pltpu.SemaphoreType.DMA((4,))]),
        out_shape=(jax.ShapeDtypeStruct((TP, D_IN), BF16), jax.ShapeDtypeStruct((TP, D), F32),
                   jax.ShapeDtypeStruct((n_m, tm, D), BF16), jax.ShapeDtypeStruct((N_DEV, D, W_IN_BLK), BF16)),
        compiler_params=pltpu.CompilerParams(dimension_semantics=("arbitrary", "arbitrary")),
    )(order, x_seq, meta_full, norm_g, w_in_b)


def _chip_rel(x, y, r):
    return (jnp.bitwise_xor(x, r >> 1), jnp.bitwise_xor(y, r & 1))


def _exchange_small(pack, srs):
    def body(pk, sr, pk_all, sr_all, send_sems, recv_sems, local_sems):
        x, y, c = _my_place()
        my_id = _dev_index(x, y, c)
        mine = [pltpu.make_async_copy(pk, pk_all.at[my_id], local_sems.at[0]),
                pltpu.make_async_copy(sr.at[my_id], sr_all.at[my_id], local_sems.at[1])]
        for cp in mine:
            cp.start()
        copies = []
        for r in range(1, N_DEV):
            peer = (jnp.bitwise_xor(x, (r >> 2) & 1), jnp.bitwise_xor(y, (r >> 1) & 1), jnp.bitwise_xor(c, r & 1))
            peer_id = _dev_index(*peer)
            for a, (src, dst) in enumerate(((pk, pk_all.at[my_id]), (sr.at[peer_id], sr_all.at[my_id]))):
                cp = pltpu.make_async_remote_copy(
                    src_ref=src, dst_ref=dst, send_sem=send_sems.at[a * 7 + r - 1], recv_sem=recv_sems.at[a * 7 + r - 1],
                    device_id=peer, device_id_type=MESH_ID)
                cp.start()
                copies.append(cp)
        for cp in copies:
            cp.wait_recv()
        for cp in copies:
            cp.wait_send()
        for cp in mine:
            cp.wait()

    return pl.pallas_call(
        body, name="exchange_small",
        out_shape=(jax.ShapeDtypeStruct((N_DEV,) + pack.shape, F32), jax.ShapeDtypeStruct(srs.shape, F32)),
        in_specs=[ANY, ANY], out_specs=(ANY, ANY),
        scratch_shapes=[pltpu.SemaphoreType.DMA((14,)), pltpu.SemaphoreType.DMA((14,)), pltpu.SemaphoreType.DMA((2,))],
    )(pack, srs)


N_CB = D // HEAD_W


def _store_by_cb(ref, idx, rows, val):
    for cb in range(N_CB):
        ref[(*idx, cb, rows, slice(None))] = val[:, cb * HEAD_W:(cb + 1) * HEAD_W]


def _fill_shifts(sh, tm):
    n = tm + HALO - 8
    for s in range(1, 8):
        for cb in range(N_CB):
            sh[s, cb, 0:n, :] = sh[0, cb, s:s + n, :]


def _gates(p_ref, lbl_ref, chunk, bsc):
    lb = _sigmoid(lbl_ref[0:1, :] - lbl_ref[1:2, :])
    q_raw = p_ref[:, 0:D].astype(F32)
    f_raw = p_ref[:, D:2 * D].astype(F32)
    sq = _sigmoid(q_raw)
    q = q_raw * sq
    sg = _sigmoid(f_raw)
    f = lb + (1.0 - lb) * sg
    row = lax.broadcasted_iota(jnp.int32, (CHUNK, 1), 0) + chunk * CHUNK
    valid = row >= PAD_FRONT
    lf = jnp.where(valid, jnp.log(f), 0.0)
    k = jnp.where(valid, 1.0 - f, 0.0)
    r_i = lax.broadcasted_iota(jnp.int32, (CHUNK, CHUNK), 0)
    c_i = lax.broadcasted_iota(jnp.int32, (CHUNK, CHUNK), 1)
    causal = r_i >= c_i
    bsc[...] = _tri_matmul(causal.astype(BF16), lf)
    b = bsc[...]
    b_mid = bsc[CHUNK // 2 - 1:CHUNK // 2, :]
    b_last = bsc[CHUNK - 1:CHUNK, :]
    e_q = jnp.exp(b)
    e_qm = jnp.exp(b - b_mid)
    e_km = jnp.exp(b_mid - b)
    e_kh = jnp.exp(b_last - b)
    e_last = jnp.exp(b_last)
    return dict(lb=lb, q_raw=q_raw, sq=sq, q=q, sg=sg, f=f, k=k, valid=valid, causal=causal,
                e_q=e_q, e_qm=e_qm, e_km=e_km, e_kh=e_kh, e_last=e_last)


def _rec_conv_fwd(proj, lb_logits, conv_w, conv_b, w3_b):
    cps = CHUNKS_PER_STEP
    tm = cps * CHUNK
    n_strip = CONV_STRIPS
    strip = tm // n_strip

    def body(p_ref, lbl_ref, pg_ref, w_ref, b_ref, w3s_ref, c0_ref, o_ref, s_ref, w3o_ref,
             st, bsc, sh, c0_sc, w3buf, send_sems, recv_sems, local_sems):
        n = pl.program_id(0)
        gather_start, gather_finish = _w3_gather(w3s_ref, w3o_ref, w3buf, send_sems, recv_sems, local_sems)

        @pl.when(n == 0)
        def _():
            st[...] = jnp.zeros_like(st)
            sh[0, :, 0:HALO, :] = jnp.zeros((N_CB, HALO, HEAD_W), F32)
            gather_start()

        @pl.when(n > 0)
        def _():
            sh[0, :, 0:HALO, :] = sh[0, :, tm:tm + HALO, :]

        ga = pg_ref[:, 0:D].astype(F32)
        gb = pg_ref[:, D:2 * D].astype(F32)
        _store_by_cb(sh, (0,), slice(HALO, HALO + tm), ga * _sigmoid(gb))
        _fill_shifts(sh, tm)

        def conv_unit(cb, s_i):
            cs = slice(cb * HEAD_W, (cb + 1) * HEAD_W)
            acc = jnp.broadcast_to(b_ref[:, cs], (strip, HEAD_W))
            for j in range(CONV_K):
                off = HALO - (CONV_K - 1) + j
                lo = s_i * strip + 8 * (off // 8)
                acc = acc + w_ref[j:j + 1, cs] * sh[off % 8, cb, lo:lo + strip, :]
            c0_sc[s_i * strip:(s_i + 1) * strip, cs] = acc

        units = [(cb, s_i) for cb in range(N_CB) for s_i in range(n_strip)]

        def prep(ci):
            g = _gates(p_ref.at[pl.ds(ci * CHUNK, CHUNK)], lbl_ref, n * cps + ci, bsc.at[ci])
            g["q1"] = (g["q"] * g["e_q"]).astype(BF16)
            g["qm"] = (g["q"] * g["e_qm"]).astype(BF16)
            g["km"] = (g["k"] * g["e_km"]).astype(BF16)
            g["kh"] = (g["k"] * g["e_kh"]).astype(BF16)
            return g

        def heads(ci, g):
            rs = pl.ds(ci * CHUNK, CHUNK)
            pv = p_ref.at[rs]
            s_ref[ci] = st[...]
            for h in range(HEADS):
                if units:
                    conv_unit(*units.pop(0))
                sl = slice(h * HEAD_W, (h + 1) * HEAD_W)
                v = pv[:, 2 * D + h * HEAD_W:2 * D + (h + 1) * HEAD_W]
                att = jnp.where(g["causal"], _dot_nt(g["qm"][:, sl], g["km"][:, sl]), 0.0).astype(BF16)
                s_h = st[h]
                o_ref[rs, sl] = (_dot_nt(g["q1"][:, sl], s_h.astype(BF16)) + _dot(att, v)).astype(ACT)
                st[h] = s_h * g["e_last"][:, sl] + _dot_tn(v, g["kh"][:, sl])

        ready = prep(0)
        for ci in range(cps):
            coming = prep(ci + 1) if ci + 1 < cps else None
            heads(ci, ready)
            ready = coming
        while units:
            conv_unit(*units.pop(0))
        c0_ref[...] = c0_sc[...].astype(ACT)

        @pl.when(n == N_CHUNK // cps - 1)
        def _():
            gather_finish()

    def rows_of(width, col):
        return pl.BlockSpec((tm, width), lambda n: (n, col))

    return pl.pallas_call(
        body, name="rec_conv_fwd", grid=(N_CHUNK // cps,),
        in_specs=[rows_of(3 * D, 1), pl.BlockSpec((2, D), lambda n: (0, 0)), rows_of(2 * D, 0),
                  pl.BlockSpec((CONV_K, D), lambda n: (0, 0)), pl.BlockSpec((1, D), lambda n: (0, 0)), ANY],
        out_specs=(rows_of(D, 0), rows_of(D, 0), pl.BlockSpec((cps, HEADS, HEAD_W, HEAD_W), lambda n: (n, 0, 0, 0)), ANY),
        out_shape=(jax.ShapeDtypeStruct((TP, D), ACT), jax.ShapeDtypeStruct((TP, D), ACT),
                   jax.ShapeDtypeStruct((N_CHUNK, HEADS, HEAD_W, HEAD_W), F32), jax.ShapeDtypeStruct((3, D, D), BF16)),
        scratch_shapes=[pltpu.VMEM((HEADS, HEAD_W, HEAD_W), F32), pltpu.VMEM((cps, CHUNK, D), F32),
                        pltpu.VMEM((8, N_CB, HALO + tm, HEAD_W), F32), pltpu.VMEM((tm, D), F32),
                        pltpu.VMEM((3, W_ROW_BLK, D), BF16), pltpu.SemaphoreType.DMA((7,)),
                        pltpu.SemaphoreType.DMA((7,)), pltpu.SemaphoreType.DMA((2,))],
        compiler_params=pltpu.CompilerParams(dimension_semantics=("arbitrary",)),
    )(proj, lb_logits, proj, conv_w, conv_b, w3_b)


def _mid(xin, tgt, o, c0, proj, w3, ln_g, ln_b, gnorm_g, final_g):
    tm = TM_ELT

    def body(x_ref, t_ref, o_ref, c0_ref, z_ref, gr_ref, mc_ref, mr_ref, w_ref, lng_ref, lnb_ref, gng_ref, fg_ref,
             do_ref, dc0_ref, dz_ref, dp_ref, a3_ref, b3_ref, red_ref, on_sc, don_sc):
        i = pl.program_id(0)

        @pl.when(i == 0)
        def _():
            red_ref[...] = jnp.zeros_like(red_ref)

        w_conv, w_rec, w_out = w_ref[0], w_ref[1], w_ref[2]
        c0v = c0_ref[...].astype(F32)
        mu = jnp.mean(c0v, axis=-1, keepdims=True)
        xc = c0v - mu
        rstd = lax.rsqrt(jnp.mean(xc * xc, axis=-1, keepdims=True) + EPS)
        xh = xc * rstd
        c1 = xh * lng_ref[...] + lnb_ref[...]
        s1 = _sigmoid(c1)
        c2 = c1 * s1
        z = z_ref[...].astype(F32)
        sz = _sigmoid(z)
        silu_z = z * sz
        u_conv = (c2 * silu_z).astype(BF16)
        y_conv = _dot(u_conv, w_conv)
        ov = o_ref[...].astype(F32)
        r3 = []
        for h in range(HEADS):
            sl = slice(h * HEAD_W, (h + 1) * HEAD_W)
            oh = ov[:, sl]
            r_h = lax.rsqrt(jnp.mean(oh * oh, axis=-1, keepdims=True) + EPS)
            r3.append(r_h)
            on_sc[:, sl] = oh * r_h
        o_n = on_sc[...]
        o_g = o_n * gng_ref[...]
        gr = gr_ref[...].astype(F32)
        sgr = _sigmoid(gr)
        silu_g = gr * sgr
        u_rec = (o_g * silu_g).astype(BF16)
        y_rec = _dot(u_rec, w_rec)
        mc = mc_ref[...].astype(F32)
        mr = mr_ref[...].astype(F32)
        smc = _sigmoid(mc)
        smr = _sigmoid(mr)
        merged = (smc * y_conv + smr * y_rec).astype(BF16)
        res = x_ref[...] + _dot(merged, w_out)
        r2 = lax.rsqrt(jnp.mean(res * res, axis=-1, keepdims=True) + EPS)
        xh2 = res * r2
        row = lax.broadcasted_iota(jnp.int32, (tm, 1), 0) + i * tm
        real = row >= ROW0
        tgt = t_ref[...]
        tgt = jnp.where(i == 0, pltpu.roll(tgt, ROW0, 0), tgt)
        diff = jnp.where(real, xh2 * fg_ref[...] - tgt, 0.0)
        d_y = diff * (1.0 / D)
        d_xh2 = d_y * fg_ref[...]
        d_res = r2 * (d_xh2 - xh2 * jnp.mean(d_xh2 * xh2, axis=-1, keepdims=True))
        d_res_b = d_res.astype(BF16)
        d_merged = _dot_nt(d_res_b, w_out)
        d_yc = (d_merged * smc).astype(BF16)
        d_yr = (d_merged * smr).astype(BF16)
        dp_ref[:, D:2 * D] = (d_merged * y_conv * smc * (1.0 - smc)).astype(BF16)
        dp_ref[:, 2 * D:3 * D] = (d_merged * y_rec * smr * (1.0 - smr)).astype(BF16)
        d_ur = _dot_nt(d_yr, w_rec)
        d_og = d_ur * silu_g
        dp_ref[:, 0:D] = (d_ur * o_g * _dsilu(gr, sgr)).astype(BF16)
        d_on = d_og * gng_ref[...]
        for h in range(HEADS):
            sl = slice(h * HEAD_W, (h + 1) * HEAD_W)
            d_h = d_on[:, sl]
            n_h = o_n[:, sl]
            don_sc[:, sl] = r3[h] * (d_h - n_h * jnp.mean(d_h * n_h, axis=-1, keepdims=True))
        do_ref[...] = don_sc[...].astype(ACT)
        d_uc = _dot_nt(d_yc, w_conv)
        d_c2 = d_uc * silu_z
        dz_ref[...] = (d_uc * c2 * _dsilu(z, sz)).astype(BF16)
        d_c1 = d_c2 * _dsilu(c1, s1)
        d_xh = d_c1 * lng_ref[...]
        d_c0 = rstd * (d_xh - jnp.mean(d_xh, axis=-1, keepdims=True)
                       - xh * jnp.mean(d_xh * xh, axis=-1, keepdims=True))
        dc0_ref[...] = d_c0.astype(ACT)
        a3_ref[0] = u_conv
        b3_ref[0] = d_yc
        a3_ref[1] = u_rec
        b3_ref[1] = d_yr
        a3_ref[2] = merged
        b3_ref[2] = d_res_b
        def colsum(vv):
            return jnp.sum(vv, axis=0, keepdims=True)

        red_ref[0:1, :] += colsum(d_y * xh2)
        red_ref[1:2, :] += colsum(d_og * o_n)
        red_ref[2:3, :] += colsum(d_c1 * xh)
        red_ref[3:4, :] += colsum(d_c1)
        red_ref[4:5, :] += colsum(d_c0)
        red_ref[5:6, :] += colsum(diff * diff) * (0.5 / D)

    def row_block(width, col):
        return pl.BlockSpec((tm, width), lambda i: (i, col))

    def const_block(shape):
        return pl.BlockSpec(shape, lambda i: (0,) * len(shape))

    stack = jax.ShapeDtypeStruct((3, TP, D), BF16)
    stack_spec = pl.BlockSpec((3, tm, D), lambda i: (0, i, 0))
    return pl.pallas_call(
        body, name="mid", grid=(TP // tm,),
        in_specs=[row_block(D, 0),
                  pl.BlockSpec((pl.Element(tm), pl.Element(D)), lambda i: (_window_start(i, tm), 0)),
                  row_block(D, 0), row_block(D, 0),
                  row_block(D, 2), row_block(D, 6), row_block(D, 7), row_block(D, 8),
                  pl.BlockSpec((3, D, D), lambda i: (0, 0, 0), pipeline_mode=pl.Buffered(1)),
                  const_block((1, D)), const_block((1, D)), const_block((1, D)), const_block((1, D))],
        out_specs=(row_block(D, 0), row_block(D, 0), row_block(D, 0), row_block(3 * D, 2),
                   stack_spec, stack_spec, const_block((8, D))),
        out_shape=(jax.ShapeDtypeStruct((TP, D), ACT), jax.ShapeDtypeStruct((TP, D), ACT),
                   jax.ShapeDtypeStruct((TP, D), BF16),
                   jax.ShapeDtypeStruct((TP, D_IN), BF16), stack, stack, jax.ShapeDtypeStruct((8, D), F32)),
        scratch_shapes=[pltpu.VMEM((tm, D), F32), pltpu.VMEM((tm, D), F32)],
        compiler_params=pltpu.CompilerParams(dimension_semantics=("arbitrary",), vmem_limit_bytes=60 * 1024 * 1024),
    )(xin, tgt, o, c0, proj, proj, proj, proj, w3, ln_g, ln_b, gnorm_g, final_g)


def _rec_conv_bwd(proj, lb_logits, d_o, s_start, d_c0, d_z, conv_w, dproj, p3):
    cps = CHUNKS_PER_STEP
    tm = cps * CHUNK
    last = N_CHUNK // cps - 1
    n_strip = CONV_STRIPS
    strip = tm // n_strip

    def body(p_ref, lbl_ref, do_ref, s_ref, pg_ref, dc_ref, dz_ref, w_ref, dproj_in, p3_ref,
             dp_ref, dlb_ref, dw_ref, land_ref,
             dst, bsc, dq_sc, dk_sc, g_sc, dsh, a_sc, da_sc, acc, send_sems, recv_sems):
        del dproj_in
        n = pl.program_id(0)
        ride_start, ride_finish = _p3_to_sibling(p3_ref, land_ref, send_sems, recv_sems)

        @pl.when(n == 0)
        def _():
            ride_start()
            dst[...] = jnp.zeros_like(dst)
            dlb_ref[...] = jnp.zeros_like(dlb_ref)
            dsh[0, :, tm:tm + HALO, :] = jnp.zeros((N_CB, HALO, HEAD_W), F32)
            acc[...] = jnp.zeros_like(acc)

        @pl.when(n > 0)
        def _():
            dsh[0, :, tm:tm + HALO, :] = dsh[0, :, 0:HALO, :]

        _store_by_cb(dsh, (0,), slice(0, tm), dc_ref[...].astype(F32))
        _fill_shifts(dsh, tm)
        ga = pg_ref[:, 0:D].astype(F32)
        sb = _sigmoid(pg_ref[:, D:2 * D].astype(F32))
        a = ga * sb
        _store_by_cb(a_sc, (), slice(0, tm), a)

        def conv_unit(cb, st):
            cs = slice(cb * HEAD_W, (cb + 1) * HEAD_W)
            rows = slice(st * strip, (st + 1) * strip)
            a_s = a_sc[cb, rows, :]
            d_a = jnp.zeros((strip, HEAD_W), F32)
            for j in range(CONV_K):
                off = CONV_K - 1 - j
                lo = st * strip + 8 * (off // 8)
                slab = dsh[off % 8, cb, lo:lo + strip, :]
                d_a = d_a + w_ref[j:j + 1, cs] * slab
                acc[j, :, cs] += jnp.sum((a_s * slab).reshape(strip // 8, 8, HEAD_W), axis=0)
            da_sc[rows, cs] = d_a

        units = [(cb, st) for cb in range(N_CB) for st in range(n_strip)]

        def prep(ci):
            g = _gates(p_ref.at[pl.ds(ci * CHUNK, CHUNK)], lbl_ref, (last - n) * cps + ci, bsc.at[ci])
            g["q1"] = (g["q"] * g["e_q"]).astype(BF16)
            qm_f = g["q"] * g["e_qm"]
            km_f = g["k"] * g["e_km"]
            g["qm"] = qm_f.astype(BF16)
            g["km"] = km_f.astype(BF16)
            g["qm_lo"] = (qm_f - g["qm"].astype(F32)).astype(BF16)
            g["km_lo"] = (km_f - g["km"].astype(F32)).astype(BF16)
            g["kh_f"] = g["k"] * g["e_kh"]
            g["kh"] = g["kh_f"].astype(BF16)
            return g

        def heads_and_post(ci, g):
            rs = pl.ds(ci * CHUNK, CHUNK)
            pv = p_ref.at[rs]
            dpv = dp_ref.at[rs]
            q1, qm, km, qm_lo, km_lo, kh_f, kh = (g[k] for k in ("q1", "qm", "km", "qm_lo", "km_lo", "kh_f", "kh"))
            for h in range(HEADS):
                if units:
                    conv_unit(*units.pop(0))
                sl = slice(h * HEAD_W, (h + 1) * HEAD_W)
                v = pv[:, 2 * D + h * HEAD_W:2 * D + (h + 1) * HEAD_W]
                d_oh = do_ref[rs, sl].astype(BF16)
                s0 = s_ref[ci, h]
                ds_end = dst[h]
                ds_end_b = ds_end.astype(BF16)
                att = jnp.where(g["causal"], _dot_nt(qm[:, sl], km[:, sl]), 0.0).astype(BF16)
                d_att = jnp.where(g["causal"], _dot_nt(d_oh, v), 0.0).astype(BF16)
                d_v = _dot_tn(att, d_oh) + _dot_nt(kh[:, sl], ds_end_b)
                d_qm2 = _dot(d_att, jnp.concatenate([km[:, sl], km_lo[:, sl]], axis=1))
                d_qm = d_qm2[:, 0:HEAD_W] + d_qm2[:, HEAD_W:2 * HEAD_W]
                d_q1 = _dot(d_oh, s0.astype(BF16))
                d_km2 = _dot_tn(d_att, jnp.concatenate([qm[:, sl], qm_lo[:, sl]], axis=1))
                d_km = d_km2[:, 0:HEAD_W] + d_km2[:, HEAD_W:2 * HEAD_W]
                d_kh = _dot(v, ds_end_b)
                dq_sc[ci, :, sl] = d_qm * g["e_qm"][:, sl] + d_q1 * g["e_q"][:, sl]
                dk_sc[ci, :, sl] = d_km * g["e_km"][:, sl] + d_kh * g["e_kh"][:, sl]
                g_sc[ci, :, sl] = (jnp.sum(kh_f[:, sl] * d_kh, axis=0, keepdims=True)
                                   + g["e_last"][:, sl] * jnp.sum(ds_end * s0, axis=0, keepdims=True))
                dst[h] = ds_end * g["e_last"][:, sl] + _dot_tn(d_oh, q1[:, sl])
                dpv[:, 5 * D + h * HEAD_W:5 * D + (h + 1) * HEAD_W] = d_v.astype(BF16)
            d_q = dq_sc[ci]
            d_k = dk_sc[ci]
            d_b = g["q"] * d_q - g["k"] * d_k
            anti = jnp.logical_not(g["causal"]) | (lax.broadcasted_iota(jnp.int32, (CHUNK, CHUNK), 0)
                                                    == lax.broadcasted_iota(jnp.int32, (CHUNK, CHUNK), 1))
            d_lf = _tri_matmul(anti.astype(BF16), d_b) + g_sc[ci]
            d_f = jnp.where(g["valid"], d_lf / g["f"] - d_k, 0.0)
            sg = g["sg"]
            dlb_ref[0:1, :] += jnp.sum(d_f * (1.0 - sg), axis=0, keepdims=True)
            dpv[:, 3 * D:4 * D] = (d_q * _dsilu(g["q_raw"], g["sq"])).astype(BF16)
            dpv[:, 4 * D:5 * D] = (d_f * (1.0 - g["lb"]) * sg * (1.0 - sg)).astype(BF16)

        ready = prep(cps - 1)
        for ci in reversed(range(cps)):
            coming = prep(ci - 1) if ci > 0 else None
            heads_and_post(ci, ready)
            ready = coming
        while units:
            conv_unit(*units.pop(0))

        d_a = da_sc[...]
        dp_ref[:, 0:D] = (d_a * sb).astype(BF16)
        dp_ref[:, D:2 * D] = (d_a * a * (1.0 - sb)).astype(BF16)
        dp_ref[:, 2 * D:3 * D] = dz_ref[...]

        @pl.when(n == last)
        def _():
            for j in range(CONV_K):
                dw_ref[j:j + 1, :] = jnp.sum(acc[j], axis=0, keepdims=True)
            dw_ref[CONV_K:CONV_K + 1, :] = jnp.zeros((1, D), F32)
            ride_finish()

    def rows_of(width, col):
        return pl.BlockSpec((tm, width), lambda n: (last - n, col))

    return pl.pallas_call(
        body, name="rec_conv_bwd", grid=(N_CHUNK // cps,),
        in_specs=[rows_of(3 * D, 1), pl.BlockSpec((2, D), lambda n: (0, 0)), rows_of(D, 0),
                  pl.BlockSpec((cps, HEADS, HEAD_W, HEAD_W), lambda n: (last - n, 0, 0, 0)),
                  rows_of(2 * D, 0), rows_of(D, 0), rows_of(D, 0), pl.BlockSpec((CONV_K, D), lambda n: (0, 0)), ANY, ANY],
        out_specs=(rows_of(6 * D, 0), pl.BlockSpec((8, D), lambda n: (0, 0)),
                   pl.BlockSpec((CONV_K + 1, D), lambda n: (0, 0)), ANY),
        out_shape=(jax.ShapeDtypeStruct((TP, D_IN), BF16), jax.ShapeDtypeStruct((8, D), F32),
                   jax.ShapeDtypeStruct((CONV_K + 1, D), F32), jax.ShapeDtypeStruct((4, 3, W_ROW_BLK, D), F32)),
        scratch_shapes=[pltpu.VMEM((HEADS, HEAD_W, HEAD_W), F32), pltpu.VMEM((cps, CHUNK, D), F32),
                        pltpu.VMEM((cps, CHUNK, D), F32), pltpu.VMEM((cps, CHUNK, D), F32),
                        pltpu.VMEM((cps, 1, D), F32),
                        pltpu.VMEM((8, N_CB, tm + HALO, HEAD_W), F32), pltpu.VMEM((N_CB, tm, HEAD_W), F32),
                        pltpu.VMEM((tm, D), F32), pltpu.VMEM((CONV_K, 8, D), F32),
                        pltpu.SemaphoreType.DMA((4,)), pltpu.SemaphoreType.DMA((4,))],
        input_output_aliases={8: 0},
        compiler_params=pltpu.CompilerParams(dimension_semantics=("arbitrary",)),
    )(proj, lb_logits, d_o, s_start, proj, d_c0, d_z, conv_w, dproj, p3)


def _wgrad3(a3, b3):
    tt = TT_WGRAD

    def body(a_ref, b_ref, o_ref):
        @pl.when(pl.program_id(1) == 0)
        def _():
            o_ref[...] = jnp.zeros_like(o_ref)

        o_ref[0] += _dot_tn(a_ref[0], b_ref[0])

    return pl.pallas_call(
        body, name="wgrad3", grid=(3, TP // tt),
        in_specs=[pl.BlockSpec((1, tt, D), lambda g, t: (g, t, 0)), pl.BlockSpec((1, tt, D), lambda g, t: (g, t, 0))],
        out_specs=pl.BlockSpec((1, D, D), lambda g, t: (g, 0, 0)),
        out_shape=jax.ShapeDtypeStruct((3, D, D), F32),
        compiler_params=pltpu.CompilerParams(dimension_semantics=("arbitrary", "arbitrary")),
    )(a3, b3)


def _wgrad_in(h, dproj, ids, chip1b):
    tt = TT_WGRAD
    n_t = TP // tt

    def body(ids_ref, a_ref, b_ref, c1_ref, o_ref, ob_ref, l0_ref, far_ref, acc, tmp, send_sems, recv_sems, tmp_sem,
             far_send_sems, far_recv_sems):
        del ids_ref
        r = pl.program_id(0)
        t = pl.program_id(1)
        x, y, c = _my_place()
        sibling = (x, y, 1 - c)
        slot = lax.rem(r, 2)
        ride_start, ride_finish = _partials_to_owners(c1_ref, far_ref, far_send_sems, far_recv_sems)

        @pl.when((r == 0) & (t == 0))
        def _():
            ride_start()

        def send_in(q):
            return pltpu.make_async_remote_copy(
                src_ref=acc.at[q % 2], dst_ref=l0_ref.at[q], send_sem=send_sems.at[q], recv_sem=recv_sems.at[q],
                device_id=sibling, device_id_type=MESH_ID)

        def landed(q):
            return pltpu.make_async_copy(l0_ref.at[q], tmp, tmp_sem)

        @pl.when(t == 0)
        def _():
            acc[slot] = jnp.zeros((D, W_IN_BLK), F32)

        acc[slot] += _dot_tn(a_ref[...], b_ref[...])

        for q in range(4):
            @pl.when((r == q) & (t == n_t - 1))
            def _(q=q):
                if q >= 1:
                    send_in(q - 1).wait_send()
                send_in(q).start()

            @pl.when((r == 4 + q) & (t == n_t - 2))
            def _(q=q):
                if q == 0:
                    send_in(3).wait_send()
                send_in(q).wait_recv()
                landed(q).start()

            @pl.when((r == 4 + q) & (t == n_t - 1))
            def _(q=q):
                landed(q).wait()
                tot = acc[q % 2] + tmp[...]
                o_ref[0] = tot
                ob_ref[0] = tot.astype(BF16)

        @pl.when((r == N_DEV - 1) & (t == n_t - 1))
        def _():
            ride_finish()

    blk = pl.BlockSpec((1, D, W_IN_BLK), lambda r, t, ids: (jnp.maximum(r - 4, 0), 0, 0))
    return pl.pallas_call(
        body, name="wgrad_in",
        grid_spec=pltpu.PrefetchScalarGridSpec(
            num_scalar_prefetch=1, grid=(N_DEV, n_t),
            in_specs=[pl.BlockSpec((tt, D), lambda r, t, ids: (t, 0)),
                      pl.BlockSpec((tt, W_IN_BLK), lambda r, t, ids: (t, ids[r])), ANY],
            out_specs=(blk, blk, ANY, ANY),
            scratch_shapes=[pltpu.VMEM((2, D, W_IN_BLK), F32), pltpu.VMEM((D, W_IN_BLK), F32),
                            pltpu.SemaphoreType.DMA((4,)), pltpu.SemaphoreType.DMA((4,)), pltpu.SemaphoreType.DMA,
                            pltpu.SemaphoreType.DMA((3,)), pltpu.SemaphoreType.DMA((3,))]),
        out_shape=(jax.ShapeDtypeStruct((4, D, W_IN_BLK), F32), jax.ShapeDtypeStruct((4, D, W_IN_BLK), BF16),
                   jax.ShapeDtypeStruct((4, D, W_IN_BLK), F32), jax.ShapeDtypeStruct((3, 3, W_ROW_BLK, D), BF16)),
        compiler_params=pltpu.CompilerParams(dimension_semantics=("arbitrary", "arbitrary")),
    )(ids, h, dproj, chip1b)


def _chip_sum_3(p3, land1, ids_mine):
    def body(ids_ref, p_ref, l_ref, o_ref, ob_ref):
        del ids_ref
        tot = p_ref[...] + l_ref[0]
        o_ref[0] = tot
        ob_ref[0] = tot.astype(BF16)

    blk = pl.BlockSpec((1, 3, W_ROW_BLK, D), lambda r, ids: (r, 0, 0, 0))
    return pl.pallas_call(
        body, name="chip_sum_3",
        grid_spec=pltpu.PrefetchScalarGridSpec(
            num_scalar_prefetch=1, grid=(4,),
            in_specs=[pl.BlockSpec((3, W_ROW_BLK, D), lambda r, ids: (0, ids[r], 0)), blk],
            out_specs=(blk, blk)),
        out_shape=(jax.ShapeDtypeStruct((4, 3, W_ROW_BLK, D), F32), jax.ShapeDtypeStruct((4, 3, W_ROW_BLK, D), BF16)),
    )(ids_mine, p3, land1)


def _dh_and_norm_bwd(dproj, w_in_full, xin, b3, norm_g, chip0b):
    tm = TM_MAT
    n_k = N_DEV // DH_K_BLKS
    n_m = TP // tm

    def body(dp_ref, w_ref, x_ref, dr_ref, g_ref, c0_ref, dx_ref, dg_ref, f0_ref, acc, send_sems, recv_sems):
        m = pl.program_id(0)
        k = pl.program_id(1)
        ride_start, ride_finish = _partials_to_owners(c0_ref, f0_ref, send_sems, recv_sems)

        @pl.when((m == 0) & (k == 0))
        def _():
            ride_start()

        @pl.when(k == 0)
        def _():
            acc[...] = jnp.zeros_like(acc)

        part = _dot_nt(dp_ref[:, 0:W_IN_BLK], w_ref[0])
        for j in range(1, DH_K_BLKS):
            part = part + _dot_nt(dp_ref[:, j * W_IN_BLK:(j + 1) * W_IN_BLK], w_ref[j])
        acc[...] += part

        @pl.when((k == n_k - 1) & (m == 0))
        def _():
            dg_ref[...] = jnp.zeros_like(dg_ref)

        @pl.when(k == n_k - 1)
        def _():
            xv = x_ref[...]
            r1 = lax.rsqrt(jnp.mean(xv * xv, axis=-1, keepdims=True) + EPS)
            xh = xv * r1
            d_h = acc[...]
            dg_ref[0:1, :] += jnp.sum(d_h * xh, axis=0, keepdims=True)
            d_xh = d_h * g_ref[...]
            dx_ref[...] = dr_ref[0].astype(F32) + r1 * (d_xh - xh * jnp.mean(d_xh * xh, axis=-1, keepdims=True))

        @pl.when((m == n_m - 1) & (k == n_k - 1))
        def _():
            ride_finish()

    return pl.pallas_call(
        body, name="dh_norm_bwd", grid=(n_m, n_k),
        in_specs=[pl.BlockSpec((tm, DH_K_BLKS * W_IN_BLK), lambda m, k: (m, k)),
                  pl.BlockSpec((DH_K_BLKS, D, W_IN_BLK), lambda m, k: (k, 0, 0)),
                  pl.BlockSpec((tm, D), lambda m, k: (m, 0)), pl.BlockSpec((1, tm, D), lambda m, k: (2, m, 0)),
                  pl.BlockSpec((1, D), lambda m, k: (0, 0)), ANY],
        out_specs=(pl.BlockSpec((tm, D), lambda m, k: (m, 0)), pl.BlockSpec((8, D), lambda m, k: (0, 0)), ANY),
        out_shape=(jax.ShapeDtypeStruct((TP, D), F32), jax.ShapeDtypeStruct((8, D), F32),
                   jax.ShapeDtypeStruct((3, D, W_IN_BLK), BF16)),
        scratch_shapes=[pltpu.VMEM((tm, D), F32), pltpu.SemaphoreType.DMA((3,)), pltpu.SemaphoreType.DMA((3,))],
        compiler_params=pltpu.CompilerParams(dimension_semantics=("arbitrary", "arbitrary")),
    )(dproj, w_in_full, xin, b3, norm_g, chip0b)


def _sum_adamw(own, landed, w, m, v, tr, name):
    rows, cols = w.shape
    n_t = rows // tr

    def body(o_ref, l1_ref, l2_ref, l3_ref, w_ref, m_ref, v_ref, g_ref, d_ref, m2_ref, v2_ref):
        g = ((o_ref[...] + l1_ref[...].astype(F32)) + l2_ref[...].astype(F32)) + l3_ref[...].astype(F32)
        delta, m2, v2 = _adamw(w_ref[...], g, m_ref[...], v_ref[...])
        g_ref[...] = g
        d_ref[...] = delta
        m2_ref[...] = m2
        v2_ref[...] = v2

    def spec(k):
        return pl.BlockSpec((tr, cols), lambda i: (i + k * n_t, 0))

    out = jax.ShapeDtypeStruct((rows, cols), F32)
    return pl.pallas_call(
        body, name=name, grid=(n_t,),
        in_specs=[spec(0), spec(0), spec(1), spec(2), spec(0), spec(0), spec(0)],
        out_specs=(spec(0),) * 4, out_shape=(out,) * 4,
    )(own, landed, landed, landed, w, m, v)


def _adamw_3(chip1, far1, ws, ms, vs):
    def body(c_ref, f_ref, *refs):
        w_refs, m_refs, v_refs, outs = refs[0:3], refs[3:6], refs[6:9], refs[9:21]
        for k in range(3):
            g = ((c_ref[0, k] + f_ref[0, k].astype(F32)) + f_ref[1, k].astype(F32)) + f_ref[2, k].astype(F32)
            delta, m2, v2 = _adamw(w_refs[k][0], g, m_refs[k][0], v_refs[k][0])
            for kind, val in enumerate((g, delta, m2, v2)):
                outs[3 * kind + k][0] = val

    full = pl.BlockSpec((1, W_ROW_BLK, D), lambda i: (0, 0, 0))
    out = jax.ShapeDtypeStruct((1, W_ROW_BLK, D), F32)
    res = pl.pallas_call(
        body, name="adamw_3", grid=(1,),
        in_specs=[pl.BlockSpec((1, 3, W_ROW_BLK, D), lambda i: (0, 0, 0, 0)),
                  pl.BlockSpec((3, 3, W_ROW_BLK, D), lambda i: (0, 0, 0, 0))] + [full] * 9,
        out_specs=(full,) * 12, out_shape=(out,) * 12,
    )(chip1, far1, *ws, *ms, *vs)
    return tuple(res[3 * kind:3 * kind + 3] for kind in range(4))


N_SMALL = 9


def _small_update(pack_all, srs_all, ws, ms, vs):
    def body(pk_ref, sr_ref, *refs):
        w_refs, m_refs, v_refs = refs[0:N_SMALL], refs[N_SMALL:2 * N_SMALL], refs[2 * N_SMALL:3 * N_SMALL]
        loss_ref = refs[3 * N_SMALL]
        outs = refs[3 * N_SMALL + 1:7 * N_SMALL + 1]
        tot_sc, tots_sc = refs[7 * N_SMALL + 1:]
        tot = pk_ref[0]
        tot_s = sr_ref[0]
        for d in range(1, N_DEV):
            tot = tot + pk_ref[d]
            tot_s = tot_s + sr_ref[d]
        tot_sc[...] = tot
        tots_sc[...] = tot_s
        loss_ref[...] = jnp.sum(tot_sc[5:6, :], axis=1, keepdims=True)
        lbl = w_refs[4]
        p0 = _sigmoid(lbl[0:1, :] - lbl[1:2, :])
        d_l0 = tot_sc[4:5, :] * p0 * (1.0 - p0)

        def update(k, sel, g):
            delta, m2, v2 = _adamw(w_refs[k][sel], g, m_refs[k][sel], v_refs[k][sel])
            for kind, val in enumerate((g, delta, m2, v2)):
                outs[N_SMALL * kind + k][sel] = val

        everything = (slice(None), slice(None))
        for k, row in ((0, 0), (1, 1), (2, 2), (3, 3), (5, 6), (6, 7)):
            update(k, everything, tot_sc[row:row + 1, :])
        update(4, (slice(0, 1), slice(None)), d_l0)
        update(4, (slice(1, 2), slice(None)), -d_l0)
        update(7, (0, slice(None), slice(None)), tots_sc[0:CONV_K, :])
        update(8, everything, tots_sc[META_ROW:META_ROW + N_META, :])

    shapes = [jax.ShapeDtypeStruct(w.shape, F32) for w in ws]
    res = pl.pallas_call(
        body, name="small_update",
        out_shape=(jax.ShapeDtypeStruct((1, 1), F32), *(shapes * 4)),
        scratch_shapes=[pltpu.VMEM((8, D), F32), pltpu.VMEM((SMALL_ROWS, HEAD_W), F32)],
    )(pack_all, srs_all, *ws, *ms, *vs)
    return res[0], tuple(res[1 + N_SMALL * kind:1 + N_SMALL * (kind + 1)] for kind in range(4))


def _local_step(xin, proj, target, conv_w_full, conv_b, ln_g, ln_b, w3_b, lb_logits, gnorm_g, final_g, ids_mine):
    fg = final_g.reshape(1, D)
    c0, o, s_start, w3_full = _rec_conv_fwd(proj, lb_logits, conv_w_full, conv_b, w3_b)
    d_o, d_c0, d_z, dproj, a3, b3, red = _mid(xin, target, o, c0, proj, w3_full, ln_g, ln_b, gnorm_g, fg)
    p3 = _wgrad3(a3, b3)
    dproj, dlb, d_conv_w, land1 = _rec_conv_bwd(proj, lb_logits, d_o, s_start, d_c0, d_z, conv_w_full, dproj, p3)
    chip1, chip1b = _chip_sum_3(p3, land1, ids_mine)
    return dproj, b3, p3, chip1, chip1b, d_conv_w, red, dlb


def kernel(x, meta_tokens, norm_g, w_in, conv_w, conv_b, ln_g, ln_b, w_conv_out, lb_logits, gnorm_g, w_rec_out, w_out, final_g, loss_target, m_meta_tokens, m_norm_g, m_w_in, m_conv_w, m_conv_b, m_ln_g, m_ln_b, m_w_conv_out, m_lb_logits, m_gnorm_g, m_w_rec_out, m_w_out, m_final_g, v_meta_tokens, v_norm_g, v_w_in, v_conv_w, v_conv_b, v_ln_g, v_ln_b, v_w_conv_out, v_lb_logits, v_gnorm_g, v_w_rec_out, v_w_out, v_final_g):
    mx, my, mc = _my_place()

    ws_s = jnp.concatenate([conv_w[0], jnp.zeros((1, HEAD_W), F32), meta_tokens], axis=0)
    small_full = jnp.transpose(_gather_small(ws_s), (1, 0, 2)).reshape(SMALL_ROWS, D)
    conv_w_full = small_full[0:CONV_K]
    meta_full = small_full[META_ROW:META_ROW + N_META]
    w_in_b, w3_b = _cast_shards(w_in[0], w_conv_out, w_rec_out, w_out)
    first, second, diag = _gather_chips(mx, my, mc)
    use_order = [(mx, my, mc), (mx, my, 1 - mc), (*first, mc), (*second, 1 - mc), (*second, mc), (*first, 1 - mc),
                 (*diag, mc), (*diag, 1 - mc)]
    order = jnp.stack([_dev_index(*p) for p in use_order]).astype(jnp.int32)
    proj, xin, h, w_in_full = _gather_and_proj(x[0], meta_full, norm_g, w_in_b, order)
    h = h.reshape(TP, D)

    ids_mine = jnp.stack([_dev_index(*_chip_rel(mx, my, r), mc) for r in range(4)]).astype(jnp.int32)
    ids_sib = jnp.stack([_dev_index(*_chip_rel(mx, my, r), 1 - mc) for r in range(4)]).astype(jnp.int32)
    dproj, b3, _, chip1, chip1b, d_conv_w, red, dlb = _local_step(
        xin, proj, loss_target[0], conv_w_full, conv_b, ln_g, ln_b, w3_b, lb_logits, gnorm_g, final_g, ids_mine)

    chip0, chip0b, _, far1 = _wgrad_in(h, dproj, jnp.concatenate([ids_sib, ids_mine]), chip1b)
    d_xin, dng, far0 = _dh_and_norm_bwd(dproj, w_in_full, xin, b3, norm_g, chip0b)
    pack = jnp.concatenate([dng[0:1], red[4:5], red[2:3], red[3:4], dlb[0:1], red[5:6], red[1:2], red[0:1]], axis=0)
    g_in, d_in, m_in, v_in = _sum_adamw(chip0.reshape(4 * D, W_IN_BLK), far0.reshape(3 * D, W_IN_BLK), w_in[0],
                                        m_w_in[0], v_w_in[0], 256, "adamw_in")
    big3 = _adamw_3(chip1, far1, (w_conv_out, w_rec_out, w_out), (m_w_conv_out, m_w_rec_out, m_w_out),
                    (v_w_conv_out, v_w_rec_out, v_w_out))

    srs = jnp.concatenate([d_conv_w, d_xin[PAD_FRONT:ROW0]], axis=0)
    srs = jnp.transpose(srs.reshape(SMALL_ROWS, N_DEV, HEAD_W), (1, 0, 2))
    pack_all, srs_all = _exchange_small(pack, srs)
    loss, small = _small_update(
        pack_all, srs_all,
        (norm_g, conv_b, ln_g, ln_b, lb_logits, gnorm_g, final_g.reshape(1, D), conv_w, meta_tokens),
        (m_norm_g, m_conv_b, m_ln_g, m_ln_b, m_lb_logits, m_gnorm_g, m_final_g.reshape(1, D), m_conv_w, m_meta_tokens),
        (v_norm_g, v_conv_b, v_ln_g, v_ln_b, v_lb_logits, v_gnorm_g, v_final_g.reshape(1, D), v_conv_w, v_meta_tokens))

    outs = [loss.reshape(()), d_xin[ROW0:][None]]
    for kind, a_in in enumerate((g_in, d_in, m_in, v_in)):
        ng, cb, lg, lb_, lbl, gg, fg, cw, mt = small[kind]
        a_3 = big3[kind]
        outs += [mt, ng, a_in[None], cw, cb, lg, lb_, a_3[0], lbl, gg, a_3[1], a_3[2], fg.reshape(D)]
    return tuple(outs)
```

```python
import jax
import jax.numpy as jnp
from jax import lax
from jax.experimental import pallas as pl
from jax.experimental.pallas import tpu as pltpu

F32 = jnp.float32
BF16 = jnp.bfloat16
ACT = BF16

D = 1024
SEQ = 4096
N_META = 16
CHUNK = 64
PAD_FRONT = 48
ROW0 = PAD_FRONT + N_META
TP = ROW0 + SEQ
N_CHUNK = TP // CHUNK
HEADS = 8
HEAD_W = 128
D_IN = 9 * D
N_DEV = 8
W_IN_BLK = D_IN // N_DEV
W_ROW_BLK = D // N_DEV
CONV_K = 31
SMALL_ROWS = 48
META_ROW = 32
EPS = 1e-6
HALO = 32

TM_MAT = 1040
TT_WGRAD = 2080
DH_K_BLKS = 2
TM_ELT = 208
CHUNKS_PER_STEP = 5
CONV_STRIPS = 4

ADAM_LR = 0.001
ADAM_B1 = 0.9
ADAM_B2 = 0.999
ADAM_EPS = 1e-08
ADAM_WD = 0.01
ADAM_STEP = 10

MESH_ID = pl.DeviceIdType.MESH
ANY = pl.BlockSpec(memory_space=pl.ANY)


def _sigmoid(v):
    return jax.nn.sigmoid(v)


def _dsilu(v, s):
    return s * (1.0 + v * (1.0 - s))


def _dot(a, b):
    return jnp.dot(a, b, preferred_element_type=F32)


def _dot_nt(a, b):
    return lax.dot_general(a, b, (((1,), (1,)), ((), ())), preferred_element_type=F32)


def _dot_tn(a, b):
    return lax.dot_general(a, b, (((0,), (0,)), ((), ())), preferred_element_type=F32)


def _split3(v):
    hi = v.astype(BF16)
    r1 = v - hi.astype(F32)
    mid = r1.astype(BF16)
    lo = (r1 - mid.astype(F32)).astype(BF16)
    return hi, mid, lo


def _tri_matmul(tri, v):
    hi, mid, lo = _split3(v)
    return _dot(tri, hi) + _dot(tri, mid) + _dot(tri, lo)


def _adamw(w, g, m, v):
    m2 = ADAM_B1 * m + (1.0 - ADAM_B1) * g
    v2 = ADAM_B2 * v + (1.0 - ADAM_B2) * jnp.square(g)
    m_hat = m2 / (1.0 - ADAM_B1 ** ADAM_STEP)
    v_hat = v2 / (1.0 - ADAM_B2 ** ADAM_STEP)
    delta = -ADAM_LR * (m_hat / (jnp.sqrt(v_hat) + ADAM_EPS) + ADAM_WD * w)
    return delta, m2, v2


def _window_start(i, tm):
    assert tm % 16 == 0 and ROW0 % 16 == 0
    return pl.multiple_of(16 * jnp.maximum((tm // 16) * i - ROW0 // 16, 0), 16)


def _my_place():
    return lax.axis_index("x"), lax.axis_index("y"), lax.axis_index("c")


def _dev_index(px, py, pc):
    return 4 * px + 2 * py + pc


def _peer(x, y, c, r):
    return (jnp.bitwise_xor(x, (r >> 2) & 1), jnp.bitwise_xor(y, (r >> 1) & 1), jnp.bitwise_xor(c, r & 1))


def _gather_small_and_cast(small_s, w_in_s, w_conv_s, w_rec_s, w_out_s):
    def body(s_ref, a_ref, c_ref, r_ref, w_ref, o_ref, oa_ref, ob_ref, send_sems, recv_sems, local_sem):
        x, y, c = _my_place()
        my_id = _dev_index(x, y, c)
        mine = pltpu.make_async_copy(s_ref, o_ref.at[my_id], local_sem)
        mine.start()
        copies = []
        for r in range(1, N_DEV):
            cp = pltpu.make_async_remote_copy(
                src_ref=s_ref, dst_ref=o_ref.at[my_id], send_sem=send_sems.at[r - 1], recv_sem=recv_sems.at[r - 1],
                device_id=_peer(x, y, c, r), device_id_type=MESH_ID)
            cp.start()
            copies.append(cp)
        oa_ref[...] = a_ref[...].astype(BF16)
        for k, ref in enumerate((c_ref, r_ref, w_ref)):
            ob_ref[k] = ref[0].astype(BF16)
        for cp in copies:
            cp.wait_recv()
        for cp in copies:
            cp.wait_send()
        mine.wait()

    vmem = pl.BlockSpec(memory_space=pltpu.VMEM)
    return pl.pallas_call(
        body, name="gather_small_and_cast",
        out_shape=(jax.ShapeDtypeStruct((N_DEV,) + small_s.shape, F32), jax.ShapeDtypeStruct(w_in_s.shape, BF16),
                   jax.ShapeDtypeStruct((3, W_ROW_BLK, D), BF16)),
        in_specs=[ANY, vmem, vmem, vmem, vmem], out_specs=(ANY, vmem, vmem),
        scratch_shapes=[pltpu.SemaphoreType.DMA((7,)), pltpu.SemaphoreType.DMA((7,)), pltpu.SemaphoreType.DMA],
    )(small_s, w_in_s, w_conv_s, w_rec_s, w_out_s)


def _w3_gather(src, out, stage, send_sems, recv_sems, local_sems):
    x, y, c = _my_place()
    me, sibling = (x, y, c), (x, y, 1 - c)
    chips = [(1 - x, y), (x, 1 - y), (1 - x, 1 - y)]

    def block(place):
        d = _dev_index(*place)
        return out.at[:, pl.ds(pl.multiple_of(d * W_ROW_BLK, W_ROW_BLK), W_ROW_BLK), :]

    def copy(k, place, to, from_src=False):
        return pltpu.make_async_remote_copy(
            src_ref=src if from_src else block(place), dst_ref=block(place),
            send_sem=send_sems.at[k], recv_sem=recv_sems.at[k], device_id=to, device_id_type=MESH_ID)

    own_in = pltpu.make_async_copy(src, stage, local_sems.at[0])
    own_out = pltpu.make_async_copy(stage, block(me), local_sems.at[1])

    def start():
        copy(0, me, sibling, from_src=True).start()
        for j, chip in enumerate(chips):
            copy(1 + j, me, (*chip, c), from_src=True).start()
        own_in.start()
        own_in.wait()
        own_out.start()

    def finish():
        for j, chip in enumerate(chips):
            copy(1 + j, (*chip, c), me).wait_recv()
            copy(4 + j, (*chip, c), sibling).start()
        copy(0, sibling, me).wait_recv()
        for j, chip in enumerate(chips):
            copy(4 + j, (*chip, 1 - c), me).wait_recv()
        for k in range(7):
            copy(k, me, me).wait_send()
        own_out.wait()

    return start, finish


def _p3_to_sibling(p3_ref, land_ref, send_sems, recv_sems):
    x, y, c = _my_place()

    def cp(q):
        d = _dev_index(*_chip_rel(x, y, q), 1 - c)
        return pltpu.make_async_remote_copy(
            src_ref=p3_ref.at[:, pl.ds(pl.multiple_of(d * W_ROW_BLK, W_ROW_BLK), W_ROW_BLK), :],
            dst_ref=land_ref.at[q], send_sem=send_sems.at[q], recv_sem=recv_sems.at[q],
            device_id=(x, y, 1 - c), device_id_type=MESH_ID)

    def start():
        for q in range(4):
            cp(q).start()

    def finish():
        for q in range(4):
            cp(q).wait_recv()
        for q in range(4):
            cp(q).wait_send()

    return start, finish


def _partials_to_owners(src_ref, far_ref, send_sems, recv_sems):
    x, y, c = _my_place()

    def cp(q):
        return pltpu.make_async_remote_copy(
            src_ref=src_ref.at[q], dst_ref=far_ref.at[q - 1], send_sem=send_sems.at[q - 1],
            recv_sem=recv_sems.at[q - 1], device_id=(*_chip_rel(x, y, q), c), device_id_type=MESH_ID)

    def start():
        for q in range(1, 4):
            cp(q).start()

    def finish():
        for q in range(1, 4):
            cp(q).wait_recv()
        for q in range(1, 4):
            cp(q).wait_send()

    return start, finish


def _gather_chips(x, y, c):
    first = (jnp.bitwise_xor(x, 1 - c), jnp.bitwise_xor(y, c))
    second = (jnp.bitwise_xor(x, c), jnp.bitwise_xor(y, 1 - c))
    return [first, second, (1 - x, 1 - y)]


def _gather_and_proj(x_seq, meta_full, norm_g, w_in_b, order):
    tm = TM_MAT
    n_m = TP // tm
    last_m = n_m - 1

    def body(order_ref, x_ref, meta_ref, g_ref, s0, proj_ref, xin_ref, h_out, o0, hbuf, wbuf, send_sems, recv_sems,
             local_sems):
        del order_ref
        n = pl.program_id(0)
        m = pl.program_id(1)
        x, y, c = _my_place()
        me, sibling = (x, y, c), (x, y, 1 - c)
        chips = _gather_chips(x, y, c)

        def block(place):
            return o0.at[_dev_index(*place)]

        def copy(k, place, to, from_src=False):
            return pltpu.make_async_remote_copy(
                src_ref=s0 if from_src else block(place), dst_ref=block(place),
                send_sem=send_sems.at[k], recv_sem=recv_sems.at[k], device_id=to, device_id_type=MESH_ID)

        def to_vmem(place, slot):
            return pltpu.make_async_copy(block(place), wbuf.at[slot], local_sems.at[slot])

        own_out = pltpu.make_async_copy(wbuf.at[0], block(me), local_sems.at[2])
        h_copy = pltpu.make_async_copy(hbuf, h_out, local_sems.at[3])

        @pl.when((n == 0) & (m == 0))
        def _():
            copy(0, me, sibling, from_src=True).start()
            for j, chip in enumerate(chips[0:2]):
                copy(1 + j, me, (*chip, c), from_src=True).start()
            mine = pltpu.make_async_copy(s0, wbuf.at[0], local_sems.at[0])
            mine.start()
            mine.wait()
            own_out.start()

        @pl.when(n == 0)
        def _():
            xv = x_ref[...]
            xin_ref[...] = jnp.where(m == 0, pltpu.roll(xv, ROW0, 0), xv)

            @pl.when(m == 0)
            def _():
                xin_ref[0:PAD_FRONT, :] = jnp.zeros((PAD_FRONT, D), F32)
                xin_ref[PAD_FRONT:ROW0, :] = meta_ref[...]

            xv = xin_ref[...]
            r = lax.rsqrt(jnp.mean(xv * xv, axis=-1, keepdims=True) + EPS)
            hbuf[m] = (xv * r * g_ref[...]).astype(BF16)

        between = [4 + c, 5 - c, 6]
        first, second, diag = chips
        plan = [(sibling, (0, sibling), None),
                ((*first, c), (1, (*first, c)), between[0]),
                ((*second, 1 - c), (between[1], (*second, 1 - c)), None),
                ((*second, c), (2, (*second, c)), between[1]),
                ((*first, 1 - c), (between[0], (*first, 1 - c)), None),
                ((*diag, c), (3, (*diag, c)), between[2]),
                ((*diag, 1 - c), (between[2], (*diag, 1 - c)), None)]

        for s, (place, (k, origin), pass_on) in enumerate(plan, start=1):
            @pl.when((n == s - 1) & (m == last_m))
            def _(s=s, place=place, k=k, origin=origin, pass_on=pass_on):
                copy(k, origin, me).wait_recv()
                if pass_on is not None:
                    copy(pass_on, place, sibling).start()
                if s == 2:
                    copy(3, place, (*chips[1], c)).start()
                    own_out.wait()
                to_vmem(place, s % 2).start()

            @pl.when((n == s) & (m == 0))
            def _(s=s, place=place):
                to_vmem(place, s % 2).wait()

        proj_ref[...] = _dot(hbuf[m], wbuf[lax.rem(n, 2)]).astype(BF16)

        @pl.when((n == 0) & (m == last_m))
        def _():
            h_copy.start()

        @pl.when((n == N_DEV - 1) & (m == last_m))
        def _():
            for k in range(7):
                copy(k, me, me).wait_send()
            h_copy.wait()

    return pl.pallas_call(
        body, name="gather_and_proj",
        grid_spec=pltpu.PrefetchScalarGridSpec(
            num_scalar_prefetch=1, grid=(N_DEV, n_m),
            in_specs=[pl.BlockSpec((pl.Element(tm), pl.Element(D)),
                                   lambda n, m, o: (_window_start(jnp.where(n == 0, m, 0), tm), 0)),
                      pl.BlockSpec((N_META, D), lambda n, m, o: (0, 0)),
                      pl.BlockSpec((1, D), lambda n, m, o: (0, 0)), ANY],
            out_specs=(pl.BlockSpec((tm, W_IN_BLK), lambda n, m, o: (m, o[n])),
                       pl.BlockSpec((tm, D), lambda n, m, o: (jnp.where(n == 0, m, last_m), 0)), ANY, ANY),
            scratch_shapes=[pltpu.VMEM((n_m, tm, D), BF16), pltpu.VMEM((2, D, W_IN_BLK), BF16),
                            pltpu.SemaphoreType.DMA((7,)), pltpu.SemaphoreType.DMA((7,)),
                            pltpu.SemaphoreType.DMA((4,))]),
        out_shape=(jax.ShapeDtypeStruct((TP, D_IN), BF16), jax.ShapeDtypeStruct((TP, D), F32),
                   jax.ShapeDtypeStruct((n_m, tm, D), BF16), jax.ShapeDtypeStruct((N_DEV, D, W_IN_BLK), BF16)),
        compiler_params=pltpu.CompilerParams(dimension_semantics=("arbitrary", "arbitrary")),
    )(order, x_seq, meta_full, norm_g, w_in_b)


def _chip_rel(x, y, r):
    return (jnp.bitwise_xor(x, r >> 1), jnp.bitwise_xor(y, r & 1))


def _exchange_small(pack, srs):
    def body(pk, sr, pk_all, sr_all, send_sems, recv_sems, local_sems):
        x, y, c = _my_place()
        my_id = _dev_index(x, y, c)
        mine = [pltpu.make_async_copy(pk, pk_all.at[my_id], local_sems.at[0]),
                pltpu.make_async_copy(sr.at[my_id], sr_all.at[my_id], local_sems.at[1])]
        for cp in mine:
            cp.start()
        copies = []
        for r in range(1, N_DEV):
            peer = (jnp.bitwise_xor(x, (r >> 2) & 1), jnp.bitwise_xor(y, (r >> 1) & 1), jnp.bitwise_xor(c, r & 1))
            peer_id = _dev_index(*peer)
            for a, (src, dst) in enumerate(((pk, pk_all.at[my_id]), (sr.at[peer_id], sr_all.at[my_id]))):
                cp = pltpu.make_async_remote_copy(
                    src_ref=src, dst_ref=dst, send_sem=send_sems.at[a * 7 + r - 1], recv_sem=recv_sems.at[a * 7 + r - 1],
                    device_id=peer, device_id_type=MESH_ID)
                cp.start()
                copies.append(cp)
        for cp in copies:
            cp.wait_recv()
        for cp in copies:
            cp.wait_send()
        for cp in mine:
            cp.wait()

    return pl.pallas_call(
        body, name="exchange_small",
        out_shape=(jax.ShapeDtypeStruct((N_DEV,) + pack.shape, F32), jax.ShapeDtypeStruct(srs.shape, F32)),
        in_specs=[ANY, ANY], out_specs=(ANY, ANY),
        scratch_shapes=[pltpu.SemaphoreType.DMA((14,)), pltpu.SemaphoreType.DMA((14,)), pltpu.SemaphoreType.DMA((2,))],
    )(pack, srs)


N_CB = D // HEAD_W


def _store_by_cb(ref, idx, rows, val):
    for cb in range(N_CB):
        ref[(*idx, cb, rows, slice(None))] = val[:, cb * HEAD_W:(cb + 1) * HEAD_W]


def _fill_shifts(sh, tm):
    n = tm + HALO - 8
    for s in range(1, 8):
        for cb in range(N_CB):
            sh[s, cb, 0:n, :] = sh[0, cb, s:s + n, :]


def _gates(p_ref, lbl_ref, chunk, bsc):
    lb = _sigmoid(lbl_ref[0:1, :] - lbl_ref[1:2, :])
    q_raw = p_ref[:, 0:D].astype(F32)
    f_raw = p_ref[:, D:2 * D].astype(F32)
    sq = _sigmoid(q_raw)
    q = q_raw * sq
    sg = _sigmoid(f_raw)
    f = lb + (1.0 - lb) * sg
    row = lax.broadcasted_iota(jnp.int32, (CHUNK, 1), 0) + chunk * CHUNK
    valid = row >= PAD_FRONT
    lf = jnp.where(valid, jnp.log(f), 0.0)
    k = jnp.where(valid, 1.0 - f, 0.0)
    r_i = lax.broadcasted_iota(jnp.int32, (CHUNK, CHUNK), 0)
    c_i = lax.broadcasted_iota(jnp.int32, (CHUNK, CHUNK), 1)
    causal = r_i >= c_i
    bsc[...] = _tri_matmul(causal.astype(BF16), lf)
    b = bsc[...]
    b_mid = bsc[CHUNK // 2 - 1:CHUNK // 2, :]
    b_last = bsc[CHUNK - 1:CHUNK, :]
    e_q = jnp.exp(b)
    e_qm = jnp.exp(b - b_mid)
    e_km = jnp.exp(b_mid - b)
    e_kh = jnp.exp(b_last - b)
    e_last = jnp.exp(b_last)
    return dict(lb=lb, q_raw=q_raw, sq=sq, q=q, sg=sg, f=f, k=k, valid=valid, causal=causal,
                e_q=e_q, e_qm=e_qm, e_km=e_km, e_kh=e_kh, e_last=e_last)


def _rec_conv_fwd(proj, lb_logits, conv_w, conv_b, w3_b):
    cps = CHUNKS_PER_STEP
    tm = cps * CHUNK
    n_strip = CONV_STRIPS
    strip = tm // n_strip

    def body(p_ref, lbl_ref, pg_ref, w_ref, b_ref, w3s_ref, c0_ref, o_ref, s_ref, w3o_ref,
             st, bsc, sh, c0_sc, w3buf, send_sems, recv_sems, local_sems):
        n = pl.program_id(0)
        gather_start, gather_finish = _w3_gather(w3s_ref, w3o_ref, w3buf, send_sems, recv_sems, local_sems)

        @pl.when(n == 0)
        def _():
            st[...] = jnp.zeros_like(st)
            sh[0, :, 0:HALO, :] = jnp.zeros((N_CB, HALO, HEAD_W), F32)
            gather_start()

        @pl.when(n > 0)
        def _():
            sh[0, :, 0:HALO, :] = sh[0, :, tm:tm + HALO, :]

        ga = pg_ref[:, 0:D].astype(F32)
        gb = pg_ref[:, D:2 * D].astype(F32)
        _store_by_cb(sh, (0,), slice(HALO, HALO + tm), ga * _sigmoid(gb))
        _fill_shifts(sh, tm)

        def conv_unit(cb, s_i):
            cs = slice(cb * HEAD_W, (cb + 1) * HEAD_W)
            acc = jnp.broadcast_to(b_ref[:, cs], (strip, HEAD_W))
            for j in range(CONV_K):
                off = HALO - (CONV_K - 1) + j
                lo = s_i * strip + 8 * (off // 8)
                acc = acc + w_ref[j:j + 1, cs] * sh[off % 8, cb, lo:lo + strip, :]
            c0_sc[s_i * strip:(s_i + 1) * strip, cs] = acc

        units = [(cb, s_i) for cb in range(N_CB) for s_i in range(n_strip)]

        def prep(ci):
            g = _gates(p_ref.at[pl.ds(ci * CHUNK, CHUNK)], lbl_ref, n * cps + ci, bsc.at[ci])
            g["q1"] = (g["q"] * g["e_q"]).astype(BF16)
            g["qm"] = (g["q"] * g["e_qm"]).astype(BF16)
            g["km"] = (g["k"] * g["e_km"]).astype(BF16)
            g["kh"] = (g["k"] * g["e_kh"]).astype(BF16)
            return g

        def heads(ci, g):
            rs = pl.ds(ci * CHUNK, CHUNK)
            pv = p_ref.at[rs]
            s_ref[ci] = st[...]
            for h in range(HEADS):
                if units:
                    conv_unit(*units.pop(0))
                sl = slice(h * HEAD_W, (h + 1) * HEAD_W)
                v = pv[:, 2 * D + h * HEAD_W:2 * D + (h + 1) * HEAD_W]
                att = jnp.where(g["causal"], _dot_nt(g["qm"][:, sl], g["km"][:, sl]), 0.0).astype(BF16)
                s_h = st[h]
                o_ref[rs, sl] = (_dot_nt(g["q1"][:, sl], s_h.astype(BF16)) + _dot(att, v)).astype(ACT)
                st[h] = s_h * g["e_last"][:, sl] + _dot_tn(v, g["kh"][:, sl])

        ready = prep(0)
        for ci in range(cps):
            coming = prep(ci + 1) if ci + 1 < cps else None
            heads(ci, ready)
            ready = coming
        while units:
            conv_unit(*units.pop(0))
        c0_ref[...] = c0_sc[...].astype(ACT)

        @pl.when(n == N_CHUNK // cps - 1)
        def _():
            gather_finish()

    def rows_of(width, col):
        return pl.BlockSpec((tm, width), lambda n: (n, col))

    return pl.pallas_call(
        body, name="rec_conv_fwd", grid=(N_CHUNK // cps,),
        in_specs=[rows_of(3 * D, 1), pl.BlockSpec((2, D), lambda n: (0, 0)), rows_of(2 * D, 0),
                  pl.BlockSpec((CONV_K, D), lambda n: (0, 0)), pl.BlockSpec((1, D), lambda n: (0, 0)), ANY],
        out_specs=(rows_of(D, 0), rows_of(D, 0), pl.BlockSpec((cps, HEADS, HEAD_W, HEAD_W), lambda n: (n, 0, 0, 0)), ANY),
        out_shape=(jax.ShapeDtypeStruct((TP, D), ACT), jax.ShapeDtypeStruct((TP, D), ACT),
                   jax.ShapeDtypeStruct((N_CHUNK, HEADS, HEAD_W, HEAD_W), F32), jax.ShapeDtypeStruct((3, D, D), BF16)),
        scratch_shapes=[pltpu.VMEM((HEADS, HEAD_W, HEAD_W), F32), pltpu.VMEM((cps, CHUNK, D), F32),
                        pltpu.VMEM((8, N_CB, HALO + tm, HEAD_W), F32), pltpu.VMEM((tm, D), F32),
                        pltpu.VMEM((3, W_ROW_BLK, D), BF16), pltpu.SemaphoreType.DMA((7,)),
                        pltpu.SemaphoreType.DMA((7,)), pltpu.SemaphoreType.DMA((2,))],
        compiler_params=pltpu.CompilerParams(dimension_semantics=("arbitrary",)),
    )(proj, lb_logits, proj, conv_w, conv_b, w3_b)


def _mid(xin, tgt, o, c0, proj, w3, ln_g, ln_b, gnorm_g, final_g):
    tm = TM_ELT

    def body(x_ref, t_ref, o_ref, c0_ref, z_ref, gr_ref, mc_ref, mr_ref, w_ref, lng_ref, lnb_ref, gng_ref, fg_ref,
             do_ref, dc0_ref, dz_ref, dp_ref, a3_ref, b3_ref, red_ref, on_sc, don_sc):
        i = pl.program_id(0)

        @pl.when(i == 0)
        def _():
            red_ref[...] = jnp.zeros_like(red_ref)

        w_conv, w_rec, w_out = w_ref[0], w_ref[1], w_ref[2]
        c0v = c0_ref[...].astype(F32)
        mu = jnp.mean(c0v, axis=-1, keepdims=True)
        xc = c0v - mu
        rstd = lax.rsqrt(jnp.mean(xc * xc, axis=-1, keepdims=True) + EPS)
        xh = xc * rstd
        c1 = xh * lng_ref[...] + lnb_ref[...]
        s1 = _sigmoid(c1)
        c2 = c1 * s1
        z = z_ref[...].astype(F32)
        sz = _sigmoid(z)
        silu_z = z * sz
        u_conv = (c2 * silu_z).astype(BF16)
        y_conv = _dot(u_conv, w_conv)
        ov = o_ref[...].astype(F32)
        r3 = []
        for h in range(HEADS):
            sl = slice(h * HEAD_W, (h + 1) * HEAD_W)
            oh = ov[:, sl]
            r_h = lax.rsqrt(jnp.mean(oh * oh, axis=-1, keepdims=True) + EPS)
            r3.append(r_h)
            on_sc[:, sl] = oh * r_h
        o_n = on_sc[...]
        o_g = o_n * gng_ref[...]
        gr = gr_ref[...].astype(F32)
        sgr = _sigmoid(gr)
        silu_g = gr * sgr
        u_rec = (o_g * silu_g).astype(BF16)
        y_rec = _dot(u_rec, w_rec)
        mc = mc_ref[...].astype(F32)
        mr = mr_ref[...].astype(F32)
        smc = _sigmoid(mc)
        smr = _sigmoid(mr)
        merged = (smc * y_conv + smr * y_rec).astype(BF16)
        res = x_ref[...] + _dot(merged, w_out)
        r2 = lax.rsqrt(jnp.mean(res * res, axis=-1, keepdims=True) + EPS)
        xh2 = res * r2
        row = lax.broadcasted_iota(jnp.int32, (tm, 1), 0) + i * tm
        real = row >= ROW0
        tgt = t_ref[...]
        tgt = jnp.where(i == 0, pltpu.roll(tgt, ROW0, 0), tgt)
        diff = jnp.where(real, xh2 * fg_ref[...] - tgt, 0.0)
        d_y = diff * (1.0 / D)
        d_xh2 = d_y * fg_ref[...]
        d_res = r2 * (d_xh2 - xh2 * jnp.mean(d_xh2 * xh2, axis=-1, keepdims=True))
        d_res_b = d_res.astype(BF16)
        d_merged = _dot_nt(d_res_b, w_out)
        d_yc = (d_merged * smc).astype(BF16)
        d_yr = (d_merged * smr).astype(BF16)
        dp_ref[:, D:2 * D] = (d_merged * y_conv * smc * (1.0 - smc)).astype(BF16)
        dp_ref[:, 2 * D:3 * D] = (d_merged * y_rec * smr * (1.0 - smr)).astype(BF16)
        d_ur = _dot_nt(d_yr, w_rec)
        d_og = d_ur * silu_g
        dp_ref[:, 0:D] = (d_ur * o_g * _dsilu(gr, sgr)).astype(BF16)
        d_on = d_og * gng_ref[...]
        for h in range(HEADS):
            sl = slice(h * HEAD_W, (h + 1) * HEAD_W)
            d_h = d_on[:, sl]
            n_h = o_n[:, sl]
            don_sc[:, sl] = r3[h] * (d_h - n_h * jnp.mean(d_h * n_h, axis=-1, keepdims=True))
        do_ref[...] = don_sc[...].astype(ACT)
        d_uc = _dot_nt(d_yc, w_conv)
        d_c2 = d_uc * silu_z
        dz_ref[...] = (d_uc * c2 * _dsilu(z, sz)).astype(BF16)
        d_c1 = d_c2 * _dsilu(c1, s1)
        d_xh = d_c1 * lng_ref[...]
        d_c0 = rstd * (d_xh - jnp.mean(d_xh, axis=-1, keepdims=True)
                       - xh * jnp.mean(d_xh * xh, axis=-1, keepdims=True))
        dc0_ref[...] = d_c0.astype(ACT)
        a3_ref[0] = u_conv
        b3_ref[0] = d_yc
        a3_ref[1] = u_rec
        b3_ref[1] = d_yr
        a3_ref[2] = merged
        b3_ref[2] = d_res_b
        def colsum(vv):
            return jnp.sum(vv, axis=0, keepdims=True)

        red_ref[0:1, :] += colsum(d_y * xh2)
        red_ref[1:2, :] += colsum(d_og * o_n)
        red_ref[2:3, :] += colsum(d_c1 * xh)
        red_ref[3:4, :] += colsum(d_c1)
        red_ref[4:5, :] += colsum(d_c0)
        red_ref[5:6, :] += colsum(diff * diff) * (0.5 / D)

    def row_block(width, col):
        return pl.BlockSpec((tm, width), lambda i: (i, col))

    def const_block(shape):
        return pl.BlockSpec(shape, lambda i: (0,) * len(shape))

    stack = jax.ShapeDtypeStruct((3, TP, D), BF16)
    stack_spec = pl.BlockSpec((3, tm, D), lambda i: (0, i, 0))
    return pl.pallas_call(
        body, name="mid", grid=(TP // tm,),
        in_specs=[row_block(D, 0),
                  pl.BlockSpec((pl.Element(tm), pl.Element(D)), lambda i: (_window_start(i, tm), 0)),
                  row_block(D, 0), row_block(D, 0),
                  row_block(D, 2), row_block(D, 6), row_block(D, 7), row_block(D, 8),
                  pl.BlockSpec((3, D, D), lambda i: (0, 0, 0), pipeline_mode=pl.Buffered(1)),
                  const_block((1, D)), const_block((1, D)), const_block((1, D)), const_block((1, D))],
        out_specs=(row_block(D, 0), row_block(D, 0), row_block(D, 0), row_block(3 * D, 2),
                   stack_spec, stack_spec, const_block((8, D))),
        out_shape=(jax.ShapeDtypeStruct((TP, D), ACT), jax.ShapeDtypeStruct((TP, D), ACT),
                   jax.ShapeDtypeStruct((TP, D), BF16),
                   jax.ShapeDtypeStruct((TP, D_IN), BF16), stack, stack, jax.ShapeDtypeStruct((8, D), F32)),
        scratch_shapes=[pltpu.VMEM((tm, D), F32), pltpu.VMEM((tm, D), F32)],
        compiler_params=pltpu.CompilerParams(dimension_semantics=("arbitrary",), vmem_limit_bytes=60 * 1024 * 1024),
    )(xin, tgt, o, c0, proj, proj, proj, proj, w3, ln_g, ln_b, gnorm_g, final_g)


def _rec_conv_bwd(proj, lb_logits, d_o, s_start, d_c0, d_z, conv_w, dproj, p3):
    cps = CHUNKS_PER_STEP
    tm = cps * CHUNK
    last = N_CHUNK // cps - 1
    n_strip = CONV_STRIPS
    strip = tm // n_strip

    def body(p_ref, lbl_ref, do_ref, s_ref, pg_ref, dc_ref, dz_ref, w_ref, dproj_in, p3_ref,
             dp_ref, dlb_ref, dw_ref, land_ref,
             dst, bsc, dq_sc, dk_sc, g_sc, dsh, a_sc, da_sc, acc, send_sems, recv_sems):
        del dproj_in
        n = pl.program_id(0)
        ride_start, ride_finish = _p3_to_sibling(p3_ref, land_ref, send_sems, recv_sems)

        @pl.when(n == 0)
        def _():
            ride_start()
            dst[...] = jnp.zeros_like(dst)
            dlb_ref[...] = jnp.zeros_like(dlb_ref)
            dsh[0, :, tm:tm + HALO, :] = jnp.zeros((N_CB, HALO, HEAD_W), F32)
            acc[...] = jnp.zeros_like(acc)

        @pl.when(n > 0)
        def _():
            dsh[0, :, tm:tm + HALO, :] = dsh[0, :, 0:HALO, :]

        _store_by_cb(dsh, (0,), slice(0, tm), dc_ref[...].astype(F32))
        _fill_shifts(dsh, tm)
        ga = pg_ref[:, 0:D].astype(F32)
        sb = _sigmoid(pg_ref[:, D:2 * D].astype(F32))
        a = ga * sb
        _store_by_cb(a_sc, (), slice(0, tm), a)

        def conv_unit(cb, st):
            cs = slice(cb * HEAD_W, (cb + 1) * HEAD_W)
            rows = slice(st * strip, (st + 1) * strip)
            a_s = a_sc[cb, rows, :]
            d_a = jnp.zeros((strip, HEAD_W), F32)
            for j in range(CONV_K):
                off = CONV_K - 1 - j
                lo = st * strip + 8 * (off // 8)
                slab = dsh[off % 8, cb, lo:lo + strip, :]
                d_a = d_a + w_ref[j:j + 1, cs] * slab
                acc[j, :, cs] += jnp.sum((a_s * slab).reshape(strip // 8, 8, HEAD_W), axis=0)
            da_sc[rows, cs] = d_a

        units = [(cb, st) for cb in range(N_CB) for st in range(n_strip)]

        def prep(ci):
            g = _gates(p_ref.at[pl.ds(ci * CHUNK, CHUNK)], lbl_ref, (last - n) * cps + ci, bsc.at[ci])
            g["q1"] = (g["q"] * g["e_q"]).astype(BF16)
            qm_f = g["q"] * g["e_qm"]
            km_f = g["k"] * g["e_km"]
            g["qm"] = qm_f.astype(BF16)
            g["km"] = km_f.astype(BF16)
            g["qm_lo"] = (qm_f - g["qm"].astype(F32)).astype(BF16)
            g["km_lo"] = (km_f - g["km"].astype(F32)).astype(BF16)
            g["kh_f"] = g["k"] * g["e_kh"]
            g["kh"] = g["kh_f"].astype(BF16)
            return g

        def heads_and_post(ci, g):
            rs = pl.ds(ci * CHUNK, CHUNK)
            pv = p_ref.at[rs]
            dpv = dp_ref.at[rs]
            q1, qm, km, qm_lo, km_lo, kh_f, kh = (g[k] for k in ("q1", "qm", "km", "qm_lo", "km_lo", "kh_f", "kh"))
            for h in range(HEADS):
                if units:
                    conv_unit(*units.pop(0))
                sl = slice(h * HEAD_W, (h + 1) * HEAD_W)
                v = pv[:, 2 * D + h * HEAD_W:2 * D + (h + 1) * HEAD_W]
                d_oh = do_ref[rs, sl].astype(BF16)
                s0 = s_ref[ci, h]
                ds_end = dst[h]
                ds_end_b = ds_end.astype(BF16)
                att = jnp.where(g["causal"], _dot_nt(qm[:, sl], km[:, sl]), 0.0).astype(BF16)
                d_att = jnp.where(g["causal"], _dot_nt(d_oh, v), 0.0).astype(BF16)
                d_v = _dot_tn(att, d_oh) + _dot_nt(kh[:, sl], ds_end_b)
                d_qm2 = _dot(d_att, jnp.concatenate([km[:, sl], km_lo[:, sl]], axis=1))
                d_qm = d_qm2[:, 0:HEAD_W] + d_qm2[:, HEAD_W:2 * HEAD_W]
                d_q1 = _dot(d_oh, s0.astype(BF16))
                d_km2 = _dot_tn(d_att, jnp.concatenate([qm[:, sl], qm_lo[:, sl]], axis=1))
                d_km = d_km2[:, 0:HEAD_W] + d_km2[:, HEAD_W:2 * HEAD_W]
                d_kh = _dot(v, ds_end_b)
                dq_sc[ci, :, sl] = d_qm * g["e_qm"][:, sl] + d_q1 * g["e_q"][:, sl]
                dk_sc[ci, :, sl] = d_km * g["e_km"][:, sl] + d_kh * g["e_kh"][:, sl]
                g_sc[ci, :, sl] = (jnp.sum(kh_f[:, sl] * d_kh, axis=0, keepdims=True)
                                   + g["e_last"][:, sl] * jnp.sum(ds_end * s0, axis=0, keepdims=True))
                dst[h] = ds_end * g["e_last"][:, sl] + _dot_tn(d_oh, q1[:, sl])
                dpv[:, 5 * D + h * HEAD_W:5 * D + (h + 1) * HEAD_W] = d_v.astype(BF16)
            d_q = dq_sc[ci]
            d_k = dk_sc[ci]
            d_b = g["q"] * d_q - g["k"] * d_k
            anti = jnp.logical_not(g["causal"]) | (lax.broadcasted_iota(jnp.int32, (CHUNK, CHUNK), 0)
                                                    == lax.broadcasted_iota(jnp.int32, (CHUNK, CHUNK), 1))
            d_lf = _tri_matmul(anti.astype(BF16), d_b) + g_sc[ci]
            d_f = jnp.where(g["valid"], d_lf / g["f"] - d_k, 0.0)
            sg = g["sg"]
            dlb_ref[0:1, :] += jnp.sum(d_f * (1.0 - sg), axis=0, keepdims=True)
            dpv[:, 3 * D:4 * D] = (d_q * _dsilu(g["q_raw"], g["sq"])).astype(BF16)
            dpv[:, 4 * D:5 * D] = (d_f * (1.0 - g["lb"]) * sg * (1.0 - sg)).astype(BF16)

        ready = prep(cps - 1)
        for ci in reversed(range(cps)):
            coming = prep(ci - 1) if ci > 0 else None
            heads_and_post(ci, ready)
            ready = coming
        while units:
            conv_unit(*units.pop(0))

        d_a = da_sc[...]
        dp_ref[:, 0:D] = (d_a * sb).astype(BF16)
        dp_ref[:, D:2 * D] = (d_a * a * (1.0 - sb)).astype(BF16)
        dp_ref[:, 2 * D:3 * D] = dz_ref[...]

        @pl.when(n == last)
        def _():
            for j in range(CONV_K):
                dw_ref[j:j + 1, :] = jnp.sum(acc[j], axis=0, keepdims=True)
            dw_ref[CONV_K:CONV_K + 1, :] = jnp.zeros((1, D), F32)
            ride_finish()

    def rows_of(width, col):
        return pl.BlockSpec((tm, width), lambda n: (last - n, col))

    return pl.pallas_call(
        body, name="rec_conv_bwd", grid=(N_CHUNK // cps,),
        in_specs=[rows_of(3 * D, 1), pl.BlockSpec((2, D), lambda n: (0, 0)), rows_of(D, 0),
                  pl.BlockSpec((cps, HEADS, HEAD_W, HEAD_W), lambda n: (last - n, 0, 0, 0)),
                  rows_of(2 * D, 0), rows_of(D, 0), rows_of(D, 0), pl.BlockSpec((CONV_K, D), lambda n: (0, 0)), ANY, ANY],
        out_specs=(rows_of(6 * D, 0), pl.BlockSpec((8, D), lambda n: (0, 0)),
                   pl.BlockSpec((CONV_K + 1, D), lambda n: (0, 0)), ANY),
        out_shape=(jax.ShapeDtypeStruct((TP, D_IN), BF16), jax.ShapeDtypeStruct((8, D), F32),
                   jax.ShapeDtypeStruct((CONV_K + 1, D), F32), jax.ShapeDtypeStruct((4, 3, W_ROW_BLK, D), F32)),
        scratch_shapes=[pltpu.VMEM((HEADS, HEAD_W, HEAD_W), F32), pltpu.VMEM((cps, CHUNK, D), F32),
                        pltpu.VMEM((cps, CHUNK, D), F32), pltpu.VMEM((cps, CHUNK, D), F32),
                        pltpu.VMEM((cps, 1, D), F32),
                        pltpu.VMEM((8, N_CB, tm + HALO, HEAD_W), F32), pltpu.VMEM((N_CB, tm, HEAD_W), F32),
                        pltpu.VMEM((tm, D), F32), pltpu.VMEM((CONV_K, 8, D), F32),
                        pltpu.SemaphoreType.DMA((4,)), pltpu.SemaphoreType.DMA((4,))],
        input_output_aliases={8: 0},
        compiler_params=pltpu.CompilerParams(dimension_semantics=("arbitrary",)),
    )(proj, lb_logits, d_o, s_start, proj, d_c0, d_z, conv_w, dproj, p3)


def _wgrad3(a3, b3):
    tt = TP

    def body(a_ref, b_ref, o_ref):
        @pl.when(pl.program_id(1) == 0)
        def _():
            o_ref[...] = jnp.zeros_like(o_ref)

        o_ref[0] += _dot_tn(a_ref[0], b_ref[0])

    return pl.pallas_call(
        body, name="wgrad3", grid=(3, TP // tt),
        in_specs=[pl.BlockSpec((1, tt, D), lambda g, t: (g, t, 0)), pl.BlockSpec((1, tt, D), lambda g, t: (g, t, 0))],
        out_specs=pl.BlockSpec((1, D, D), lambda g, t: (g, 0, 0)),
        out_shape=jax.ShapeDtypeStruct((3, D, D), F32),
        compiler_params=pltpu.CompilerParams(dimension_semantics=("arbitrary", "arbitrary")),
    )(a3, b3)


def _wgrad_in(h, dproj, ids, chip1b):
    tt = TT_WGRAD
    n_t = TP // tt

    def body(ids_ref, a_ref, b_ref, c1_ref, o_ref, ob_ref, l0_ref, far_ref, acc, tmp, send_sems, recv_sems, tmp_sem,
             far_send_sems, far_recv_sems):
        del ids_ref
        r = pl.program_id(0)
        t = pl.program_id(1)
        x, y, c = _my_place()
        sibling = (x, y, 1 - c)
        slot = lax.rem(r, 2)
        ride_start, ride_finish = _partials_to_owners(c1_ref, far_ref, far_send_sems, far_recv_sems)

        @pl.when((r == 0) & (t == 0))
        def _():
            ride_start()

        def send_in(q):
            return pltpu.make_async_remote_copy(
                src_ref=acc.at[q % 2], dst_ref=l0_ref.at[q], send_sem=send_sems.at[q], recv_sem=recv_sems.at[q],
                device_id=sibling, device_id_type=MESH_ID)

        def landed(q):
            return pltpu.make_async_copy(l0_ref.at[q], tmp, tmp_sem)

        @pl.when(t == 0)
        def _():
            acc[slot] = jnp.zeros((D, W_IN_BLK), F32)

        acc[slot] += _dot_tn(a_ref[...], b_ref[...])

        for q in range(4):
            @pl.when((r == q) & (t == n_t - 1))
            def _(q=q):
                if q >= 1:
                    send_in(q - 1).wait_send()
                send_in(q).start()

            @pl.when((r == 4 + q) & (t == n_t - 2))
            def _(q=q):
                if q == 0:
                    send_in(3).wait_send()
                send_in(q).wait_recv()
                landed(q).start()

            @pl.when((r == 4 + q) & (t == n_t - 1))
            def _(q=q):
                landed(q).wait()
                tot = acc[q % 2] + tmp[...]
                o_ref[0] = tot
                ob_ref[0] = tot.astype(BF16)

        @pl.when((r == N_DEV - 1) & (t == n_t - 1))
        def _():
            ride_finish()

    blk = pl.BlockSpec((1, D, W_IN_BLK), lambda r, t, ids: (jnp.maximum(r - 4, 0), 0, 0))
    return pl.pallas_call(
        body, name="wgrad_in",
        grid_spec=pltpu.PrefetchScalarGridSpec(
            num_scalar_prefetch=1, grid=(N_DEV, n_t),
            in_specs=[pl.BlockSpec((tt, D), lambda r, t, ids: (t, 0)),
                      pl.BlockSpec((tt, W_IN_BLK), lambda r, t, ids: (t, ids[r])), ANY],
            out_specs=(blk, blk, ANY, ANY),
            scratch_shapes=[pltpu.VMEM((2, D, W_IN_BLK), F32), pltpu.VMEM((D, W_IN_BLK), F32),
                            pltpu.SemaphoreType.DMA((4,)), pltpu.SemaphoreType.DMA((4,)), pltpu.SemaphoreType.DMA,
                            pltpu.SemaphoreType.DMA((3,)), pltpu.SemaphoreType.DMA((3,))]),
        out_shape=(jax.ShapeDtypeStruct((4, D, W_IN_BLK), F32), jax.ShapeDtypeStruct((4, D, W_IN_BLK), BF16),
                   jax.ShapeDtypeStruct((4, D, W_IN_BLK), F32), jax.ShapeDtypeStruct((3, 3, W_ROW_BLK, D), BF16)),
        compiler_params=pltpu.CompilerParams(dimension_semantics=("arbitrary", "arbitrary")),
    )(ids, h, dproj, chip1b)


def _chip_sum_3(p3, land1, ids_mine):
    def body(ids_ref, p_ref, l_ref, o_ref, ob_ref):
        del ids_ref
        tot = p_ref[...] + l_ref[0]
        o_ref[0] = tot
        ob_ref[0] = tot.astype(BF16)

    blk = pl.BlockSpec((1, 3, W_ROW_BLK, D), lambda r, ids: (r, 0, 0, 0))
    return pl.pallas_call(
        body, name="chip_sum_3",
        grid_spec=pltpu.PrefetchScalarGridSpec(
            num_scalar_prefetch=1, grid=(4,),
            in_specs=[pl.BlockSpec((3, W_ROW_BLK, D), lambda r, ids: (0, ids[r], 0)), blk],
            out_specs=(blk, blk)),
        out_shape=(jax.ShapeDtypeStruct((4, 3, W_ROW_BLK, D), F32), jax.ShapeDtypeStruct((4, 3, W_ROW_BLK, D), BF16)),
    )(ids_mine, p3, land1)


def _dh_and_norm_bwd(dproj, w_in_full, xin, b3, norm_g, chip0b):
    tm = TM_MAT
    n_k = N_DEV // DH_K_BLKS
    n_m = TP // tm

    def body(dp_ref, w_ref, x_ref, dr_ref, g_ref, c0_ref, dx_ref, dg_ref, f0_ref, acc, send_sems, recv_sems):
        m = pl.program_id(0)
        k = pl.program_id(1)
        ride_start, ride_finish = _partials_to_owners(c0_ref, f0_ref, send_sems, recv_sems)

        @pl.when((m == 0) & (k == 0))
        def _():
            ride_start()

        @pl.when(k == 0)
        def _():
            acc[...] = jnp.zeros_like(acc)

        part = _dot_nt(dp_ref[:, 0:W_IN_BLK], w_ref[0])
        for j in range(1, DH_K_BLKS):
            part = part + _dot_nt(dp_ref[:, j * W_IN_BLK:(j + 1) * W_IN_BLK], w_ref[j])
        acc[...] += part

        @pl.when((k == n_k - 1) & (m == 0))
        def _():
            dg_ref[...] = jnp.zeros_like(dg_ref)

        @pl.when(k == n_k - 1)
        def _():
            xv = x_ref[...]
            r1 = lax.rsqrt(jnp.mean(xv * xv, axis=-1, keepdims=True) + EPS)
            xh = xv * r1
            d_h = acc[...]
            dg_ref[0:1, :] += jnp.sum(d_h * xh, axis=0, keepdims=True)
            d_xh = d_h * g_ref[...]
            dx_ref[...] = dr_ref[0].astype(F32) + r1 * (d_xh - xh * jnp.mean(d_xh * xh, axis=-1, keepdims=True))

        @pl.when((m == n_m - 1) & (k == n_k - 1))
        def _():
            ride_finish()

    return pl.pallas_call(
        body, name="dh_norm_bwd", grid=(n_m, n_k),
        in_specs=[pl.BlockSpec((tm, DH_K_BLKS * W_IN_BLK), lambda m, k: (m, k)),
                  pl.BlockSpec((DH_K_BLKS, D, W_IN_BLK), lambda m, k: (k, 0, 0)),
                  pl.BlockSpec((tm, D), lambda m, k: (m, 0)), pl.BlockSpec((1, tm, D), lambda m, k: (2, m, 0)),
                  pl.BlockSpec((1, D), lambda m, k: (0, 0)), ANY],
        out_specs=(pl.BlockSpec((tm, D), lambda m, k: (m, 0)), pl.BlockSpec((8, D), lambda m, k: (0, 0)), ANY),
        out_shape=(jax.ShapeDtypeStruct((TP, D), F32), jax.ShapeDtypeStruct((8, D), F32),
                   jax.ShapeDtypeStruct((3, D, W_IN_BLK), BF16)),
        scratch_shapes=[pltpu.VMEM((tm, D), F32), pltpu.SemaphoreType.DMA((3,)), pltpu.SemaphoreType.DMA((3,))],
        compiler_params=pltpu.CompilerParams(dimension_semantics=("arbitrary", "arbitrary")),
    )(dproj, w_in_full, xin, b3, norm_g, chip0b)


def _sum_adamw(own, landed, w, m, v, tr, name):
    rows, cols = w.shape
    n_t = rows // tr

    def body(o_ref, l1_ref, l2_ref, l3_ref, w_ref, m_ref, v_ref, g_ref, d_ref, m2_ref, v2_ref):
        g = ((o_ref[...] + l1_ref[...].astype(F32)) + l2_ref[...].astype(F32)) + l3_ref[...].astype(F32)
        delta, m2, v2 = _adamw(w_ref[...], g, m_ref[...], v_ref[...])
        g_ref[...] = g
        d_ref[...] = delta
        m2_ref[...] = m2
        v2_ref[...] = v2

    def spec(k):
        return pl.BlockSpec((tr, cols), lambda i: (i + k * n_t, 0))

    out = jax.ShapeDtypeStruct((rows, cols), F32)
    return pl.pallas_call(
        body, name=name, grid=(n_t,),
        in_specs=[spec(0), spec(0), spec(1), spec(2), spec(0), spec(0), spec(0)],
        out_specs=(spec(0),) * 4, out_shape=(out,) * 4,
    )(own, landed, landed, landed, w, m, v)


def _adamw_3(chip1, far1, ws, ms, vs):
    def body(c_ref, f_ref, *refs):
        w_refs, m_refs, v_refs, outs = refs[0:3], refs[3:6], refs[6:9], refs[9:21]
        for k in range(3):
            g = ((c_ref[0, k] + f_ref[0, k].astype(F32)) + f_ref[1, k].astype(F32)) + f_ref[2, k].astype(F32)
            delta, m2, v2 = _adamw(w_refs[k][0], g, m_refs[k][0], v_refs[k][0])
            for kind, val in enumerate((g, delta, m2, v2)):
                outs[3 * kind + k][0] = val

    full = pl.BlockSpec((1, W_ROW_BLK, D), lambda i: (0, 0, 0))
    out = jax.ShapeDtypeStruct((1, W_ROW_BLK, D), F32)
    res = pl.pallas_call(
        body, name="adamw_3", grid=(1,),
        in_specs=[pl.BlockSpec((1, 3, W_ROW_BLK, D), lambda i: (0, 0, 0, 0)),
                  pl.BlockSpec((3, 3, W_ROW_BLK, D), lambda i: (0, 0, 0, 0))] + [full] * 9,
        out_specs=(full,) * 12, out_shape=(out,) * 12,
    )(chip1, far1, *ws, *ms, *vs)
    return tuple(res[3 * kind:3 * kind + 3] for kind in range(4))


N_SMALL = 9


def _small_update(pack_all, srs_all, ws, ms, vs):
    def body(pk_ref, sr_ref, *refs):
        w_refs, m_refs, v_refs = refs[0:N_SMALL], refs[N_SMALL:2 * N_SMALL], refs[2 * N_SMALL:3 * N_SMALL]
        loss_ref = refs[3 * N_SMALL]
        outs = refs[3 * N_SMALL + 1:7 * N_SMALL + 1]
        tot_sc, tots_sc = refs[7 * N_SMALL + 1:]
        tot = pk_ref[0]
        tot_s = sr_ref[0]
        for d in range(1, N_DEV):
            tot = tot + pk_ref[d]
            tot_s = tot_s + sr_ref[d]
        tot_sc[...] = tot
        tots_sc[...] = tot_s
        loss_ref[...] = jnp.sum(tot_sc[5:6, :], axis=1, keepdims=True)
        lbl = w_refs[4]
        p0 = _sigmoid(lbl[0:1, :] - lbl[1:2, :])
        d_l0 = tot_sc[4:5, :] * p0 * (1.0 - p0)

        def update(k, sel, g):
            delta, m2, v2 = _adamw(w_refs[k][sel], g, m_refs[k][sel], v_refs[k][sel])
            for kind, val in enumerate((g, delta, m2, v2)):
                outs[N_SMALL * kind + k][sel] = val

        everything = (slice(None), slice(None))
        for k, row in ((0, 0), (1, 1), (2, 2), (3, 3), (5, 6), (6, 7)):
            update(k, everything, tot_sc[row:row + 1, :])
        update(4, (slice(0, 1), slice(None)), d_l0)
        update(4, (slice(1, 2), slice(None)), -d_l0)
        update(7, (0, slice(None), slice(None)), tots_sc[0:CONV_K, :])
        update(8, everything, tots_sc[META_ROW:META_ROW + N_META, :])

    shapes = [jax.ShapeDtypeStruct(w.shape, F32) for w in ws]
    res = pl.pallas_call(
        body, name="small_update",
        out_shape=(jax.ShapeDtypeStruct((1, 1), F32), *(shapes * 4)),
        scratch_shapes=[pltpu.VMEM((8, D), F32), pltpu.VMEM((SMALL_ROWS, HEAD_W), F32)],
    )(pack_all, srs_all, *ws, *ms, *vs)
    return res[0], tuple(res[1 + N_SMALL * kind:1 + N_SMALL * (kind + 1)] for kind in range(4))


def _local_step(xin, proj, target, conv_w_full, conv_b, ln_g, ln_b, w3_b, lb_logits, gnorm_g, final_g, ids_mine):
    fg = final_g.reshape(1, D)
    c0, o, s_start, w3_full = _rec_conv_fwd(proj, lb_logits, conv_w_full, conv_b, w3_b)
    d_o, d_c0, d_z, dproj, a3, b3, red = _mid(xin, target, o, c0, proj, w3_full, ln_g, ln_b, gnorm_g, fg)
    p3 = _wgrad3(a3, b3)
    dproj, dlb, d_conv_w, land1 = _rec_conv_bwd(proj, lb_logits, d_o, s_start, d_c0, d_z, conv_w_full, dproj, p3)
    chip1, chip1b = _chip_sum_3(p3, land1, ids_mine)
    return dproj, b3, p3, chip1, chip1b, d_conv_w, red, dlb


def kernel(x, meta_tokens, norm_g, w_in, conv_w, conv_b, ln_g, ln_b, w_conv_out, lb_logits, gnorm_g, w_rec_out, w_out, final_g, loss_target, m_meta_tokens, m_norm_g, m_w_in, m_conv_w, m_conv_b, m_ln_g, m_ln_b, m_w_conv_out, m_lb_logits, m_gnorm_g, m_w_rec_out, m_w_out, m_final_g, v_meta_tokens, v_norm_g, v_w_in, v_conv_w, v_conv_b, v_ln_g, v_ln_b, v_w_conv_out, v_lb_logits, v_gnorm_g, v_w_rec_out, v_w_out, v_final_g):
    mx, my, mc = _my_place()

    ws_s = jnp.concatenate([conv_w[0], jnp.zeros((1, HEAD_W), F32), meta_tokens], axis=0)
    small_all, w_in_b, w3_b = _gather_small_and_cast(ws_s, w_in[0], w_conv_out, w_rec_out, w_out)
    small_full = jnp.transpose(small_all, (1, 0, 2)).reshape(SMALL_ROWS, D)
    conv_w_full = small_full[0:CONV_K]
    meta_full = small_full[META_ROW:META_ROW + N_META]
    first, second, diag = _gather_chips(mx, my, mc)
    use_order = [(mx, my, mc), (mx, my, 1 - mc), (*first, mc), (*second, 1 - mc), (*second, mc), (*first, 1 - mc),
                 (*diag, mc), (*diag, 1 - mc)]
    order = jnp.stack([_dev_index(*p) for p in use_order]).astype(jnp.int32)
    proj, xin, h, w_in_full = _gather_and_proj(x[0], meta_full, norm_g, w_in_b, order)
    h = h.reshape(TP, D)

    ids_mine = jnp.stack([_dev_index(*_chip_rel(mx, my, r), mc) for r in range(4)]).astype(jnp.int32)
    ids_sib = jnp.stack([_dev_index(*_chip_rel(mx, my, r), 1 - mc) for r in range(4)]).astype(jnp.int32)
    dproj, b3, _, chip1, chip1b, d_conv_w, red, dlb = _local_step(
        xin, proj, loss_target[0], conv_w_full, conv_b, ln_g, ln_b, w3_b, lb_logits, gnorm_g, final_g, ids_mine)

    chip0, chip0b, _, far1 = _wgrad_in(h, dproj, jnp.concatenate([ids_sib, ids_mine]), chip1b)
    d_xin, dng, far0 = _dh_and_norm_bwd(dproj, w_in_full, xin, b3, norm_g, chip0b)
    pack = jnp.concatenate([dng[0:1], red[4:5], red[2:3], red[3:4], dlb[0:1], red[5:6], red[1:2], red[0:1]], axis=0)
    g_in, d_in, m_in, v_in = _sum_adamw(chip0.reshape(4 * D, W_IN_BLK), far0.reshape(3 * D, W_IN_BLK), w_in[0],
                                        m_w_in[0], v_w_in[0], 256, "adamw_in")
    big3 = _adamw_3(chip1, far1, (w_conv_out, w_rec_out, w_out), (m_w_conv_out, m_w_rec_out, m_w_out),
                    (v_w_conv_out, v_w_rec_out, v_w_out))

    srs = jnp.concatenate([d_conv_w, d_xin[PAD_FRONT:ROW0]], axis=0)
    srs = jnp.transpose(srs.reshape(SMALL_ROWS, N_DEV, HEAD_W), (1, 0, 2))
    pack_all, srs_all = _exchange_small(pack, srs)
    loss, small = _small_update(
        pack_all, srs_all,
        (norm_g, conv_b, ln_g, ln_b, lb_logits, gnorm_g, final_g.reshape(1, D), conv_w, meta_tokens),
        (m_norm_g, m_conv_b, m_ln_g, m_ln_b, m_lb_logits, m_gnorm_g, m_final_g.reshape(1, D), m_conv_w, m_meta_tokens),
        (v_norm_g, v_conv_b, v_ln_g, v_ln_b, v_lb_logits, v_gnorm_g, v_final_g.reshape(1, D), v_conv_w, v_meta_tokens))

    outs = [loss.reshape(()), d_xin[ROW0:][None]]
    for kind, a_in in enumerate((g_in, d_in, m_in, v_in)):
        ng, cb, lg, lb_, lbl, gg, fg, cw, mt = small[kind]
        a_3 = big3[kind]
        outs += [mt, ng, a_in[None], cw, cb, lg, lb_, a_3[0], lbl, gg, a_3[1], a_3[2], fg.reshape(D)]
    return tuple(outs)
```

```python
import jax
import jax.numpy as jnp
from jax import lax
from jax.experimental import pallas as pl
from jax.experimental.pallas import tpu as pltpu

F32 = jnp.float32
BF16 = jnp.bfloat16
ACT = BF16

D = 1024
SEQ = 4096
N_META = 16
CHUNK = 64
PAD_FRONT = 48
ROW0 = PAD_FRONT + N_META
TP = ROW0 + SEQ
N_CHUNK = TP // CHUNK
HEADS = 8
HEAD_W = 128
D_IN = 9 * D
N_DEV = 8
W_IN_BLK = D_IN // N_DEV
W_ROW_BLK = D // N_DEV
CONV_K = 31
SMALL_ROWS = 48
META_ROW = 32
EPS = 1e-6
HALO = 32

TM_MAT = 1040
TT_WGRAD = 2080
DH_K_BLKS = 2
TM_ELT = 208
CHUNKS_PER_STEP = 5
CONV_STRIPS = 5

ADAM_LR = 0.001
ADAM_B1 = 0.9
ADAM_B2 = 0.999
ADAM_EPS = 1e-08
ADAM_WD = 0.01
ADAM_STEP = 10

MESH_ID = pl.DeviceIdType.MESH
ANY = pl.BlockSpec(memory_space=pl.ANY)


def _sigmoid(v):
    return jax.nn.sigmoid(v)


def _dsilu(v, s):
    return s * (1.0 + v * (1.0 - s))


def _dot(a, b):
    return jnp.dot(a, b, preferred_element_type=F32)


def _dot_nt(a, b):
    return lax.dot_general(a, b, (((1,), (1,)), ((), ())), preferred_element_type=F32)


def _dot_tn(a, b):
    return lax.dot_general(a, b, (((0,), (0,)), ((), ())), preferred_element_type=F32)


def _split3(v):
    hi = v.astype(BF16)
    r1 = v - hi.astype(F32)
    mid = r1.astype(BF16)
    lo = (r1 - mid.astype(F32)).astype(BF16)
    return hi, mid, lo


def _tri_matmul(tri, v):
    hi, mid, lo = _split3(v)
    return _dot(tri, hi) + _dot(tri, mid) + _dot(tri, lo)


def _adamw(w, g, m, v):
    m2 = ADAM_B1 * m + (1.0 - ADAM_B1) * g
    v2 = ADAM_B2 * v + (1.0 - ADAM_B2) * jnp.square(g)
    m_hat = m2 / (1.0 - ADAM_B1 ** ADAM_STEP)
    v_hat = v2 / (1.0 - ADAM_B2 ** ADAM_STEP)
    delta = -ADAM_LR * (m_hat / (jnp.sqrt(v_hat) + ADAM_EPS) + ADAM_WD * w)
    return delta, m2, v2


def _window_start(i, tm):
    assert tm % 16 == 0 and ROW0 % 16 == 0
    return pl.multiple_of(16 * jnp.maximum((tm // 16) * i - ROW0 // 16, 0), 16)


def _my_place():
    return lax.axis_index("x"), lax.axis_index("y"), lax.axis_index("c")


def _dev_index(px, py, pc):
    return 4 * px + 2 * py + pc


def _cast_shards(w_in_s, w_conv_s, w_rec_s, w_out_s):
    def body(a_ref, c_ref, r_ref, o_ref, oa_ref, ob_ref):
        oa_ref[...] = a_ref[...].astype(BF16)
        for k, ref in enumerate((c_ref, r_ref, o_ref)):
            ob_ref[k] = ref[0].astype(BF16)

    return pl.pallas_call(
        body, name="cast_shards",
        out_shape=(jax.ShapeDtypeStruct(w_in_s.shape, BF16), jax.ShapeDtypeStruct((3, W_ROW_BLK, D), BF16)),
    )(w_in_s, w_conv_s, w_rec_s, w_out_s)


def _peer(x, y, c, r):
    return (jnp.bitwise_xor(x, (r >> 2) & 1), jnp.bitwise_xor(y, (r >> 1) & 1), jnp.bitwise_xor(c, r & 1))


def _gather_small(small_s):
    def body(s_ref, o_ref, send_sems, recv_sems, local_sem):
        x, y, c = _my_place()
        my_id = _dev_index(x, y, c)
        mine = pltpu.make_async_copy(s_ref, o_ref.at[my_id], local_sem)
        mine.start()
        copies = []
        for r in range(1, N_DEV):
            cp = pltpu.make_async_remote_copy(
                src_ref=s_ref, dst_ref=o_ref.at[my_id], send_sem=send_sems.at[r - 1], recv_sem=recv_sems.at[r - 1],
                device_id=_peer(x, y, c, r), device_id_type=MESH_ID)
            cp.start()
            copies.append(cp)
        for cp in copies:
            cp.wait_recv()
        for cp in copies:
            cp.wait_send()
        mine.wait()

    return pl.pallas_call(
        body, name="gather_small", out_shape=jax.ShapeDtypeStruct((N_DEV,) + small_s.shape, F32),
        in_specs=[ANY], out_specs=ANY,
        scratch_shapes=[pltpu.SemaphoreType.DMA((7,)), pltpu.SemaphoreType.DMA((7,)), pltpu.SemaphoreType.DMA],
    )(small_s)


def _w3_gather(src, out, stage, send_sems, recv_sems, local_sems):
    x, y, c = _my_place()
    me, sibling = (x, y, c), (x, y, 1 - c)
    chips = [(1 - x, y), (x, 1 - y), (1 - x, 1 - y)]

    def block(place):
        d = _dev_index(*place)
        return out.at[:, pl.ds(pl.multiple_of(d * W_ROW_BLK, W_ROW_BLK), W_ROW_BLK), :]

    def copy(k, place, to, from_src=False):
        return pltpu.make_async_remote_copy(
            src_ref=src if from_src else block(place), dst_ref=block(place),
            send_sem=send_sems.at[k], recv_sem=recv_sems.at[k], device_id=to, device_id_type=MESH_ID)

    own_in = pltpu.make_async_copy(src, stage, local_sems.at[0])
    own_out = pltpu.make_async_copy(stage, block(me), local_sems.at[1])

    def start():
        copy(0, me, sibling, from_src=True).start()
        for j, chip in enumerate(chips):
            copy(1 + j, me, (*chip, c), from_src=True).start()
        own_in.start()
        own_in.wait()
        own_out.start()

    def finish():
        for j, chip in enumerate(chips):
            copy(1 + j, (*chip, c), me).wait_recv()
            copy(4 + j, (*chip, c), sibling).start()
        copy(0, sibling, me).wait_recv()
        for j, chip in enumerate(chips):
            copy(4 + j, (*chip, 1 - c), me).wait_recv()
        for k in range(7):
            copy(k, me, me).wait_send()
        own_out.wait()

    return start, finish


def _p3_to_sibling(p3_ref, land_ref, send_sems, recv_sems):
    x, y, c = _my_place()

    def cp(q):
        d = _dev_index(*_chip_rel(x, y, q), 1 - c)
        return pltpu.make_async_remote_copy(
            src_ref=p3_ref.at[:, pl.ds(pl.multiple_of(d * W_ROW_BLK, W_ROW_BLK), W_ROW_BLK), :],
            dst_ref=land_ref.at[q], send_sem=send_sems.at[q], recv_sem=recv_sems.at[q],
            device_id=(x, y, 1 - c), device_id_type=MESH_ID)

    def start():
        for q in range(4):
            cp(q).start()

    def finish():
        for q in range(4):
            cp(q).wait_recv()
        for q in range(4):
            cp(q).wait_send()

    return start, finish


def _partials_to_owners(src_ref, far_ref, send_sems, recv_sems):
    x, y, c = _my_place()

    def cp(q):
        return pltpu.make_async_remote_copy(
            src_ref=src_ref.at[q], dst_ref=far_ref.at[q - 1], send_sem=send_sems.at[q - 1],
            recv_sem=recv_sems.at[q - 1], device_id=(*_chip_rel(x, y, q), c), device_id_type=MESH_ID)

    def start():
        for q in range(1, 4):
            cp(q).start()

    def finish():
        for q in range(1, 4):
            cp(q).wait_recv()
        for q in range(1, 4):
            cp(q).wait_send()

    return start, finish


def _gather_chips(x, y, c):
    first = (jnp.bitwise_xor(x, 1 - c), jnp.bitwise_xor(y, c))
    second = (jnp.bitwise_xor(x, c), jnp.bitwise_xor(y, 1 - c))
    return [first, second, (1 - x, 1 - y)]


def _gather_and_proj(x_seq, meta_full, norm_g, w_in_b, order):
    tm = TM_MAT
    n_m = TP // tm
    last_m = n_m - 1

    def body(order_ref, x_ref, meta_ref, g_ref, s0, proj_ref, xin_ref, h_out, o0, hbuf, wbuf, send_sems, recv_sems,
             local_sems):
        del order_ref
        n = pl.program_id(0)
        m = pl.program_id(1)
        x, y, c = _my_place()
        me, sibling = (x, y, c), (x, y, 1 - c)
        chips = _gather_chips(x, y, c)

        def block(place):
            return o0.at[_dev_index(*place)]

        def copy(k, place, to, from_src=False):
            return pltpu.make_async_remote_copy(
                src_ref=s0 if from_src else block(place), dst_ref=block(place),
                send_sem=send_sems.at[k], recv_sem=recv_sems.at[k], device_id=to, device_id_type=MESH_ID)

        def to_vmem(place, slot):
            return pltpu.make_async_copy(block(place), wbuf.at[slot], local_sems.at[slot])

        own_out = pltpu.make_async_copy(wbuf.at[0], block(me), local_sems.at[2])
        h_copy = pltpu.make_async_copy(hbuf, h_out, local_sems.at[3])

        @pl.when((n == 0) & (m == 0))
        def _():
            copy(0, me, sibling, from_src=True).start()
            for j, chip in enumerate(chips[0:2]):
                copy(1 + j, me, (*chip, c), from_src=True).start()
            mine = pltpu.make_async_copy(s0, wbuf.at[0], local_sems.at[0])
            mine.start()
            mine.wait()
            own_out.start()

        @pl.when(n == 0)
        def _():
            xv = x_ref[...]
            xin_ref[...] = jnp.where(m == 0, pltpu.roll(xv, ROW0, 0), xv)

            @pl.when(m == 0)
            def _():
                xin_ref[0:PAD_FRONT, :] = jnp.zeros((PAD_FRONT, D), F32)
                xin_ref[PAD_FRONT:ROW0, :] = meta_ref[...]

            xv = xin_ref[...]
            r = lax.rsqrt(jnp.mean(xv * xv, axis=-1, keepdims=True) + EPS)
            hbuf[m] = (xv * r * g_ref[...]).astype(BF16)

        between = [4 + c, 5 - c, 6]
        first, second, diag = chips
        plan = [(sibling, (0, sibling), None),
                ((*first, c), (1, (*first, c)), between[0]),
                ((*second, 1 - c), (between[1], (*second, 1 - c)), None),
                ((*second, c), (2, (*second, c)), between[1]),
                ((*first, 1 - c), (between[0], (*first, 1 - c)), None),
                ((*diag, c), (3, (*diag, c)), between[2]),
                ((*diag, 1 - c), (between[2], (*diag, 1 - c)), None)]

        for s, (place, (k, origin), pass_on) in enumerate(plan, start=1):
            @pl.when((n == s - 1) & (m == last_m))
            def _(s=s, place=place, k=k, origin=origin, pass_on=pass_on):
                copy(k, origin, me).wait_recv()
                if pass_on is not None:
                    copy(pass_on, place, sibling).start()
                if s == 2:
                    copy(3, place, (*chips[1], c)).start()
                    own_out.wait()
                to_vmem(place, s % 2).start()

            @pl.when((n == s) & (m == 0))
            def _(s=s, place=place):
                to_vmem(place, s % 2).wait()

        proj_ref[...] = _dot(hbuf[m], wbuf[lax.rem(n, 2)]).astype(BF16)

        @pl.when((n == 0) & (m == last_m))
        def _():
            h_copy.start()

        @pl.when((n == N_DEV - 1) & (m == last_m))
        def _():
            for k in range(7):
                copy(k, me, me).wait_send()
            h_copy.wait()

    return pl.pallas_call(
        body, name="gather_and_proj",
        grid_spec=pltpu.PrefetchScalarGridSpec(
            num_scalar_prefetch=1, grid=(N_DEV, n_m),
            in_specs=[pl.BlockSpec((pl.Element(tm), pl.Element(D)),
                                   lambda n, m, o: (_window_start(jnp.where(n == 0, m, 0), tm), 0)),
                      pl.BlockSpec((N_META, D), lambda n, m, o: (0, 0)),
                      pl.BlockSpec((1, D), lambda n, m, o: (0, 0)), ANY],
            out_specs=(pl.BlockSpec((tm, W_IN_BLK), lambda n, m, o: (m, o[n])),
                       pl.BlockSpec((tm, D), lambda n, m, o: (jnp.where(n == 0, m, last_m), 0)), ANY, ANY),
            scratch_shapes=[pltpu.VMEM((n_m, tm, D), BF16), pltpu.VMEM((2, D, W_IN_BLK), BF16),
                            pltpu.SemaphoreType.DMA((7,)), pltpu.SemaphoreType.DMA((7,)),
                            pltpu.SemaphoreType.DMA((4,))]),
        out_shape=(jax.ShapeDtypeStruct((TP, D_IN), BF16), jax.ShapeDtypeStruct((TP, D), F32),
                   jax.ShapeDtypeStruct((n_m, tm, D), BF16), jax.ShapeDtypeStruct((N_DEV, D, W_IN_BLK), BF16)),
        compiler_params=pltpu.CompilerParams(dimension_semantics=("arbitrary", "arbitrary")),
    )(order, x_seq, meta_full, norm_g, w_in_b)


def _chip_rel(x, y, r):
    return (jnp.bitwise_xor(x, r >> 1), jnp.bitwise_xor(y, r & 1))


def _exchange_small(pack, srs):
    def body(pk, sr, pk_all, sr_all, send_sems, recv_sems, local_sems):
        x, y, c = _my_place()
        my_id = _dev_index(x, y, c)
        mine = [pltpu.make_async_copy(pk, pk_all.at[my_id], local_sems.at[0]),
                pltpu.make_async_copy(sr.at[my_id], sr_all.at[my_id], local_sems.at[1])]
        for cp in mine:
            cp.start()
        copies = []
        for r in range(1, N_DEV):
            peer = (jnp.bitwise_xor(x, (r >> 2) & 1), jnp.bitwise_xor(y, (r >> 1) & 1), jnp.bitwise_xor(c, r & 1))
            peer_id = _dev_index(*peer)
            for a, (src, dst) in enumerate(((pk, pk_all.at[my_id]), (sr.at[peer_id], sr_all.at[my_id]))):
                cp = pltpu.make_async_remote_copy(
                    src_ref=src, dst_ref=dst, send_sem=send_sems.at[a * 7 + r - 1], recv_sem=recv_sems.at[a * 7 + r - 1],
                    device_id=peer, device_id_type=MESH_ID)
                cp.start()
                copies.append(cp)
        for cp in copies:
            cp.wait_recv()
        for cp in copies:
            cp.wait_send()
        for cp in mine:
            cp.wait()

    return pl.pallas_call(
        body, name="exchange_small",
        out_shape=(jax.ShapeDtypeStruct((N_DEV,) + pack.shape, F32), jax.ShapeDtypeStruct(srs.shape, F32)),
        in_specs=[ANY, ANY], out_specs=(ANY, ANY),
        scratch_shapes=[pltpu.SemaphoreType.DMA((14,)), pltpu.SemaphoreType.DMA((14,)), pltpu.SemaphoreType.DMA((2,))],
    )(pack, srs)


N_CB = D // HEAD_W


def _store_by_cb(ref, idx, rows, val):
    for cb in range(N_CB):
        ref[(*idx, cb, rows, slice(None))] = val[:, cb * HEAD_W:(cb + 1) * HEAD_W]


def _fill_shifts(sh, tm):
    n = tm + HALO - 8
    for s in range(1, 8):
        for cb in range(N_CB):
            sh[s, cb, 0:n, :] = sh[0, cb, s:s + n, :]


def _gates(p_ref, lbl_ref, chunk, bsc):
    lb = _sigmoid(lbl_ref[0:1, :] - lbl_ref[1:2, :])
    q_raw = p_ref[:, 0:D].astype(F32)
    f_raw = p_ref[:, D:2 * D].astype(F32)
    sq = _sigmoid(q_raw)
    q = q_raw * sq
    sg = _sigmoid(f_raw)
    f = lb + (1.0 - lb) * sg
    row = lax.broadcasted_iota(jnp.int32, (CHUNK, 1), 0) + chunk * CHUNK
    valid = row >= PAD_FRONT
    lf = jnp.where(valid, jnp.log(f), 0.0)
    k = jnp.where(valid, 1.0 - f, 0.0)
    r_i = lax.broadcasted_iota(jnp.int32, (CHUNK, CHUNK), 0)
    c_i = lax.broadcasted_iota(jnp.int32, (CHUNK, CHUNK), 1)
    causal = r_i >= c_i
    bsc[...] = _tri_matmul(causal.astype(BF16), lf)
    b = bsc[...]
    b_mid = bsc[CHUNK // 2 - 1:CHUNK // 2, :]
    b_last = bsc[CHUNK - 1:CHUNK, :]
    e_q = jnp.exp(b)
    e_qm = jnp.exp(b - b_mid)
    e_km = jnp.exp(b_mid - b)
    e_kh = jnp.exp(b_last - b)
    e_last = jnp.exp(b_last)
    return dict(lb=lb, q_raw=q_raw, sq=sq, q=q, sg=sg, f=f, k=k, valid=valid, causal=causal,
                e_q=e_q, e_qm=e_qm, e_km=e_km, e_kh=e_kh, e_last=e_last)


def _rec_conv_fwd(proj, lb_logits, conv_w, conv_b, w3_b):
    cps = CHUNKS_PER_STEP
    tm = cps * CHUNK
    n_strip = CONV_STRIPS
    strip = tm // n_strip

    def body(p_ref, lbl_ref, pg_ref, w_ref, b_ref, w3s_ref, c0_ref, o_ref, s_ref, w3o_ref,
             st, bsc, sh, c0_sc, w3buf, send_sems, recv_sems, local_sems):
        n = pl.program_id(0)
        gather_start, gather_finish = _w3_gather(w3s_ref, w3o_ref, w3buf, send_sems, recv_sems, local_sems)

        @pl.when(n == 0)
        def _():
            st[...] = jnp.zeros_like(st)
            sh[0, :, 0:HALO, :] = jnp.zeros((N_CB, HALO, HEAD_W), F32)
            gather_start()

        @pl.when(n > 0)
        def _():
            sh[0, :, 0:HALO, :] = sh[0, :, tm:tm + HALO, :]

        ga = pg_ref[:, 0:D].astype(F32)
        gb = pg_ref[:, D:2 * D].astype(F32)
        _store_by_cb(sh, (0,), slice(HALO, HALO + tm), ga * _sigmoid(gb))
        _fill_shifts(sh, tm)

        def conv_unit(cb, s_i):
            cs = slice(cb * HEAD_W, (cb + 1) * HEAD_W)
            acc = jnp.broadcast_to(b_ref[:, cs], (strip, HEAD_W))
            for j in range(CONV_K):
                off = HALO - (CONV_K - 1) + j
                lo = s_i * strip + 8 * (off // 8)
                acc = acc + w_ref[j:j + 1, cs] * sh[off % 8, cb, lo:lo + strip, :]
            c0_sc[s_i * strip:(s_i + 1) * strip, cs] = acc

        units = [(cb, s_i) for cb in range(N_CB) for s_i in range(n_strip)]

        def prep(ci):
            g = _gates(p_ref.at[pl.ds(ci * CHUNK, CHUNK)], lbl_ref, n * cps + ci, bsc.at[ci])
            g["q1"] = (g["q"] * g["e_q"]).astype(BF16)
            g["qm"] = (g["q"] * g["e_qm"]).astype(BF16)
            g["km"] = (g["k"] * g["e_km"]).astype(BF16)
            g["kh"] = (g["k"] * g["e_kh"]).astype(BF16)
            return g

        def heads(ci, g):
            rs = pl.ds(ci * CHUNK, CHUNK)
            pv = p_ref.at[rs]
            s_ref[ci] = st[...]
            for h in range(HEADS):
                if units:
                    conv_unit(*units.pop(0))
                sl = slice(h * HEAD_W, (h + 1) * HEAD_W)
                v = pv[:, 2 * D + h * HEAD_W:2 * D + (h + 1) * HEAD_W]
                att = jnp.where(g["causal"], _dot_nt(g["qm"][:, sl], g["km"][:, sl]), 0.0).astype(BF16)
                s_h = st[h]
                o_ref[rs, sl] = (_dot_nt(g["q1"][:, sl], s_h.astype(BF16)) + _dot(att, v)).astype(ACT)
                st[h] = s_h * g["e_last"][:, sl] + _dot_tn(v, g["kh"][:, sl])

        ready = prep(0)
        for ci in range(cps):
            coming = prep(ci + 1) if ci + 1 < cps else None
            heads(ci, ready)
            ready = coming
        while units:
            conv_unit(*units.pop(0))
        c0_ref[...] = c0_sc[...].astype(ACT)

        @pl.when(n == N_CHUNK // cps - 1)
        def _():
            gather_finish()

    def rows_of(width, col):
        return pl.BlockSpec((tm, width), lambda n: (n, col))

    return pl.pallas_call(
        body, name="rec_conv_fwd", grid=(N_CHUNK // cps,),
        in_specs=[rows_of(3 * D, 1), pl.BlockSpec((2, D), lambda n: (0, 0)), rows_of(2 * D, 0),
                  pl.BlockSpec((CONV_K, D), lambda n: (0, 0)), pl.BlockSpec((1, D), lambda n: (0, 0)), ANY],
        out_specs=(rows_of(D, 0), rows_of(D, 0), pl.BlockSpec((cps, HEADS, HEAD_W, HEAD_W), lambda n: (n, 0, 0, 0)), ANY),
        out_shape=(jax.ShapeDtypeStruct((TP, D), ACT), jax.ShapeDtypeStruct((TP, D), ACT),
                   jax.ShapeDtypeStruct((N_CHUNK, HEADS, HEAD_W, HEAD_W), F32), jax.ShapeDtypeStruct((3, D, D), BF16)),
        scratch_shapes=[pltpu.VMEM((HEADS, HEAD_W, HEAD_W), F32), pltpu.VMEM((cps, CHUNK, D), F32),
                        pltpu.VMEM((8, N_CB, HALO + tm, HEAD_W), F32), pltpu.VMEM((tm, D), F32),
                        pltpu.VMEM((3, W_ROW_BLK, D), BF16), pltpu.SemaphoreType.DMA((7,)),
                        pltpu.SemaphoreType.DMA((7,)), pltpu.SemaphoreType.DMA((2,))],
        compiler_params=pltpu.CompilerParams(dimension_semantics=("arbitrary",)),
    )(proj, lb_logits, proj, conv_w, conv_b, w3_b)


def _mid(xin, tgt, o, c0, proj, w3, ln_g, ln_b, gnorm_g, final_g):
    tm = TM_ELT

    def body(x_ref, t_ref, o_ref, c0_ref, z_ref, gr_ref, mc_ref, mr_ref, w_ref, lng_ref, lnb_ref, gng_ref, fg_ref,
             do_ref, dc0_ref, dz_ref, dp_ref, a3_ref, b3_ref, red_ref, on_sc, don_sc):
        i = pl.program_id(0)

        @pl.when(i == 0)
        def _():
            red_ref[...] = jnp.zeros_like(red_ref)

        w_conv, w_rec, w_out = w_ref[0], w_ref[1], w_ref[2]
        c0v = c0_ref[...].astype(F32)
        mu = jnp.mean(c0v, axis=-1, keepdims=True)
        xc = c0v - mu
        rstd = lax.rsqrt(jnp.mean(xc * xc, axis=-1, keepdims=True) + EPS)
        xh = xc * rstd
        c1 = xh * lng_ref[...] + lnb_ref[...]
        s1 = _sigmoid(c1)
        c2 = c1 * s1
        z = z_ref[...].astype(F32)
        sz = _sigmoid(z)
        silu_z = z * sz
        u_conv = (c2 * silu_z).astype(BF16)
        y_conv = _dot(u_conv, w_conv)
        ov = o_ref[...].astype(F32)
        r3 = []
        for h in range(HEADS):
            sl = slice(h * HEAD_W, (h + 1) * HEAD_W)
            oh = ov[:, sl]
            r_h = lax.rsqrt(jnp.mean(oh * oh, axis=-1, keepdims=True) + EPS)
            r3.append(r_h)
            on_sc[:, sl] = oh * r_h
        o_n = on_sc[...]
        o_g = o_n * gng_ref[...]
        gr = gr_ref[...].astype(F32)
        sgr = _sigmoid(gr)
        silu_g = gr * sgr
        u_rec = (o_g * silu_g).astype(BF16)
        y_rec = _dot(u_rec, w_rec)
        mc = mc_ref[...].astype(F32)
        mr = mr_ref[...].astype(F32)
        smc = _sigmoid(mc)
        smr = _sigmoid(mr)
        merged = (smc * y_conv + smr * y_rec).astype(BF16)
        res = x_ref[...] + _dot(merged, w_out)
        r2 = lax.rsqrt(jnp.mean(res * res, axis=-1, keepdims=True) + EPS)
        xh2 = res * r2
        row = lax.broadcasted_iota(jnp.int32, (tm, 1), 0) + i * tm
        real = row >= ROW0
        tgt = t_ref[...]
        tgt = jnp.where(i == 0, pltpu.roll(tgt, ROW0, 0), tgt)
        diff = jnp.where(real, xh2 * fg_ref[...] - tgt, 0.0)
        d_y = diff * (1.0 / D)
        d_xh2 = d_y * fg_ref[...]
        d_res = r2 * (d_xh2 - xh2 * jnp.mean(d_xh2 * xh2, axis=-1, keepdims=True))
        d_res_b = d_res.astype(BF16)
        d_merged = _dot_nt(d_res_b, w_out)
        d_yc = (d_merged * smc).astype(BF16)
        d_yr = (d_merged * smr).astype(BF16)
        dp_ref[:, D:2 * D] = (d_merged * y_conv * smc * (1.0 - smc)).astype(BF16)
        dp_ref[:, 2 * D:3 * D] = (d_merged * y_rec * smr * (1.0 - smr)).astype(BF16)
        d_ur = _dot_nt(d_yr, w_rec)
        d_og = d_ur * silu_g
        dp_ref[:, 0:D] = (d_ur * o_g * _dsilu(gr, sgr)).astype(BF16)
        d_on = d_og * gng_ref[...]
        for h in range(HEADS):
            sl = slice(h * HEAD_W, (h + 1) * HEAD_W)
            d_h = d_on[:, sl]
            n_h = o_n[:, sl]
            don_sc[:, sl] = r3[h] * (d_h - n_h * jnp.mean(d_h * n_h, axis=-1, keepdims=True))
        do_ref[...] = don_sc[...].astype(ACT)
        d_uc = _dot_nt(d_yc, w_conv)
        d_c2 = d_uc * silu_z
        dz_ref[...] = (d_uc * c2 * _dsilu(z, sz)).astype(BF16)
        d_c1 = d_c2 * _dsilu(c1, s1)
        d_xh = d_c1 * lng_ref[...]
        d_c0 = rstd * (d_xh - jnp.mean(d_xh, axis=-1, keepdims=True)
                       - xh * jnp.mean(d_xh * xh, axis=-1, keepdims=True))
        dc0_ref[...] = d_c0.astype(ACT)
        a3_ref[0] = u_conv
        b3_ref[0] = d_yc
        a3_ref[1] = u_rec
        b3_ref[1] = d_yr
        a3_ref[2] = merged
        b3_ref[2] = d_res_b
        def colsum(vv):
            return jnp.sum(vv, axis=0, keepdims=True)

        red_ref[0:1, :] += colsum(d_y * xh2)
        red_ref[1:2, :] += colsum(d_og * o_n)
        red_ref[2:3, :] += colsum(d_c1 * xh)
        red_ref[3:4, :] += colsum(d_c1)
        red_ref[4:5, :] += colsum(d_c0)
        red_ref[5:6, :] += colsum(diff * diff) * (0.5 / D)

    def row_block(width, col):
        return pl.BlockSpec((tm, width), lambda i: (i, col))

    def const_block(shape):
        return pl.BlockSpec(shape, lambda i: (0,) * len(shape))

    stack = jax.ShapeDtypeStruct((3, TP, D), BF16)
    stack_spec = pl.BlockSpec((3, tm, D), lambda i: (0, i, 0))
    return pl.pallas_call(
        body, name="mid", grid=(TP // tm,),
        in_specs=[row_block(D, 0),
                  pl.BlockSpec((pl.Element(tm), pl.Element(D)), lambda i: (_window_start(i, tm), 0)),
                  row_block(D, 0), row_block(D, 0),
                  row_block(D, 2), row_block(D, 6), row_block(D, 7), row_block(D, 8),
                  pl.BlockSpec((3, D, D), lambda i: (0, 0, 0), pipeline_mode=pl.Buffered(1)),
                  const_block((1, D)), const_block((1, D)), const_block((1, D)), const_block((1, D))],
        out_specs=(row_block(D, 0), row_block(D, 0), row_block(D, 0), row_block(3 * D, 2),
                   stack_spec, stack_spec, const_block((8, D))),
        out_shape=(jax.ShapeDtypeStruct((TP, D), ACT), jax.ShapeDtypeStruct((TP, D), ACT),
                   jax.ShapeDtypeStruct((TP, D), BF16),
                   jax.ShapeDtypeStruct((TP, D_IN), BF16), stack, stack, jax.ShapeDtypeStruct((8, D), F32)),
        scratch_shapes=[pltpu.VMEM((tm, D), F32), pltpu.VMEM((tm, D), F32)],
        compiler_params=pltpu.CompilerParams(dimension_semantics=("arbitrary",), vmem_limit_bytes=60 * 1024 * 1024),
    )(xin, tgt, o, c0, proj, proj, proj, proj, w3, ln_g, ln_b, gnorm_g, final_g)


def _rec_conv_bwd(proj, lb_logits, d_o, s_start, d_c0, d_z, conv_w, dproj, p3):
    cps = CHUNKS_PER_STEP
    tm = cps * CHUNK
    last = N_CHUNK // cps - 1
    n_strip = CONV_STRIPS
    strip = tm // n_strip

    def body(p_ref, lbl_ref, do_ref, s_ref, pg_ref, dc_ref, dz_ref, w_ref, dproj_in, p3_ref,
             dp_ref, dlb_ref, dw_ref, land_ref,
             dst, bsc, dq_sc, dk_sc, g_sc, dsh, a_sc, da_sc, acc, send_sems, recv_sems):
        del dproj_in
        n = pl.program_id(0)
        ride_start, ride_finish = _p3_to_sibling(p3_ref, land_ref, send_sems, recv_sems)

        @pl.when(n == 0)
        def _():
            ride_start()
            dst[...] = jnp.zeros_like(dst)
            dlb_ref[...] = jnp.zeros_like(dlb_ref)
            dsh[0, :, tm:tm + HALO, :] = jnp.zeros((N_CB, HALO, HEAD_W), F32)
            acc[...] = jnp.zeros_like(acc)

        @pl.when(n > 0)
        def _():
            dsh[0, :, tm:tm + HALO, :] = dsh[0, :, 0:HALO, :]

        _store_by_cb(dsh, (0,), slice(0, tm), dc_ref[...].astype(F32))
        _fill_shifts(dsh, tm)
        ga = pg_ref[:, 0:D].astype(F32)
        sb = _sigmoid(pg_ref[:, D:2 * D].astype(F32))
        a = ga * sb
        _store_by_cb(a_sc, (), slice(0, tm), a)

        def conv_unit(cb, st):
            cs = slice(cb * HEAD_W, (cb + 1) * HEAD_W)
            rows = slice(st * strip, (st + 1) * strip)
            a_s = a_sc[cb, rows, :]
            d_a = jnp.zeros((strip, HEAD_W), F32)
            for j in range(CONV_K):
                off = CONV_K - 1 - j
                lo = st * strip + 8 * (off // 8)
                slab = dsh[off % 8, cb, lo:lo + strip, :]
                d_a = d_a + w_ref[j:j + 1, cs] * slab
                acc[j, :, cs] += jnp.sum((a_s * slab).reshape(strip // 8, 8, HEAD_W), axis=0)
            da_sc[rows, cs] = d_a

        units = [(cb, st) for cb in range(N_CB) for st in range(n_strip)]

        def prep(ci):
            g = _gates(p_ref.at[pl.ds(ci * CHUNK, CHUNK)], lbl_ref, (last - n) * cps + ci, bsc.at[ci])
            g["q1"] = (g["q"] * g["e_q"]).astype(BF16)
            qm_f = g["q"] * g["e_qm"]
            km_f = g["k"] * g["e_km"]
            g["qm"] = qm_f.astype(BF16)
            g["km"] = km_f.astype(BF16)
            g["qm_lo"] = (qm_f - g["qm"].astype(F32)).astype(BF16)
            g["km_lo"] = (km_f - g["km"].astype(F32)).astype(BF16)
            g["kh_f"] = g["k"] * g["e_kh"]
            g["kh"] = g["kh_f"].astype(BF16)
            return g

        def heads_and_post(ci, g):
            rs = pl.ds(ci * CHUNK, CHUNK)
            pv = p_ref.at[rs]
            dpv = dp_ref.at[rs]
            q1, qm, km, qm_lo, km_lo, kh_f, kh = (g[k] for k in ("q1", "qm", "km", "qm_lo", "km_lo", "kh_f", "kh"))
            for h in range(HEADS):
                if units:
                    conv_unit(*units.pop(0))
                sl = slice(h * HEAD_W, (h + 1) * HEAD_W)
                v = pv[:, 2 * D + h * HEAD_W:2 * D + (h + 1) * HEAD_W]
                d_oh = do_ref[rs, sl].astype(BF16)
                s0 = s_ref[ci, h]
                ds_end = dst[h]
                ds_end_b = ds_end.astype(BF16)
                att = jnp.where(g["causal"], _dot_nt(qm[:, sl], km[:, sl]), 0.0).astype(BF16)
                d_att = jnp.where(g["causal"], _dot_nt(d_oh, v), 0.0).astype(BF16)
                d_v = _dot_tn(att, d_oh) + _dot_nt(kh[:, sl], ds_end_b)
                d_qm2 = _dot(d_att, jnp.concatenate([km[:, sl], km_lo[:, sl]], axis=1))
                d_qm = d_qm2[:, 0:HEAD_W] + d_qm2[:, HEAD_W:2 * HEAD_W]
                d_q1 = _dot(d_oh, s0.astype(BF16))
                d_km2 = _dot_tn(d_att, jnp.concatenate([qm[:, sl], qm_lo[:, sl]], axis=1))
                d_km = d_km2[:, 0:HEAD_W] + d_km2[:, HEAD_W:2 * HEAD_W]
                d_kh = _dot(v, ds_end_b)
                dq_sc[ci, :, sl] = d_qm * g["e_qm"][:, sl] + d_q1 * g["e_q"][:, sl]
                dk_sc[ci, :, sl] = d_km * g["e_km"][:, sl] + d_kh * g["e_kh"][:, sl]
                g_sc[ci, :, sl] = (jnp.sum(kh_f[:, sl] * d_kh, axis=0, keepdims=True)
                                   + g["e_last"][:, sl] * jnp.sum(ds_end * s0, axis=0, keepdims=True))
                dst[h] = ds_end * g["e_last"][:, sl] + _dot_tn(d_oh, q1[:, sl])
                dpv[:, 5 * D + h * HEAD_W:5 * D + (h + 1) * HEAD_W] = d_v.astype(BF16)
            d_q = dq_sc[ci]
            d_k = dk_sc[ci]
            d_b = g["q"] * d_q - g["k"] * d_k
            anti = jnp.logical_not(g["causal"]) | (lax.broadcasted_iota(jnp.int32, (CHUNK, CHUNK), 0)
                                                    == lax.broadcasted_iota(jnp.int32, (CHUNK, CHUNK), 1))
            d_lf = _tri_matmul(anti.astype(BF16), d_b) + g_sc[ci]
            d_f = jnp.where(g["valid"], d_lf / g["f"] - d_k, 0.0)
            sg = g["sg"]
            dlb_ref[0:1, :] += jnp.sum(d_f * (1.0 - sg), axis=0, keepdims=True)
            dpv[:, 3 * D:4 * D] = (d_q * _dsilu(g["q_raw"], g["sq"])).astype(BF16)
            dpv[:, 4 * D:5 * D] = (d_f * (1.0 - g["lb"]) * sg * (1.0 - sg)).astype(BF16)

        ready = prep(cps - 1)
        for ci in reversed(range(cps)):
            coming = prep(ci - 1) if ci > 0 else None
            heads_and_post(ci, ready)
            ready = coming
        while units:
            conv_unit(*units.pop(0))

        d_a = da_sc[...]
        dp_ref[:, 0:D] = (d_a * sb).astype(BF16)
        dp_ref[:, D:2 * D] = (d_a * a * (1.0 - sb)).astype(BF16)
        dp_ref[:, 2 * D:3 * D] = dz_ref[...]

        @pl.when(n == last)
        def _():
            for j in range(CONV_K):
                dw_ref[j:j + 1, :] = jnp.sum(acc[j], axis=0, keepdims=True)
            dw_ref[CONV_K:CONV_K + 1, :] = jnp.zeros((1, D), F32)
            ride_finish()

    def rows_of(width, col):
        return pl.BlockSpec((tm, width), lambda n: (last - n, col))

    return pl.pallas_call(
        body, name="rec_conv_bwd", grid=(N_CHUNK // cps,),
        in_specs=[rows_of(3 * D, 1), pl.BlockSpec((2, D), lambda n: (0, 0)), rows_of(D, 0),
                  pl.BlockSpec((cps, HEADS, HEAD_W, HEAD_W), lambda n: (last - n, 0, 0, 0)),
                  rows_of(2 * D, 0), rows_of(D, 0), rows_of(D, 0), pl.BlockSpec((CONV_K, D), lambda n: (0, 0)), ANY, ANY],
        out_specs=(rows_of(6 * D, 0), pl.BlockSpec((8, D), lambda n: (0, 0)),
                   pl.BlockSpec((CONV_K + 1, D), lambda n: (0, 0)), ANY),
        out_shape=(jax.ShapeDtypeStruct((TP, D_IN), BF16), jax.ShapeDtypeStruct((8, D), F32),
                   jax.ShapeDtypeStruct((CONV_K + 1, D), F32), jax.ShapeDtypeStruct((4, 3, W_ROW_BLK, D), F32)),
        scratch_shapes=[pltpu.VMEM((HEADS, HEAD_W, HEAD_W), F32), pltpu.VMEM((cps, CHUNK, D), F32),
                        pltpu.VMEM((cps, CHUNK, D), F32), pltpu.VMEM((cps, CHUNK, D), F32),
                        pltpu.VMEM((cps, 1, D), F32),
                        pltpu.VMEM((8, N_CB, tm + HALO, HEAD_W), F32), pltpu.VMEM((N_CB, tm, HEAD_W), F32),
                        pltpu.VMEM((tm, D), F32), pltpu.VMEM((CONV_K, 8, D), F32),
                        pltpu.SemaphoreType.DMA((4,)), pltpu.SemaphoreType.DMA((4,))],
        input_output_aliases={8: 0},
        compiler_params=pltpu.CompilerParams(dimension_semantics=("arbitrary",)),
    )(proj, lb_logits, d_o, s_start, proj, d_c0, d_z, conv_w, dproj, p3)


def _wgrad3(a3, b3):
    tt = TT_WGRAD

    def body(a_ref, b_ref, o_ref):
        @pl.when(pl.program_id(1) == 0)
        def _():
            o_ref[...] = jnp.zeros_like(o_ref)

        o_ref[0] += _dot_tn(a_ref[0], b_ref[0])

    return pl.pallas_call(
        body, name="wgrad3", grid=(3, TP // tt),
        in_specs=[pl.BlockSpec((1, tt, D), lambda g, t: (g, t, 0)), pl.BlockSpec((1, tt, D), lambda g, t: (g, t, 0))],
        out_specs=pl.BlockSpec((1, D, D), lambda g, t: (g, 0, 0)),
        out_shape=jax.ShapeDtypeStruct((3, D, D), F32),
        compiler_params=pltpu.CompilerParams(dimension_semantics=("arbitrary", "arbitrary")),
    )(a3, b3)


def _wgrad_in(h, dproj, ids, chip1b):
    tt = TT_WGRAD
    n_t = TP // tt

    def body(ids_ref, a_ref, b_ref, c1_ref, o_ref, ob_ref, l0_ref, far_ref, acc, tmp, send_sems, recv_sems, tmp_sem,
             far_send_sems, far_recv_sems):
        del ids_ref
        r = pl.program_id(0)
        t = pl.program_id(1)
        x, y, c = _my_place()
        sibling = (x, y, 1 - c)
        slot = lax.rem(r, 2)
        ride_start, ride_finish = _partials_to_owners(c1_ref, far_ref, far_send_sems, far_recv_sems)

        @pl.when((r == 0) & (t == 0))
        def _():
            ride_start()

        def send_in(q):
            return pltpu.make_async_remote_copy(
                src_ref=acc.at[q % 2], dst_ref=l0_ref.at[q], send_sem=send_sems.at[q], recv_sem=recv_sems.at[q],
                device_id=sibling, device_id_type=MESH_ID)

        def landed(q):
            return pltpu.make_async_copy(l0_ref.at[q], tmp, tmp_sem)

        @pl.when(t == 0)
        def _():
            acc[slot] = jnp.zeros((D, W_IN_BLK), F32)

        acc[slot] += _dot_tn(a_ref[...], b_ref[...])

        for q in range(4):
            @pl.when((r == q) & (t == n_t - 1))
            def _(q=q):
                if q >= 1:
                    send_in(q - 1).wait_send()
                send_in(q).start()

            @pl.when((r == 4 + q) & (t == n_t - 2))
            def _(q=q):
                if q == 0:
                    send_in(3).wait_send()
                send_in(q).wait_recv()
                landed(q).start()

            @pl.when((r == 4 + q) & (t == n_t - 1))
            def _(q=q):
                landed(q).wait()
                tot = acc[q % 2] + tmp[...]
                o_ref[0] = tot
                ob_ref[0] = tot.astype(BF16)

        @pl.when((r == N_DEV - 1) & (t == n_t - 1))
        def _():
            ride_finish()

    blk = pl.BlockSpec((1, D, W_IN_BLK), lambda r, t, ids: (jnp.maximum(r - 4, 0), 0, 0))
    return pl.pallas_call(
        body, name="wgrad_in",
        grid_spec=pltpu.PrefetchScalarGridSpec(
            num_scalar_prefetch=1, grid=(N_DEV, n_t),
            in_specs=[pl.BlockSpec((tt, D), lambda r, t, ids: (t, 0)),
                      pl.BlockSpec((tt, W_IN_BLK), lambda r, t, ids: (t, ids[r])), ANY],
            out_specs=(blk, blk, ANY, ANY),
            scratch_shapes=[pltpu.VMEM((2, D, W_IN_BLK), F32), pltpu.VMEM((D, W_IN_BLK), F32),
                            pltpu.SemaphoreType.DMA((4,)), pltpu.SemaphoreType.DMA((4,)), pltpu.SemaphoreType.DMA,
                            pltpu.SemaphoreType.DMA((3,)), pltpu.SemaphoreType.DMA((3,))]),
        out_shape=(jax.ShapeDtypeStruct((4, D, W_IN_BLK), F32), jax.ShapeDtypeStruct((4, D, W_IN_BLK), BF16),
                   jax.ShapeDtypeStruct((4, D, W_IN_BLK), F32), jax.ShapeDtypeStruct((3, 3, W_ROW_BLK, D), BF16)),
        compiler_params=pltpu.CompilerParams(dimension_semantics=("arbitrary", "arbitrary")),
    )(ids, h, dproj, chip1b)


def _chip_sum_3(p3, land1, ids_mine):
    def body(ids_ref, p_ref, l_ref, o_ref, ob_ref):
        del ids_ref
        tot = p_ref[...] + l_ref[0]
        o_ref[0] = tot
        ob_ref[0] = tot.astype(BF16)

    blk = pl.BlockSpec((1, 3, W_ROW_BLK, D), lambda r, ids: (r, 0, 0, 0))
    return pl.pallas_call(
        body, name="chip_sum_3",
        grid_spec=pltpu.PrefetchScalarGridSpec(
            num_scalar_prefetch=1, grid=(4,),
            in_specs=[pl.BlockSpec((3, W_ROW_BLK, D), lambda r, ids: (0, ids[r], 0)), blk],
            out_specs=(blk, blk)),
        out_shape=(jax.ShapeDtypeStruct((4, 3, W_ROW_BLK, D), F32), jax.ShapeDtypeStruct((4, 3, W_ROW_BLK, D), BF16)),
    )(ids_mine, p3, land1)


def _dh_and_norm_bwd(dproj, w_in_full, xin, b3, norm_g, chip0b):
    tm = TM_MAT
    n_k = N_DEV // DH_K_BLKS
    n_m = TP // tm

    def body(dp_ref, w_ref, x_ref, dr_ref, g_ref, c0_ref, dx_ref, dg_ref, f0_ref, acc, send_sems, recv_sems):
        m = pl.program_id(0)
        k = pl.program_id(1)
        ride_start, ride_finish = _partials_to_owners(c0_ref, f0_ref, send_sems, recv_sems)

        @pl.when((m == 0) & (k == 0))
        def _():
            ride_start()

        @pl.when(k == 0)
        def _():
            acc[...] = jnp.zeros_like(acc)

        part = _dot_nt(dp_ref[:, 0:W_IN_BLK], w_ref[0])
        for j in range(1, DH_K_BLKS):
            part = part + _dot_nt(dp_ref[:, j * W_IN_BLK:(j + 1) * W_IN_BLK], w_ref[j])
        acc[...] += part

        @pl.when((k == n_k - 1) & (m == 0))
        def _():
            dg_ref[...] = jnp.zeros_like(dg_ref)

        @pl.when(k == n_k - 1)
        def _():
            xv = x_ref[...]
            r1 = lax.rsqrt(jnp.mean(xv * xv, axis=-1, keepdims=True) + EPS)
            xh = xv * r1
            d_h = acc[...]
            dg_ref[0:1, :] += jnp.sum(d_h * xh, axis=0, keepdims=True)
            d_xh = d_h * g_ref[...]
            dx_ref[...] = dr_ref[0].astype(F32) + r1 * (d_xh - xh * jnp.mean(d_xh * xh, axis=-1, keepdims=True))

        @pl.when((m == n_m - 1) & (k == n_k - 1))
        def _():
            ride_finish()

    return pl.pallas_call(
        body, name="dh_norm_bwd", grid=(n_m, n_k),
        in_specs=[pl.BlockSpec((tm, DH_K_BLKS * W_IN_BLK), lambda m, k: (m, k)),
                  pl.BlockSpec((DH_K_BLKS, D, W_IN_BLK), lambda m, k: (k, 0, 0)),
                  pl.BlockSpec((tm, D), lambda m, k: (m, 0)), pl.BlockSpec((1, tm, D), lambda m, k: (2, m, 0)),
                  pl.BlockSpec((1, D), lambda m, k: (0, 0)), ANY],
        out_specs=(pl.BlockSpec((tm, D), lambda m, k: (m, 0)), pl.BlockSpec((8, D), lambda m, k: (0, 0)), ANY),
        out_shape=(jax.ShapeDtypeStruct((TP, D), F32), jax.ShapeDtypeStruct((8, D), F32),
                   jax.ShapeDtypeStruct((3, D, W_IN_BLK), BF16)),
        scratch_shapes=[pltpu.VMEM((tm, D), F32), pltpu.SemaphoreType.DMA((3,)), pltpu.SemaphoreType.DMA((3,))],
        compiler_params=pltpu.CompilerParams(dimension_semantics=("arbitrary", "arbitrary")),
    )(dproj, w_in_full, xin, b3, norm_g, chip0b)


def _sum_adamw(own, landed, w, m, v, tr, name):
    rows, cols = w.shape
    n_t = rows // tr

    def body(o_ref, l1_ref, l2_ref, l3_ref, w_ref, m_ref, v_ref, g_ref, d_ref, m2_ref, v2_ref):
        g = ((o_ref[...] + l1_ref[...].astype(F32)) + l2_ref[...].astype(F32)) + l3_ref[...].astype(F32)
        delta, m2, v2 = _adamw(w_ref[...], g, m_ref[...], v_ref[...])
        g_ref[...] = g
        d_ref[...] = delta
        m2_ref[...] = m2
        v2_ref[...] = v2

    def spec(k):
        return pl.BlockSpec((tr, cols), lambda i: (i + k * n_t, 0))

    out = jax.ShapeDtypeStruct((rows, cols), F32)
    return pl.pallas_call(
        body, name=name, grid=(n_t,),
        in_specs=[spec(0), spec(0), spec(1), spec(2), spec(0), spec(0), spec(0)],
        out_specs=(spec(0),) * 4, out_shape=(out,) * 4,
    )(own, landed, landed, landed, w, m, v)


def _adamw_3(chip1, far1, ws, ms, vs):
    def body(c_ref, f_ref, *refs):
        w_refs, m_refs, v_refs, outs = refs[0:3], refs[3:6], refs[6:9], refs[9:21]
        for k in range(3):
            g = ((c_ref[0, k] + f_ref[0, k].astype(F32)) + f_ref[1, k].astype(F32)) + f_ref[2, k].astype(F32)
            delta, m2, v2 = _adamw(w_refs[k][0], g, m_refs[k][0], v_refs[k][0])
            for kind, val in enumerate((g, delta, m2, v2)):
                outs[3 * kind + k][0] = val

    full = pl.BlockSpec((1, W_ROW_BLK, D), lambda i: (0, 0, 0))
    out = jax.ShapeDtypeStruct((1, W_ROW_BLK, D), F32)
    res = pl.pallas_call(
        body, name="adamw_3", grid=(1,),
        in_specs=[pl.BlockSpec((1, 3, W_ROW_BLK, D), lambda i: (0, 0, 0, 0)),
                  pl.BlockSpec((3, 3, W_ROW_BLK, D), lambda i: (0, 0, 0, 0))] + [full] * 9,
        out_specs=(full,) * 12, out_shape=(out,) * 12,
    )(chip1, far1, *ws, *ms, *vs)
    return tuple(res[3 * kind:3 * kind + 3] for kind in range(4))


N_SMALL = 9


def _small_update(pack_all, srs_all, ws, ms, vs):
    def body(pk_ref, sr_ref, *refs):
        w_refs, m_refs, v_refs = refs[0:N_SMALL], refs[N_SMALL:2 * N_SMALL], refs[2 * N_SMALL:3 * N_SMALL]
        loss_ref = refs[3 * N_SMALL]
        outs = refs[3 * N_SMALL + 1:7 * N_SMALL + 1]
        tot_sc, tots_sc = refs[7 * N_SMALL + 1:]
        tot = pk_ref[0]
        tot_s = sr_ref[0]
        for d in range(1, N_DEV):
            tot = tot + pk_ref[d]
            tot_s = tot_s + sr_ref[d]
        tot_sc[...] = tot
        tots_sc[...] = tot_s
        loss_ref[...] = jnp.sum(tot_sc[5:6, :], axis=1, keepdims=True)
        lbl = w_refs[4]
        p0 = _sigmoid(lbl[0:1, :] - lbl[1:2, :])
        d_l0 = tot_sc[4:5, :] * p0 * (1.0 - p0)

        def update(k, sel, g):
            delta, m2, v2 = _adamw(w_refs[k][sel], g, m_refs[k][sel], v_refs[k][sel])
            for kind, val in enumerate((g, delta, m2, v2)):
                outs[N_SMALL * kind + k][sel] = val

        everything = (slice(None), slice(None))
        for k, row in ((0, 0), (1, 1), (2, 2), (3, 3), (5, 6), (6, 7)):
            update(k, everything, tot_sc[row:row + 1, :])
        update(4, (slice(0, 1), slice(None)), d_l0)
        update(4, (slice(1, 2), slice(None)), -d_l0)
        update(7, (0, slice(None), slice(None)), tots_sc[0:CONV_K, :])
        update(8, everything, tots_sc[META_ROW:META_ROW + N_META, :])

    shapes = [jax.ShapeDtypeStruct(w.shape, F32) for w in ws]
    res = pl.pallas_call(
        body, name="small_update",
        out_shape=(jax.ShapeDtypeStruct((1, 1), F32), *(shapes * 4)),
        scratch_shapes=[pltpu.VMEM((8, D), F32), pltpu.VMEM((SMALL_ROWS, HEAD_W), F32)],
    )(pack_all, srs_all, *ws, *ms, *vs)
    return res[0], tuple(res[1 + N_SMALL * kind:1 + N_SMALL * (kind + 1)] for kind in range(4))


def _local_step(xin, proj, target, conv_w_full, conv_b, ln_g, ln_b, w3_b, lb_logits, gnorm_g, final_g, ids_mine):
    fg = final_g.reshape(1, D)
    c0, o, s_start, w3_full = _rec_conv_fwd(proj, lb_logits, conv_w_full, conv_b, w3_b)
    d_o, d_c0, d_z, dproj, a3, b3, red = _mid(xin, target, o, c0, proj, w3_full, ln_g, ln_b, gnorm_g, fg)
    p3 = _wgrad3(a3, b3)
    dproj, dlb, d_conv_w, land1 = _rec_conv_bwd(proj, lb_logits, d_o, s_start, d_c0, d_z, conv_w_full, dproj, p3)
    chip1, chip1b = _chip_sum_3(p3, land1, ids_mine)
    return dproj, b3, p3, chip1, chip1b, d_conv_w, red, dlb


def kernel(x, meta_tokens, norm_g, w_in, conv_w, conv_b, ln_g, ln_b, w_conv_out, lb_logits, gnorm_g, w_rec_out, w_out, final_g, loss_target, m_meta_tokens, m_norm_g, m_w_in, m_conv_w, m_conv_b, m_ln_g, m_ln_b, m_w_conv_out, m_lb_logits, m_gnorm_g, m_w_rec_out, m_w_out, m_final_g, v_meta_tokens, v_norm_g, v_w_in, v_conv_w, v_conv_b, v_ln_g, v_ln_b, v_w_conv_out, v_lb_logits, v_gnorm_g, v_w_rec_out, v_w_out, v_final_g):
    mx, my, mc = _my_place()

    ws_s = jnp.concatenate([conv_w[0], jnp.zeros((1, HEAD_W), F32), meta_tokens], axis=0)
    small_full = jnp.transpose(_gather_small(ws_s), (1, 0, 2)).reshape(SMALL_ROWS, D)
    conv_w_full = small_full[0:CONV_K]
    meta_full = small_full[META_ROW:META_ROW + N_META]
    w_in_b, w3_b = _cast_shards(w_in[0], w_conv_out, w_rec_out, w_out)
    first, second, diag = _gather_chips(mx, my, mc)
    use_order = [(mx, my, mc), (mx, my, 1 - mc), (*first, mc), (*second, 1 - mc), (*second, mc), (*first, 1 - mc),
                 (*diag, mc), (*diag, 1 - mc)]
    order = jnp.stack([_dev_index(*p) for p in use_order]).astype(jnp.int32)
    proj, xin, h, w_in_full = _gather_and_proj(x[0], meta_full, norm_g, w_in_b, order)
    h = h.reshape(TP, D)

    ids_mine = jnp.stack([_dev_index(*_chip_rel(mx, my, r), mc) for r in range(4)]).astype(jnp.int32)
    ids_sib = jnp.stack([_dev_index(*_chip_rel(mx, my, r), 1 - mc) for r in range(4)]).astype(jnp.int32)
    dproj, b3, _, chip1, chip1b, d_conv_w, red, dlb = _local_step(
        xin, proj, loss_target[0], conv_w_full, conv_b, ln_g, ln_b, w3_b, lb_logits, gnorm_g, final_g, ids_mine)

    chip0, chip0b, _, far1 = _wgrad_in(h, dproj, jnp.concatenate([ids_sib, ids_mine]), chip1b)
    d_xin, dng, far0 = _dh_and_norm_bwd(dproj, w_in_full, xin, b3, norm_g, chip0b)
    pack = jnp.concatenate([dng[0:1], red[4:5], red[2:3], red[3:4], dlb[0:1], red[5:6], red[1:2], red[0:1]], axis=0)
    g_in, d_in, m_in, v_in = _sum_adamw(chip0.reshape(4 * D, W_IN_BLK), far0.reshape(3 * D, W_IN_BLK), w_in[0],
                                        m_w_in[0], v_w_in[0], 256, "adamw_in")
    big3 = _adamw_3(chip1, far1, (w_conv_out, w_rec_out, w_out), (m_w_conv_out, m_w_rec_out, m_w_out),
                    (v_w_conv_out, v_w_rec_out, v_w_out))

    srs = jnp.concatenate([d_conv_w, d_xin[PAD_FRONT:ROW0]], axis=0)
    srs = jnp.transpose(srs.reshape(SMALL_ROWS, N_DEV, HEAD_W), (1, 0, 2))
    pack_all, srs_all = _exchange_small(pack, srs)
    loss, small = _small_update(
        pack_all, srs_all,
        (norm_g, conv_b, ln_g, ln_b, lb_logits, gnorm_g, final_g.reshape(1, D), conv_w, meta_tokens),
        (m_norm_g, m_conv_b, m_ln_g, m_ln_b, m_lb_logits, m_gnorm_g, m_final_g.reshape(1, D), m_conv_w, m_meta_tokens),
        (v_norm_g, v_conv_b, v_ln_g, v_ln_b, v_lb_logits, v_gnorm_g, v_final_g.reshape(1, D), v_conv_w, v_meta_tokens))

    outs = [loss.reshape(()), d_xin[ROW0:][None]]
    for kind, a_in in enumerate((g_in, d_in, m_in, v_in)):
        ng, cb, lg, lb_, lbl, gg, fg, cw, mt = small[kind]
        a_3 = big3[kind]
        outs += [mt, ng, a_in[None], cw, cb, lg, lb_, a_3[0], lbl, gg, a_3[1], a_3[2], fg.reshape(D)]
    return tuple(outs)
```

```python
import jax
import jax.numpy as jnp
from jax import lax
from jax.experimental import pallas as pl
from jax.experimental.pallas import tpu as pltpu

F32 = jnp.float32
BF16 = jnp.bfloat16
ACT = BF16

D = 1024
SEQ = 4096
N_META = 16
CHUNK = 64
PAD_FRONT = 48
ROW0 = PAD_FRONT + N_META
TP = ROW0 + SEQ
N_CHUNK = TP // CHUNK
HEADS = 8
HEAD_W = 128
D_IN = 9 * D
N_DEV = 8
W_IN_BLK = D_IN // N_DEV
W_ROW_BLK = D // N_DEV
CONV_K = 31
SMALL_ROWS = 48
META_ROW = 32
EPS = 1e-6
HALO = 32

TM_MAT = 1040
TT_WGRAD = 2080
DH_K_BLKS = 2
TM_ELT = 208
CHUNKS_PER_STEP = 5
CONV_STRIPS = 5

ADAM_LR = 0.001
ADAM_B1 = 0.9
ADAM_B2 = 0.999
ADAM_EPS = 1e-08
ADAM_WD = 0.01
ADAM_STEP = 10

MESH_ID = pl.DeviceIdType.MESH
ANY = pl.BlockSpec(memory_space=pl.ANY)


def _sigmoid(v):
    return jax.nn.sigmoid(v)


def _dsilu(silu, s):
    return s + silu * (1.0 - s)


def _dot(a, b):
    return jnp.dot(a, b, preferred_element_type=F32)


def _dot_nt(a, b):
    return lax.dot_general(a, b, (((1,), (1,)), ((), ())), preferred_element_type=F32)


def _dot_tn(a, b):
    return lax.dot_general(a, b, (((0,), (0,)), ((), ())), preferred_element_type=F32)


def _split3(v):
    hi = v.astype(BF16)
    r1 = v - hi.astype(F32)
    mid = r1.astype(BF16)
    lo = (r1 - mid.astype(F32)).astype(BF16)
    return hi, mid, lo


def _tri_matmul(tri, v):
    hi, mid, lo = _split3(v)
    return _dot(tri, hi) + _dot(tri, mid) + _dot(tri, lo)


def _adamw(w, g, m, v):
    m2 = ADAM_B1 * m + (1.0 - ADAM_B1) * g
    v2 = ADAM_B2 * v + (1.0 - ADAM_B2) * jnp.square(g)
    m_hat = m2 / (1.0 - ADAM_B1 ** ADAM_STEP)
    v_hat = v2 / (1.0 - ADAM_B2 ** ADAM_STEP)
    delta = -ADAM_LR * (m_hat / (jnp.sqrt(v_hat) + ADAM_EPS) + ADAM_WD * w)
    return delta, m2, v2


def _window_start(i, tm):
    assert tm % 16 == 0 and ROW0 % 16 == 0
    return pl.multiple_of(16 * jnp.maximum((tm // 16) * i - ROW0 // 16, 0), 16)


def _my_place():
    return lax.axis_index("x"), lax.axis_index("y"), lax.axis_index("c")


def _dev_index(px, py, pc):
    return 4 * px + 2 * py + pc


def _cast_shards(w_in_s, w_conv_s, w_rec_s, w_out_s):
    def body(a_ref, c_ref, r_ref, o_ref, oa_ref, ob_ref):
        oa_ref[...] = a_ref[...].astype(BF16)
        for k, ref in enumerate((c_ref, r_ref, o_ref)):
            ob_ref[k] = ref[0].astype(BF16)

    return pl.pallas_call(
        body, name="cast_shards",
        out_shape=(jax.ShapeDtypeStruct(w_in_s.shape, BF16), jax.ShapeDtypeStruct((3, W_ROW_BLK, D), BF16)),
    )(w_in_s, w_conv_s, w_rec_s, w_out_s)


def _peer(x, y, c, r):
    return (jnp.bitwise_xor(x, (r >> 2) & 1), jnp.bitwise_xor(y, (r >> 1) & 1), jnp.bitwise_xor(c, r & 1))


def _gather_small(small_s):
    def body(s_ref, o_ref, send_sems, recv_sems, local_sem):
        x, y, c = _my_place()
        my_id = _dev_index(x, y, c)
        mine = pltpu.make_async_copy(s_ref, o_ref.at[my_id], local_sem)
        mine.start()
        copies = []
        for r in range(1, N_DEV):
            cp = pltpu.make_async_remote_copy(
                src_ref=s_ref, dst_ref=o_ref.at[my_id], send_sem=send_sems.at[r - 1], recv_sem=recv_sems.at[r - 1],
                device_id=_peer(x, y, c, r), device_id_type=MESH_ID)
            cp.start()
            copies.append(cp)
        for cp in copies:
            cp.wait_recv()
        for cp in copies:
            cp.wait_send()
        mine.wait()

    return pl.pallas_call(
        body, name="gather_small", out_shape=jax.ShapeDtypeStruct((N_DEV,) + small_s.shape, F32),
        in_specs=[ANY], out_specs=ANY,
        scratch_shapes=[pltpu.SemaphoreType.DMA((7,)), pltpu.SemaphoreType.DMA((7,)), pltpu.SemaphoreType.DMA],
    )(small_s)


def _w3_gather(src, out, stage, send_sems, recv_sems, local_sems):
    x, y, c = _my_place()
    me, sibling = (x, y, c), (x, y, 1 - c)
    chips = [(1 - x, y), (x, 1 - y), (1 - x, 1 - y)]

    def block(place):
        d = _dev_index(*place)
        return out.at[:, pl.ds(pl.multiple_of(d * W_ROW_BLK, W_ROW_BLK), W_ROW_BLK), :]

    def copy(k, place, to, from_src=False):
        return pltpu.make_async_remote_copy(
            src_ref=src if from_src else block(place), dst_ref=block(place),
            send_sem=send_sems.at[k], recv_sem=recv_sems.at[k], device_id=to, device_id_type=MESH_ID)

    own_in = pltpu.make_async_copy(src, stage, local_sems.at[0])
    own_out = pltpu.make_async_copy(stage, block(me), local_sems.at[1])

    def start():
        copy(0, me, sibling, from_src=True).start()
        for j, chip in enumerate(chips):
            copy(1 + j, me, (*chip, c), from_src=True).start()
        own_in.start()
        own_in.wait()
        own_out.start()

    def finish():
        for j, chip in enumerate(chips):
            copy(1 + j, (*chip, c), me).wait_recv()
            copy(4 + j, (*chip, c), sibling).start()
        copy(0, sibling, me).wait_recv()
        for j, chip in enumerate(chips):
            copy(4 + j, (*chip, 1 - c), me).wait_recv()
        for k in range(7):
            copy(k, me, me).wait_send()
        own_out.wait()

    return start, finish


def _p3_to_sibling(p3_ref, land_ref, send_sems, recv_sems):
    x, y, c = _my_place()

    def cp(q):
        d = _dev_index(*_chip_rel(x, y, q), 1 - c)
        return pltpu.make_async_remote_copy(
            src_ref=p3_ref.at[:, pl.ds(pl.multiple_of(d * W_ROW_BLK, W_ROW_BLK), W_ROW_BLK), :],
            dst_ref=land_ref.at[q], send_sem=send_sems.at[q], recv_sem=recv_sems.at[q],
            device_id=(x, y, 1 - c), device_id_type=MESH_ID)

    def start():
        for q in range(4):
            cp(q).start()

    def finish():
        for q in range(4):
            cp(q).wait_recv()
        for q in range(4):
            cp(q).wait_send()

    return start, finish


def _partials_to_owners(src_ref, far_ref, send_sems, recv_sems):
    x, y, c = _my_place()

    def cp(q):
        return pltpu.make_async_remote_copy(
            src_ref=src_ref.at[q], dst_ref=far_ref.at[q - 1], send_sem=send_sems.at[q - 1],
            recv_sem=recv_sems.at[q - 1], device_id=(*_chip_rel(x, y, q), c), device_id_type=MESH_ID)

    def start():
        for q in range(1, 4):
            cp(q).start()

    def finish():
        for q in range(1, 4):
            cp(q).wait_recv()
        for q in range(1, 4):
            cp(q).wait_send()

    return start, finish


def _gather_chips(x, y, c):
    first = (jnp.bitwise_xor(x, 1 - c), jnp.bitwise_xor(y, c))
    second = (jnp.bitwise_xor(x, c), jnp.bitwise_xor(y, 1 - c))
    return [first, second, (1 - x, 1 - y)]


def _gather_and_proj(x_seq, meta_full, norm_g, w_in_b, order):
    tm = TM_MAT
    n_m = TP // tm
    last_m = n_m - 1

    def body(order_ref, x_ref, meta_ref, g_ref, s0, proj_ref, xin_ref, h_out, o0, hbuf, wbuf, send_sems, recv_sems,
             local_sems):
        del order_ref
        n = pl.program_id(0)
        m = pl.program_id(1)
        x, y, c = _my_place()
        me, sibling = (x, y, c), (x, y, 1 - c)
        chips = _gather_chips(x, y, c)

        def block(place):
            return o0.at[_dev_index(*place)]

        def copy(k, place, to, from_src=False):
            return pltpu.make_async_remote_copy(
                src_ref=s0 if from_src else block(place), dst_ref=block(place),
                send_sem=send_sems.at[k], recv_sem=recv_sems.at[k], device_id=to, device_id_type=MESH_ID)

        def to_vmem(place, slot):
            return pltpu.make_async_copy(block(place), wbuf.at[slot], local_sems.at[slot])

        own_out = pltpu.make_async_copy(wbuf.at[0], block(me), local_sems.at[2])
        h_copy = pltpu.make_async_copy(hbuf, h_out, local_sems.at[3])

        @pl.when((n == 0) & (m == 0))
        def _():
            copy(0, me, sibling, from_src=True).start()
            for j, chip in enumerate(chips[0:2]):
                copy(1 + j, me, (*chip, c), from_src=True).start()
            mine = pltpu.make_async_copy(s0, wbuf.at[0], local_sems.at[0])
            mine.start()
            mine.wait()
            own_out.start()

        @pl.when(n == 0)
        def _():
            xv = x_ref[...]
            xin_ref[...] = jnp.where(m == 0, pltpu.roll(xv, ROW0, 0), xv)

            @pl.when(m == 0)
            def _():
                xin_ref[0:PAD_FRONT, :] = jnp.zeros((PAD_FRONT, D), F32)
                xin_ref[PAD_FRONT:ROW0, :] = meta_ref[...]

            xv = xin_ref[...]
            r = lax.rsqrt(jnp.mean(xv * xv, axis=-1, keepdims=True) + EPS)
            hbuf[m] = (xv * r * g_ref[...]).astype(BF16)

        between = [4 + c, 5 - c, 6]
        first, second, diag = chips
        plan = [(sibling, (0, sibling), None),
                ((*first, c), (1, (*first, c)), between[0]),
                ((*second, 1 - c), (between[1], (*second, 1 - c)), None),
                ((*second, c), (2, (*second, c)), between[1]),
                ((*first, 1 - c), (between[0], (*first, 1 - c)), None),
                ((*diag, c), (3, (*diag, c)), between[2]),
                ((*diag, 1 - c), (between[2], (*diag, 1 - c)), None)]

        for s, (place, (k, origin), pass_on) in enumerate(plan, start=1):
            @pl.when((n == s - 1) & (m == last_m))
            def _(s=s, place=place, k=k, origin=origin, pass_on=pass_on):
                copy(k, origin, me).wait_recv()
                if pass_on is not None:
                    copy(pass_on, place, sibling).start()
                if s == 2:
                    copy(3, place, (*chips[1], c)).start()
                    own_out.wait()
                to_vmem(place, s % 2).start()

            @pl.when((n == s) & (m == 0))
            def _(s=s, place=place):
                to_vmem(place, s % 2).wait()

        proj_ref[...] = _dot(hbuf[m], wbuf[lax.rem(n, 2)]).astype(BF16)

        @pl.when((n == 0) & (m == last_m))
        def _():
            h_copy.start()

        @pl.when((n == N_DEV - 1) & (m == last_m))
        def _():
            for k in range(7):
                copy(k, me, me).wait_send()
            h_copy.wait()

    return pl.pallas_call(
        body, name="gather_and_proj",
        grid_spec=pltpu.PrefetchScalarGridSpec(
            num_scalar_prefetch=1, grid=(N_DEV, n_m),
            in_specs=[pl.BlockSpec((pl.Element(tm), pl.Element(D)),
                                   lambda n, m, o: (_window_start(jnp.where(n == 0, m, 0), tm), 0)),
                      pl.BlockSpec((N_META, D), lambda n, m, o: (0, 0)),
                      pl.BlockSpec((1, D), lambda n, m, o: (0, 0)), ANY],
            out_specs=(pl.BlockSpec((tm, W_IN_BLK), lambda n, m, o: (m, o[n])),
                       pl.BlockSpec((tm, D), lambda n, m, o: (jnp.where(n == 0, m, last_m), 0)), ANY, ANY),
            scratch_shapes=[pltpu.VMEM((n_m, tm, D), BF16), pltpu.VMEM((2, D, W_IN_BLK), BF16),
                            pltpu.SemaphoreType.DMA((7,)), pltpu.SemaphoreType.DMA((7,)),
                            pltpu.SemaphoreType.DMA((4,))]),
        out_shape=(jax.ShapeDtypeStruct((TP, D_IN), BF16), jax.ShapeDtypeStruct((TP, D), F32),
                   jax.ShapeDtypeStruct((n_m, tm, D), BF16), jax.ShapeDtypeStruct((N_DEV, D, W_IN_BLK), BF16)),
        compiler_params=pltpu.CompilerParams(dimension_semantics=("arbitrary", "arbitrary")),
    )(order, x_seq, meta_full, norm_g, w_in_b)


def _chip_rel(x, y, r):
    return (jnp.bitwise_xor(x, r >> 1), jnp.bitwise_xor(y, r & 1))


def _exchange_small(pack, srs):
    def body(pk, sr, pk_all, sr_all, send_sems, recv_sems, local_sems):
        x, y, c = _my_place()
        my_id = _dev_index(x, y, c)
        mine = [pltpu.make_async_copy(pk, pk_all.at[my_id], local_sems.at[0]),
                pltpu.make_async_copy(sr.at[my_id], sr_all.at[my_id], local_sems.at[1])]
        for cp in mine:
            cp.start()
        copies = []
        for r in range(1, N_DEV):
            peer = (jnp.bitwise_xor(x, (r >> 2) & 1), jnp.bitwise_xor(y, (r >> 1) & 1), jnp.bitwise_xor(c, r & 1))
            peer_id = _dev_index(*peer)
            for a, (src, dst) in enumerate(((pk, pk_all.at[my_id]), (sr.at[peer_id], sr_all.at[my_id]))):
                cp = pltpu.make_async_remote_copy(
                    src_ref=src, dst_ref=dst, send_sem=send_sems.at[a * 7 + r - 1], recv_sem=recv_sems.at[a * 7 + r - 1],
                    device_id=peer, device_id_type=MESH_ID)
                cp.start()
                copies.append(cp)
        for cp in copies:
            cp.wait_recv()
        for cp in copies:
            cp.wait_send()
        for cp in mine:
            cp.wait()

    return pl.pallas_call(
        body, name="exchange_small",
        out_shape=(jax.ShapeDtypeStruct((N_DEV,) + pack.shape, F32), jax.ShapeDtypeStruct(srs.shape, F32)),
        in_specs=[ANY, ANY], out_specs=(ANY, ANY),
        scratch_shapes=[pltpu.SemaphoreType.DMA((14,)), pltpu.SemaphoreType.DMA((14,)), pltpu.SemaphoreType.DMA((2,))],
    )(pack, srs)


N_CB = D // HEAD_W


def _store_by_cb(ref, idx, rows, val):
    for cb in range(N_CB):
        ref[(*idx, cb, rows, slice(None))] = val[:, cb * HEAD_W:(cb + 1) * HEAD_W]


def _fill_shifts(sh, tm):
    n = tm + HALO - 8
    for s in range(1, 8):
        for cb in range(N_CB):
            sh[s, cb, 0:n, :] = sh[0, cb, s:s + n, :]


def _gates(p_ref, lbl_ref, chunk, bsc):
    lb = _sigmoid(lbl_ref[0:1, :] - lbl_ref[1:2, :])
    q_raw = p_ref[:, 0:D].astype(F32)
    f_raw = p_ref[:, D:2 * D].astype(F32)
    sq = _sigmoid(q_raw)
    q = q_raw * sq
    sg = _sigmoid(f_raw)
    f = lb + (1.0 - lb) * sg
    row = lax.broadcasted_iota(jnp.int32, (CHUNK, 1), 0) + chunk * CHUNK
    valid = row >= PAD_FRONT
    lf = jnp.where(valid, jnp.log(f), 0.0)
    k = jnp.where(valid, 1.0 - f, 0.0)
    r_i = lax.broadcasted_iota(jnp.int32, (CHUNK, CHUNK), 0)
    c_i = lax.broadcasted_iota(jnp.int32, (CHUNK, CHUNK), 1)
    causal = r_i >= c_i
    bsc[...] = _tri_matmul(causal.astype(BF16), lf)
    b = bsc[...]
    b_mid = bsc[CHUNK // 2 - 1:CHUNK // 2, :]
    b_last = bsc[CHUNK - 1:CHUNK, :]
    e_q = jnp.exp(b)
    e_qm = jnp.exp(b - b_mid)
    e_km = jnp.exp(b_mid - b)
    e_kh = jnp.exp(b_last - b)
    e_last = jnp.exp(b_last)
    return dict(lb=lb, q_raw=q_raw, sq=sq, q=q, sg=sg, f=f, k=k, valid=valid, causal=causal,
                e_q=e_q, e_qm=e_qm, e_km=e_km, e_kh=e_kh, e_last=e_last)


def _rec_conv_fwd(proj, lb_logits, conv_w, conv_b, w3_b):
    cps = CHUNKS_PER_STEP
    tm = cps * CHUNK
    n_strip = CONV_STRIPS
    strip = tm // n_strip

    def body(p_ref, lbl_ref, pg_ref, w_ref, b_ref, w3s_ref, c0_ref, o_ref, s_ref, w3o_ref,
             st, bsc, sh, c0_sc, w3buf, send_sems, recv_sems, local_sems):
        n = pl.program_id(0)
        gather_start, gather_finish = _w3_gather(w3s_ref, w3o_ref, w3buf, send_sems, recv_sems, local_sems)

        @pl.when(n == 0)
        def _():
            st[...] = jnp.zeros_like(st)
            sh[0, :, 0:HALO, :] = jnp.zeros((N_CB, HALO, HEAD_W), F32)
            gather_start()

        @pl.when(n > 0)
        def _():
            sh[0, :, 0:HALO, :] = sh[0, :, tm:tm + HALO, :]

        ga = pg_ref[:, 0:D].astype(F32)
        gb = pg_ref[:, D:2 * D].astype(F32)
        _store_by_cb(sh, (0,), slice(HALO, HALO + tm), ga * _sigmoid(gb))
        _fill_shifts(sh, tm)

        def conv_unit(cb, s_i):
            cs = slice(cb * HEAD_W, (cb + 1) * HEAD_W)
            acc = jnp.broadcast_to(b_ref[:, cs], (strip, HEAD_W))
            for j in range(CONV_K):
                off = HALO - (CONV_K - 1) + j
                lo = s_i * strip + 8 * (off // 8)
                acc = acc + w_ref[j:j + 1, cs] * sh[off % 8, cb, lo:lo + strip, :]
            c0_sc[s_i * strip:(s_i + 1) * strip, cs] = acc

        units = [(cb, s_i) for cb in range(N_CB) for s_i in range(n_strip)]

        def prep(ci):
            g = _gates(p_ref.at[pl.ds(ci * CHUNK, CHUNK)], lbl_ref, n * cps + ci, bsc.at[ci])
            g["q1"] = (g["q"] * g["e_q"]).astype(BF16)
            g["qm"] = (g["q"] * g["e_qm"]).astype(BF16)
            g["km"] = (g["k"] * g["e_km"]).astype(BF16)
            g["kh"] = (g["k"] * g["e_kh"]).astype(BF16)
            return g

        def heads(ci, g):
            rs = pl.ds(ci * CHUNK, CHUNK)
            pv = p_ref.at[rs]
            s_ref[ci] = st[...]
            for h in range(HEADS):
                if units:
                    conv_unit(*units.pop(0))
                sl = slice(h * HEAD_W, (h + 1) * HEAD_W)
                v = pv[:, 2 * D + h * HEAD_W:2 * D + (h + 1) * HEAD_W]
                att = jnp.where(g["causal"], _dot_nt(g["qm"][:, sl], g["km"][:, sl]), 0.0).astype(BF16)
                s_h = st[h]
                o_ref[rs, sl] = (_dot_nt(g["q1"][:, sl], s_h.astype(BF16)) + _dot(att, v)).astype(ACT)
                st[h] = s_h * g["e_last"][:, sl] + _dot_tn(v, g["kh"][:, sl])

        ready = prep(0)
        for ci in range(cps):
            coming = prep(ci + 1) if ci + 1 < cps else None
            heads(ci, ready)
            ready = coming
        while units:
            conv_unit(*units.pop(0))
        c0_ref[...] = c0_sc[...].astype(ACT)

        @pl.when(n == N_CHUNK // cps - 1)
        def _():
            gather_finish()

    def rows_of(width, col):
        return pl.BlockSpec((tm, width), lambda n: (n, col))

    return pl.pallas_call(
        body, name="rec_conv_fwd", grid=(N_CHUNK // cps,),
        in_specs=[rows_of(3 * D, 1), pl.BlockSpec((2, D), lambda n: (0, 0)), rows_of(2 * D, 0),
                  pl.BlockSpec((CONV_K, D), lambda n: (0, 0)), pl.BlockSpec((1, D), lambda n: (0, 0)), ANY],
        out_specs=(rows_of(D, 0), rows_of(D, 0), pl.BlockSpec((cps, HEADS, HEAD_W, HEAD_W), lambda n: (n, 0, 0, 0)), ANY),
        out_shape=(jax.ShapeDtypeStruct((TP, D), ACT), jax.ShapeDtypeStruct((TP, D), ACT),
                   jax.ShapeDtypeStruct((N_CHUNK, HEADS, HEAD_W, HEAD_W), F32), jax.ShapeDtypeStruct((3, D, D), BF16)),
        scratch_shapes=[pltpu.VMEM((HEADS, HEAD_W, HEAD_W), F32), pltpu.VMEM((cps, CHUNK, D), F32),
                        pltpu.VMEM((8, N_CB, HALO + tm, HEAD_W), F32), pltpu.VMEM((tm, D), F32),
                        pltpu.VMEM((3, W_ROW_BLK, D), BF16), pltpu.SemaphoreType.DMA((7,)),
                        pltpu.SemaphoreType.DMA((7,)), pltpu.SemaphoreType.DMA((2,))],
        compiler_params=pltpu.CompilerParams(dimension_semantics=("arbitrary",)),
    )(proj, lb_logits, proj, conv_w, conv_b, w3_b)


def _mid(xin, tgt, o, c0, proj, w3, ln_g, ln_b, gnorm_g, final_g):
    tm = TM_ELT

    def body(x_ref, t_ref, o_ref, c0_ref, z_ref, gr_ref, mc_ref, mr_ref, w_ref, lng_ref, lnb_ref, gng_ref, fg_ref,
             do_ref, dc0_ref, dz_ref, dp_ref, a3_ref, b3_ref, red_ref, on_sc, don_sc):
        i = pl.program_id(0)

        @pl.when(i == 0)
        def _():
            red_ref[...] = jnp.zeros_like(red_ref)

        w_conv, w_rec, w_out = w_ref[0], w_ref[1], w_ref[2]
        c0v = c0_ref[...].astype(F32)
        mu = jnp.mean(c0v, axis=-1, keepdims=True)
        xc = c0v - mu
        rstd = lax.rsqrt(jnp.mean(xc * xc, axis=-1, keepdims=True) + EPS)
        xh = xc * rstd
        c1 = xh * lng_ref[...] + lnb_ref[...]
        s1 = _sigmoid(c1)
        c2 = c1 * s1
        z = z_ref[...].astype(F32)
        sz = _sigmoid(z)
        silu_z = z * sz
        u_conv = (c2 * silu_z).astype(BF16)
        y_conv = _dot(u_conv, w_conv)
        ov = o_ref[...].astype(F32)
        r3 = []
        for h in range(HEADS):
            sl = slice(h * HEAD_W, (h + 1) * HEAD_W)
            oh = ov[:, sl]
            r_h = lax.rsqrt(jnp.mean(oh * oh, axis=-1, keepdims=True) + EPS)
            r3.append(r_h)
            on_sc[:, sl] = oh * r_h
        o_n = on_sc[...]
        o_g = o_n * gng_ref[...]
        gr = gr_ref[...].astype(F32)
        sgr = _sigmoid(gr)
        silu_g = gr * sgr
        u_rec = (o_g * silu_g).astype(BF16)
        y_rec = _dot(u_rec, w_rec)
        mc = mc_ref[...].astype(F32)
        mr = mr_ref[...].astype(F32)
        smc = _sigmoid(mc)
        smr = _sigmoid(mr)
        merged = (smc * y_conv + smr * y_rec).astype(BF16)
        res = x_ref[...] + _dot(merged, w_out)
        r2 = lax.rsqrt(jnp.mean(res * res, axis=-1, keepdims=True) + EPS)
        xh2 = res * r2
        row = lax.broadcasted_iota(jnp.int32, (tm, 1), 0) + i * tm
        real = row >= ROW0
        tgt = t_ref[...]
        tgt = jnp.where(i == 0, pltpu.roll(tgt, ROW0, 0), tgt)
        diff = jnp.where(real, xh2 * fg_ref[...] - tgt, 0.0)
        d_y = diff * (1.0 / D)
        d_xh2 = d_y * fg_ref[...]
        d_res = r2 * (d_xh2 - xh2 * jnp.mean(d_xh2 * xh2, axis=-1, keepdims=True))
        d_res_b = d_res.astype(BF16)
        d_merged = _dot_nt(d_res_b, w_out)
        d_yc_f = d_merged * smc
        d_yr_f = d_merged * smr
        d_yc = d_yc_f.astype(BF16)
        d_yr = d_yr_f.astype(BF16)
        dp_ref[:, D:2 * D] = (d_yc_f * y_conv * (1.0 - smc)).astype(BF16)
        dp_ref[:, 2 * D:3 * D] = (d_yr_f * y_rec * (1.0 - smr)).astype(BF16)
        d_ur = _dot_nt(d_yr, w_rec)
        d_og = d_ur * silu_g
        dp_ref[:, 0:D] = (d_ur * o_g * _dsilu(silu_g, sgr)).astype(BF16)
        d_on = d_og * gng_ref[...]
        for h in range(HEADS):
            sl = slice(h * HEAD_W, (h + 1) * HEAD_W)
            d_h = d_on[:, sl]
            n_h = o_n[:, sl]
            don_sc[:, sl] = r3[h] * (d_h - n_h * jnp.mean(d_h * n_h, axis=-1, keepdims=True))
        do_ref[...] = don_sc[...].astype(ACT)
        d_uc = _dot_nt(d_yc, w_conv)
        d_c2 = d_uc * silu_z
        dz_ref[...] = (d_uc * c2 * _dsilu(silu_z, sz)).astype(BF16)
        d_c1 = d_c2 * _dsilu(c2, s1)
        d_xh = d_c1 * lng_ref[...]
        d_c0 = rstd * (d_xh - jnp.mean(d_xh, axis=-1, keepdims=True)
                       - xh * jnp.mean(d_xh * xh, axis=-1, keepdims=True))
        dc0_ref[...] = d_c0.astype(ACT)
        a3_ref[0] = u_conv
        b3_ref[0] = d_yc
        a3_ref[1] = u_rec
        b3_ref[1] = d_yr
        a3_ref[2] = merged
        b3_ref[2] = d_res_b
        def colsum(vv):
            return jnp.sum(vv, axis=0, keepdims=True)

        red_ref[0:1, :] += colsum(d_y * xh2)
        red_ref[1:2, :] += colsum(d_og * o_n)
        red_ref[2:3, :] += colsum(d_c1 * xh)
        red_ref[3:4, :] += colsum(d_c1)
        red_ref[4:5, :] += colsum(d_c0)
        red_ref[5:6, :] += colsum(diff * diff) * (0.5 / D)

    def row_block(width, col):
        return pl.BlockSpec((tm, width), lambda i: (i, col))

    def const_block(shape):
        return pl.BlockSpec(shape, lambda i: (0,) * len(shape))

    stack = jax.ShapeDtypeStruct((3, TP, D), BF16)
    stack_spec = pl.BlockSpec((3, tm, D), lambda i: (0, i, 0))
    return pl.pallas_call(
        body, name="mid", grid=(TP // tm,),
        in_specs=[row_block(D, 0),
                  pl.BlockSpec((pl.Element(tm), pl.Element(D)), lambda i: (_window_start(i, tm), 0)),
                  row_block(D, 0), row_block(D, 0),
                  row_block(D, 2), row_block(D, 6), row_block(D, 7), row_block(D, 8),
                  pl.BlockSpec((3, D, D), lambda i: (0, 0, 0), pipeline_mode=pl.Buffered(1)),
                  const_block((1, D)), const_block((1, D)), const_block((1, D)), const_block((1, D))],
        out_specs=(row_block(D, 0), row_block(D, 0), row_block(D, 0), row_block(3 * D, 2),
                   stack_spec, stack_spec, const_block((8, D))),
        out_shape=(jax.ShapeDtypeStruct((TP, D), ACT), jax.ShapeDtypeStruct((TP, D), ACT),
                   jax.ShapeDtypeStruct((TP, D), BF16),
                   jax.ShapeDtypeStruct((TP, D_IN), BF16), stack, stack, jax.ShapeDtypeStruct((8, D), F32)),
        scratch_shapes=[pltpu.VMEM((tm, D), F32), pltpu.VMEM((tm, D), F32)],
        compiler_params=pltpu.CompilerParams(dimension_semantics=("arbitrary",), vmem_limit_bytes=60 * 1024 * 1024),
    )(xin, tgt, o, c0, proj, proj, proj, proj, w3, ln_g, ln_b, gnorm_g, final_g)


def _rec_conv_bwd(proj, lb_logits, d_o, s_start, d_c0, d_z, conv_w, dproj, p3):
    cps = CHUNKS_PER_STEP
    tm = cps * CHUNK
    last = N_CHUNK // cps - 1
    n_strip = CONV_STRIPS
    strip = tm // n_strip

    def body(p_ref, lbl_ref, do_ref, s_ref, pg_ref, dc_ref, dz_ref, w_ref, dproj_in, p3_ref,
             dp_ref, dlb_ref, dw_ref, land_ref,
             dst, bsc, dq_sc, dk_sc, g_sc, dsh, a_sc, da_sc, acc, send_sems, recv_sems):
        del dproj_in
        n = pl.program_id(0)
        ride_start, ride_finish = _p3_to_sibling(p3_ref, land_ref, send_sems, recv_sems)

        @pl.when(n == 0)
        def _():
            ride_start()
            dst[...] = jnp.zeros_like(dst)
            dlb_ref[...] = jnp.zeros_like(dlb_ref)
            dsh[0, :, tm:tm + HALO, :] = jnp.zeros((N_CB, HALO, HEAD_W), F32)
            acc[...] = jnp.zeros_like(acc)

        @pl.when(n > 0)
        def _():
            dsh[0, :, tm:tm + HALO, :] = dsh[0, :, 0:HALO, :]

        _store_by_cb(dsh, (0,), slice(0, tm), dc_ref[...].astype(F32))
        _fill_shifts(dsh, tm)
        ga = pg_ref[:, 0:D].astype(F32)
        sb = _sigmoid(pg_ref[:, D:2 * D].astype(F32))
        a = ga * sb
        _store_by_cb(a_sc, (), slice(0, tm), a)

        def conv_unit(cb, st):
            cs = slice(cb * HEAD_W, (cb + 1) * HEAD_W)
            rows = slice(st * strip, (st + 1) * strip)
            a_s = a_sc[cb, rows, :]
            d_a = jnp.zeros((strip, HEAD_W), F32)
            for j in range(CONV_K):
                off = CONV_K - 1 - j
                lo = st * strip + 8 * (off // 8)
                slab = dsh[off % 8, cb, lo:lo + strip, :]
                d_a = d_a + w_ref[j:j + 1, cs] * slab
                acc[j, :, cs] += jnp.sum((a_s * slab).reshape(strip // 8, 8, HEAD_W), axis=0)
            da_sc[rows, cs] = d_a

        units = [(cb, st) for cb in range(N_CB) for st in range(n_strip)]

        def prep(ci):
            g = _gates(p_ref.at[pl.ds(ci * CHUNK, CHUNK)], lbl_ref, (last - n) * cps + ci, bsc.at[ci])
            g["q1"] = (g["q"] * g["e_q"]).astype(BF16)
            qm_f = g["q"] * g["e_qm"]
            km_f = g["k"] * g["e_km"]
            g["qm"] = qm_f.astype(BF16)
            g["km"] = km_f.astype(BF16)
            g["qm_lo"] = (qm_f - g["qm"].astype(F32)).astype(BF16)
            g["km_lo"] = (km_f - g["km"].astype(F32)).astype(BF16)
            g["kh_f"] = g["k"] * g["e_kh"]
            g["kh"] = g["kh_f"].astype(BF16)
            return g

        def heads_and_post(ci, g):
            rs = pl.ds(ci * CHUNK, CHUNK)
            pv = p_ref.at[rs]
            dpv = dp_ref.at[rs]
            q1, qm, km, qm_lo, km_lo, kh_f, kh = (g[k] for k in ("q1", "qm", "km", "qm_lo", "km_lo", "kh_f", "kh"))
            for h in range(HEADS):
                if units:
                    conv_unit(*units.pop(0))
                sl = slice(h * HEAD_W, (h + 1) * HEAD_W)
                v = pv[:, 2 * D + h * HEAD_W:2 * D + (h + 1) * HEAD_W]
                d_oh = do_ref[rs, sl].astype(BF16)
                s0 = s_ref[ci, h]
                ds_end = dst[h]
                ds_end_b = ds_end.astype(BF16)
                att = jnp.where(g["causal"], _dot_nt(qm[:, sl], km[:, sl]), 0.0).astype(BF16)
                d_att = jnp.where(g["causal"], _dot_nt(d_oh, v), 0.0).astype(BF16)
                d_v = _dot_tn(att, d_oh) + _dot_nt(kh[:, sl], ds_end_b)
                d_qm2 = _dot(d_att, jnp.concatenate([km[:, sl], km_lo[:, sl]], axis=1))
                d_qm = d_qm2[:, 0:HEAD_W] + d_qm2[:, HEAD_W:2 * HEAD_W]
                d_q1 = _dot(d_oh, s0.astype(BF16))
                d_km2 = _dot_tn(d_att, jnp.concatenate([qm[:, sl], qm_lo[:, sl]], axis=1))
                d_km = d_km2[:, 0:HEAD_W] + d_km2[:, HEAD_W:2 * HEAD_W]
                d_kh = _dot(v, ds_end_b)
                dq_sc[ci, :, sl] = d_qm * g["e_qm"][:, sl] + d_q1 * g["e_q"][:, sl]
                dk_sc[ci, :, sl] = d_km * g["e_km"][:, sl] + d_kh * g["e_kh"][:, sl]
                g_sc[ci, :, sl] = (jnp.sum(kh_f[:, sl] * d_kh, axis=0, keepdims=True)
                                   + g["e_last"][:, sl] * jnp.sum(ds_end * s0, axis=0, keepdims=True))
                dst[h] = ds_end * g["e_last"][:, sl] + _dot_tn(d_oh, q1[:, sl])
                dpv[:, 5 * D + h * HEAD_W:5 * D + (h + 1) * HEAD_W] = d_v.astype(BF16)
            d_q = dq_sc[ci]
            d_k = dk_sc[ci]
            d_b = g["q"] * d_q - g["k"] * d_k
            anti = jnp.logical_not(g["causal"]) | (lax.broadcasted_iota(jnp.int32, (CHUNK, CHUNK), 0)
                                                    == lax.broadcasted_iota(jnp.int32, (CHUNK, CHUNK), 1))
            d_lf = _tri_matmul(anti.astype(BF16), d_b) + g_sc[ci]
            d_f = jnp.where(g["valid"], d_lf / g["f"] - d_k, 0.0)
            sg = g["sg"]
            dlb_ref[0:1, :] += jnp.sum(d_f * (1.0 - sg), axis=0, keepdims=True)
            dpv[:, 3 * D:4 * D] = (d_q * _dsilu(g["q"], g["sq"])).astype(BF16)
            dpv[:, 4 * D:5 * D] = (d_f * (1.0 - g["lb"]) * sg * (1.0 - sg)).astype(BF16)

        ready = prep(cps - 1)
        for ci in reversed(range(cps)):
            coming = prep(ci - 1) if ci > 0 else None
            heads_and_post(ci, ready)
            ready = coming
        while units:
            conv_unit(*units.pop(0))

        d_a = da_sc[...]
        dp_ref[:, 0:D] = (d_a * sb).astype(BF16)
        dp_ref[:, D:2 * D] = (d_a * a * (1.0 - sb)).astype(BF16)
        dp_ref[:, 2 * D:3 * D] = dz_ref[...]

        @pl.when(n == last)
        def _():
            for j in range(CONV_K):
                dw_ref[j:j + 1, :] = jnp.sum(acc[j], axis=0, keepdims=True)
            dw_ref[CONV_K:CONV_K + 1, :] = jnp.zeros((1, D), F32)
            ride_finish()

    def rows_of(width, col):
        return pl.BlockSpec((tm, width), lambda n: (last - n, col))

    return pl.pallas_call(
        body, name="rec_conv_bwd", grid=(N_CHUNK // cps,),
        in_specs=[rows_of(3 * D, 1), pl.BlockSpec((2, D), lambda n: (0, 0)), rows_of(D, 0),
                  pl.BlockSpec((cps, HEADS, HEAD_W, HEAD_W), lambda n: (last - n, 0, 0, 0)),
                  rows_of(2 * D, 0), rows_of(D, 0), rows_of(D, 0), pl.BlockSpec((CONV_K, D), lambda n: (0, 0)), ANY, ANY],
        out_specs=(rows_of(6 * D, 0), pl.BlockSpec((8, D), lambda n: (0, 0)),
                   pl.BlockSpec((CONV_K + 1, D), lambda n: (0, 0)), ANY),
        out_shape=(jax.ShapeDtypeStruct((TP, D_IN), BF16), jax.ShapeDtypeStruct((8, D), F32),
                   jax.ShapeDtypeStruct((CONV_K + 1, D), F32), jax.ShapeDtypeStruct((4, 3, W_ROW_BLK, D), F32)),
        scratch_shapes=[pltpu.VMEM((HEADS, HEAD_W, HEAD_W), F32), pltpu.VMEM((cps, CHUNK, D), F32),
                        pltpu.VMEM((cps, CHUNK, D), F32), pltpu.VMEM((cps, CHUNK, D), F32),
                        pltpu.VMEM((cps, 1, D), F32),
                        pltpu.VMEM((8, N_CB, tm + HALO, HEAD_W), F32), pltpu.VMEM((N_CB, tm, HEAD_W), F32),
                        pltpu.VMEM((tm, D), F32), pltpu.VMEM((CONV_K, 8, D), F32),
                        pltpu.SemaphoreType.DMA((4,)), pltpu.SemaphoreType.DMA((4,))],
        input_output_aliases={8: 0},
        compiler_params=pltpu.CompilerParams(dimension_semantics=("arbitrary",)),
    )(proj, lb_logits, d_o, s_start, proj, d_c0, d_z, conv_w, dproj, p3)


def _wgrad3(a3, b3):
    tt = TT_WGRAD

    def body(a_ref, b_ref, o_ref):
        @pl.when(pl.program_id(1) == 0)
        def _():
            o_ref[...] = jnp.zeros_like(o_ref)

        o_ref[0] += _dot_tn(a_ref[0], b_ref[0])

    return pl.pallas_call(
        body, name="wgrad3", grid=(3, TP // tt),
        in_specs=[pl.BlockSpec((1, tt, D), lambda g, t: (g, t, 0)), pl.BlockSpec((1, tt, D), lambda g, t: (g, t, 0))],
        out_specs=pl.BlockSpec((1, D, D), lambda g, t: (g, 0, 0)),
        out_shape=jax.ShapeDtypeStruct((3, D, D), F32),
        compiler_params=pltpu.CompilerParams(dimension_semantics=("arbitrary", "arbitrary")),
    )(a3, b3)


def _wgrad_in(h, dproj, ids, chip1b):
    tt = TT_WGRAD
    n_t = TP // tt

    def body(ids_ref, a_ref, b_ref, c1_ref, o_ref, ob_ref, l0_ref, far_ref, acc, tmp, send_sems, recv_sems, tmp_sem,
             far_send_sems, far_recv_sems):
        del ids_ref
        r = pl.program_id(0)
        t = pl.program_id(1)
        x, y, c = _my_place()
        sibling = (x, y, 1 - c)
        slot = lax.rem(r, 2)
        ride_start, ride_finish = _partials_to_owners(c1_ref, far_ref, far_send_sems, far_recv_sems)

        @pl.when((r == 0) & (t == 0))
        def _():
            ride_start()

        def send_in(q):
            return pltpu.make_async_remote_copy(
                src_ref=acc.at[q % 2], dst_ref=l0_ref.at[q], send_sem=send_sems.at[q], recv_sem=recv_sems.at[q],
                device_id=sibling, device_id_type=MESH_ID)

        def landed(q):
            return pltpu.make_async_copy(l0_ref.at[q], tmp, tmp_sem)

        @pl.when(t == 0)
        def _():
            acc[slot] = jnp.zeros((D, W_IN_BLK), F32)

        acc[slot] += _dot_tn(a_ref[...], b_ref[...])

        for q in range(4):
            @pl.when((r == q) & (t == n_t - 1))
            def _(q=q):
                if q >= 1:
                    send_in(q - 1).wait_send()
                send_in(q).start()

            @pl.when((r == 4 + q) & (t == n_t - 2))
            def _(q=q):
                if q == 0:
                    send_in(3).wait_send()
                send_in(q).wait_recv()
                landed(q).start()

            @pl.when((r == 4 + q) & (t == n_t - 1))
            def _(q=q):
                landed(q).wait()
                tot = acc[q % 2] + tmp[...]
                o_ref[0] = tot
                ob_ref[0] = tot.astype(BF16)

        @pl.when((r == N_DEV - 1) & (t == n_t - 1))
        def _():
            ride_finish()

    blk = pl.BlockSpec((1, D, W_IN_BLK), lambda r, t, ids: (jnp.maximum(r - 4, 0), 0, 0))
    return pl.pallas_call(
        body, name="wgrad_in",
        grid_spec=pltpu.PrefetchScalarGridSpec(
            num_scalar_prefetch=1, grid=(N_DEV, n_t),
            in_specs=[pl.BlockSpec((tt, D), lambda r, t, ids: (t, 0)),
                      pl.BlockSpec((tt, W_IN_BLK), lambda r, t, ids: (t, ids[r])), ANY],
            out_specs=(blk, blk, ANY, ANY),
            scratch_shapes=[pltpu.VMEM((2, D, W_IN_BLK), F32), pltpu.VMEM((D, W_IN_BLK), F32),
                            pltpu.SemaphoreType.DMA((4,)), pltpu.SemaphoreType.DMA((4,)), pltpu.SemaphoreType.DMA,
                            pltpu.SemaphoreType.DMA((3,)), pltpu.SemaphoreType.DMA((3,))]),
        out_shape=(jax.ShapeDtypeStruct((4, D, W_IN_BLK), F32), jax.ShapeDtypeStruct((4, D, W_IN_BLK), BF16),
                   jax.ShapeDtypeStruct((4, D, W_IN_BLK), F32), jax.ShapeDtypeStruct((3, 3, W_ROW_BLK, D), BF16)),
        compiler_params=pltpu.CompilerParams(dimension_semantics=("arbitrary", "arbitrary")),
    )(ids, h, dproj, chip1b)


def _chip_sum_3(p3, land1, ids_mine):
    def body(ids_ref, p_ref, l_ref, o_ref, ob_ref):
        del ids_ref
        tot = p_ref[...] + l_ref[0]
        o_ref[0] = tot
        ob_ref[0] = tot.astype(BF16)

    blk = pl.BlockSpec((1, 3, W_ROW_BLK, D), lambda r, ids: (r, 0, 0, 0))
    return pl.pallas_call(
        body, name="chip_sum_3",
        grid_spec=pltpu.PrefetchScalarGridSpec(
            num_scalar_prefetch=1, grid=(4,),
            in_specs=[pl.BlockSpec((3, W_ROW_BLK, D), lambda r, ids: (0, ids[r], 0)), blk],
            out_specs=(blk, blk)),
        out_shape=(jax.ShapeDtypeStruct((4, 3, W_ROW_BLK, D), F32), jax.ShapeDtypeStruct((4, 3, W_ROW_BLK, D), BF16)),
    )(ids_mine, p3, land1)


def _dh_and_norm_bwd(dproj, w_in_full, xin, b3, norm_g, chip0b):
    tm = TM_MAT
    n_k = N_DEV // DH_K_BLKS
    n_m = TP // tm

    def body(dp_ref, w_ref, x_ref, dr_ref, g_ref, c0_ref, dx_ref, dg_ref, f0_ref, acc, send_sems, recv_sems):
        m = pl.program_id(0)
        k = pl.program_id(1)
        ride_start, ride_finish = _partials_to_owners(c0_ref, f0_ref, send_sems, recv_sems)

        @pl.when((m == 0) & (k == 0))
        def _():
            ride_start()

        @pl.when(k == 0)
        def _():
            acc[...] = jnp.zeros_like(acc)

        part = _dot_nt(dp_ref[:, 0:W_IN_BLK], w_ref[0])
        for j in range(1, DH_K_BLKS):
            part = part + _dot_nt(dp_ref[:, j * W_IN_BLK:(j + 1) * W_IN_BLK], w_ref[j])
        acc[...] += part

        @pl.when((k == n_k - 1) & (m == 0))
        def _():
            dg_ref[...] = jnp.zeros_like(dg_ref)

        @pl.when(k == n_k - 1)
        def _():
            xv = x_ref[...]
            r1 = lax.rsqrt(jnp.mean(xv * xv, axis=-1, keepdims=True) + EPS)
            xh = xv * r1
            d_h = acc[...]
            dg_ref[0:1, :] += jnp.sum(d_h * xh, axis=0, keepdims=True)
            d_xh = d_h * g_ref[...]
            dx_ref[...] = dr_ref[0].astype(F32) + r1 * (d_xh - xh * jnp.mean(d_xh * xh, axis=-1, keepdims=True))

        @pl.when((m == n_m - 1) & (k == n_k - 1))
        def _():
            ride_finish()

    return pl.pallas_call(
        body, name="dh_norm_bwd", grid=(n_m, n_k),
        in_specs=[pl.BlockSpec((tm, DH_K_BLKS * W_IN_BLK), lambda m, k: (m, k)),
                  pl.BlockSpec((DH_K_BLKS, D, W_IN_BLK), lambda m, k: (k, 0, 0)),
                  pl.BlockSpec((tm, D), lambda m, k: (m, 0)), pl.BlockSpec((1, tm, D), lambda m, k: (2, m, 0)),
                  pl.BlockSpec((1, D), lambda m, k: (0, 0)), ANY],
        out_specs=(pl.BlockSpec((tm, D), lambda m, k: (m, 0)), pl.BlockSpec((8, D), lambda m, k: (0, 0)), ANY),
        out_shape=(jax.ShapeDtypeStruct((TP, D), F32), jax.ShapeDtypeStruct((8, D), F32),
                   jax.ShapeDtypeStruct((3, D, W_IN_BLK), BF16)),
        scratch_shapes=[pltpu.VMEM((tm, D), F32), pltpu.SemaphoreType.DMA((3,)), pltpu.SemaphoreType.DMA((3,))],
        compiler_params=pltpu.CompilerParams(dimension_semantics=("arbitrary", "arbitrary")),
    )(dproj, w_in_full, xin, b3, norm_g, chip0b)


def _sum_adamw(own, landed, w, m, v, tr, name):
    rows, cols = w.shape
    n_t = rows // tr

    def body(o_ref, l1_ref, l2_ref, l3_ref, w_ref, m_ref, v_ref, g_ref, d_ref, m2_ref, v2_ref):
        g = ((o_ref[...] + l1_ref[...].astype(F32)) + l2_ref[...].astype(F32)) + l3_ref[...].astype(F32)
        delta, m2, v2 = _adamw(w_ref[...], g, m_ref[...], v_ref[...])
        g_ref[...] = g
        d_ref[...] = delta
        m2_ref[...] = m2
        v2_ref[...] = v2

    def spec(k):
        return pl.BlockSpec((tr, cols), lambda i: (i + k * n_t, 0))

    out = jax.ShapeDtypeStruct((rows, cols), F32)
    return pl.pallas_call(
        body, name=name, grid=(n_t,),
        in_specs=[spec(0), spec(0), spec(1), spec(2), spec(0), spec(0), spec(0)],
        out_specs=(spec(0),) * 4, out_shape=(out,) * 4,
    )(own, landed, landed, landed, w, m, v)


def _adamw_3(chip1, far1, ws, ms, vs):
    def body(c_ref, f_ref, *refs):
        w_refs, m_refs, v_refs, outs = refs[0:3], refs[3:6], refs[6:9], refs[9:21]
        for k in range(3):
            g = ((c_ref[0, k] + f_ref[0, k].astype(F32)) + f_ref[1, k].astype(F32)) + f_ref[2, k].astype(F32)
            delta, m2, v2 = _adamw(w_refs[k][0], g, m_refs[k][0], v_refs[k][0])
            for kind, val in enumerate((g, delta, m2, v2)):
                outs[3 * kind + k][0] = val

    full = pl.BlockSpec((1, W_ROW_BLK, D), lambda i: (0, 0, 0))
    out = jax.ShapeDtypeStruct((1, W_ROW_BLK, D), F32)
    res = pl.pallas_call(
        body, name="adamw_3", grid=(1,),
        in_specs=[pl.BlockSpec((1, 3, W_ROW_BLK, D), lambda i: (0, 0, 0, 0)),
                  pl.BlockSpec((3, 3, W_ROW_BLK, D), lambda i: (0, 0, 0, 0))] + [full] * 9,
        out_specs=(full,) * 12, out_shape=(out,) * 12,
    )(chip1, far1, *ws, *ms, *vs)
    return tuple(res[3 * kind:3 * kind + 3] for kind in range(4))


N_SMALL = 9


def _small_update(pack_all, srs_all, ws, ms, vs):
    def body(pk_ref, sr_ref, *refs):
        w_refs, m_refs, v_refs = refs[0:N_SMALL], refs[N_SMALL:2 * N_SMALL], refs[2 * N_SMALL:3 * N_SMALL]
        loss_ref = refs[3 * N_SMALL]
        outs = refs[3 * N_SMALL + 1:7 * N_SMALL + 1]
        tot_sc, tots_sc = refs[7 * N_SMALL + 1:]
        tot = pk_ref[0]
        tot_s = sr_ref[0]
        for d in range(1, N_DEV):
            tot = tot + pk_ref[d]
            tot_s = tot_s + sr_ref[d]
        tot_sc[...] = tot
        tots_sc[...] = tot_s
        loss_ref[...] = jnp.sum(tot_sc[5:6, :], axis=1, keepdims=True)
        lbl = w_refs[4]
        p0 = _sigmoid(lbl[0:1, :] - lbl[1:2, :])
        d_l0 = tot_sc[4:5, :] * p0 * (1.0 - p0)

        def update(k, sel, g):
            delta, m2, v2 = _adamw(w_refs[k][sel], g, m_refs[k][sel], v_refs[k][sel])
            for kind, val in enumerate((g, delta, m2, v2)):
                outs[N_SMALL * kind + k][sel] = val

        everything = (slice(None), slice(None))
        for k, row in ((0, 0), (1, 1), (2, 2), (3, 3), (5, 6), (6, 7)):
            update(k, everything, tot_sc[row:row + 1, :])
        update(4, (slice(0, 1), slice(None)), d_l0)
        update(4, (slice(1, 2), slice(None)), -d_l0)
        update(7, (0, slice(None), slice(None)), tots_sc[0:CONV_K, :])
        update(8, everything, tots_sc[META_ROW:META_ROW + N_META, :])

    shapes = [jax.ShapeDtypeStruct(w.shape, F32) for w in ws]
    res = pl.pallas_call(
        body, name="small_update",
        out_shape=(jax.ShapeDtypeStruct((1, 1), F32), *(shapes * 4)),
        scratch_shapes=[pltpu.VMEM((8, D), F32), pltpu.VMEM((SMALL_ROWS, HEAD_W), F32)],
    )(pack_all, srs_all, *ws, *ms, *vs)
    return res[0], tuple(res[1 + N_SMALL * kind:1 + N_SMALL * (kind + 1)] for kind in range(4))


def _local_step(xin, proj, target, conv_w_full, conv_b, ln_g, ln_b, w3_b, lb_logits, gnorm_g, final_g, ids_mine):
    fg = final_g.reshape(1, D)
    c0, o, s_start, w3_full = _rec_conv_fwd(proj, lb_logits, conv_w_full, conv_b, w3_b)
    d_o, d_c0, d_z, dproj, a3, b3, red = _mid(xin, target, o, c0, proj, w3_full, ln_g, ln_b, gnorm_g, fg)
    p3 = _wgrad3(a3, b3)
    dproj, dlb, d_conv_w, land1 = _rec_conv_bwd(proj, lb_logits, d_o, s_start, d_c0, d_z, conv_w_full, dproj, p3)
    chip1, chip1b = _chip_sum_3(p3, land1, ids_mine)
    return dproj, b3, p3, chip1, chip1b, d_conv_w, red, dlb


def kernel(x, meta_tokens, norm_g, w_in, conv_w, conv_b, ln_g, ln_b, w_conv_out, lb_logits, gnorm_g, w_rec_out, w_out, final_g, loss_target, m_meta_tokens, m_norm_g, m_w_in, m_conv_w, m_conv_b, m_ln_g, m_ln_b, m_w_conv_out, m_lb_logits, m_gnorm_g, m_w_rec_out, m_w_out, m_final_g, v_meta_tokens, v_norm_g, v_w_in, v_conv_w, v_conv_b, v_ln_g, v_ln_b, v_w_conv_out, v_lb_logits, v_gnorm_g, v_w_rec_out, v_w_out, v_final_g):
    mx, my, mc = _my_place()

    ws_s = jnp.concatenate([conv_w[0], jnp.zeros((1, HEAD_W), F32), meta_tokens], axis=0)
    small_full = jnp.transpose(_gather_small(ws_s), (1, 0, 2)).reshape(SMALL_ROWS, D)
    conv_w_full = small_full[0:CONV_K]
    meta_full = small_full[META_ROW:META_ROW + N_META]
    w_in_b, w3_b = _cast_shards(w_in[0], w_conv_out, w_rec_out, w_out)
    first, second, diag = _gather_chips(mx, my, mc)
    use_order = [(mx, my, mc), (mx, my, 1 - mc), (*first, mc), (*second, 1 - mc), (*second, mc), (*first, 1 - mc),
                 (*diag, mc), (*diag, 1 - mc)]
    order = jnp.stack([_dev_index(*p) for p in use_order]).astype(jnp.int32)
    proj, xin, h, w_in_full = _gather_and_proj(x[0], meta_full, norm_g, w_in_b, order)
    h = h.reshape(TP, D)

    ids_mine = jnp.stack([_dev_index(*_chip_rel(mx, my, r), mc) for r in range(4)]).astype(jnp.int32)
    ids_sib = jnp.stack([_dev_index(*_chip_rel(mx, my, r), 1 - mc) for r in range(4)]).astype(jnp.int32)
    dproj, b3, _, chip1, chip1b, d_conv_w, red, dlb = _local_step(
        xin, proj, loss_target[0], conv_w_full, conv_b, ln_g, ln_b, w3_b, lb_logits, gnorm_g, final_g, ids_mine)

    chip0, chip0b, _, far1 = _wgrad_in(h, dproj, jnp.concatenate([ids_sib, ids_mine]), chip1b)
    d_xin, dng, far0 = _dh_and_norm_bwd(dproj, w_in_full, xin, b3, norm_g, chip0b)
    pack = jnp.concatenate([dng[0:1], red[4:5], red[2:3], red[3:4], dlb[0:1], red[5:6], red[1:2], red[0:1]], axis=0)
    g_in, d_in, m_in, v_in = _sum_adamw(chip0.reshape(4 * D, W_IN_BLK), far0.reshape(3 * D, W_IN_BLK), w_in[0],
                                        m_w_in[0], v_w_in[0], 256, "adamw_in")
    big3 = _adamw_3(chip1, far1, (w_conv_out, w_rec_out, w_out), (m_w_conv_out, m_w_rec_out, m_w_out),
                    (v_w_conv_out, v_w_rec_out, v_w_out))

    srs = jnp.concatenate([d_conv_w, d_xin[PAD_FRONT:ROW0]], axis=0)
    srs = jnp.transpose(srs.reshape(SMALL_ROWS, N_DEV, HEAD_W), (1, 0, 2))
    pack_all, srs_all = _exchange_small(pack, srs)
    loss, small = _small_update(
        pack_all, srs_all,
        (norm_g, conv_b, ln_g, ln_b, lb_logits, gnorm_g, final_g.reshape(1, D), conv_w, meta_tokens),
        (m_norm_g, m_conv_b, m_ln_g, m_ln_b, m_lb_logits, m_gnorm_g, m_final_g.reshape(1, D), m_conv_w, m_meta_tokens),
        (v_norm_g, v_conv_b, v_ln_g, v_ln_b, v_lb_logits, v_gnorm_g, v_final_g.reshape(1, D), v_conv_w, v_meta_tokens))

    outs = [loss.reshape(()), d_xin[ROW0:][None]]
    for kind, a_in in enumerate((g_in, d_in, m_in, v_in)):
        ng, cb, lg, lb_, lbl, gg, fg, cw, mt = small[kind]
        a_3 = big3[kind]
        outs += [mt, ng, a_in[None], cw, cb, lg, lb_, a_3[0], lbl, gg, a_3[1], a_3[2], fg.reshape(D)]
    return tuple(outs)
```

```python
import jax
import jax.numpy as jnp
from jax import lax
from jax.experimental import pallas as pl
from jax.experimental.pallas import tpu as pltpu

F32 = jnp.float32
BF16 = jnp.bfloat16
ACT = BF16

D = 1024
SEQ = 4096
N_META = 16
CHUNK = 64
PAD_FRONT = 48
ROW0 = PAD_FRONT + N_META
TP = ROW0 + SEQ
N_CHUNK = TP // CHUNK
HEADS = 8
HEAD_W = 128
D_IN = 9 * D
N_DEV = 8
W_IN_BLK = D_IN // N_DEV
W_ROW_BLK = D // N_DEV
CONV_K = 31
SMALL_ROWS = 48
META_ROW = 32
EPS = 1e-6
HALO = 32

TM_MAT = 1040
TT_WGRAD = 2080
DH_K_BLKS = 2
TM_ELT = 208
CHUNKS_PER_STEP = 5
CONV_STRIPS = 5

ADAM_LR = 0.001
ADAM_B1 = 0.9
ADAM_B2 = 0.999
ADAM_EPS = 1e-08
ADAM_WD = 0.01
ADAM_STEP = 10

MESH_ID = pl.DeviceIdType.MESH
ANY = pl.BlockSpec(memory_space=pl.ANY)


def _sigmoid(v):
    return jax.nn.sigmoid(v)


def _dsilu(silu, s):
    return s + silu * (1.0 - s)


def _dot(a, b):
    return jnp.dot(a, b, preferred_element_type=F32)


def _dot_nt(a, b):
    return lax.dot_general(a, b, (((1,), (1,)), ((), ())), preferred_element_type=F32)


def _dot_tn(a, b):
    return lax.dot_general(a, b, (((0,), (0,)), ((), ())), preferred_element_type=F32)


def _tri_matmul(tri, v):
    hi = v.astype(BF16)
    lo = (v - hi.astype(F32)).astype(BF16)
    return _dot(tri, hi) + _dot(tri, lo)


def _adamw(w, g, m, v):
    m2 = ADAM_B1 * m + (1.0 - ADAM_B1) * g
    v2 = ADAM_B2 * v + (1.0 - ADAM_B2) * jnp.square(g)
    m_hat = m2 / (1.0 - ADAM_B1 ** ADAM_STEP)
    v_hat = v2 / (1.0 - ADAM_B2 ** ADAM_STEP)
    delta = -ADAM_LR * (m_hat / (jnp.sqrt(v_hat) + ADAM_EPS) + ADAM_WD * w)
    return delta, m2, v2


def _window_start(i, tm):
    assert tm % 16 == 0 and ROW0 % 16 == 0
    return pl.multiple_of(16 * jnp.maximum((tm // 16) * i - ROW0 // 16, 0), 16)


def _my_place():
    return lax.axis_index("x"), lax.axis_index("y"), lax.axis_index("c")


def _dev_index(px, py, pc):
    return 4 * px + 2 * py + pc


def _cast_shards(w_in_s, w_conv_s, w_rec_s, w_out_s):
    def body(a_ref, c_ref, r_ref, o_ref, oa_ref, ob_ref):
        oa_ref[...] = a_ref[...].astype(BF16)
        for k, ref in enumerate((c_ref, r_ref, o_ref)):
            ob_ref[k] = ref[0].astype(BF16)

    return pl.pallas_call(
        body, name="cast_shards",
        out_shape=(jax.ShapeDtypeStruct(w_in_s.shape, BF16), jax.ShapeDtypeStruct((3, W_ROW_BLK, D), BF16)),
    )(w_in_s, w_conv_s, w_rec_s, w_out_s)


def _peer(x, y, c, r):
    return (jnp.bitwise_xor(x, (r >> 2) & 1), jnp.bitwise_xor(y, (r >> 1) & 1), jnp.bitwise_xor(c, r & 1))


def _gather_small(small_s):
    def body(s_ref, o_ref, send_sems, recv_sems, local_sem):
        x, y, c = _my_place()
        my_id = _dev_index(x, y, c)
        mine = pltpu.make_async_copy(s_ref, o_ref.at[my_id], local_sem)
        mine.start()
        copies = []
        for r in range(1, N_DEV):
            cp = pltpu.make_async_remote_copy(
                src_ref=s_ref, dst_ref=o_ref.at[my_id], send_sem=send_sems.at[r - 1], recv_sem=recv_sems.at[r - 1],
                device_id=_peer(x, y, c, r), device_id_type=MESH_ID)
            cp.start()
            copies.append(cp)
        for cp in copies:
            cp.wait_recv()
        for cp in copies:
            cp.wait_send()
        mine.wait()

    return pl.pallas_call(
        body, name="gather_small", out_shape=jax.ShapeDtypeStruct((N_DEV,) + small_s.shape, F32),
        in_specs=[ANY], out_specs=ANY,
        scratch_shapes=[pltpu.SemaphoreType.DMA((7,)), pltpu.SemaphoreType.DMA((7,)), pltpu.SemaphoreType.DMA],
    )(small_s)


def _w3_gather(src, out, stage, send_sems, recv_sems, local_sems):
    x, y, c = _my_place()
    me, sibling = (x, y, c), (x, y, 1 - c)
    chips = [(1 - x, y), (x, 1 - y), (1 - x, 1 - y)]

    def block(place):
        d = _dev_index(*place)
        return out.at[:, pl.ds(pl.multiple_of(d * W_ROW_BLK, W_ROW_BLK), W_ROW_BLK), :]

    def copy(k, place, to, from_src=False):
        return pltpu.make_async_remote_copy(
            src_ref=src if from_src else block(place), dst_ref=block(place),
            send_sem=send_sems.at[k], recv_sem=recv_sems.at[k], device_id=to, device_id_type=MESH_ID)

    own_in = pltpu.make_async_copy(src, stage, local_sems.at[0])
    own_out = pltpu.make_async_copy(stage, block(me), local_sems.at[1])

    def start():
        copy(0, me, sibling, from_src=True).start()
        for j, chip in enumerate(chips):
            copy(1 + j, me, (*chip, c), from_src=True).start()
        own_in.start()
        own_in.wait()
        own_out.start()

    def finish():
        for j, chip in enumerate(chips):
            copy(1 + j, (*chip, c), me).wait_recv()
            copy(4 + j, (*chip, c), sibling).start()
        copy(0, sibling, me).wait_recv()
        for j, chip in enumerate(chips):
            copy(4 + j, (*chip, 1 - c), me).wait_recv()
        for k in range(7):
            copy(k, me, me).wait_send()
        own_out.wait()

    return start, finish


def _p3_to_sibling(p3_ref, land_ref, send_sems, recv_sems):
    x, y, c = _my_place()

    def cp(q):
        d = _dev_index(*_chip_rel(x, y, q), 1 - c)
        return pltpu.make_async_remote_copy(
            src_ref=p3_ref.at[:, pl.ds(pl.multiple_of(d * W_ROW_BLK, W_ROW_BLK), W_ROW_BLK), :],
            dst_ref=land_ref.at[q], send_sem=send_sems.at[q], recv_sem=recv_sems.at[q],
            device_id=(x, y, 1 - c), device_id_type=MESH_ID)

    def start():
        for q in range(4):
            cp(q).start()

    def finish():
        for q in range(4):
            cp(q).wait_recv()
        for q in range(4):
            cp(q).wait_send()

    return start, finish


def _partials_to_owners(src_ref, far_ref, send_sems, recv_sems):
    x, y, c = _my_place()

    def cp(q):
        return pltpu.make_async_remote_copy(
            src_ref=src_ref.at[q], dst_ref=far_ref.at[q - 1], send_sem=send_sems.at[q - 1],
            recv_sem=recv_sems.at[q - 1], device_id=(*_chip_rel(x, y, q), c), device_id_type=MESH_ID)

    def start():
        for q in range(1, 4):
            cp(q).start()

    def finish():
        for q in range(1, 4):
            cp(q).wait_recv()
        for q in range(1, 4):
            cp(q).wait_send()

    return start, finish


def _gather_chips(x, y, c):
    first = (jnp.bitwise_xor(x, 1 - c), jnp.bitwise_xor(y, c))
    second = (jnp.bitwise_xor(x, c), jnp.bitwise_xor(y, 1 - c))
    return [first, second, (1 - x, 1 - y)]


def _gather_and_proj(x_seq, meta_full, norm_g, w_in_b, order):
    tm = TM_MAT
    n_m = TP // tm
    last_m = n_m - 1

    def body(order_ref, x_ref, meta_ref, g_ref, s0, proj_ref, xin_ref, h_out, o0, hbuf, wbuf, send_sems, recv_sems,
             local_sems):
        del order_ref
        n = pl.program_id(0)
        m = pl.program_id(1)
        x, y, c = _my_place()
        me, sibling = (x, y, c), (x, y, 1 - c)
        chips = _gather_chips(x, y, c)

        def block(place):
            return o0.at[_dev_index(*place)]

        def copy(k, place, to, from_src=False):
            return pltpu.make_async_remote_copy(
                src_ref=s0 if from_src else block(place), dst_ref=block(place),
                send_sem=send_sems.at[k], recv_sem=recv_sems.at[k], device_id=to, device_id_type=MESH_ID)

        def to_vmem(place, slot):
            return pltpu.make_async_copy(block(place), wbuf.at[slot], local_sems.at[slot])

        own_out = pltpu.make_async_copy(wbuf.at[0], block(me), local_sems.at[2])
        h_copy = pltpu.make_async_copy(hbuf, h_out, local_sems.at[3])

        @pl.when((n == 0) & (m == 0))
        def _():
            copy(0, me, sibling, from_src=True).start()
            for j, chip in enumerate(chips[0:2]):
                copy(1 + j, me, (*chip, c), from_src=True).start()
            mine = pltpu.make_async_copy(s0, wbuf.at[0], local_sems.at[0])
            mine.start()
            mine.wait()
            own_out.start()

        @pl.when(n == 0)
        def _():
            xv = x_ref[...]
            xin_ref[...] = jnp.where(m == 0, pltpu.roll(xv, ROW0, 0), xv)

            @pl.when(m == 0)
            def _():
                xin_ref[0:PAD_FRONT, :] = jnp.zeros((PAD_FRONT, D), F32)
                xin_ref[PAD_FRONT:ROW0, :] = meta_ref[...]

            xv = xin_ref[...]
            r = lax.rsqrt(jnp.mean(xv * xv, axis=-1, keepdims=True) + EPS)
            hbuf[m] = (xv * r * g_ref[...]).astype(BF16)

        between = [4 + c, 5 - c, 6]
        first, second, diag = chips
        plan = [(sibling, (0, sibling), None),
                ((*first, c), (1, (*first, c)), between[0]),
                ((*second, 1 - c), (between[1], (*second, 1 - c)), None),
                ((*second, c), (2, (*second, c)), between[1]),
                ((*first, 1 - c), (between[0], (*first, 1 - c)), None),
                ((*diag, c), (3, (*diag, c)), between[2]),
                ((*diag, 1 - c), (between[2], (*diag, 1 - c)), None)]

        for s, (place, (k, origin), pass_on) in enumerate(plan, start=1):
            @pl.when((n == s - 1) & (m == last_m))
            def _(s=s, place=place, k=k, origin=origin, pass_on=pass_on):
                copy(k, origin, me).wait_recv()
                if pass_on is not None:
                    copy(pass_on, place, sibling).start()
                if s == 2:
                    copy(3, place, (*chips[1], c)).start()
                    own_out.wait()
                to_vmem(place, s % 2).start()

            @pl.when((n == s) & (m == 0))
            def _(s=s, place=place):
                to_vmem(place, s % 2).wait()

        proj_ref[...] = _dot(hbuf[m], wbuf[lax.rem(n, 2)]).astype(BF16)

        @pl.when((n == 0) & (m == last_m))
        def _():
            h_copy.start()

        @pl.when((n == N_DEV - 1) & (m == last_m))
        def _():
            for k in range(7):
                copy(k, me, me).wait_send()
            h_copy.wait()

    return pl.pallas_call(
        body, name="gather_and_proj",
        grid_spec=pltpu.PrefetchScalarGridSpec(
            num_scalar_prefetch=1, grid=(N_DEV, n_m),
            in_specs=[pl.BlockSpec((pl.Element(tm), pl.Element(D)),
                                   lambda n, m, o: (_window_start(jnp.where(n == 0, m, 0), tm), 0)),
                      pl.BlockSpec((N_META, D), lambda n, m, o: (0, 0)),
                      pl.BlockSpec((1, D), lambda n, m, o: (0, 0)), ANY],
            out_specs=(pl.BlockSpec((tm, W_IN_BLK), lambda n, m, o: (m, o[n])),
                       pl.BlockSpec((tm, D), lambda n, m, o: (jnp.where(n == 0, m, last_m), 0)), ANY, ANY),
            scratch_shapes=[pltpu.VMEM((n_m, tm, D), BF16), pltpu.VMEM((2, D, W_IN_BLK), BF16),
                            pltpu.SemaphoreType.DMA((7,)), pltpu.SemaphoreType.DMA((7,)),
                            pltpu.SemaphoreType.DMA((4,))]),
        out_shape=(jax.ShapeDtypeStruct((TP, D_IN), BF16), jax.ShapeDtypeStruct((TP, D), F32),
                   jax.ShapeDtypeStruct((n_m, tm, D), BF16), jax.ShapeDtypeStruct((N_DEV, D, W_IN_BLK), BF16)),
        compiler_params=pltpu.CompilerParams(dimension_semantics=("arbitrary", "arbitrary")),
    )(order, x_seq, meta_full, norm_g, w_in_b)


def _chip_rel(x, y, r):
    return (jnp.bitwise_xor(x, r >> 1), jnp.bitwise_xor(y, r & 1))


def _exchange_small(pack, srs):
    def body(pk, sr, pk_all, sr_all, send_sems, recv_sems, local_sems):
        x, y, c = _my_place()
        my_id = _dev_index(x, y, c)
        mine = [pltpu.make_async_copy(pk, pk_all.at[my_id], local_sems.at[0]),
                pltpu.make_async_copy(sr.at[my_id], sr_all.at[my_id], local_sems.at[1])]
        for cp in mine:
            cp.start()
        copies = []
        for r in range(1, N_DEV):
            peer = (jnp.bitwise_xor(x, (r >> 2) & 1), jnp.bitwise_xor(y, (r >> 1) & 1), jnp.bitwise_xor(c, r & 1))
            peer_id = _dev_index(*peer)
            for a, (src, dst) in enumerate(((pk, pk_all.at[my_id]), (sr.at[peer_id], sr_all.at[my_id]))):
                cp = pltpu.make_async_remote_copy(
                    src_ref=src, dst_ref=dst, send_sem=send_sems.at[a * 7 + r - 1], recv_sem=recv_sems.at[a * 7 + r - 1],
                    device_id=peer, device_id_type=MESH_ID)
                cp.start()
                copies.append(cp)
        for cp in copies:
            cp.wait_recv()
        for cp in copies:
            cp.wait_send()
        for cp in mine:
            cp.wait()

    return pl.pallas_call(
        body, name="exchange_small",
        out_shape=(jax.ShapeDtypeStruct((N_DEV,) + pack.shape, F32), jax.ShapeDtypeStruct(srs.shape, F32)),
        in_specs=[ANY, ANY], out_specs=(ANY, ANY),
        scratch_shapes=[pltpu.SemaphoreType.DMA((14,)), pltpu.SemaphoreType.DMA((14,)), pltpu.SemaphoreType.DMA((2,))],
    )(pack, srs)


N_CB = D // HEAD_W


def _store_by_cb(ref, idx, rows, val):
    for cb in range(N_CB):
        ref[(*idx, cb, rows, slice(None))] = val[:, cb * HEAD_W:(cb + 1) * HEAD_W]


def _fill_shifts(sh, tm):
    n = tm + HALO - 8
    for s in range(1, 8):
        for cb in range(N_CB):
            sh[s, cb, 0:n, :] = sh[0, cb, s:s + n, :]


def _gates(p_ref, lbl_ref, chunk, bsc):
    lb = _sigmoid(lbl_ref[0:1, :] - lbl_ref[1:2, :])
    q_raw = p_ref[:, 0:D].astype(F32)
    f_raw = p_ref[:, D:2 * D].astype(F32)
    sq = _sigmoid(q_raw)
    q = q_raw * sq
    sg = _sigmoid(f_raw)
    f = lb + (1.0 - lb) * sg
    row = lax.broadcasted_iota(jnp.int32, (CHUNK, 1), 0) + chunk * CHUNK
    valid = row >= PAD_FRONT
    lf = jnp.where(valid, jnp.log(f), 0.0)
    k = jnp.where(valid, 1.0 - f, 0.0)
    r_i = lax.broadcasted_iota(jnp.int32, (CHUNK, CHUNK), 0)
    c_i = lax.broadcasted_iota(jnp.int32, (CHUNK, CHUNK), 1)
    causal = r_i >= c_i
    bsc[...] = _tri_matmul(causal.astype(BF16), lf)
    b = bsc[...]
    b_mid = bsc[CHUNK // 2 - 1:CHUNK // 2, :]
    b_last = bsc[CHUNK - 1:CHUNK, :]
    e_q = jnp.exp(b)
    e_qm = jnp.exp(b - b_mid)
    e_km = jnp.exp(b_mid - b)
    e_kh = jnp.exp(b_last - b)
    e_last = jnp.exp(b_last)
    return dict(lb=lb, q_raw=q_raw, sq=sq, q=q, sg=sg, f=f, k=k, valid=valid, causal=causal,
                e_q=e_q, e_qm=e_qm, e_km=e_km, e_kh=e_kh, e_last=e_last)


def _rec_conv_fwd(proj, lb_logits, conv_w, conv_b, w3_b):
    cps = CHUNKS_PER_STEP
    tm = cps * CHUNK
    n_strip = CONV_STRIPS
    strip = tm // n_strip

    def body(p_ref, lbl_ref, pg_ref, w_ref, b_ref, w3s_ref, c0_ref, o_ref, s_ref, w3o_ref,
             st, bsc, sh, c0_sc, w3buf, send_sems, recv_sems, local_sems):
        n = pl.program_id(0)
        gather_start, gather_finish = _w3_gather(w3s_ref, w3o_ref, w3buf, send_sems, recv_sems, local_sems)

        @pl.when(n == 0)
        def _():
            st[...] = jnp.zeros_like(st)
            sh[0, :, 0:HALO, :] = jnp.zeros((N_CB, HALO, HEAD_W), F32)
            gather_start()

        @pl.when(n > 0)
        def _():
            sh[0, :, 0:HALO, :] = sh[0, :, tm:tm + HALO, :]

        ga = pg_ref[:, 0:D].astype(F32)
        gb = pg_ref[:, D:2 * D].astype(F32)
        _store_by_cb(sh, (0,), slice(HALO, HALO + tm), ga * _sigmoid(gb))
        _fill_shifts(sh, tm)

        def conv_unit(cb, s_i):
            cs = slice(cb * HEAD_W, (cb + 1) * HEAD_W)
            acc = jnp.broadcast_to(b_ref[:, cs], (strip, HEAD_W))
            for j in range(CONV_K):
                off = HALO - (CONV_K - 1) + j
                lo = s_i * strip + 8 * (off // 8)
                acc = acc + w_ref[j:j + 1, cs] * sh[off % 8, cb, lo:lo + strip, :]
            c0_sc[s_i * strip:(s_i + 1) * strip, cs] = acc

        units = [(cb, s_i) for cb in range(N_CB) for s_i in range(n_strip)]

        def prep(ci):
            g = _gates(p_ref.at[pl.ds(ci * CHUNK, CHUNK)], lbl_ref, n * cps + ci, bsc.at[ci])
            g["q1"] = (g["q"] * g["e_q"]).astype(BF16)
            g["qm"] = (g["q"] * g["e_qm"]).astype(BF16)
            g["km"] = (g["k"] * g["e_km"]).astype(BF16)
            g["kh"] = (g["k"] * g["e_kh"]).astype(BF16)
            return g

        def heads(ci, g):
            rs = pl.ds(ci * CHUNK, CHUNK)
            pv = p_ref.at[rs]
            s_ref[ci] = st[...]
            for h in range(HEADS):
                if units:
                    conv_unit(*units.pop(0))
                sl = slice(h * HEAD_W, (h + 1) * HEAD_W)
                v = pv[:, 2 * D + h * HEAD_W:2 * D + (h + 1) * HEAD_W]
                att = jnp.where(g["causal"], _dot_nt(g["qm"][:, sl], g["km"][:, sl]), 0.0).astype(BF16)
                s_h = st[h]
                o_ref[rs, sl] = (_dot_nt(g["q1"][:, sl], s_h.astype(BF16)) + _dot(att, v)).astype(ACT)
                st[h] = s_h * g["e_last"][:, sl] + _dot_tn(v, g["kh"][:, sl])

        ready = prep(0)
        for ci in range(cps):
            coming = prep(ci + 1) if ci + 1 < cps else None
            heads(ci, ready)
            ready = coming
        while units:
            conv_unit(*units.pop(0))
        c0_ref[...] = c0_sc[...].astype(ACT)

        @pl.when(n == N_CHUNK // cps - 1)
        def _():
            gather_finish()

    def rows_of(width, col):
        return pl.BlockSpec((tm, width), lambda n: (n, col))

    return pl.pallas_call(
        body, name="rec_conv_fwd", grid=(N_CHUNK // cps,),
        in_specs=[rows_of(3 * D, 1), pl.BlockSpec((2, D), lambda n: (0, 0)), rows_of(2 * D, 0),
                  pl.BlockSpec((CONV_K, D), lambda n: (0, 0)), pl.BlockSpec((1, D), lambda n: (0, 0)), ANY],
        out_specs=(rows_of(D, 0), rows_of(D, 0), pl.BlockSpec((cps, HEADS, HEAD_W, HEAD_W), lambda n: (n, 0, 0, 0)), ANY),
        out_shape=(jax.ShapeDtypeStruct((TP, D), ACT), jax.ShapeDtypeStruct((TP, D), ACT),
                   jax.ShapeDtypeStruct((N_CHUNK, HEADS, HEAD_W, HEAD_W), F32), jax.ShapeDtypeStruct((3, D, D), BF16)),
        scratch_shapes=[pltpu.VMEM((HEADS, HEAD_W, HEAD_W), F32), pltpu.VMEM((cps, CHUNK, D), F32),
                        pltpu.VMEM((8, N_CB, HALO + tm, HEAD_W), F32), pltpu.VMEM((tm, D), F32),
                        pltpu.VMEM((3, W_ROW_BLK, D), BF16), pltpu.SemaphoreType.DMA((7,)),
                        pltpu.SemaphoreType.DMA((7,)), pltpu.SemaphoreType.DMA((2,))],
        compiler_params=pltpu.CompilerParams(dimension_semantics=("arbitrary",)),
    )(proj, lb_logits, proj, conv_w, conv_b, w3_b)


def _mid(xin, tgt, o, c0, proj, w3, ln_g, ln_b, gnorm_g, final_g):
    tm = TM_ELT

    def body(x_ref, t_ref, o_ref, c0_ref, z_ref, gr_ref, mc_ref, mr_ref, w_ref, lng_ref, lnb_ref, gng_ref, fg_ref,
             do_ref, dc0_ref, dz_ref, dp_ref, a3_ref, b3_ref, red_ref, on_sc, don_sc):
        i = pl.program_id(0)

        @pl.when(i == 0)
        def _():
            red_ref[...] = jnp.zeros_like(red_ref)

        w_conv, w_rec, w_out = w_ref[0], w_ref[1], w_ref[2]
        c0v = c0_ref[...].astype(F32)
        mu = jnp.mean(c0v, axis=-1, keepdims=True)
        xc = c0v - mu
        rstd = lax.rsqrt(jnp.mean(xc * xc, axis=-1, keepdims=True) + EPS)
        xh = xc * rstd
        c1 = xh * lng_ref[...] + lnb_ref[...]
        s1 = _sigmoid(c1)
        c2 = c1 * s1
        z = z_ref[...].astype(F32)
        sz = _sigmoid(z)
        silu_z = z * sz
        u_conv = (c2 * silu_z).astype(BF16)
        y_conv = _dot(u_conv, w_conv)
        ov = o_ref[...].astype(F32)
        r3 = []
        for h in range(HEADS):
            sl = slice(h * HEAD_W, (h + 1) * HEAD_W)
            oh = ov[:, sl]
            r_h = lax.rsqrt(jnp.mean(oh * oh, axis=-1, keepdims=True) + EPS)
            r3.append(r_h)
            on_sc[:, sl] = oh * r_h
        o_n = on_sc[...]
        o_g = o_n * gng_ref[...]
        gr = gr_ref[...].astype(F32)
        sgr = _sigmoid(gr)
        silu_g = gr * sgr
        u_rec = (o_g * silu_g).astype(BF16)
        y_rec = _dot(u_rec, w_rec)
        mc = mc_ref[...].astype(F32)
        mr = mr_ref[...].astype(F32)
        smc = _sigmoid(mc)
        smr = _sigmoid(mr)
        merged = (smc * y_conv + smr * y_rec).astype(BF16)
        res = x_ref[...] + _dot(merged, w_out)
        r2 = lax.rsqrt(jnp.mean(res * res, axis=-1, keepdims=True) + EPS)
        xh2 = res * r2
        row = lax.broadcasted_iota(jnp.int32, (tm, 1), 0) + i * tm
        real = row >= ROW0
        tgt = t_ref[...]
        tgt = jnp.where(i == 0, pltpu.roll(tgt, ROW0, 0), tgt)
        diff = jnp.where(real, xh2 * fg_ref[...] - tgt, 0.0)
        d_y = diff * (1.0 / D)
        d_xh2 = d_y * fg_ref[...]
        d_res = r2 * (d_xh2 - xh2 * jnp.mean(d_xh2 * xh2, axis=-1, keepdims=True))
        d_res_b = d_res.astype(BF16)
        d_merged = _dot_nt(d_res_b, w_out)
        d_yc_f = d_merged * smc
        d_yr_f = d_merged * smr
        d_yc = d_yc_f.astype(BF16)
        d_yr = d_yr_f.astype(BF16)
        dp_ref[:, D:2 * D] = (d_yc_f * y_conv * (1.0 - smc)).astype(BF16)
        dp_ref[:, 2 * D:3 * D] = (d_yr_f * y_rec * (1.0 - smr)).astype(BF16)
        d_ur = _dot_nt(d_yr, w_rec)
        d_og = d_ur * silu_g
        dp_ref[:, 0:D] = (d_ur * o_g * _dsilu(silu_g, sgr)).astype(BF16)
        d_on = d_og * gng_ref[...]
        for h in range(HEADS):
            sl = slice(h * HEAD_W, (h + 1) * HEAD_W)
            d_h = d_on[:, sl]
            n_h = o_n[:, sl]
            don_sc[:, sl] = r3[h] * (d_h - n_h * jnp.mean(d_h * n_h, axis=-1, keepdims=True))
        do_ref[...] = don_sc[...].astype(ACT)
        d_uc = _dot_nt(d_yc, w_conv)
        d_c2 = d_uc * silu_z
        dz_ref[...] = (d_uc * c2 * _dsilu(silu_z, sz)).astype(BF16)
        d_c1 = d_c2 * _dsilu(c2, s1)
        d_xh = d_c1 * lng_ref[...]
        d_c0 = rstd * (d_xh - jnp.mean(d_xh, axis=-1, keepdims=True)
                       - xh * jnp.mean(d_xh * xh, axis=-1, keepdims=True))
        dc0_ref[...] = d_c0.astype(ACT)
        a3_ref[0] = u_conv
        b3_ref[0] = d_yc
        a3_ref[1] = u_rec
        b3_ref[1] = d_yr
        a3_ref[2] = merged
        b3_ref[2] = d_res_b
        def colsum(vv):
            return jnp.sum(vv, axis=0, keepdims=True)

        red_ref[0:1, :] += colsum(d_y * xh2)
        red_ref[1:2, :] += colsum(d_og * o_n)
        red_ref[2:3, :] += colsum(d_c1 * xh)
        red_ref[3:4, :] += colsum(d_c1)
        red_ref[4:5, :] += colsum(d_c0)
        red_ref[5:6, :] += colsum(diff * diff) * (0.5 / D)

    def row_block(width, col):
        return pl.BlockSpec((tm, width), lambda i: (i, col))

    def const_block(shape):
        return pl.BlockSpec(shape, lambda i: (0,) * len(shape))

    stack = jax.ShapeDtypeStruct((3, TP, D), BF16)
    stack_spec = pl.BlockSpec((3, tm, D), lambda i: (0, i, 0))
    return pl.pallas_call(
        body, name="mid", grid=(TP // tm,),
        in_specs=[row_block(D, 0),
                  pl.BlockSpec((pl.Element(tm), pl.Element(D)), lambda i: (_window_start(i, tm), 0)),
                  row_block(D, 0), row_block(D, 0),
                  row_block(D, 2), row_block(D, 6), row_block(D, 7), row_block(D, 8),
                  pl.BlockSpec((3, D, D), lambda i: (0, 0, 0), pipeline_mode=pl.Buffered(1)),
                  const_block((1, D)), const_block((1, D)), const_block((1, D)), const_block((1, D))],
        out_specs=(row_block(D, 0), row_block(D, 0), row_block(D, 0), row_block(3 * D, 2),
                   stack_spec, stack_spec, const_block((8, D))),
        out_shape=(jax.ShapeDtypeStruct((TP, D), ACT), jax.ShapeDtypeStruct((TP, D), ACT),
                   jax.ShapeDtypeStruct((TP, D), BF16),
                   jax.ShapeDtypeStruct((TP, D_IN), BF16), stack, stack, jax.ShapeDtypeStruct((8, D), F32)),
        scratch_shapes=[pltpu.VMEM((tm, D), F32), pltpu.VMEM((tm, D), F32)],
        compiler_params=pltpu.CompilerParams(dimension_semantics=("arbitrary",), vmem_limit_bytes=60 * 1024 * 1024),
    )(xin, tgt, o, c0, proj, proj, proj, proj, w3, ln_g, ln_b, gnorm_g, final_g)


def _rec_conv_bwd(proj, lb_logits, d_o, s_start, d_c0, d_z, conv_w, dproj, p3):
    cps = CHUNKS_PER_STEP
    tm = cps * CHUNK
    last = N_CHUNK // cps - 1
    n_strip = CONV_STRIPS
    strip = tm // n_strip

    def body(p_ref, lbl_ref, do_ref, s_ref, pg_ref, dc_ref, dz_ref, w_ref, dproj_in, p3_ref,
             dp_ref, dlb_ref, dw_ref, land_ref,
             dst, bsc, dq_sc, dk_sc, g_sc, dsh, a_sc, da_sc, acc, send_sems, recv_sems):
        del dproj_in
        n = pl.program_id(0)
        ride_start, ride_finish = _p3_to_sibling(p3_ref, land_ref, send_sems, recv_sems)

        @pl.when(n == 0)
        def _():
            ride_start()
            dst[...] = jnp.zeros_like(dst)
            dlb_ref[...] = jnp.zeros_like(dlb_ref)
            dsh[0, :, tm:tm + HALO, :] = jnp.zeros((N_CB, HALO, HEAD_W), F32)
            acc[...] = jnp.zeros_like(acc)

        @pl.when(n > 0)
        def _():
            dsh[0, :, tm:tm + HALO, :] = dsh[0, :, 0:HALO, :]

        _store_by_cb(dsh, (0,), slice(0, tm), dc_ref[...].astype(F32))
        _fill_shifts(dsh, tm)
        ga = pg_ref[:, 0:D].astype(F32)
        sb = _sigmoid(pg_ref[:, D:2 * D].astype(F32))
        a = ga * sb
        _store_by_cb(a_sc, (), slice(0, tm), a)

        def conv_unit(cb, st):
            cs = slice(cb * HEAD_W, (cb + 1) * HEAD_W)
            rows = slice(st * strip, (st + 1) * strip)
            a_s = a_sc[cb, rows, :]
            d_a = jnp.zeros((strip, HEAD_W), F32)
            for j in range(CONV_K):
                off = CONV_K - 1 - j
                lo = st * strip + 8 * (off // 8)
                slab = dsh[off % 8, cb, lo:lo + strip, :]
                d_a = d_a + w_ref[j:j + 1, cs] * slab
                acc[j, :, cs] += jnp.sum((a_s * slab).reshape(strip // 8, 8, HEAD_W), axis=0)
            da_sc[rows, cs] = d_a

        units = [(cb, st) for cb in range(N_CB) for st in range(n_strip)]

        def prep(ci):
            g = _gates(p_ref.at[pl.ds(ci * CHUNK, CHUNK)], lbl_ref, (last - n) * cps + ci, bsc.at[ci])
            g["q1"] = (g["q"] * g["e_q"]).astype(BF16)
            qm_f = g["q"] * g["e_qm"]
            km_f = g["k"] * g["e_km"]
            g["qm"] = qm_f.astype(BF16)
            g["km"] = km_f.astype(BF16)
            g["qm_lo"] = (qm_f - g["qm"].astype(F32)).astype(BF16)
            g["km_lo"] = (km_f - g["km"].astype(F32)).astype(BF16)
            g["kh_f"] = g["k"] * g["e_kh"]
            g["kh"] = g["kh_f"].astype(BF16)
            return g

        def heads_and_post(ci, g):
            rs = pl.ds(ci * CHUNK, CHUNK)
            pv = p_ref.at[rs]
            dpv = dp_ref.at[rs]
            q1, qm, km, qm_lo, km_lo, kh_f, kh = (g[k] for k in ("q1", "qm", "km", "qm_lo", "km_lo", "kh_f", "kh"))
            for h in range(HEADS):
                if units:
                    conv_unit(*units.pop(0))
                sl = slice(h * HEAD_W, (h + 1) * HEAD_W)
                v = pv[:, 2 * D + h * HEAD_W:2 * D + (h + 1) * HEAD_W]
                d_oh = do_ref[rs, sl].astype(BF16)
                s0 = s_ref[ci, h]
                ds_end = dst[h]
                ds_end_b = ds_end.astype(BF16)
                att = jnp.where(g["causal"], _dot_nt(qm[:, sl], km[:, sl]), 0.0).astype(BF16)
                d_att = jnp.where(g["causal"], _dot_nt(d_oh, v), 0.0).astype(BF16)
                d_v = _dot_tn(att, d_oh) + _dot_nt(kh[:, sl], ds_end_b)
                d_qm2 = _dot(d_att, jnp.concatenate([km[:, sl], km_lo[:, sl]], axis=1))
                d_qm = d_qm2[:, 0:HEAD_W] + d_qm2[:, HEAD_W:2 * HEAD_W]
                d_q1 = _dot(d_oh, s0.astype(BF16))
                d_km2 = _dot_tn(d_att, jnp.concatenate([qm[:, sl], qm_lo[:, sl]], axis=1))
                d_km = d_km2[:, 0:HEAD_W] + d_km2[:, HEAD_W:2 * HEAD_W]
                d_kh = _dot(v, ds_end_b)
                dq_sc[ci, :, sl] = d_qm * g["e_qm"][:, sl] + d_q1 * g["e_q"][:, sl]
                dk_sc[ci, :, sl] = d_km * g["e_km"][:, sl] + d_kh * g["e_kh"][:, sl]
                g_sc[ci, :, sl] = (jnp.sum(kh_f[:, sl] * d_kh, axis=0, keepdims=True)
                                   + g["e_last"][:, sl] * jnp.sum(ds_end * s0, axis=0, keepdims=True))
                dst[h] = ds_end * g["e_last"][:, sl] + _dot_tn(d_oh, q1[:, sl])
                dpv[:, 5 * D + h * HEAD_W:5 * D + (h + 1) * HEAD_W] = d_v.astype(BF16)
            d_q = dq_sc[ci]
            d_k = dk_sc[ci]
            d_b = g["q"] * d_q - g["k"] * d_k
            anti = jnp.logical_not(g["causal"]) | (lax.broadcasted_iota(jnp.int32, (CHUNK, CHUNK), 0)
                                                    == lax.broadcasted_iota(jnp.int32, (CHUNK, CHUNK), 1))
            d_lf = _tri_matmul(anti.astype(BF16), d_b) + g_sc[ci]
            d_f = jnp.where(g["valid"], d_lf / g["f"] - d_k, 0.0)
            sg = g["sg"]
            dlb_ref[0:1, :] += jnp.sum(d_f * (1.0 - sg), axis=0, keepdims=True)
            dpv[:, 3 * D:4 * D] = (d_q * _dsilu(g["q"], g["sq"])).astype(BF16)
            dpv[:, 4 * D:5 * D] = (d_f * (1.0 - g["lb"]) * sg * (1.0 - sg)).astype(BF16)

        ready = prep(cps - 1)
        for ci in reversed(range(cps)):
            coming = prep(ci - 1) if ci > 0 else None
            heads_and_post(ci, ready)
            ready = coming
        while units:
            conv_unit(*units.pop(0))

        d_a = da_sc[...]
        dp_ref[:, 0:D] = (d_a * sb).astype(BF16)
        dp_ref[:, D:2 * D] = (d_a * a * (1.0 - sb)).astype(BF16)
        dp_ref[:, 2 * D:3 * D] = dz_ref[...]

        @pl.when(n == last)
        def _():
            for j in range(CONV_K):
                dw_ref[j:j + 1, :] = jnp.sum(acc[j], axis=0, keepdims=True)
            dw_ref[CONV_K:CONV_K + 1, :] = jnp.zeros((1, D), F32)
            ride_finish()

    def rows_of(width, col):
        return pl.BlockSpec((tm, width), lambda n: (last - n, col))

    return pl.pallas_call(
        body, name="rec_conv_bwd", grid=(N_CHUNK // cps,),
        in_specs=[rows_of(3 * D, 1), pl.BlockSpec((2, D), lambda n: (0, 0)), rows_of(D, 0),
                  pl.BlockSpec((cps, HEADS, HEAD_W, HEAD_W), lambda n: (last - n, 0, 0, 0)),
                  rows_of(2 * D, 0), rows_of(D, 0), rows_of(D, 0), pl.BlockSpec((CONV_K, D), lambda n: (0, 0)), ANY, ANY],
        out_specs=(rows_of(6 * D, 0), pl.BlockSpec((8, D), lambda n: (0, 0)),
                   pl.BlockSpec((CONV_K + 1, D), lambda n: (0, 0)), ANY),
        out_shape=(jax.ShapeDtypeStruct((TP, D_IN), BF16), jax.ShapeDtypeStruct((8, D), F32),
                   jax.ShapeDtypeStruct((CONV_K + 1, D), F32), jax.ShapeDtypeStruct((4, 3, W_ROW_BLK, D), F32)),
        scratch_shapes=[pltpu.VMEM((HEADS, HEAD_W, HEAD_W), F32), pltpu.VMEM((cps, CHUNK, D), F32),
                        pltpu.VMEM((cps, CHUNK, D), F32), pltpu.VMEM((cps, CHUNK, D), F32),
                        pltpu.VMEM((cps, 1, D), F32),
                        pltpu.VMEM((8, N_CB, tm + HALO, HEAD_W), F32), pltpu.VMEM((N_CB, tm, HEAD_W), F32),
                        pltpu.VMEM((tm, D), F32), pltpu.VMEM((CONV_K, 8, D), F32),
                        pltpu.SemaphoreType.DMA((4,)), pltpu.SemaphoreType.DMA((4,))],
        input_output_aliases={8: 0},
        compiler_params=pltpu.CompilerParams(dimension_semantics=("arbitrary",)),
    )(proj, lb_logits, d_o, s_start, proj, d_c0, d_z, conv_w, dproj, p3)


def _wgrad3(a3, b3):
    tt = TT_WGRAD

    def body(a_ref, b_ref, o_ref):
        @pl.when(pl.program_id(1) == 0)
        def _():
            o_ref[...] = jnp.zeros_like(o_ref)

        o_ref[0] += _dot_tn(a_ref[0], b_ref[0])

    return pl.pallas_call(
        body, name="wgrad3", grid=(3, TP // tt),
        in_specs=[pl.BlockSpec((1, tt, D), lambda g, t: (g, t, 0)), pl.BlockSpec((1, tt, D), lambda g, t: (g, t, 0))],
        out_specs=pl.BlockSpec((1, D, D), lambda g, t: (g, 0, 0)),
        out_shape=jax.ShapeDtypeStruct((3, D, D), F32),
        compiler_params=pltpu.CompilerParams(dimension_semantics=("arbitrary", "arbitrary")),
    )(a3, b3)


def _wgrad_in(h, dproj, ids, chip1b):
    tt = TT_WGRAD
    n_t = TP // tt

    def body(ids_ref, a_ref, b_ref, c1_ref, o_ref, ob_ref, l0_ref, far_ref, acc, tmp, send_sems, recv_sems, tmp_sem,
             far_send_sems, far_recv_sems):
        del ids_ref
        r = pl.program_id(0)
        t = pl.program_id(1)
        x, y, c = _my_place()
        sibling = (x, y, 1 - c)
        slot = lax.rem(r, 2)
        ride_start, ride_finish = _partials_to_owners(c1_ref, far_ref, far_send_sems, far_recv_sems)

        @pl.when((r == 0) & (t == 0))
        def _():
            ride_start()

        def send_in(q):
            return pltpu.make_async_remote_copy(
                src_ref=acc.at[q % 2], dst_ref=l0_ref.at[q], send_sem=send_sems.at[q], recv_sem=recv_sems.at[q],
                device_id=sibling, device_id_type=MESH_ID)

        def landed(q):
            return pltpu.make_async_copy(l0_ref.at[q], tmp, tmp_sem)

        @pl.when(t == 0)
        def _():
            acc[slot] = jnp.zeros((D, W_IN_BLK), F32)

        acc[slot] += _dot_tn(a_ref[...], b_ref[...])

        for q in range(4):
            @pl.when((r == q) & (t == n_t - 1))
            def _(q=q):
                if q >= 1:
                    send_in(q - 1).wait_send()
                send_in(q).start()

            @pl.when((r == 4 + q) & (t == n_t - 2))
            def _(q=q):
                if q == 0:
                    send_in(3).wait_send()
                send_in(q).wait_recv()
                landed(q).start()

            @pl.when((r == 4 + q) & (t == n_t - 1))
            def _(q=q):
                landed(q).wait()
                tot = acc[q % 2] + tmp[...]
                o_ref[0] = tot
                ob_ref[0] = tot.astype(BF16)

        @pl.when((r == N_DEV - 1) & (t == n_t - 1))
        def _():
            ride_finish()

    blk = pl.BlockSpec((1, D, W_IN_BLK), lambda r, t, ids: (jnp.maximum(r - 4, 0), 0, 0))
    return pl.pallas_call(
        body, name="wgrad_in",
        grid_spec=pltpu.PrefetchScalarGridSpec(
            num_scalar_prefetch=1, grid=(N_DEV, n_t),
            in_specs=[pl.BlockSpec((tt, D), lambda r, t, ids: (t, 0)),
                      pl.BlockSpec((tt, W_IN_BLK), lambda r, t, ids: (t, ids[r])), ANY],
            out_specs=(blk, blk, ANY, ANY),
            scratch_shapes=[pltpu.VMEM((2, D, W_IN_BLK), F32), pltpu.VMEM((D, W_IN_BLK), F32),
                            pltpu.SemaphoreType.DMA((4,)), pltpu.SemaphoreType.DMA((4,)), pltpu.SemaphoreType.DMA,
                            pltpu.SemaphoreType.DMA((3,)), pltpu.SemaphoreType.DMA((3,))]),
        out_shape=(jax.ShapeDtypeStruct((4, D, W_IN_BLK), F32), jax.ShapeDtypeStruct((4, D, W_IN_BLK), BF16),
                   jax.ShapeDtypeStruct((4, D, W_IN_BLK), F32), jax.ShapeDtypeStruct((3, 3, W_ROW_BLK, D), BF16)),
        compiler_params=pltpu.CompilerParams(dimension_semantics=("arbitrary", "arbitrary")),
    )(ids, h, dproj, chip1b)


def _chip_sum_3(p3, land1, ids_mine):
    def body(ids_ref, p_ref, l_ref, o_ref, ob_ref):
        del ids_ref
        tot = p_ref[...] + l_ref[0]
        o_ref[0] = tot
        ob_ref[0] = tot.astype(BF16)

    blk = pl.BlockSpec((1, 3, W_ROW_BLK, D), lambda r, ids: (r, 0, 0, 0))
    return pl.pallas_call(
        body, name="chip_sum_3",
        grid_spec=pltpu.PrefetchScalarGridSpec(
            num_scalar_prefetch=1, grid=(4,),
            in_specs=[pl.BlockSpec((3, W_ROW_BLK, D), lambda r, ids: (0, ids[r], 0)), blk],
            out_specs=(blk, blk)),
        out_shape=(jax.ShapeDtypeStruct((4, 3, W_ROW_BLK, D), F32), jax.ShapeDtypeStruct((4, 3, W_ROW_BLK, D), BF16)),
    )(ids_mine, p3, land1)


def _dh_and_norm_bwd(dproj, w_in_full, xin, b3, norm_g, chip0b):
    tm = TM_MAT
    n_k = N_DEV // DH_K_BLKS
    n_m = TP // tm

    def body(dp_ref, w_ref, x_ref, dr_ref, g_ref, c0_ref, dx_ref, dg_ref, f0_ref, acc, send_sems, recv_sems):
        m = pl.program_id(0)
        k = pl.program_id(1)
        ride_start, ride_finish = _partials_to_owners(c0_ref, f0_ref, send_sems, recv_sems)

        @pl.when((m == 0) & (k == 0))
        def _():
            ride_start()

        @pl.when(k == 0)
        def _():
            acc[...] = jnp.zeros_like(acc)

        part = _dot_nt(dp_ref[:, 0:W_IN_BLK], w_ref[0])
        for j in range(1, DH_K_BLKS):
            part = part + _dot_nt(dp_ref[:, j * W_IN_BLK:(j + 1) * W_IN_BLK], w_ref[j])
        acc[...] += part

        @pl.when((k == n_k - 1) & (m == 0))
        def _():
            dg_ref[...] = jnp.zeros_like(dg_ref)

        @pl.when(k == n_k - 1)
        def _():
            xv = x_ref[...]
            r1 = lax.rsqrt(jnp.mean(xv * xv, axis=-1, keepdims=True) + EPS)
            xh = xv * r1
            d_h = acc[...]
            dg_ref[0:1, :] += jnp.sum(d_h * xh, axis=0, keepdims=True)
            d_xh = d_h * g_ref[...]
            dx_ref[...] = dr_ref[0].astype(F32) + r1 * (d_xh - xh * jnp.mean(d_xh * xh, axis=-1, keepdims=True))

        @pl.when((m == n_m - 1) & (k == n_k - 1))
        def _():
            ride_finish()

    return pl.pallas_call(
        body, name="dh_norm_bwd", grid=(n_m, n_k),
        in_specs=[pl.BlockSpec((tm, DH_K_BLKS * W_IN_BLK), lambda m, k: (m, k)),
                  pl.BlockSpec((DH_K_BLKS, D, W_IN_BLK), lambda m, k: (k, 0, 0)),
                  pl.BlockSpec((tm, D), lambda m, k: (m, 0)), pl.BlockSpec((1, tm, D), lambda m, k: (2, m, 0)),
                  pl.BlockSpec((1, D), lambda m, k: (0, 0)), ANY],
        out_specs=(pl.BlockSpec((tm, D), lambda m, k: (m, 0)), pl.BlockSpec((8, D), lambda m, k: (0, 0)), ANY),
        out_shape=(jax.ShapeDtypeStruct((TP, D), F32), jax.ShapeDtypeStruct((8, D), F32),
                   jax.ShapeDtypeStruct((3, D, W_IN_BLK), BF16)),
        scratch_shapes=[pltpu.VMEM((tm, D), F32), pltpu.SemaphoreType.DMA((3,)), pltpu.SemaphoreType.DMA((3,))],
        compiler_params=pltpu.CompilerParams(dimension_semantics=("arbitrary", "arbitrary")),
    )(dproj, w_in_full, xin, b3, norm_g, chip0b)


def _sum_adamw(own, landed, w, m, v, tr, name):
    rows, cols = w.shape
    n_t = rows // tr

    def body(o_ref, l1_ref, l2_ref, l3_ref, w_ref, m_ref, v_ref, g_ref, d_ref, m2_ref, v2_ref):
        g = ((o_ref[...] + l1_ref[...].astype(F32)) + l2_ref[...].astype(F32)) + l3_ref[...].astype(F32)
        delta, m2, v2 = _adamw(w_ref[...], g, m_ref[...], v_ref[...])
        g_ref[...] = g
        d_ref[...] = delta
        m2_ref[...] = m2
        v2_ref[...] = v2

    def spec(k):
        return pl.BlockSpec((tr, cols), lambda i: (i + k * n_t, 0))

    out = jax.ShapeDtypeStruct((rows, cols), F32)
    return pl.pallas_call(
        body, name=name, grid=(n_t,),
        in_specs=[spec(0), spec(0), spec(1), spec(2), spec(0), spec(0), spec(0)],
        out_specs=(spec(0),) * 4, out_shape=(out,) * 4,
    )(own, landed, landed, landed, w, m, v)


def _adamw_3(chip1, far1, ws, ms, vs):
    def body(c_ref, f_ref, *refs):
        w_refs, m_refs, v_refs, outs = refs[0:3], refs[3:6], refs[6:9], refs[9:21]
        for k in range(3):
            g = ((c_ref[0, k] + f_ref[0, k].astype(F32)) + f_ref[1, k].astype(F32)) + f_ref[2, k].astype(F32)
            delta, m2, v2 = _adamw(w_refs[k][0], g, m_refs[k][0], v_refs[k][0])
            for kind, val in enumerate((g, delta, m2, v2)):
                outs[3 * kind + k][0] = val

    full = pl.BlockSpec((1, W_ROW_BLK, D), lambda i: (0, 0, 0))
    out = jax.ShapeDtypeStruct((1, W_ROW_BLK, D), F32)
    res = pl.pallas_call(
        body, name="adamw_3", grid=(1,),
        in_specs=[pl.BlockSpec((1, 3, W_ROW_BLK, D), lambda i: (0, 0, 0, 0)),
                  pl.BlockSpec((3, 3, W_ROW_BLK, D), lambda i: (0, 0, 0, 0))] + [full] * 9,
        out_specs=(full,) * 12, out_shape=(out,) * 12,
    )(chip1, far1, *ws, *ms, *vs)
    return tuple(res[3 * kind:3 * kind + 3] for kind in range(4))


N_SMALL = 9


def _small_update(pack_all, srs_all, ws, ms, vs):
    def body(pk_ref, sr_ref, *refs):
        w_refs, m_refs, v_refs = refs[0:N_SMALL], refs[N_SMALL:2 * N_SMALL], refs[2 * N_SMALL:3 * N_SMALL]
        loss_ref = refs[3 * N_SMALL]
        outs = refs[3 * N_SMALL + 1:7 * N_SMALL + 1]
        tot_sc, tots_sc = refs[7 * N_SMALL + 1:]
        tot = pk_ref[0]
        tot_s = sr_ref[0]
        for d in range(1, N_DEV):
            tot = tot + pk_ref[d]
            tot_s = tot_s + sr_ref[d]
        tot_sc[...] = tot
        tots_sc[...] = tot_s
        loss_ref[...] = jnp.sum(tot_sc[5:6, :], axis=1, keepdims=True)
        lbl = w_refs[4]
        p0 = _sigmoid(lbl[0:1, :] - lbl[1:2, :])
        d_l0 = tot_sc[4:5, :] * p0 * (1.0 - p0)

        def update(k, sel, g):
            delta, m2, v2 = _adamw(w_refs[k][sel], g, m_refs[k][sel], v_refs[k][sel])
            for kind, val in enumerate((g, delta, m2, v2)):
                outs[N_SMALL * kind + k][sel] = val

        everything = (slice(None), slice(None))
        for k, row in ((0, 0), (1, 1), (2, 2), (3, 3), (5, 6), (6, 7)):
            update(k, everything, tot_sc[row:row + 1, :])
        update(4, (slice(0, 1), slice(None)), d_l0)
        update(4, (slice(1, 2), slice(None)), -d_l0)
        update(7, (0, slice(None), slice(None)), tots_sc[0:CONV_K, :])
        update(8, everything, tots_sc[META_ROW:META_ROW + N_META, :])

    shapes = [jax.ShapeDtypeStruct(w.shape, F32) for w in ws]
    res = pl.pallas_call(
        body, name="small_update",
        out_shape=(jax.ShapeDtypeStruct((1, 1), F32), *(shapes * 4)),
        scratch_shapes=[pltpu.VMEM((8, D), F32), pltpu.VMEM((SMALL_ROWS, HEAD_W), F32)],
    )(pack_all, srs_all, *ws, *ms, *vs)
    return res[0], tuple(res[1 + N_SMALL * kind:1 + N_SMALL * (kind + 1)] for kind in range(4))


def _local_step(xin, proj, target, conv_w_full, conv_b, ln_g, ln_b, w3_b, lb_logits, gnorm_g, final_g, ids_mine):
    fg = final_g.reshape(1, D)
    c0, o, s_start, w3_full = _rec_conv_fwd(proj, lb_logits, conv_w_full, conv_b, w3_b)
    d_o, d_c0, d_z, dproj, a3, b3, red = _mid(xin, target, o, c0, proj, w3_full, ln_g, ln_b, gnorm_g, fg)
    p3 = _wgrad3(a3, b3)
    dproj, dlb, d_conv_w, land1 = _rec_conv_bwd(proj, lb_logits, d_o, s_start, d_c0, d_z, conv_w_full, dproj, p3)
    chip1, chip1b = _chip_sum_3(p3, land1, ids_mine)
    return dproj, b3, p3, chip1, chip1b, d_conv_w, red, dlb


def kernel(x, meta_tokens, norm_g, w_in, conv_w, conv_b, ln_g, ln_b, w_conv_out, lb_logits, gnorm_g, w_rec_out, w_out, final_g, loss_target, m_meta_tokens, m_norm_g, m_w_in, m_conv_w, m_conv_b, m_ln_g, m_ln_b, m_w_conv_out, m_lb_logits, m_gnorm_g, m_w_rec_out, m_w_out, m_final_g, v_meta_tokens, v_norm_g, v_w_in, v_conv_w, v_conv_b, v_ln_g, v_ln_b, v_w_conv_out, v_lb_logits, v_gnorm_g, v_w_rec_out, v_w_out, v_final_g):
    mx, my, mc = _my_place()

    ws_s = jnp.concatenate([conv_w[0], jnp.zeros((1, HEAD_W), F32), meta_tokens], axis=0)
    small_full = jnp.transpose(_gather_small(ws_s), (1, 0, 2)).reshape(SMALL_ROWS, D)
    conv_w_full = small_full[0:CONV_K]
    meta_full = small_full[META_ROW:META_ROW + N_META]
    w_in_b, w3_b = _cast_shards(w_in[0], w_conv_out, w_rec_out, w_out)
    first, second, diag = _gather_chips(mx, my, mc)
    use_order = [(mx, my, mc), (mx, my, 1 - mc), (*first, mc), (*second, 1 - mc), (*second, mc), (*first, 1 - mc),
                 (*diag, mc), (*diag, 1 - mc)]
    order = jnp.stack([_dev_index(*p) for p in use_order]).astype(jnp.int32)
    proj, xin, h, w_in_full = _gather_and_proj(x[0], meta_full, norm_g, w_in_b, order)
    h = h.reshape(TP, D)

    ids_mine = jnp.stack([_dev_index(*_chip_rel(mx, my, r), mc) for r in range(4)]).astype(jnp.int32)
    ids_sib = jnp.stack([_dev_index(*_chip_rel(mx, my, r), 1 - mc) for r in range(4)]).astype(jnp.int32)
    dproj, b3, _, chip1, chip1b, d_conv_w, red, dlb = _local_step(
        xin, proj, loss_target[0], conv_w_full, conv_b, ln_g, ln_b, w3_b, lb_logits, gnorm_g, final_g, ids_mine)

    chip0, chip0b, _, far1 = _wgrad_in(h, dproj, jnp.concatenate([ids_sib, ids_mine]), chip1b)
    d_xin, dng, far0 = _dh_and_norm_bwd(dproj, w_in_full, xin, b3, norm_g, chip0b)
    pack = jnp.concatenate([dng[0:1], red[4:5], red[2:3], red[3:4], dlb[0:1], red[5:6], red[1:2], red[0:1]], axis=0)
    g_in, d_in, m_in, v_in = _sum_adamw(chip0.reshape(4 * D, W_IN_BLK), far0.reshape(3 * D, W_IN_BLK), w_in[0],
                                        m_w_in[0], v_w_in[0], 256, "adamw_in")
    big3 = _adamw_3(chip1, far1, (w_conv_out, w_rec_out, w_out), (m_w_conv_out, m_w_rec_out, m_w_out),
                    (v_w_conv_out, v_w_rec_out, v_w_out))

    srs = jnp.concatenate([d_conv_w, d_xin[PAD_FRONT:ROW0]], axis=0)
    srs = jnp.transpose(srs.reshape(SMALL_ROWS, N_DEV, HEAD_W), (1, 0, 2))
    pack_all, srs_all = _exchange_small(pack, srs)
    loss, small = _small_update(
        pack_all, srs_all,
        (norm_g, conv_b, ln_g, ln_b, lb_logits, gnorm_g, final_g.reshape(1, D), conv_w, meta_tokens),
        (m_norm_g, m_conv_b, m_ln_g, m_ln_b, m_lb_logits, m_gnorm_g, m_final_g.reshape(1, D), m_conv_w, m_meta_tokens),
        (v_norm_g, v_conv_b, v_ln_g, v_ln_b, v_lb_logits, v_gnorm_g, v_final_g.reshape(1, D), v_conv_w, v_meta_tokens))

    outs = [loss.reshape(()), d_xin[ROW0:][None]]
    for kind, a_in in enumerate((g_in, d_in, m_in, v_in)):
        ng, cb, lg, lb_, lbl, gg, fg, cw, mt = small[kind]
        a_3 = big3[kind]
        outs += [mt, ng, a_in[None], cw, cb, lg, lb_, a_3[0], lbl, gg, a_3[1], a_3[2], fg.reshape(D)]
    return tuple(outs)
```

```python
import jax
import jax.numpy as jnp
from jax import lax
from jax.experimental import pallas as pl
from jax.experimental.pallas import tpu as pltpu

F32 = jnp.float32
BF16 = jnp.bfloat16
ACT = BF16

D = 1024
SEQ = 4096
N_META = 16
CHUNK = 64
PAD_FRONT = 48
ROW0 = PAD_FRONT + N_META
TP = ROW0 + SEQ
N_CHUNK = TP // CHUNK
HEADS = 8
HEAD_W = 128
D_IN = 9 * D
N_DEV = 8
W_IN_BLK = D_IN // N_DEV
W_ROW_BLK = D // N_DEV
CONV_K = 31
SMALL_ROWS = 48
META_ROW = 32
EPS = 1e-6
HALO = 32

TM_MAT = 1040
TT_WGRAD = 2080
DH_K_BLKS = 2
TM_ELT = 208
CHUNKS_PER_STEP = 5
CONV_STRIPS = 5

ADAM_LR = 0.001
ADAM_B1 = 0.9
ADAM_B2 = 0.999
ADAM_EPS = 1e-08
ADAM_WD = 0.01
ADAM_STEP = 10

MESH_ID = pl.DeviceIdType.MESH
ANY = pl.BlockSpec(memory_space=pl.ANY)


def _sigmoid(v):
    return jax.nn.sigmoid(v)


def _dsilu(silu, s):
    return s + silu * (1.0 - s)


def _dot(a, b):
    return jnp.dot(a, b, preferred_element_type=F32)


def _dot_nt(a, b):
    return lax.dot_general(a, b, (((1,), (1,)), ((), ())), preferred_element_type=F32)


def _dot_tn(a, b):
    return lax.dot_general(a, b, (((0,), (0,)), ((), ())), preferred_element_type=F32)


def _tri_matmul(tri, v):
    hi = v.astype(BF16)
    lo = (v - hi.astype(F32)).astype(BF16)
    return _dot(tri, hi) + _dot(tri, lo)


def _adamw(w, g, m, v):
    m2 = ADAM_B1 * m + (1.0 - ADAM_B1) * g
    v2 = ADAM_B2 * v + (1.0 - ADAM_B2) * jnp.square(g)
    m_hat = m2 / (1.0 - ADAM_B1 ** ADAM_STEP)
    v_hat = v2 / (1.0 - ADAM_B2 ** ADAM_STEP)
    delta = -ADAM_LR * (m_hat / (jnp.sqrt(v_hat) + ADAM_EPS) + ADAM_WD * w)
    return delta, m2, v2


def _window_start(i, tm):
    assert tm % 16 == 0 and ROW0 % 16 == 0
    return pl.multiple_of(16 * jnp.maximum((tm // 16) * i - ROW0 // 16, 0), 16)


def _my_place():
    return lax.axis_index("x"), lax.axis_index("y"), lax.axis_index("c")


def _dev_index(px, py, pc):
    return 4 * px + 2 * py + pc


def _cast_shards(w_in_s, w_conv_s, w_rec_s, w_out_s):
    def body(a_ref, c_ref, r_ref, o_ref, oa_ref, ob_ref):
        oa_ref[...] = a_ref[...].astype(BF16)
        for k, ref in enumerate((c_ref, r_ref, o_ref)):
            ob_ref[k] = ref[0].astype(BF16)

    return pl.pallas_call(
        body, name="cast_shards",
        out_shape=(jax.ShapeDtypeStruct(w_in_s.shape, BF16), jax.ShapeDtypeStruct((3, W_ROW_BLK, D), BF16)),
    )(w_in_s, w_conv_s, w_rec_s, w_out_s)


def _peer(x, y, c, r):
    return (jnp.bitwise_xor(x, (r >> 2) & 1), jnp.bitwise_xor(y, (r >> 1) & 1), jnp.bitwise_xor(c, r & 1))


def _gather_small(small_s):
    def body(s_ref, o_ref, send_sems, recv_sems, local_sem):
        x, y, c = _my_place()
        my_id = _dev_index(x, y, c)
        mine = pltpu.make_async_copy(s_ref, o_ref.at[my_id], local_sem)
        mine.start()
        copies = []
        for r in range(1, N_DEV):
            cp = pltpu.make_async_remote_copy(
                src_ref=s_ref, dst_ref=o_ref.at[my_id], send_sem=send_sems.at[r - 1], recv_sem=recv_sems.at[r - 1],
                device_id=_peer(x, y, c, r), device_id_type=MESH_ID)
            cp.start()
            copies.append(cp)
        for cp in copies:
            cp.wait_recv()
        for cp in copies:
            cp.wait_send()
        mine.wait()

    return pl.pallas_call(
        body, name="gather_small", out_shape=jax.ShapeDtypeStruct((N_DEV,) + small_s.shape, F32),
        in_specs=[ANY], out_specs=ANY,
        scratch_shapes=[pltpu.SemaphoreType.DMA((7,)), pltpu.SemaphoreType.DMA((7,)), pltpu.SemaphoreType.DMA],
    )(small_s)


def _w3_gather(src, out, stage, send_sems, recv_sems, local_sems):
    x, y, c = _my_place()
    me, sibling = (x, y, c), (x, y, 1 - c)
    chips = [(1 - x, y), (x, 1 - y), (1 - x, 1 - y)]

    def block(place):
        d = _dev_index(*place)
        return out.at[:, pl.ds(pl.multiple_of(d * W_ROW_BLK, W_ROW_BLK), W_ROW_BLK), :]

    def copy(k, place, to, from_src=False):
        return pltpu.make_async_remote_copy(
            src_ref=src if from_src else block(place), dst_ref=block(place),
            send_sem=send_sems.at[k], recv_sem=recv_sems.at[k], device_id=to, device_id_type=MESH_ID)

    own_in = pltpu.make_async_copy(src, stage, local_sems.at[0])
    own_out = pltpu.make_async_copy(stage, block(me), local_sems.at[1])

    def start():
        copy(0, me, sibling, from_src=True).start()
        for j, chip in enumerate(chips):
            copy(1 + j, me, (*chip, c), from_src=True).start()
        own_in.start()
        own_in.wait()
        own_out.start()

    def finish():
        for j, chip in enumerate(chips):
            copy(1 + j, (*chip, c), me).wait_recv()
            copy(4 + j, (*chip, c), sibling).start()
        copy(0, sibling, me).wait_recv()
        for j, chip in enumerate(chips):
            copy(4 + j, (*chip, 1 - c), me).wait_recv()
        for k in range(7):
            copy(k, me, me).wait_send()
        own_out.wait()

    return start, finish


def _p3_to_sibling(p3_ref, land_ref, send_sems, recv_sems):
    x, y, c = _my_place()

    def cp(q):
        d = _dev_index(*_chip_rel(x, y, q), 1 - c)
        return pltpu.make_async_remote_copy(
            src_ref=p3_ref.at[:, pl.ds(pl.multiple_of(d * W_ROW_BLK, W_ROW_BLK), W_ROW_BLK), :],
            dst_ref=land_ref.at[q], send_sem=send_sems.at[q], recv_sem=recv_sems.at[q],
            device_id=(x, y, 1 - c), device_id_type=MESH_ID)

    def start():
        for q in range(4):
            cp(q).start()

    def finish():
        for q in range(4):
            cp(q).wait_recv()
        for q in range(4):
            cp(q).wait_send()

    return start, finish


def _partials_to_owners(src_ref, far_ref, send_sems, recv_sems):
    x, y, c = _my_place()

    def cp(q):
        return pltpu.make_async_remote_copy(
            src_ref=src_ref.at[q], dst_ref=far_ref.at[q - 1], send_sem=send_sems.at[q - 1],
            recv_sem=recv_sems.at[q - 1], device_id=(*_chip_rel(x, y, q), c), device_id_type=MESH_ID)

    def start():
        for q in range(1, 4):
            cp(q).start()

    def finish():
        for q in range(1, 4):
            cp(q).wait_recv()
        for q in range(1, 4):
            cp(q).wait_send()

    return start, finish


def _gather_chips(x, y, c):
    first = (jnp.bitwise_xor(x, 1 - c), jnp.bitwise_xor(y, c))
    second = (jnp.bitwise_xor(x, c), jnp.bitwise_xor(y, 1 - c))
    return [first, second, (1 - x, 1 - y)]


def _gather_and_proj(x_seq, meta_full, norm_g, w_in_b, order):
    tm = TM_MAT
    n_m = TP // tm
    last_m = n_m - 1

    def body(order_ref, x_ref, meta_ref, g_ref, s0, proj_ref, xin_ref, h_out, o0, hbuf, wbuf, send_sems, recv_sems,
             local_sems):
        del order_ref
        n = pl.program_id(0)
        m = pl.program_id(1)
        x, y, c = _my_place()
        me, sibling = (x, y, c), (x, y, 1 - c)
        chips = _gather_chips(x, y, c)

        def block(place):
            return o0.at[_dev_index(*place)]

        def copy(k, place, to, from_src=False):
            return pltpu.make_async_remote_copy(
                src_ref=s0 if from_src else block(place), dst_ref=block(place),
                send_sem=send_sems.at[k], recv_sem=recv_sems.at[k], device_id=to, device_id_type=MESH_ID)

        def to_vmem(place, slot):
            return pltpu.make_async_copy(block(place), wbuf.at[slot], local_sems.at[slot])

        own_out = pltpu.make_async_copy(wbuf.at[0], block(me), local_sems.at[2])
        h_copy = pltpu.make_async_copy(hbuf, h_out, local_sems.at[3])

        @pl.when((n == 0) & (m == 0))
        def _():
            copy(0, me, sibling, from_src=True).start()
            for j, chip in enumerate(chips[0:2]):
                copy(1 + j, me, (*chip, c), from_src=True).start()
            mine = pltpu.make_async_copy(s0, wbuf.at[0], local_sems.at[0])
            mine.start()
            mine.wait()
            own_out.start()

        @pl.when(n == 0)
        def _():
            xv = x_ref[...]
            xin_ref[...] = jnp.where(m == 0, pltpu.roll(xv, ROW0, 0), xv)

            @pl.when(m == 0)
            def _():
                xin_ref[0:PAD_FRONT, :] = jnp.zeros((PAD_FRONT, D), F32)
                xin_ref[PAD_FRONT:ROW0, :] = meta_ref[...]

            xv = xin_ref[...]
            r = lax.rsqrt(jnp.mean(xv * xv, axis=-1, keepdims=True) + EPS)
            hbuf[m] = (xv * r * g_ref[...]).astype(BF16)

        between = [4 + c, 5 - c, 6]
        first, second, diag = chips
        plan = [(sibling, (0, sibling), None),
                ((*first, c), (1, (*first, c)), between[0]),
                ((*second, 1 - c), (between[1], (*second, 1 - c)), None),
                ((*second, c), (2, (*second, c)), between[1]),
                ((*first, 1 - c), (between[0], (*first, 1 - c)), None),
                ((*diag, c), (3, (*diag, c)), between[2]),
                ((*diag, 1 - c), (between[2], (*diag, 1 - c)), None)]

        for s, (place, (k, origin), pass_on) in enumerate(plan, start=1):
            @pl.when((n == s - 1) & (m == last_m))
            def _(s=s, place=place, k=k, origin=origin, pass_on=pass_on):
                copy(k, origin, me).wait_recv()
                if pass_on is not None:
                    copy(pass_on, place, sibling).start()
                if s == 2:
                    copy(3, place, (*chips[1], c)).start()
                    own_out.wait()
                to_vmem(place, s % 2).start()

            @pl.when((n == s) & (m == 0))
            def _(s=s, place=place):
                to_vmem(place, s % 2).wait()

        proj_ref[...] = _dot(hbuf[m], wbuf[lax.rem(n, 2)]).astype(BF16)

        @pl.when((n == 0) & (m == last_m))
        def _():
            h_copy.start()

        @pl.when((n == N_DEV - 1) & (m == last_m))
        def _():
            for k in range(7):
                copy(k, me, me).wait_send()
            h_copy.wait()

    return pl.pallas_call(
        body, name="gather_and_proj",
        grid_spec=pltpu.PrefetchScalarGridSpec(
            num_scalar_prefetch=1, grid=(N_DEV, n_m),
            in_specs=[pl.BlockSpec((pl.Element(tm), pl.Element(D)),
                                   lambda n, m, o: (_window_start(jnp.where(n == 0, m, 0), tm), 0)),
                      pl.BlockSpec((N_META, D), lambda n, m, o: (0, 0)),
                      pl.BlockSpec((1, D), lambda n, m, o: (0, 0)), ANY],
            out_specs=(pl.BlockSpec((tm, W_IN_BLK), lambda n, m, o: (m, o[n])),
                       pl.BlockSpec((tm, D), lambda n, m, o: (jnp.where(n == 0, m, last_m), 0)), ANY, ANY),
            scratch_shapes=[pltpu.VMEM((n_m, tm, D), BF16), pltpu.VMEM((2, D, W_IN_BLK), BF16),
                            pltpu.SemaphoreType.DMA((7,)), pltpu.SemaphoreType.DMA((7,)),
                            pltpu.SemaphoreType.DMA((4,))]),
        out_shape=(jax.ShapeDtypeStruct((TP, D_IN), BF16), jax.ShapeDtypeStruct((TP, D), F32),
                   jax.ShapeDtypeStruct((n_m, tm, D), BF16), jax.ShapeDtypeStruct((N_DEV, D, W_IN_BLK), BF16)),
        compiler_params=pltpu.CompilerParams(dimension_semantics=("arbitrary", "arbitrary")),
    )(order, x_seq, meta_full, norm_g, w_in_b)


def _chip_rel(x, y, r):
    return (jnp.bitwise_xor(x, r >> 1), jnp.bitwise_xor(y, r & 1))


def _exchange_small(pack, srs):
    def body(pk, sr, pk_all, sr_all, send_sems, recv_sems, local_sems):
        x, y, c = _my_place()
        my_id = _dev_index(x, y, c)
        mine = [pltpu.make_async_copy(pk, pk_all.at[my_id], local_sems.at[0]),
                pltpu.make_async_copy(sr.at[my_id], sr_all.at[my_id], local_sems.at[1])]
        for cp in mine:
            cp.start()
        copies = []
        for r in range(1, N_DEV):
            peer = (jnp.bitwise_xor(x, (r >> 2) & 1), jnp.bitwise_xor(y, (r >> 1) & 1), jnp.bitwise_xor(c, r & 1))
            peer_id = _dev_index(*peer)
            for a, (src, dst) in enumerate(((pk, pk_all.at[my_id]), (sr.at[peer_id], sr_all.at[my_id]))):
                cp = pltpu.make_async_remote_copy(
                    src_ref=src, dst_ref=dst, send_sem=send_sems.at[a * 7 + r - 1], recv_sem=recv_sems.at[a * 7 + r - 1],
                    device_id=peer, device_id_type=MESH_ID)
                cp.start()
                copies.append(cp)
        for cp in copies:
            cp.wait_recv()
        for cp in copies:
            cp.wait_send()
        for cp in mine:
            cp.wait()

    return pl.pallas_call(
        body, name="exchange_small",
        out_shape=(jax.ShapeDtypeStruct((N_DEV,) + pack.shape, F32), jax.ShapeDtypeStruct(srs.shape, F32)),
        in_specs=[ANY, ANY], out_specs=(ANY, ANY),
        scratch_shapes=[pltpu.SemaphoreType.DMA((14,)), pltpu.SemaphoreType.DMA((14,)), pltpu.SemaphoreType.DMA((2,))],
    )(pack, srs)


N_CB = D // HEAD_W


def _store_by_cb(ref, idx, rows, val):
    for cb in range(N_CB):
        ref[(*idx, cb, rows, slice(None))] = val[:, cb * HEAD_W:(cb + 1) * HEAD_W]


def _fill_shifts(sh, tm):
    n = tm + HALO - 8
    for s in range(1, 8):
        for cb in range(N_CB):
            sh[s, cb, 0:n, :] = sh[0, cb, s:s + n, :]


def _gates(p_ref, lbl_ref, chunk, bsc):
    lb = _sigmoid(lbl_ref[0:1, :] - lbl_ref[1:2, :])
    q_raw = p_ref[:, 0:D].astype(F32)
    f_raw = p_ref[:, D:2 * D].astype(F32)
    sq = _sigmoid(q_raw)
    q = q_raw * sq
    sg = _sigmoid(f_raw)
    f = lb + (1.0 - lb) * sg
    row = lax.broadcasted_iota(jnp.int32, (CHUNK, 1), 0) + chunk * CHUNK
    valid = row >= PAD_FRONT
    lf = jnp.where(valid, jnp.log(f), 0.0)
    k = jnp.where(valid, 1.0 - f, 0.0)
    r_i = lax.broadcasted_iota(jnp.int32, (CHUNK, CHUNK), 0)
    c_i = lax.broadcasted_iota(jnp.int32, (CHUNK, CHUNK), 1)
    causal = r_i >= c_i
    bsc[...] = _tri_matmul(causal.astype(BF16), lf)
    b = bsc[...]
    b_mid = bsc[CHUNK // 2 - 1:CHUNK // 2, :]
    b_last = bsc[CHUNK - 1:CHUNK, :]
    e_q = jnp.exp(b)
    e_qm = jnp.exp(b - b_mid)
    e_km = jnp.exp(b_mid - b)
    e_kh = jnp.exp(b_last - b)
    e_last = jnp.exp(b_last)
    return dict(lb=lb, q_raw=q_raw, sq=sq, q=q, sg=sg, f=f, k=k, valid=valid, causal=causal,
                e_q=e_q, e_qm=e_qm, e_km=e_km, e_kh=e_kh, e_last=e_last)


def _rec_conv_fwd(proj, lb_logits, conv_w, conv_b, w3_b):
    cps = CHUNKS_PER_STEP
    tm = cps * CHUNK
    n_strip = CONV_STRIPS
    strip = tm // n_strip

    def body(p_ref, lbl_ref, pg_ref, w_ref, b_ref, w3s_ref, c0_ref, o_ref, s_ref, w3o_ref,
             st, bsc, sh, c0_sc, w3buf, send_sems, recv_sems, local_sems):
        n = pl.program_id(0)
        gather_start, gather_finish = _w3_gather(w3s_ref, w3o_ref, w3buf, send_sems, recv_sems, local_sems)

        @pl.when(n == 0)
        def _():
            st[...] = jnp.zeros_like(st)
            sh[0, :, 0:HALO, :] = jnp.zeros((N_CB, HALO, HEAD_W), F32)
            gather_start()

        @pl.when(n > 0)
        def _():
            sh[0, :, 0:HALO, :] = sh[0, :, tm:tm + HALO, :]

        ga = pg_ref[:, 0:D].astype(F32)
        gb = pg_ref[:, D:2 * D].astype(F32)
        _store_by_cb(sh, (0,), slice(HALO, HALO + tm), ga * _sigmoid(gb))
        _fill_shifts(sh, tm)

        def conv_unit(cb, s_i):
            cs = slice(cb * HEAD_W, (cb + 1) * HEAD_W)
            acc = jnp.broadcast_to(b_ref[:, cs], (strip, HEAD_W))
            for j in range(CONV_K):
                off = HALO - (CONV_K - 1) + j
                lo = s_i * strip + 8 * (off // 8)
                acc = acc + w_ref[j:j + 1, cs] * sh[off % 8, cb, lo:lo + strip, :]
            c0_sc[s_i * strip:(s_i + 1) * strip, cs] = acc

        units = [(cb, s_i) for cb in range(N_CB) for s_i in range(n_strip)]

        def prep(ci):
            g = _gates(p_ref.at[pl.ds(ci * CHUNK, CHUNK)], lbl_ref, n * cps + ci, bsc.at[ci])
            g["q1"] = (g["q"] * g["e_q"]).astype(BF16)
            g["qm"] = (g["q"] * g["e_qm"]).astype(BF16)
            g["km"] = (g["k"] * g["e_km"]).astype(BF16)
            g["kh"] = (g["k"] * g["e_kh"]).astype(BF16)
            return g

        def heads(ci, g):
            rs = pl.ds(ci * CHUNK, CHUNK)
            pv = p_ref.at[rs]
            s_ref[ci] = st[...]
            for h in range(HEADS):
                sl = slice(h * HEAD_W, (h + 1) * HEAD_W)
                v = pv[:, 2 * D + h * HEAD_W:2 * D + (h + 1) * HEAD_W]
                att = jnp.where(g["causal"], _dot_nt(g["qm"][:, sl], g["km"][:, sl]), 0.0).astype(BF16)
                s_h = st[h]
                o_ref[rs, sl] = (_dot_nt(g["q1"][:, sl], s_h.astype(BF16)) + _dot(att, v)).astype(ACT)
                st[h] = s_h * g["e_last"][:, sl] + _dot_tn(v, g["kh"][:, sl])
                if units:
                    conv_unit(*units.pop(0))

        ready = prep(0)
        for ci in range(cps):
            coming = prep(ci + 1) if ci + 1 < cps else None
            heads(ci, ready)
            ready = coming
        while units:
            conv_unit(*units.pop(0))
        c0_ref[...] = c0_sc[...].astype(ACT)

        @pl.when(n == N_CHUNK // cps - 1)
        def _():
            gather_finish()

    def rows_of(width, col):
        return pl.BlockSpec((tm, width), lambda n: (n, col))

    return pl.pallas_call(
        body, name="rec_conv_fwd", grid=(N_CHUNK // cps,),
        in_specs=[rows_of(3 * D, 1), pl.BlockSpec((2, D), lambda n: (0, 0)), rows_of(2 * D, 0),
                  pl.BlockSpec((CONV_K, D), lambda n: (0, 0)), pl.BlockSpec((1, D), lambda n: (0, 0)), ANY],
        out_specs=(rows_of(D, 0), rows_of(D, 0), pl.BlockSpec((cps, HEADS, HEAD_W, HEAD_W), lambda n: (n, 0, 0, 0)), ANY),
        out_shape=(jax.ShapeDtypeStruct((TP, D), ACT), jax.ShapeDtypeStruct((TP, D), ACT),
                   jax.ShapeDtypeStruct((N_CHUNK, HEADS, HEAD_W, HEAD_W), F32), jax.ShapeDtypeStruct((3, D, D), BF16)),
        scratch_shapes=[pltpu.VMEM((HEADS, HEAD_W, HEAD_W), F32), pltpu.VMEM((cps, CHUNK, D), F32),
                        pltpu.VMEM((8, N_CB, HALO + tm, HEAD_W), F32), pltpu.VMEM((tm, D), F32),
                        pltpu.VMEM((3, W_ROW_BLK, D), BF16), pltpu.SemaphoreType.DMA((7,)),
                        pltpu.SemaphoreType.DMA((7,)), pltpu.SemaphoreType.DMA((2,))],
        compiler_params=pltpu.CompilerParams(dimension_semantics=("arbitrary",)),
    )(proj, lb_logits, proj, conv_w, conv_b, w3_b)


def _mid(xin, tgt, o, c0, proj, w3, ln_g, ln_b, gnorm_g, final_g):
    tm = TM_ELT

    def body(x_ref, t_ref, o_ref, c0_ref, z_ref, gr_ref, mc_ref, mr_ref, w_ref, lng_ref, lnb_ref, gng_ref, fg_ref,
             do_ref, dc0_ref, dz_ref, dp_ref, a3_ref, b3_ref, red_ref, on_sc, don_sc):
        i = pl.program_id(0)

        @pl.when(i == 0)
        def _():
            red_ref[...] = jnp.zeros_like(red_ref)

        w_conv, w_rec, w_out = w_ref[0], w_ref[1], w_ref[2]
        c0v = c0_ref[...].astype(F32)
        mu = jnp.mean(c0v, axis=-1, keepdims=True)
        xc = c0v - mu
        rstd = lax.rsqrt(jnp.mean(xc * xc, axis=-1, keepdims=True) + EPS)
        xh = xc * rstd
        c1 = xh * lng_ref[...] + lnb_ref[...]
        s1 = _sigmoid(c1)
        c2 = c1 * s1
        z = z_ref[...].astype(F32)
        sz = _sigmoid(z)
        silu_z = z * sz
        u_conv = (c2 * silu_z).astype(BF16)
        y_conv = _dot(u_conv, w_conv)
        ov = o_ref[...].astype(F32)
        r3 = []
        for h in range(HEADS):
            sl = slice(h * HEAD_W, (h + 1) * HEAD_W)
            oh = ov[:, sl]
            r_h = lax.rsqrt(jnp.mean(oh * oh, axis=-1, keepdims=True) + EPS)
            r3.append(r_h)
            on_sc[:, sl] = oh * r_h
        o_n = on_sc[...]
        o_g = o_n * gng_ref[...]
        gr = gr_ref[...].astype(F32)
        sgr = _sigmoid(gr)
        silu_g = gr * sgr
        u_rec = (o_g * silu_g).astype(BF16)
        y_rec = _dot(u_rec, w_rec)
        mc = mc_ref[...].astype(F32)
        mr = mr_ref[...].astype(F32)
        smc = _sigmoid(mc)
        smr = _sigmoid(mr)
        merged = (smc * y_conv + smr * y_rec).astype(BF16)
        res = x_ref[...] + _dot(merged, w_out)
        r2 = lax.rsqrt(jnp.mean(res * res, axis=-1, keepdims=True) + EPS)
        xh2 = res * r2
        row = lax.broadcasted_iota(jnp.int32, (tm, 1), 0) + i * tm
        real = row >= ROW0
        tgt = t_ref[...]
        tgt = jnp.where(i == 0, pltpu.roll(tgt, ROW0, 0), tgt)
        diff = jnp.where(real, xh2 * fg_ref[...] - tgt, 0.0)
        d_y = diff * (1.0 / D)
        d_xh2 = d_y * fg_ref[...]
        d_res = r2 * (d_xh2 - xh2 * jnp.mean(d_xh2 * xh2, axis=-1, keepdims=True))
        d_res_b = d_res.astype(BF16)
        d_merged = _dot_nt(d_res_b, w_out)
        d_yc_f = d_merged * smc
        d_yr_f = d_merged * smr
        d_yc = d_yc_f.astype(BF16)
        d_yr = d_yr_f.astype(BF16)
        dp_ref[:, D:2 * D] = (d_yc_f * y_conv * (1.0 - smc)).astype(BF16)
        dp_ref[:, 2 * D:3 * D] = (d_yr_f * y_rec * (1.0 - smr)).astype(BF16)
        d_ur = _dot_nt(d_yr, w_rec)
        d_og = d_ur * silu_g
        dp_ref[:, 0:D] = (d_ur * o_g * _dsilu(silu_g, sgr)).astype(BF16)
        d_on = d_og * gng_ref[...]
        for h in range(HEADS):
            sl = slice(h * HEAD_W, (h + 1) * HEAD_W)
            d_h = d_on[:, sl]
            n_h = o_n[:, sl]
            don_sc[:, sl] = r3[h] * (d_h - n_h * jnp.mean(d_h * n_h, axis=-1, keepdims=True))
        do_ref[...] = don_sc[...].astype(ACT)
        d_uc = _dot_nt(d_yc, w_conv)
        d_c2 = d_uc * silu_z
        dz_ref[...] = (d_uc * c2 * _dsilu(silu_z, sz)).astype(BF16)
        d_c1 = d_c2 * _dsilu(c2, s1)
        d_xh = d_c1 * lng_ref[...]
        d_c0 = rstd * (d_xh - jnp.mean(d_xh, axis=-1, keepdims=True)
                       - xh * jnp.mean(d_xh * xh, axis=-1, keepdims=True))
        dc0_ref[...] = d_c0.astype(ACT)
        a3_ref[0] = u_conv
        b3_ref[0] = d_yc
        a3_ref[1] = u_rec
        b3_ref[1] = d_yr
        a3_ref[2] = merged
        b3_ref[2] = d_res_b
        def colsum(vv):
            return jnp.sum(vv, axis=0, keepdims=True)

        red_ref[0:1, :] += colsum(d_y * xh2)
        red_ref[1:2, :] += colsum(d_og * o_n)
        red_ref[2:3, :] += colsum(d_c1 * xh)
        red_ref[3:4, :] += colsum(d_c1)
        red_ref[4:5, :] += colsum(d_c0)
        red_ref[5:6, :] += colsum(diff * diff) * (0.5 / D)

    def row_block(width, col):
        return pl.BlockSpec((tm, width), lambda i: (i, col))

    def const_block(shape):
        return pl.BlockSpec(shape, lambda i: (0,) * len(shape))

    stack = jax.ShapeDtypeStruct((3, TP, D), BF16)
    stack_spec = pl.BlockSpec((3, tm, D), lambda i: (0, i, 0))
    return pl.pallas_call(
        body, name="mid", grid=(TP // tm,),
        in_specs=[row_block(D, 0),
                  pl.BlockSpec((pl.Element(tm), pl.Element(D)), lambda i: (_window_start(i, tm), 0)),
                  row_block(D, 0), row_block(D, 0),
                  row_block(D, 2), row_block(D, 6), row_block(D, 7), row_block(D, 8),
                  pl.BlockSpec((3, D, D), lambda i: (0, 0, 0), pipeline_mode=pl.Buffered(1)),
                  const_block((1, D)), const_block((1, D)), const_block((1, D)), const_block((1, D))],
        out_specs=(row_block(D, 0), row_block(D, 0), row_block(D, 0), row_block(3 * D, 2),
                   stack_spec, stack_spec, const_block((8, D))),
        out_shape=(jax.ShapeDtypeStruct((TP, D), ACT), jax.ShapeDtypeStruct((TP, D), ACT),
                   jax.ShapeDtypeStruct((TP, D), BF16),
                   jax.ShapeDtypeStruct((TP, D_IN), BF16), stack, stack, jax.ShapeDtypeStruct((8, D), F32)),
        scratch_shapes=[pltpu.VMEM((tm, D), F32), pltpu.VMEM((tm, D), F32)],
        compiler_params=pltpu.CompilerParams(dimension_semantics=("arbitrary",), vmem_limit_bytes=60 * 1024 * 1024),
    )(xin, tgt, o, c0, proj, proj, proj, proj, w3, ln_g, ln_b, gnorm_g, final_g)


def _rec_conv_bwd(proj, lb_logits, d_o, s_start, d_c0, d_z, conv_w, dproj, p3):
    cps = CHUNKS_PER_STEP
    tm = cps * CHUNK
    last = N_CHUNK // cps - 1
    n_strip = CONV_STRIPS
    strip = tm // n_strip

    def body(p_ref, lbl_ref, do_ref, s_ref, pg_ref, dc_ref, dz_ref, w_ref, dproj_in, p3_ref,
             dp_ref, dlb_ref, dw_ref, land_ref,
             dst, bsc, dq_sc, dk_sc, g_sc, dsh, a_sc, da_sc, acc, send_sems, recv_sems):
        del dproj_in
        n = pl.program_id(0)
        ride_start, ride_finish = _p3_to_sibling(p3_ref, land_ref, send_sems, recv_sems)

        @pl.when(n == 0)
        def _():
            ride_start()
            dst[...] = jnp.zeros_like(dst)
            dlb_ref[...] = jnp.zeros_like(dlb_ref)
            dsh[0, :, tm:tm + HALO, :] = jnp.zeros((N_CB, HALO, HEAD_W), F32)
            acc[...] = jnp.zeros_like(acc)

        @pl.when(n > 0)
        def _():
            dsh[0, :, tm:tm + HALO, :] = dsh[0, :, 0:HALO, :]

        _store_by_cb(dsh, (0,), slice(0, tm), dc_ref[...].astype(F32))
        _fill_shifts(dsh, tm)
        ga = pg_ref[:, 0:D].astype(F32)
        sb = _sigmoid(pg_ref[:, D:2 * D].astype(F32))
        a = ga * sb
        _store_by_cb(a_sc, (), slice(0, tm), a)

        def conv_unit(cb, st):
            cs = slice(cb * HEAD_W, (cb + 1) * HEAD_W)
            rows = slice(st * strip, (st + 1) * strip)
            a_s = a_sc[cb, rows, :]
            d_a = jnp.zeros((strip, HEAD_W), F32)
            for j in range(CONV_K):
                off = CONV_K - 1 - j
                lo = st * strip + 8 * (off // 8)
                slab = dsh[off % 8, cb, lo:lo + strip, :]
                d_a = d_a + w_ref[j:j + 1, cs] * slab
                acc[j, :, cs] += jnp.sum((a_s * slab).reshape(strip // 8, 8, HEAD_W), axis=0)
            da_sc[rows, cs] = d_a

        units = [(cb, st) for cb in range(N_CB) for st in range(n_strip)]

        def prep(ci):
            g = _gates(p_ref.at[pl.ds(ci * CHUNK, CHUNK)], lbl_ref, (last - n) * cps + ci, bsc.at[ci])
            g["q1"] = (g["q"] * g["e_q"]).astype(BF16)
            qm_f = g["q"] * g["e_qm"]
            km_f = g["k"] * g["e_km"]
            g["qm"] = qm_f.astype(BF16)
            g["km"] = km_f.astype(BF16)
            g["qm_lo"] = (qm_f - g["qm"].astype(F32)).astype(BF16)
            g["km_lo"] = (km_f - g["km"].astype(F32)).astype(BF16)
            g["kh_f"] = g["k"] * g["e_kh"]
            g["kh"] = g["kh_f"].astype(BF16)
            return g

        def heads_and_post(ci, g):
            rs = pl.ds(ci * CHUNK, CHUNK)
            pv = p_ref.at[rs]
            dpv = dp_ref.at[rs]
            q1, qm, km, qm_lo, km_lo, kh_f, kh = (g[k] for k in ("q1", "qm", "km", "qm_lo", "km_lo", "kh_f", "kh"))
            for h in range(HEADS):
                sl = slice(h * HEAD_W, (h + 1) * HEAD_W)
                v = pv[:, 2 * D + h * HEAD_W:2 * D + (h + 1) * HEAD_W]
                d_oh = do_ref[rs, sl].astype(BF16)
                s0 = s_ref[ci, h]
                ds_end = dst[h]
                ds_end_b = ds_end.astype(BF16)
                att = jnp.where(g["causal"], _dot_nt(qm[:, sl], km[:, sl]), 0.0).astype(BF16)
                d_att = jnp.where(g["causal"], _dot_nt(d_oh, v), 0.0).astype(BF16)
                d_v = _dot_tn(att, d_oh) + _dot_nt(kh[:, sl], ds_end_b)
                d_qm2 = _dot(d_att, jnp.concatenate([km[:, sl], km_lo[:, sl]], axis=1))
                d_qm = d_qm2[:, 0:HEAD_W] + d_qm2[:, HEAD_W:2 * HEAD_W]
                d_q1 = _dot(d_oh, s0.astype(BF16))
                d_km2 = _dot_tn(d_att, jnp.concatenate([qm[:, sl], qm_lo[:, sl]], axis=1))
                d_km = d_km2[:, 0:HEAD_W] + d_km2[:, HEAD_W:2 * HEAD_W]
                d_kh = _dot(v, ds_end_b)
                dq_sc[ci, :, sl] = d_qm * g["e_qm"][:, sl] + d_q1 * g["e_q"][:, sl]
                dk_sc[ci, :, sl] = d_km * g["e_km"][:, sl] + d_kh * g["e_kh"][:, sl]
                g_sc[ci, :, sl] = (jnp.sum(kh_f[:, sl] * d_kh, axis=0, keepdims=True)
                                   + g["e_last"][:, sl] * jnp.sum(ds_end * s0, axis=0, keepdims=True))
                dst[h] = ds_end * g["e_last"][:, sl] + _dot_tn(d_oh, q1[:, sl])
                dpv[:, 5 * D + h * HEAD_W:5 * D + (h + 1) * HEAD_W] = d_v.astype(BF16)
                if units:
                    conv_unit(*units.pop(0))
            d_q = dq_sc[ci]
            d_k = dk_sc[ci]
            d_b = g["q"] * d_q - g["k"] * d_k
            anti = jnp.logical_not(g["causal"]) | (lax.broadcasted_iota(jnp.int32, (CHUNK, CHUNK), 0)
                                                    == lax.broadcasted_iota(jnp.int32, (CHUNK, CHUNK), 1))
            d_lf = _tri_matmul(anti.astype(BF16), d_b) + g_sc[ci]
            d_f = jnp.where(g["valid"], d_lf / g["f"] - d_k, 0.0)
            sg = g["sg"]
            dlb_ref[0:1, :] += jnp.sum(d_f * (1.0 - sg), axis=0, keepdims=True)
            dpv[:, 3 * D:4 * D] = (d_q * _dsilu(g["q"], g["sq"])).astype(BF16)
            dpv[:, 4 * D:5 * D] = (d_f * (1.0 - g["lb"]) * sg * (1.0 - sg)).astype(BF16)

        ready = prep(cps - 1)
        for ci in reversed(range(cps)):
            coming = prep(ci - 1) if ci > 0 else None
            heads_and_post(ci, ready)
            ready = coming
        while units:
            conv_unit(*units.pop(0))

        d_a = da_sc[...]
        dp_ref[:, 0:D] = (d_a * sb).astype(BF16)
        dp_ref[:, D:2 * D] = (d_a * a * (1.0 - sb)).astype(BF16)
        dp_ref[:, 2 * D:3 * D] = dz_ref[...]

        @pl.when(n == last)
        def _():
            for j in range(CONV_K):
                dw_ref[j:j + 1, :] = jnp.sum(acc[j], axis=0, keepdims=True)
            dw_ref[CONV_K:CONV_K + 1, :] = jnp.zeros((1, D), F32)
            ride_finish()

    def rows_of(width, col):
        return pl.BlockSpec((tm, width), lambda n: (last - n, col))

    return pl.pallas_call(
        body, name="rec_conv_bwd", grid=(N_CHUNK // cps,),
        in_specs=[rows_of(3 * D, 1), pl.BlockSpec((2, D), lambda n: (0, 0)), rows_of(D, 0),
                  pl.BlockSpec((cps, HEADS, HEAD_W, HEAD_W), lambda n: (last - n, 0, 0, 0)),
                  rows_of(2 * D, 0), rows_of(D, 0), rows_of(D, 0), pl.BlockSpec((CONV_K, D), lambda n: (0, 0)), ANY, ANY],
        out_specs=(rows_of(6 * D, 0), pl.BlockSpec((8, D), lambda n: (0, 0)),
                   pl.BlockSpec((CONV_K + 1, D), lambda n: (0, 0)), ANY),
        out_shape=(jax.ShapeDtypeStruct((TP, D_IN), BF16), jax.ShapeDtypeStruct((8, D), F32),
                   jax.ShapeDtypeStruct((CONV_K + 1, D), F32), jax.ShapeDtypeStruct((4, 3, W_ROW_BLK, D), F32)),
        scratch_shapes=[pltpu.VMEM((HEADS, HEAD_W, HEAD_W), F32), pltpu.VMEM((cps, CHUNK, D), F32),
                        pltpu.VMEM((cps, CHUNK, D), F32), pltpu.VMEM((cps, CHUNK, D), F32),
                        pltpu.VMEM((cps, 1, D), F32),
                        pltpu.VMEM((8, N_CB, tm + HALO, HEAD_W), F32), pltpu.VMEM((N_CB, tm, HEAD_W), F32),
                        pltpu.VMEM((tm, D), F32), pltpu.VMEM((CONV_K, 8, D), F32),
                        pltpu.SemaphoreType.DMA((4,)), pltpu.SemaphoreType.DMA((4,))],
        input_output_aliases={8: 0},
        compiler_params=pltpu.CompilerParams(dimension_semantics=("arbitrary",)),
    )(proj, lb_logits, d_o, s_start, proj, d_c0, d_z, conv_w, dproj, p3)


def _wgrad3(a3, b3):
    tt = TT_WGRAD

    def body(a_ref, b_ref, o_ref):
        @pl.when(pl.program_id(1) == 0)
        def _():
            o_ref[...] = jnp.zeros_like(o_ref)

        o_ref[0] += _dot_tn(a_ref[0], b_ref[0])

    return pl.pallas_call(
        body, name="wgrad3", grid=(3, TP // tt),
        in_specs=[pl.BlockSpec((1, tt, D), lambda g, t: (g, t, 0)), pl.BlockSpec((1, tt, D), lambda g, t: (g, t, 0))],
        out_specs=pl.BlockSpec((1, D, D), lambda g, t: (g, 0, 0)),
        out_shape=jax.ShapeDtypeStruct((3, D, D), F32),
        compiler_params=pltpu.CompilerParams(dimension_semantics=("arbitrary", "arbitrary")),
    )(a3, b3)


def _wgrad_in(h, dproj, ids, chip1b):
    tt = TT_WGRAD
    n_t = TP // tt

    def body(ids_ref, a_ref, b_ref, c1_ref, o_ref, ob_ref, l0_ref, far_ref, acc, tmp, send_sems, recv_sems, tmp_sem,
             far_send_sems, far_recv_sems):
        del ids_ref
        r = pl.program_id(0)
        t = pl.program_id(1)
        x, y, c = _my_place()
        sibling = (x, y, 1 - c)
        slot = lax.rem(r, 2)
        ride_start, ride_finish = _partials_to_owners(c1_ref, far_ref, far_send_sems, far_recv_sems)

        @pl.when((r == 0) & (t == 0))
        def _():
            ride_start()

        def send_in(q):
            return pltpu.make_async_remote_copy(
                src_ref=acc.at[q % 2], dst_ref=l0_ref.at[q], send_sem=send_sems.at[q], recv_sem=recv_sems.at[q],
                device_id=sibling, device_id_type=MESH_ID)

        def landed(q):
            return pltpu.make_async_copy(l0_ref.at[q], tmp, tmp_sem)

        @pl.when(t == 0)
        def _():
            acc[slot] = jnp.zeros((D, W_IN_BLK), F32)

        acc[slot] += _dot_tn(a_ref[...], b_ref[...])

        for q in range(4):
            @pl.when((r == q) & (t == n_t - 1))
            def _(q=q):
                if q >= 1:
                    send_in(q - 1).wait_send()
                send_in(q).start()

            @pl.when((r == 4 + q) & (t == n_t - 2))
            def _(q=q):
                if q == 0:
                    send_in(3).wait_send()
                send_in(q).wait_recv()
                landed(q).start()

            @pl.when((r == 4 + q) & (t == n_t - 1))
            def _(q=q):
                landed(q).wait()
                tot = acc[q % 2] + tmp[...]
                o_ref[0] = tot
                ob_ref[0] = tot.astype(BF16)

        @pl.when((r == N_DEV - 1) & (t == n_t - 1))
        def _():
            ride_finish()

    blk = pl.BlockSpec((1, D, W_IN_BLK), lambda r, t, ids: (jnp.maximum(r - 4, 0), 0, 0))
    return pl.pallas_call(
        body, name="wgrad_in",
        grid_spec=pltpu.PrefetchScalarGridSpec(
            num_scalar_prefetch=1, grid=(N_DEV, n_t),
            in_specs=[pl.BlockSpec((tt, D), lambda r, t, ids: (t, 0)),
                      pl.BlockSpec((tt, W_IN_BLK), lambda r, t, ids: (t, ids[r])), ANY],
            out_specs=(blk, blk, ANY, ANY),
            scratch_shapes=[pltpu.VMEM((2, D, W_IN_BLK), F32), pltpu.VMEM((D, W_IN_BLK), F32),
                            pltpu.SemaphoreType.DMA((4,)), pltpu.SemaphoreType.DMA((4,)), pltpu.SemaphoreType.DMA,
                            pltpu.SemaphoreType.DMA((3,)), pltpu.SemaphoreType.DMA((3,))]),
        out_shape=(jax.ShapeDtypeStruct((4, D, W_IN_BLK), F32), jax.ShapeDtypeStruct((4, D, W_IN_BLK), BF16),
                   jax.ShapeDtypeStruct((4, D, W_IN_BLK), F32), jax.ShapeDtypeStruct((3, 3, W_ROW_BLK, D), BF16)),
        compiler_params=pltpu.CompilerParams(dimension_semantics=("arbitrary", "arbitrary")),
    )(ids, h, dproj, chip1b)


def _chip_sum_3(p3, land1, ids_mine):
    def body(ids_ref, p_ref, l_ref, o_ref, ob_ref):
        del ids_ref
        tot = p_ref[...] + l_ref[0]
        o_ref[0] = tot
        ob_ref[0] = tot.astype(BF16)

    blk = pl.BlockSpec((1, 3, W_ROW_BLK, D), lambda r, ids: (r, 0, 0, 0))
    return pl.pallas_call(
        body, name="chip_sum_3",
        grid_spec=pltpu.PrefetchScalarGridSpec(
            num_scalar_prefetch=1, grid=(4,),
            in_specs=[pl.BlockSpec((3, W_ROW_BLK, D), lambda r, ids: (0, ids[r], 0)), blk],
            out_specs=(blk, blk)),
        out_shape=(jax.ShapeDtypeStruct((4, 3, W_ROW_BLK, D), F32), jax.ShapeDtypeStruct((4, 3, W_ROW_BLK, D), BF16)),
    )(ids_mine, p3, land1)


def _dh_and_norm_bwd(dproj, w_in_full, xin, b3, norm_g, chip0b):
    tm = TM_MAT
    n_k = N_DEV // DH_K_BLKS
    n_m = TP // tm

    def body(dp_ref, w_ref, x_ref, dr_ref, g_ref, c0_ref, dx_ref, dg_ref, f0_ref, acc, send_sems, recv_sems):
        m = pl.program_id(0)
        k = pl.program_id(1)
        ride_start, ride_finish = _partials_to_owners(c0_ref, f0_ref, send_sems, recv_sems)

        @pl.when((m == 0) & (k == 0))
        def _():
            ride_start()

        @pl.when(k == 0)
        def _():
            acc[...] = jnp.zeros_like(acc)

        part = _dot_nt(dp_ref[:, 0:W_IN_BLK], w_ref[0])
        for j in range(1, DH_K_BLKS):
            part = part + _dot_nt(dp_ref[:, j * W_IN_BLK:(j + 1) * W_IN_BLK], w_ref[j])
        acc[...] += part

        @pl.when((k == n_k - 1) & (m == 0))
        def _():
            dg_ref[...] = jnp.zeros_like(dg_ref)

        @pl.when(k == n_k - 1)
        def _():
            xv = x_ref[...]
            r1 = lax.rsqrt(jnp.mean(xv * xv, axis=-1, keepdims=True) + EPS)
            xh = xv * r1
            d_h = acc[...]
            dg_ref[0:1, :] += jnp.sum(d_h * xh, axis=0, keepdims=True)
            d_xh = d_h * g_ref[...]
            dx_ref[...] = dr_ref[0].astype(F32) + r1 * (d_xh - xh * jnp.mean(d_xh * xh, axis=-1, keepdims=True))

        @pl.when((m == n_m - 1) & (k == n_k - 1))
        def _():
            ride_finish()

    return pl.pallas_call(
        body, name="dh_norm_bwd", grid=(n_m, n_k),
        in_specs=[pl.BlockSpec((tm, DH_K_BLKS * W_IN_BLK), lambda m, k: (m, k)),
                  pl.BlockSpec((DH_K_BLKS, D, W_IN_BLK), lambda m, k: (k, 0, 0)),
                  pl.BlockSpec((tm, D), lambda m, k: (m, 0)), pl.BlockSpec((1, tm, D), lambda m, k: (2, m, 0)),
                  pl.BlockSpec((1, D), lambda m, k: (0, 0)), ANY],
        out_specs=(pl.BlockSpec((tm, D), lambda m, k: (m, 0)), pl.BlockSpec((8, D), lambda m, k: (0, 0)), ANY),
        out_shape=(jax.ShapeDtypeStruct((TP, D), F32), jax.ShapeDtypeStruct((8, D), F32),
                   jax.ShapeDtypeStruct((3, D, W_IN_BLK), BF16)),
        scratch_shapes=[pltpu.VMEM((tm, D), F32), pltpu.SemaphoreType.DMA((3,)), pltpu.SemaphoreType.DMA((3,))],
        compiler_params=pltpu.CompilerParams(dimension_semantics=("arbitrary", "arbitrary")),
    )(dproj, w_in_full, xin, b3, norm_g, chip0b)


def _sum_adamw(own, landed, w, m, v, tr, name):
    rows, cols = w.shape
    n_t = rows // tr

    def body(o_ref, l1_ref, l2_ref, l3_ref, w_ref, m_ref, v_ref, g_ref, d_ref, m2_ref, v2_ref):
        g = ((o_ref[...] + l1_ref[...].astype(F32)) + l2_ref[...].astype(F32)) + l3_ref[...].astype(F32)
        delta, m2, v2 = _adamw(w_ref[...], g, m_ref[...], v_ref[...])
        g_ref[...] = g
        d_ref[...] = delta
        m2_ref[...] = m2
        v2_ref[...] = v2

    def spec(k):
        return pl.BlockSpec((tr, cols), lambda i: (i + k * n_t, 0))

    out = jax.ShapeDtypeStruct((rows, cols), F32)
    return pl.pallas_call(
        body, name=name, grid=(n_t,),
        in_specs=[spec(0), spec(0), spec(1), spec(2), spec(0), spec(0), spec(0)],
        out_specs=(spec(0),) * 4, out_shape=(out,) * 4,
    )(own, landed, landed, landed, w, m, v)


def _adamw_3(chip1, far1, ws, ms, vs):
    def body(c_ref, f_ref, *refs):
        w_refs, m_refs, v_refs, outs = refs[0:3], refs[3:6], refs[6:9], refs[9:21]
        for k in range(3):
            g = ((c_ref[0, k] + f_ref[0, k].astype(F32)) + f_ref[1, k].astype(F32)) + f_ref[2, k].astype(F32)
            delta, m2, v2 = _adamw(w_refs[k][0], g, m_refs[k][0], v_refs[k][0])
            for kind, val in enumerate((g, delta, m2, v2)):
                outs[3 * kind + k][0] = val

    full = pl.BlockSpec((1, W_ROW_BLK, D), lambda i: (0, 0, 0))
    out = jax.ShapeDtypeStruct((1, W_ROW_BLK, D), F32)
    res = pl.pallas_call(
        body, name="adamw_3", grid=(1,),
        in_specs=[pl.BlockSpec((1, 3, W_ROW_BLK, D), lambda i: (0, 0, 0, 0)),
                  pl.BlockSpec((3, 3, W_ROW_BLK, D), lambda i: (0, 0, 0, 0))] + [full] * 9,
        out_specs=(full,) * 12, out_shape=(out,) * 12,
    )(chip1, far1, *ws, *ms, *vs)
    return tuple(res[3 * kind:3 * kind + 3] for kind in range(4))


N_SMALL = 9


def _small_update(pack_all, srs_all, ws, ms, vs):
    def body(pk_ref, sr_ref, *refs):
        w_refs, m_refs, v_refs = refs[0:N_SMALL], refs[N_SMALL:2 * N_SMALL], refs[2 * N_SMALL:3 * N_SMALL]
        loss_ref = refs[3 * N_SMALL]
        outs = refs[3 * N_SMALL + 1:7 * N_SMALL + 1]
        tot_sc, tots_sc = refs[7 * N_SMALL + 1:]
        tot = pk_ref[0]
        tot_s = sr_ref[0]
        for d in range(1, N_DEV):
            tot = tot + pk_ref[d]
            tot_s = tot_s + sr_ref[d]
        tot_sc[...] = tot
        tots_sc[...] = tot_s
        loss_ref[...] = jnp.sum(tot_sc[5:6, :], axis=1, keepdims=True)
        lbl = w_refs[4]
        p0 = _sigmoid(lbl[0:1, :] - lbl[1:2, :])
        d_l0 = tot_sc[4:5, :] * p0 * (1.0 - p0)

        def update(k, sel, g):
            delta, m2, v2 = _adamw(w_refs[k][sel], g, m_refs[k][sel], v_refs[k][sel])
            for kind, val in enumerate((g, delta, m2, v2)):
                outs[N_SMALL * kind + k][sel] = val

        everything = (slice(None), slice(None))
        for k, row in ((0, 0), (1, 1), (2, 2), (3, 3), (5, 6), (6, 7)):
            update(k, everything, tot_sc[row:row + 1, :])
        update(4, (slice(0, 1), slice(None)), d_l0)
        update(4, (slice(1, 2), slice(None)), -d_l0)
        update(7, (0, slice(None), slice(None)), tots_sc[0:CONV_K, :])
        update(8, everything, tots_sc[META_ROW:META_ROW + N_META, :])

    shapes = [jax.ShapeDtypeStruct(w.shape, F32) for w in ws]
    res = pl.pallas_call(
        body, name="small_update",
        out_shape=(jax.ShapeDtypeStruct((1, 1), F32), *(shapes * 4)),
        scratch_shapes=[pltpu.VMEM((8, D), F32), pltpu.VMEM((SMALL_ROWS, HEAD_W), F32)],
    )(pack_all, srs_all, *ws, *ms, *vs)
    return res[0], tuple(res[1 + N_SMALL * kind:1 + N_SMALL * (kind + 1)] for kind in range(4))


def _local_step(xin, proj, target, conv_w_full, conv_b, ln_g, ln_b, w3_b, lb_logits, gnorm_g, final_g, ids_mine):
    fg = final_g.reshape(1, D)
    c0, o, s_start, w3_full = _rec_conv_fwd(proj, lb_logits, conv_w_full, conv_b, w3_b)
    d_o, d_c0, d_z, dproj, a3, b3, red = _mid(xin, target, o, c0, proj, w3_full, ln_g, ln_b, gnorm_g, fg)
    p3 = _wgrad3(a3, b3)
    dproj, dlb, d_conv_w, land1 = _rec_conv_bwd(proj, lb_logits, d_o, s_start, d_c0, d_z, conv_w_full, dproj, p3)
    chip1, chip1b = _chip_sum_3(p3, land1, ids_mine)
    return dproj, b3, p3, chip1, chip1b, d_conv_w, red, dlb


def kernel(x, meta_tokens, norm_g, w_in, conv_w, conv_b, ln_g, ln_b, w_conv_out, lb_logits, gnorm_g, w_rec_out, w_out, final_g, loss_target, m_meta_tokens, m_norm_g, m_w_in, m_conv_w, m_conv_b, m_ln_g, m_ln_b, m_w_conv_out, m_lb_logits, m_gnorm_g, m_w_rec_out, m_w_out, m_final_g, v_meta_tokens, v_norm_g, v_w_in, v_conv_w, v_conv_b, v_ln_g, v_ln_b, v_w_conv_out, v_lb_logits, v_gnorm_g, v_w_rec_out, v_w_out, v_final_g):
    mx, my, mc = _my_place()

    ws_s = jnp.concatenate([conv_w[0], jnp.zeros((1, HEAD_W), F32), meta_tokens], axis=0)
    small_full = jnp.transpose(_gather_small(ws_s), (1, 0, 2)).reshape(SMALL_ROWS, D)
    conv_w_full = small_full[0:CONV_K]
    meta_full = small_full[META_ROW:META_ROW + N_META]
    w_in_b, w3_b = _cast_shards(w_in[0], w_conv_out, w_rec_out, w_out)
    first, second, diag = _gather_chips(mx, my, mc)
    use_order = [(mx, my, mc), (mx, my, 1 - mc), (*first, mc), (*second, 1 - mc), (*second, mc), (*first, 1 - mc),
                 (*diag, mc), (*diag, 1 - mc)]
    order = jnp.stack([_dev_index(*p) for p in use_order]).astype(jnp.int32)
    proj, xin, h, w_in_full = _gather_and_proj(x[0], meta_full, norm_g, w_in_b, order)
    h = h.reshape(TP, D)

    ids_mine = jnp.stack([_dev_index(*_chip_rel(mx, my, r), mc) for r in range(4)]).astype(jnp.int32)
    ids_sib = jnp.stack([_dev_index(*_chip_rel(mx, my, r), 1 - mc) for r in range(4)]).astype(jnp.int32)
    dproj, b3, _, chip1, chip1b, d_conv_w, red, dlb = _local_step(
        xin, proj, loss_target[0], conv_w_full, conv_b, ln_g, ln_b, w3_b, lb_logits, gnorm_g, final_g, ids_mine)

    chip0, chip0b, _, far1 = _wgrad_in(h, dproj, jnp.concatenate([ids_sib, ids_mine]), chip1b)
    d_xin, dng, far0 = _dh_and_norm_bwd(dproj, w_in_full, xin, b3, norm_g, chip0b)
    pack = jnp.concatenate([dng[0:1], red[4:5], red[2:3], red[3:4], dlb[0:1], red[5:6], red[1:2], red[0:1]], axis=0)
    g_in, d_in, m_in, v_in = _sum_adamw(chip0.reshape(4 * D, W_IN_BLK), far0.reshape(3 * D, W_IN_BLK), w_in[0],
                                        m_w_in[0], v_w_in[0], 256, "adamw_in")
    big3 = _adamw_3(chip1, far1, (w_conv_out, w_rec_out, w_out), (m_w_conv_out, m_w_rec_out, m_w_out),
                    (v_w_conv_out, v_w_rec_out, v_w_out))

    srs = jnp.concatenate([d_conv_w, d_xin[PAD_FRONT:ROW0]], axis=0)
    srs = jnp.transpose(srs.reshape(SMALL_ROWS, N_DEV, HEAD_W), (1, 0, 2))
    pack_all, srs_all = _exchange_small(pack, srs)
    loss, small = _small_update(
        pack_all, srs_all,
        (norm_g, conv_b, ln_g, ln_b, lb_logits, gnorm_g, final_g.reshape(1, D), conv_w, meta_tokens),
        (m_norm_g, m_conv_b, m_ln_g, m_ln_b, m_lb_logits, m_gnorm_g, m_final_g.reshape(1, D), m_conv_w, m_meta_tokens),
        (v_norm_g, v_conv_b, v_ln_g, v_ln_b, v_lb_logits, v_gnorm_g, v_final_g.reshape(1, D), v_conv_w, v_meta_tokens))

    outs = [loss.reshape(()), d_xin[ROW0:][None]]
    for kind, a_in in enumerate((g_in, d_in, m_in, v_in)):
        ng, cb, lg, lb_, lbl, gg, fg, cw, mt = small[kind]
        a_3 = big3[kind]
        outs += [mt, ng, a_in[None], cw, cb, lg, lb_, a_3[0], lbl, gg, a_3[1], a_3[2], fg.reshape(D)]
    return tuple(outs)
```

```python
import jax
import jax.numpy as jnp
from jax import lax
from jax.experimental import pallas as pl
from jax.experimental.pallas import tpu as pltpu

F32 = jnp.float32
BF16 = jnp.bfloat16
ACT = BF16

D = 1024
SEQ = 4096
N_META = 16
CHUNK = 64
PAD_FRONT = 48
ROW0 = PAD_FRONT + N_META
TP = ROW0 + SEQ
N_CHUNK = TP // CHUNK
HEADS = 8
HEAD_W = 128
D_IN = 9 * D
N_DEV = 8
W_IN_BLK = D_IN // N_DEV
W_ROW_BLK = D // N_DEV
CONV_K = 31
SMALL_ROWS = 48
META_ROW = 32
EPS = 1e-6
HALO = 32

TM_MAT = 1040
TT_WGRAD = 2080
DH_K_BLKS = 2
TM_ELT = 208
CHUNKS_PER_STEP = 5
CONV_STRIPS = 5

ADAM_LR = 0.001
ADAM_B1 = 0.9
ADAM_B2 = 0.999
ADAM_EPS = 1e-08
ADAM_WD = 0.01
ADAM_STEP = 10

MESH_ID = pl.DeviceIdType.MESH
ANY = pl.BlockSpec(memory_space=pl.ANY)


def _sigmoid(v):
    return jax.nn.sigmoid(v)


def _dsilu(silu, s):
    return s + silu * (1.0 - s)


def _dot(a, b):
    return jnp.dot(a, b, preferred_element_type=F32)


def _dot_nt(a, b):
    return lax.dot_general(a, b, (((1,), (1,)), ((), ())), preferred_element_type=F32)


def _dot_tn(a, b):
    return lax.dot_general(a, b, (((0,), (0,)), ((), ())), preferred_element_type=F32)


def _tri_matmul(tri, v):
    hi = v.astype(BF16)
    lo = (v - hi.astype(F32)).astype(BF16)
    return _dot(tri, hi) + _dot(tri, lo)


def _adamw(w, g, m, v):
    m2 = ADAM_B1 * m + (1.0 - ADAM_B1) * g
    v2 = ADAM_B2 * v + (1.0 - ADAM_B2) * jnp.square(g)
    m_hat = m2 / (1.0 - ADAM_B1 ** ADAM_STEP)
    v_hat = v2 / (1.0 - ADAM_B2 ** ADAM_STEP)
    delta = -ADAM_LR * (m_hat / (jnp.sqrt(v_hat) + ADAM_EPS) + ADAM_WD * w)
    return delta, m2, v2


def _window_start(i, tm):
    assert tm % 16 == 0 and ROW0 % 16 == 0
    return pl.multiple_of(16 * jnp.maximum((tm // 16) * i - ROW0 // 16, 0), 16)


def _my_place():
    return lax.axis_index("x"), lax.axis_index("y"), lax.axis_index("c")


def _dev_index(px, py, pc):
    return 4 * px + 2 * py + pc


def _cast_shards(w_in_s, w_conv_s, w_rec_s, w_out_s):
    def body(a_ref, c_ref, r_ref, o_ref, oa_ref, ob_ref):
        oa_ref[...] = a_ref[...].astype(BF16)
        for k, ref in enumerate((c_ref, r_ref, o_ref)):
            ob_ref[k] = ref[0].astype(BF16)

    return pl.pallas_call(
        body, name="cast_shards",
        out_shape=(jax.ShapeDtypeStruct(w_in_s.shape, BF16), jax.ShapeDtypeStruct((3, W_ROW_BLK, D), BF16)),
    )(w_in_s, w_conv_s, w_rec_s, w_out_s)


def _peer(x, y, c, r):
    return (jnp.bitwise_xor(x, (r >> 2) & 1), jnp.bitwise_xor(y, (r >> 1) & 1), jnp.bitwise_xor(c, r & 1))


def _gather_small(small_s):
    def body(s_ref, o_ref, send_sems, recv_sems, local_sem):
        x, y, c = _my_place()
        my_id = _dev_index(x, y, c)
        mine = pltpu.make_async_copy(s_ref, o_ref.at[my_id], local_sem)
        mine.start()
        copies = []
        for r in range(1, N_DEV):
            cp = pltpu.make_async_remote_copy(
                src_ref=s_ref, dst_ref=o_ref.at[my_id], send_sem=send_sems.at[r - 1], recv_sem=recv_sems.at[r - 1],
                device_id=_peer(x, y, c, r), device_id_type=MESH_ID)
            cp.start()
            copies.append(cp)
        for cp in copies:
            cp.wait_recv()
        for cp in copies:
            cp.wait_send()
        mine.wait()

    return pl.pallas_call(
        body, name="gather_small", out_shape=jax.ShapeDtypeStruct((N_DEV,) + small_s.shape, F32),
        in_specs=[ANY], out_specs=ANY,
        scratch_shapes=[pltpu.SemaphoreType.DMA((7,)), pltpu.SemaphoreType.DMA((7,)), pltpu.SemaphoreType.DMA],
    )(small_s)


def _w3_gather(src, out, stage, send_sems, recv_sems, local_sems):
    x, y, c = _my_place()
    me, sibling = (x, y, c), (x, y, 1 - c)
    chips = [(1 - x, y), (x, 1 - y), (1 - x, 1 - y)]

    def block(place):
        d = _dev_index(*place)
        return out.at[:, pl.ds(pl.multiple_of(d * W_ROW_BLK, W_ROW_BLK), W_ROW_BLK), :]

    def copy(k, place, to, from_src=False):
        return pltpu.make_async_remote_copy(
            src_ref=src if from_src else block(place), dst_ref=block(place),
            send_sem=send_sems.at[k], recv_sem=recv_sems.at[k], device_id=to, device_id_type=MESH_ID)

    own_in = pltpu.make_async_copy(src, stage, local_sems.at[0])
    own_out = pltpu.make_async_copy(stage, block(me), local_sems.at[1])

    def start():
        copy(0, me, sibling, from_src=True).start()
        for j, chip in enumerate(chips):
            copy(1 + j, me, (*chip, c), from_src=True).start()
        own_in.start()
        own_in.wait()
        own_out.start()

    def finish():
        for j, chip in enumerate(chips):
            copy(1 + j, (*chip, c), me).wait_recv()
            copy(4 + j, (*chip, c), sibling).start()
        copy(0, sibling, me).wait_recv()
        for j, chip in enumerate(chips):
            copy(4 + j, (*chip, 1 - c), me).wait_recv()
        for k in range(7):
            copy(k, me, me).wait_send()
        own_out.wait()

    return start, finish


def _p3_to_sibling(p3_ref, land_ref, send_sems, recv_sems):
    x, y, c = _my_place()

    def cp(q):
        d = _dev_index(*_chip_rel(x, y, q), 1 - c)
        return pltpu.make_async_remote_copy(
            src_ref=p3_ref.at[:, pl.ds(pl.multiple_of(d * W_ROW_BLK, W_ROW_BLK), W_ROW_BLK), :],
            dst_ref=land_ref.at[q], send_sem=send_sems.at[q], recv_sem=recv_sems.at[q],
            device_id=(x, y, 1 - c), device_id_type=MESH_ID)

    def start():
        for q in range(4):
            cp(q).start()

    def finish():
        for q in range(4):
            cp(q).wait_recv()
        for q in range(4):
            cp(q).wait_send()

    return start, finish


def _partials_to_owners(src_ref, far_ref, send_sems, recv_sems):
    x, y, c = _my_place()

    def cp(q):
        return pltpu.make_async_remote_copy(
            src_ref=src_ref.at[q], dst_ref=far_ref.at[q - 1], send_sem=send_sems.at[q - 1],
            recv_sem=recv_sems.at[q - 1], device_id=(*_chip_rel(x, y, q), c), device_id_type=MESH_ID)

    def start():
        for q in range(1, 4):
            cp(q).start()

    def finish():
        for q in range(1, 4):
            cp(q).wait_recv()
        for q in range(1, 4):
            cp(q).wait_send()

    return start, finish


def _gather_chips(x, y, c):
    first = (jnp.bitwise_xor(x, 1 - c), jnp.bitwise_xor(y, c))
    second = (jnp.bitwise_xor(x, c), jnp.bitwise_xor(y, 1 - c))
    return [first, second, (1 - x, 1 - y)]


def _gather_and_proj(x_seq, meta_full, norm_g, w_in_b, order):
    tm = TM_MAT
    n_m = TP // tm
    last_m = n_m - 1

    def body(order_ref, x_ref, meta_ref, g_ref, s0, proj_ref, xin_ref, h_out, o0, hbuf, wbuf, send_sems, recv_sems,
             local_sems):
        del order_ref
        n = pl.program_id(0)
        m = pl.program_id(1)
        x, y, c = _my_place()
        me, sibling = (x, y, c), (x, y, 1 - c)
        chips = _gather_chips(x, y, c)

        def block(place):
            return o0.at[_dev_index(*place)]

        def copy(k, place, to, from_src=False):
            return pltpu.make_async_remote_copy(
                src_ref=s0 if from_src else block(place), dst_ref=block(place),
                send_sem=send_sems.at[k], recv_sem=recv_sems.at[k], device_id=to, device_id_type=MESH_ID)

        def to_vmem(place, slot):
            return pltpu.make_async_copy(block(place), wbuf.at[slot], local_sems.at[slot])

        own_out = pltpu.make_async_copy(wbuf.at[0], block(me), local_sems.at[2])
        h_copy = pltpu.make_async_copy(hbuf, h_out, local_sems.at[3])

        @pl.when((n == 0) & (m == 0))
        def _():
            copy(0, me, sibling, from_src=True).start()
            for j, chip in enumerate(chips[0:2]):
                copy(1 + j, me, (*chip, c), from_src=True).start()
            mine = pltpu.make_async_copy(s0, wbuf.at[0], local_sems.at[0])
            mine.start()
            mine.wait()
            own_out.start()

        @pl.when(n == 0)
        def _():
            xv = x_ref[...]
            xin_ref[...] = jnp.where(m == 0, pltpu.roll(xv, ROW0, 0), xv)

            @pl.when(m == 0)
            def _():
                xin_ref[0:PAD_FRONT, :] = jnp.zeros((PAD_FRONT, D), F32)
                xin_ref[PAD_FRONT:ROW0, :] = meta_ref[...]

            xv = xin_ref[...]
            r = lax.rsqrt(jnp.mean(xv * xv, axis=-1, keepdims=True) + EPS)
            hbuf[m] = (xv * r * g_ref[...]).astype(BF16)

        between = [4 + c, 5 - c, 6]
        first, second, diag = chips
        plan = [(sibling, (0, sibling), None),
                ((*first, c), (1, (*first, c)), between[0]),
                ((*second, 1 - c), (between[1], (*second, 1 - c)), None),
                ((*second, c), (2, (*second, c)), between[1]),
                ((*first, 1 - c), (between[0], (*first, 1 - c)), None),
                ((*diag, c), (3, (*diag, c)), between[2]),
                ((*diag, 1 - c), (between[2], (*diag, 1 - c)), None)]

        for s, (place, (k, origin), pass_on) in enumerate(plan, start=1):
            @pl.when((n == s - 1) & (m == last_m))
            def _(s=s, place=place, k=k, origin=origin, pass_on=pass_on):
                copy(k, origin, me).wait_recv()
                if pass_on is not None:
                    copy(pass_on, place, sibling).start()
                if s == 2:
                    copy(3, place, (*chips[1], c)).start()
                    own_out.wait()
                to_vmem(place, s % 2).start()

            @pl.when((n == s) & (m == 0))
            def _(s=s, place=place):
                to_vmem(place, s % 2).wait()

        proj_ref[...] = _dot(hbuf[m], wbuf[lax.rem(n, 2)]).astype(BF16)

        @pl.when((n == 0) & (m == last_m))
        def _():
            h_copy.start()

        @pl.when((n == N_DEV - 1) & (m == last_m))
        def _():
            for k in range(7):
                copy(k, me, me).wait_send()
            h_copy.wait()

    return pl.pallas_call(
        body, name="gather_and_proj",
        grid_spec=pltpu.PrefetchScalarGridSpec(
            num_scalar_prefetch=1, grid=(N_DEV, n_m),
            in_specs=[pl.BlockSpec((pl.Element(tm), pl.Element(D)),
                                   lambda n, m, o: (_window_start(jnp.where(n == 0, m, 0), tm), 0)),
                      pl.BlockSpec((N_META, D), lambda n, m, o: (0, 0)),
                      pl.BlockSpec((1, D), lambda n, m, o: (0, 0)), ANY],
            out_specs=(pl.BlockSpec((tm, W_IN_BLK), lambda n, m, o: (m, o[n])),
                       pl.BlockSpec((tm, D), lambda n, m, o: (jnp.where(n == 0, m, last_m), 0)), ANY, ANY),
            scratch_shapes=[pltpu.VMEM((n_m, tm, D), BF16), pltpu.VMEM((2, D, W_IN_BLK), BF16),
                            pltpu.SemaphoreType.DMA((7,)), pltpu.SemaphoreType.DMA((7,)),
                            pltpu.SemaphoreType.DMA((4,))]),
        out_shape=(jax.ShapeDtypeStruct((TP, D_IN), BF16), jax.ShapeDtypeStruct((TP, D), F32),
                   jax.ShapeDtypeStruct((n_m, tm, D), BF16), jax.ShapeDtypeStruct((N_DEV, D, W_IN_BLK), BF16)),
        compiler_params=pltpu.CompilerParams(dimension_semantics=("arbitrary", "arbitrary")),
    )(order, x_seq, meta_full, norm_g, w_in_b)


def _chip_rel(x, y, r):
    return (jnp.bitwise_xor(x, r >> 1), jnp.bitwise_xor(y, r & 1))


def _exchange_small(pack, srs):
    def body(pk, sr, pk_all, sr_all, send_sems, recv_sems, local_sems):
        x, y, c = _my_place()
        my_id = _dev_index(x, y, c)
        mine = [pltpu.make_async_copy(pk, pk_all.at[my_id], local_sems.at[0]),
                pltpu.make_async_copy(sr.at[my_id], sr_all.at[my_id], local_sems.at[1])]
        for cp in mine:
            cp.start()
        copies = []
        for r in range(1, N_DEV):
            peer = (jnp.bitwise_xor(x, (r >> 2) & 1), jnp.bitwise_xor(y, (r >> 1) & 1), jnp.bitwise_xor(c, r & 1))
            peer_id = _dev_index(*peer)
            for a, (src, dst) in enumerate(((pk, pk_all.at[my_id]), (sr.at[peer_id], sr_all.at[my_id]))):
                cp = pltpu.make_async_remote_copy(
                    src_ref=src, dst_ref=dst, send_sem=send_sems.at[a * 7 + r - 1], recv_sem=recv_sems.at[a * 7 + r - 1],
                    device_id=peer, device_id_type=MESH_ID)
                cp.start()
                copies.append(cp)
        for cp in copies:
            cp.wait_recv()
        for cp in copies:
            cp.wait_send()
        for cp in mine:
            cp.wait()

    return pl.pallas_call(
        body, name="exchange_small",
        out_shape=(jax.ShapeDtypeStruct((N_DEV,) + pack.shape, F32), jax.ShapeDtypeStruct(srs.shape, F32)),
        in_specs=[ANY, ANY], out_specs=(ANY, ANY),
        scratch_shapes=[pltpu.SemaphoreType.DMA((14,)), pltpu.SemaphoreType.DMA((14,)), pltpu.SemaphoreType.DMA((2,))],
    )(pack, srs)


N_CB = D // HEAD_W


def _store_by_cb(ref, idx, rows, val):
    for cb in range(N_CB):
        ref[(*idx, cb, rows, slice(None))] = val[:, cb * HEAD_W:(cb + 1) * HEAD_W]


def _fill_shifts(sh, tm):
    n = tm + HALO - 8
    for s in range(1, 8):
        for cb in range(N_CB):
            sh[s, cb, 0:n, :] = sh[0, cb, s:s + n, :]


def _gates(p_ref, lbl_ref, chunk, bsc):
    lb = _sigmoid(lbl_ref[0:1, :] - lbl_ref[1:2, :])
    q_raw = p_ref[:, 0:D].astype(F32)
    f_raw = p_ref[:, D:2 * D].astype(F32)
    sq = _sigmoid(q_raw)
    q = q_raw * sq
    sg = _sigmoid(f_raw)
    f = lb + (1.0 - lb) * sg
    row = lax.broadcasted_iota(jnp.int32, (CHUNK, 1), 0) + chunk * CHUNK
    valid = row >= PAD_FRONT
    lf = jnp.where(valid, jnp.log(f), 0.0)
    k = jnp.where(valid, 1.0 - f, 0.0)
    r_i = lax.broadcasted_iota(jnp.int32, (CHUNK, CHUNK), 0)
    c_i = lax.broadcasted_iota(jnp.int32, (CHUNK, CHUNK), 1)
    causal = r_i >= c_i
    bsc[...] = _tri_matmul(causal.astype(BF16), lf)
    b = bsc[...]
    b_mid = bsc[CHUNK // 2 - 1:CHUNK // 2, :]
    b_last = bsc[CHUNK - 1:CHUNK, :]
    e_q = jnp.exp(b)
    e_qm = jnp.exp(b - b_mid)
    e_km = jnp.exp(b_mid - b)
    e_kh = jnp.exp(b_last - b)
    e_last = jnp.exp(b_last)
    return dict(lb=lb, q_raw=q_raw, sq=sq, q=q, sg=sg, f=f, k=k, valid=valid, causal=causal,
                e_q=e_q, e_qm=e_qm, e_km=e_km, e_kh=e_kh, e_last=e_last)


def _rec_conv_fwd(proj, lb_logits, conv_w, conv_b, w3_b):
    cps = CHUNKS_PER_STEP
    tm = cps * CHUNK
    n_strip = CONV_STRIPS
    strip = tm // n_strip

    def body(p_ref, lbl_ref, pg_ref, w_ref, b_ref, w3s_ref, c0_ref, o_ref, s_ref, att_ref, w3o_ref,
             st, bsc, sh, c0_sc, w3buf, send_sems, recv_sems, local_sems):
        n = pl.program_id(0)
        gather_start, gather_finish = _w3_gather(w3s_ref, w3o_ref, w3buf, send_sems, recv_sems, local_sems)

        @pl.when(n == 0)
        def _():
            st[...] = jnp.zeros_like(st)
            sh[0, :, 0:HALO, :] = jnp.zeros((N_CB, HALO, HEAD_W), F32)
            gather_start()

        @pl.when(n > 0)
        def _():
            sh[0, :, 0:HALO, :] = sh[0, :, tm:tm + HALO, :]

        ga = pg_ref[:, 0:D].astype(F32)
        gb = pg_ref[:, D:2 * D].astype(F32)
        _store_by_cb(sh, (0,), slice(HALO, HALO + tm), ga * _sigmoid(gb))
        _fill_shifts(sh, tm)

        def conv_unit(cb, s_i):
            cs = slice(cb * HEAD_W, (cb + 1) * HEAD_W)
            acc = jnp.broadcast_to(b_ref[:, cs], (strip, HEAD_W))
            for j in range(CONV_K):
                off = HALO - (CONV_K - 1) + j
                lo = s_i * strip + 8 * (off // 8)
                acc = acc + w_ref[j:j + 1, cs] * sh[off % 8, cb, lo:lo + strip, :]
            c0_sc[s_i * strip:(s_i + 1) * strip, cs] = acc

        units = [(cb, s_i) for cb in range(N_CB) for s_i in range(n_strip)]

        def prep(ci):
            g = _gates(p_ref.at[pl.ds(ci * CHUNK, CHUNK)], lbl_ref, n * cps + ci, bsc.at[ci])
            g["q1"] = (g["q"] * g["e_q"]).astype(BF16)
            g["qm"] = (g["q"] * g["e_qm"]).astype(BF16)
            g["km"] = (g["k"] * g["e_km"]).astype(BF16)
            g["kh"] = (g["k"] * g["e_kh"]).astype(BF16)
            return g

        def heads(ci, g):
            rs = pl.ds(ci * CHUNK, CHUNK)
            pv = p_ref.at[rs]
            s_ref[ci] = st[...]
            for h in range(HEADS):
                if units:
                    conv_unit(*units.pop(0))
                sl = slice(h * HEAD_W, (h + 1) * HEAD_W)
                v = pv[:, 2 * D + h * HEAD_W:2 * D + (h + 1) * HEAD_W]
                att = jnp.where(g["causal"], _dot_nt(g["qm"][:, sl], g["km"][:, sl]), 0.0).astype(BF16)
                att_ref[ci, h] = att
                s_h = st[h]
                o_ref[rs, sl] = (_dot_nt(g["q1"][:, sl], s_h.astype(BF16)) + _dot(att, v)).astype(ACT)
                st[h] = s_h * g["e_last"][:, sl] + _dot_tn(v, g["kh"][:, sl])

        ready = prep(0)
        for ci in range(cps):
            coming = prep(ci + 1) if ci + 1 < cps else None
            heads(ci, ready)
            ready = coming
        while units:
            conv_unit(*units.pop(0))
        c0_ref[...] = c0_sc[...].astype(ACT)

        @pl.when(n == N_CHUNK // cps - 1)
        def _():
            gather_finish()

    def rows_of(width, col):
        return pl.BlockSpec((tm, width), lambda n: (n, col))

    return pl.pallas_call(
        body, name="rec_conv_fwd", grid=(N_CHUNK // cps,),
        in_specs=[rows_of(3 * D, 1), pl.BlockSpec((2, D), lambda n: (0, 0)), rows_of(2 * D, 0),
                  pl.BlockSpec((CONV_K, D), lambda n: (0, 0)), pl.BlockSpec((1, D), lambda n: (0, 0)), ANY],
        out_specs=(rows_of(D, 0), rows_of(D, 0), pl.BlockSpec((cps, HEADS, HEAD_W, HEAD_W), lambda n: (n, 0, 0, 0)),
                   pl.BlockSpec((cps, HEADS, CHUNK, CHUNK), lambda n: (n, 0, 0, 0)), ANY),
        out_shape=(jax.ShapeDtypeStruct((TP, D), ACT), jax.ShapeDtypeStruct((TP, D), ACT),
                   jax.ShapeDtypeStruct((N_CHUNK, HEADS, HEAD_W, HEAD_W), F32),
                   jax.ShapeDtypeStruct((N_CHUNK, HEADS, CHUNK, CHUNK), BF16), jax.ShapeDtypeStruct((3, D, D), BF16)),
        scratch_shapes=[pltpu.VMEM((HEADS, HEAD_W, HEAD_W), F32), pltpu.VMEM((cps, CHUNK, D), F32),
                        pltpu.VMEM((8, N_CB, HALO + tm, HEAD_W), F32), pltpu.VMEM((tm, D), F32),
                        pltpu.VMEM((3, W_ROW_BLK, D), BF16), pltpu.SemaphoreType.DMA((7,)),
                        pltpu.SemaphoreType.DMA((7,)), pltpu.SemaphoreType.DMA((2,))],
        compiler_params=pltpu.CompilerParams(dimension_semantics=("arbitrary",)),
    )(proj, lb_logits, proj, conv_w, conv_b, w3_b)


def _mid(xin, tgt, o, c0, proj, w3, ln_g, ln_b, gnorm_g, final_g):
    tm = TM_ELT

    def body(x_ref, t_ref, o_ref, c0_ref, z_ref, gr_ref, mc_ref, mr_ref, w_ref, lng_ref, lnb_ref, gng_ref, fg_ref,
             do_ref, dc0_ref, dz_ref, dp_ref, a3_ref, b3_ref, red_ref, on_sc, don_sc):
        i = pl.program_id(0)

        @pl.when(i == 0)
        def _():
            red_ref[...] = jnp.zeros_like(red_ref)

        w_conv, w_rec, w_out = w_ref[0], w_ref[1], w_ref[2]
        c0v = c0_ref[...].astype(F32)
        mu = jnp.mean(c0v, axis=-1, keepdims=True)
        xc = c0v - mu
        rstd = lax.rsqrt(jnp.mean(xc * xc, axis=-1, keepdims=True) + EPS)
        xh = xc * rstd
        c1 = xh * lng_ref[...] + lnb_ref[...]
        s1 = _sigmoid(c1)
        c2 = c1 * s1
        z = z_ref[...].astype(F32)
        sz = _sigmoid(z)
        silu_z = z * sz
        u_conv = (c2 * silu_z).astype(BF16)
        y_conv = _dot(u_conv, w_conv)
        ov = o_ref[...].astype(F32)
        r3 = []
        for h in range(HEADS):
            sl = slice(h * HEAD_W, (h + 1) * HEAD_W)
            oh = ov[:, sl]
            r_h = lax.rsqrt(jnp.mean(oh * oh, axis=-1, keepdims=True) + EPS)
            r3.append(r_h)
            on_sc[:, sl] = oh * r_h
        o_n = on_sc[...]
        o_g = o_n * gng_ref[...]
        gr = gr_ref[...].astype(F32)
        sgr = _sigmoid(gr)
        silu_g = gr * sgr
        u_rec = (o_g * silu_g).astype(BF16)
        y_rec = _dot(u_rec, w_rec)
        mc = mc_ref[...].astype(F32)
        mr = mr_ref[...].astype(F32)
        smc = _sigmoid(mc)
        smr = _sigmoid(mr)
        merged = (smc * y_conv + smr * y_rec).astype(BF16)
        res = x_ref[...] + _dot(merged, w_out)
        r2 = lax.rsqrt(jnp.mean(res * res, axis=-1, keepdims=True) + EPS)
        xh2 = res * r2
        row = lax.broadcasted_iota(jnp.int32, (tm, 1), 0) + i * tm
        real = row >= ROW0
        tgt = t_ref[...]
        tgt = jnp.where(i == 0, pltpu.roll(tgt, ROW0, 0), tgt)
        diff = jnp.where(real, xh2 * fg_ref[...] - tgt, 0.0)
        d_y = diff * (1.0 / D)
        d_xh2 = d_y * fg_ref[...]
        d_res = r2 * (d_xh2 - xh2 * jnp.mean(d_xh2 * xh2, axis=-1, keepdims=True))
        d_res_b = d_res.astype(BF16)
        d_merged = _dot_nt(d_res_b, w_out)
        d_yc_f = d_merged * smc
        d_yr_f = d_merged * smr
        d_yc = d_yc_f.astype(BF16)
        d_yr = d_yr_f.astype(BF16)
        dp_ref[:, D:2 * D] = (d_yc_f * y_conv * (1.0 - smc)).astype(BF16)
        dp_ref[:, 2 * D:3 * D] = (d_yr_f * y_rec * (1.0 - smr)).astype(BF16)
        d_ur = _dot_nt(d_yr, w_rec)
        d_og = d_ur * silu_g
        dp_ref[:, 0:D] = (d_ur * o_g * _dsilu(silu_g, sgr)).astype(BF16)
        d_on = d_og * gng_ref[...]
        for h in range(HEADS):
            sl = slice(h * HEAD_W, (h + 1) * HEAD_W)
            d_h = d_on[:, sl]
            n_h = o_n[:, sl]
            don_sc[:, sl] = r3[h] * (d_h - n_h * jnp.mean(d_h * n_h, axis=-1, keepdims=True))
        do_ref[...] = don_sc[...].astype(ACT)
        d_uc = _dot_nt(d_yc, w_conv)
        d_c2 = d_uc * silu_z
        dz_ref[...] = (d_uc * c2 * _dsilu(silu_z, sz)).astype(BF16)
        d_c1 = d_c2 * _dsilu(c2, s1)
        d_xh = d_c1 * lng_ref[...]
        d_c0 = rstd * (d_xh - jnp.mean(d_xh, axis=-1, keepdims=True)
                       - xh * jnp.mean(d_xh * xh, axis=-1, keepdims=True))
        dc0_ref[...] = d_c0.astype(ACT)
        a3_ref[0] = u_conv
        b3_ref[0] = d_yc
        a3_ref[1] = u_rec
        b3_ref[1] = d_yr
        a3_ref[2] = merged
        b3_ref[2] = d_res_b
        def colsum(vv):
            return jnp.sum(vv, axis=0, keepdims=True)

        red_ref[0:1, :] += colsum(d_y * xh2)
        red_ref[1:2, :] += colsum(d_og * o_n)
        red_ref[2:3, :] += colsum(d_c1 * xh)
        red_ref[3:4, :] += colsum(d_c1)
        red_ref[4:5, :] += colsum(d_c0)
        red_ref[5:6, :] += colsum(diff * diff) * (0.5 / D)

    def row_block(width, col):
        return pl.BlockSpec((tm, width), lambda i: (i, col))

    def const_block(shape):
        return pl.BlockSpec(shape, lambda i: (0,) * len(shape))

    stack = jax.ShapeDtypeStruct((3, TP, D), BF16)
    stack_spec = pl.BlockSpec((3, tm, D), lambda i: (0, i, 0))
    return pl.pallas_call(
        body, name="mid", grid=(TP // tm,),
        in_specs=[row_block(D, 0),
                  pl.BlockSpec((pl.Element(tm), pl.Element(D)), lambda i: (_window_start(i, tm), 0)),
                  row_block(D, 0), row_block(D, 0),
                  row_block(D, 2), row_block(D, 6), row_block(D, 7), row_block(D, 8),
                  pl.BlockSpec((3, D, D), lambda i: (0, 0, 0), pipeline_mode=pl.Buffered(1)),
                  const_block((1, D)), const_block((1, D)), const_block((1, D)), const_block((1, D))],
        out_specs=(row_block(D, 0), row_block(D, 0), row_block(D, 0), row_block(3 * D, 2),
                   stack_spec, stack_spec, const_block((8, D))),
        out_shape=(jax.ShapeDtypeStruct((TP, D), ACT), jax.ShapeDtypeStruct((TP, D), ACT),
                   jax.ShapeDtypeStruct((TP, D), BF16),
                   jax.ShapeDtypeStruct((TP, D_IN), BF16), stack, stack, jax.ShapeDtypeStruct((8, D), F32)),
        scratch_shapes=[pltpu.VMEM((tm, D), F32), pltpu.VMEM((tm, D), F32)],
        compiler_params=pltpu.CompilerParams(dimension_semantics=("arbitrary",), vmem_limit_bytes=60 * 1024 * 1024),
    )(xin, tgt, o, c0, proj, proj, proj, proj, w3, ln_g, ln_b, gnorm_g, final_g)


def _rec_conv_bwd(proj, lb_logits, d_o, s_start, att, d_c0, d_z, conv_w, dproj, p3):
    cps = CHUNKS_PER_STEP
    tm = cps * CHUNK
    last = N_CHUNK // cps - 1
    n_strip = CONV_STRIPS
    strip = tm // n_strip

    def body(p_ref, lbl_ref, do_ref, s_ref, pg_ref, dc_ref, dz_ref, w_ref, dproj_in, p3_ref, att_ref,
             dp_ref, dlb_ref, dw_ref, land_ref,
             dst, bsc, dq_sc, dk_sc, g_sc, dsh, a_sc, da_sc, acc, send_sems, recv_sems):
        del dproj_in
        n = pl.program_id(0)
        ride_start, ride_finish = _p3_to_sibling(p3_ref, land_ref, send_sems, recv_sems)

        @pl.when(n == 0)
        def _():
            ride_start()
            dst[...] = jnp.zeros_like(dst)
            dlb_ref[...] = jnp.zeros_like(dlb_ref)
            dsh[0, :, tm:tm + HALO, :] = jnp.zeros((N_CB, HALO, HEAD_W), F32)
            acc[...] = jnp.zeros_like(acc)

        @pl.when(n > 0)
        def _():
            dsh[0, :, tm:tm + HALO, :] = dsh[0, :, 0:HALO, :]

        _store_by_cb(dsh, (0,), slice(0, tm), dc_ref[...].astype(F32))
        _fill_shifts(dsh, tm)
        ga = pg_ref[:, 0:D].astype(F32)
        sb = _sigmoid(pg_ref[:, D:2 * D].astype(F32))
        a = ga * sb
        _store_by_cb(a_sc, (), slice(0, tm), a)

        def conv_unit(cb, st):
            cs = slice(cb * HEAD_W, (cb + 1) * HEAD_W)
            rows = slice(st * strip, (st + 1) * strip)
            a_s = a_sc[cb, rows, :]
            d_a = jnp.zeros((strip, HEAD_W), F32)
            for j in range(CONV_K):
                off = CONV_K - 1 - j
                lo = st * strip + 8 * (off // 8)
                slab = dsh[off % 8, cb, lo:lo + strip, :]
                d_a = d_a + w_ref[j:j + 1, cs] * slab
                acc[j, :, cs] += jnp.sum((a_s * slab).reshape(strip // 8, 8, HEAD_W), axis=0)
            da_sc[rows, cs] = d_a

        units = [(cb, st) for cb in range(N_CB) for st in range(n_strip)]

        def prep(ci):
            g = _gates(p_ref.at[pl.ds(ci * CHUNK, CHUNK)], lbl_ref, (last - n) * cps + ci, bsc.at[ci])
            g["q1"] = (g["q"] * g["e_q"]).astype(BF16)
            qm_f = g["q"] * g["e_qm"]
            km_f = g["k"] * g["e_km"]
            g["qm"] = qm_f.astype(BF16)
            g["km"] = km_f.astype(BF16)
            g["qm_lo"] = (qm_f - g["qm"].astype(F32)).astype(BF16)
            g["km_lo"] = (km_f - g["km"].astype(F32)).astype(BF16)
            g["kh_f"] = g["k"] * g["e_kh"]
            g["kh"] = g["kh_f"].astype(BF16)
            return g

        def heads_and_post(ci, g):
            rs = pl.ds(ci * CHUNK, CHUNK)
            pv = p_ref.at[rs]
            dpv = dp_ref.at[rs]
            q1, qm, km, qm_lo, km_lo, kh_f, kh = (g[k] for k in ("q1", "qm", "km", "qm_lo", "km_lo", "kh_f", "kh"))
            for h in range(HEADS):
                if units:
                    conv_unit(*units.pop(0))
                sl = slice(h * HEAD_W, (h + 1) * HEAD_W)
                v = pv[:, 2 * D + h * HEAD_W:2 * D + (h + 1) * HEAD_W]
                d_oh = do_ref[rs, sl].astype(BF16)
                s0 = s_ref[ci, h]
                ds_end = dst[h]
                ds_end_b = ds_end.astype(BF16)
                att = att_ref[ci, h]
                d_att = jnp.where(g["causal"], _dot_nt(d_oh, v), 0.0).astype(BF16)
                d_v = _dot_tn(att, d_oh) + _dot_nt(kh[:, sl], ds_end_b)
                d_qm2 = _dot(d_att, jnp.concatenate([km[:, sl], km_lo[:, sl]], axis=1))
                d_qm = d_qm2[:, 0:HEAD_W] + d_qm2[:, HEAD_W:2 * HEAD_W]
                d_q1 = _dot(d_oh, s0.astype(BF16))
                d_km2 = _dot_tn(d_att, jnp.concatenate([qm[:, sl], qm_lo[:, sl]], axis=1))
                d_km = d_km2[:, 0:HEAD_W] + d_km2[:, HEAD_W:2 * HEAD_W]
                d_kh = _dot(v, ds_end_b)
                dq_sc[ci, :, sl] = d_qm * g["e_qm"][:, sl] + d_q1 * g["e_q"][:, sl]
                dk_sc[ci, :, sl] = d_km * g["e_km"][:, sl] + d_kh * g["e_kh"][:, sl]
                g_sc[ci, :, sl] = (jnp.sum(kh_f[:, sl] * d_kh, axis=0, keepdims=True)
                                   + g["e_last"][:, sl] * jnp.sum(ds_end * s0, axis=0, keepdims=True))
                dst[h] = ds_end * g["e_last"][:, sl] + _dot_tn(d_oh, q1[:, sl])
                dpv[:, 5 * D + h * HEAD_W:5 * D + (h + 1) * HEAD_W] = d_v.astype(BF16)
            d_q = dq_sc[ci]
            d_k = dk_sc[ci]
            d_b = g["q"] * d_q - g["k"] * d_k
            anti = jnp.logical_not(g["causal"]) | (lax.broadcasted_iota(jnp.int32, (CHUNK, CHUNK), 0)
                                                    == lax.broadcasted_iota(jnp.int32, (CHUNK, CHUNK), 1))
            d_lf = _tri_matmul(anti.astype(BF16), d_b) + g_sc[ci]
            d_f = jnp.where(g["valid"], d_lf / g["f"] - d_k, 0.0)
            sg = g["sg"]
            dlb_ref[0:1, :] += jnp.sum(d_f * (1.0 - sg), axis=0, keepdims=True)
            dpv[:, 3 * D:4 * D] = (d_q * _dsilu(g["q"], g["sq"])).astype(BF16)
            dpv[:, 4 * D:5 * D] = (d_f * (1.0 - g["lb"]) * sg * (1.0 - sg)).astype(BF16)

        ready = prep(cps - 1)
        for ci in reversed(range(cps)):
            coming = prep(ci - 1) if ci > 0 else None
            heads_and_post(ci, ready)
            ready = coming
        while units:
            conv_unit(*units.pop(0))

        d_a = da_sc[...]
        dp_ref[:, 0:D] = (d_a * sb).astype(BF16)
        dp_ref[:, D:2 * D] = (d_a * a * (1.0 - sb)).astype(BF16)
        dp_ref[:, 2 * D:3 * D] = dz_ref[...]

        @pl.when(n == last)
        def _():
            for j in range(CONV_K):
                dw_ref[j:j + 1, :] = jnp.sum(acc[j], axis=0, keepdims=True)
            dw_ref[CONV_K:CONV_K + 1, :] = jnp.zeros((1, D), F32)
            ride_finish()

    def rows_of(width, col):
        return pl.BlockSpec((tm, width), lambda n: (last - n, col))

    return pl.pallas_call(
        body, name="rec_conv_bwd", grid=(N_CHUNK // cps,),
        in_specs=[rows_of(3 * D, 1), pl.BlockSpec((2, D), lambda n: (0, 0)), rows_of(D, 0),
                  pl.BlockSpec((cps, HEADS, HEAD_W, HEAD_W), lambda n: (last - n, 0, 0, 0)),
                  rows_of(2 * D, 0), rows_of(D, 0), rows_of(D, 0), pl.BlockSpec((CONV_K, D), lambda n: (0, 0)), ANY, ANY,
                  pl.BlockSpec((cps, HEADS, CHUNK, CHUNK), lambda n: (last - n, 0, 0, 0))],
        out_specs=(rows_of(6 * D, 0), pl.BlockSpec((8, D), lambda n: (0, 0)),
                   pl.BlockSpec((CONV_K + 1, D), lambda n: (0, 0)), ANY),
        out_shape=(jax.ShapeDtypeStruct((TP, D_IN), BF16), jax.ShapeDtypeStruct((8, D), F32),
                   jax.ShapeDtypeStruct((CONV_K + 1, D), F32), jax.ShapeDtypeStruct((4, 3, W_ROW_BLK, D), F32)),
        scratch_shapes=[pltpu.VMEM((HEADS, HEAD_W, HEAD_W), F32), pltpu.VMEM((cps, CHUNK, D), F32),
                        pltpu.VMEM((cps, CHUNK, D), F32), pltpu.VMEM((cps, CHUNK, D), F32),
                        pltpu.VMEM((cps, 1, D), F32),
                        pltpu.VMEM((8, N_CB, tm + HALO, HEAD_W), F32), pltpu.VMEM((N_CB, tm, HEAD_W), F32),
                        pltpu.VMEM((tm, D), F32), pltpu.VMEM((CONV_K, 8, D), F32),
                        pltpu.SemaphoreType.DMA((4,)), pltpu.SemaphoreType.DMA((4,))],
        input_output_aliases={8: 0},
        compiler_params=pltpu.CompilerParams(dimension_semantics=("arbitrary",)),
    )(proj, lb_logits, d_o, s_start, proj, d_c0, d_z, conv_w, dproj, p3, att)


def _wgrad3(a3, b3):
    tt = TT_WGRAD

    def body(a_ref, b_ref, o_ref):
        @pl.when(pl.program_id(1) == 0)
        def _():
            o_ref[...] = jnp.zeros_like(o_ref)

        o_ref[0] += _dot_tn(a_ref[0], b_ref[0])

    return pl.pallas_call(
        body, name="wgrad3", grid=(3, TP // tt),
        in_specs=[pl.BlockSpec((1, tt, D), lambda g, t: (g, t, 0)), pl.BlockSpec((1, tt, D), lambda g, t: (g, t, 0))],
        out_specs=pl.BlockSpec((1, D, D), lambda g, t: (g, 0, 0)),
        out_shape=jax.ShapeDtypeStruct((3, D, D), F32),
        compiler_params=pltpu.CompilerParams(dimension_semantics=("arbitrary", "arbitrary")),
    )(a3, b3)


def _wgrad_in(h, dproj, ids, chip1b):
    tt = TT_WGRAD
    n_t = TP // tt

    def body(ids_ref, a_ref, b_ref, c1_ref, o_ref, ob_ref, l0_ref, far_ref, acc, tmp, send_sems, recv_sems, tmp_sem,
             far_send_sems, far_recv_sems):
        del ids_ref
        r = pl.program_id(0)
        t = pl.program_id(1)
        x, y, c = _my_place()
        sibling = (x, y, 1 - c)
        slot = lax.rem(r, 2)
        ride_start, ride_finish = _partials_to_owners(c1_ref, far_ref, far_send_sems, far_recv_sems)

        @pl.when((r == 0) & (t == 0))
        def _():
            ride_start()

        def send_in(q):
            return pltpu.make_async_remote_copy(
                src_ref=acc.at[q % 2], dst_ref=l0_ref.at[q], send_sem=send_sems.at[q], recv_sem=recv_sems.at[q],
                device_id=sibling, device_id_type=MESH_ID)

        def landed(q):
            return pltpu.make_async_copy(l0_ref.at[q], tmp, tmp_sem)

        @pl.when(t == 0)
        def _():
            acc[slot] = jnp.zeros((D, W_IN_BLK), F32)

        acc[slot] += _dot_tn(a_ref[...], b_ref[...])

        for q in range(4):
            @pl.when((r == q) & (t == n_t - 1))
            def _(q=q):
                if q >= 1:
                    send_in(q - 1).wait_send()
                send_in(q).start()

            @pl.when((r == 4 + q) & (t == n_t - 2))
            def _(q=q):
                if q == 0:
                    send_in(3).wait_send()
                send_in(q).wait_recv()
                landed(q).start()

            @pl.when((r == 4 + q) & (t == n_t - 1))
            def _(q=q):
                landed(q).wait()
                tot = acc[q % 2] + tmp[...]
                o_ref[0] = tot
                ob_ref[0] = tot.astype(BF16)

        @pl.when((r == N_DEV - 1) & (t == n_t - 1))
        def _():
            ride_finish()

    blk = pl.BlockSpec((1, D, W_IN_BLK), lambda r, t, ids: (jnp.maximum(r - 4, 0), 0, 0))
    return pl.pallas_call(
        body, name="wgrad_in",
        grid_spec=pltpu.PrefetchScalarGridSpec(
            num_scalar_prefetch=1, grid=(N_DEV, n_t),
            in_specs=[pl.BlockSpec((tt, D), lambda r, t, ids: (t, 0)),
                      pl.BlockSpec((tt, W_IN_BLK), lambda r, t, ids: (t, ids[r])), ANY],
            out_specs=(blk, blk, ANY, ANY),
            scratch_shapes=[pltpu.VMEM((2, D, W_IN_BLK), F32), pltpu.VMEM((D, W_IN_BLK), F32),
                            pltpu.SemaphoreType.DMA((4,)), pltpu.SemaphoreType.DMA((4,)), pltpu.SemaphoreType.DMA,
                            pltpu.SemaphoreType.DMA((3,)), pltpu.SemaphoreType.DMA((3,))]),
        out_shape=(jax.ShapeDtypeStruct((4, D, W_IN_BLK), F32), jax.ShapeDtypeStruct((4, D, W_IN_BLK), BF16),
                   jax.ShapeDtypeStruct((4, D, W_IN_BLK), F32), jax.ShapeDtypeStruct((3, 3, W_ROW_BLK, D), BF16)),
        compiler_params=pltpu.CompilerParams(dimension_semantics=("arbitrary", "arbitrary")),
    )(ids, h, dproj, chip1b)


def _chip_sum_3(p3, land1, ids_mine):
    def body(ids_ref, p_ref, l_ref, o_ref, ob_ref):
        del ids_ref
        tot = p_ref[...] + l_ref[0]
        o_ref[0] = tot
        ob_ref[0] = tot.astype(BF16)

    blk = pl.BlockSpec((1, 3, W_ROW_BLK, D), lambda r, ids: (r, 0, 0, 0))
    return pl.pallas_call(
        body, name="chip_sum_3",
        grid_spec=pltpu.PrefetchScalarGridSpec(
            num_scalar_prefetch=1, grid=(4,),
            in_specs=[pl.BlockSpec((3, W_ROW_BLK, D), lambda r, ids: (0, ids[r], 0)), blk],
            out_specs=(blk, blk)),
        out_shape=(jax.ShapeDtypeStruct((4, 3, W_ROW_BLK, D), F32), jax.ShapeDtypeStruct((4, 3, W_ROW_BLK, D), BF16)),
    )(ids_mine, p3, land1)


def _dh_and_norm_bwd(dproj, w_in_full, xin, b3, norm_g, chip0b):
    tm = TM_MAT
    n_k = N_DEV // DH_K_BLKS
    n_m = TP // tm

    def body(dp_ref, w_ref, x_ref, dr_ref, g_ref, c0_ref, dx_ref, dg_ref, f0_ref, acc, send_sems, recv_sems):
        m = pl.program_id(0)
        k = pl.program_id(1)
        ride_start, ride_finish = _partials_to_owners(c0_ref, f0_ref, send_sems, recv_sems)

        @pl.when((m == 0) & (k == 0))
        def _():
            ride_start()

        @pl.when(k == 0)
        def _():
            acc[...] = jnp.zeros_like(acc)

        part = _dot_nt(dp_ref[:, 0:W_IN_BLK], w_ref[0])
        for j in range(1, DH_K_BLKS):
            part = part + _dot_nt(dp_ref[:, j * W_IN_BLK:(j + 1) * W_IN_BLK], w_ref[j])
        acc[...] += part

        @pl.when((k == n_k - 1) & (m == 0))
        def _():
            dg_ref[...] = jnp.zeros_like(dg_ref)

        @pl.when(k == n_k - 1)
        def _():
            xv = x_ref[...]
            r1 = lax.rsqrt(jnp.mean(xv * xv, axis=-1, keepdims=True) + EPS)
            xh = xv * r1
            d_h = acc[...]
            dg_ref[0:1, :] += jnp.sum(d_h * xh, axis=0, keepdims=True)
            d_xh = d_h * g_ref[...]
            dx_ref[...] = dr_ref[0].astype(F32) + r1 * (d_xh - xh * jnp.mean(d_xh * xh, axis=-1, keepdims=True))

        @pl.when((m == n_m - 1) & (k == n_k - 1))
        def _():
            ride_finish()

    return pl.pallas_call(
        body, name="dh_norm_bwd", grid=(n_m, n_k),
        in_specs=[pl.BlockSpec((tm, DH_K_BLKS * W_IN_BLK), lambda m, k: (m, k)),
                  pl.BlockSpec((DH_K_BLKS, D, W_IN_BLK), lambda m, k: (k, 0, 0)),
                  pl.BlockSpec((tm, D), lambda m, k: (m, 0)), pl.BlockSpec((1, tm, D), lambda m, k: (2, m, 0)),
                  pl.BlockSpec((1, D), lambda m, k: (0, 0)), ANY],
        out_specs=(pl.BlockSpec((tm, D), lambda m, k: (m, 0)), pl.BlockSpec((8, D), lambda m, k: (0, 0)), ANY),
        out_shape=(jax.ShapeDtypeStruct((TP, D), F32), jax.ShapeDtypeStruct((8, D), F32),
                   jax.ShapeDtypeStruct((3, D, W_IN_BLK), BF16)),
        scratch_shapes=[pltpu.VMEM((tm, D), F32), pltpu.SemaphoreType.DMA((3,)), pltpu.SemaphoreType.DMA((3,))],
        compiler_params=pltpu.CompilerParams(dimension_semantics=("arbitrary", "arbitrary")),
    )(dproj, w_in_full, xin, b3, norm_g, chip0b)


def _sum_adamw(own, landed, w, m, v, tr, name):
    rows, cols = w.shape
    n_t = rows // tr

    def body(o_ref, l1_ref, l2_ref, l3_ref, w_ref, m_ref, v_ref, g_ref, d_ref, m2_ref, v2_ref):
        g = ((o_ref[...] + l1_ref[...].astype(F32)) + l2_ref[...].astype(F32)) + l3_ref[...].astype(F32)
        delta, m2, v2 = _adamw(w_ref[...], g, m_ref[...], v_ref[...])
        g_ref[...] = g
        d_ref[...] = delta
        m2_ref[...] = m2
        v2_ref[...] = v2

    def spec(k):
        return pl.BlockSpec((tr, cols), lambda i: (i + k * n_t, 0))

    out = jax.ShapeDtypeStruct((rows, cols), F32)
    return pl.pallas_call(
        body, name=name, grid=(n_t,),
        in_specs=[spec(0), spec(0), spec(1), spec(2), spec(0), spec(0), spec(0)],
        out_specs=(spec(0),) * 4, out_shape=(out,) * 4,
    )(own, landed, landed, landed, w, m, v)


def _adamw_3(chip1, far1, ws, ms, vs):
    def body(c_ref, f_ref, *refs):
        w_refs, m_refs, v_refs, outs = refs[0:3], refs[3:6], refs[6:9], refs[9:21]
        for k in range(3):
            g = ((c_ref[0, k] + f_ref[0, k].astype(F32)) + f_ref[1, k].astype(F32)) + f_ref[2, k].astype(F32)
            delta, m2, v2 = _adamw(w_refs[k][0], g, m_refs[k][0], v_refs[k][0])
            for kind, val in enumerate((g, delta, m2, v2)):
                outs[3 * kind + k][0] = val

    full = pl.BlockSpec((1, W_ROW_BLK, D), lambda i: (0, 0, 0))
    out = jax.ShapeDtypeStruct((1, W_ROW_BLK, D), F32)
    res = pl.pallas_call(
        body, name="adamw_3", grid=(1,),
        in_specs=[pl.BlockSpec((1, 3, W_ROW_BLK, D), lambda i: (0, 0, 0, 0)),
                  pl.BlockSpec((3, 3, W_ROW_BLK, D), lambda i: (0, 0, 0, 0))] + [full] * 9,
        out_specs=(full,) * 12, out_shape=(out,) * 12,
    )(chip1, far1, *ws, *ms, *vs)
    return tuple(res[3 * kind:3 * kind + 3] for kind in range(4))


N_SMALL = 9


def _small_update(pack_all, srs_all, ws, ms, vs):
    def body(pk_ref, sr_ref, *refs):
        w_refs, m_refs, v_refs = refs[0:N_SMALL], refs[N_SMALL:2 * N_SMALL], refs[2 * N_SMALL:3 * N_SMALL]
        loss_ref = refs[3 * N_SMALL]
        outs = refs[3 * N_SMALL + 1:7 * N_SMALL + 1]
        tot_sc, tots_sc = refs[7 * N_SMALL + 1:]
        tot = pk_ref[0]
        tot_s = sr_ref[0]
        for d in range(1, N_DEV):
            tot = tot + pk_ref[d]
            tot_s = tot_s + sr_ref[d]
        tot_sc[...] = tot
        tots_sc[...] = tot_s
        loss_ref[...] = jnp.sum(tot_sc[5:6, :], axis=1, keepdims=True)
        lbl = w_refs[4]
        p0 = _sigmoid(lbl[0:1, :] - lbl[1:2, :])
        d_l0 = tot_sc[4:5, :] * p0 * (1.0 - p0)

        def update(k, sel, g):
            delta, m2, v2 = _adamw(w_refs[k][sel], g, m_refs[k][sel], v_refs[k][sel])
            for kind, val in enumerate((g, delta, m2, v2)):
                outs[N_SMALL * kind + k][sel] = val

        everything = (slice(None), slice(None))
        for k, row in ((0, 0), (1, 1), (2, 2), (3, 3), (5, 6), (6, 7)):
            update(k, everything, tot_sc[row:row + 1, :])
        update(4, (slice(0, 1), slice(None)), d_l0)
        update(4, (slice(1, 2), slice(None)), -d_l0)
        update(7, (0, slice(None), slice(None)), tots_sc[0:CONV_K, :])
        update(8, everything, tots_sc[META_ROW:META_ROW + N_META, :])

    shapes = [jax.ShapeDtypeStruct(w.shape, F32) for w in ws]
    res = pl.pallas_call(
        body, name="small_update",
        out_shape=(jax.ShapeDtypeStruct((1, 1), F32), *(shapes * 4)),
        scratch_shapes=[pltpu.VMEM((8, D), F32), pltpu.VMEM((SMALL_ROWS, HEAD_W), F32)],
    )(pack_all, srs_all, *ws, *ms, *vs)
    return res[0], tuple(res[1 + N_SMALL * kind:1 + N_SMALL * (kind + 1)] for kind in range(4))


def _local_step(xin, proj, target, conv_w_full, conv_b, ln_g, ln_b, w3_b, lb_logits, gnorm_g, final_g, ids_mine):
    fg = final_g.reshape(1, D)
    c0, o, s_start, att, w3_full = _rec_conv_fwd(proj, lb_logits, conv_w_full, conv_b, w3_b)
    d_o, d_c0, d_z, dproj, a3, b3, red = _mid(xin, target, o, c0, proj, w3_full, ln_g, ln_b, gnorm_g, fg)
    p3 = _wgrad3(a3, b3)
    dproj, dlb, d_conv_w, land1 = _rec_conv_bwd(proj, lb_logits, d_o, s_start, att, d_c0, d_z, conv_w_full, dproj, p3)
    chip1, chip1b = _chip_sum_3(p3, land1, ids_mine)
    return dproj, b3, p3, chip1, chip1b, d_conv_w, red, dlb


def kernel(x, meta_tokens, norm_g, w_in, conv_w, conv_b, ln_g, ln_b, w_conv_out, lb_logits, gnorm_g, w_rec_out, w_out, final_g, loss_target, m_meta_tokens, m_norm_g, m_w_in, m_conv_w, m_conv_b, m_ln_g, m_ln_b, m_w_conv_out, m_lb_logits, m_gnorm_g, m_w_rec_out, m_w_out, m_final_g, v_meta_tokens, v_norm_g, v_w_in, v_conv_w, v_conv_b, v_ln_g, v_ln_b, v_w_conv_out, v_lb_logits, v_gnorm_g, v_w_rec_out, v_w_out, v_final_g):
    mx, my, mc = _my_place()

    ws_s = jnp.concatenate([conv_w[0], jnp.zeros((1, HEAD_W), F32), meta_tokens], axis=0)
    small_full = jnp.transpose(_gather_small(ws_s), (1, 0, 2)).reshape(SMALL_ROWS, D)
    conv_w_full = small_full[0:CONV_K]
    meta_full = small_full[META_ROW:META_ROW + N_META]
    w_in_b, w3_b = _cast_shards(w_in[0], w_conv_out, w_rec_out, w_out)
    first, second, diag = _gather_chips(mx, my, mc)
    use_order = [(mx, my, mc), (mx, my, 1 - mc), (*first, mc), (*second, 1 - mc), (*second, mc), (*first, 1 - mc),
                 (*diag, mc), (*diag, 1 - mc)]
    order = jnp.stack([_dev_index(*p) for p in use_order]).astype(jnp.int32)
    proj, xin, h, w_in_full = _gather_and_proj(x[0], meta_full, norm_g, w_in_b, order)
    h = h.reshape(TP, D)

    ids_mine = jnp.stack([_dev_index(*_chip_rel(mx, my, r), mc) for r in range(4)]).astype(jnp.int32)
    ids_sib = jnp.stack([_dev_index(*_chip_rel(mx, my, r), 1 - mc) for r in range(4)]).astype(jnp.int32)
    dproj, b3, _, chip1, chip1b, d_conv_w, red, dlb = _local_step(
        xin, proj, loss_target[0], conv_w_full, conv_b, ln_g, ln_b, w3_b, lb_logits, gnorm_g, final_g, ids_mine)

    chip0, chip0b, _, far1 = _wgrad_in(h, dproj, jnp.concatenate([ids_sib, ids_mine]), chip1b)
    d_xin, dng, far0 = _dh_and_norm_bwd(dproj, w_in_full, xin, b3, norm_g, chip0b)
    pack = jnp.concatenate([dng[0:1], red[4:5], red[2:3], red[3:4], dlb[0:1], red[5:6], red[1:2], red[0:1]], axis=0)
    g_in, d_in, m_in, v_in = _sum_adamw(chip0.reshape(4 * D, W_IN_BLK), far0.reshape(3 * D, W_IN_BLK), w_in[0],
                                        m_w_in[0], v_w_in[0], 256, "adamw_in")
    big3 = _adamw_3(chip1, far1, (w_conv_out, w_rec_out, w_out), (m_w_conv_out, m_w_rec_out, m_w_out),
                    (v_w_conv_out, v_w_rec_out, v_w_out))

    srs = jnp.concatenate([d_conv_w, d_xin[PAD_FRONT:ROW0]], axis=0)
    srs = jnp.transpose(srs.reshape(SMALL_ROWS, N_DEV, HEAD_W), (1, 0, 2))
    pack_all, srs_all = _exchange_small(pack, srs)
    loss, small = _small_update(
        pack_all, srs_all,
        (norm_g, conv_b, ln_g, ln_b, lb_logits, gnorm_g, final_g.reshape(1, D), conv_w, meta_tokens),
        (m_norm_g, m_conv_b, m_ln_g, m_ln_b, m_lb_logits, m_gnorm_g, m_final_g.reshape(1, D), m_conv_w, m_meta_tokens),
        (v_norm_g, v_conv_b, v_ln_g, v_ln_b, v_lb_logits, v_gnorm_g, v_final_g.reshape(1, D), v_conv_w, v_meta_tokens))

    outs = [loss.reshape(()), d_xin[ROW0:][None]]
    for kind, a_in in enumerate((g_in, d_in, m_in, v_in)):
        ng, cb, lg, lb_, lbl, gg, fg, cw, mt = small[kind]
        a_3 = big3[kind]
        outs += [mt, ng, a_in[None], cw, cb, lg, lb_, a_3[0], lbl, gg, a_3[1], a_3[2], fg.reshape(D)]
    return tuple(outs)
```

```python
import jax
import jax.numpy as jnp
from jax import lax
from jax.experimental import pallas as pl
from jax.experimental.pallas import tpu as pltpu

F32 = jnp.float32
BF16 = jnp.bfloat16
ACT = BF16

D = 1024
SEQ = 4096
N_META = 16
CHUNK = 64
PAD_FRONT = 48
ROW0 = PAD_FRONT + N_META
TP = ROW0 + SEQ
N_CHUNK = TP // CHUNK
HEADS = 8
HEAD_W = 128
D_IN = 9 * D
N_DEV = 8
W_IN_BLK = D_IN // N_DEV
W_ROW_BLK = D // N_DEV
CONV_K = 31
SMALL_ROWS = 48
META_ROW = 32
EPS = 1e-6
HALO = 32

TM_MAT = 1040
TT_WGRAD = 2080
DH_K_BLKS = 2
TM_ELT = 208
CHUNKS_PER_STEP = 5
CONV_STRIPS = 5

ADAM_LR = 0.001
ADAM_B1 = 0.9
ADAM_B2 = 0.999
ADAM_EPS = 1e-08
ADAM_WD = 0.01
ADAM_STEP = 10

MESH_ID = pl.DeviceIdType.MESH
ANY = pl.BlockSpec(memory_space=pl.ANY)


def _sigmoid(v):
    return jax.nn.sigmoid(v)


def _dsilu(silu, s):
    return s + silu * (1.0 - s)


def _dot(a, b):
    return jnp.dot(a, b, preferred_element_type=F32)


def _dot_nt(a, b):
    return lax.dot_general(a, b, (((1,), (1,)), ((), ())), preferred_element_type=F32)


def _dot_tn(a, b):
    return lax.dot_general(a, b, (((0,), (0,)), ((), ())), preferred_element_type=F32)


def _tri_matmul(tri, v):
    hi = v.astype(BF16)
    lo = (v - hi.astype(F32)).astype(BF16)
    return _dot(tri, hi) + _dot(tri, lo)


def _adamw(w, g, m, v):
    m2 = ADAM_B1 * m + (1.0 - ADAM_B1) * g
    v2 = ADAM_B2 * v + (1.0 - ADAM_B2) * jnp.square(g)
    m_hat = m2 / (1.0 - ADAM_B1 ** ADAM_STEP)
    v_hat = v2 / (1.0 - ADAM_B2 ** ADAM_STEP)
    delta = -ADAM_LR * (m_hat / (jnp.sqrt(v_hat) + ADAM_EPS) + ADAM_WD * w)
    return delta, m2, v2


def _window_start(i, tm):
    assert tm % 16 == 0 and ROW0 % 16 == 0
    return pl.multiple_of(16 * jnp.maximum((tm // 16) * i - ROW0 // 16, 0), 16)


def _my_place():
    return lax.axis_index("x"), lax.axis_index("y"), lax.axis_index("c")


def _dev_index(px, py, pc):
    return 4 * px + 2 * py + pc


def _peer(x, y, c, r):
    return (jnp.bitwise_xor(x, (r >> 2) & 1), jnp.bitwise_xor(y, (r >> 1) & 1), jnp.bitwise_xor(c, r & 1))


def _gather_small_and_cast(small_s, w_in_s, w_conv_s, w_rec_s, w_out_s):
    def body(s_ref, a_ref, c_ref, r_ref, w_ref, o_ref, oa_ref, ob_ref, send_sems, recv_sems, local_sem):
        x, y, c = _my_place()
        my_id = _dev_index(x, y, c)
        mine = pltpu.make_async_copy(s_ref, o_ref.at[my_id], local_sem)
        mine.start()
        copies = []
        for r in range(1, N_DEV):
            cp = pltpu.make_async_remote_copy(
                src_ref=s_ref, dst_ref=o_ref.at[my_id], send_sem=send_sems.at[r - 1], recv_sem=recv_sems.at[r - 1],
                device_id=_peer(x, y, c, r), device_id_type=MESH_ID)
            cp.start()
            copies.append(cp)
        oa_ref[...] = a_ref[...].astype(BF16)
        for k, ref in enumerate((c_ref, r_ref, w_ref)):
            ob_ref[k] = ref[0].astype(BF16)
        for cp in copies:
            cp.wait_recv()
        for cp in copies:
            cp.wait_send()
        mine.wait()

    vmem = pl.BlockSpec(memory_space=pltpu.VMEM)
    return pl.pallas_call(
        body, name="gather_small_and_cast",
        out_shape=(jax.ShapeDtypeStruct((N_DEV,) + small_s.shape, F32), jax.ShapeDtypeStruct(w_in_s.shape, BF16),
                   jax.ShapeDtypeStruct((3, W_ROW_BLK, D), BF16)),
        in_specs=[ANY, vmem, vmem, vmem, vmem], out_specs=(ANY, vmem, vmem),
        scratch_shapes=[pltpu.SemaphoreType.DMA((7,)), pltpu.SemaphoreType.DMA((7,)), pltpu.SemaphoreType.DMA],
    )(small_s, w_in_s, w_conv_s, w_rec_s, w_out_s)


def _w3_gather(src, out, stage, send_sems, recv_sems, local_sems):
    x, y, c = _my_place()
    me, sibling = (x, y, c), (x, y, 1 - c)
    chips = [(1 - x, y), (x, 1 - y), (1 - x, 1 - y)]

    def block(place):
        d = _dev_index(*place)
        return out.at[:, pl.ds(pl.multiple_of(d * W_ROW_BLK, W_ROW_BLK), W_ROW_BLK), :]

    def copy(k, place, to, from_src=False):
        return pltpu.make_async_remote_copy(
            src_ref=src if from_src else block(place), dst_ref=block(place),
            send_sem=send_sems.at[k], recv_sem=recv_sems.at[k], device_id=to, device_id_type=MESH_ID)

    own_in = pltpu.make_async_copy(src, stage, local_sems.at[0])
    own_out = pltpu.make_async_copy(stage, block(me), local_sems.at[1])

    def start():
        copy(0, me, sibling, from_src=True).start()
        for j, chip in enumerate(chips):
            copy(1 + j, me, (*chip, c), from_src=True).start()
        own_in.start()
        own_in.wait()
        own_out.start()

    def finish():
        for j, chip in enumerate(chips):
            copy(1 + j, (*chip, c), me).wait_recv()
            copy(4 + j, (*chip, c), sibling).start()
        copy(0, sibling, me).wait_recv()
        for j, chip in enumerate(chips):
            copy(4 + j, (*chip, 1 - c), me).wait_recv()
        for k in range(7):
            copy(k, me, me).wait_send()
        own_out.wait()

    return start, finish


def _p3_to_sibling(p3_ref, land_ref, send_sems, recv_sems):
    x, y, c = _my_place()

    def cp(q):
        d = _dev_index(*_chip_rel(x, y, q), 1 - c)
        return pltpu.make_async_remote_copy(
            src_ref=p3_ref.at[:, pl.ds(pl.multiple_of(d * W_ROW_BLK, W_ROW_BLK), W_ROW_BLK), :],
            dst_ref=land_ref.at[q], send_sem=send_sems.at[q], recv_sem=recv_sems.at[q],
            device_id=(x, y, 1 - c), device_id_type=MESH_ID)

    def start():
        for q in range(4):
            cp(q).start()

    def finish():
        for q in range(4):
            cp(q).wait_recv()
        for q in range(4):
            cp(q).wait_send()

    return start, finish


def _partials_to_owners(src_ref, far_ref, send_sems, recv_sems):
    x, y, c = _my_place()

    def cp(q):
        return pltpu.make_async_remote_copy(
            src_ref=src_ref.at[q], dst_ref=far_ref.at[q - 1], send_sem=send_sems.at[q - 1],
            recv_sem=recv_sems.at[q - 1], device_id=(*_chip_rel(x, y, q), c), device_id_type=MESH_ID)

    def start():
        for q in range(1, 4):
            cp(q).start()

    def finish():
        for q in range(1, 4):
            cp(q).wait_recv()
        for q in range(1, 4):
            cp(q).wait_send()

    return start, finish


def _gather_chips(x, y, c):
    first = (jnp.bitwise_xor(x, 1 - c), jnp.bitwise_xor(y, c))
    second = (jnp.bitwise_xor(x, c), jnp.bitwise_xor(y, 1 - c))
    return [first, second, (1 - x, 1 - y)]


def _gather_and_proj(x_seq, meta_full, norm_g, w_in_b, order):
    tm = TM_MAT
    n_m = TP // tm
    last_m = n_m - 1

    def body(order_ref, x_ref, meta_ref, g_ref, s0, proj_ref, xin_ref, h_out, o0, hbuf, wbuf, send_sems, recv_sems,
             local_sems):
        del order_ref
        n = pl.program_id(0)
        m = pl.program_id(1)
        x, y, c = _my_place()
        me, sibling = (x, y, c), (x, y, 1 - c)
        chips = _gather_chips(x, y, c)

        def block(place):
            return o0.at[_dev_index(*place)]

        def copy(k, place, to, from_src=False):
            return pltpu.make_async_remote_copy(
                src_ref=s0 if from_src else block(place), dst_ref=block(place),
                send_sem=send_sems.at[k], recv_sem=recv_sems.at[k], device_id=to, device_id_type=MESH_ID)

        def to_vmem(place, slot):
            return pltpu.make_async_copy(block(place), wbuf.at[slot], local_sems.at[slot])

        own_out = pltpu.make_async_copy(wbuf.at[0], block(me), local_sems.at[2])
        h_copy = pltpu.make_async_copy(hbuf, h_out, local_sems.at[3])

        @pl.when((n == 0) & (m == 0))
        def _():
            copy(0, me, sibling, from_src=True).start()
            for j, chip in enumerate(chips[0:2]):
                copy(1 + j, me, (*chip, c), from_src=True).start()
            mine = pltpu.make_async_copy(s0, wbuf.at[0], local_sems.at[0])
            mine.start()
            mine.wait()
            own_out.start()

        @pl.when(n == 0)
        def _():
            xv = x_ref[...]
            xin_ref[...] = jnp.where(m == 0, pltpu.roll(xv, ROW0, 0), xv)

            @pl.when(m == 0)
            def _():
                xin_ref[0:PAD_FRONT, :] = jnp.zeros((PAD_FRONT, D), F32)
                xin_ref[PAD_FRONT:ROW0, :] = meta_ref[...]

            xv = xin_ref[...]
            r = lax.rsqrt(jnp.mean(xv * xv, axis=-1, keepdims=True) + EPS)
            hbuf[m] = (xv * r * g_ref[...]).astype(BF16)

        between = [4 + c, 5 - c, 6]
        first, second, diag = chips
        plan = [(sibling, (0, sibling), None),
                ((*first, c), (1, (*first, c)), between[0]),
                ((*second, 1 - c), (between[1], (*second, 1 - c)), None),
                ((*second, c), (2, (*second, c)), between[1]),
                ((*first, 1 - c), (between[0], (*first, 1 - c)), None),
                ((*diag, c), (3, (*diag, c)), between[2]),
                ((*diag, 1 - c), (between[2], (*diag, 1 - c)), None)]

        for s, (place, (k, origin), pass_on) in enumerate(plan, start=1):
            @pl.when((n == s - 1) & (m == last_m))
            def _(s=s, place=place, k=k, origin=origin, pass_on=pass_on):
                copy(k, origin, me).wait_recv()
                if pass_on is not None:
                    copy(pass_on, place, sibling).start()
                if s == 2:
                    copy(3, place, (*chips[1], c)).start()
                    own_out.wait()
                to_vmem(place, s % 2).start()

            @pl.when((n == s) & (m == 0))
            def _(s=s, place=place):
                to_vmem(place, s % 2).wait()

        proj_ref[...] = _dot(hbuf[m], wbuf[lax.rem(n, 2)]).astype(BF16)

        @pl.when((n == 0) & (m == last_m))
        def _():
            h_copy.start()

        @pl.when((n == N_DEV - 1) & (m == last_m))
        def _():
            for k in range(7):
                copy(k, me, me).wait_send()
            h_copy.wait()

    return pl.pallas_call(
        body, name="gather_and_proj",
        grid_spec=pltpu.PrefetchScalarGridSpec(
            num_scalar_prefetch=1, grid=(N_DEV, n_m),
            in_specs=[pl.BlockSpec((pl.Element(tm), pl.Element(D)),
                                   lambda n, m, o: (_window_start(jnp.where(n == 0, m, 0), tm), 0)),
                      pl.BlockSpec((N_META, D), lambda n, m, o: (0, 0)),
                      pl.BlockSpec((1, D), lambda n, m, o: (0, 0)), ANY],
            out_specs=(pl.BlockSpec((tm, W_IN_BLK), lambda n, m, o: (m, o[n])),
                       pl.BlockSpec((tm, D), lambda n, m, o: (jnp.where(n == 0, m, last_m), 0)), ANY, ANY),
            scratch_shapes=[pltpu.VMEM((n_m, tm, D), BF16), pltpu.VMEM((2, D, W_IN_BLK), BF16),
                            pltpu.SemaphoreType.DMA((7,)), pltpu.SemaphoreType.DMA((7,)),
                            pltpu.SemaphoreType.DMA((4,))]),
        out_shape=(jax.ShapeDtypeStruct((TP, D_IN), BF16), jax.ShapeDtypeStruct((TP, D), F32),
                   jax.ShapeDtypeStruct((n_m, tm, D), BF16), jax.ShapeDtypeStruct((N_DEV, D, W_IN_BLK), BF16)),
        compiler_params=pltpu.CompilerParams(dimension_semantics=("arbitrary", "arbitrary")),
    )(order, x_seq, meta_full, norm_g, w_in_b)


def _chip_rel(x, y, r):
    return (jnp.bitwise_xor(x, r >> 1), jnp.bitwise_xor(y, r & 1))


def _exchange_small(pack, srs):
    def body(pk, sr, pk_all, sr_all, send_sems, recv_sems, local_sems):
        x, y, c = _my_place()
        my_id = _dev_index(x, y, c)
        mine = [pltpu.make_async_copy(pk, pk_all.at[my_id], local_sems.at[0]),
                pltpu.make_async_copy(sr.at[my_id], sr_all.at[my_id], local_sems.at[1])]
        for cp in mine:
            cp.start()
        copies = []
        for r in range(1, N_DEV):
            peer = (jnp.bitwise_xor(x, (r >> 2) & 1), jnp.bitwise_xor(y, (r >> 1) & 1), jnp.bitwise_xor(c, r & 1))
            peer_id = _dev_index(*peer)
            for a, (src, dst) in enumerate(((pk, pk_all.at[my_id]), (sr.at[peer_id], sr_all.at[my_id]))):
                cp = pltpu.make_async_remote_copy(
                    src_ref=src, dst_ref=dst, send_sem=send_sems.at[a * 7 + r - 1], recv_sem=recv_sems.at[a * 7 + r - 1],
                    device_id=peer, device_id_type=MESH_ID)
                cp.start()
                copies.append(cp)
        for cp in copies:
            cp.wait_recv()
        for cp in copies:
            cp.wait_send()
        for cp in mine:
            cp.wait()

    return pl.pallas_call(
        body, name="exchange_small",
        out_shape=(jax.ShapeDtypeStruct((N_DEV,) + pack.shape, F32), jax.ShapeDtypeStruct(srs.shape, F32)),
        in_specs=[ANY, ANY], out_specs=(ANY, ANY),
        scratch_shapes=[pltpu.SemaphoreType.DMA((14,)), pltpu.SemaphoreType.DMA((14,)), pltpu.SemaphoreType.DMA((2,))],
    )(pack, srs)


N_CB = D // HEAD_W


def _store_by_cb(ref, idx, rows, val):
    for cb in range(N_CB):
        ref[(*idx, cb, rows, slice(None))] = val[:, cb * HEAD_W:(cb + 1) * HEAD_W]


def _fill_shifts(sh, tm):
    n = tm + HALO - 8
    for s in range(1, 8):
        for cb in range(N_CB):
            sh[s, cb, 0:n, :] = sh[0, cb, s:s + n, :]


def _gates(p_ref, lbl_ref, chunk, bsc):
    lb = _sigmoid(lbl_ref[0:1, :] - lbl_ref[1:2, :])
    q_raw = p_ref[:, 0:D].astype(F32)
    f_raw = p_ref[:, D:2 * D].astype(F32)
    sq = _sigmoid(q_raw)
    q = q_raw * sq
    sg = _sigmoid(f_raw)
    f = lb + (1.0 - lb) * sg
    row = lax.broadcasted_iota(jnp.int32, (CHUNK, 1), 0) + chunk * CHUNK
    valid = row >= PAD_FRONT
    lf = jnp.where(valid, jnp.log(f), 0.0)
    k = jnp.where(valid, 1.0 - f, 0.0)
    r_i = lax.broadcasted_iota(jnp.int32, (CHUNK, CHUNK), 0)
    c_i = lax.broadcasted_iota(jnp.int32, (CHUNK, CHUNK), 1)
    causal = r_i >= c_i
    bsc[...] = _tri_matmul(causal.astype(BF16), lf)
    b = bsc[...]
    b_mid = bsc[CHUNK // 2 - 1:CHUNK // 2, :]
    b_last = bsc[CHUNK - 1:CHUNK, :]
    e_q = jnp.exp(b)
    e_qm = jnp.exp(b - b_mid)
    e_km = jnp.exp(b_mid - b)
    e_kh = jnp.exp(b_last - b)
    e_last = jnp.exp(b_last)
    return dict(lb=lb, q_raw=q_raw, sq=sq, q=q, sg=sg, f=f, k=k, valid=valid, causal=causal,
                e_q=e_q, e_qm=e_qm, e_km=e_km, e_kh=e_kh, e_last=e_last)


def _rec_conv_fwd(proj, lb_logits, conv_w, conv_b, w3_b):
    cps = CHUNKS_PER_STEP
    tm = cps * CHUNK
    n_strip = CONV_STRIPS
    strip = tm // n_strip

    def body(p_ref, lbl_ref, pg_ref, w_ref, b_ref, w3s_ref, c0_ref, o_ref, s_ref, att_ref, w3o_ref,
             st, bsc, sh, c0_sc, w3buf, send_sems, recv_sems, local_sems):
        n = pl.program_id(0)
        gather_start, gather_finish = _w3_gather(w3s_ref, w3o_ref, w3buf, send_sems, recv_sems, local_sems)

        @pl.when(n == 0)
        def _():
            st[...] = jnp.zeros_like(st)
            sh[0, :, 0:HALO, :] = jnp.zeros((N_CB, HALO, HEAD_W), F32)
            gather_start()

        @pl.when(n > 0)
        def _():
            sh[0, :, 0:HALO, :] = sh[0, :, tm:tm + HALO, :]

        ga = pg_ref[:, 0:D].astype(F32)
        gb = pg_ref[:, D:2 * D].astype(F32)
        _store_by_cb(sh, (0,), slice(HALO, HALO + tm), ga * _sigmoid(gb))
        _fill_shifts(sh, tm)

        def conv_unit(cb, s_i):
            cs = slice(cb * HEAD_W, (cb + 1) * HEAD_W)
            acc = jnp.broadcast_to(b_ref[:, cs], (strip, HEAD_W))
            for j in range(CONV_K):
                off = HALO - (CONV_K - 1) + j
                lo = s_i * strip + 8 * (off // 8)
                acc = acc + w_ref[j:j + 1, cs] * sh[off % 8, cb, lo:lo + strip, :]
            c0_sc[s_i * strip:(s_i + 1) * strip, cs] = acc

        units = [(cb, s_i) for cb in range(N_CB) for s_i in range(n_strip)]

        def prep(ci):
            g = _gates(p_ref.at[pl.ds(ci * CHUNK, CHUNK)], lbl_ref, n * cps + ci, bsc.at[ci])
            g["q1"] = (g["q"] * g["e_q"]).astype(BF16)
            g["qm"] = (g["q"] * g["e_qm"]).astype(BF16)
            g["km"] = (g["k"] * g["e_km"]).astype(BF16)
            g["kh"] = (g["k"] * g["e_kh"]).astype(BF16)
            return g

        def heads(ci, g):
            rs = pl.ds(ci * CHUNK, CHUNK)
            pv = p_ref.at[rs]
            s_ref[ci] = st[...]
            for h in range(HEADS):
                if units:
                    conv_unit(*units.pop(0))
                sl = slice(h * HEAD_W, (h + 1) * HEAD_W)
                v = pv[:, 2 * D + h * HEAD_W:2 * D + (h + 1) * HEAD_W]
                att = jnp.where(g["causal"], _dot_nt(g["qm"][:, sl], g["km"][:, sl]), 0.0).astype(BF16)
                att_ref[ci, h] = att
                s_h = st[h]
                o_ref[rs, sl] = (_dot_nt(g["q1"][:, sl], s_h.astype(BF16)) + _dot(att, v)).astype(ACT)
                st[h] = s_h * g["e_last"][:, sl] + _dot_tn(v, g["kh"][:, sl])

        ready = prep(0)
        for ci in range(cps):
            coming = prep(ci + 1) if ci + 1 < cps else None
            heads(ci, ready)
            ready = coming
        while units:
            conv_unit(*units.pop(0))
        c0_ref[...] = c0_sc[...].astype(ACT)

        @pl.when(n == N_CHUNK // cps - 1)
        def _():
            gather_finish()

    def rows_of(width, col):
        return pl.BlockSpec((tm, width), lambda n: (n, col))

    return pl.pallas_call(
        body, name="rec_conv_fwd", grid=(N_CHUNK // cps,),
        in_specs=[rows_of(3 * D, 1), pl.BlockSpec((2, D), lambda n: (0, 0)), rows_of(2 * D, 0),
                  pl.BlockSpec((CONV_K, D), lambda n: (0, 0)), pl.BlockSpec((1, D), lambda n: (0, 0)), ANY],
        out_specs=(rows_of(D, 0), rows_of(D, 0), pl.BlockSpec((cps, HEADS, HEAD_W, HEAD_W), lambda n: (n, 0, 0, 0)),
                   pl.BlockSpec((cps, HEADS, CHUNK, CHUNK), lambda n: (n, 0, 0, 0)), ANY),
        out_shape=(jax.ShapeDtypeStruct((TP, D), ACT), jax.ShapeDtypeStruct((TP, D), ACT),
                   jax.ShapeDtypeStruct((N_CHUNK, HEADS, HEAD_W, HEAD_W), F32),
                   jax.ShapeDtypeStruct((N_CHUNK, HEADS, CHUNK, CHUNK), BF16), jax.ShapeDtypeStruct((3, D, D), BF16)),
        scratch_shapes=[pltpu.VMEM((HEADS, HEAD_W, HEAD_W), F32), pltpu.VMEM((cps, CHUNK, D), F32),
                        pltpu.VMEM((8, N_CB, HALO + tm, HEAD_W), F32), pltpu.VMEM((tm, D), F32),
                        pltpu.VMEM((3, W_ROW_BLK, D), BF16), pltpu.SemaphoreType.DMA((7,)),
                        pltpu.SemaphoreType.DMA((7,)), pltpu.SemaphoreType.DMA((2,))],
        compiler_params=pltpu.CompilerParams(dimension_semantics=("arbitrary",)),
    )(proj, lb_logits, proj, conv_w, conv_b, w3_b)


def _mid(xin, tgt, o, c0, proj, w3, ln_g, ln_b, gnorm_g, final_g):
    tm = TM_ELT

    def body(x_ref, t_ref, o_ref, c0_ref, z_ref, gr_ref, mc_ref, mr_ref, w_ref, lng_ref, lnb_ref, gng_ref, fg_ref,
             do_ref, dc0_ref, dz_ref, dp_ref, a3_ref, b3_ref, red_ref, on_sc, don_sc):
        i = pl.program_id(0)

        @pl.when(i == 0)
        def _():
            red_ref[...] = jnp.zeros_like(red_ref)

        w_conv, w_rec, w_out = w_ref[0], w_ref[1], w_ref[2]
        c0v = c0_ref[...].astype(F32)
        mu = jnp.mean(c0v, axis=-1, keepdims=True)
        xc = c0v - mu
        rstd = lax.rsqrt(jnp.mean(xc * xc, axis=-1, keepdims=True) + EPS)
        xh = xc * rstd
        c1 = xh * lng_ref[...] + lnb_ref[...]
        s1 = _sigmoid(c1)
        c2 = c1 * s1
        z = z_ref[...].astype(F32)
        sz = _sigmoid(z)
        silu_z = z * sz
        u_conv = (c2 * silu_z).astype(BF16)
        y_conv = _dot(u_conv, w_conv)
        ov = o_ref[...].astype(F32)
        r3 = []
        for h in range(HEADS):
            sl = slice(h * HEAD_W, (h + 1) * HEAD_W)
            oh = ov[:, sl]
            r_h = lax.rsqrt(jnp.mean(oh * oh, axis=-1, keepdims=True) + EPS)
            r3.append(r_h)
            on_sc[:, sl] = oh * r_h
        o_n = on_sc[...]
        o_g = o_n * gng_ref[...]
        gr = gr_ref[...].astype(F32)
        sgr = _sigmoid(gr)
        silu_g = gr * sgr
        u_rec = (o_g * silu_g).astype(BF16)
        y_rec = _dot(u_rec, w_rec)
        mc = mc_ref[...].astype(F32)
        mr = mr_ref[...].astype(F32)
        smc = _sigmoid(mc)
        smr = _sigmoid(mr)
        merged = (smc * y_conv + smr * y_rec).astype(BF16)
        res = x_ref[...] + _dot(merged, w_out)
        r2 = lax.rsqrt(jnp.mean(res * res, axis=-1, keepdims=True) + EPS)
        xh2 = res * r2
        row = lax.broadcasted_iota(jnp.int32, (tm, 1), 0) + i * tm
        real = row >= ROW0
        tgt = t_ref[...]
        tgt = jnp.where(i == 0, pltpu.roll(tgt, ROW0, 0), tgt)
        diff = jnp.where(real, xh2 * fg_ref[...] - tgt, 0.0)
        d_y = diff * (1.0 / D)
        d_xh2 = d_y * fg_ref[...]
        d_res = r2 * (d_xh2 - xh2 * jnp.mean(d_xh2 * xh2, axis=-1, keepdims=True))
        d_res_b = d_res.astype(BF16)
        d_merged = _dot_nt(d_res_b, w_out)
        d_yc_f = d_merged * smc
        d_yr_f = d_merged * smr
        d_yc = d_yc_f.astype(BF16)
        d_yr = d_yr_f.astype(BF16)
        dp_ref[:, D:2 * D] = (d_yc_f * y_conv * (1.0 - smc)).astype(BF16)
        dp_ref[:, 2 * D:3 * D] = (d_yr_f * y_rec * (1.0 - smr)).astype(BF16)
        d_ur = _dot_nt(d_yr, w_rec)
        d_og = d_ur * silu_g
        dp_ref[:, 0:D] = (d_ur * o_g * _dsilu(silu_g, sgr)).astype(BF16)
        d_on = d_og * gng_ref[...]
        for h in range(HEADS):
            sl = slice(h * HEAD_W, (h + 1) * HEAD_W)
            d_h = d_on[:, sl]
            n_h = o_n[:, sl]
            don_sc[:, sl] = r3[h] * (d_h - n_h * jnp.mean(d_h * n_h, axis=-1, keepdims=True))
        do_ref[...] = don_sc[...].astype(ACT)
        d_uc = _dot_nt(d_yc, w_conv)
        d_c2 = d_uc * silu_z
        dz_ref[...] = (d_uc * c2 * _dsilu(silu_z, sz)).astype(BF16)
        d_c1 = d_c2 * _dsilu(c2, s1)
        d_xh = d_c1 * lng_ref[...]
        d_c0 = rstd * (d_xh - jnp.mean(d_xh, axis=-1, keepdims=True)
                       - xh * jnp.mean(d_xh * xh, axis=-1, keepdims=True))
        dc0_ref[...] = d_c0.astype(ACT)
        a3_ref[0] = u_conv
        b3_ref[0] = d_yc
        a3_ref[1] = u_rec
        b3_ref[1] = d_yr
        a3_ref[2] = merged
        b3_ref[2] = d_res_b
        def colsum(vv):
            return jnp.sum(vv, axis=0, keepdims=True)

        red_ref[0:1, :] += colsum(d_y * xh2)
        red_ref[1:2, :] += colsum(d_og * o_n)
        red_ref[2:3, :] += colsum(d_c1 * xh)
        red_ref[3:4, :] += colsum(d_c1)
        red_ref[4:5, :] += colsum(d_c0)
        red_ref[5:6, :] += colsum(diff * diff) * (0.5 / D)

    def row_block(width, col):
        return pl.BlockSpec((tm, width), lambda i: (i, col))

    def const_block(shape):
        return pl.BlockSpec(shape, lambda i: (0,) * len(shape))

    stack = jax.ShapeDtypeStruct((3, TP, D), BF16)
    stack_spec = pl.BlockSpec((3, tm, D), lambda i: (0, i, 0))
    return pl.pallas_call(
        body, name="mid", grid=(TP // tm,),
        in_specs=[row_block(D, 0),
                  pl.BlockSpec((pl.Element(tm), pl.Element(D)), lambda i: (_window_start(i, tm), 0)),
                  row_block(D, 0), row_block(D, 0),
                  row_block(D, 2), row_block(D, 6), row_block(D, 7), row_block(D, 8),
                  pl.BlockSpec((3, D, D), lambda i: (0, 0, 0), pipeline_mode=pl.Buffered(1)),
                  const_block((1, D)), const_block((1, D)), const_block((1, D)), const_block((1, D))],
        out_specs=(row_block(D, 0), row_block(D, 0), row_block(D, 0), row_block(3 * D, 2),
                   stack_spec, stack_spec, const_block((8, D))),
        out_shape=(jax.ShapeDtypeStruct((TP, D), ACT), jax.ShapeDtypeStruct((TP, D), ACT),
                   jax.ShapeDtypeStruct((TP, D), BF16),
                   jax.ShapeDtypeStruct((TP, D_IN), BF16), stack, stack, jax.ShapeDtypeStruct((8, D), F32)),
        scratch_shapes=[pltpu.VMEM((tm, D), F32), pltpu.VMEM((tm, D), F32)],
        compiler_params=pltpu.CompilerParams(dimension_semantics=("arbitrary",), vmem_limit_bytes=60 * 1024 * 1024),
    )(xin, tgt, o, c0, proj, proj, proj, proj, w3, ln_g, ln_b, gnorm_g, final_g)


def _rec_conv_bwd(proj, lb_logits, d_o, s_start, att, d_c0, d_z, conv_w, dproj, p3):
    cps = CHUNKS_PER_STEP
    tm = cps * CHUNK
    last = N_CHUNK // cps - 1
    n_strip = CONV_STRIPS
    strip = tm // n_strip

    def body(p_ref, lbl_ref, do_ref, s_ref, pg_ref, dc_ref, dz_ref, w_ref, dproj_in, p3_ref, att_ref,
             dp_ref, dlb_ref, dw_ref, land_ref,
             dst, bsc, dq_sc, dk_sc, g_sc, dsh, a_sc, da_sc, acc, send_sems, recv_sems):
        del dproj_in
        n = pl.program_id(0)
        ride_start, ride_finish = _p3_to_sibling(p3_ref, land_ref, send_sems, recv_sems)

        @pl.when(n == 0)
        def _():
            ride_start()
            dst[...] = jnp.zeros_like(dst)
            dlb_ref[...] = jnp.zeros_like(dlb_ref)
            dsh[0, :, tm:tm + HALO, :] = jnp.zeros((N_CB, HALO, HEAD_W), F32)
            acc[...] = jnp.zeros_like(acc)

        @pl.when(n > 0)
        def _():
            dsh[0, :, tm:tm + HALO, :] = dsh[0, :, 0:HALO, :]

        _store_by_cb(dsh, (0,), slice(0, tm), dc_ref[...].astype(F32))
        _fill_shifts(dsh, tm)
        ga = pg_ref[:, 0:D].astype(F32)
        sb = _sigmoid(pg_ref[:, D:2 * D].astype(F32))
        a = ga * sb
        _store_by_cb(a_sc, (), slice(0, tm), a)

        def conv_unit(cb, st):
            cs = slice(cb * HEAD_W, (cb + 1) * HEAD_W)
            rows = slice(st * strip, (st + 1) * strip)
            a_s = a_sc[cb, rows, :]
            d_a = jnp.zeros((strip, HEAD_W), F32)
            for j in range(CONV_K):
                off = CONV_K - 1 - j
                lo = st * strip + 8 * (off // 8)
                slab = dsh[off % 8, cb, lo:lo + strip, :]
                d_a = d_a + w_ref[j:j + 1, cs] * slab
                acc[j, :, cs] += jnp.sum((a_s * slab).reshape(strip // 8, 8, HEAD_W), axis=0)
            da_sc[rows, cs] = d_a

        units = [(cb, st) for cb in range(N_CB) for st in range(n_strip)]

        def prep(ci):
            g = _gates(p_ref.at[pl.ds(ci * CHUNK, CHUNK)], lbl_ref, (last - n) * cps + ci, bsc.at[ci])
            g["q1"] = (g["q"] * g["e_q"]).astype(BF16)
            qm_f = g["q"] * g["e_qm"]
            km_f = g["k"] * g["e_km"]
            g["qm"] = qm_f.astype(BF16)
            g["km"] = km_f.astype(BF16)
            g["qm_lo"] = (qm_f - g["qm"].astype(F32)).astype(BF16)
            g["km_lo"] = (km_f - g["km"].astype(F32)).astype(BF16)
            g["kh_f"] = g["k"] * g["e_kh"]
            g["kh"] = g["kh_f"].astype(BF16)
            return g

        def heads_and_post(ci, g):
            rs = pl.ds(ci * CHUNK, CHUNK)
            pv = p_ref.at[rs]
            dpv = dp_ref.at[rs]
            q1, qm, km, qm_lo, km_lo, kh_f, kh = (g[k] for k in ("q1", "qm", "km", "qm_lo", "km_lo", "kh_f", "kh"))
            for h in range(HEADS):
                if units:
                    conv_unit(*units.pop(0))
                sl = slice(h * HEAD_W, (h + 1) * HEAD_W)
                v = pv[:, 2 * D + h * HEAD_W:2 * D + (h + 1) * HEAD_W]
                d_oh = do_ref[rs, sl].astype(BF16)
                s0 = s_ref[ci, h]
                ds_end = dst[h]
                ds_end_b = ds_end.astype(BF16)
                att = att_ref[ci, h]
                d_att = jnp.where(g["causal"], _dot_nt(d_oh, v), 0.0).astype(BF16)
                d_v = _dot_tn(att, d_oh) + _dot_nt(kh[:, sl], ds_end_b)
                d_qm2 = _dot(d_att, jnp.concatenate([km[:, sl], km_lo[:, sl]], axis=1))
                d_qm = d_qm2[:, 0:HEAD_W] + d_qm2[:, HEAD_W:2 * HEAD_W]
                d_q1 = _dot(d_oh, s0.astype(BF16))
                d_km2 = _dot_tn(d_att, jnp.concatenate([qm[:, sl], qm_lo[:, sl]], axis=1))
                d_km = d_km2[:, 0:HEAD_W] + d_km2[:, HEAD_W:2 * HEAD_W]
                d_kh = _dot(v, ds_end_b)
                dq_sc[ci, :, sl] = d_qm * g["e_qm"][:, sl] + d_q1 * g["e_q"][:, sl]
                dk_sc[ci, :, sl] = d_km * g["e_km"][:, sl] + d_kh * g["e_kh"][:, sl]
                g_sc[ci, :, sl] = (jnp.sum(kh_f[:, sl] * d_kh, axis=0, keepdims=True)
                                   + g["e_last"][:, sl] * jnp.sum(ds_end * s0, axis=0, keepdims=True))
                dst[h] = ds_end * g["e_last"][:, sl] + _dot_tn(d_oh, q1[:, sl])
                dpv[:, 5 * D + h * HEAD_W:5 * D + (h + 1) * HEAD_W] = d_v.astype(BF16)
            d_q = dq_sc[ci]
            d_k = dk_sc[ci]
            d_b = g["q"] * d_q - g["k"] * d_k
            anti = jnp.logical_not(g["causal"]) | (lax.broadcasted_iota(jnp.int32, (CHUNK, CHUNK), 0)
                                                    == lax.broadcasted_iota(jnp.int32, (CHUNK, CHUNK), 1))
            d_lf = _tri_matmul(anti.astype(BF16), d_b) + g_sc[ci]
            d_f = jnp.where(g["valid"], d_lf / g["f"] - d_k, 0.0)
            sg = g["sg"]
            dlb_ref[0:1, :] += jnp.sum(d_f * (1.0 - sg), axis=0, keepdims=True)
            dpv[:, 3 * D:4 * D] = (d_q * _dsilu(g["q"], g["sq"])).astype(BF16)
            dpv[:, 4 * D:5 * D] = (d_f * (1.0 - g["lb"]) * sg * (1.0 - sg)).astype(BF16)

        ready = prep(cps - 1)
        for ci in reversed(range(cps)):
            coming = prep(ci - 1) if ci > 0 else None
            heads_and_post(ci, ready)
            ready = coming
        while units:
            conv_unit(*units.pop(0))

        d_a = da_sc[...]
        dp_ref[:, 0:D] = (d_a * sb).astype(BF16)
        dp_ref[:, D:2 * D] = (d_a * a * (1.0 - sb)).astype(BF16)
        dp_ref[:, 2 * D:3 * D] = dz_ref[...]

        @pl.when(n == last)
        def _():
            for j in range(CONV_K):
                dw_ref[j:j + 1, :] = jnp.sum(acc[j], axis=0, keepdims=True)
            dw_ref[CONV_K:CONV_K + 1, :] = jnp.zeros((1, D), F32)
            ride_finish()

    def rows_of(width, col):
        return pl.BlockSpec((tm, width), lambda n: (last - n, col))

    return pl.pallas_call(
        body, name="rec_conv_bwd", grid=(N_CHUNK // cps,),
        in_specs=[rows_of(3 * D, 1), pl.BlockSpec((2, D), lambda n: (0, 0)), rows_of(D, 0),
                  pl.BlockSpec((cps, HEADS, HEAD_W, HEAD_W), lambda n: (last - n, 0, 0, 0)),
                  rows_of(2 * D, 0), rows_of(D, 0), rows_of(D, 0), pl.BlockSpec((CONV_K, D), lambda n: (0, 0)), ANY, ANY,
                  pl.BlockSpec((cps, HEADS, CHUNK, CHUNK), lambda n: (last - n, 0, 0, 0))],
        out_specs=(rows_of(6 * D, 0), pl.BlockSpec((8, D), lambda n: (0, 0)),
                   pl.BlockSpec((CONV_K + 1, D), lambda n: (0, 0)), ANY),
        out_shape=(jax.ShapeDtypeStruct((TP, D_IN), BF16), jax.ShapeDtypeStruct((8, D), F32),
                   jax.ShapeDtypeStruct((CONV_K + 1, D), F32), jax.ShapeDtypeStruct((4, 3, W_ROW_BLK, D), F32)),
        scratch_shapes=[pltpu.VMEM((HEADS, HEAD_W, HEAD_W), F32), pltpu.VMEM((cps, CHUNK, D), F32),
                        pltpu.VMEM((cps, CHUNK, D), F32), pltpu.VMEM((cps, CHUNK, D), F32),
                        pltpu.VMEM((cps, 1, D), F32),
                        pltpu.VMEM((8, N_CB, tm + HALO, HEAD_W), F32), pltpu.VMEM((N_CB, tm, HEAD_W), F32),
                        pltpu.VMEM((tm, D), F32), pltpu.VMEM((CONV_K, 8, D), F32),
                        pltpu.SemaphoreType.DMA((4,)), pltpu.SemaphoreType.DMA((4,))],
        input_output_aliases={8: 0},
        compiler_params=pltpu.CompilerParams(dimension_semantics=("arbitrary",)),
    )(proj, lb_logits, d_o, s_start, proj, d_c0, d_z, conv_w, dproj, p3, att)


def _wgrad3(a3, b3):
    tt = TT_WGRAD

    def body(a_ref, b_ref, o_ref):
        @pl.when(pl.program_id(1) == 0)
        def _():
            o_ref[...] = jnp.zeros_like(o_ref)

        o_ref[0] += _dot_tn(a_ref[0], b_ref[0])

    return pl.pallas_call(
        body, name="wgrad3", grid=(3, TP // tt),
        in_specs=[pl.BlockSpec((1, tt, D), lambda g, t: (g, t, 0)), pl.BlockSpec((1, tt, D), lambda g, t: (g, t, 0))],
        out_specs=pl.BlockSpec((1, D, D), lambda g, t: (g, 0, 0)),
        out_shape=jax.ShapeDtypeStruct((3, D, D), F32),
        compiler_params=pltpu.CompilerParams(dimension_semantics=("arbitrary", "arbitrary")),
    )(a3, b3)


def _wgrad_in(h, dproj, ids, chip1b):
    tt = TT_WGRAD
    n_t = TP // tt

    def body(ids_ref, a_ref, b_ref, c1_ref, o_ref, ob_ref, l0_ref, far_ref, acc, tmp, send_sems, recv_sems, tmp_sem,
             far_send_sems, far_recv_sems):
        del ids_ref
        r = pl.program_id(0)
        t = pl.program_id(1)
        x, y, c = _my_place()
        sibling = (x, y, 1 - c)
        slot = lax.rem(r, 2)
        ride_start, ride_finish = _partials_to_owners(c1_ref, far_ref, far_send_sems, far_recv_sems)

        @pl.when((r == 0) & (t == 0))
        def _():
            ride_start()

        def send_in(q):
            return pltpu.make_async_remote_copy(
                src_ref=acc.at[q % 2], dst_ref=l0_ref.at[q], send_sem=send_sems.at[q], recv_sem=recv_sems.at[q],
                device_id=sibling, device_id_type=MESH_ID)

        def landed(q):
            return pltpu.make_async_copy(l0_ref.at[q], tmp, tmp_sem)

        @pl.when(t == 0)
        def _():
            acc[slot] = jnp.zeros((D, W_IN_BLK), F32)

        acc[slot] += _dot_tn(a_ref[...], b_ref[...])

        for q in range(4):
            @pl.when((r == q) & (t == n_t - 1))
            def _(q=q):
                if q >= 1:
                    send_in(q - 1).wait_send()
                send_in(q).start()

            @pl.when((r == 4 + q) & (t == n_t - 2))
            def _(q=q):
                if q == 0:
                    send_in(3).wait_send()
                send_in(q).wait_recv()
                landed(q).start()

            @pl.when((r == 4 + q) & (t == n_t - 1))
            def _(q=q):
                landed(q).wait()
                tot = acc[q % 2] + tmp[...]
                o_ref[0] = tot
                ob_ref[0] = tot.astype(BF16)

        @pl.when((r == N_DEV - 1) & (t == n_t - 1))
        def _():
            ride_finish()

    blk = pl.BlockSpec((1, D, W_IN_BLK), lambda r, t, ids: (jnp.maximum(r - 4, 0), 0, 0))
    return pl.pallas_call(
        body, name="wgrad_in",
        grid_spec=pltpu.PrefetchScalarGridSpec(
            num_scalar_prefetch=1, grid=(N_DEV, n_t),
            in_specs=[pl.BlockSpec((tt, D), lambda r, t, ids: (t, 0)),
                      pl.BlockSpec((tt, W_IN_BLK), lambda r, t, ids: (t, ids[r])), ANY],
            out_specs=(blk, blk, ANY, ANY),
            scratch_shapes=[pltpu.VMEM((2, D, W_IN_BLK), F32), pltpu.VMEM((D, W_IN_BLK), F32),
                            pltpu.SemaphoreType.DMA((4,)), pltpu.SemaphoreType.DMA((4,)), pltpu.SemaphoreType.DMA,
                            pltpu.SemaphoreType.DMA((3,)), pltpu.SemaphoreType.DMA((3,))]),
        out_shape=(jax.ShapeDtypeStruct((4, D, W_IN_BLK), F32), jax.ShapeDtypeStruct((4, D, W_IN_BLK), BF16),
                   jax.ShapeDtypeStruct((4, D, W_IN_BLK), F32), jax.ShapeDtypeStruct((3, 3, W_ROW_BLK, D), BF16)),
        compiler_params=pltpu.CompilerParams(dimension_semantics=("arbitrary", "arbitrary")),
    )(ids, h, dproj, chip1b)


def _chip_sum_3(p3, land1, ids_mine):
    def body(ids_ref, p_ref, l_ref, o_ref, ob_ref):
        del ids_ref
        tot = p_ref[...] + l_ref[0]
        o_ref[0] = tot
        ob_ref[0] = tot.astype(BF16)

    blk = pl.BlockSpec((1, 3, W_ROW_BLK, D), lambda r, ids: (r, 0, 0, 0))
    return pl.pallas_call(
        body, name="chip_sum_3",
        grid_spec=pltpu.PrefetchScalarGridSpec(
            num_scalar_prefetch=1, grid=(4,),
            in_specs=[pl.BlockSpec((3, W_ROW_BLK, D), lambda r, ids: (0, ids[r], 0)), blk],
            out_specs=(blk, blk)),
        out_shape=(jax.ShapeDtypeStruct((4, 3, W_ROW_BLK, D), F32), jax.ShapeDtypeStruct((4, 3, W_ROW_BLK, D), BF16)),
    )(ids_mine, p3, land1)


def _dh_and_norm_bwd(dproj, w_in_full, xin, b3, norm_g, chip0b):
    tm = TM_MAT
    n_k = N_DEV // DH_K_BLKS
    n_m = TP // tm

    def body(dp_ref, w_ref, x_ref, dr_ref, g_ref, c0_ref, dx_ref, dg_ref, f0_ref, acc, send_sems, recv_sems):
        m = pl.program_id(0)
        k = pl.program_id(1)
        ride_start, ride_finish = _partials_to_owners(c0_ref, f0_ref, send_sems, recv_sems)

        @pl.when((m == 0) & (k == 0))
        def _():
            ride_start()

        @pl.when(k == 0)
        def _():
            acc[...] = jnp.zeros_like(acc)

        part = _dot_nt(dp_ref[:, 0:W_IN_BLK], w_ref[0])
        for j in range(1, DH_K_BLKS):
            part = part + _dot_nt(dp_ref[:, j * W_IN_BLK:(j + 1) * W_IN_BLK], w_ref[j])
        acc[...] += part

        @pl.when((k == n_k - 1) & (m == 0))
        def _():
            dg_ref[...] = jnp.zeros_like(dg_ref)

        @pl.when(k == n_k - 1)
        def _():
            xv = x_ref[...]
            r1 = lax.rsqrt(jnp.mean(xv * xv, axis=-1, keepdims=True) + EPS)
            xh = xv * r1
            d_h = acc[...]
            dg_ref[0:1, :] += jnp.sum(d_h * xh, axis=0, keepdims=True)
            d_xh = d_h * g_ref[...]
            dx_ref[...] = dr_ref[0].astype(F32) + r1 * (d_xh - xh * jnp.mean(d_xh * xh, axis=-1, keepdims=True))

        @pl.when((m == n_m - 1) & (k == n_k - 1))
        def _():
            ride_finish()

    return pl.pallas_call(
        body, name="dh_norm_bwd", grid=(n_m, n_k),
        in_specs=[pl.BlockSpec((tm, DH_K_BLKS * W_IN_BLK), lambda m, k: (m, k)),
                  pl.BlockSpec((DH_K_BLKS, D, W_IN_BLK), lambda m, k: (k, 0, 0)),
                  pl.BlockSpec((tm, D), lambda m, k: (m, 0)), pl.BlockSpec((1, tm, D), lambda m, k: (2, m, 0)),
                  pl.BlockSpec((1, D), lambda m, k: (0, 0)), ANY],
        out_specs=(pl.BlockSpec((tm, D), lambda m, k: (m, 0)), pl.BlockSpec((8, D), lambda m, k: (0, 0)), ANY),
        out_shape=(jax.ShapeDtypeStruct((TP, D), F32), jax.ShapeDtypeStruct((8, D), F32),
                   jax.ShapeDtypeStruct((3, D, W_IN_BLK), BF16)),
        scratch_shapes=[pltpu.VMEM((tm, D), F32), pltpu.SemaphoreType.DMA((3,)), pltpu.SemaphoreType.DMA((3,))],
        compiler_params=pltpu.CompilerParams(dimension_semantics=("arbitrary", "arbitrary")),
    )(dproj, w_in_full, xin, b3, norm_g, chip0b)


def _sum_adamw(own, landed, w, m, v, tr, name):
    rows, cols = w.shape
    n_t = rows // tr

    def body(o_ref, l1_ref, l2_ref, l3_ref, w_ref, m_ref, v_ref, g_ref, d_ref, m2_ref, v2_ref):
        g = ((o_ref[...] + l1_ref[...].astype(F32)) + l2_ref[...].astype(F32)) + l3_ref[...].astype(F32)
        delta, m2, v2 = _adamw(w_ref[...], g, m_ref[...], v_ref[...])
        g_ref[...] = g
        d_ref[...] = delta
        m2_ref[...] = m2
        v2_ref[...] = v2

    def spec(k):
        return pl.BlockSpec((tr, cols), lambda i: (i + k * n_t, 0))

    out = jax.ShapeDtypeStruct((rows, cols), F32)
    return pl.pallas_call(
        body, name=name, grid=(n_t,),
        in_specs=[spec(0), spec(0), spec(1), spec(2), spec(0), spec(0), spec(0)],
        out_specs=(spec(0),) * 4, out_shape=(out,) * 4,
    )(own, landed, landed, landed, w, m, v)


def _adamw_3(chip1, far1, ws, ms, vs):
    def body(c_ref, f_ref, *refs):
        w_refs, m_refs, v_refs, outs = refs[0:3], refs[3:6], refs[6:9], refs[9:21]
        for k in range(3):
            g = ((c_ref[0, k] + f_ref[0, k].astype(F32)) + f_ref[1, k].astype(F32)) + f_ref[2, k].astype(F32)
            delta, m2, v2 = _adamw(w_refs[k][0], g, m_refs[k][0], v_refs[k][0])
            for kind, val in enumerate((g, delta, m2, v2)):
                outs[3 * kind + k][0] = val

    full = pl.BlockSpec((1, W_ROW_BLK, D), lambda i: (0, 0, 0))
    out = jax.ShapeDtypeStruct((1, W_ROW_BLK, D), F32)
    res = pl.pallas_call(
        body, name="adamw_3", grid=(1,),
        in_specs=[pl.BlockSpec((1, 3, W_ROW_BLK, D), lambda i: (0, 0, 0, 0)),
                  pl.BlockSpec((3, 3, W_ROW_BLK, D), lambda i: (0, 0, 0, 0))] + [full] * 9,
        out_specs=(full,) * 12, out_shape=(out,) * 12,
    )(chip1, far1, *ws, *ms, *vs)
    return tuple(res[3 * kind:3 * kind + 3] for kind in range(4))


N_SMALL = 9


def _small_update(pack_all, srs_all, ws, ms, vs):
    def body(pk_ref, sr_ref, *refs):
        w_refs, m_refs, v_refs = refs[0:N_SMALL], refs[N_SMALL:2 * N_SMALL], refs[2 * N_SMALL:3 * N_SMALL]
        loss_ref = refs[3 * N_SMALL]
        outs = refs[3 * N_SMALL + 1:7 * N_SMALL + 1]
        tot_sc, tots_sc = refs[7 * N_SMALL + 1:]
        tot = pk_ref[0]
        tot_s = sr_ref[0]
        for d in range(1, N_DEV):
            tot = tot + pk_ref[d]
            tot_s = tot_s + sr_ref[d]
        tot_sc[...] = tot
        tots_sc[...] = tot_s
        loss_ref[...] = jnp.sum(tot_sc[5:6, :], axis=1, keepdims=True)
        lbl = w_refs[4]
        p0 = _sigmoid(lbl[0:1, :] - lbl[1:2, :])
        d_l0 = tot_sc[4:5, :] * p0 * (1.0 - p0)

        def update(k, sel, g):
            delta, m2, v2 = _adamw(w_refs[k][sel], g, m_refs[k][sel], v_refs[k][sel])
            for kind, val in enumerate((g, delta, m2, v2)):
                outs[N_SMALL * kind + k][sel] = val

        everything = (slice(None), slice(None))
        for k, row in ((0, 0), (1, 1), (2, 2), (3, 3), (5, 6), (6, 7)):
            update(k, everything, tot_sc[row:row + 1, :])
        update(4, (slice(0, 1), slice(None)), d_l0)
        update(4, (slice(1, 2), slice(None)), -d_l0)
        update(7, (0, slice(None), slice(None)), tots_sc[0:CONV_K, :])
        update(8, everything, tots_sc[META_ROW:META_ROW + N_META, :])

    shapes = [jax.ShapeDtypeStruct(w.shape, F32) for w in ws]
    res = pl.pallas_call(
        body, name="small_update",
        out_shape=(jax.ShapeDtypeStruct((1, 1), F32), *(shapes * 4)),
        scratch_shapes=[pltpu.VMEM((8, D), F32), pltpu.VMEM((SMALL_ROWS, HEAD_W), F32)],
    )(pack_all, srs_all, *ws, *ms, *vs)
    return res[0], tuple(res[1 + N_SMALL * kind:1 + N_SMALL * (kind + 1)] for kind in range(4))


def _local_step(xin, proj, target, conv_w_full, conv_b, ln_g, ln_b, w3_b, lb_logits, gnorm_g, final_g, ids_mine):
    fg = final_g.reshape(1, D)
    c0, o, s_start, att, w3_full = _rec_conv_fwd(proj, lb_logits, conv_w_full, conv_b, w3_b)
    d_o, d_c0, d_z, dproj, a3, b3, red = _mid(xin, target, o, c0, proj, w3_full, ln_g, ln_b, gnorm_g, fg)
    p3 = _wgrad3(a3, b3)
    dproj, dlb, d_conv_w, land1 = _rec_conv_bwd(proj, lb_logits, d_o, s_start, att, d_c0, d_z, conv_w_full, dproj, p3)
    chip1, chip1b = _chip_sum_3(p3, land1, ids_mine)
    return dproj, b3, p3, chip1, chip1b, d_conv_w, red, dlb


def kernel(x, meta_tokens, norm_g, w_in, conv_w, conv_b, ln_g, ln_b, w_conv_out, lb_logits, gnorm_g, w_rec_out, w_out, final_g, loss_target, m_meta_tokens, m_norm_g, m_w_in, m_conv_w, m_conv_b, m_ln_g, m_ln_b, m_w_conv_out, m_lb_logits, m_gnorm_g, m_w_rec_out, m_w_out, m_final_g, v_meta_tokens, v_norm_g, v_w_in, v_conv_w, v_conv_b, v_ln_g, v_ln_b, v_w_conv_out, v_lb_logits, v_gnorm_g, v_w_rec_out, v_w_out, v_final_g):
    mx, my, mc = _my_place()

    ws_s = jnp.concatenate([conv_w[0], jnp.zeros((1, HEAD_W), F32), meta_tokens], axis=0)
    small_all, w_in_b, w3_b = _gather_small_and_cast(ws_s, w_in[0], w_conv_out, w_rec_out, w_out)
    small_full = jnp.transpose(small_all, (1, 0, 2)).reshape(SMALL_ROWS, D)
    conv_w_full = small_full[0:CONV_K]
    meta_full = small_full[META_ROW:META_ROW + N_META]
    first, second, diag = _gather_chips(mx, my, mc)
    use_order = [(mx, my, mc), (mx, my, 1 - mc), (*first, mc), (*second, 1 - mc), (*second, mc), (*first, 1 - mc),
                 (*diag, mc), (*diag, 1 - mc)]
    order = jnp.stack([_dev_index(*p) for p in use_order]).astype(jnp.int32)
    proj, xin, h, w_in_full = _gather_and_proj(x[0], meta_full, norm_g, w_in_b, order)
    h = h.reshape(TP, D)

    ids_mine = jnp.stack([_dev_index(*_chip_rel(mx, my, r), mc) for r in range(4)]).astype(jnp.int32)
    ids_sib = jnp.stack([_dev_index(*_chip_rel(mx, my, r), 1 - mc) for r in range(4)]).astype(jnp.int32)
    dproj, b3, _, chip1, chip1b, d_conv_w, red, dlb = _local_step(
        xin, proj, loss_target[0], conv_w_full, conv_b, ln_g, ln_b, w3_b, lb_logits, gnorm_g, final_g, ids_mine)

    chip0, chip0b, _, far1 = _wgrad_in(h, dproj, jnp.concatenate([ids_sib, ids_mine]), chip1b)
    d_xin, dng, far0 = _dh_and_norm_bwd(dproj, w_in_full, xin, b3, norm_g, chip0b)
    pack = jnp.concatenate([dng[0:1], red[4:5], red[2:3], red[3:4], dlb[0:1], red[5:6], red[1:2], red[0:1]], axis=0)
    g_in, d_in, m_in, v_in = _sum_adamw(chip0.reshape(4 * D, W_IN_BLK), far0.reshape(3 * D, W_IN_BLK), w_in[0],
                                        m_w_in[0], v_w_in[0], 256, "adamw_in")
    big3 = _adamw_3(chip1, far1, (w_conv_out, w_rec_out, w_out), (m_w_conv_out, m_w_rec_out, m_w_out),
                    (v_w_conv_out, v_w_rec_out, v_w_out))

    srs = jnp.concatenate([d_conv_w, d_xin[PAD_FRONT:ROW0]], axis=0)
    srs = jnp.transpose(srs.reshape(SMALL_ROWS, N_DEV, HEAD_W), (1, 0, 2))
    pack_all, srs_all = _exchange_small(pack, srs)
    loss, small = _small_update(
        pack_all, srs_all,
        (norm_g, conv_b, ln_g, ln_b, lb_logits, gnorm_g, final_g.reshape(1, D), conv_w, meta_tokens),
        (m_norm_g, m_conv_b, m_ln_g, m_ln_b, m_lb_logits, m_gnorm_g, m_final_g.reshape(1, D), m_conv_w, m_meta_tokens),
        (v_norm_g, v_conv_b, v_ln_g, v_ln_b, v_lb_logits, v_gnorm_g, v_final_g.reshape(1, D), v_conv_w, v_meta_tokens))

    outs = [loss.reshape(()), d_xin[ROW0:][None]]
    for kind, a_in in enumerate((g_in, d_in, m_in, v_in)):
        ng, cb, lg, lb_, lbl, gg, fg, cw, mt = small[kind]
        a_3 = big3[kind]
        outs += [mt, ng, a_in[None], cw, cb, lg, lb_, a_3[0], lbl, gg, a_3[1], a_3[2], fg.reshape(D)]
    return tuple(outs)
```
